```python
import jax, jax.numpy as jnp
from jax import lax
import numpy as np


D_MODEL = 1024
BATCH = 8
SEQ = 8192
DEPTH = 1

N_META = 16
CHUNK = 128
FRONT = CHUNK - N_META
EPS = 1e-6

SSD_EXPAND = 2
SSD_D_INNER = SSD_EXPAND * D_MODEL
SSD_HEAD_DIM = 64
SSD_HEADS = SSD_D_INNER // SSD_HEAD_DIM
SSD_GROUPS = 4
SSD_STATE = 128
SSD_CONV = 4
SSD_CONV_DIM = SSD_D_INNER + 2 * SSD_GROUPS * SSD_STATE

RET_HEADS = 4
RET_QK_DIM = D_MODEL
RET_V_DIM = 2 * D_MODEL
RET_HEAD_QK = RET_QK_DIM // RET_HEADS
RET_HEAD_V = RET_V_DIM // RET_HEADS
ROPE_BASE = 10000.0

D_FF = ((8 * D_MODEL // 3 + 127) // 128) * 128
FFN_CONV = 3

IN_PROJ_DIM = SSD_D_INNER + SSD_CONV_DIM + SSD_HEADS + 2 * RET_QK_DIM + 2 * RET_V_DIM + 2 * D_MODEL

kernel_name = 'hybrid_ssd_retention_meta_block'


def _split_points():
    widths = [SSD_D_INNER, SSD_CONV_DIM, SSD_HEADS, RET_QK_DIM, RET_QK_DIM,
              RET_V_DIM, RET_V_DIM, D_MODEL, D_MODEL]
    pts = []
    acc = 0
    for w in widths[:-1]:
        acc += w
        pts.append(acc)
    return pts


def rmsnorm(x, w):
    xf = x.astype(jnp.float32)
    y = xf * lax.rsqrt(jnp.mean(xf * xf, axis=-1, keepdims=True) + EPS)
    return (y * w.astype(jnp.float32)).astype(x.dtype)


def causal_dwconv(x, w, b):
    k_width = w.shape[0]
    seq_len = x.shape[1]
    xp = jnp.pad(x, ((0, 0), (k_width - 1, 0), (0, 0)))
    y = b + xp[:, 0:seq_len] * w[0]
    for k in range(1, k_width):
        y = y + xp[:, k:k + seq_len] * w[k]
    return y


def _pad_to_chunks(t):
    n_tok = t.shape[1] - N_META
    end = (-n_tok) % CHUNK
    widths = [(0, 0), (FRONT, end)] + [(0, 0)] * (t.ndim - 2)
    return jnp.pad(t, widths)


def ssd_mixer(z, xbc, dt_raw, conv_w, conv_b, dt_bias, A_log, D_skip, norm_w):
    f32 = jnp.float32
    bsz, seq_len, _ = z.shape
    hpg = SSD_HEADS // SSD_GROUPS
    xbc = jax.nn.silu(causal_dwconv(xbc, conv_w, conv_b))
    xs, b_in, c_in = jnp.split(xbc, [SSD_D_INNER, SSD_D_INNER + SSD_GROUPS * SSD_STATE], axis=-1)
    dt = jax.nn.softplus(dt_raw.astype(f32) + dt_bias.astype(f32))
    xs, b_in, c_in, dt = [_pad_to_chunks(t) for t in (xs, b_in, c_in, dt)]
    n_chunks = xs.shape[1] // CHUNK
    xs = xs.reshape(bsz, n_chunks, CHUNK, SSD_GROUPS, hpg, SSD_HEAD_DIM)
    b_in = b_in.reshape(bsz, n_chunks, CHUNK, SSD_GROUPS, SSD_STATE)
    c_in = c_in.reshape(bsz, n_chunks, CHUNK, SSD_GROUPS, SSD_STATE)
    dt = dt.reshape(bsz, n_chunks, CHUNK, SSD_GROUPS, hpg)
    A = -jnp.exp(A_log.astype(f32)).reshape(SSD_GROUPS, hpg)
    a_cs = jnp.cumsum(dt * A, axis=2)
    xdt = xs * dt[..., None]
    causal = jnp.tril(jnp.ones((CHUNK, CHUNK), dtype=bool))
    seg = a_cs[:, :, :, None] - a_cs[:, :, None, :]
    decay_ls = jnp.exp(jnp.where(causal[:, :, None, None], seg, -jnp.inf))
    cb = jnp.einsum('bclgn,bcsgn->bclsg', c_in, b_in)
    y = jnp.einsum('bclsgh,bcsghp->bclghp', cb[..., None] * decay_ls, xdt)
    decay_to_end = jnp.exp(a_cs[:, :, -1:] - a_cs)
    chunk_states = jnp.einsum('bcsgn,bcsghp->bcghpn', b_in, xdt * decay_to_end[..., None])
    chunk_decay = jnp.exp(a_cs[:, :, -1])

    def step(state, inp):
        st, dec = inp
        return state * dec[..., None, None] + st, state

    init = jnp.zeros_like(chunk_states[:, 0])
    _, prev = lax.scan(step, init, (jnp.moveaxis(chunk_states, 1, 0), jnp.moveaxis(chunk_decay, 1, 0)))
    prev = jnp.moveaxis(prev, 0, 1)
    y = y + jnp.einsum('bclgn,bcghpn->bclghp', c_in, prev) * jnp.exp(a_cs)[..., None]
    y = y + D_skip.astype(f32).reshape(SSD_GROUPS, hpg, 1) * xs
    y = y.reshape(bsz, n_chunks * CHUNK, SSD_D_INNER)[:, FRONT:FRONT + seq_len]
    yz = (y * jax.nn.silu(z.astype(f32))).reshape(bsz, seq_len, SSD_GROUPS, -1)
    yz = yz * lax.rsqrt(jnp.mean(yz * yz, axis=-1, keepdims=True) + EPS)
    return (yz.reshape(bsz, seq_len, SSD_D_INNER) * norm_w.astype(f32)).astype(z.dtype)


def _rotary(t, cos, sin):
    half = t.shape[-1] // 2
    t1, t2 = t[..., :half], t[..., half:]
    return jnp.concatenate([t1 * cos - t2 * sin, t2 * cos + t1 * sin], axis=-1).astype(t.dtype)


def retention_mixer(q, k, v, g):
    f32 = jnp.float32
    bsz, seq_len, _ = q.shape
    q = q.reshape(bsz, seq_len, RET_HEADS, RET_HEAD_QK)
    k = k.reshape(bsz, seq_len, RET_HEADS, RET_HEAD_QK)
    v = v.reshape(bsz, seq_len, RET_HEADS, RET_HEAD_V)
    pos = jnp.arange(seq_len, dtype=f32)
    inv_freq = ROPE_BASE ** (-jnp.linspace(0.0, 1.0, RET_HEAD_QK // 2, dtype=f32))
    ang = pos[:, None] * inv_freq[None, :]
    cos = jnp.cos(ang)[None, :, None, :]
    sin = jnp.sin(ang)[None, :, None, :]
    q = _rotary(q, cos, sin)
    k = _rotary(k, cos, sin) * (RET_HEAD_QK ** -0.5)
    q, k, v = [_pad_to_chunks(t) for t in (q, k, v)]
    n_chunks = q.shape[1] // CHUNK
    q = q.reshape(bsz, n_chunks, CHUNK, RET_HEADS, RET_HEAD_QK)
    k = k.reshape(bsz, n_chunks, CHUNK, RET_HEADS, RET_HEAD_QK)
    v = v.reshape(bsz, n_chunks, CHUNK, RET_HEADS, RET_HEAD_V)
    log_gamma = jnp.log(1.0 - 2.0 ** (-5.0 - jnp.arange(RET_HEADS, dtype=f32)))
    idx = jnp.arange(CHUNK, dtype=f32)
    causal = jnp.tril(jnp.ones((CHUNK, CHUNK), dtype=bool))
    dist = (idx[:, None] - idx[None, :])[..., None] * log_gamma
    decay_ls = jnp.exp(jnp.where(causal[..., None], dist, -jnp.inf)).transpose(2, 0, 1)
    scores = jnp.einsum('bclhd,bcshd->bchls', q, k) * decay_ls
    out = jnp.einsum('bchls,bcshe->bclhe', scores, v)
    k_dec = k * jnp.exp((CHUNK - 1.0 - idx)[:, None] * log_gamma)[..., None]
    kv = jnp.einsum('bcshd,bcshe->bchde', k_dec, v)
    chunk_decay = jnp.exp(CHUNK * log_gamma)

    def step(state, kv_c):
        return state * chunk_decay[:, None, None] + kv_c, state

    init = jnp.zeros_like(kv[:, 0])
    _, prev = lax.scan(step, init, jnp.moveaxis(kv, 1, 0))
    prev = jnp.moveaxis(prev, 0, 1)
    cross = jnp.einsum('bclhd,bchde->bclhe', q, prev) * jnp.exp((idx + 1.0)[:, None] * log_gamma)[..., None]
    out = (out + cross).reshape(bsz, n_chunks * CHUNK, RET_HEADS, RET_HEAD_V)[:, FRONT:FRONT + seq_len]
    out = out.astype(f32)
    out = out * lax.rsqrt(jnp.mean(out * out, axis=-1, keepdims=True) + EPS)
    return (jax.nn.silu(g.astype(f32)) * out.reshape(bsz, seq_len, RET_V_DIM)).astype(g.dtype)


def _fwd_setup_inputs(seed: int = 0) -> dict:
    key = jax.random.key(seed)
    ks = jax.random.split(key, 24)
    f32 = jnp.float32
    nrm = lambda k, shape, scale: jax.random.normal(k, shape, f32) * scale
    dt0 = jnp.exp(jax.random.uniform(ks[6], (DEPTH, SSD_HEADS), f32, np.log(1e-3), np.log(1e-1)))
    return {
        'x': nrm(ks[0], (BATCH, SEQ, D_MODEL), 1.0),
        'meta_tokens': nrm(ks[1], (N_META, D_MODEL), 1.0),
        'mix_norm_w': 1.0 + nrm(ks[2], (DEPTH, D_MODEL), 0.01),
        'w_in': nrm(ks[3], (DEPTH, D_MODEL, IN_PROJ_DIM), D_MODEL ** -0.5),
        'ssd_conv_w': nrm(ks[4], (DEPTH, SSD_CONV, SSD_CONV_DIM), SSD_CONV ** -0.5),
        'ssd_conv_b': nrm(ks[5], (DEPTH, SSD_CONV_DIM), 0.01),
        'ssd_dt_bias': dt0 + jnp.log(-jnp.expm1(-dt0)),
        'ssd_A_log': jnp.log(jax.random.uniform(ks[7], (DEPTH, SSD_HEADS), f32, 1.0, 16.0)),
        'ssd_D': 1.0 + nrm(ks[8], (DEPTH, SSD_HEADS), 0.01),
        'ssd_norm_w': 1.0 + nrm(ks[9], (DEPTH, SSD_D_INNER), 0.01),
        'w_branch_ssd': nrm(ks[10], (DEPTH, SSD_D_INNER, D_MODEL), SSD_D_INNER ** -0.5),
        'w_branch_ret': nrm(ks[11], (DEPTH, RET_V_DIM, D_MODEL), RET_V_DIM ** -0.5),
        'w_out': nrm(ks[12], (DEPTH, D_MODEL, D_MODEL), D_MODEL ** -0.5),
        'ffn_norm_w': 1.0 + nrm(ks[13], (DEPTH, D_MODEL), 0.01),
        'w_up': nrm(ks[14], (DEPTH, D_MODEL, 2 * D_FF), D_MODEL ** -0.5),
        'ffn_conv_w': nrm(ks[15], (DEPTH, FFN_CONV, 2 * D_FF), FFN_CONV ** -0.5),
        'ffn_conv_b': nrm(ks[16], (DEPTH, 2 * D_FF), 0.01),
        'w_down': nrm(ks[17], (DEPTH, D_FF, D_MODEL), D_FF ** -0.5),
        'final_norm_w': 1.0 + nrm(ks[18], (D_MODEL,), 0.01),
    }


def _fwd_reference(x, meta_tokens, mix_norm_w, w_in, ssd_conv_w, ssd_conv_b, ssd_dt_bias, ssd_A_log,
              ssd_D, ssd_norm_w, w_branch_ssd, w_branch_ret, w_out, ffn_norm_w, w_up,
              ffn_conv_w, ffn_conv_b, w_down, final_norm_w):
    bsz = x.shape[0]
    meta = jnp.broadcast_to(meta_tokens[None].astype(x.dtype), (bsz, N_META, D_MODEL))
    h = jnp.concatenate([meta, x], axis=1)
    for layer in range(DEPTH):
        u = rmsnorm(h, mix_norm_w[layer])
        proj = u @ w_in[layer]
        z, xbc, dt_raw, q, k, v, g, gate_ssd, gate_ret = jnp.split(proj, _split_points(), axis=-1)
        y_ssd = ssd_mixer(z, xbc, dt_raw, ssd_conv_w[layer], ssd_conv_b[layer], ssd_dt_bias[layer],
                          ssd_A_log[layer], ssd_D[layer], ssd_norm_w[layer])
        y_ret = retention_mixer(q, k, v, g)
        merged = (jax.nn.sigmoid(gate_ssd) * (y_ssd @ w_branch_ssd[layer])
                  + jax.nn.sigmoid(gate_ret) * (y_ret @ w_branch_ret[layer]))
        h = h + merged @ w_out[layer]
        u = rmsnorm(h, ffn_norm_w[layer])
        a = causal_dwconv(u @ w_up[layer], ffn_conv_w[layer], ffn_conv_b[layer])
        a_gate, a_val = jnp.split(a, 2, axis=-1)
        h = h + (jax.nn.silu(a_gate) * a_val) @ w_down[layer]
    out = rmsnorm(h, final_norm_w)
    return out[:, N_META:]


import jax as _jax
import jax.numpy as _jnp

TWIN_FORMAT = 'train_step'
FWD_PARAMS = ['x', 'meta_tokens', 'mix_norm_w', 'w_in', 'ssd_conv_w', 'ssd_conv_b', 'ssd_dt_bias', 'ssd_A_log', 'ssd_D', 'ssd_norm_w', 'w_branch_ssd', 'w_branch_ret', 'w_out', 'ffn_norm_w', 'w_up', 'ffn_conv_w', 'ffn_conv_b', 'w_down', 'final_norm_w']
TWIN_WEIGHTS = ['meta_tokens', 'mix_norm_w', 'w_in', 'ssd_conv_w', 'ssd_conv_b', 'ssd_dt_bias', 'ssd_A_log', 'ssd_D', 'ssd_norm_w', 'w_branch_ssd', 'w_branch_ret', 'w_out', 'ffn_norm_w', 'w_up', 'ffn_conv_w', 'ffn_conv_b', 'w_down', 'final_norm_w']
TWIN_DIFF_INPUT = 'x'
TWIN_INPUTS = ['x', 'meta_tokens', 'mix_norm_w', 'w_in', 'ssd_conv_w', 'ssd_conv_b', 'ssd_dt_bias', 'ssd_A_log', 'ssd_D', 'ssd_norm_w', 'w_branch_ssd', 'w_branch_ret', 'w_out', 'ffn_norm_w', 'w_up', 'ffn_conv_w', 'ffn_conv_b', 'w_down', 'final_norm_w', 'loss_target', 'm_meta_tokens', 'm_mix_norm_w', 'm_w_in', 'm_ssd_conv_w', 'm_ssd_conv_b', 'm_ssd_dt_bias', 'm_ssd_A_log', 'm_ssd_D', 'm_ssd_norm_w', 'm_w_branch_ssd', 'm_w_branch_ret', 'm_w_out', 'm_ffn_norm_w', 'm_w_up', 'm_ffn_conv_w', 'm_ffn_conv_b', 'm_w_down', 'm_final_norm_w', 'v_meta_tokens', 'v_mix_norm_w', 'v_w_in', 'v_ssd_conv_w', 'v_ssd_conv_b', 'v_ssd_dt_bias', 'v_ssd_A_log', 'v_ssd_D', 'v_ssd_norm_w', 'v_w_branch_ssd', 'v_w_branch_ret', 'v_w_out', 'v_ffn_norm_w', 'v_w_up', 'v_ffn_conv_w', 'v_ffn_conv_b', 'v_w_down', 'v_final_norm_w']
TWIN_OUTPUTS = ['loss', 'grad_x', 'grad_meta_tokens', 'grad_mix_norm_w', 'grad_w_in', 'grad_ssd_conv_w', 'grad_ssd_conv_b', 'grad_ssd_dt_bias', 'grad_ssd_A_log', 'grad_ssd_D', 'grad_ssd_norm_w', 'grad_w_branch_ssd', 'grad_w_branch_ret', 'grad_w_out', 'grad_ffn_norm_w', 'grad_w_up', 'grad_ffn_conv_w', 'grad_ffn_conv_b', 'grad_w_down', 'grad_final_norm_w', 'delta_meta_tokens', 'delta_mix_norm_w', 'delta_w_in', 'delta_ssd_conv_w', 'delta_ssd_conv_b', 'delta_ssd_dt_bias', 'delta_ssd_A_log', 'delta_ssd_D', 'delta_ssd_norm_w', 'delta_w_branch_ssd', 'delta_w_branch_ret', 'delta_w_out', 'delta_ffn_norm_w', 'delta_w_up', 'delta_ffn_conv_w', 'delta_ffn_conv_b', 'delta_w_down', 'delta_final_norm_w', 'new_m_meta_tokens', 'new_m_mix_norm_w', 'new_m_w_in', 'new_m_ssd_conv_w', 'new_m_ssd_conv_b', 'new_m_ssd_dt_bias', 'new_m_ssd_A_log', 'new_m_ssd_D', 'new_m_ssd_norm_w', 'new_m_w_branch_ssd', 'new_m_w_branch_ret', 'new_m_w_out', 'new_m_ffn_norm_w', 'new_m_w_up', 'new_m_ffn_conv_w', 'new_m_ffn_conv_b', 'new_m_w_down', 'new_m_final_norm_w', 'new_v_meta_tokens', 'new_v_mix_norm_w', 'new_v_w_in', 'new_v_ssd_conv_w', 'new_v_ssd_conv_b', 'new_v_ssd_dt_bias', 'new_v_ssd_A_log', 'new_v_ssd_D', 'new_v_ssd_norm_w', 'new_v_w_branch_ssd', 'new_v_w_branch_ret', 'new_v_w_out', 'new_v_ffn_norm_w', 'new_v_w_up', 'new_v_ffn_conv_w', 'new_v_ffn_conv_b', 'new_v_w_down', 'new_v_final_norm_w']
TWIN_LEAF_KINDS = {'loss': 'loss', 'grad_x': 'grad_x', 'grad_meta_tokens': 'grad_w', 'grad_mix_norm_w': 'grad_w', 'grad_w_in': 'grad_w', 'grad_ssd_conv_w': 'grad_w', 'grad_ssd_conv_b': 'grad_w', 'grad_ssd_dt_bias': 'grad_w', 'grad_ssd_A_log': 'grad_w', 'grad_ssd_D': 'grad_w', 'grad_ssd_norm_w': 'grad_w', 'grad_w_branch_ssd': 'grad_w', 'grad_w_branch_ret': 'grad_w', 'grad_w_out': 'grad_w', 'grad_ffn_norm_w': 'grad_w', 'grad_w_up': 'grad_w', 'grad_ffn_conv_w': 'grad_w', 'grad_ffn_conv_b': 'grad_w', 'grad_w_down': 'grad_w', 'grad_final_norm_w': 'grad_w', 'delta_meta_tokens': 'delta_w', 'delta_mix_norm_w': 'delta_w', 'delta_w_in': 'delta_w', 'delta_ssd_conv_w': 'delta_w', 'delta_ssd_conv_b': 'delta_w', 'delta_ssd_dt_bias': 'delta_w', 'delta_ssd_A_log': 'delta_w', 'delta_ssd_D': 'delta_w', 'delta_ssd_norm_w': 'delta_w', 'delta_w_branch_ssd': 'delta_w', 'delta_w_branch_ret': 'delta_w', 'delta_w_out': 'delta_w', 'delta_ffn_norm_w': 'delta_w', 'delta_w_up': 'delta_w', 'delta_ffn_conv_w': 'delta_w', 'delta_ffn_conv_b': 'delta_w', 'delta_w_down': 'delta_w', 'delta_final_norm_w': 'delta_w', 'new_m_meta_tokens': 'new_m', 'new_m_mix_norm_w': 'new_m', 'new_m_w_in': 'new_m', 'new_m_ssd_conv_w': 'new_m', 'new_m_ssd_conv_b': 'new_m', 'new_m_ssd_dt_bias': 'new_m', 'new_m_ssd_A_log': 'new_m', 'new_m_ssd_D': 'new_m', 'new_m_ssd_norm_w': 'new_m', 'new_m_w_branch_ssd': 'new_m', 'new_m_w_branch_ret': 'new_m', 'new_m_w_out': 'new_m', 'new_m_ffn_norm_w': 'new_m', 'new_m_w_up': 'new_m', 'new_m_ffn_conv_w': 'new_m', 'new_m_ffn_conv_b': 'new_m', 'new_m_w_down': 'new_m', 'new_m_final_norm_w': 'new_m', 'new_v_meta_tokens': 'new_v', 'new_v_mix_norm_w': 'new_v', 'new_v_w_in': 'new_v', 'new_v_ssd_conv_w': 'new_v', 'new_v_ssd_conv_b': 'new_v', 'new_v_ssd_dt_bias': 'new_v', 'new_v_ssd_A_log': 'new_v', 'new_v_ssd_D': 'new_v', 'new_v_ssd_norm_w': 'new_v', 'new_v_w_branch_ssd': 'new_v', 'new_v_w_branch_ret': 'new_v', 'new_v_w_out': 'new_v', 'new_v_ffn_norm_w': 'new_v', 'new_v_w_up': 'new_v', 'new_v_ffn_conv_w': 'new_v', 'new_v_ffn_conv_b': 'new_v', 'new_v_w_down': 'new_v', 'new_v_final_norm_w': 'new_v'}


def _forward(args):
    return _fwd_reference(*[args[k] for k in FWD_PARAMS])


def _output_shape():
    def fwd():
        inp = _fwd_setup_inputs(0)
        return _fwd_reference(*[inp[k] for k in FWD_PARAMS])
    out = _jax.eval_shape(fwd)
    return out.shape, out.dtype

N_MICROBATCH = 1
ADAM_LR = 0.001
ADAM_B1 = 0.9
ADAM_B2 = 0.999
ADAM_EPS = 1e-08
ADAM_WD = 0.01
ADAM_STEP = 10
PER_EXAMPLE_BATCH_AXIS = {'x': 0, 'loss_target': 0}
SHARED_INPUTS = []
_WEIGHT_DTYPES = {'meta_tokens': _jnp.float32, 'mix_norm_w': _jnp.float32, 'w_in': _jnp.float32, 'ssd_conv_w': _jnp.float32, 'ssd_conv_b': _jnp.float32, 'ssd_dt_bias': _jnp.float32, 'ssd_A_log': _jnp.float32, 'ssd_D': _jnp.float32, 'ssd_norm_w': _jnp.float32, 'w_branch_ssd': _jnp.float32, 'w_branch_ret': _jnp.float32, 'w_out': _jnp.float32, 'ffn_norm_w': _jnp.float32, 'w_up': _jnp.float32, 'ffn_conv_w': _jnp.float32, 'ffn_conv_b': _jnp.float32, 'w_down': _jnp.float32, 'final_norm_w': _jnp.float32}
MOMENT_SCALE = {'meta_tokens': 9.424080e-03, 'mix_norm_w': 2.494747e-01, 'w_in': 7.000088e-02, 'ssd_conv_w': 7.714109e-02, 'ssd_conv_b': 1.070737e-01, 'ssd_dt_bias': 1.762915e-01, 'ssd_A_log': 3.090906e-01, 'ssd_D': 4.875109e-01, 'ssd_norm_w': 9.332126e-02, 'w_branch_ssd': 1.272595e-01, 'w_branch_ret': 7.609632e-02, 'w_out': 1.478968e-01, 'ffn_norm_w': 1.691539e-01, 'w_up': 7.160383e-02, 'ffn_conv_w': 7.051700e-02, 'ffn_conv_b': 7.129245e-02, 'w_down': 1.176304e-01, 'final_norm_w': 6.392745e+01}


def _to_microbatches(a, axis):
    t = _jnp.moveaxis(a, axis, 0)
    t = t.reshape((N_MICROBATCH, t.shape[0] // N_MICROBATCH) + t.shape[1:])
    return _jnp.moveaxis(t, 1, axis + 1)


def setup_inputs(seed: int = 0) -> dict:
    inp = _fwd_setup_inputs(seed)
    key = _jax.random.fold_in(_jax.random.key(seed), 7919)
    shape, _ = _output_shape()
    out = dict(inp)
    out["loss_target"] = _jax.random.normal(_jax.random.fold_in(key, 0), shape, _jnp.float32)
    for i, name in enumerate(TWIN_WEIGHTS):
        w = inp[name].astype(_jnp.float32)
        if MOMENT_SCALE is None:
            s = _jnp.sqrt(_jnp.mean(_jnp.square(w)) + 1e-30)
        else:
            s = MOMENT_SCALE[name]
        km, kv = _jax.random.split(_jax.random.fold_in(key, i + 1))
        out[name] = w
        out["m_" + name] = s * _jax.random.normal(km, w.shape, _jnp.float32)
        out["v_" + name] = (s * s) * _jax.random.uniform(kv, w.shape, _jnp.float32, 0.5, 1.5)
    if N_MICROBATCH > 1:
        for name, axis in PER_EXAMPLE_BATCH_AXIS.items():
            out[name] = _to_microbatches(out[name], axis)
    return {'x': out['x'], 'meta_tokens': out['meta_tokens'], 'mix_norm_w': out['mix_norm_w'], 'w_in': out['w_in'], 'ssd_conv_w': out['ssd_conv_w'], 'ssd_conv_b': out['ssd_conv_b'], 'ssd_dt_bias': out['ssd_dt_bias'], 'ssd_A_log': out['ssd_A_log'], 'ssd_D': out['ssd_D'], 'ssd_norm_w': out['ssd_norm_w'], 'w_branch_ssd': out['w_branch_ssd'], 'w_branch_ret': out['w_branch_ret'], 'w_out': out['w_out'], 'ffn_norm_w': out['ffn_norm_w'], 'w_up': out['w_up'], 'ffn_conv_w': out['ffn_conv_w'], 'ffn_conv_b': out['ffn_conv_b'], 'w_down': out['w_down'], 'final_norm_w': out['final_norm_w'], 'loss_target': out['loss_target'], 'm_meta_tokens': out['m_meta_tokens'], 'm_mix_norm_w': out['m_mix_norm_w'], 'm_w_in': out['m_w_in'], 'm_ssd_conv_w': out['m_ssd_conv_w'], 'm_ssd_conv_b': out['m_ssd_conv_b'], 'm_ssd_dt_bias': out['m_ssd_dt_bias'], 'm_ssd_A_log': out['m_ssd_A_log'], 'm_ssd_D': out['m_ssd_D'], 'm_ssd_norm_w': out['m_ssd_norm_w'], 'm_w_branch_ssd': out['m_w_branch_ssd'], 'm_w_branch_ret': out['m_w_branch_ret'], 'm_w_out': out['m_w_out'], 'm_ffn_norm_w': out['m_ffn_norm_w'], 'm_w_up': out['m_w_up'], 'm_ffn_conv_w': out['m_ffn_conv_w'], 'm_ffn_conv_b': out['m_ffn_conv_b'], 'm_w_down': out['m_w_down'], 'm_final_norm_w': out['m_final_norm_w'], 'v_meta_tokens': out['v_meta_tokens'], 'v_mix_norm_w': out['v_mix_norm_w'], 'v_w_in': out['v_w_in'], 'v_ssd_conv_w': out['v_ssd_conv_w'], 'v_ssd_conv_b': out['v_ssd_conv_b'], 'v_ssd_dt_bias': out['v_ssd_dt_bias'], 'v_ssd_A_log': out['v_ssd_A_log'], 'v_ssd_D': out['v_ssd_D'], 'v_ssd_norm_w': out['v_ssd_norm_w'], 'v_w_branch_ssd': out['v_w_branch_ssd'], 'v_w_branch_ret': out['v_w_branch_ret'], 'v_w_out': out['v_w_out'], 'v_ffn_norm_w': out['v_ffn_norm_w'], 'v_w_up': out['v_w_up'], 'v_ffn_conv_w': out['v_ffn_conv_w'], 'v_ffn_conv_b': out['v_ffn_conv_b'], 'v_w_down': out['v_w_down'], 'v_final_norm_w': out['v_final_norm_w']}


def _loss(weights, diff, rest, loss_target):
    with _jax.named_scope("forward"):
        args = {**rest, TWIN_DIFF_INPUT: diff, **{k: w.astype(_WEIGHT_DTYPES[k]) for k, w in weights.items()}}
        y = _forward(args)
    with _jax.named_scope("loss_head"):
        err = _jnp.square(y.astype(_jnp.float32) - loss_target)
        return 0.5 * _jnp.sum(_jnp.mean(err, axis=-1)) if err.ndim else 0.5 * err


def _adamw(w, g, m, v):
    m = ADAM_B1 * m + (1.0 - ADAM_B1) * g
    v = ADAM_B2 * v + (1.0 - ADAM_B2) * _jnp.square(g)
    m_hat = m / (1.0 - ADAM_B1 ** ADAM_STEP)
    v_hat = v / (1.0 - ADAM_B2 ** ADAM_STEP)
    delta = -ADAM_LR * (m_hat / (_jnp.sqrt(v_hat) + ADAM_EPS) + ADAM_WD * w)
    return delta, m, v


def reference(x, meta_tokens, mix_norm_w, w_in, ssd_conv_w, ssd_conv_b, ssd_dt_bias, ssd_A_log, ssd_D, ssd_norm_w, w_branch_ssd, w_branch_ret, w_out, ffn_norm_w, w_up, ffn_conv_w, ffn_conv_b, w_down, final_norm_w, loss_target, m_meta_tokens, m_mix_norm_w, m_w_in, m_ssd_conv_w, m_ssd_conv_b, m_ssd_dt_bias, m_ssd_A_log, m_ssd_D, m_ssd_norm_w, m_w_branch_ssd, m_w_branch_ret, m_w_out, m_ffn_norm_w, m_w_up, m_ffn_conv_w, m_ffn_conv_b, m_w_down, m_final_norm_w, v_meta_tokens, v_mix_norm_w, v_w_in, v_ssd_conv_w, v_ssd_conv_b, v_ssd_dt_bias, v_ssd_A_log, v_ssd_D, v_ssd_norm_w, v_w_branch_ssd, v_w_branch_ret, v_w_out, v_ffn_norm_w, v_w_up, v_ffn_conv_w, v_ffn_conv_b, v_w_down, v_final_norm_w):
    given = dict(x=x, meta_tokens=meta_tokens, mix_norm_w=mix_norm_w, w_in=w_in, ssd_conv_w=ssd_conv_w, ssd_conv_b=ssd_conv_b, ssd_dt_bias=ssd_dt_bias, ssd_A_log=ssd_A_log, ssd_D=ssd_D, ssd_norm_w=ssd_norm_w, w_branch_ssd=w_branch_ssd, w_branch_ret=w_branch_ret, w_out=w_out, ffn_norm_w=ffn_norm_w, w_up=w_up, ffn_conv_w=ffn_conv_w, ffn_conv_b=ffn_conv_b, w_down=w_down, final_norm_w=final_norm_w, loss_target=loss_target, m_meta_tokens=m_meta_tokens, m_mix_norm_w=m_mix_norm_w, m_w_in=m_w_in, m_ssd_conv_w=m_ssd_conv_w, m_ssd_conv_b=m_ssd_conv_b, m_ssd_dt_bias=m_ssd_dt_bias, m_ssd_A_log=m_ssd_A_log, m_ssd_D=m_ssd_D, m_ssd_norm_w=m_ssd_norm_w, m_w_branch_ssd=m_w_branch_ssd, m_w_branch_ret=m_w_branch_ret, m_w_out=m_w_out, m_ffn_norm_w=m_ffn_norm_w, m_w_up=m_w_up, m_ffn_conv_w=m_ffn_conv_w, m_ffn_conv_b=m_ffn_conv_b, m_w_down=m_w_down, m_final_norm_w=m_final_norm_w, v_meta_tokens=v_meta_tokens, v_mix_norm_w=v_mix_norm_w, v_w_in=v_w_in, v_ssd_conv_w=v_ssd_conv_w, v_ssd_conv_b=v_ssd_conv_b, v_ssd_dt_bias=v_ssd_dt_bias, v_ssd_A_log=v_ssd_A_log, v_ssd_D=v_ssd_D, v_ssd_norm_w=v_ssd_norm_w, v_w_branch_ssd=v_w_branch_ssd, v_w_branch_ret=v_w_branch_ret, v_w_out=v_w_out, v_ffn_norm_w=v_ffn_norm_w, v_w_up=v_w_up, v_ffn_conv_w=v_ffn_conv_w, v_ffn_conv_b=v_ffn_conv_b, v_w_down=v_w_down, v_final_norm_w=v_final_norm_w)
    weights = {n: given[n] for n in TWIN_WEIGHTS}
    shared = {n: given[n] for n in SHARED_INPUTS}
    per_example = {n: given[n] for n in ['x']}
    grad_fn = _jax.value_and_grad(_loss, argnums=(0, 1))

    def one_microbatch(ex, loss_target):
        ex = dict(ex)
        diff = ex.pop(TWIN_DIFF_INPUT)
        return grad_fn(weights, diff, {**shared, **ex}, loss_target)

    if N_MICROBATCH == 1:
        loss, (grad_w, grad_x) = one_microbatch(per_example, given["loss_target"])
    else:
        def body(carry, xs):
            loss_sum, grad_sum = carry
            l_k, (gw_k, gx_k) = one_microbatch(xs[0], xs[1])
            with _jax.named_scope("update"):
                return (loss_sum + l_k, _jax.tree.map(_jnp.add, grad_sum, gw_k)), gx_k

        init = (_jnp.zeros((), _jnp.float32), _jax.tree.map(_jnp.zeros_like, weights))
        (loss, grad_w), grad_x = _jax.lax.scan(body, init, (per_example, given["loss_target"]))
    with _jax.named_scope("update"):
        delta_w, new_m, new_v = {}, {}, {}
        for n in TWIN_WEIGHTS:
            delta_w[n], new_m[n], new_v[n] = _adamw(weights[n], grad_w[n], given["m_" + n], given["v_" + n])
    return (loss, grad_x, *[grad_w[n] for n in TWIN_WEIGHTS], *[delta_w[n] for n in TWIN_WEIGHTS],
            *[new_m[n] for n in TWIN_WEIGHTS], *[new_v[n] for n in TWIN_WEIGHTS])
```

```python
import math

import jax
import jax.numpy as jnp
import numpy as np
from jax import lax
from jax.experimental import pallas as pl
from jax.experimental.pallas import tpu as pltpu

F32 = jnp.float32
BF16 = jnp.bfloat16
MXU_DTYPE = BF16
WIRE_DTYPE = BF16

N_META = 16
CHUNK = 128
FRONT = CHUNK - N_META
EPS = 1e-6
SSD_HEAD_DIM = 64
SSD_GROUPS = 4
SSD_STATE = 128
SSD_CONV = 4
RET_HEADS = 4
ROPE_BASE = 10000.0
FFN_CONV = 3
LANES = 128
SUBLANES = 8
VMEM_LIMIT = 56 * 1024 * 1024

ADAM_LR = 0.001
ADAM_B1 = 0.9
ADAM_B2 = 0.999
ADAM_EPS = 1e-08
ADAM_WD = 0.01
ADAM_STEP = 10
MESH = pl.DeviceIdType.MESH


def _params(sem=None, vmem=VMEM_LIMIT):
    return pltpu.CompilerParams(dimension_semantics=sem, vmem_limit_bytes=vmem)


def _pick(n, cands):
    for c in cands:
        if n % c == 0:
            return c
    return n


def _silu(x):
    return x * jax.nn.sigmoid(x)


def _dsilu(x):
    s = jax.nn.sigmoid(x)
    return s * (1.0 + x * (1.0 - s))


def _dot(a, b, dims=(((1,), (0,)), ((), ()))):
    return lax.dot_general(a.astype(MXU_DTYPE), b.astype(MXU_DTYPE), dims, preferred_element_type=F32)


def _dot_nt(a, b):
    return _dot(a, b, (((1,), (1,)), ((), ())))


def _dot_tn(a, b):
    return _dot(a, b, (((0,), (0,)), ((), ())))


def _dot_exact(a, b, dims=(((1,), (0,)), ((), ()))):
    return lax.dot_general(a.astype(F32), b.astype(F32), dims, precision=lax.Precision.HIGHEST,
                           preferred_element_type=F32)


def _iota(shape, dim):
    return lax.broadcasted_iota(jnp.int32, shape, dim)


def _shift_down(cur, prev8, k):
    if k == 0:
        return cur
    rolled = pltpu.roll(cur, k, 0)
    i8 = _iota((SUBLANES, cur.shape[1]), 0)
    head = jnp.where(i8 < k, pltpu.roll(prev8, k, 0), rolled[0:SUBLANES])
    return jnp.concatenate([head, rolled[SUBLANES:]], axis=0)


def _shift_up(cur, next8, k):
    if k == 0:
        return cur
    n = cur.shape[0]
    rolled = pltpu.roll(cur, n - k, 0)
    i8 = _iota((SUBLANES, cur.shape[1]), 0)
    tail = jnp.where(i8 >= SUBLANES - k, pltpu.roll(next8, SUBLANES - k, 0), rolled[n - SUBLANES:])
    return jnp.concatenate([rolled[:n - SUBLANES], tail], axis=0)


def _mm(a, b, mode, out_dtype, name, tm, tn, tk, res=None):
    if mode == "nn":
        (m, kd), n = a.shape, b.shape[1]
        a_spec = pl.BlockSpec((tm, tk), lambda i, j, k: (i, k))
        b_spec = pl.BlockSpec((tk, tn), lambda i, j, k: (k, j))
        dims = (((1,), (0,)), ((), ()))
    elif mode == "nt":
        (m, kd), n = a.shape, b.shape[0]
        a_spec = pl.BlockSpec((tm, tk), lambda i, j, k: (i, k))
        b_spec = pl.BlockSpec((tn, tk), lambda i, j, k: (j, k))
        dims = (((1,), (1,)), ((), ()))
    else:
        (kd, m), n = a.shape, b.shape[1]
        a_spec = pl.BlockSpec((tk, tm), lambda i, j, k: (k, i))
        b_spec = pl.BlockSpec((tk, tn), lambda i, j, k: (k, j))
        dims = (((0,), (0,)), ((), ()))
    assert m % tm == 0 and n % tn == 0 and kd % tk == 0, (name, m, n, kd, tm, tn, tk)
    nk = kd // tk
    has_res = res is not None

    def body(*refs):
        if has_res:
            a_ref, b_ref, r_ref, o_ref, acc = refs
        else:
            a_ref, b_ref, o_ref, acc = refs
        k = pl.program_id(2)

        @pl.when(k == 0)
        def _():
            acc[...] = jnp.zeros_like(acc)

        acc[...] += _dot(a_ref[...], b_ref[...], dims)

        @pl.when(k == nk - 1)
        def _():
            r = acc[...]
            if has_res:
                r = r + r_ref[...].astype(F32)
            o_ref[...] = r.astype(out_dtype)

    in_specs = [a_spec, b_spec]
    args = [a, b]
    if has_res:
        in_specs.append(pl.BlockSpec((tm, tn), lambda i, j, k: (i, j)))
        args.append(res)
    return pl.pallas_call(
        body, name=name, grid=(m // tm, n // tn, nk), in_specs=in_specs,
        out_specs=pl.BlockSpec((tm, tn), lambda i, j, k: (i, j)),
        out_shape=jax.ShapeDtypeStruct((m, n), out_dtype),
        scratch_shapes=[pltpu.VMEM((tm, tn), F32)],
        compiler_params=_params(("parallel", "parallel", "arbitrary")),
    )(*args)


def _rms_fwd(h, w, name):
    t, d = h.shape
    tr = _pick(t, (640, 512, 384, 256, 128))

    def body(h_ref, w_ref, u_ref):
        x = h_ref[...]
        r = lax.rsqrt(jnp.mean(x * x, axis=1, keepdims=True) + EPS)
        u_ref[...] = (x * r * w_ref[...]).astype(MXU_DTYPE)

    return pl.pallas_call(
        body, name=name, grid=(t // tr,),
        in_specs=[pl.BlockSpec((tr, d), lambda i: (i, 0)), pl.BlockSpec((1, d), lambda i: (0, 0))],
        out_specs=pl.BlockSpec((tr, d), lambda i: (i, 0)),
        out_shape=jax.ShapeDtypeStruct((t, d), MXU_DTYPE),
        compiler_params=_params(("parallel",)),
    )(h, w)


def _rms_bwd(h, w, du, res, name):
    t, d = h.shape
    tr = _pick(t, (640, 512, 384, 256, 128))

    def body(h_ref, w_ref, du_ref, res_ref, dh_ref, gw_ref):
        @pl.when(pl.program_id(0) == 0)
        def _():
            gw_ref[...] = jnp.zeros_like(gw_ref)

        x = h_ref[...]
        r = lax.rsqrt(jnp.mean(x * x, axis=1, keepdims=True) + EPS)
        xhat = x * r
        dy = du_ref[...].astype(F32)
        dxh = dy * w_ref[...]
        dh = r * (dxh - xhat * jnp.mean(dxh * xhat, axis=1, keepdims=True))
        dh_ref[...] = dh + res_ref[...]
        gw_ref[...] += jnp.sum(dy * xhat, axis=0, keepdims=True)

    row = pl.BlockSpec((tr, d), lambda i: (i, 0))
    vec = pl.BlockSpec((1, d), lambda i: (0, 0))
    return pl.pallas_call(
        body, name=name, grid=(t // tr,), in_specs=[row, vec, row, row], out_specs=[row, vec],
        out_shape=[jax.ShapeDtypeStruct((t, d), F32), jax.ShapeDtypeStruct((1, d), F32)],
        compiler_params=_params(("arbitrary",)),
    )(h, w, du, res)


def _loss_bwd(h2, w, target, name):
    t, d = h2.shape
    nc = t // CHUNK

    def body(h_ref, w_ref, tg_ref, loss_ref, dh_ref, gw_ref):
        i = pl.program_id(0)

        @pl.when(i == 0)
        def _():
            gw_ref[...] = jnp.zeros_like(gw_ref)
            loss_ref[...] = jnp.zeros_like(loss_ref)
            dh_ref[...] = jnp.zeros_like(dh_ref)

        @pl.when(i > 0)
        def _():
            x = h_ref[...]
            r = lax.rsqrt(jnp.mean(x * x, axis=1, keepdims=True) + EPS)
            xhat = x * r
            diff = xhat * w_ref[...] - tg_ref[...]
            loss_ref[...] += 0.5 * jnp.sum(jnp.sum(diff * diff, axis=1, keepdims=True), axis=0, keepdims=True) / d
            dy = diff / d
            dxh = dy * w_ref[...]
            dh_ref[...] = r * (dxh - xhat * jnp.mean(dxh * xhat, axis=1, keepdims=True))
            gw_ref[...] += jnp.sum(dy * xhat, axis=0, keepdims=True)

    row = pl.BlockSpec((CHUNK, d), lambda i: (i, 0))
    vec = pl.BlockSpec((1, d), lambda i: (0, 0))
    return pl.pallas_call(
        body, name=name, grid=(nc,),
        in_specs=[row, vec, pl.BlockSpec((CHUNK, d), lambda i: (jnp.maximum(i - 1, 0), 0))],
        out_specs=[pl.BlockSpec((SUBLANES, LANES), lambda i: (0, 0)), row, vec],
        out_shape=[jax.ShapeDtypeStruct((SUBLANES, LANES), F32), jax.ShapeDtypeStruct((t, d), F32),
                   jax.ShapeDtypeStruct((1, d), F32)],
        compiler_params=_params(("arbitrary",)),
    )(h2, w, target)


def _conv_pre(cur, prev8, w_ref, b_ref, kw):
    y = b_ref[...] + _shift_down(cur, prev8, 0) * w_ref[kw - 1:kw, :]
    for k in range(kw - 1):
        y = y + _shift_down(cur, prev8, kw - 1 - k) * w_ref[k:k + 1, :]
    return y


def _row_tile(t):
    return _pick(t, (640, 512, 384, 256, 128))


def _ssd_conv_fwd(proj, col0, width, w, b, name):
    t = proj.shape[0]
    kw = w.shape[0]
    tr, tc = _row_tile(t), _pick(width, (512, 256, 128))
    c0, rb = col0 // tc, tr // SUBLANES
    assert col0 % tc == 0

    def body(x_ref, p_ref, w_ref, b_ref, o_ref):
        i = pl.program_id(1)
        prev8 = jnp.where(i > 0, p_ref[...], 0.0)
        pre = _conv_pre(x_ref[...], prev8, w_ref, b_ref, kw)
        rows = _iota((tr, 1), 0) + i * tr
        o_ref[...] = jnp.where(rows >= FRONT, _silu(pre), 0.0)

    return pl.pallas_call(
        body, name=name, grid=(width // tc, t // tr),
        in_specs=[pl.BlockSpec((tr, tc), lambda j, i: (i, c0 + j)),
                  pl.BlockSpec((SUBLANES, tc), lambda j, i: (jnp.maximum(i * rb - 1, 0), c0 + j)),
                  pl.BlockSpec((kw, tc), lambda j, i: (0, j)),
                  pl.BlockSpec((1, tc), lambda j, i: (0, j))],
        out_specs=pl.BlockSpec((tr, tc), lambda j, i: (i, j)),
        out_shape=jax.ShapeDtypeStruct((t, width), F32),
        compiler_params=_params(("parallel", "parallel")),
    )(proj, proj, w, b)


def _ssd_conv_bwd_pre(proj, col0, width, w, b, dact, name):
    t = proj.shape[0]
    kw = w.shape[0]
    tr, tc = _row_tile(t), _pick(width, (512, 256, 128))
    c0, rb = col0 // tc, tr // SUBLANES

    def body(x_ref, p_ref, w_ref, b_ref, d_ref, o_ref, gw_ref, gb_ref):
        i = pl.program_id(1)

        @pl.when(i == 0)
        def _():
            gw_ref[...] = jnp.zeros_like(gw_ref)
            gb_ref[...] = jnp.zeros_like(gb_ref)

        cur = x_ref[...]
        prev8 = jnp.where(i > 0, p_ref[...], 0.0)
        pre = _conv_pre(cur, prev8, w_ref, b_ref, kw)
        rows = _iota((tr, 1), 0) + i * tr
        dpre = jnp.where(rows >= FRONT, d_ref[...] * _dsilu(pre), 0.0)
        o_ref[...] = dpre
        gb_ref[...] += jnp.sum(dpre, axis=0, keepdims=True)
        for k in range(kw):
            gw_ref[k:k + 1, :] += jnp.sum(dpre * _shift_down(cur, prev8, kw - 1 - k), axis=0, keepdims=True)

    return pl.pallas_call(
        body, name=name, grid=(width // tc, t // tr),
        in_specs=[pl.BlockSpec((tr, tc), lambda j, i: (i, c0 + j)),
                  pl.BlockSpec((SUBLANES, tc), lambda j, i: (jnp.maximum(i * rb - 1, 0), c0 + j)),
                  pl.BlockSpec((kw, tc), lambda j, i: (0, j)),
                  pl.BlockSpec((1, tc), lambda j, i: (0, j)),
                  pl.BlockSpec((tr, tc), lambda j, i: (i, j))],
        out_specs=[pl.BlockSpec((tr, tc), lambda j, i: (i, j)),
                   pl.BlockSpec((SUBLANES, tc), lambda j, i: (0, j)),
                   pl.BlockSpec((1, tc), lambda j, i: (0, j))],
        out_shape=[jax.ShapeDtypeStruct((t, width), F32), jax.ShapeDtypeStruct((SUBLANES, width), F32),
                   jax.ShapeDtypeStruct((1, width), F32)],
        compiler_params=_params(("parallel", "arbitrary")),
    )(proj, proj, w, b, dact)


def _conv_bwd_dx(dpre, w, name):
    t, width = dpre.shape
    kw = w.shape[0]
    tr, tc = _row_tile(t), _pick(width, (512, 256, 128))
    rb, nrow = tr // SUBLANES, t // tr
    last8 = t // SUBLANES - 1

    def body(d_ref, n_ref, w_ref, o_ref):
        i = pl.program_id(1)
        cur = d_ref[...]
        next8 = jnp.where(i < nrow - 1, n_ref[...], 0.0)
        acc = cur * w_ref[kw - 1:kw, :]
        for k in range(kw - 1):
            acc = acc + _shift_up(cur, next8, kw - 1 - k) * w_ref[k:k + 1, :]
        rows = _iota((tr, 1), 0) + i * tr
        o_ref[...] = jnp.where(rows >= FRONT, acc, 0.0).astype(MXU_DTYPE)

    return pl.pallas_call(
        body, name=name, grid=(width // tc, nrow),
        in_specs=[pl.BlockSpec((tr, tc), lambda j, i: (i, j)),
                  pl.BlockSpec((SUBLANES, tc), lambda j, i: (jnp.minimum((i + 1) * rb, last8), j)),
                  pl.BlockSpec((kw, tc), lambda j, i: (0, j))],
        out_specs=pl.BlockSpec((tr, tc), lambda j, i: (i, j)),
        out_shape=jax.ShapeDtypeStruct((t, width), MXU_DTYPE),
        compiler_params=_params(("parallel", "parallel")),
    )(dpre, dpre, w)


def _ffn_conv_fwd(up, w, b, name):
    t, f2 = up.shape
    f = f2 // 2
    kw = w.shape[0]
    tr, tc = _row_tile(t), _pick(f, (256, 128))
    nf, rb = f // tc, tr // SUBLANES

    def body(xg, pg, xv, pv, wg, wv, bg, bv, o_ref):
        i = pl.program_id(1)
        ag = _conv_pre(xg[...], jnp.where(i > 0, pg[...], 0.0), wg, bg, kw)
        av = _conv_pre(xv[...], jnp.where(i > 0, pv[...], 0.0), wv, bv, kw)
        o_ref[...] = (_silu(ag) * av).astype(MXU_DTYPE)

    def cur(off):
        return pl.BlockSpec((tr, tc), lambda j, i: (i, j + off))

    def prev(off):
        return pl.BlockSpec((SUBLANES, tc), lambda j, i: (jnp.maximum(i * rb - 1, 0), j + off))

    def par(rows, off):
        return pl.BlockSpec((rows, tc), lambda j, i: (0, j + off))

    return pl.pallas_call(
        body, name=name, grid=(nf, t // tr),
        in_specs=[cur(0), prev(0), cur(nf), prev(nf), par(kw, 0), par(kw, nf), par(1, 0), par(1, nf)],
        out_specs=pl.BlockSpec((tr, tc), lambda j, i: (i, j)),
        out_shape=jax.ShapeDtypeStruct((t, f), MXU_DTYPE),
        compiler_params=_params(("parallel", "parallel")),
    )(up, up, up, up, w, w, b, b)


def _ffn_conv_bwd_pre(up, w, b, dact, name):
    t, f2 = up.shape
    f = f2 // 2
    kw = w.shape[0]
    tr, tc = _row_tile(t), _pick(f, (256, 128))
    nf, rb = f // tc, tr // SUBLANES

    def body(xg, pg, xv, pv, wg, wv, bg, bv, d_ref, o_ref, gw_ref, gb_ref):
        j, i = pl.program_id(0), pl.program_id(1)

        @pl.when(i == 0)
        def _():
            gw_ref[...] = jnp.zeros_like(gw_ref)
            gb_ref[...] = jnp.zeros_like(gb_ref)

        is_gate = j < nf
        cg, cv = xg[...], xv[...]
        p8g, p8v = jnp.where(i > 0, pg[...], 0.0), jnp.where(i > 0, pv[...], 0.0)
        ag = _conv_pre(cg, p8g, wg, bg, kw)
        av = _conv_pre(cv, p8v, wv, bv, kw)
        d = d_ref[...]
        da = jnp.where(is_gate, d * av * _dsilu(ag), d * _silu(ag))
        o_ref[...] = da
        gb_ref[...] += jnp.sum(da, axis=0, keepdims=True)
        cur = jnp.where(is_gate, cg, cv)
        p8 = jnp.where(is_gate, p8g, p8v)
        for k in range(kw):
            gw_ref[k:k + 1, :] += jnp.sum(da * _shift_down(cur, p8, kw - 1 - k), axis=0, keepdims=True)

    def cur(off):
        return pl.BlockSpec((tr, tc), lambda j, i: (i, j % nf + off))

    def prev(off):
        return pl.BlockSpec((SUBLANES, tc), lambda j, i: (jnp.maximum(i * rb - 1, 0), j % nf + off))

    def par(rows, off):
        return pl.BlockSpec((rows, tc), lambda j, i: (0, j % nf + off))

    return pl.pallas_call(
        body, name=name, grid=(2 * nf, t // tr),
        in_specs=[cur(0), prev(0), cur(nf), prev(nf), par(kw, 0), par(kw, nf), par(1, 0), par(1, nf),
                  pl.BlockSpec((tr, tc), lambda j, i: (i, j % nf))],
        out_specs=[pl.BlockSpec((tr, tc), lambda j, i: (i, j)),
                   pl.BlockSpec((SUBLANES, tc), lambda j, i: (0, j)),
                   pl.BlockSpec((1, tc), lambda j, i: (0, j))],
        out_shape=[jax.ShapeDtypeStruct((t, f2), F32), jax.ShapeDtypeStruct((SUBLANES, f2), F32),
                   jax.ShapeDtypeStruct((1, f2), F32)],
        compiler_params=_params(("parallel", "arbitrary")),
    )(up, up, up, up, w, w, b, b, dact)


def _gate_fwd(bs, br, proj, c_gs, c_gr, name):
    t, d = bs.shape
    tr = _row_tile(t)

    def body(bs_ref, br_ref, gs_ref, gr_ref, o_ref):
        o_ref[...] = (jax.nn.sigmoid(gs_ref[...]) * bs_ref[...] + jax.nn.sigmoid(gr_ref[...]) * br_ref[...]).astype(MXU_DTYPE)

    row = pl.BlockSpec((tr, d), lambda i: (i, 0))
    return pl.pallas_call(
        body, name=name, grid=(t // tr,),
        in_specs=[row, row, pl.BlockSpec((tr, d), lambda i: (i, c_gs // d)), pl.BlockSpec((tr, d), lambda i: (i, c_gr // d))],
        out_specs=row, out_shape=jax.ShapeDtypeStruct((t, d), MXU_DTYPE),
        compiler_params=_params(("parallel",)),
    )(bs, br, proj, proj)


def _gate_bwd(dm, bs, br, proj, c_gs, c_gr, name):
    t, d = bs.shape
    tr = _row_tile(t)

    def body(dm_ref, bs_ref, br_ref, gs_ref, gr_ref, dbs_ref, dbr_ref, dgs_ref, dgr_ref):
        g = dm_ref[...]
        ss, sr = jax.nn.sigmoid(gs_ref[...]), jax.nn.sigmoid(gr_ref[...])
        dbs_ref[...] = (g * ss).astype(MXU_DTYPE)
        dbr_ref[...] = (g * sr).astype(MXU_DTYPE)
        dgs_ref[...] = (g * bs_ref[...] * ss * (1.0 - ss)).astype(MXU_DTYPE)
        dgr_ref[...] = (g * br_ref[...] * sr * (1.0 - sr)).astype(MXU_DTYPE)

    row = pl.BlockSpec((tr, d), lambda i: (i, 0))
    out = jax.ShapeDtypeStruct((t, d), MXU_DTYPE)
    return pl.pallas_call(
        body, name=name, grid=(t // tr,),
        in_specs=[row, row, row, pl.BlockSpec((tr, d), lambda i: (i, c_gs // d)), pl.BlockSpec((tr, d), lambda i: (i, c_gr // d))],
        out_specs=[row] * 4, out_shape=[out] * 4,
        compiler_params=_params(("parallel",)),
    )(dm, bs, br, proj, proj)


def _ret_consts(h):
    lg = math.log(1.0 - 2.0 ** (-5.0 - h))
    l = _iota((CHUNK, 1), 0).astype(F32)
    diff = l - _iota((1, CHUNK), 1).astype(F32)
    dm = jnp.exp(jnp.where(diff >= 0, diff * lg, -jnp.inf))
    dmt = jnp.exp(jnp.where(diff <= 0, -diff * lg, -jnp.inf))
    cs = jnp.exp((l + 1.0) * lg)
    kdec = jnp.exp((CHUNK - 1.0 - l) * lg)
    return dm, dmt, cs, kdec, math.exp(CHUNK * lg)


def _ret_fwd(proj, c_q, c_k, c_v, c_g, cos, sin, d, name):
    t = proj.shape[0]
    nc = t // CHUNK
    hq, hv = d // RET_HEADS, 2 * d // RET_HEADS
    half = hq // 2
    scale = hq ** -0.5

    def body(q_ref, k_ref, v_ref, g_ref, cos_ref, sin_ref, o_ref, y_ref, qr_ref, kr_ref, st_ref, rs):
        @pl.when(pl.program_id(0) == 0)
        def _():
            rs[...] = jnp.zeros_like(rs)

        co, si = cos_ref[...], sin_ref[...]
        for h in range(RET_HEADS):
            dm, _, cs, kdec, gam = _ret_consts(h)
            q1, q2 = q_ref[:, h * hq:h * hq + half], q_ref[:, h * hq + half:(h + 1) * hq]
            k1, k2 = k_ref[:, h * hq:h * hq + half], k_ref[:, h * hq + half:(h + 1) * hq]
            qr = jnp.concatenate([q1 * co - q2 * si, q2 * co + q1 * si], axis=1)
            kr = jnp.concatenate([k1 * co - k2 * si, k2 * co + k1 * si], axis=1) * scale
            qr_ref[:, h * hq:(h + 1) * hq] = qr.astype(MXU_DTYPE)
            kr_ref[:, h * hq:(h + 1) * hq] = kr.astype(MXU_DTYPE)
            v = v_ref[:, h * hv:(h + 1) * hv]
            r_in = rs[h * hq:(h + 1) * hq, :]
            st_ref[0, h * hq:(h + 1) * hq, :] = r_in.astype(MXU_DTYPE)
            s = _dot_nt(qr, kr) * dm
            o = _dot(s, v) + cs * _dot(qr, r_in)
            rs[h * hq:(h + 1) * hq, :] = gam * r_in + _dot_tn(kr * kdec, v)
            o_ref[:, h * hv:(h + 1) * hv] = o
            on = o * lax.rsqrt(jnp.mean(o * o, axis=1, keepdims=True) + EPS)
            y_ref[:, h * hv:(h + 1) * hv] = (_silu(g_ref[:, h * hv:(h + 1) * hv]) * on).astype(MXU_DTYPE)

    def col(width, c0):
        return pl.BlockSpec((CHUNK, width), lambda i: (i, c0 // width))

    tab = pl.BlockSpec((CHUNK, half), lambda i: (i, 0))
    return pl.pallas_call(
        body, name=name, grid=(nc,),
        in_specs=[col(d, c_q), col(d, c_k), col(2 * d, c_v), col(2 * d, c_g), tab, tab],
        out_specs=[col(2 * d, 0), col(2 * d, 0), col(d, 0), col(d, 0),
                   pl.BlockSpec((1, d, hv), lambda i: (i, 0, 0))],
        out_shape=[jax.ShapeDtypeStruct((t, 2 * d), F32), jax.ShapeDtypeStruct((t, 2 * d), MXU_DTYPE),
                   jax.ShapeDtypeStruct((t, d), MXU_DTYPE), jax.ShapeDtypeStruct((t, d), MXU_DTYPE),
                   jax.ShapeDtypeStruct((nc, d, hv), MXU_DTYPE)],
        scratch_shapes=[pltpu.VMEM((d, hv), F32)],
        compiler_params=_params(("arbitrary",)),
    )(proj, proj, proj, proj, cos, sin)


def _ret_bwd(dy, proj, c_v, c_g, o, qr, kr, st, cos, sin, d, name):
    t = proj.shape[0]
    nc = t // CHUNK
    hq, hv = d // RET_HEADS, 2 * d // RET_HEADS
    half = hq // 2
    scale = hq ** -0.5

    def body(dy_ref, v_ref, g_ref, o_ref, qr_ref, kr_ref, st_ref, cos_ref, sin_ref,
             dq_ref, dk_ref, dv_ref, dg_ref, drs):
        @pl.when(pl.program_id(0) == 0)
        def _():
            drs[...] = jnp.zeros_like(drs)

        co, si = cos_ref[...], sin_ref[...]
        for h in range(RET_HEADS):
            dm, dmt, cs, kdec, gam = _ret_consts(h)
            vs = slice(h * hv, (h + 1) * hv)
            qs = slice(h * hq, (h + 1) * hq)
            o_h = o_ref[:, vs]
            g_h = g_ref[:, vs]
            d_y = dy_ref[:, vs]
            r = lax.rsqrt(jnp.mean(o_h * o_h, axis=1, keepdims=True) + EPS)
            on = o_h * r
            d_on = d_y * _silu(g_h)
            dg_ref[:, vs] = (d_y * on * _dsilu(g_h)).astype(MXU_DTYPE)
            d_o = r * (d_on - on * jnp.mean(d_on * on, axis=1, keepdims=True))
            q_h, k_h, v_h = qr_ref[:, qs], kr_ref[:, qs], v_ref[:, vs]
            r_in = st_ref[0, qs, :]
            dr_n = drs[qs, :]
            csdo = cs * d_o
            ds = _dot_nt(d_o, v_h) * dm
            dst = _dot_nt(v_h, d_o) * dmt
            s_t = _dot_nt(k_h, q_h) * dmt
            dqr = _dot(ds, k_h) + _dot_nt(csdo, r_in)
            dkr = _dot(dst, q_h) + kdec * _dot_nt(v_h, dr_n)
            dv_ref[:, vs] = (_dot(s_t, d_o) + _dot(k_h.astype(F32) * kdec, dr_n)).astype(MXU_DTYPE)
            drs[qs, :] = gam * dr_n + _dot_tn(q_h, csdo)
            a1, a2 = dqr[:, :half], dqr[:, half:]
            dq_ref[:, qs] = jnp.concatenate([a1 * co + a2 * si, a2 * co - a1 * si], axis=1).astype(MXU_DTYPE)
            b1, b2 = dkr[:, :half] * scale, dkr[:, half:] * scale
            dk_ref[:, qs] = jnp.concatenate([b1 * co + b2 * si, b2 * co - b1 * si], axis=1).astype(MXU_DTYPE)

    def col(width, c0=0):
        return pl.BlockSpec((CHUNK, width), lambda i: (nc - 1 - i, c0 // width))

    tab = pl.BlockSpec((CHUNK, half), lambda i: (nc - 1 - i, 0))
    return pl.pallas_call(
        body, name=name, grid=(nc,),
        in_specs=[col(2 * d), col(2 * d, c_v), col(2 * d, c_g), col(2 * d), col(d), col(d),
                  pl.BlockSpec((1, d, hv), lambda i: (nc - 1 - i, 0, 0)), tab, tab],
        out_specs=[col(d), col(d), col(2 * d), col(2 * d)],
        out_shape=[jax.ShapeDtypeStruct((t, d), MXU_DTYPE), jax.ShapeDtypeStruct((t, d), MXU_DTYPE),
                   jax.ShapeDtypeStruct((t, 2 * d), MXU_DTYPE), jax.ShapeDtypeStruct((t, 2 * d), MXU_DTYPE)],
        scratch_shapes=[pltpu.VMEM((d, hv), F32)],
        compiler_params=_params(("arbitrary",)),
    )(dy, proj, proj, o, qr, kr, st, cos, sin)


def _ssd_small(dtraw_ref, dtb_ref, alog_ref, chunk_idx, nh):
    rows = _iota((CHUNK, 1), 0)
    ok = ((rows >= FRONT) | (chunk_idx > 0)) & (_iota((1, LANES), 1) < nh)
    z = dtraw_ref[...] + dtb_ref[...]
    dt = jnp.where(ok, jax.nn.softplus(z), 0.0)
    sig = jnp.where(ok, jax.nn.sigmoid(z), 0.0)
    a = jnp.where(_iota((1, LANES), 1) < nh, -jnp.exp(alog_ref[...]), 0.0)
    tri = (_iota((CHUNK, CHUNK), 0) >= _iota((CHUNK, CHUNK), 1)).astype(F32)
    acs = _dot_exact(tri, dt * a)
    return dt, sig, a, acs, acs.T


def _head_expand(g, hpg, gw):
    shift = int(math.log2(SSD_HEAD_DIM))
    return (_iota((LANES, gw), 0) == g * hpg + lax.shift_right_logical(_iota((LANES, gw), 1), shift)).astype(F32)


def _ssd_fwd(xa, proj, c_dt, c_z, dtb, alog, dvec, nw, di, name):
    t = xa.shape[0]
    nc = t // CHUNK
    nh = di // SSD_HEAD_DIM
    hpg = nh // SSD_GROUPS
    gw = di // SSD_GROUPS
    n = SSD_STATE
    gn = SSD_GROUPS * n
    hd = SSD_HEAD_DIM

    def body(x_ref, b_ref, c_ref, dtraw_ref, z_ref, dtb_ref, alog_ref, d_ref, nw_ref,
             y_ref, ys_ref, st_ref, hts, xdt_s):
        c = pl.program_id(0)

        @pl.when(c == 0)
        def _():
            hts[...] = jnp.zeros_like(hts)

        dt, _, _, acs, acs_t = _ssd_small(dtraw_ref, dtb_ref, alog_ref, c, nh)
        tri = _iota((CHUNK, CHUNK), 0) >= _iota((CHUNK, CHUNK), 1)
        dvec8 = jnp.broadcast_to(d_ref[...], (SUBLANES, LANES))
        for g in range(SSD_GROUPS):
            gs = slice(g * gw, (g + 1) * gw)
            ns = slice(g * n, (g + 1) * n)
            e_mat = _head_expand(g, hpg, gw)
            ax = _dot_exact(acs, e_mat)
            dtx = _dot_exact(dt, e_mat)
            dx = _dot_exact(dvec8, e_mat)[0:1, :]
            xg, bg, cg = x_ref[:, gs], b_ref[:, ns], c_ref[:, ns]
            xdt = xg * dtx
            xdt_s[...] = xdt.astype(MXU_DTYPE)
            cb = _dot_nt(cg, bg)
            ht = hts[ns, :]
            st_ref[0, ns, :] = ht.astype(MXU_DTYPE)
            y_ref[:, gs] = jnp.exp(ax) * _dot(cg, ht) + dx * xg
            for hh in range(hpg):
                h = g * hpg + hh
                lmat = jnp.exp(jnp.where(tri, acs[:, h:h + 1] - acs_t[h:h + 1, :], -jnp.inf))
                hs = slice(g * gw + hh * hd, g * gw + (hh + 1) * hd)
                y_ref[:, hs] += _dot(cb * lmat, xdt_s[:, hh * hd:(hh + 1) * hd])
            aend = ax[CHUNK - 1:CHUNK, :]
            hts[ns, :] = jnp.exp(aend) * ht + _dot_tn(bg, xdt * jnp.exp(aend - ax))
        for g in range(SSD_GROUPS):
            gs = slice(g * gw, (g + 1) * gw)
            yz = y_ref[:, gs] * _silu(z_ref[:, gs])
            r = lax.rsqrt(jnp.mean(yz * yz, axis=1, keepdims=True) + EPS)
            ys_ref[:, gs] = (yz * r * nw_ref[:, gs]).astype(MXU_DTYPE)

    def col(width, c0, arr_is_xa=False):
        return pl.BlockSpec((CHUNK, width), lambda i: (i, c0 // width))

    vec = pl.BlockSpec((1, LANES), lambda i: (0, 0))
    assert di % gn == 0 and c_dt % LANES == 0 and c_z % di == 0
    return pl.pallas_call(
        body, name=name, grid=(nc,),
        in_specs=[col(di, 0), col(gn, di), col(gn, di + gn), col(LANES, c_dt), col(di, c_z), vec, vec, vec,
                  pl.BlockSpec((1, di), lambda i: (0, 0))],
        out_specs=[col(di, 0), col(di, 0), pl.BlockSpec((1, gn, gw), lambda i: (i, 0, 0))],
        out_shape=[jax.ShapeDtypeStruct((t, di), F32), jax.ShapeDtypeStruct((t, di), MXU_DTYPE),
                   jax.ShapeDtypeStruct((nc, gn, gw), MXU_DTYPE)],
        scratch_shapes=[pltpu.VMEM((gn, gw), F32), pltpu.VMEM((CHUNK, gw), MXU_DTYPE)],
        compiler_params=_params(("arbitrary",)),
    )(xa, xa, xa, proj, proj, dtb, alog, dvec, nw)


def _ssd_bwd(dys, xa, proj, c_dt, c_z, ypre, st, dtb, alog, dvec, nw, di, name):
    t = xa.shape[0]
    nc = t // CHUNK
    nh = di // SSD_HEAD_DIM
    hpg = nh // SSD_GROUPS
    gw = di // SSD_GROUPS
    n = SSD_STATE
    gn = SSD_GROUPS * n
    hd = SSD_HEAD_DIM

    def body(dys_ref, x_ref, b_ref, c_ref, dtraw_ref, z_ref, y_ref, st_ref, dtb_ref, alog_ref, d_ref, nw_ref,
             dxa_ref, dz_ref, ddt_ref, gb_ref, ga_ref, gd_ref, gnw_ref, dhts, dy_s, xdt_s, dxdt_s):
        i = pl.program_id(0)
        c = nc - 1 - i

        @pl.when(i == 0)
        def _():
            dhts[...] = jnp.zeros_like(dhts)
            gb_ref[...] = jnp.zeros_like(gb_ref)
            ga_ref[...] = jnp.zeros_like(ga_ref)
            gd_ref[...] = jnp.zeros_like(gd_ref)
            gnw_ref[...] = jnp.zeros_like(gnw_ref)

        dt, sig, a, acs, acs_t = _ssd_small(dtraw_ref, dtb_ref, alog_ref, c, nh)
        tri = _iota((CHUNK, CHUNK), 0) >= _iota((CHUNK, CHUNK), 1)
        triu = _iota((CHUNK, CHUNK), 0) <= _iota((CHUNK, CHUNK), 1)
        lane = _iota((1, LANES), 1)
        rows = _iota((CHUNK, 1), 0)
        dvec8 = jnp.broadcast_to(d_ref[...], (SUBLANES, LANES))
        da = jnp.zeros((CHUNK, LANES), F32)
        ddt = jnp.zeros((CHUNK, LANES), F32)
        gd = jnp.zeros((1, LANES), F32)
        for g in range(SSD_GROUPS):
            gs = slice(g * gw, (g + 1) * gw)
            ns = slice(g * n, (g + 1) * n)
            y_g, z_g = y_ref[:, gs], z_ref[:, gs]
            sz = _silu(z_g)
            yz = y_g * sz
            r = lax.rsqrt(jnp.mean(yz * yz, axis=1, keepdims=True) + EPS)
            nrm = yz * r
            dyo = dys_ref[:, gs]
            gnw_ref[:, gs] += jnp.sum(dyo * nrm, axis=0, keepdims=True)
            dn = dyo * nw_ref[:, gs]
            dyz = r * (dn - nrm * jnp.mean(dn * nrm, axis=1, keepdims=True))
            dz_ref[:, gs] = (dyz * y_g * _dsilu(z_g)).astype(MXU_DTYPE)
            dy_g = dyz * sz
            dy_s[...] = dy_g.astype(MXU_DTYPE)
            e_mat = _head_expand(g, hpg, gw)
            ax = _dot_exact(acs, e_mat)
            dtx = _dot_exact(dt, e_mat)
            dx = _dot_exact(dvec8, e_mat)[0:1, :]
            xg, bg, cg = x_ref[:, gs], b_ref[:, ns], c_ref[:, ns]
            xdt = xg * dtx
            xdt_s[...] = xdt.astype(MXU_DTYPE)
            aend = ax[CHUNK - 1:CHUNK, :]
            e = jnp.exp(aend - ax)
            ea = jnp.exp(ax)
            eend = jnp.exp(aend)
            htp = st_ref[0, ns, :].astype(F32)
            dht = dhts[ns, :]
            cb = _dot_nt(cg, bg)
            cbt = _dot_nt(bg, cg)
            q = _dot(bg, dht)
            dxdt_s[...] = e * q
            wl = e * q * xdt
            d_b = _dot_nt(e * xdt, dht)
            yi = ea * _dot(cg, htp)
            eady = ea * dy_g
            d_c = _dot_nt(eady, htp)
            t1 = jnp.sum(dht * htp, axis=0, keepdims=True) * eend
            dhts[ns, :] = eend * dht + _dot_tn(cg, eady)
            da = da + _dot_exact(dy_g * yi - wl, e_mat, (((1,), (1,)), ((), ())))
            tail = jnp.broadcast_to(jnp.sum(wl, axis=0, keepdims=True) + t1, (SUBLANES, gw))
            da_end = _dot_exact(tail, e_mat, (((1,), (1,)), ((), ())))[0:1, :]
            da = da + jnp.where(rows == CHUNK - 1, da_end, 0.0)
            dcb = jnp.zeros((CHUNK, CHUNK), F32)
            dcbt = jnp.zeros((CHUNK, CHUNK), F32)
            for hh in range(hpg):
                h = g * hpg + hh
                a_col, a_row = acs[:, h:h + 1], acs_t[h:h + 1, :]
                lmat = jnp.exp(jnp.where(tri, a_col - a_row, -jnp.inf))
                lmat_t = jnp.exp(jnp.where(triu, a_row - a_col, -jnp.inf))
                m, mt = cb * lmat, cbt * lmat_t
                hl = slice(hh * hd, (hh + 1) * hd)
                dy_h, xdt_h = dy_s[:, hl], xdt_s[:, hl]
                dxdt_s[:, hl] += _dot(mt, dy_h)
                dm = _dot_nt(dy_h, xdt_h)
                dmt = _dot_nt(xdt_h, dy_h)
                dcb = dcb + dm * lmat
                dcbt = dcbt + dmt * lmat_t
                da_h = jnp.sum(dm * m, axis=1, keepdims=True) - jnp.sum(dmt * mt, axis=1, keepdims=True)
                da = da + jnp.where(lane == h, da_h, 0.0)
            d_c = d_c + _dot(dcb, bg)
            d_b = d_b + _dot(dcbt, cg)
            dxdt = dxdt_s[...]
            dxa_ref[:, gs] = dxdt * dtx + dx * dy_g
            dxa_ref[:, di + g * n:di + (g + 1) * n] = d_b
            dxa_ref[:, di + gn + g * n:di + gn + (g + 1) * n] = d_c
            ddt = ddt + _dot_exact(dxdt * xg, e_mat, (((1,), (1,)), ((), ())))
            gd8 = jnp.broadcast_to(jnp.sum(dy_g * xg, axis=0, keepdims=True), (SUBLANES, gw))
            gd = gd + _dot_exact(gd8, e_mat, (((1,), (1,)), ((), ())))[0:1, :]
        triu_f = triu.astype(F32)
        ddta = _dot_exact(triu_f, da)
        ddt = ddt + ddta * a
        draw = ddt * sig
        ddt_ref[...] = draw.astype(MXU_DTYPE)
        gb_ref[...] += jnp.sum(draw, axis=0, keepdims=True)
        ga_ref[...] += jnp.sum(ddta * dt, axis=0, keepdims=True) * a
        gd_ref[...] += gd

    def col(width, c0):
        return pl.BlockSpec((CHUNK, width), lambda i: (nc - 1 - i, c0 // width))

    vec = pl.BlockSpec((1, LANES), lambda i: (0, 0))
    wide = pl.BlockSpec((1, di), lambda i: (0, 0))
    wa = di + 2 * gn
    return pl.pallas_call(
        body, name=name, grid=(nc,),
        in_specs=[col(di, 0), col(di, 0), col(gn, di), col(gn, di + gn), col(LANES, c_dt), col(di, c_z), col(di, 0),
                  pl.BlockSpec((1, gn, gw), lambda i: (nc - 1 - i, 0, 0)), vec, vec, vec, wide],
        out_specs=[col(wa, 0), col(di, 0), col(LANES, 0), vec, vec, vec, wide],
        out_shape=[jax.ShapeDtypeStruct((t, wa), F32), jax.ShapeDtypeStruct((t, di), MXU_DTYPE),
                   jax.ShapeDtypeStruct((t, LANES), MXU_DTYPE), jax.ShapeDtypeStruct((1, LANES), F32),
                   jax.ShapeDtypeStruct((1, LANES), F32), jax.ShapeDtypeStruct((1, LANES), F32),
                   jax.ShapeDtypeStruct((1, di), F32)],
        scratch_shapes=[pltpu.VMEM((gn, gw), F32), pltpu.VMEM((CHUNK, gw), MXU_DTYPE),
                        pltpu.VMEM((CHUNK, gw), MXU_DTYPE), pltpu.VMEM((CHUNK, gw), F32)],
        compiler_params=_params(("arbitrary",)),
    )(dys, xa, xa, xa, proj, proj, ypre, st, dtb, alog, dvec, nw)


def _adam_math(w, g, m, v):
    m2 = ADAM_B1 * m + (1.0 - ADAM_B1) * g
    v2 = ADAM_B2 * v + (1.0 - ADAM_B2) * (g * g)
    m_hat = m2 / (1.0 - ADAM_B1 ** ADAM_STEP)
    v_hat = v2 / (1.0 - ADAM_B2 ** ADAM_STEP)
    delta = -ADAM_LR * (m_hat / (jnp.sqrt(v_hat) + ADAM_EPS) + ADAM_WD * w)
    return delta, m2, v2


def _adam_big(w, p_mine, p_sib, m, v, name):
    r, c = w.shape
    tr = _pick(r, (128, 64, 32, 16, 8))

    def body(w_ref, a_ref, b_ref, m_ref, v_ref, g_ref, d_ref, m2_ref, v2_ref):
        g = a_ref[...] + b_ref[...]
        delta, m2, v2 = _adam_math(w_ref[...], g, m_ref[...], v_ref[...])
        g_ref[...] = g
        d_ref[...] = delta
        m2_ref[...] = m2
        v2_ref[...] = v2

    blk = pl.BlockSpec((tr, c), lambda i: (i, 0))
    out = jax.ShapeDtypeStruct((r, c), F32)
    return pl.pallas_call(
        body, name=name, grid=(r // tr,), in_specs=[blk] * 5, out_specs=[blk] * 4, out_shape=[out] * 4,
        compiler_params=_params(("parallel",)),
    )(w, p_mine, p_sib, m, v)


def _sum4(parts, name):
    _, r, c = parts.shape
    tr = _pick(r, (128, 64, 32, 16, 8))

    def body(p_ref, o_ref):
        acc = p_ref[0].astype(F32)
        for j in range(1, 4):
            acc = acc + p_ref[j].astype(F32)
        o_ref[...] = acc

    return pl.pallas_call(
        body, name=name, grid=(r // tr,),
        in_specs=[pl.BlockSpec((4, tr, c), lambda i: (0, i, 0))],
        out_specs=pl.BlockSpec((tr, c), lambda i: (i, 0)),
        out_shape=jax.ShapeDtypeStruct((r, c), F32),
        compiler_params=_params(("parallel",)),
    )(parts)


def _adam_small(w, gparts, m, v, name):
    r = w.shape[0]

    def body(w_ref, g_ref, m_ref, v_ref, go_ref, d_ref, m2_ref, v2_ref):
        g = g_ref[0]
        for j in range(1, 8):
            g = g + g_ref[j]
        delta, m2, v2 = _adam_math(w_ref[...], g, m_ref[...], v_ref[...])
        go_ref[...] = g
        d_ref[...] = delta
        m2_ref[...] = m2
        v2_ref[...] = v2

    out = jax.ShapeDtypeStruct((r, LANES), F32)
    return pl.pallas_call(body, name=name, out_shape=[out] * 4)(w, gparts, m, v)


def _chip_exchange(arrays, scatter, name):
    n = len(arrays)
    outs = [jax.ShapeDtypeStruct((4,) + (a.shape[1:] if scatter else a.shape), a.dtype) for a in arrays]

    def body(*refs):
        srcs, dsts = refs[:n], refs[n:2 * n]
        send, recv, loc = refs[2 * n:]
        x, y, c = lax.axis_index("x"), lax.axis_index("y"), lax.axis_index("c")
        mine = 2 * x + y
        peers = [(1 - x, y), (x, 1 - y), (1 - x, 1 - y)]
        copies, locals_ = [], []
        for i in range(n):
            own = srcs[i].at[mine] if scatter else srcs[i]
            lc = pltpu.make_async_copy(own, dsts[i].at[mine], loc.at[i])
            lc.start()
            locals_.append(lc)
            for k, (px, py) in enumerate(peers):
                src = srcs[i].at[2 * px + py] if scatter else srcs[i]
                cp = pltpu.make_async_remote_copy(src_ref=src, dst_ref=dsts[i].at[mine], send_sem=send.at[i, k],
                                                  recv_sem=recv.at[i, k], device_id=(px, py, c), device_id_type=MESH)
                cp.start()
                copies.append(cp)
        for cp in copies:
            cp.wait()
        for lc in locals_:
            lc.wait()

    anyspec = pl.BlockSpec(memory_space=pl.ANY)
    return pl.pallas_call(
        body, name=name, in_specs=[anyspec] * n, out_specs=[anyspec] * n, out_shape=outs,
        scratch_shapes=[pltpu.SemaphoreType.DMA((n, 3)), pltpu.SemaphoreType.DMA((n, 3)), pltpu.SemaphoreType.DMA((n,))],
        compiler_params=pltpu.CompilerParams(has_side_effects=True),
    )(*arrays)


def _sibling_swap(arrays, name):
    n = len(arrays)

    def body(*refs):
        srcs, dsts = refs[:n], refs[n:2 * n]
        send, recv = refs[2 * n:]
        x, y, c = lax.axis_index("x"), lax.axis_index("y"), lax.axis_index("c")
        copies = []
        for i in range(n):
            cp = pltpu.make_async_remote_copy(src_ref=srcs[i], dst_ref=dsts[i], send_sem=send.at[i], recv_sem=recv.at[i],
                                              device_id=(x, y, 1 - c), device_id_type=MESH)
            cp.start()
            copies.append(cp)
        for cp in copies:
            cp.wait()

    anyspec = pl.BlockSpec(memory_space=pl.ANY)
    return pl.pallas_call(
        body, name=name, in_specs=[anyspec] * n, out_specs=[anyspec] * n,
        out_shape=[jax.ShapeDtypeStruct(a.shape, a.dtype) for a in arrays],
        scratch_shapes=[pltpu.SemaphoreType.DMA((n,)), pltpu.SemaphoreType.DMA((n,))],
        compiler_params=pltpu.CompilerParams(has_side_effects=True),
    )(*arrays)


def _allgather8(v, name):
    def body(src, dst, send, recv, loc):
        x, y, c = lax.axis_index("x"), lax.axis_index("y"), lax.axis_index("c")
        mine = 4 * x + 2 * y + c
        lc = pltpu.make_async_copy(src, dst.at[mine], loc)
        lc.start()
        copies = []
        for k in range(1, 8):
            fx, fy, fc = (k >> 2) & 1, (k >> 1) & 1, k & 1
            peer = (1 - x if fx else x, 1 - y if fy else y, 1 - c if fc else c)
            cp = pltpu.make_async_remote_copy(src_ref=src, dst_ref=dst.at[mine], send_sem=send.at[k - 1],
                                              recv_sem=recv.at[k - 1], device_id=peer, device_id_type=MESH)
            cp.start()
            copies.append(cp)
        for cp in copies:
            cp.wait()
        lc.wait()

    anyspec = pl.BlockSpec(memory_space=pl.ANY)
    return pl.pallas_call(
        body, name=name, in_specs=[anyspec], out_specs=anyspec,
        out_shape=jax.ShapeDtypeStruct((8,) + v.shape, v.dtype),
        scratch_shapes=[pltpu.SemaphoreType.DMA((7,)), pltpu.SemaphoreType.DMA((7,)), pltpu.SemaphoreType.DMA],
        compiler_params=pltpu.CompilerParams(has_side_effects=True),
    )(v)


def _pack(parts):
    flat = jnp.concatenate([p.reshape(-1).astype(F32) for p in parts])
    pad = (-flat.shape[0]) % (SUBLANES * LANES)
    return jnp.pad(flat, (0, pad)).reshape(-1, LANES)


def _unpack(slab, shapes):
    flat = slab.reshape(-1)
    out, off = [], 0
    for s in shapes:
        size = int(np.prod(s))
        out.append(flat[off:off + size].reshape(s))
        off += size
    return out


def _pad_lanes(v):
    return jnp.pad(v.reshape(1, -1), ((0, 0), (0, LANES - v.shape[-1])))


def kernel(x, meta_tokens, mix_norm_w, w_in, ssd_conv_w, ssd_conv_b, ssd_dt_bias, ssd_A_log, ssd_D, ssd_norm_w, w_branch_ssd, w_branch_ret, w_out, ffn_norm_w, w_up, ffn_conv_w, ffn_conv_b, w_down, final_norm_w, loss_target, m_meta_tokens, m_mix_norm_w, m_w_in, m_ssd_conv_w, m_ssd_conv_b, m_ssd_dt_bias, m_ssd_A_log, m_ssd_D, m_ssd_norm_w, m_w_branch_ssd, m_w_branch_ret, m_w_out, m_ffn_norm_w, m_w_up, m_ffn_conv_w, m_ffn_conv_b, m_w_down, m_final_norm_w, v_meta_tokens, v_mix_norm_w, v_w_in, v_ssd_conv_w, v_ssd_conv_b, v_ssd_dt_bias, v_ssd_A_log, v_ssd_D, v_ssd_norm_w, v_w_branch_ssd, v_w_branch_ret, v_w_out, v_ffn_norm_w, v_w_up, v_ffn_conv_w, v_ffn_conv_b, v_w_down, v_final_norm_w):
    seq, d = x.shape[1], x.shape[2]
    t = CHUNK + seq
    di = 2 * d
    nh = di // SSD_HEAD_DIM
    gn = SSD_GROUPS * SSD_STATE
    cw = di + 2 * gn
    f = w_down.shape[1] * 4
    chip = 2 * lax.axis_index("x") + lax.axis_index("y")

    order = [("z", di), ("v", di), ("g", di), ("xbc", cw), ("q", d), ("k", d), ("gs", d), ("gr", d), ("dt", LANES)]
    col, acc = {}, 0
    for nm, wd in order:
        col[nm] = acc
        acc += wd
    wp = acc
    ref_order = [("z", di), ("xbc", cw), ("dt", nh), ("q", d), ("k", d), ("v", di), ("g", di), ("gs", d), ("gr", d)]
    ref_off, acc = {}, 0
    for nm, wd in ref_order:
        ref_off[nm] = (acc, wd)
        acc += wd
    in_dim = acc

    small_shapes = [meta_tokens.shape, ssd_conv_w.shape[1:], ffn_conv_w.shape[1:]]
    small_local = _pack([meta_tokens, ssd_conv_w[0], ffn_conv_w[0]])
    big_local = [w_in[0], w_branch_ssd[0], w_branch_ret[0], w_out[0], w_up[0], w_down[0]]
    gathered = _chip_exchange([a.astype(WIRE_DTYPE) for a in big_local] + [small_local], False, "gather_weights")
    g_in, g_bs, g_br, g_out, g_up, g_down, g_small = gathered
    w_in_full = jnp.moveaxis(g_in, 0, 1).reshape(d, in_dim)
    pieces = []
    for nm, wd in order:
        o, rw = ref_off[nm]
        p = w_in_full[:, o:o + rw]
        if rw < wd:
            p = jnp.pad(p, ((0, 0), (0, wd - rw)))
        pieces.append(p)
    w_p = jnp.concatenate(pieces, axis=1)
    w_bs = g_bs.reshape(di, d)
    w_br = g_br.reshape(di, d)
    w_o = g_out.reshape(d, d)
    w_u = jnp.moveaxis(g_up, 0, 1).reshape(d, 2 * f)
    w_d = g_down.reshape(f, d)
    smalls = [_unpack(g_small[j], small_shapes) for j in range(4)]
    meta_full = jnp.concatenate([s[0] for s in smalls], axis=1)
    scw = jnp.concatenate([s[1] for s in smalls], axis=1)
    fcw = jnp.concatenate([s[2] for s in smalls], axis=1)
    scb, fcb = ssd_conv_b, ffn_conv_b
    dtb, alog, dvec = _pad_lanes(ssd_dt_bias), _pad_lanes(ssd_A_log), _pad_lanes(ssd_D)
    fin_w = final_norm_w.reshape(1, d)

    hq = d // RET_HEADS
    pos = jnp.arange(t, dtype=F32) - FRONT
    inv_freq = ROPE_BASE ** (-jnp.linspace(0.0, 1.0, hq // 2, dtype=F32))
    ang = pos[:, None] * inv_freq[None, :]
    cos, sin = jnp.cos(ang), jnp.sin(ang)

    h0 = jnp.concatenate([jnp.zeros((FRONT, d), F32), meta_full, x[0]], axis=0)
    tm = _row_tile(t)
    u1 = _rms_fwd(h0, mix_norm_w, "rms1_fwd")
    proj = _mm(u1, w_p, "nn", F32, "proj", tm, _pick(wp, (1920, 1536, 1280, 1024, 896, 768, 640, 512, 384, 256, 128)), d)
    xa = _ssd_conv_fwd(proj, col["xbc"], cw, scw, scb, "ssd_conv_fwd")
    ypre, yssd, st_ssd = _ssd_fwd(xa, proj, col["dt"], col["z"], dtb, alog, dvec, ssd_norm_w, di, "ssd_fwd")
    o_ret, yret, qr, kr, st_ret = _ret_fwd(proj, col["q"], col["k"], col["v"], col["g"], cos, sin, d, "ret_fwd")
    tn_d = _pick(d, (1024, 512, 256, 128))
    bs = _mm(yssd, w_bs, "nn", F32, "branch_ssd", tm, tn_d, _pick(di, (1024, 512, 256)))
    br = _mm(yret, w_br, "nn", F32, "branch_ret", tm, tn_d, _pick(di, (1024, 512, 256)))
    merged = _gate_fwd(bs, br, proj, col["gs"], col["gr"], "gate_fwd")
    h1 = _mm(merged, w_o, "nn", F32, "out_proj", tm, tn_d, d, res=h0)
    u2 = _rms_fwd(h1, ffn_norm_w, "rms2_fwd")
    tn_f = _pick(2 * f, (1408, 1024, 768, 512, 256, 128))
    up = _mm(u2, w_u, "nn", F32, "up_proj", tm, tn_f, d)
    act = _ffn_conv_fwd(up, fcw, fcb, "ffn_conv_fwd")
    tk_f = _pick(f, (1408, 768, 704, 512, 256, 128))
    h2 = _mm(act, w_d, "nn", F32, "down_proj", tm, tn_d, tk_f, res=h1)
    loss8, d_h2, g_fin = _loss_bwd(h2, fin_w, loss_target[0], "loss_head")

    d_act = _mm(d_h2, w_d, "nt", F32, "d_act", tm, tk_f, d)
    g_wd = _mm(act, d_h2, "tn", WIRE_DTYPE, "g_w_down", tk_f, tn_d, tm)
    d_a, g_fcw, g_fcb = _ffn_conv_bwd_pre(up, fcw, fcb, d_act, "ffn_conv_bwd_pre")
    d_up = _conv_bwd_dx(d_a, fcw, "ffn_conv_bwd_dx")
    d_u2 = _mm(d_up, w_u, "nt", F32, "d_u2", tm, tn_d, tn_f)
    g_wu = _mm(u2, d_up, "tn", WIRE_DTYPE, "g_w_up", tn_d, tn_f, tm)
    d_h1, g_ffnw = _rms_bwd(h1, ffn_norm_w, d_u2, d_h2, "rms2_bwd")
    d_merged = _mm(d_h1, w_o, "nt", F32, "d_merged", tm, tn_d, d)
    g_wo = _mm(merged, d_h1, "tn", WIRE_DTYPE, "g_w_out", tn_d, tn_d, tm)
    d_bs, d_br, d_gs, d_gr = _gate_bwd(d_merged, bs, br, proj, col["gs"], col["gr"], "gate_bwd")
    tk_i = _pick(di, (1024, 512, 256))
    d_yssd = _mm(d_bs, w_bs, "nt", F32, "d_y_ssd", tm, tk_i, d)
    g_wbs = _mm(yssd, d_bs, "tn", WIRE_DTYPE, "g_w_branch_ssd", tk_i, tn_d, tm)
    d_yret = _mm(d_br, w_br, "nt", F32, "d_y_ret", tm, tk_i, d)
    g_wbr = _mm(yret, d_br, "tn", WIRE_DTYPE, "g_w_branch_ret", tk_i, tn_d, tm)
    dq, dk, dv, dg = _ret_bwd(d_yret, proj, col["v"], col["g"], o_ret, qr, kr, st_ret, cos, sin, d, "ret_bwd")
    d_xa, dz, ddt, g_dtb, g_alog, g_dvec, g_snw = _ssd_bwd(d_yssd, xa, proj, col["dt"], col["z"], ypre, st_ssd,
                                                             dtb, alog, dvec, ssd_norm_w, di, "ssd_bwd")
    d_pre, g_scw, g_scb = _ssd_conv_bwd_pre(proj, col["xbc"], cw, scw, scb, d_xa, "ssd_conv_bwd_pre")
    d_xbc = _conv_bwd_dx(d_pre, scw, "ssd_conv_bwd_dx")
    d_proj = jnp.concatenate([dz, dv, dg, d_xbc, dq, dk, d_gs, d_gr, ddt], axis=1)
    tn_p = _pick(wp, (1920, 1536, 1280, 1024, 896, 768, 640, 512, 384, 256, 128))
    d_u1 = _mm(d_proj, w_p, "nt", F32, "d_u1", tm, tn_d, tn_p)
    g_wp = _mm(u1, d_proj, "tn", WIRE_DTYPE, "g_w_in", tn_d, tn_p, tm)
    d_h0, g_mixw = _rms_bwd(h0, mix_norm_w, d_u1, d_h1, "rms1_bwd")
    grad_x = d_h0[CHUNK:][None]
    g_meta = d_h0[FRONT:CHUNK]

    g_in_ref = jnp.concatenate([g_wp[:, col[nm]:col[nm] + rw] for nm, rw in ref_order], axis=1)
    sc_in = jnp.moveaxis(g_in_ref.reshape(d, 4, in_dim // 4), 1, 0)
    sc_up = jnp.moveaxis(g_wu.reshape(d, 4, 2 * f // 4), 1, 0)
    scattered = _chip_exchange(
        [sc_in, g_wbs.reshape(4, di // 4, d), g_wbr.reshape(4, di // 4, d), g_wo.reshape(4, d // 4, d), sc_up,
         g_wd.reshape(4, f // 4, d)], True, "scatter_grads")
    names = ["w_in", "w_branch_ssd", "w_branch_ret", "w_out", "w_up", "w_down"]
    part = [_sum4(p, "sum4_" + nm) for p, nm in zip(scattered, names)]
    part_sib = _sibling_swap(part, "swap_partials")
    big_w = [w_in, w_branch_ssd, w_branch_ret, w_out, w_up, w_down]
    big_m = [m_w_in, m_w_branch_ssd, m_w_branch_ret, m_w_out, m_w_up, m_w_down]
    big_v = [v_w_in, v_w_branch_ssd, v_w_branch_ret, v_w_out, v_w_up, v_w_down]
    big_out = {}
    for nm, w_, p_, s_, m_, v_ in zip(names, big_w, part, part_sib, big_m, big_v):
        res = _adam_big(w_[0], p_, s_, m_[0], v_[0], "adam_" + nm)
        big_out[nm] = [r[None] for r in res]

    kws, kwf = ssd_conv_w.shape[1], ffn_conv_w.shape[1]
    small_grads = [g_meta, g_mixw, g_scw[:kws], g_scb, g_dtb[:, :nh], g_alog[:, :nh], g_dvec[:, :nh], g_snw, g_ffnw,
                   g_fcw[:kwf], g_fcb, g_fin, loss8[0:1, 0:1]]
    sg_shapes = [g.shape for g in small_grads]
    gparts = _allgather8(_pack(small_grads), "gather_small_grads")

    def own_cols(a, width):
        return lax.dynamic_slice_in_dim(a, chip * width, width, axis=1)

    def widen(a, width_full):
        z = jnp.zeros(a.shape[:-1] + (width_full,), F32)
        return lax.dynamic_update_slice_in_dim(z, a, chip * a.shape[-1], axis=a.ndim - 1)

    def small_slab(meta_, mix_, scw_, scb_, dtb_, alog_, d_, snw_, ffnw_, fcw_, fcb_, fin_):
        return _pack([widen(meta_, d), mix_, widen(scw_[0], cw), scb_, dtb_, alog_, d_, snw_, ffnw_, widen(fcw_[0], 2 * f),
                      fcb_, fin_.reshape(1, d), jnp.zeros((1, 1), F32)])

    w_slab = small_slab(meta_tokens, mix_norm_w, ssd_conv_w, ssd_conv_b, ssd_dt_bias, ssd_A_log, ssd_D, ssd_norm_w,
                        ffn_norm_w, ffn_conv_w, ffn_conv_b, final_norm_w)
    m_slab = small_slab(m_meta_tokens, m_mix_norm_w, m_ssd_conv_w, m_ssd_conv_b, m_ssd_dt_bias, m_ssd_A_log, m_ssd_D,
                        m_ssd_norm_w, m_ffn_norm_w, m_ffn_conv_w, m_ffn_conv_b, m_final_norm_w)
    v_slab = small_slab(v_meta_tokens, v_mix_norm_w, v_ssd_conv_w, v_ssd_conv_b, v_ssd_dt_bias, v_ssd_A_log, v_ssd_D,
                        v_ssd_norm_w, v_ffn_norm_w, v_ffn_conv_w, v_ffn_conv_b, v_final_norm_w)
    small_res = [_unpack(s, sg_shapes) for s in _adam_small(w_slab, gparts, m_slab, v_slab, "adam_small")]
    loss = small_res[0][12].reshape(())

    def small_outputs(vals):
        meta_, mix_, scw_, scb_, dtb_, alog_, d_, snw_, ffnw_, fcw_, fcb_, fin_ = vals[:12]
        return {
            "meta_tokens": own_cols(meta_, d // 4), "mix_norm_w": mix_, "ssd_conv_w": own_cols(scw_, cw // 4)[None],
            "ssd_conv_b": scb_, "ssd_dt_bias": dtb_, "ssd_A_log": alog_, "ssd_D": d_, "ssd_norm_w": snw_,
            "ffn_norm_w": ffnw_, "ffn_conv_w": own_cols(fcw_, 2 * f // 4)[None], "ffn_conv_b": fcb_,
            "final_norm_w": fin_.reshape(d),
        }

    weights = ["meta_tokens", "mix_norm_w", "w_in", "ssd_conv_w", "ssd_conv_b", "ssd_dt_bias", "ssd_A_log", "ssd_D",
               "ssd_norm_w", "w_branch_ssd", "w_branch_ret", "w_out", "ffn_norm_w", "w_up", "ffn_conv_w", "ffn_conv_b",
               "w_down", "final_norm_w"]
    outs = [loss, grad_x]
    for kind in range(4):
        so = small_outputs(small_res[kind])
        for nm in weights:
            outs.append(big_out[nm][kind] if nm in big_out else so[nm])
    return tuple(outs)
```

```python
import math

import jax
import jax.numpy as jnp
import numpy as np
from jax import lax
from jax.experimental import pallas as pl
from jax.experimental.pallas import tpu as pltpu

F32 = jnp.float32
BF16 = jnp.bfloat16
MXU_DTYPE = BF16
WIRE_DTYPE = BF16

N_META = 16
CHUNK = 128
FRONT = CHUNK - N_META
EPS = 1e-6
SSD_HEAD_DIM = 64
SSD_GROUPS = 4
SSD_STATE = 128
SSD_CONV = 4
RET_HEADS = 4
ROPE_BASE = 10000.0
FFN_CONV = 3
LANES = 128
SUBLANES = 8
VMEM_LIMIT = 56 * 1024 * 1024

ADAM_LR = 0.001
ADAM_B1 = 0.9
ADAM_B2 = 0.999
ADAM_EPS = 1e-08
ADAM_WD = 0.01
ADAM_STEP = 10
MESH = pl.DeviceIdType.MESH


def _params(sem=None, vmem=VMEM_LIMIT):
    return pltpu.CompilerParams(dimension_semantics=sem, vmem_limit_bytes=vmem)


def _pick(n, cands):
    for c in cands:
        if n % c == 0:
            return c
    return n


def _silu(x):
    return x * jax.nn.sigmoid(x)


def _dsilu(x):
    s = jax.nn.sigmoid(x)
    return s * (1.0 + x * (1.0 - s))


def _dot(a, b, dims=(((1,), (0,)), ((), ()))):
    return lax.dot_general(a.astype(MXU_DTYPE), b.astype(MXU_DTYPE), dims, preferred_element_type=F32)


def _dot_nt(a, b):
    return _dot(a, b, (((1,), (1,)), ((), ())))


def _dot_tn(a, b):
    return _dot(a, b, (((0,), (0,)), ((), ())))


def _dot01(a, b, split, npass, dims=(((1,), (0,)), ((), ()))):
    rest = (a if split == "a" else b).astype(F32)
    fixed = (b if split == "a" else a).astype(BF16)
    acc = None
    for p in range(npass):
        piece = rest.astype(BF16)
        ops = (piece, fixed) if split == "a" else (fixed, piece)
        term = lax.dot_general(ops[0], ops[1], dims, preferred_element_type=F32)
        acc = term if acc is None else acc + term
        if p + 1 < npass:
            rest = rest - piece.astype(F32)
    return acc


_NT = (((1,), (1,)), ((), ()))


def _iota(shape, dim):
    return lax.broadcasted_iota(jnp.int32, shape, dim)


def _shift_down(cur, prev8, k):
    if k == 0:
        return cur
    rolled = pltpu.roll(cur, k, 0)
    i8 = _iota((SUBLANES, cur.shape[1]), 0)
    head = jnp.where(i8 < k, pltpu.roll(prev8, k, 0), rolled[0:SUBLANES])
    return jnp.concatenate([head, rolled[SUBLANES:]], axis=0)


def _shift_up(cur, next8, k):
    if k == 0:
        return cur
    n = cur.shape[0]
    rolled = pltpu.roll(cur, n - k, 0)
    i8 = _iota((SUBLANES, cur.shape[1]), 0)
    tail = jnp.where(i8 >= SUBLANES - k, pltpu.roll(next8, SUBLANES - k, 0), rolled[n - SUBLANES:])
    return jnp.concatenate([rolled[:n - SUBLANES], tail], axis=0)


def _mm(a, b, mode, out_dtype, name, tm, tn, tk, res=None):
    if mode == "nn":
        (m, kd), n = a.shape, b.shape[1]
        a_spec = pl.BlockSpec((tm, tk), lambda i, j, k: (i, k))
        b_spec = pl.BlockSpec((tk, tn), lambda i, j, k: (k, j))
        dims = (((1,), (0,)), ((), ()))
    elif mode == "nt":
        (m, kd), n = a.shape, b.shape[0]
        a_spec = pl.BlockSpec((tm, tk), lambda i, j, k: (i, k))
        b_spec = pl.BlockSpec((tn, tk), lambda i, j, k: (j, k))
        dims = (((1,), (1,)), ((), ()))
    else:
        (kd, m), n = a.shape, b.shape[1]
        a_spec = pl.BlockSpec((tk, tm), lambda i, j, k: (k, i))
        b_spec = pl.BlockSpec((tk, tn), lambda i, j, k: (k, j))
        dims = (((0,), (0,)), ((), ()))
    assert m % tm == 0 and n % tn == 0 and kd % tk == 0, (name, m, n, kd, tm, tn, tk)
    nk = kd // tk
    has_res = res is not None

    def body(*refs):
        if has_res:
            a_ref, b_ref, r_ref, o_ref, acc = refs
        else:
            a_ref, b_ref, o_ref, acc = refs
        k = pl.program_id(2)

        @pl.when(k == 0)
        def _():
            acc[...] = jnp.zeros_like(acc)

        acc[...] += _dot(a_ref[...], b_ref[...], dims)

        @pl.when(k == nk - 1)
        def _():
            r = acc[...]
            if has_res:
                r = r + r_ref[...].astype(F32)
            o_ref[...] = r.astype(out_dtype)

    in_specs = [a_spec, b_spec]
    args = [a, b]
    if has_res:
        in_specs.append(pl.BlockSpec((tm, tn), lambda i, j, k: (i, j)))
        args.append(res)
    return pl.pallas_call(
        body, name=name, grid=(m // tm, n // tn, nk), in_specs=in_specs,
        out_specs=pl.BlockSpec((tm, tn), lambda i, j, k: (i, j)),
        out_shape=jax.ShapeDtypeStruct((m, n), out_dtype),
        scratch_shapes=[pltpu.VMEM((tm, tn), F32)],
        compiler_params=_params(("parallel", "parallel", "arbitrary")),
    )(*args)


def _rms_fwd(h, w, name):
    t, d = h.shape
    tr = _pick(t, (640, 512, 384, 256, 128))

    def body(h_ref, w_ref, u_ref):
        x = h_ref[...]
        r = lax.rsqrt(jnp.mean(x * x, axis=1, keepdims=True) + EPS)
        u_ref[...] = (x * r * w_ref[...]).astype(MXU_DTYPE)

    return pl.pallas_call(
        body, name=name, grid=(t // tr,),
        in_specs=[pl.BlockSpec((tr, d), lambda i: (i, 0)), pl.BlockSpec((1, d), lambda i: (0, 0))],
        out_specs=pl.BlockSpec((tr, d), lambda i: (i, 0)),
        out_shape=jax.ShapeDtypeStruct((t, d), MXU_DTYPE),
        compiler_params=_params(("parallel",)),
    )(h, w)


def _rms_bwd(h, w, du, res, name):
    t, d = h.shape
    tr = _pick(t, (640, 512, 384, 256, 128))

    def body(h_ref, w_ref, du_ref, res_ref, dh_ref, gw_ref):
        @pl.when(pl.program_id(0) == 0)
        def _():
            gw_ref[...] = jnp.zeros_like(gw_ref)

        x = h_ref[...]
        r = lax.rsqrt(jnp.mean(x * x, axis=1, keepdims=True) + EPS)
        xhat = x * r
        dy = du_ref[...].astype(F32)
        dxh = dy * w_ref[...]
        dh = r * (dxh - xhat * jnp.mean(dxh * xhat, axis=1, keepdims=True))
        dh_ref[...] = dh + res_ref[...]
        gw_ref[...] += jnp.sum(dy * xhat, axis=0, keepdims=True)

    row = pl.BlockSpec((tr, d), lambda i: (i, 0))
    vec = pl.BlockSpec((1, d), lambda i: (0, 0))
    return pl.pallas_call(
        body, name=name, grid=(t // tr,), in_specs=[row, vec, row, row], out_specs=[row, vec],
        out_shape=[jax.ShapeDtypeStruct((t, d), F32), jax.ShapeDtypeStruct((1, d), F32)],
        compiler_params=_params(("arbitrary",)),
    )(h, w, du, res)


def _loss_bwd(h2, w, target, name):
    t, d = h2.shape
    nc = t // CHUNK

    def body(h_ref, w_ref, tg_ref, loss_ref, dh_ref, gw_ref):
        i = pl.program_id(0)

        @pl.when(i == 0)
        def _():
            gw_ref[...] = jnp.zeros_like(gw_ref)
            loss_ref[...] = jnp.zeros_like(loss_ref)
            dh_ref[...] = jnp.zeros_like(dh_ref)

        @pl.when(i > 0)
        def _():
            x = h_ref[...]
            r = lax.rsqrt(jnp.mean(x * x, axis=1, keepdims=True) + EPS)
            xhat = x * r
            diff = xhat * w_ref[...] - tg_ref[...]
            loss_ref[...] += 0.5 * jnp.sum(jnp.sum(diff * diff, axis=1, keepdims=True), axis=0, keepdims=True) / d
            dy = diff / d
            dxh = dy * w_ref[...]
            dh_ref[...] = r * (dxh - xhat * jnp.mean(dxh * xhat, axis=1, keepdims=True))
            gw_ref[...] += jnp.sum(dy * xhat, axis=0, keepdims=True)

    row = pl.BlockSpec((CHUNK, d), lambda i: (i, 0))
    vec = pl.BlockSpec((1, d), lambda i: (0, 0))
    return pl.pallas_call(
        body, name=name, grid=(nc,),
        in_specs=[row, vec, pl.BlockSpec((CHUNK, d), lambda i: (jnp.maximum(i - 1, 0), 0))],
        out_specs=[pl.BlockSpec((SUBLANES, LANES), lambda i: (0, 0)), row, vec],
        out_shape=[jax.ShapeDtypeStruct((SUBLANES, LANES), F32), jax.ShapeDtypeStruct((t, d), F32),
                   jax.ShapeDtypeStruct((1, d), F32)],
        compiler_params=_params(("arbitrary",)),
    )(h2, w, target)


def _conv_taps(cur, prev8, w_ref, b_ref, kw):
    taps = [_shift_down(cur, prev8, kw - 1 - k) for k in range(kw)]
    y = b_ref[...] + taps[kw - 1] * w_ref[kw - 1:kw, :]
    for k in range(kw - 1):
        y = y + taps[k] * w_ref[k:k + 1, :]
    return y, taps


def _conv_pre(cur, prev8, w_ref, b_ref, kw):
    return _conv_taps(cur, prev8, w_ref, b_ref, kw)[0]


def _conv_dx(dpre, next8, w_ref, kw):
    acc = dpre * w_ref[kw - 1:kw, :]
    for k in range(kw - 1):
        acc = acc + _shift_up(dpre, next8, kw - 1 - k) * w_ref[k:k + 1, :]
    return acc


def _row_tile(t):
    return _pick(t, (640, 512, 384, 256, 128))


def _ssd_conv_fwd(proj, col0, width, w, b, name):
    t = proj.shape[0]
    kw = w.shape[0]
    tr, tc = _row_tile(t), _pick(width, (512, 256, 128))
    c0, rb = col0 // tc, tr // SUBLANES
    assert col0 % tc == 0

    def body(x_ref, p_ref, w_ref, b_ref, o_ref):
        i = pl.program_id(1)
        prev8 = jnp.where(i > 0, p_ref[...], 0.0)
        pre = _conv_pre(x_ref[...], prev8, w_ref, b_ref, kw)
        rows = _iota((tr, 1), 0) + i * tr
        o_ref[...] = jnp.where(rows >= FRONT, _silu(pre), 0.0)

    return pl.pallas_call(
        body, name=name, grid=(width // tc, t // tr),
        in_specs=[pl.BlockSpec((tr, tc), lambda j, i: (i, c0 + j)),
                  pl.BlockSpec((SUBLANES, tc), lambda j, i: (jnp.maximum(i * rb - 1, 0), c0 + j)),
                  pl.BlockSpec((kw, tc), lambda j, i: (0, j)),
                  pl.BlockSpec((1, tc), lambda j, i: (0, j))],
        out_specs=pl.BlockSpec((tr, tc), lambda j, i: (i, j)),
        out_shape=jax.ShapeDtypeStruct((t, width), F32),
        compiler_params=_params(("parallel", "parallel")),
    )(proj, proj, w, b)


def _ssd_conv_bwd(proj, col0, width, w, b, dact, name):
    t = proj.shape[0]
    kw = w.shape[0]
    tr, tc = _row_tile(t), _pick(width, (512, 256, 128))
    c0, rb, nrow = col0 // tc, tr // SUBLANES, t // tr

    def body(x_ref, p_ref, w_ref, b_ref, d_ref, o_ref, gw_ref, gb_ref, carry):
        i = pl.program_id(1)
        ti = nrow - 1 - i

        @pl.when(i == 0)
        def _():
            gw_ref[...] = jnp.zeros_like(gw_ref)
            gb_ref[...] = jnp.zeros_like(gb_ref)
            carry[...] = jnp.zeros_like(carry)

        prev8 = jnp.where(ti > 0, p_ref[...], 0.0)
        pre, taps = _conv_taps(x_ref[...], prev8, w_ref, b_ref, kw)
        valid = _iota((tr, 1), 0) + ti * tr >= FRONT
        dpre = jnp.where(valid, d_ref[...] * _dsilu(pre), 0.0)
        gb_ref[...] += jnp.sum(dpre, axis=0, keepdims=True)
        for k in range(kw):
            gw_ref[k:k + 1, :] += jnp.sum(dpre * taps[k], axis=0, keepdims=True)
        o_ref[...] = jnp.where(valid, _conv_dx(dpre, carry[...], w_ref, kw), 0.0).astype(MXU_DTYPE)
        carry[...] = dpre[0:SUBLANES]

    return pl.pallas_call(
        body, name=name, grid=(width // tc, nrow),
        in_specs=[pl.BlockSpec((tr, tc), lambda j, i: (nrow - 1 - i, c0 + j)),
                  pl.BlockSpec((SUBLANES, tc), lambda j, i: (jnp.maximum((nrow - 1 - i) * rb - 1, 0), c0 + j)),
                  pl.BlockSpec((kw, tc), lambda j, i: (0, j)),
                  pl.BlockSpec((1, tc), lambda j, i: (0, j)),
                  pl.BlockSpec((tr, tc), lambda j, i: (nrow - 1 - i, j))],
        out_specs=[pl.BlockSpec((tr, tc), lambda j, i: (nrow - 1 - i, j)),
                   pl.BlockSpec((SUBLANES, tc), lambda j, i: (0, j)),
                   pl.BlockSpec((1, tc), lambda j, i: (0, j))],
        out_shape=[jax.ShapeDtypeStruct((t, width), MXU_DTYPE), jax.ShapeDtypeStruct((SUBLANES, width), F32),
                   jax.ShapeDtypeStruct((1, width), F32)],
        scratch_shapes=[pltpu.VMEM((SUBLANES, tc), F32)],
        compiler_params=_params(("parallel", "arbitrary")),
    )(proj, proj, w, b, dact)


def _ffn_conv_fwd(up, w, b, name):
    t, f2 = up.shape
    f = f2 // 2
    kw = w.shape[0]
    tr, tc = _row_tile(t), _pick(f, (256, 128))
    nf, rb = f // tc, tr // SUBLANES

    def body(xg, pg, xv, pv, wg, wv, bg, bv, o_ref):
        i = pl.program_id(1)
        ag = _conv_pre(xg[...], jnp.where(i > 0, pg[...], 0.0), wg, bg, kw)
        av = _conv_pre(xv[...], jnp.where(i > 0, pv[...], 0.0), wv, bv, kw)
        o_ref[...] = (_silu(ag) * av).astype(MXU_DTYPE)

    def cur(off):
        return pl.BlockSpec((tr, tc), lambda j, i: (i, j + off))

    def prev(off):
        return pl.BlockSpec((SUBLANES, tc), lambda j, i: (jnp.maximum(i * rb - 1, 0), j + off))

    def par(rows, off):
        return pl.BlockSpec((rows, tc), lambda j, i: (0, j + off))

    return pl.pallas_call(
        body, name=name, grid=(nf, t // tr),
        in_specs=[cur(0), prev(0), cur(nf), prev(nf), par(kw, 0), par(kw, nf), par(1, 0), par(1, nf)],
        out_specs=pl.BlockSpec((tr, tc), lambda j, i: (i, j)),
        out_shape=jax.ShapeDtypeStruct((t, f), MXU_DTYPE),
        compiler_params=_params(("parallel", "parallel")),
    )(up, up, up, up, w, w, b, b)


def _ffn_conv_bwd(up, w, b, dact, name):
    t, f2 = up.shape
    f = f2 // 2
    kw = w.shape[0]
    tr, tc = _row_tile(t), _pick(f, (256, 128))
    nf, rb, nrow = f // tc, tr // SUBLANES, t // tr

    def body(xg, pg, xv, pv, wg, wv, bg, bv, d_ref, og_ref, ov_ref, gwg_ref, gwv_ref, gbg_ref, gbv_ref, cg, cv):
        i = pl.program_id(1)
        ti = nrow - 1 - i

        @pl.when(i == 0)
        def _():
            for r in (gwg_ref, gwv_ref, gbg_ref, gbv_ref, cg, cv):
                r[...] = jnp.zeros_like(r)

        ag, tg = _conv_taps(xg[...], jnp.where(ti > 0, pg[...], 0.0), wg, bg, kw)
        av, tv = _conv_taps(xv[...], jnp.where(ti > 0, pv[...], 0.0), wv, bv, kw)
        d = d_ref[...]
        s = jax.nn.sigmoid(ag)
        dag = d * av * (s * (1.0 + ag * (1.0 - s)))
        dav = d * (ag * s)
        gbg_ref[...] += jnp.sum(dag, axis=0, keepdims=True)
        gbv_ref[...] += jnp.sum(dav, axis=0, keepdims=True)
        for k in range(kw):
            gwg_ref[k:k + 1, :] += jnp.sum(dag * tg[k], axis=0, keepdims=True)
            gwv_ref[k:k + 1, :] += jnp.sum(dav * tv[k], axis=0, keepdims=True)
        valid = _iota((tr, 1), 0) + ti * tr >= FRONT
        og_ref[...] = jnp.where(valid, _conv_dx(dag, cg[...], wg, kw), 0.0).astype(MXU_DTYPE)
        ov_ref[...] = jnp.where(valid, _conv_dx(dav, cv[...], wv, kw), 0.0).astype(MXU_DTYPE)
        cg[...] = dag[0:SUBLANES]
        cv[...] = dav[0:SUBLANES]

    def cur(off):
        return pl.BlockSpec((tr, tc), lambda j, i: (nrow - 1 - i, j + off))

    def prev(off):
        return pl.BlockSpec((SUBLANES, tc), lambda j, i: (jnp.maximum((nrow - 1 - i) * rb - 1, 0), j + off))

    def par(rows, off):
        return pl.BlockSpec((rows, tc), lambda j, i: (0, j + off))

    acc8 = pl.BlockSpec((SUBLANES, tc), lambda j, i: (0, j))
    acc1 = pl.BlockSpec((1, tc), lambda j, i: (0, j))
    return pl.pallas_call(
        body, name=name, grid=(nf, nrow),
        in_specs=[cur(0), prev(0), cur(nf), prev(nf), par(kw, 0), par(kw, nf), par(1, 0), par(1, nf), cur(0)],
        out_specs=[cur(0), cur(0), acc8, acc8, acc1, acc1],
        out_shape=[jax.ShapeDtypeStruct((t, f), MXU_DTYPE), jax.ShapeDtypeStruct((t, f), MXU_DTYPE),
                   jax.ShapeDtypeStruct((SUBLANES, f), F32), jax.ShapeDtypeStruct((SUBLANES, f), F32),
                   jax.ShapeDtypeStruct((1, f), F32), jax.ShapeDtypeStruct((1, f), F32)],
        scratch_shapes=[pltpu.VMEM((SUBLANES, tc), F32), pltpu.VMEM((SUBLANES, tc), F32)],
        compiler_params=_params(("parallel", "arbitrary")),
    )(up, up, up, up, w, w, b, b, dact)


def _gate_fwd(bs, br, proj, c_gs, c_gr, name):
    t, d = bs.shape
    tr = _row_tile(t)

    def body(bs_ref, br_ref, gs_ref, gr_ref, o_ref):
        o_ref[...] = (jax.nn.sigmoid(gs_ref[...]) * bs_ref[...] + jax.nn.sigmoid(gr_ref[...]) * br_ref[...]).astype(MXU_DTYPE)

    row = pl.BlockSpec((tr, d), lambda i: (i, 0))
    return pl.pallas_call(
        body, name=name, grid=(t // tr,),
        in_specs=[row, row, pl.BlockSpec((tr, d), lambda i: (i, c_gs // d)), pl.BlockSpec((tr, d), lambda i: (i, c_gr // d))],
        out_specs=row, out_shape=jax.ShapeDtypeStruct((t, d), MXU_DTYPE),
        compiler_params=_params(("parallel",)),
    )(bs, br, proj, proj)


def _gate_bwd(dm, bs, br, proj, c_gs, c_gr, name):
    t, d = bs.shape
    tr = _row_tile(t)

    def body(dm_ref, bs_ref, br_ref, gs_ref, gr_ref, dbs_ref, dbr_ref, dgs_ref, dgr_ref):
        g = dm_ref[...]
        ss, sr = jax.nn.sigmoid(gs_ref[...]), jax.nn.sigmoid(gr_ref[...])
        dbs_ref[...] = (g * ss).astype(MXU_DTYPE)
        dbr_ref[...] = (g * sr).astype(MXU_DTYPE)
        dgs_ref[...] = (g * bs_ref[...] * ss * (1.0 - ss)).astype(MXU_DTYPE)
        dgr_ref[...] = (g * br_ref[...] * sr * (1.0 - sr)).astype(MXU_DTYPE)

    row = pl.BlockSpec((tr, d), lambda i: (i, 0))
    out = jax.ShapeDtypeStruct((t, d), MXU_DTYPE)
    return pl.pallas_call(
        body, name=name, grid=(t // tr,),
        in_specs=[row, row, row, pl.BlockSpec((tr, d), lambda i: (i, c_gs // d)), pl.BlockSpec((tr, d), lambda i: (i, c_gr // d))],
        out_specs=[row] * 4, out_shape=[out] * 4,
        compiler_params=_params(("parallel",)),
    )(dm, bs, br, proj, proj)


def _ret_consts(h):
    lg = math.log(1.0 - 2.0 ** (-5.0 - h))
    l = _iota((CHUNK, 1), 0).astype(F32)
    diff = l - _iota((1, CHUNK), 1).astype(F32)
    dm = jnp.exp(jnp.where(diff >= 0, diff * lg, -jnp.inf))
    dmt = jnp.exp(jnp.where(diff <= 0, -diff * lg, -jnp.inf))
    cs = jnp.exp((l + 1.0) * lg)
    kdec = jnp.exp((CHUNK - 1.0 - l) * lg)
    return dm, dmt, cs, kdec, math.exp(CHUNK * lg)


def _ret_fwd(proj, c_q, c_k, c_v, c_g, cos, sin, d, name):
    t = proj.shape[0]
    nc = t // CHUNK
    hq, hv = d // RET_HEADS, 2 * d // RET_HEADS
    half = hq // 2
    scale = hq ** -0.5

    def body(q_ref, k_ref, v_ref, g_ref, cos_ref, sin_ref, o_ref, y_ref, qr_ref, kr_ref, st_ref, rs):
        @pl.when(pl.program_id(0) == 0)
        def _():
            rs[...] = jnp.zeros_like(rs)

        co, si = cos_ref[...], sin_ref[...]
        for h in range(RET_HEADS):
            dm, _, cs, kdec, gam = _ret_consts(h)
            q1, q2 = q_ref[:, h * hq:h * hq + half], q_ref[:, h * hq + half:(h + 1) * hq]
            k1, k2 = k_ref[:, h * hq:h * hq + half], k_ref[:, h * hq + half:(h + 1) * hq]
            qr = jnp.concatenate([q1 * co - q2 * si, q2 * co + q1 * si], axis=1)
            kr = jnp.concatenate([k1 * co - k2 * si, k2 * co + k1 * si], axis=1) * scale
            qr_ref[:, h * hq:(h + 1) * hq] = qr.astype(MXU_DTYPE)
            kr_ref[:, h * hq:(h + 1) * hq] = kr.astype(MXU_DTYPE)
            v = v_ref[:, h * hv:(h + 1) * hv]
            r_in = rs[h * hq:(h + 1) * hq, :]
            st_ref[0, h * hq:(h + 1) * hq, :] = r_in.astype(MXU_DTYPE)
            s = _dot_nt(qr, kr) * dm
            o = _dot(s, v) + cs * _dot(qr, r_in)
            rs[h * hq:(h + 1) * hq, :] = gam * r_in + _dot_tn(kr * kdec, v)
            o_ref[:, h * hv:(h + 1) * hv] = o
            on = o * lax.rsqrt(jnp.mean(o * o, axis=1, keepdims=True) + EPS)
            y_ref[:, h * hv:(h + 1) * hv] = (_silu(g_ref[:, h * hv:(h + 1) * hv]) * on).astype(MXU_DTYPE)

    def col(width, c0):
        return pl.BlockSpec((CHUNK, width), lambda i: (i, c0 // width))

    tab = pl.BlockSpec((CHUNK, half), lambda i: (i, 0))
    return pl.pallas_call(
        body, name=name, grid=(nc,),
        in_specs=[col(d, c_q), col(d, c_k), col(2 * d, c_v), col(2 * d, c_g), tab, tab],
        out_specs=[col(2 * d, 0), col(2 * d, 0), col(d, 0), col(d, 0),
                   pl.BlockSpec((1, d, hv), lambda i: (i, 0, 0))],
        out_shape=[jax.ShapeDtypeStruct((t, 2 * d), F32), jax.ShapeDtypeStruct((t, 2 * d), MXU_DTYPE),
                   jax.ShapeDtypeStruct((t, d), MXU_DTYPE), jax.ShapeDtypeStruct((t, d), MXU_DTYPE),
                   jax.ShapeDtypeStruct((nc, d, hv), MXU_DTYPE)],
        scratch_shapes=[pltpu.VMEM((d, hv), F32)],
        compiler_params=_params(("arbitrary",)),
    )(proj, proj, proj, proj, cos, sin)


def _ret_bwd(dy, proj, c_v, c_g, o, qr, kr, st, cos, sin, d, name):
    t = proj.shape[0]
    nc = t // CHUNK
    hq, hv = d // RET_HEADS, 2 * d // RET_HEADS
    half = hq // 2
    scale = hq ** -0.5

    def body(dy_ref, v_ref, g_ref, o_ref, qr_ref, kr_ref, st_ref, cos_ref, sin_ref,
             dq_ref, dk_ref, dv_ref, dg_ref, drs):
        @pl.when(pl.program_id(0) == 0)
        def _():
            drs[...] = jnp.zeros_like(drs)

        co, si = cos_ref[...], sin_ref[...]
        for h in range(RET_HEADS):
            dm, dmt, cs, kdec, gam = _ret_consts(h)
            vs = slice(h * hv, (h + 1) * hv)
            qs = slice(h * hq, (h + 1) * hq)
            o_h = o_ref[:, vs]
            g_h = g_ref[:, vs]
            d_y = dy_ref[:, vs]
            r = lax.rsqrt(jnp.mean(o_h * o_h, axis=1, keepdims=True) + EPS)
            on = o_h * r
            d_on = d_y * _silu(g_h)
            dg_ref[:, vs] = (d_y * on * _dsilu(g_h)).astype(MXU_DTYPE)
            d_o = r * (d_on - on * jnp.mean(d_on * on, axis=1, keepdims=True))
            q_h, k_h, v_h = qr_ref[:, qs], kr_ref[:, qs], v_ref[:, vs]
            r_in = st_ref[0, qs, :]
            dr_n = drs[qs, :]
            csdo = cs * d_o
            ds = _dot_nt(d_o, v_h) * dm
            dst = _dot_nt(v_h, d_o) * dmt
            s_t = _dot_nt(k_h, q_h) * dmt
            dqr = _dot(ds, k_h) + _dot_nt(csdo, r_in)
            dkr = _dot(dst, q_h) + kdec * _dot_nt(v_h, dr_n)
            dv_ref[:, vs] = (_dot(s_t, d_o) + _dot(k_h.astype(F32) * kdec, dr_n)).astype(MXU_DTYPE)
            drs[qs, :] = gam * dr_n + _dot_tn(q_h, csdo)
            a1, a2 = dqr[:, :half], dqr[:, half:]
            dq_ref[:, qs] = jnp.concatenate([a1 * co + a2 * si, a2 * co - a1 * si], axis=1).astype(MXU_DTYPE)
            b1, b2 = dkr[:, :half] * scale, dkr[:, half:] * scale
            dk_ref[:, qs] = jnp.concatenate([b1 * co + b2 * si, b2 * co - b1 * si], axis=1).astype(MXU_DTYPE)

    def col(width, c0=0):
        return pl.BlockSpec((CHUNK, width), lambda i: (nc - 1 - i, c0 // width))

    tab = pl.BlockSpec((CHUNK, half), lambda i: (nc - 1 - i, 0))
    return pl.pallas_call(
        body, name=name, grid=(nc,),
        in_specs=[col(2 * d), col(2 * d, c_v), col(2 * d, c_g), col(2 * d), col(d), col(d),
                  pl.BlockSpec((1, d, hv), lambda i: (nc - 1 - i, 0, 0)), tab, tab],
        out_specs=[col(d), col(d), col(2 * d), col(2 * d)],
        out_shape=[jax.ShapeDtypeStruct((t, d), MXU_DTYPE), jax.ShapeDtypeStruct((t, d), MXU_DTYPE),
                   jax.ShapeDtypeStruct((t, 2 * d), MXU_DTYPE), jax.ShapeDtypeStruct((t, 2 * d), MXU_DTYPE)],
        scratch_shapes=[pltpu.VMEM((d, hv), F32)],
        compiler_params=_params(("arbitrary",)),
    )(dy, proj, proj, o, qr, kr, st, cos, sin)


def _ssd_small(dtraw_ref, dtb_ref, alog_ref, chunk_idx, nh):
    rows = _iota((CHUNK, 1), 0)
    ok = ((rows >= FRONT) | (chunk_idx > 0)) & (_iota((1, LANES), 1) < nh)
    z = dtraw_ref[...] + dtb_ref[...]
    dt = jnp.where(ok, jax.nn.softplus(z), 0.0)
    sig = jnp.where(ok, jax.nn.sigmoid(z), 0.0)
    a = jnp.where(_iota((1, LANES), 1) < nh, -jnp.exp(alog_ref[...]), 0.0)
    tri = (_iota((CHUNK, CHUNK), 0) >= _iota((CHUNK, CHUNK), 1)).astype(F32)
    acs = _dot01(tri, dt * a, "b", 3)
    return dt, sig, a, acs, acs.T


def _head_expand(g, hpg, gw):
    shift = int(math.log2(SSD_HEAD_DIM))
    return (_iota((LANES, gw), 0) == g * hpg + lax.shift_right_logical(_iota((LANES, gw), 1), shift)).astype(F32)


def _ssd_fwd(xa, proj, c_dt, c_z, dtb, alog, dvec, nw, di, name):
    t = xa.shape[0]
    nc = t // CHUNK
    nh = di // SSD_HEAD_DIM
    hpg = nh // SSD_GROUPS
    gw = di // SSD_GROUPS
    n = SSD_STATE
    gn = SSD_GROUPS * n
    hd = SSD_HEAD_DIM

    def body(x_ref, b_ref, c_ref, dtraw_ref, z_ref, dtb_ref, alog_ref, d_ref, nw_ref,
             y_ref, ys_ref, st_ref, hts, xdt_s):
        c = pl.program_id(0)

        @pl.when(c == 0)
        def _():
            hts[...] = jnp.zeros_like(hts)

        dt, _, _, acs, acs_t = _ssd_small(dtraw_ref, dtb_ref, alog_ref, c, nh)
        tri = _iota((CHUNK, CHUNK), 0) >= _iota((CHUNK, CHUNK), 1)
        dvec8 = jnp.broadcast_to(d_ref[...], (SUBLANES, LANES))
        for g in range(SSD_GROUPS):
            gs = slice(g * gw, (g + 1) * gw)
            ns = slice(g * n, (g + 1) * n)
            e_mat = _head_expand(g, hpg, gw)
            ax = _dot01(acs, e_mat, "a", 3)
            dtx = _dot01(dt, e_mat, "a", 3)
            dx = _dot01(dvec8, e_mat, "a", 3)[0:1, :]
            xg, bg, cg = x_ref[:, gs], b_ref[:, ns], c_ref[:, ns]
            xdt = xg * dtx
            xdt_s[...] = xdt.astype(MXU_DTYPE)
            cb = _dot_nt(cg, bg)
            ht = hts[ns, :]
            st_ref[0, ns, :] = ht.astype(MXU_DTYPE)
            y_ref[:, gs] = jnp.exp(ax) * _dot(cg, ht) + dx * xg
            for hh in range(hpg):
                h = g * hpg + hh
                lmat = jnp.exp(jnp.where(tri, acs[:, h:h + 1] - acs_t[h:h + 1, :], -jnp.inf))
                hs = slice(g * gw + hh * hd, g * gw + (hh + 1) * hd)
                y_ref[:, hs] += _dot(cb * lmat, xdt_s[:, hh * hd:(hh + 1) * hd])
            aend = ax[CHUNK - 1:CHUNK, :]
            hts[ns, :] = jnp.exp(aend) * ht + _dot_tn(bg, xdt * jnp.exp(aend - ax))
        for g in range(SSD_GROUPS):
            gs = slice(g * gw, (g + 1) * gw)
            yz = y_ref[:, gs] * _silu(z_ref[:, gs])
            r = lax.rsqrt(jnp.mean(yz * yz, axis=1, keepdims=True) + EPS)
            ys_ref[:, gs] = (yz * r * nw_ref[:, gs]).astype(MXU_DTYPE)

    def col(width, c0, arr_is_xa=False):
        return pl.BlockSpec((CHUNK, width), lambda i: (i, c0 // width))

    vec = pl.BlockSpec((1, LANES), lambda i: (0, 0))
    assert di % gn == 0 and c_dt % LANES == 0 and c_z % di == 0
    return pl.pallas_call(
        body, name=name, grid=(nc,),
        in_specs=[col(di, 0), col(gn, di), col(gn, di + gn), col(LANES, c_dt), col(di, c_z), vec, vec, vec,
                  pl.BlockSpec((1, di), lambda i: (0, 0))],
        out_specs=[col(di, 0), col(di, 0), pl.BlockSpec((1, gn, gw), lambda i: (i, 0, 0))],
        out_shape=[jax.ShapeDtypeStruct((t, di), F32), jax.ShapeDtypeStruct((t, di), MXU_DTYPE),
                   jax.ShapeDtypeStruct((nc, gn, gw), MXU_DTYPE)],
        scratch_shapes=[pltpu.VMEM((gn, gw), F32), pltpu.VMEM((CHUNK, gw), MXU_DTYPE)],
        compiler_params=_params(("arbitrary",)),
    )(xa, xa, xa, proj, proj, dtb, alog, dvec, nw)


def _ssd_bwd(dys, xa, proj, c_dt, c_z, ypre, st, dtb, alog, dvec, nw, di, name):
    t = xa.shape[0]
    nc = t // CHUNK
    nh = di // SSD_HEAD_DIM
    hpg = nh // SSD_GROUPS
    gw = di // SSD_GROUPS
    n = SSD_STATE
    gn = SSD_GROUPS * n
    hd = SSD_HEAD_DIM

    def body(dys_ref, x_ref, b_ref, c_ref, dtraw_ref, z_ref, y_ref, st_ref, dtb_ref, alog_ref, d_ref, nw_ref,
             dxa_ref, dz_ref, ddt_ref, gb_ref, ga_ref, gd_ref, gnw_ref, dhts, dy_s, xdt_s, dxdt_s):
        i = pl.program_id(0)
        c = nc - 1 - i

        @pl.when(i == 0)
        def _():
            dhts[...] = jnp.zeros_like(dhts)
            gb_ref[...] = jnp.zeros_like(gb_ref)
            ga_ref[...] = jnp.zeros_like(ga_ref)
            gd_ref[...] = jnp.zeros_like(gd_ref)
            gnw_ref[...] = jnp.zeros_like(gnw_ref)

        dt, sig, a, acs, acs_t = _ssd_small(dtraw_ref, dtb_ref, alog_ref, c, nh)
        tri = _iota((CHUNK, CHUNK), 0) >= _iota((CHUNK, CHUNK), 1)
        triu = _iota((CHUNK, CHUNK), 0) <= _iota((CHUNK, CHUNK), 1)
        lane = _iota((1, LANES), 1)
        rows = _iota((CHUNK, 1), 0)
        head_row = _iota((LANES, 1), 0)
        dvec8 = jnp.broadcast_to(d_ref[...], (SUBLANES, LANES))
        da = jnp.zeros((CHUNK, LANES), F32)
        da_t = jnp.zeros((LANES, CHUNK), F32)
        ddt = jnp.zeros((CHUNK, LANES), F32)
        gd = jnp.zeros((1, LANES), F32)
        for g in range(SSD_GROUPS):
            gs = slice(g * gw, (g + 1) * gw)
            ns = slice(g * n, (g + 1) * n)
            y_g, z_g = y_ref[:, gs], z_ref[:, gs]
            sz = _silu(z_g)
            yz = y_g * sz
            r = lax.rsqrt(jnp.mean(yz * yz, axis=1, keepdims=True) + EPS)
            nrm = yz * r
            dyo = dys_ref[:, gs]
            gnw_ref[:, gs] += jnp.sum(dyo * nrm, axis=0, keepdims=True)
            dn = dyo * nw_ref[:, gs]
            dyz = r * (dn - nrm * jnp.mean(dn * nrm, axis=1, keepdims=True))
            dz_ref[:, gs] = (dyz * y_g * _dsilu(z_g)).astype(MXU_DTYPE)
            dy_g = dyz * sz
            dy_s[...] = dy_g.astype(MXU_DTYPE)
            e_mat = _head_expand(g, hpg, gw)
            ax = _dot01(acs, e_mat, "a", 3)
            dtx = _dot01(dt, e_mat, "a", 3)
            dx = _dot01(dvec8, e_mat, "a", 3)[0:1, :]
            xg, bg, cg = x_ref[:, gs], b_ref[:, ns], c_ref[:, ns]
            xdt = xg * dtx
            xdt_s[...] = xdt.astype(MXU_DTYPE)
            aend = ax[CHUNK - 1:CHUNK, :]
            e = jnp.exp(aend - ax)
            ea = jnp.exp(ax)
            eend = jnp.exp(aend)
            htp = st_ref[0, ns, :].astype(F32)
            dht = dhts[ns, :]
            cb = _dot_nt(cg, bg)
            q = _dot(bg, dht)
            dxdt_s[...] = e * q
            wl = e * q * xdt
            d_b = _dot_nt(e * xdt, dht)
            yi = ea * _dot(cg, htp)
            eady = ea * dy_g
            d_c = _dot_nt(eady, htp)
            t1 = jnp.sum(dht * htp, axis=0, keepdims=True) * eend
            dhts[ns, :] = eend * dht + _dot_tn(cg, eady)
            da = da + _dot01(dy_g * yi - wl, e_mat, "a", 3, _NT)
            tail = jnp.broadcast_to(jnp.sum(wl, axis=0, keepdims=True) + t1, (SUBLANES, gw))
            da_end = _dot01(tail, e_mat, "a", 3, _NT)[0:1, :]
            da = da + jnp.where(rows == CHUNK - 1, da_end, 0.0)
            dcb = jnp.zeros((CHUNK, CHUNK), F32)
            for hh in range(hpg):
                h = g * hpg + hh
                lmat = jnp.exp(jnp.where(tri, acs[:, h:h + 1] - acs_t[h:h + 1, :], -jnp.inf))
                hl = slice(hh * hd, (hh + 1) * hd)
                dy_h, xdt_h = dy_s[:, hl], xdt_s[:, hl]
                dxdt_s[:, hl] += _dot_tn(cb * lmat, dy_h)
                dml = _dot_nt(dy_h, xdt_h) * lmat
                dcb = dcb + dml
                gmat = dml * cb
                da = da + jnp.where(lane == h, jnp.sum(gmat, axis=1, keepdims=True), 0.0)
                da_t = da_t - jnp.where(head_row == h, jnp.sum(gmat, axis=0, keepdims=True), 0.0)
            d_c = d_c + _dot(dcb, bg)
            d_b = d_b + _dot_tn(dcb, cg)
            dxdt = dxdt_s[...]
            dxa_ref[:, gs] = dxdt * dtx + dx * dy_g
            dxa_ref[:, di + g * n:di + (g + 1) * n] = d_b
            dxa_ref[:, di + gn + g * n:di + gn + (g + 1) * n] = d_c
            ddt = ddt + _dot01(dxdt * xg, e_mat, "a", 3, _NT)
            gd8 = jnp.broadcast_to(jnp.sum(dy_g * xg, axis=0, keepdims=True), (SUBLANES, gw))
            gd = gd + _dot01(gd8, e_mat, "a", 3, _NT)[0:1, :]
        da = da + da_t.T
        triu_f = triu.astype(F32)
        ddta = _dot01(triu_f, da, "b", 3)
        ddt = ddt + ddta * a
        draw = ddt * sig
        ddt_ref[...] = draw.astype(MXU_DTYPE)
        gb_ref[...] += jnp.sum(draw, axis=0, keepdims=True)
        ga_ref[...] += jnp.sum(ddta * dt, axis=0, keepdims=True) * a
        gd_ref[...] += gd

    def col(width, c0):
        return pl.BlockSpec((CHUNK, width), lambda i: (nc - 1 - i, c0 // width))

    vec = pl.BlockSpec((1, LANES), lambda i: (0, 0))
    wide = pl.BlockSpec((1, di), lambda i: (0, 0))
    wa = di + 2 * gn
    return pl.pallas_call(
        body, name=name, grid=(nc,),
        in_specs=[col(di, 0), col(di, 0), col(gn, di), col(gn, di + gn), col(LANES, c_dt), col(di, c_z), col(di, 0),
                  pl.BlockSpec((1, gn, gw), lambda i: (nc - 1 - i, 0, 0)), vec, vec, vec, wide],
        out_specs=[col(wa, 0), col(di, 0), col(LANES, 0), vec, vec, vec, wide],
        out_shape=[jax.ShapeDtypeStruct((t, wa), F32), jax.ShapeDtypeStruct((t, di), MXU_DTYPE),
                   jax.ShapeDtypeStruct((t, LANES), MXU_DTYPE), jax.ShapeDtypeStruct((1, LANES), F32),
                   jax.ShapeDtypeStruct((1, LANES), F32), jax.ShapeDtypeStruct((1, LANES), F32),
                   jax.ShapeDtypeStruct((1, di), F32)],
        scratch_shapes=[pltpu.VMEM((gn, gw), F32), pltpu.VMEM((CHUNK, gw), MXU_DTYPE),
                        pltpu.VMEM((CHUNK, gw), MXU_DTYPE), pltpu.VMEM((CHUNK, gw), F32)],
        compiler_params=_params(("arbitrary",)),
    )(dys, xa, xa, xa, proj, proj, ypre, st, dtb, alog, dvec, nw)


def _adam_math(w, g, m, v):
    m2 = ADAM_B1 * m + (1.0 - ADAM_B1) * g
    v2 = ADAM_B2 * v + (1.0 - ADAM_B2) * (g * g)
    m_hat = m2 / (1.0 - ADAM_B1 ** ADAM_STEP)
    v_hat = v2 / (1.0 - ADAM_B2 ** ADAM_STEP)
    delta = -ADAM_LR * (m_hat / (jnp.sqrt(v_hat) + ADAM_EPS) + ADAM_WD * w)
    return delta, m2, v2


def _adam_big(w, p_mine, p_sib, m, v, name):
    r, c = w.shape
    tr = _pick(r, (128, 64, 32, 16, 8))

    def body(w_ref, a_ref, b_ref, m_ref, v_ref, g_ref, d_ref, m2_ref, v2_ref):
        g = a_ref[...] + b_ref[...]
        delta, m2, v2 = _adam_math(w_ref[...], g, m_ref[...], v_ref[...])
        g_ref[...] = g
        d_ref[...] = delta
        m2_ref[...] = m2
        v2_ref[...] = v2

    blk = pl.BlockSpec((tr, c), lambda i: (i, 0))
    out = jax.ShapeDtypeStruct((r, c), F32)
    return pl.pallas_call(
        body, name=name, grid=(r // tr,), in_specs=[blk] * 5, out_specs=[blk] * 4, out_shape=[out] * 4,
        compiler_params=_params(("parallel",)),
    )(w, p_mine, p_sib, m, v)


def _sum4(parts, name):
    _, r, c = parts.shape
    tr = _pick(r, (128, 64, 32, 16, 8))

    def body(p_ref, o_ref):
        acc = p_ref[0].astype(F32)
        for j in range(1, 4):
            acc = acc + p_ref[j].astype(F32)
        o_ref[...] = acc

    return pl.pallas_call(
        body, name=name, grid=(r // tr,),
        in_specs=[pl.BlockSpec((4, tr, c), lambda i: (0, i, 0))],
        out_specs=pl.BlockSpec((tr, c), lambda i: (i, 0)),
        out_shape=jax.ShapeDtypeStruct((r, c), F32),
        compiler_params=_params(("parallel",)),
    )(parts)


def _adam_small(w, gparts, m, v, name):
    r = w.shape[0]

    def body(w_ref, g_ref, m_ref, v_ref, go_ref, d_ref, m2_ref, v2_ref):
        g = g_ref[0]
        for j in range(1, 8):
            g = g + g_ref[j]
        delta, m2, v2 = _adam_math(w_ref[...], g, m_ref[...], v_ref[...])
        go_ref[...] = g
        d_ref[...] = delta
        m2_ref[...] = m2
        v2_ref[...] = v2

    out = jax.ShapeDtypeStruct((r, LANES), F32)
    return pl.pallas_call(body, name=name, out_shape=[out] * 4)(w, gparts, m, v)


def _chip_exchange(arrays, scatter, name):
    n = len(arrays)
    outs = [jax.ShapeDtypeStruct((4,) + (a.shape[1:] if scatter else a.shape), a.dtype) for a in arrays]

    def body(*refs):
        srcs, dsts = refs[:n], refs[n:2 * n]
        send, recv, loc = refs[2 * n:]
        x, y, c = lax.axis_index("x"), lax.axis_index("y"), lax.axis_index("c")
        mine = 2 * x + y
        peers = [(1 - x, y), (x, 1 - y), (1 - x, 1 - y)]
        copies, locals_ = [], []
        for i in range(n):
            own = srcs[i].at[mine] if scatter else srcs[i]
            lc = pltpu.make_async_copy(own, dsts[i].at[mine], loc.at[i])
            lc.start()
            locals_.append(lc)
            for k, (px, py) in enumerate(peers):
                src = srcs[i].at[2 * px + py] if scatter else srcs[i]
                cp = pltpu.make_async_remote_copy(src_ref=src, dst_ref=dsts[i].at[mine], send_sem=send.at[i, k],
                                                  recv_sem=recv.at[i, k], device_id=(px, py, c), device_id_type=MESH)
                cp.start()
                copies.append(cp)
        for cp in copies:
            cp.wait()
        for lc in locals_:
            lc.wait()

    anyspec = pl.BlockSpec(memory_space=pl.ANY)
    return pl.pallas_call(
        body, name=name, in_specs=[anyspec] * n, out_specs=[anyspec] * n, out_shape=outs,
        scratch_shapes=[pltpu.SemaphoreType.DMA((n, 3)), pltpu.SemaphoreType.DMA((n, 3)), pltpu.SemaphoreType.DMA((n,))],
        compiler_params=pltpu.CompilerParams(has_side_effects=True),
    )(*arrays)


def _sibling_swap(arrays, name):
    n = len(arrays)

    def body(*refs):
        srcs, dsts = refs[:n], refs[n:2 * n]
        send, recv = refs[2 * n:]
        x, y, c = lax.axis_index("x"), lax.axis_index("y"), lax.axis_index("c")
        copies = []
        for i in range(n):
            cp = pltpu.make_async_remote_copy(src_ref=srcs[i], dst_ref=dsts[i], send_sem=send.at[i], recv_sem=recv.at[i],
                                              device_id=(x, y, 1 - c), device_id_type=MESH)
            cp.start()
            copies.append(cp)
        for cp in copies:
            cp.wait()

    anyspec = pl.BlockSpec(memory_space=pl.ANY)
    return pl.pallas_call(
        body, name=name, in_specs=[anyspec] * n, out_specs=[anyspec] * n,
        out_shape=[jax.ShapeDtypeStruct(a.shape, a.dtype) for a in arrays],
        scratch_shapes=[pltpu.SemaphoreType.DMA((n,)), pltpu.SemaphoreType.DMA((n,))],
        compiler_params=pltpu.CompilerParams(has_side_effects=True),
    )(*arrays)


def _allgather8(v, name):
    def body(src, dst, send, recv, loc):
        x, y, c = lax.axis_index("x"), lax.axis_index("y"), lax.axis_index("c")
        mine = 4 * x + 2 * y + c
        lc = pltpu.make_async_copy(src, dst.at[mine], loc)
        lc.start()
        copies = []
        for k in range(1, 8):
            fx, fy, fc = (k >> 2) & 1, (k >> 1) & 1, k & 1
            peer = (1 - x if fx else x, 1 - y if fy else y, 1 - c if fc else c)
            cp = pltpu.make_async_remote_copy(src_ref=src, dst_ref=dst.at[mine], send_sem=send.at[k - 1],
                                              recv_sem=recv.at[k - 1], device_id=peer, device_id_type=MESH)
            cp.start()
            copies.append(cp)
        for cp in copies:
            cp.wait()
        lc.wait()

    anyspec = pl.BlockSpec(memory_space=pl.ANY)
    return pl.pallas_call(
        body, name=name, in_specs=[anyspec], out_specs=anyspec,
        out_shape=jax.ShapeDtypeStruct((8,) + v.shape, v.dtype),
        scratch_shapes=[pltpu.SemaphoreType.DMA((7,)), pltpu.SemaphoreType.DMA((7,)), pltpu.SemaphoreType.DMA],
        compiler_params=pltpu.CompilerParams(has_side_effects=True),
    )(v)


def _pack(parts):
    flat = jnp.concatenate([p.reshape(-1).astype(F32) for p in parts])
    pad = (-flat.shape[0]) % (SUBLANES * LANES)
    return jnp.pad(flat, (0, pad)).reshape(-1, LANES)


def _unpack(slab, shapes):
    flat = slab.reshape(-1)
    out, off = [], 0
    for s in shapes:
        size = int(np.prod(s))
        out.append(flat[off:off + size].reshape(s))
        off += size
    return out


def _pad_lanes(v):
    return jnp.pad(v.reshape(1, -1), ((0, 0), (0, LANES - v.shape[-1])))


def kernel(x, meta_tokens, mix_norm_w, w_in, ssd_conv_w, ssd_conv_b, ssd_dt_bias, ssd_A_log, ssd_D, ssd_norm_w, w_branch_ssd, w_branch_ret, w_out, ffn_norm_w, w_up, ffn_conv_w, ffn_conv_b, w_down, final_norm_w, loss_target, m_meta_tokens, m_mix_norm_w, m_w_in, m_ssd_conv_w, m_ssd_conv_b, m_ssd_dt_bias, m_ssd_A_log, m_ssd_D, m_ssd_norm_w, m_w_branch_ssd, m_w_branch_ret, m_w_out, m_ffn_norm_w, m_w_up, m_ffn_conv_w, m_ffn_conv_b, m_w_down, m_final_norm_w, v_meta_tokens, v_mix_norm_w, v_w_in, v_ssd_conv_w, v_ssd_conv_b, v_ssd_dt_bias, v_ssd_A_log, v_ssd_D, v_ssd_norm_w, v_w_branch_ssd, v_w_branch_ret, v_w_out, v_ffn_norm_w, v_w_up, v_ffn_conv_w, v_ffn_conv_b, v_w_down, v_final_norm_w):
    seq, d = x.shape[1], x.shape[2]
    t = CHUNK + seq
    di = 2 * d
    nh = di // SSD_HEAD_DIM
    gn = SSD_GROUPS * SSD_STATE
    cw = di + 2 * gn
    f = w_down.shape[1] * 4
    chip = 2 * lax.axis_index("x") + lax.axis_index("y")

    order = [("z", di), ("v", di), ("g", di), ("xbc", cw), ("q", d), ("k", d), ("gs", d), ("gr", d), ("dt", LANES)]
    col, acc = {}, 0
    for nm, wd in order:
        col[nm] = acc
        acc += wd
    wp = acc
    ref_order = [("z", di), ("xbc", cw), ("dt", nh), ("q", d), ("k", d), ("v", di), ("g", di), ("gs", d), ("gr", d)]
    ref_off, acc = {}, 0
    for nm, wd in ref_order:
        ref_off[nm] = (acc, wd)
        acc += wd
    in_dim = acc

    small_shapes = [meta_tokens.shape, ssd_conv_w.shape[1:], ffn_conv_w.shape[1:]]
    small_local = _pack([meta_tokens, ssd_conv_w[0], ffn_conv_w[0]])
    big_local = [w_in[0], w_branch_ssd[0], w_branch_ret[0], w_out[0], w_up[0], w_down[0]]
    gathered = _chip_exchange([a.astype(WIRE_DTYPE) for a in big_local] + [small_local], False, "gather_weights")
    g_in, g_bs, g_br, g_out, g_up, g_down, g_small = gathered
    w_in_full = jnp.moveaxis(g_in, 0, 1).reshape(d, in_dim)
    pieces = []
    for nm, wd in order:
        o, rw = ref_off[nm]
        p = w_in_full[:, o:o + rw]
        if rw < wd:
            p = jnp.pad(p, ((0, 0), (0, wd - rw)))
        pieces.append(p)
    w_p = jnp.concatenate(pieces, axis=1)
    w_bs = g_bs.reshape(di, d)
    w_br = g_br.reshape(di, d)
    w_o = g_out.reshape(d, d)
    w_u = jnp.moveaxis(g_up, 0, 1).reshape(d, 2 * f)
    w_d = g_down.reshape(f, d)
    smalls = [_unpack(g_small[j], small_shapes) for j in range(4)]
    meta_full = jnp.concatenate([s[0] for s in smalls], axis=1)
    scw = jnp.concatenate([s[1] for s in smalls], axis=1)
    fcw = jnp.concatenate([s[2] for s in smalls], axis=1)
    scb, fcb = ssd_conv_b, ffn_conv_b
    dtb, alog, dvec = _pad_lanes(ssd_dt_bias), _pad_lanes(ssd_A_log), _pad_lanes(ssd_D)
    fin_w = final_norm_w.reshape(1, d)

    hq = d // RET_HEADS
    pos = jnp.arange(t, dtype=F32) - FRONT
    inv_freq = ROPE_BASE ** (-jnp.linspace(0.0, 1.0, hq // 2, dtype=F32))
    ang = pos[:, None] * inv_freq[None, :]
    cos, sin = jnp.cos(ang), jnp.sin(ang)

    h0 = jnp.concatenate([jnp.zeros((FRONT, d), F32), meta_full, x[0]], axis=0)
    tm = _row_tile(t)
    u1 = _rms_fwd(h0, mix_norm_w, "rms1_fwd")
    proj = _mm(u1, w_p, "nn", F32, "proj", tm, _pick(wp, (1920, 1536, 1280, 1024, 896, 768, 640, 512, 384, 256, 128)), d)
    xa = _ssd_conv_fwd(proj, col["xbc"], cw, scw, scb, "ssd_conv_fwd")
    ypre, yssd, st_ssd = _ssd_fwd(xa, proj, col["dt"], col["z"], dtb, alog, dvec, ssd_norm_w, di, "ssd_fwd")
    o_ret, yret, qr, kr, st_ret = _ret_fwd(proj, col["q"], col["k"], col["v"], col["g"], cos, sin, d, "ret_fwd")
    tn_d = _pick(d, (1024, 512, 256, 128))
    bs = _mm(yssd, w_bs, "nn", F32, "branch_ssd", tm, tn_d, _pick(di, (1024, 512, 256)))
    br = _mm(yret, w_br, "nn", F32, "branch_ret", tm, tn_d, _pick(di, (1024, 512, 256)))
    merged = _gate_fwd(bs, br, proj, col["gs"], col["gr"], "gate_fwd")
    h1 = _mm(merged, w_o, "nn", F32, "out_proj", tm, tn_d, d, res=h0)
    u2 = _rms_fwd(h1, ffn_norm_w, "rms2_fwd")
    tn_f = _pick(2 * f, (1408, 1024, 768, 512, 256, 128))
    up = _mm(u2, w_u, "nn", F32, "up_proj", tm, tn_f, d)
    act = _ffn_conv_fwd(up, fcw, fcb, "ffn_conv_fwd")
    tk_f = _pick(f, (1408, 768, 704, 512, 256, 128))
    h2 = _mm(act, w_d, "nn", F32, "down_proj", tm, tn_d, tk_f, res=h1)
    loss8, d_h2, g_fin = _loss_bwd(h2, fin_w, loss_target[0], "loss_head")

    d_act = _mm(d_h2, w_d, "nt", F32, "d_act", tm, tk_f, d)
    g_wd = _mm(act, d_h2, "tn", WIRE_DTYPE, "g_w_down", tk_f, tn_d, tm)
    d_upg, d_upv, g_fcwg, g_fcwv, g_fcbg, g_fcbv = _ffn_conv_bwd(up, fcw, fcb, d_act, "ffn_conv_bwd")
    g_fcw = jnp.concatenate([g_fcwg, g_fcwv], axis=1)
    g_fcb = jnp.concatenate([g_fcbg, g_fcbv], axis=1)
    d_u2 = _mm(d_upg, w_u[:, :f], "nt", F32, "d_u2_gate", tm, tn_d, tk_f)
    d_u2 = _mm(d_upv, w_u[:, f:], "nt", F32, "d_u2_value", tm, tn_d, tk_f, res=d_u2)
    g_wu = jnp.concatenate([_mm(u2, d_upg, "tn", WIRE_DTYPE, "g_w_up_gate", tn_d, tk_f, tm),
                            _mm(u2, d_upv, "tn", WIRE_DTYPE, "g_w_up_value", tn_d, tk_f, tm)], axis=1)
    d_h1, g_ffnw = _rms_bwd(h1, ffn_norm_w, d_u2, d_h2, "rms2_bwd")
    d_merged = _mm(d_h1, w_o, "nt", F32, "d_merged", tm, tn_d, d)
    g_wo = _mm(merged, d_h1, "tn", WIRE_DTYPE, "g_w_out", tn_d, tn_d, tm)
    d_bs, d_br, d_gs, d_gr = _gate_bwd(d_merged, bs, br, proj, col["gs"], col["gr"], "gate_bwd")
    tk_i = _pick(di, (1024, 512, 256))
    d_yssd = _mm(d_bs, w_bs, "nt", F32, "d_y_ssd", tm, tk_i, d)
    g_wbs = _mm(yssd, d_bs, "tn", WIRE_DTYPE, "g_w_branch_ssd", tk_i, tn_d, tm)
    d_yret = _mm(d_br, w_br, "nt", F32, "d_y_ret", tm, tk_i, d)
    g_wbr = _mm(yret, d_br, "tn", WIRE_DTYPE, "g_w_branch_ret", tk_i, tn_d, tm)
    dq, dk, dv, dg = _ret_bwd(d_yret, proj, col["v"], col["g"], o_ret, qr, kr, st_ret, cos, sin, d, "ret_bwd")
    d_xa, dz, ddt, g_dtb, g_alog, g_dvec, g_snw = _ssd_bwd(d_yssd, xa, proj, col["dt"], col["z"], ypre, st_ssd,
                                                             dtb, alog, dvec, ssd_norm_w, di, "ssd_bwd")
    d_xbc, g_scw, g_scb = _ssd_conv_bwd(proj, col["xbc"], cw, scw, scb, d_xa, "ssd_conv_bwd")
    d_proj = jnp.concatenate([dz, dv, dg, d_xbc, dq, dk, d_gs, d_gr, ddt], axis=1)
    tn_p = _pick(wp, (1920, 1536, 1280, 1024, 896, 768, 640, 512, 384, 256, 128))
    d_u1 = _mm(d_proj, w_p, "nt", F32, "d_u1", tm, tn_d, tn_p)
    g_wp = _mm(u1, d_proj, "tn", WIRE_DTYPE, "g_w_in", tn_d, tn_p, tm)
    d_h0, g_mixw = _rms_bwd(h0, mix_norm_w, d_u1, d_h1, "rms1_bwd")
    grad_x = d_h0[CHUNK:][None]
    g_meta = d_h0[FRONT:CHUNK]

    g_in_ref = jnp.concatenate([g_wp[:, col[nm]:col[nm] + rw] for nm, rw in ref_order], axis=1)
    sc_in = jnp.moveaxis(g_in_ref.reshape(d, 4, in_dim // 4), 1, 0)
    sc_up = jnp.moveaxis(g_wu.reshape(d, 4, 2 * f // 4), 1, 0)
    scattered = _chip_exchange(
        [sc_in, g_wbs.reshape(4, di // 4, d), g_wbr.reshape(4, di // 4, d), g_wo.reshape(4, d // 4, d), sc_up,
         g_wd.reshape(4, f // 4, d)], True, "scatter_grads")
    names = ["w_in", "w_branch_ssd", "w_branch_ret", "w_out", "w_up", "w_down"]
    part = [_sum4(p, "sum4_" + nm) for p, nm in zip(scattered, names)]
    part_sib = _sibling_swap(part, "swap_partials")
    big_w = [w_in, w_branch_ssd, w_branch_ret, w_out, w_up, w_down]
    big_m = [m_w_in, m_w_branch_ssd, m_w_branch_ret, m_w_out, m_w_up, m_w_down]
    big_v = [v_w_in, v_w_branch_ssd, v_w_branch_ret, v_w_out, v_w_up, v_w_down]
    big_out = {}
    for nm, w_, p_, s_, m_, v_ in zip(names, big_w, part, part_sib, big_m, big_v):
        res = _adam_big(w_[0], p_, s_, m_[0], v_[0], "adam_" + nm)
        big_out[nm] = [r[None] for r in res]

    kws, kwf = ssd_conv_w.shape[1], ffn_conv_w.shape[1]
    small_grads = [g_meta, g_mixw, g_scw[:kws], g_scb, g_dtb[:, :nh], g_alog[:, :nh], g_dvec[:, :nh], g_snw, g_ffnw,
                   g_fcw[:kwf], g_fcb, g_fin, loss8[0:1, 0:1]]
    sg_shapes = [g.shape for g in small_grads]
    gparts = _allgather8(_pack(small_grads), "gather_small_grads")

    def own_cols(a, width):
        return lax.dynamic_slice_in_dim(a, chip * width, width, axis=1)

    def widen(a, width_full):
        z = jnp.zeros(a.shape[:-1] + (width_full,), F32)
        return lax.dynamic_update_slice_in_dim(z, a, chip * a.shape[-1], axis=a.ndim - 1)

    def small_slab(meta_, mix_, scw_, scb_, dtb_, alog_, d_, snw_, ffnw_, fcw_, fcb_, fin_):
        return _pack([widen(meta_, d), mix_, widen(scw_[0], cw), scb_, dtb_, alog_, d_, snw_, ffnw_, widen(fcw_[0], 2 * f),
                      fcb_, fin_.reshape(1, d), jnp.zeros((1, 1), F32)])

    w_slab = small_slab(meta_tokens, mix_norm_w, ssd_conv_w, ssd_conv_b, ssd_dt_bias, ssd_A_log, ssd_D, ssd_norm_w,
                        ffn_norm_w, ffn_conv_w, ffn_conv_b, final_norm_w)
    m_slab = small_slab(m_meta_tokens, m_mix_norm_w, m_ssd_conv_w, m_ssd_conv_b, m_ssd_dt_bias, m_ssd_A_log, m_ssd_D,
                        m_ssd_norm_w, m_ffn_norm_w, m_ffn_conv_w, m_ffn_conv_b, m_final_norm_w)
    v_slab = small_slab(v_meta_tokens, v_mix_norm_w, v_ssd_conv_w, v_ssd_conv_b, v_ssd_dt_bias, v_ssd_A_log, v_ssd_D,
                        v_ssd_norm_w, v_ffn_norm_w, v_ffn_conv_w, v_ffn_conv_b, v_final_norm_w)
    small_res = [_unpack(s, sg_shapes) for s in _adam_small(w_slab, gparts, m_slab, v_slab, "adam_small")]
    loss = small_res[0][12].reshape(())

    def small_outputs(vals):
        meta_, mix_, scw_, scb_, dtb_, alog_, d_, snw_, ffnw_, fcw_, fcb_, fin_ = vals[:12]
        return {
            "meta_tokens": own_cols(meta_, d // 4), "mix_norm_w": mix_, "ssd_conv_w": own_cols(scw_, cw // 4)[None],
            "ssd_conv_b": scb_, "ssd_dt_bias": dtb_, "ssd_A_log": alog_, "ssd_D": d_, "ssd_norm_w": snw_,
            "ffn_norm_w": ffnw_, "ffn_conv_w": own_cols(fcw_, 2 * f // 4)[None], "ffn_conv_b": fcb_,
            "final_norm_w": fin_.reshape(d),
        }

    weights = ["meta_tokens", "mix_norm_w", "w_in", "ssd_conv_w", "ssd_conv_b", "ssd_dt_bias", "ssd_A_log", "ssd_D",
               "ssd_norm_w", "w_branch_ssd", "w_branch_ret", "w_out", "ffn_norm_w", "w_up", "ffn_conv_w", "ffn_conv_b",
               "w_down", "final_norm_w"]
    outs = [loss, grad_x]
    for kind in range(4):
        so = small_outputs(small_res[kind])
        for nm in weights:
            outs.append(big_out[nm][kind] if nm in big_out else so[nm])
    return tuple(outs)
```

```python
import functools
import math

import jax
import jax.numpy as jnp
import numpy as np
from jax import lax
from jax.experimental import pallas as pl
from jax.experimental.pallas import tpu as pltpu

F32 = jnp.float32
BF16 = jnp.bfloat16
MXU_DTYPE = BF16
WIRE_DTYPE = BF16

N_META = 16
CHUNK = 128
FRONT = CHUNK - N_META
EPS = 1e-6
SSD_HEAD_DIM = 64
SSD_GROUPS = 4
SSD_STATE = 128
SSD_CONV = 4
RET_HEADS = 4
ROPE_BASE = 10000.0
FFN_CONV = 3
LANES = 128
SUBLANES = 8
VMEM_LIMIT = 56 * 1024 * 1024

ADAM_LR = 0.001
ADAM_B1 = 0.9
ADAM_B2 = 0.999
ADAM_EPS = 1e-08
ADAM_WD = 0.01
ADAM_STEP = 10
MESH = pl.DeviceIdType.MESH


def _params(sem=None, vmem=VMEM_LIMIT):
    return pltpu.CompilerParams(dimension_semantics=sem, vmem_limit_bytes=vmem)


def _pick(n, cands):
    for c in cands:
        if n % c == 0:
            return c
    return n


def _silu(x):
    return x * jax.nn.sigmoid(x)


def _dsilu(x):
    s = jax.nn.sigmoid(x)
    return s * (1.0 + x * (1.0 - s))


def _dot(a, b, dims=(((1,), (0,)), ((), ()))):
    return lax.dot_general(a.astype(MXU_DTYPE), b.astype(MXU_DTYPE), dims, preferred_element_type=F32)


def _dot_nt(a, b):
    return _dot(a, b, (((1,), (1,)), ((), ())))


def _dot_tn(a, b):
    return _dot(a, b, (((0,), (0,)), ((), ())))


def _dot01(a, b, split, npass, dims=(((1,), (0,)), ((), ()))):
    rest = (a if split == "a" else b).astype(F32)
    fixed = (b if split == "a" else a).astype(BF16)
    acc = None
    for p in range(npass):
        piece = rest.astype(BF16)
        ops = (piece, fixed) if split == "a" else (fixed, piece)
        term = lax.dot_general(ops[0], ops[1], dims, preferred_element_type=F32)
        acc = term if acc is None else acc + term
        if p + 1 < npass:
            rest = rest - piece.astype(F32)
    return acc


_NT = (((1,), (1,)), ((), ()))


def _iota(shape, dim):
    return lax.broadcasted_iota(jnp.int32, shape, dim)


def _shift_down(cur, prev8, k):
    if k == 0:
        return cur
    rolled = pltpu.roll(cur, k, 0)
    i8 = _iota((SUBLANES, cur.shape[1]), 0)
    head = jnp.where(i8 < k, pltpu.roll(prev8, k, 0), rolled[0:SUBLANES])
    return jnp.concatenate([head, rolled[SUBLANES:]], axis=0)


def _shift_up(cur, next8, k):
    if k == 0:
        return cur
    n = cur.shape[0]
    rolled = pltpu.roll(cur, n - k, 0)
    i8 = _iota((SUBLANES, cur.shape[1]), 0)
    tail = jnp.where(i8 >= SUBLANES - k, pltpu.roll(next8, SUBLANES - k, 0), rolled[n - SUBLANES:])
    return jnp.concatenate([rolled[:n - SUBLANES], tail], axis=0)


class _Comm:
    def __init__(self, ins, outs, nsem, make):
        self.ins, self.outs, self.nsem, self.make = list(ins), list(outs), nsem, make


def _place():
    x, y, c = lax.axis_index("x"), lax.axis_index("y"), lax.axis_index("c")
    return x, y, c, 2 * x + y, [(1 - x, y), (x, 1 - y), (1 - x, 1 - y)]


def _call(body, name, grid, in_specs, out_specs, out_shape, scratch, sem, args, comm=None):
    if comm is None:
        return pl.pallas_call(body, name=name, grid=grid, in_specs=in_specs, out_specs=out_specs, out_shape=out_shape,
                              scratch_shapes=scratch, compiler_params=_params(sem))(*args)
    n_in, n_out, n_scr = len(in_specs), len(out_specs), len(scratch)
    ci, co = len(comm.ins), len(comm.outs)

    def wrapped(*refs):
        ins, refs = refs[:n_in], refs[n_in:]
        cins, refs = refs[:ci], refs[ci:]
        outs, refs = refs[:n_out], refs[n_out:]
        couts, refs = refs[:co], refs[co:]
        scr, sems = refs[:n_scr], refs[n_scr:]
        first = functools.reduce(jnp.logical_and, [pl.program_id(a) == 0 for a in range(len(grid))])
        last = functools.reduce(jnp.logical_and, [pl.program_id(a) == grid[a] - 1 for a in range(len(grid))])

        @pl.when(first)
        def _():
            for cp in comm.make(cins, couts, *sems):
                cp.start()

        body(*ins, *outs, *scr)

        @pl.when(last)
        def _():
            for cp in comm.make(cins, couts, *sems):
                cp.wait()

    anyspec = pl.BlockSpec(memory_space=pl.ANY)
    dma = pltpu.SemaphoreType.DMA((comm.nsem,))
    return pl.pallas_call(
        wrapped, name=name, grid=grid, in_specs=list(in_specs) + [anyspec] * ci,
        out_specs=list(out_specs) + [anyspec] * co, out_shape=list(out_shape) + comm.outs,
        scratch_shapes=list(scratch) + [dma, dma, dma],
        compiler_params=_params(("arbitrary",) * len(grid)))(*args, *comm.ins)


def _run_comm(comm, name):
    ci, co = len(comm.ins), len(comm.outs)

    def body(*refs):
        cins, couts, sems = refs[:ci], refs[ci:ci + co], refs[ci + co:]
        for cp in comm.make(cins, couts, *sems):
            cp.start()
        for cp in comm.make(cins, couts, *sems):
            cp.wait()

    anyspec = pl.BlockSpec(memory_space=pl.ANY)
    dma = pltpu.SemaphoreType.DMA((comm.nsem,))
    return pl.pallas_call(body, name=name, in_specs=[anyspec] * ci, out_specs=[anyspec] * co, out_shape=comm.outs,
                          scratch_shapes=[dma, dma, dma])(*comm.ins)


def _half_rows(c, rows):
    h = rows // 2
    return pl.ds(pl.multiple_of(c * h, 16), h)


def _gather_ici(arrays):
    for a in arrays:
        assert a.shape[0] % 32 == 0, a.shape

    def make(ins, outs, send, recv, loc):
        x, y, c, mine, peers = _place()
        cps = []
        for i, a in enumerate(arrays):
            half = _half_rows(c, a.shape[0])
            cps.append(pltpu.make_async_copy(ins[i], outs[i].at[mine], loc.at[i]))
            for k, (px, py) in enumerate(peers):
                cps.append(pltpu.make_async_remote_copy(
                    src_ref=ins[i].at[half], dst_ref=outs[i].at[mine, half], send_sem=send.at[3 * i + k],
                    recv_sem=recv.at[3 * i + k], device_id=(px, py, c), device_id_type=MESH))
        return cps

    outs = [jax.ShapeDtypeStruct((4,) + a.shape, a.dtype) for a in arrays]
    return _Comm(arrays, outs, 3 * len(arrays), make)


def _gather_d2d(bufs):
    def make(ins, outs, send, recv, loc):
        x, y, c, mine, peers = _place()
        cps = []
        for i, a in enumerate(bufs):
            half = _half_rows(c, a.shape[1])
            cps.append(pltpu.make_async_copy(ins[i].at[mine], outs[i].at[mine], loc.at[4 * i]))
            for k, (px, py) in enumerate(peers):
                pj = 2 * px + py
                cps.append(pltpu.make_async_copy(ins[i].at[pj, half], outs[i].at[pj, half], loc.at[4 * i + 1 + k]))
                cps.append(pltpu.make_async_remote_copy(
                    src_ref=ins[i].at[pj, half], dst_ref=outs[i].at[pj, half], send_sem=send.at[3 * i + k],
                    recv_sem=recv.at[3 * i + k], device_id=(x, y, 1 - c), device_id_type=MESH))
        return cps

    outs = [jax.ShapeDtypeStruct(a.shape, a.dtype) for a in bufs]
    return _Comm(bufs, outs, 4 * len(bufs), make)


def _scatter_d2d(grads):
    for a in grads:
        assert a.shape[1] % 32 == 0, a.shape

    def make(ins, outs, send, recv, loc):
        x, y, c, mine, peers = _place()
        cps = []
        for i, a in enumerate(grads):
            other = _half_rows(1 - c, a.shape[1])
            cps.append(pltpu.make_async_remote_copy(
                src_ref=ins[i].at[:, other], dst_ref=outs[i], send_sem=send.at[i], recv_sem=recv.at[i],
                device_id=(x, y, 1 - c), device_id_type=MESH))
        return cps

    outs = [jax.ShapeDtypeStruct((4, a.shape[1] // 2, a.shape[2]), a.dtype) for a in grads]
    return _Comm(grads, outs, len(grads), make)


def _scatter_ici(parts):
    def make(ins, outs, send, recv, loc):
        x, y, c, mine, peers = _place()
        cps = []
        for i in range(len(parts)):
            cps.append(pltpu.make_async_copy(ins[i].at[mine], outs[i].at[mine], loc.at[i]))
            for k, (px, py) in enumerate(peers):
                cps.append(pltpu.make_async_remote_copy(
                    src_ref=ins[i].at[2 * px + py], dst_ref=outs[i].at[mine], send_sem=send.at[3 * i + k],
                    recv_sem=recv.at[3 * i + k], device_id=(px, py, c), device_id_type=MESH))
        return cps

    outs = [jax.ShapeDtypeStruct(a.shape, a.dtype) for a in parts]
    return _Comm(parts, outs, 3 * len(parts), make)


def _sibling_swap(arrays):
    def make(ins, outs, send, recv, loc):
        x, y, c, mine, peers = _place()
        return [pltpu.make_async_remote_copy(src_ref=ins[i], dst_ref=outs[i], send_sem=send.at[i], recv_sem=recv.at[i],
                                             device_id=(x, y, 1 - c), device_id_type=MESH) for i in range(len(arrays))]

    outs = [jax.ShapeDtypeStruct(a.shape, a.dtype) for a in arrays]
    return _Comm(arrays, outs, len(arrays), make)


def _mm(a, b, mode, out_dtype, name, tm, tn, tk, res=None, comm=None):
    if mode == "nn":
        (m, kd), n = a.shape, b.shape[1]
        a_spec = pl.BlockSpec((tm, tk), lambda i, j, k: (i, k))
        b_spec = pl.BlockSpec((tk, tn), lambda i, j, k: (k, j))
        dims = (((1,), (0,)), ((), ()))
    elif mode == "nt":
        (m, kd), n = a.shape, b.shape[0]
        a_spec = pl.BlockSpec((tm, tk), lambda i, j, k: (i, k))
        b_spec = pl.BlockSpec((tn, tk), lambda i, j, k: (j, k))
        dims = (((1,), (1,)), ((), ()))
    else:
        (kd, m), n = a.shape, b.shape[1]
        a_spec = pl.BlockSpec((tk, tm), lambda i, j, k: (k, i))
        b_spec = pl.BlockSpec((tk, tn), lambda i, j, k: (k, j))
        dims = (((0,), (0,)), ((), ()))
    assert m % tm == 0 and n % tn == 0 and kd % tk == 0, (name, m, n, kd, tm, tn, tk)
    nk = kd // tk
    has_res = res is not None

    def body(*refs):
        if has_res:
            a_ref, b_ref, r_ref, o_ref, acc = refs
        else:
            a_ref, b_ref, o_ref, acc = refs
        k = pl.program_id(2)

        @pl.when(k == 0)
        def _():
            acc[...] = jnp.zeros_like(acc)

        acc[...] += _dot(a_ref[...], b_ref[...], dims)

        @pl.when(k == nk - 1)
        def _():
            r = acc[...]
            if has_res:
                r = r + r_ref[...].astype(F32)
            o_ref[...] = r.astype(out_dtype)

    in_specs = [a_spec, b_spec]
    args = [a, b]
    if has_res:
        in_specs.append(pl.BlockSpec((tm, tn), lambda i, j, k: (i, j)))
        args.append(res)
    res = _call(body, name, (m // tm, n // tn, nk), in_specs, [pl.BlockSpec((tm, tn), lambda i, j, k: (i, j))],
                [jax.ShapeDtypeStruct((m, n), out_dtype)], [pltpu.VMEM((tm, tn), F32)],
                ("parallel", "parallel", "arbitrary"), args, comm)
    return res[0] if comm is None else (res[0], res[1:])


def _rms_fwd(h, w, name):
    t, d = h.shape
    tr = _pick(t, (640, 512, 384, 256, 128))

    def body(h_ref, w_ref, u_ref):
        x = h_ref[...]
        r = lax.rsqrt(jnp.mean(x * x, axis=1, keepdims=True) + EPS)
        u_ref[...] = (x * r * w_ref[...]).astype(MXU_DTYPE)

    return pl.pallas_call(
        body, name=name, grid=(t // tr,),
        in_specs=[pl.BlockSpec((tr, d), lambda i: (i, 0)), pl.BlockSpec((1, d), lambda i: (0, 0))],
        out_specs=pl.BlockSpec((tr, d), lambda i: (i, 0)),
        out_shape=jax.ShapeDtypeStruct((t, d), MXU_DTYPE),
        compiler_params=_params(("parallel",)),
    )(h, w)


def _rms_bwd(h, w, du, res, name, comm=None):
    t, d = h.shape
    tr = _pick(t, (640, 512, 384, 256, 128))

    def body(h_ref, w_ref, du_ref, res_ref, dh_ref, gw_ref):
        @pl.when(pl.program_id(0) == 0)
        def _():
            gw_ref[...] = jnp.zeros_like(gw_ref)

        x = h_ref[...]
        r = lax.rsqrt(jnp.mean(x * x, axis=1, keepdims=True) + EPS)
        xhat = x * r
        dy = du_ref[...].astype(F32)
        dxh = dy * w_ref[...]
        dh = r * (dxh - xhat * jnp.mean(dxh * xhat, axis=1, keepdims=True))
        dh_ref[...] = dh + res_ref[...]
        gw_ref[...] += jnp.sum(dy * xhat, axis=0, keepdims=True)

    row = pl.BlockSpec((tr, d), lambda i: (i, 0))
    vec = pl.BlockSpec((1, d), lambda i: (0, 0))
    return _call(body, name, (t // tr,), [row, vec, row, row], [row, vec],
                 [jax.ShapeDtypeStruct((t, d), F32), jax.ShapeDtypeStruct((1, d), F32)], [], ("arbitrary",),
                 (h, w, du, res), comm)


def _loss_bwd(h2, w, target, name):
    t, d = h2.shape
    nc = t // CHUNK

    def body(h_ref, w_ref, tg_ref, loss_ref, dh_ref, gw_ref):
        i = pl.program_id(0)

        @pl.when(i == 0)
        def _():
            gw_ref[...] = jnp.zeros_like(gw_ref)
            loss_ref[...] = jnp.zeros_like(loss_ref)
            dh_ref[...] = jnp.zeros_like(dh_ref)

        @pl.when(i > 0)
        def _():
            x = h_ref[...]
            r = lax.rsqrt(jnp.mean(x * x, axis=1, keepdims=True) + EPS)
            xhat = x * r
            diff = xhat * w_ref[...] - tg_ref[...]
            loss_ref[...] += 0.5 * jnp.sum(jnp.sum(diff * diff, axis=1, keepdims=True), axis=0, keepdims=True) / d
            dy = diff / d
            dxh = dy * w_ref[...]
            dh_ref[...] = r * (dxh - xhat * jnp.mean(dxh * xhat, axis=1, keepdims=True))
            gw_ref[...] += jnp.sum(dy * xhat, axis=0, keepdims=True)

    row = pl.BlockSpec((CHUNK, d), lambda i: (i, 0))
    vec = pl.BlockSpec((1, d), lambda i: (0, 0))
    return pl.pallas_call(
        body, name=name, grid=(nc,),
        in_specs=[row, vec, pl.BlockSpec((CHUNK, d), lambda i: (jnp.maximum(i - 1, 0), 0))],
        out_specs=[pl.BlockSpec((SUBLANES, LANES), lambda i: (0, 0)), row, vec],
        out_shape=[jax.ShapeDtypeStruct((SUBLANES, LANES), F32), jax.ShapeDtypeStruct((t, d), F32),
                   jax.ShapeDtypeStruct((1, d), F32)],
        compiler_params=_params(("arbitrary",)),
    )(h2, w, target)


def _conv_taps(cur, prev8, w_ref, b_ref, kw):
    taps = [_shift_down(cur, prev8, kw - 1 - k) for k in range(kw)]
    y = b_ref[...] + taps[kw - 1] * w_ref[kw - 1:kw, :]
    for k in range(kw - 1):
        y = y + taps[k] * w_ref[k:k + 1, :]
    return y, taps


def _conv_pre(cur, prev8, w_ref, b_ref, kw):
    return _conv_taps(cur, prev8, w_ref, b_ref, kw)[0]


def _conv_dx(dpre, next8, w_ref, kw):
    acc = dpre * w_ref[kw - 1:kw, :]
    for k in range(kw - 1):
        acc = acc + _shift_up(dpre, next8, kw - 1 - k) * w_ref[k:k + 1, :]
    return acc


def _row_tile(t):
    return _pick(t, (640, 512, 384, 256, 128))


def _ssd_conv_fwd(proj, col0, width, w, b, name):
    t = proj.shape[0]
    kw = w.shape[0]
    tr, tc = _row_tile(t), _pick(width, (512, 256, 128))
    c0, rb = col0 // tc, tr // SUBLANES
    assert col0 % tc == 0

    def body(x_ref, p_ref, w_ref, b_ref, o_ref):
        i = pl.program_id(1)
        prev8 = jnp.where(i > 0, p_ref[...], 0.0)
        pre = _conv_pre(x_ref[...], prev8, w_ref, b_ref, kw)
        rows = _iota((tr, 1), 0) + i * tr
        o_ref[...] = jnp.where(rows >= FRONT, _silu(pre), 0.0)

    return pl.pallas_call(
        body, name=name, grid=(width // tc, t // tr),
        in_specs=[pl.BlockSpec((tr, tc), lambda j, i: (i, c0 + j)),
                  pl.BlockSpec((SUBLANES, tc), lambda j, i: (jnp.maximum(i * rb - 1, 0), c0 + j)),
                  pl.BlockSpec((kw, tc), lambda j, i: (0, j)),
                  pl.BlockSpec((1, tc), lambda j, i: (0, j))],
        out_specs=pl.BlockSpec((tr, tc), lambda j, i: (i, j)),
        out_shape=jax.ShapeDtypeStruct((t, width), F32),
        compiler_params=_params(("parallel", "parallel")),
    )(proj, proj, w, b)


def _ssd_conv_bwd(proj, col0, width, w, b, dact, name, comm=None):
    t = proj.shape[0]
    kw = w.shape[0]
    tr, tc = _row_tile(t), _pick(width, (512, 256, 128))
    c0, rb, nrow = col0 // tc, tr // SUBLANES, t // tr

    def body(x_ref, p_ref, w_ref, b_ref, d_ref, o_ref, gw_ref, gb_ref, carry):
        i = pl.program_id(1)
        ti = nrow - 1 - i

        @pl.when(i == 0)
        def _():
            gw_ref[...] = jnp.zeros_like(gw_ref)
            gb_ref[...] = jnp.zeros_like(gb_ref)
            carry[...] = jnp.zeros_like(carry)

        prev8 = jnp.where(ti > 0, p_ref[...], 0.0)
        pre, taps = _conv_taps(x_ref[...], prev8, w_ref, b_ref, kw)
        valid = _iota((tr, 1), 0) + ti * tr >= FRONT
        dpre = jnp.where(valid, d_ref[...] * _dsilu(pre), 0.0)
        gb_ref[...] += jnp.sum(dpre, axis=0, keepdims=True)
        for k in range(kw):
            gw_ref[k:k + 1, :] += jnp.sum(dpre * taps[k], axis=0, keepdims=True)
        o_ref[...] = jnp.where(valid, _conv_dx(dpre, carry[...], w_ref, kw), 0.0).astype(MXU_DTYPE)
        carry[...] = dpre[0:SUBLANES]

    return _call(
        body, name, (width // tc, nrow),
        [pl.BlockSpec((tr, tc), lambda j, i: (nrow - 1 - i, c0 + j)),
         pl.BlockSpec((SUBLANES, tc), lambda j, i: (jnp.maximum((nrow - 1 - i) * rb - 1, 0), c0 + j)),
         pl.BlockSpec((kw, tc), lambda j, i: (0, j)),
         pl.BlockSpec((1, tc), lambda j, i: (0, j)),
         pl.BlockSpec((tr, tc), lambda j, i: (nrow - 1 - i, j))],
        [pl.BlockSpec((tr, tc), lambda j, i: (nrow - 1 - i, j)),
         pl.BlockSpec((SUBLANES, tc), lambda j, i: (0, j)),
         pl.BlockSpec((1, tc), lambda j, i: (0, j))],
        [jax.ShapeDtypeStruct((t, width), MXU_DTYPE), jax.ShapeDtypeStruct((SUBLANES, width), F32),
         jax.ShapeDtypeStruct((1, width), F32)],
        [pltpu.VMEM((SUBLANES, tc), F32)], ("parallel", "arbitrary"), (proj, proj, w, b, dact), comm)


def _ffn_conv_fwd(up, w, b, name):
    t, f2 = up.shape
    f = f2 // 2
    kw = w.shape[0]
    tr, tc = _row_tile(t), _pick(f, (256, 128))
    nf, rb = f // tc, tr // SUBLANES

    def body(xg, pg, xv, pv, wg, wv, bg, bv, o_ref):
        i = pl.program_id(1)
        ag = _conv_pre(xg[...], jnp.where(i > 0, pg[...], 0.0), wg, bg, kw)
        av = _conv_pre(xv[...], jnp.where(i > 0, pv[...], 0.0), wv, bv, kw)
        o_ref[...] = (_silu(ag) * av).astype(MXU_DTYPE)

    def cur(off):
        return pl.BlockSpec((tr, tc), lambda j, i: (i, j + off))

    def prev(off):
        return pl.BlockSpec((SUBLANES, tc), lambda j, i: (jnp.maximum(i * rb - 1, 0), j + off))

    def par(rows, off):
        return pl.BlockSpec((rows, tc), lambda j, i: (0, j + off))

    return pl.pallas_call(
        body, name=name, grid=(nf, t // tr),
        in_specs=[cur(0), prev(0), cur(nf), prev(nf), par(kw, 0), par(kw, nf), par(1, 0), par(1, nf)],
        out_specs=pl.BlockSpec((tr, tc), lambda j, i: (i, j)),
        out_shape=jax.ShapeDtypeStruct((t, f), MXU_DTYPE),
        compiler_params=_params(("parallel", "parallel")),
    )(up, up, up, up, w, w, b, b)


def _ffn_conv_bwd(up, w, b, dact, name):
    t, f2 = up.shape
    f = f2 // 2
    kw = w.shape[0]
    tr, tc = _row_tile(t), _pick(f, (256, 128))
    nf, rb, nrow = f // tc, tr // SUBLANES, t // tr

    def body(xg, pg, xv, pv, wg, wv, bg, bv, d_ref, og_ref, ov_ref, gwg_ref, gwv_ref, gbg_ref, gbv_ref, cg, cv):
        i = pl.program_id(1)
        ti = nrow - 1 - i

        @pl.when(i == 0)
        def _():
            for r in (gwg_ref, gwv_ref, gbg_ref, gbv_ref, cg, cv):
                r[...] = jnp.zeros_like(r)

        ag, tg = _conv_taps(xg[...], jnp.where(ti > 0, pg[...], 0.0), wg, bg, kw)
        av, tv = _conv_taps(xv[...], jnp.where(ti > 0, pv[...], 0.0), wv, bv, kw)
        d = d_ref[...]
        s = jax.nn.sigmoid(ag)
        dag = d * av * (s * (1.0 + ag * (1.0 - s)))
        dav = d * (ag * s)
        gbg_ref[...] += jnp.sum(dag, axis=0, keepdims=True)
        gbv_ref[...] += jnp.sum(dav, axis=0, keepdims=True)
        for k in range(kw):
            gwg_ref[k:k + 1, :] += jnp.sum(dag * tg[k], axis=0, keepdims=True)
            gwv_ref[k:k + 1, :] += jnp.sum(dav * tv[k], axis=0, keepdims=True)
        valid = _iota((tr, 1), 0) + ti * tr >= FRONT
        og_ref[...] = jnp.where(valid, _conv_dx(dag, cg[...], wg, kw), 0.0).astype(MXU_DTYPE)
        ov_ref[...] = jnp.where(valid, _conv_dx(dav, cv[...], wv, kw), 0.0).astype(MXU_DTYPE)
        cg[...] = dag[0:SUBLANES]
        cv[...] = dav[0:SUBLANES]

    def cur(off):
        return pl.BlockSpec((tr, tc), lambda j, i: (nrow - 1 - i, j + off))

    def prev(off):
        return pl.BlockSpec((SUBLANES, tc), lambda j, i: (jnp.maximum((nrow - 1 - i) * rb - 1, 0), j + off))

    def par(rows, off):
        return pl.BlockSpec((rows, tc), lambda j, i: (0, j + off))

    acc8 = pl.BlockSpec((SUBLANES, tc), lambda j, i: (0, j))
    acc1 = pl.BlockSpec((1, tc), lambda j, i: (0, j))
    return pl.pallas_call(
        body, name=name, grid=(nf, nrow),
        in_specs=[cur(0), prev(0), cur(nf), prev(nf), par(kw, 0), par(kw, nf), par(1, 0), par(1, nf), cur(0)],
        out_specs=[cur(0), cur(0), acc8, acc8, acc1, acc1],
        out_shape=[jax.ShapeDtypeStruct((t, f), MXU_DTYPE), jax.ShapeDtypeStruct((t, f), MXU_DTYPE),
                   jax.ShapeDtypeStruct((SUBLANES, f), F32), jax.ShapeDtypeStruct((SUBLANES, f), F32),
                   jax.ShapeDtypeStruct((1, f), F32), jax.ShapeDtypeStruct((1, f), F32)],
        scratch_shapes=[pltpu.VMEM((SUBLANES, tc), F32), pltpu.VMEM((SUBLANES, tc), F32)],
        compiler_params=_params(("parallel", "arbitrary")),
    )(up, up, up, up, w, w, b, b, dact)


def _gate_fwd(bs, br, proj, c_gs, c_gr, name):
    t, d = bs.shape
    tr = _row_tile(t)

    def body(bs_ref, br_ref, gs_ref, gr_ref, o_ref):
        o_ref[...] = (jax.nn.sigmoid(gs_ref[...]) * bs_ref[...] + jax.nn.sigmoid(gr_ref[...]) * br_ref[...]).astype(MXU_DTYPE)

    row = pl.BlockSpec((tr, d), lambda i: (i, 0))
    return pl.pallas_call(
        body, name=name, grid=(t // tr,),
        in_specs=[row, row, pl.BlockSpec((tr, d), lambda i: (i, c_gs // d)), pl.BlockSpec((tr, d), lambda i: (i, c_gr // d))],
        out_specs=row, out_shape=jax.ShapeDtypeStruct((t, d), MXU_DTYPE),
        compiler_params=_params(("parallel",)),
    )(bs, br, proj, proj)


def _gate_bwd(dm, bs, br, proj, c_gs, c_gr, name):
    t, d = bs.shape
    tr = _row_tile(t)

    def body(dm_ref, bs_ref, br_ref, gs_ref, gr_ref, dbs_ref, dbr_ref, dgs_ref, dgr_ref):
        g = dm_ref[...]
        ss, sr = jax.nn.sigmoid(gs_ref[...]), jax.nn.sigmoid(gr_ref[...])
        dbs_ref[...] = (g * ss).astype(MXU_DTYPE)
        dbr_ref[...] = (g * sr).astype(MXU_DTYPE)
        dgs_ref[...] = (g * bs_ref[...] * ss * (1.0 - ss)).astype(MXU_DTYPE)
        dgr_ref[...] = (g * br_ref[...] * sr * (1.0 - sr)).astype(MXU_DTYPE)

    row = pl.BlockSpec((tr, d), lambda i: (i, 0))
    out = jax.ShapeDtypeStruct((t, d), MXU_DTYPE)
    return pl.pallas_call(
        body, name=name, grid=(t // tr,),
        in_specs=[row, row, row, pl.BlockSpec((tr, d), lambda i: (i, c_gs // d)), pl.BlockSpec((tr, d), lambda i: (i, c_gr // d))],
        out_specs=[row] * 4, out_shape=[out] * 4,
        compiler_params=_params(("parallel",)),
    )(dm, bs, br, proj, proj)


def _ret_consts(h):
    lg = math.log(1.0 - 2.0 ** (-5.0 - h))
    l = _iota((CHUNK, 1), 0).astype(F32)
    diff = l - _iota((1, CHUNK), 1).astype(F32)
    dm = jnp.exp(jnp.where(diff >= 0, diff * lg, -jnp.inf))
    dmt = jnp.exp(jnp.where(diff <= 0, -diff * lg, -jnp.inf))
    cs = jnp.exp((l + 1.0) * lg)
    kdec = jnp.exp((CHUNK - 1.0 - l) * lg)
    return dm, dmt, cs, kdec, math.exp(CHUNK * lg)


def _ret_fwd(proj, c_q, c_k, c_v, c_g, cos, sin, d, name, comm=None):
    t = proj.shape[0]
    nc = t // CHUNK
    hq, hv = d // RET_HEADS, 2 * d // RET_HEADS
    half = hq // 2
    scale = hq ** -0.5

    def body(q_ref, k_ref, v_ref, g_ref, cos_ref, sin_ref, o_ref, y_ref, qr_ref, kr_ref, st_ref, rs):
        @pl.when(pl.program_id(0) == 0)
        def _():
            rs[...] = jnp.zeros_like(rs)

        co, si = cos_ref[...], sin_ref[...]
        for h in range(RET_HEADS):
            dm, _, cs, kdec, gam = _ret_consts(h)
            q1, q2 = q_ref[:, h * hq:h * hq + half], q_ref[:, h * hq + half:(h + 1) * hq]
            k1, k2 = k_ref[:, h * hq:h * hq + half], k_ref[:, h * hq + half:(h + 1) * hq]
            qr = jnp.concatenate([q1 * co - q2 * si, q2 * co + q1 * si], axis=1)
            kr = jnp.concatenate([k1 * co - k2 * si, k2 * co + k1 * si], axis=1) * scale
            qr_ref[:, h * hq:(h + 1) * hq] = qr.astype(MXU_DTYPE)
            kr_ref[:, h * hq:(h + 1) * hq] = kr.astype(MXU_DTYPE)
            v = v_ref[:, h * hv:(h + 1) * hv]
            r_in = rs[h * hq:(h + 1) * hq, :]
            st_ref[0, h * hq:(h + 1) * hq, :] = r_in.astype(MXU_DTYPE)
            s = _dot_nt(qr, kr) * dm
            o = _dot(s, v) + cs * _dot(qr, r_in)
            rs[h * hq:(h + 1) * hq, :] = gam * r_in + _dot_tn(kr * kdec, v)
            o_ref[:, h * hv:(h + 1) * hv] = o
            on = o * lax.rsqrt(jnp.mean(o * o, axis=1, keepdims=True) + EPS)
            y_ref[:, h * hv:(h + 1) * hv] = (_silu(g_ref[:, h * hv:(h + 1) * hv]) * on).astype(MXU_DTYPE)

    def col(width, c0):
        return pl.BlockSpec((CHUNK, width), lambda i: (i, c0 // width))

    tab = pl.BlockSpec((CHUNK, half), lambda i: (i, 0))
    return _call(
        body, name, (nc,),
        [col(d, c_q), col(d, c_k), col(2 * d, c_v), col(2 * d, c_g), tab, tab],
        [col(2 * d, 0), col(2 * d, 0), col(d, 0), col(d, 0), pl.BlockSpec((1, d, hv), lambda i: (i, 0, 0))],
        [jax.ShapeDtypeStruct((t, 2 * d), F32), jax.ShapeDtypeStruct((t, 2 * d), MXU_DTYPE),
         jax.ShapeDtypeStruct((t, d), MXU_DTYPE), jax.ShapeDtypeStruct((t, d), MXU_DTYPE),
         jax.ShapeDtypeStruct((nc, d, hv), MXU_DTYPE)],
        [pltpu.VMEM((d, hv), F32)], ("arbitrary",), (proj, proj, proj, proj, cos, sin), comm)


def _ret_bwd(dy, proj, c_v, c_g, o, qr, kr, st, cos, sin, d, name, comm=None):
    t = proj.shape[0]
    nc = t // CHUNK
    hq, hv = d // RET_HEADS, 2 * d // RET_HEADS
    half = hq // 2
    scale = hq ** -0.5

    def body(dy_ref, v_ref, g_ref, o_ref, qr_ref, kr_ref, st_ref, cos_ref, sin_ref,
             dq_ref, dk_ref, dv_ref, dg_ref, drs):
        @pl.when(pl.program_id(0) == 0)
        def _():
            drs[...] = jnp.zeros_like(drs)

        co, si = cos_ref[...], sin_ref[...]
        for h in range(RET_HEADS):
            dm, dmt, cs, kdec, gam = _ret_consts(h)
            vs = slice(h * hv, (h + 1) * hv)
            qs = slice(h * hq, (h + 1) * hq)
            o_h = o_ref[:, vs]
            g_h = g_ref[:, vs]
            d_y = dy_ref[:, vs]
            r = lax.rsqrt(jnp.mean(o_h * o_h, axis=1, keepdims=True) + EPS)
            on = o_h * r
            d_on = d_y * _silu(g_h)
            dg_ref[:, vs] = (d_y * on * _dsilu(g_h)).astype(MXU_DTYPE)
            d_o = r * (d_on - on * jnp.mean(d_on * on, axis=1, keepdims=True))
            q_h, k_h, v_h = qr_ref[:, qs], kr_ref[:, qs], v_ref[:, vs]
            r_in = st_ref[0, qs, :]
            dr_n = drs[qs, :]
            csdo = cs * d_o
            ds = _dot_nt(d_o, v_h) * dm
            dst = _dot_nt(v_h, d_o) * dmt
            s_t = _dot_nt(k_h, q_h) * dmt
            dqr = _dot(ds, k_h) + _dot_nt(csdo, r_in)
            dkr = _dot(dst, q_h) + kdec * _dot_nt(v_h, dr_n)
            dv_ref[:, vs] = (_dot(s_t, d_o) + _dot(k_h.astype(F32) * kdec, dr_n)).astype(MXU_DTYPE)
            drs[qs, :] = gam * dr_n + _dot_tn(q_h, csdo)
            a1, a2 = dqr[:, :half], dqr[:, half:]
            dq_ref[:, qs] = jnp.concatenate([a1 * co + a2 * si, a2 * co - a1 * si], axis=1).astype(MXU_DTYPE)
            b1, b2 = dkr[:, :half] * scale, dkr[:, half:] * scale
            dk_ref[:, qs] = jnp.concatenate([b1 * co + b2 * si, b2 * co - b1 * si], axis=1).astype(MXU_DTYPE)

    def col(width, c0=0):
        return pl.BlockSpec((CHUNK, width), lambda i: (nc - 1 - i, c0 // width))

    tab = pl.BlockSpec((CHUNK, half), lambda i: (nc - 1 - i, 0))
    return _call(
        body, name, (nc,),
        [col(2 * d), col(2 * d, c_v), col(2 * d, c_g), col(2 * d), col(d), col(d),
         pl.BlockSpec((1, d, hv), lambda i: (nc - 1 - i, 0, 0)), tab, tab],
        [col(d), col(d), col(2 * d), col(2 * d)],
        [jax.ShapeDtypeStruct((t, d), MXU_DTYPE), jax.ShapeDtypeStruct((t, d), MXU_DTYPE),
         jax.ShapeDtypeStruct((t, 2 * d), MXU_DTYPE), jax.ShapeDtypeStruct((t, 2 * d), MXU_DTYPE)],
        [pltpu.VMEM((d, hv), F32)], ("arbitrary",), (dy, proj, proj, o, qr, kr, st, cos, sin), comm)


def _ssd_small(dtraw_ref, dtb_ref, alog_ref, chunk_idx, nh):
    rows = _iota((CHUNK, 1), 0)
    ok = ((rows >= FRONT) | (chunk_idx > 0)) & (_iota((1, LANES), 1) < nh)
    z = dtraw_ref[...] + dtb_ref[...]
    dt = jnp.where(ok, jax.nn.softplus(z), 0.0)
    sig = jnp.where(ok, jax.nn.sigmoid(z), 0.0)
    a = jnp.where(_iota((1, LANES), 1) < nh, -jnp.exp(alog_ref[...]), 0.0)
    tri = (_iota((CHUNK, CHUNK), 0) >= _iota((CHUNK, CHUNK), 1)).astype(F32)
    acs = _dot01(tri, dt * a, "b", 3)
    return dt, sig, a, acs, acs.T


def _head_expand(g, hpg, gw):
    shift = int(math.log2(SSD_HEAD_DIM))
    return (_iota((LANES, gw), 0) == g * hpg + lax.shift_right_logical(_iota((LANES, gw), 1), shift)).astype(F32)


def _ssd_fwd(xa, proj, c_dt, c_z, dtb, alog, dvec, nw, di, name, comm=None):
    t = xa.shape[0]
    nc = t // CHUNK
    nh = di // SSD_HEAD_DIM
    hpg = nh // SSD_GROUPS
    gw = di // SSD_GROUPS
    n = SSD_STATE
    gn = SSD_GROUPS * n
    hd = SSD_HEAD_DIM

    def body(x_ref, b_ref, c_ref, dtraw_ref, z_ref, dtb_ref, alog_ref, d_ref, nw_ref,
             y_ref, ys_ref, st_ref, hts, xdt_s):
        c = pl.program_id(0)

        @pl.when(c == 0)
        def _():
            hts[...] = jnp.zeros_like(hts)

        dt, _, _, acs, acs_t = _ssd_small(dtraw_ref, dtb_ref, alog_ref, c, nh)
        tri = _iota((CHUNK, CHUNK), 0) >= _iota((CHUNK, CHUNK), 1)
        dvec8 = jnp.broadcast_to(d_ref[...], (SUBLANES, LANES))
        for g in range(SSD_GROUPS):
            gs = slice(g * gw, (g + 1) * gw)
            ns = slice(g * n, (g + 1) * n)
            e_mat = _head_expand(g, hpg, gw)
            ax = _dot01(acs, e_mat, "a", 3)
            dtx = _dot01(dt, e_mat, "a", 3)
            dx = _dot01(dvec8, e_mat, "a", 3)[0:1, :]
            xg, bg, cg = x_ref[:, gs], b_ref[:, ns], c_ref[:, ns]
            xdt = xg * dtx
            xdt_s[...] = xdt.astype(MXU_DTYPE)
            cb = _dot_nt(cg, bg)
            ht = hts[ns, :]
            st_ref[0, ns, :] = ht.astype(MXU_DTYPE)
            y_ref[:, gs] = jnp.exp(ax) * _dot(cg, ht) + dx * xg
            for hh in range(hpg):
                h = g * hpg + hh
                lmat = jnp.exp(jnp.where(tri, acs[:, h:h + 1] - acs_t[h:h + 1, :], -jnp.inf))
                hs = slice(g * gw + hh * hd, g * gw + (hh + 1) * hd)
                y_ref[:, hs] += _dot(cb * lmat, xdt_s[:, hh * hd:(hh + 1) * hd])
            aend = ax[CHUNK - 1:CHUNK, :]
            hts[ns, :] = jnp.exp(aend) * ht + _dot_tn(bg, xdt * jnp.exp(aend - ax))
        for g in range(SSD_GROUPS):
            gs = slice(g * gw, (g + 1) * gw)
            yz = y_ref[:, gs] * _silu(z_ref[:, gs])
            r = lax.rsqrt(jnp.mean(yz * yz, axis=1, keepdims=True) + EPS)
            ys_ref[:, gs] = (yz * r * nw_ref[:, gs]).astype(MXU_DTYPE)

    def col(width, c0, arr_is_xa=False):
        return pl.BlockSpec((CHUNK, width), lambda i: (i, c0 // width))

    vec = pl.BlockSpec((1, LANES), lambda i: (0, 0))
    assert di % gn == 0 and c_dt % LANES == 0 and c_z % di == 0
    return _call(
        body, name, (nc,),
        [col(di, 0), col(gn, di), col(gn, di + gn), col(LANES, c_dt), col(di, c_z), vec, vec, vec,
         pl.BlockSpec((1, di), lambda i: (0, 0))],
        [col(di, 0), col(di, 0), pl.BlockSpec((1, gn, gw), lambda i: (i, 0, 0))],
        [jax.ShapeDtypeStruct((t, di), F32), jax.ShapeDtypeStruct((t, di), MXU_DTYPE),
         jax.ShapeDtypeStruct((nc, gn, gw), MXU_DTYPE)],
        [pltpu.VMEM((gn, gw), F32), pltpu.VMEM((CHUNK, gw), MXU_DTYPE)], ("arbitrary",),
        (xa, xa, xa, proj, proj, dtb, alog, dvec, nw), comm)


def _ssd_bwd(dys, xa, proj, c_dt, c_z, ypre, st, dtb, alog, dvec, nw, di, name, comm=None):
    t = xa.shape[0]
    nc = t // CHUNK
    nh = di // SSD_HEAD_DIM
    hpg = nh // SSD_GROUPS
    gw = di // SSD_GROUPS
    n = SSD_STATE
    gn = SSD_GROUPS * n
    hd = SSD_HEAD_DIM

    def body(dys_ref, x_ref, b_ref, c_ref, dtraw_ref, z_ref, y_ref, st_ref, dtb_ref, alog_ref, d_ref, nw_ref,
             dxa_ref, dz_ref, ddt_ref, gb_ref, ga_ref, gd_ref, gnw_ref, dhts, dy_s, xdt_s, dxdt_s):
        i = pl.program_id(0)
        c = nc - 1 - i

        @pl.when(i == 0)
        def _():
            dhts[...] = jnp.zeros_like(dhts)
            gb_ref[...] = jnp.zeros_like(gb_ref)
            ga_ref[...] = jnp.zeros_like(ga_ref)
            gd_ref[...] = jnp.zeros_like(gd_ref)
            gnw_ref[...] = jnp.zeros_like(gnw_ref)

        dt, sig, a, acs, acs_t = _ssd_small(dtraw_ref, dtb_ref, alog_ref, c, nh)
        tri = _iota((CHUNK, CHUNK), 0) >= _iota((CHUNK, CHUNK), 1)
        triu = _iota((CHUNK, CHUNK), 0) <= _iota((CHUNK, CHUNK), 1)
        lane = _iota((1, LANES), 1)
        rows = _iota((CHUNK, 1), 0)
        head_row = _iota((LANES, 1), 0)
        dvec8 = jnp.broadcast_to(d_ref[...], (SUBLANES, LANES))
        da = jnp.zeros((CHUNK, LANES), F32)
        da_t = jnp.zeros((LANES, CHUNK), F32)
        ddt = jnp.zeros((CHUNK, LANES), F32)
        gd = jnp.zeros((1, LANES), F32)
        for g in range(SSD_GROUPS):
            gs = slice(g * gw, (g + 1) * gw)
            ns = slice(g * n, (g + 1) * n)
            y_g, z_g = y_ref[:, gs], z_ref[:, gs]
            sz = _silu(z_g)
            yz = y_g * sz
            r = lax.rsqrt(jnp.mean(yz * yz, axis=1, keepdims=True) + EPS)
            nrm = yz * r
            dyo = dys_ref[:, gs]
            gnw_ref[:, gs] += jnp.sum(dyo * nrm, axis=0, keepdims=True)
            dn = dyo * nw_ref[:, gs]
            dyz = r * (dn - nrm * jnp.mean(dn * nrm, axis=1, keepdims=True))
            dz_ref[:, gs] = (dyz * y_g * _dsilu(z_g)).astype(MXU_DTYPE)
            dy_g = dyz * sz
            dy_s[...] = dy_g.astype(MXU_DTYPE)
            e_mat = _head_expand(g, hpg, gw)
            ax = _dot01(acs, e_mat, "a", 3)
            dtx = _dot01(dt, e_mat, "a", 3)
            dx = _dot01(dvec8, e_mat, "a", 3)[0:1, :]
            xg, bg, cg = x_ref[:, gs], b_ref[:, ns], c_ref[:, ns]
            xdt = xg * dtx
            xdt_s[...] = xdt.astype(MXU_DTYPE)
            aend = ax[CHUNK - 1:CHUNK, :]
            e = jnp.exp(aend - ax)
            ea = jnp.exp(ax)
            eend = jnp.exp(aend)
            htp = st_ref[0, ns, :].astype(F32)
            dht = dhts[ns, :]
            cb = _dot_nt(cg, bg)
            q = _dot(bg, dht)
            dxdt_s[...] = e * q
            wl = e * q * xdt
            d_b = _dot_nt(e * xdt, dht)
            yi = ea * _dot(cg, htp)
            eady = ea * dy_g
            d_c = _dot_nt(eady, htp)
            t1 = jnp.sum(dht * htp, axis=0, keepdims=True) * eend
            dhts[ns, :] = eend * dht + _dot_tn(cg, eady)
            da = da + _dot01(dy_g * yi - wl, e_mat, "a", 3, _NT)
            tail = jnp.broadcast_to(jnp.sum(wl, axis=0, keepdims=True) + t1, (SUBLANES, gw))
            da_end = _dot01(tail, e_mat, "a", 3, _NT)[0:1, :]
            da = da + jnp.where(rows == CHUNK - 1, da_end, 0.0)
            dcb = jnp.zeros((CHUNK, CHUNK), F32)
            for hh in range(hpg):
                h = g * hpg + hh
                lmat = jnp.exp(jnp.where(tri, acs[:, h:h + 1] - acs_t[h:h + 1, :], -jnp.inf))
                hl = slice(hh * hd, (hh + 1) * hd)
                dy_h, xdt_h = dy_s[:, hl], xdt_s[:, hl]
                dxdt_s[:, hl] += _dot_tn(cb * lmat, dy_h)
                dml = _dot_nt(dy_h, xdt_h) * lmat
                dcb = dcb + dml
                gmat = dml * cb
                da = da + jnp.where(lane == h, jnp.sum(gmat, axis=1, keepdims=True), 0.0)
                da_t = da_t - jnp.where(head_row == h, jnp.sum(gmat, axis=0, keepdims=True), 0.0)
            d_c = d_c + _dot(dcb, bg)
            d_b = d_b + _dot_tn(dcb, cg)
            dxdt = dxdt_s[...]
            dxa_ref[:, gs] = dxdt * dtx + dx * dy_g
            dxa_ref[:, di + g * n:di + (g + 1) * n] = d_b
            dxa_ref[:, di + gn + g * n:di + gn + (g + 1) * n] = d_c
            ddt = ddt + _dot01(dxdt * xg, e_mat, "a", 3, _NT)
            gd8 = jnp.broadcast_to(jnp.sum(dy_g * xg, axis=0, keepdims=True), (SUBLANES, gw))
            gd = gd + _dot01(gd8, e_mat, "a", 3, _NT)[0:1, :]
        da = da + da_t.T
        triu_f = triu.astype(F32)
        ddta = _dot01(triu_f, da, "b", 3)
        ddt = ddt + ddta * a
        draw = ddt * sig
        ddt_ref[...] = draw.astype(MXU_DTYPE)
        gb_ref[...] += jnp.sum(draw, axis=0, keepdims=True)
        ga_ref[...] += jnp.sum(ddta * dt, axis=0, keepdims=True) * a
        gd_ref[...] += gd

    def col(width, c0):
        return pl.BlockSpec((CHUNK, width), lambda i: (nc - 1 - i, c0 // width))

    vec = pl.BlockSpec((1, LANES), lambda i: (0, 0))
    wide = pl.BlockSpec((1, di), lambda i: (0, 0))
    wa = di + 2 * gn
    return _call(
        body, name, (nc,),
        [col(di, 0), col(di, 0), col(gn, di), col(gn, di + gn), col(LANES, c_dt), col(di, c_z), col(di, 0),
         pl.BlockSpec((1, gn, gw), lambda i: (nc - 1 - i, 0, 0)), vec, vec, vec, wide],
        [col(wa, 0), col(di, 0), col(LANES, 0), vec, vec, vec, wide],
        [jax.ShapeDtypeStruct((t, wa), F32), jax.ShapeDtypeStruct((t, di), MXU_DTYPE),
         jax.ShapeDtypeStruct((t, LANES), MXU_DTYPE), jax.ShapeDtypeStruct((1, LANES), F32),
         jax.ShapeDtypeStruct((1, LANES), F32), jax.ShapeDtypeStruct((1, LANES), F32),
         jax.ShapeDtypeStruct((1, di), F32)],
        [pltpu.VMEM((gn, gw), F32), pltpu.VMEM((CHUNK, gw), MXU_DTYPE), pltpu.VMEM((CHUNK, gw), MXU_DTYPE),
         pltpu.VMEM((CHUNK, gw), F32)], ("arbitrary",),
        (dys, xa, xa, xa, proj, proj, ypre, st, dtb, alog, dvec, nw), comm)


def _adam_math(w, g, m, v):
    m2 = ADAM_B1 * m + (1.0 - ADAM_B1) * g
    v2 = ADAM_B2 * v + (1.0 - ADAM_B2) * (g * g)
    m_hat = m2 / (1.0 - ADAM_B1 ** ADAM_STEP)
    v_hat = v2 / (1.0 - ADAM_B2 ** ADAM_STEP)
    delta = -ADAM_LR * (m_hat / (jnp.sqrt(v_hat) + ADAM_EPS) + ADAM_WD * w)
    return delta, m2, v2


def _adam_big(w, g_mine, g_sib, m, v, core, name):
    r, c = w.shape
    h = r // 2
    tr = _pick(h, (128, 64, 32, 16, 8))
    nbh = h // tr

    def body(core_ref, w_ref, a_ref, b_ref, m_ref, v_ref, g_ref, d_ref, m2_ref, v2_ref):
        g = jnp.where(pl.program_id(0) // nbh == core_ref[0], a_ref[...], b_ref[...])
        delta, m2, v2 = _adam_math(w_ref[...], g, m_ref[...], v_ref[...])
        g_ref[...] = g
        d_ref[...] = delta
        m2_ref[...] = m2
        v2_ref[...] = v2

    blk = pl.BlockSpec((tr, c), lambda i, core_ref: (i, 0))
    hblk = pl.BlockSpec((tr, c), lambda i, core_ref: (i % nbh, 0))
    out = jax.ShapeDtypeStruct((r, c), F32)
    return pl.pallas_call(
        body, name=name,
        grid_spec=pltpu.PrefetchScalarGridSpec(num_scalar_prefetch=1, grid=(2 * nbh,),
                                               in_specs=[blk, hblk, hblk, blk, blk], out_specs=[blk] * 4),
        out_shape=[out] * 4, compiler_params=_params(("parallel",)),
    )(core, w, g_mine, g_sib, m, v)


def _pair_sum(g, sib, core, name):
    _, r, c = g.shape
    h = r // 2
    tr = _pick(h, (128, 64, 32, 16))
    nb = h // tr

    def body(core_ref, g_ref, s_ref, o_ref):
        o_ref[...] = (g_ref[...] + s_ref[...]).astype(WIRE_DTYPE)

    return pl.pallas_call(
        body, name=name,
        grid_spec=pltpu.PrefetchScalarGridSpec(
            num_scalar_prefetch=1, grid=(4, nb),
            in_specs=[pl.BlockSpec((1, tr, c), lambda j, i, core_ref: (j, core_ref[0] * nb + i, 0)),
                      pl.BlockSpec((1, tr, c), lambda j, i, core_ref: (j, i, 0))],
            out_specs=pl.BlockSpec((1, tr, c), lambda j, i, core_ref: (j, i, 0))),
        out_shape=jax.ShapeDtypeStruct((4, h, c), WIRE_DTYPE), compiler_params=_params(("parallel", "parallel")),
    )(core, g, sib)


def _sum4(parts, name):
    _, r, c = parts.shape
    tr = _pick(r, (128, 64, 32, 16, 8))

    def body(p_ref, o_ref):
        acc = p_ref[0].astype(F32)
        for j in range(1, 4):
            acc = acc + p_ref[j].astype(F32)
        o_ref[...] = acc

    return pl.pallas_call(
        body, name=name, grid=(r // tr,),
        in_specs=[pl.BlockSpec((4, tr, c), lambda i: (0, i, 0))],
        out_specs=pl.BlockSpec((tr, c), lambda i: (i, 0)),
        out_shape=jax.ShapeDtypeStruct((r, c), F32),
        compiler_params=_params(("parallel",)),
    )(parts)


def _adam_small(w, gparts, m, v, name):
    r = w.shape[0]

    def body(w_ref, g_ref, m_ref, v_ref, go_ref, d_ref, m2_ref, v2_ref):
        g = g_ref[0]
        for j in range(1, 8):
            g = g + g_ref[j]
        delta, m2, v2 = _adam_math(w_ref[...], g, m_ref[...], v_ref[...])
        go_ref[...] = g
        d_ref[...] = delta
        m2_ref[...] = m2
        v2_ref[...] = v2

    out = jax.ShapeDtypeStruct((r, LANES), F32)
    return pl.pallas_call(body, name=name, out_shape=[out] * 4)(w, gparts, m, v)


def _allgather8(v, name):
    def body(src, dst, send, recv, loc):
        x, y, c = lax.axis_index("x"), lax.axis_index("y"), lax.axis_index("c")
        mine = 4 * x + 2 * y + c
        lc = pltpu.make_async_copy(src, dst.at[mine], loc)
        lc.start()
        copies = []
        for k in range(1, 8):
            fx, fy, fc = (k >> 2) & 1, (k >> 1) & 1, k & 1
            peer = (1 - x if fx else x, 1 - y if fy else y, 1 - c if fc else c)
            cp = pltpu.make_async_remote_copy(src_ref=src, dst_ref=dst.at[mine], send_sem=send.at[k - 1],
                                              recv_sem=recv.at[k - 1], device_id=peer, device_id_type=MESH)
            cp.start()
            copies.append(cp)
        for cp in copies:
            cp.wait()
        lc.wait()

    anyspec = pl.BlockSpec(memory_space=pl.ANY)
    return pl.pallas_call(
        body, name=name, in_specs=[anyspec], out_specs=anyspec,
        out_shape=jax.ShapeDtypeStruct((8,) + v.shape, v.dtype),
        scratch_shapes=[pltpu.SemaphoreType.DMA((7,)), pltpu.SemaphoreType.DMA((7,)), pltpu.SemaphoreType.DMA],
        compiler_params=pltpu.CompilerParams(has_side_effects=True),
    )(v)


def _pack(parts):
    flat = jnp.concatenate([p.reshape(-1).astype(F32) for p in parts])
    pad = (-flat.shape[0]) % (32 * LANES)
    return jnp.pad(flat, (0, pad)).reshape(-1, LANES)


def _unpack(slab, shapes):
    flat = slab.reshape(-1)
    out, off = [], 0
    for s in shapes:
        size = int(np.prod(s))
        out.append(flat[off:off + size].reshape(s))
        off += size
    return out


def _pad_lanes(v):
    return jnp.pad(v.reshape(1, -1), ((0, 0), (0, LANES - v.shape[-1])))


def kernel(x, meta_tokens, mix_norm_w, w_in, ssd_conv_w, ssd_conv_b, ssd_dt_bias, ssd_A_log, ssd_D, ssd_norm_w, w_branch_ssd, w_branch_ret, w_out, ffn_norm_w, w_up, ffn_conv_w, ffn_conv_b, w_down, final_norm_w, loss_target, m_meta_tokens, m_mix_norm_w, m_w_in, m_ssd_conv_w, m_ssd_conv_b, m_ssd_dt_bias, m_ssd_A_log, m_ssd_D, m_ssd_norm_w, m_w_branch_ssd, m_w_branch_ret, m_w_out, m_ffn_norm_w, m_w_up, m_ffn_conv_w, m_ffn_conv_b, m_w_down, m_final_norm_w, v_meta_tokens, v_mix_norm_w, v_w_in, v_ssd_conv_w, v_ssd_conv_b, v_ssd_dt_bias, v_ssd_A_log, v_ssd_D, v_ssd_norm_w, v_w_branch_ssd, v_w_branch_ret, v_w_out, v_ffn_norm_w, v_w_up, v_ffn_conv_w, v_ffn_conv_b, v_w_down, v_final_norm_w):
    seq, d = x.shape[1], x.shape[2]
    t = CHUNK + seq
    di = 2 * d
    nh = di // SSD_HEAD_DIM
    gn = SSD_GROUPS * SSD_STATE
    cw = di + 2 * gn
    f = w_down.shape[1] * 4
    chip = 2 * lax.axis_index("x") + lax.axis_index("y")

    order = [("z", di), ("v", di), ("g", di), ("xbc", cw), ("q", d), ("k", d), ("gs", d), ("gr", d), ("dt", LANES)]
    col, acc = {}, 0
    for nm, wd in order:
        col[nm] = acc
        acc += wd
    wp = acc
    ref_order = [("z", di), ("xbc", cw), ("dt", nh), ("q", d), ("k", d), ("v", di), ("g", di), ("gs", d), ("gr", d)]
    ref_off, acc = {}, 0
    for nm, wd in ref_order:
        ref_off[nm] = (acc, wd)
        acc += wd
    in_dim = acc

    core = lax.axis_index("c").astype(jnp.int32).reshape(1)
    small_shapes = [meta_tokens.shape, ssd_conv_w.shape[1:], ffn_conv_w.shape[1:]]
    small_local = _pack([meta_tokens, ssd_conv_w[0], ffn_conv_w[0]])
    first_half = _run_comm(_gather_ici([w_in[0].astype(WIRE_DTYPE), small_local]), "gather_w_in_ici")
    g_in, g_small = _run_comm(_gather_d2d(first_half), "gather_w_in_d2d")
    rest_local = [a[0].astype(WIRE_DTYPE) for a in (w_branch_ssd, w_branch_ret, w_out, w_up, w_down)]
    w_in_full = jnp.moveaxis(g_in, 0, 1).reshape(d, in_dim)
    pieces = []
    for nm, wd in order:
        o, rw = ref_off[nm]
        p = w_in_full[:, o:o + rw]
        if rw < wd:
            p = jnp.pad(p, ((0, 0), (0, wd - rw)))
        pieces.append(p)
    w_p = jnp.concatenate(pieces, axis=1)
    smalls = [_unpack(g_small[j], small_shapes) for j in range(4)]
    meta_full = jnp.concatenate([s[0] for s in smalls], axis=1)
    scw = jnp.concatenate([s[1] for s in smalls], axis=1)
    fcw = jnp.concatenate([s[2] for s in smalls], axis=1)
    scb, fcb = ssd_conv_b, ffn_conv_b
    dtb, alog, dvec = _pad_lanes(ssd_dt_bias), _pad_lanes(ssd_A_log), _pad_lanes(ssd_D)
    fin_w = final_norm_w.reshape(1, d)

    hq = d // RET_HEADS
    pos = jnp.arange(t, dtype=F32) - FRONT
    inv_freq = ROPE_BASE ** (-jnp.linspace(0.0, 1.0, hq // 2, dtype=F32))
    ang = pos[:, None] * inv_freq[None, :]
    cos, sin = jnp.cos(ang), jnp.sin(ang)

    h0 = jnp.concatenate([jnp.zeros((FRONT, d), F32), meta_full, x[0]], axis=0)
    tm = _row_tile(t)
    u1 = _rms_fwd(h0, mix_norm_w, "rms1_fwd")
    proj = _mm(u1, w_p, "nn", F32, "proj", tm, _pick(wp, (1920, 1536, 1280, 1024, 896, 768, 640, 512, 384, 256, 128)), d)
    xa = _ssd_conv_fwd(proj, col["xbc"], cw, scw, scb, "ssd_conv_fwd")
    res = _ssd_fwd(xa, proj, col["dt"], col["z"], dtb, alog, dvec, ssd_norm_w, di, "ssd_fwd", comm=_gather_ici(rest_local))
    (ypre, yssd, st_ssd), rest_half = res[:3], res[3:]
    res = _ret_fwd(proj, col["q"], col["k"], col["v"], col["g"], cos, sin, d, "ret_fwd", comm=_gather_d2d(rest_half))
    (o_ret, yret, qr, kr, st_ret), (g_bs, g_br, g_out, g_up, g_down) = res[:5], res[5:]
    w_bs = g_bs.reshape(di, d)
    w_br = g_br.reshape(di, d)
    w_o = g_out.reshape(d, d)
    w_u = jnp.moveaxis(g_up, 0, 1).reshape(d, 2 * f)
    w_d = g_down.reshape(f, d)
    tn_d = _pick(d, (1024, 512, 256, 128))
    bs = _mm(yssd, w_bs, "nn", F32, "branch_ssd", tm, tn_d, _pick(di, (1024, 512, 256)))
    br = _mm(yret, w_br, "nn", F32, "branch_ret", tm, tn_d, _pick(di, (1024, 512, 256)))
    merged = _gate_fwd(bs, br, proj, col["gs"], col["gr"], "gate_fwd")
    h1 = _mm(merged, w_o, "nn", F32, "out_proj", tm, tn_d, d, res=h0)
    u2 = _rms_fwd(h1, ffn_norm_w, "rms2_fwd")
    tn_f = _pick(2 * f, (1408, 1024, 768, 512, 256, 128))
    up = _mm(u2, w_u, "nn", F32, "up_proj", tm, tn_f, d)
    act = _ffn_conv_fwd(up, fcw, fcb, "ffn_conv_fwd")
    tk_f = _pick(f, (1408, 768, 704, 512, 256, 128))
    h2 = _mm(act, w_d, "nn", F32, "down_proj", tm, tn_d, tk_f, res=h1)
    loss8, d_h2, g_fin = _loss_bwd(h2, fin_w, loss_target[0], "loss_head")

    d_act = _mm(d_h2, w_d, "nt", F32, "d_act", tm, tk_f, d)
    g_wd = _mm(act, d_h2, "tn", F32, "g_w_down", tk_f, tn_d, tm)
    d_upg, d_upv, g_fcwg, g_fcwv, g_fcbg, g_fcbv = _ffn_conv_bwd(up, fcw, fcb, d_act, "ffn_conv_bwd")
    g_fcw = jnp.concatenate([g_fcwg, g_fcwv], axis=1)
    g_fcb = jnp.concatenate([g_fcbg, g_fcbv], axis=1)
    d_u2 = _mm(d_upg, w_u[:, :f], "nt", F32, "d_u2_gate", tm, tn_d, tk_f)
    d_u2 = _mm(d_upv, w_u[:, f:], "nt", F32, "d_u2_value", tm, tn_d, tk_f, res=d_u2)
    g_wu = jnp.concatenate([_mm(u2, d_upg, "tn", F32, "g_w_up_gate", tn_d, tk_f, tm),
                            _mm(u2, d_upv, "tn", F32, "g_w_up_value", tn_d, tk_f, tm)], axis=1)
    d_h1, g_ffnw = _rms_bwd(h1, ffn_norm_w, d_u2, d_h2, "rms2_bwd")
    d_merged = _mm(d_h1, w_o, "nt", F32, "d_merged", tm, tn_d, d)
    g_wo = _mm(merged, d_h1, "tn", F32, "g_w_out", tn_d, tn_d, tm)
    d_bs, d_br, d_gs, d_gr = _gate_bwd(d_merged, bs, br, proj, col["gs"], col["gr"], "gate_bwd")
    tk_i = _pick(di, (1024, 512, 256))
    d_yssd = _mm(d_bs, w_bs, "nt", F32, "d_y_ssd", tm, tk_i, d)
    g_wbs = _mm(yssd, d_bs, "tn", F32, "g_w_branch_ssd", tk_i, tn_d, tm)
    d_yret = _mm(d_br, w_br, "nt", F32, "d_y_ret", tm, tk_i, d)
    g_wbr = _mm(yret, d_br, "tn", F32, "g_w_branch_ret", tk_i, tn_d, tm)

    early_names = ["w_branch_ssd", "w_branch_ret", "w_out", "w_up", "w_down"]
    early = [g_wbs.reshape(4, di // 4, d), g_wbr.reshape(4, di // 4, d), g_wo.reshape(4, d // 4, d),
             jnp.moveaxis(g_wu.reshape(d, 4, 2 * f // 4), 1, 0), g_wd.reshape(4, f // 4, d)]
    res = _ret_bwd(d_yret, proj, col["v"], col["g"], o_ret, qr, kr, st_ret, cos, sin, d, "ret_bwd", comm=_scatter_d2d(early))
    (dq, dk, dv, dg), early_sib = res[:4], res[4:]
    early_pair = [_pair_sum(g_, s_, core, "pair_" + nm) for g_, s_, nm in zip(early, early_sib, early_names)]
    res = _ssd_bwd(d_yssd, xa, proj, col["dt"], col["z"], ypre, st_ssd, dtb, alog, dvec, ssd_norm_w, di, "ssd_bwd",
                   comm=_scatter_ici(early_pair))
    (d_xa, dz, ddt, g_dtb, g_alog, g_dvec, g_snw), early_recv = res[:7], res[7:]
    early_mine = [_sum4(p, "sum4_" + nm) for p, nm in zip(early_recv, early_names)]
    res = _ssd_conv_bwd(proj, col["xbc"], cw, scw, scb, d_xa, "ssd_conv_bwd", comm=_sibling_swap(early_mine))
    (d_xbc, g_scw, g_scb), early_other = res[:3], res[3:]
    d_proj = jnp.concatenate([dz, dv, dg, d_xbc, dq, dk, d_gs, d_gr, ddt], axis=1)
    tn_p = _pick(wp, (1920, 1536, 1280, 1024, 896, 768, 640, 512, 384, 256, 128))

    g_wp = _mm(u1, d_proj, "tn", F32, "g_w_in", tn_d, tn_p, tm)
    g_in_ref = jnp.concatenate([g_wp[:, col[nm]:col[nm] + rw] for nm, rw in ref_order], axis=1)
    sc_in = jnp.moveaxis(g_in_ref.reshape(d, 4, in_dim // 4), 1, 0)
    in_sib = _run_comm(_scatter_d2d([sc_in]), "scatter_w_in_d2d")[0]
    in_pair = _pair_sum(sc_in, in_sib, core, "pair_w_in")
    d_u1, (in_recv,) = _mm(d_proj, w_p, "nt", F32, "d_u1", tm, tn_d, tn_p, comm=_scatter_ici([in_pair]))
    in_mine = _sum4(in_recv, "sum4_w_in")
    d_h0, g_mixw, in_other = _rms_bwd(h0, mix_norm_w, d_u1, d_h1, "rms1_bwd", comm=_sibling_swap([in_mine]))
    grad_x = d_h0[CHUNK:][None]
    g_meta = d_h0[FRONT:CHUNK]

    names = ["w_in"] + early_names
    mine_half = [in_mine] + early_mine
    other_half = [in_other] + list(early_other)
    big_w = [w_in, w_branch_ssd, w_branch_ret, w_out, w_up, w_down]
    big_m = [m_w_in, m_w_branch_ssd, m_w_branch_ret, m_w_out, m_w_up, m_w_down]
    big_v = [v_w_in, v_w_branch_ssd, v_w_branch_ret, v_w_out, v_w_up, v_w_down]
    big_out = {}
    for nm, w_, p_, s_, m_, v_ in zip(names, big_w, mine_half, other_half, big_m, big_v):
        res = _adam_big(w_[0], p_, s_, m_[0], v_[0], core, "adam_" + nm)
        big_out[nm] = [r[None] for r in res]

    kws, kwf = ssd_conv_w.shape[1], ffn_conv_w.shape[1]
    small_grads = [g_meta, g_mixw, g_scw[:kws], g_scb, g_dtb[:, :nh], g_alog[:, :nh], g_dvec[:, :nh], g_snw, g_ffnw,
                   g_fcw[:kwf], g_fcb, g_fin, loss8[0:1, 0:1]]
    sg_shapes = [g.shape for g in small_grads]
    gparts = _allgather8(_pack(small_grads), "gather_small_grads")

    def own_cols(a, width):
        return lax.dynamic_slice_in_dim(a, chip * width, width, axis=1)

    def widen(a, width_full):
        z = jnp.zeros(a.shape[:-1] + (width_full,), F32)
        return lax.dynamic_update_slice_in_dim(z, a, chip * a.shape[-1], axis=a.ndim - 1)

    def small_slab(meta_, mix_, scw_, scb_, dtb_, alog_, d_, snw_, ffnw_, fcw_, fcb_, fin_):
        return _pack([widen(meta_, d), mix_, widen(scw_[0], cw), scb_, dtb_, alog_, d_, snw_, ffnw_, widen(fcw_[0], 2 * f),
                      fcb_, fin_.reshape(1, d), jnp.zeros((1, 1), F32)])

    w_slab = small_slab(meta_tokens, mix_norm_w, ssd_conv_w, ssd_conv_b, ssd_dt_bias, ssd_A_log, ssd_D, ssd_norm_w,
                        ffn_norm_w, ffn_conv_w, ffn_conv_b, final_norm_w)
    m_slab = small_slab(m_meta_tokens, m_mix_norm_w, m_ssd_conv_w, m_ssd_conv_b, m_ssd_dt_bias, m_ssd_A_log, m_ssd_D,
                        m_ssd_norm_w, m_ffn_norm_w, m_ffn_conv_w, m_ffn_conv_b, m_final_norm_w)
    v_slab = small_slab(v_meta_tokens, v_mix_norm_w, v_ssd_conv_w, v_ssd_conv_b, v_ssd_dt_bias, v_ssd_A_log, v_ssd_D,
                        v_ssd_norm_w, v_ffn_norm_w, v_ffn_conv_w, v_ffn_conv_b, v_final_norm_w)
    small_res = [_unpack(s, sg_shapes) for s in _adam_small(w_slab, gparts, m_slab, v_slab, "adam_small")]
    loss = small_res[0][12].reshape(())

    def small_outputs(vals):
        meta_, mix_, scw_, scb_, dtb_, alog_, d_, snw_, ffnw_, fcw_, fcb_, fin_ = vals[:12]
        return {
            "meta_tokens": own_cols(meta_, d // 4), "mix_norm_w": mix_, "ssd_conv_w": own_cols(scw_, cw // 4)[None],
            "ssd_conv_b": scb_, "ssd_dt_bias": dtb_, "ssd_A_log": alog_, "ssd_D": d_, "ssd_norm_w": snw_,
            "ffn_norm_w": ffnw_, "ffn_conv_w": own_cols(fcw_, 2 * f // 4)[None], "ffn_conv_b": fcb_,
            "final_norm_w": fin_.reshape(d),
        }

    weights = ["meta_tokens", "mix_norm_w", "w_in", "ssd_conv_w", "ssd_conv_b", "ssd_dt_bias", "ssd_A_log", "ssd_D",
               "ssd_norm_w", "w_branch_ssd", "w_branch_ret", "w_out", "ffn_norm_w", "w_up", "ffn_conv_w", "ffn_conv_b",
               "w_down", "final_norm_w"]
    outs = [loss, grad_x]
    for kind in range(4):
        so = small_outputs(small_res[kind])
        for nm in weights:
            outs.append(big_out[nm][kind] if nm in big_out else so[nm])
    return tuple(outs)
```

```python
import functools
import math

import jax
import jax.numpy as jnp
import numpy as np
from jax import lax
from jax.experimental import pallas as pl
from jax.experimental.pallas import tpu as pltpu

F32 = jnp.float32
BF16 = jnp.bfloat16
MXU_DTYPE = BF16
WIRE_DTYPE = BF16

N_META = 16
CHUNK = 128
FRONT = CHUNK - N_META
EPS = 1e-6
SSD_HEAD_DIM = 64
SSD_GROUPS = 4
SSD_STATE = 128
SSD_CONV = 4
RET_HEADS = 4
ROPE_BASE = 10000.0
FFN_CONV = 3
LANES = 128
SUBLANES = 8
VMEM_LIMIT = 56 * 1024 * 1024

ADAM_LR = 0.001
ADAM_B1 = 0.9
ADAM_B2 = 0.999
ADAM_EPS = 1e-08
ADAM_WD = 0.01
ADAM_STEP = 10
MESH = pl.DeviceIdType.MESH


def _params(sem=None, vmem=VMEM_LIMIT):
    return pltpu.CompilerParams(dimension_semantics=sem, vmem_limit_bytes=vmem)


def _pick(n, cands):
    for c in cands:
        if n % c == 0:
            return c
    return n


def _silu(x):
    return x * jax.nn.sigmoid(x)


def _dsilu(x):
    s = jax.nn.sigmoid(x)
    return s * (1.0 + x * (1.0 - s))


def _dot(a, b, dims=(((1,), (0,)), ((), ()))):
    return lax.dot_general(a.astype(MXU_DTYPE), b.astype(MXU_DTYPE), dims, preferred_element_type=F32)


def _dot_nt(a, b):
    return _dot(a, b, (((1,), (1,)), ((), ())))


def _dot_tn(a, b):
    return _dot(a, b, (((0,), (0,)), ((), ())))


def _dot01(a, b, split, npass, dims=(((1,), (0,)), ((), ()))):
    rest = (a if split == "a" else b).astype(F32)
    fixed = (b if split == "a" else a).astype(BF16)
    acc = None
    for p in range(npass):
        piece = rest.astype(BF16)
        ops = (piece, fixed) if split == "a" else (fixed, piece)
        term = lax.dot_general(ops[0], ops[1], dims, preferred_element_type=F32)
        acc = term if acc is None else acc + term
        if p + 1 < npass:
            rest = rest - piece.astype(F32)
    return acc


_NT = (((1,), (1,)), ((), ()))


def _iota(shape, dim):
    return lax.broadcasted_iota(jnp.int32, shape, dim)


def _shift_down(cur, prev8, k):
    if k == 0:
        return cur
    rolled = pltpu.roll(cur, k, 0)
    i8 = _iota((SUBLANES, cur.shape[1]), 0)
    head = jnp.where(i8 < k, pltpu.roll(prev8, k, 0), rolled[0:SUBLANES])
    return jnp.concatenate([head, rolled[SUBLANES:]], axis=0)


def _shift_up(cur, next8, k):
    if k == 0:
        return cur
    n = cur.shape[0]
    rolled = pltpu.roll(cur, n - k, 0)
    i8 = _iota((SUBLANES, cur.shape[1]), 0)
    tail = jnp.where(i8 >= SUBLANES - k, pltpu.roll(next8, SUBLANES - k, 0), rolled[n - SUBLANES:])
    return jnp.concatenate([rolled[:n - SUBLANES], tail], axis=0)


class _Comm:
    def __init__(self, ins, outs, nsem, make, in_place=False):
        self.ins, self.outs, self.nsem, self.make = list(ins), list(outs), nsem, make
        self.in_place = in_place


def _place():
    x, y, c = lax.axis_index("x"), lax.axis_index("y"), lax.axis_index("c")
    return x, y, c, 2 * x + y, [(1 - x, y), (x, 1 - y), (1 - x, 1 - y)]


def _call(body, name, grid, in_specs, out_specs, out_shape, scratch, sem, args, comm=None):
    if comm is None:
        return pl.pallas_call(body, name=name, grid=grid, in_specs=in_specs, out_specs=out_specs, out_shape=out_shape,
                              scratch_shapes=scratch, compiler_params=_params(sem))(*args)
    n_in, n_out, n_scr = len(in_specs), len(out_specs), len(scratch)
    ci, co = len(comm.ins), len(comm.outs)

    def wrapped(*refs):
        ins, refs = refs[:n_in], refs[n_in:]
        cins, refs = refs[:ci], refs[ci:]
        outs, refs = refs[:n_out], refs[n_out:]
        couts, refs = refs[:co], refs[co:]
        scr, sems = refs[:n_scr], refs[n_scr:]
        first = functools.reduce(jnp.logical_and, [pl.program_id(a) == 0 for a in range(len(grid))])
        last = functools.reduce(jnp.logical_and, [pl.program_id(a) == grid[a] - 1 for a in range(len(grid))])

        @pl.when(first)
        def _():
            for cp in comm.make(cins, couts, *sems):
                cp.start()

        body(*ins, *outs, *scr)

        @pl.when(last)
        def _():
            for cp in comm.make(cins, couts, *sems):
                cp.wait()

    anyspec = pl.BlockSpec(memory_space=pl.ANY)
    dma = pltpu.SemaphoreType.DMA((comm.nsem,))
    aliases = {n_in + i: n_out + i for i in range(ci)} if comm.in_place else {}
    return pl.pallas_call(
        wrapped, name=name, grid=grid, in_specs=list(in_specs) + [anyspec] * ci,
        out_specs=list(out_specs) + [anyspec] * co, out_shape=list(out_shape) + comm.outs,
        scratch_shapes=list(scratch) + [dma, dma, dma], input_output_aliases=aliases,
        compiler_params=_params(("arbitrary",) * len(grid)))(*args, *comm.ins)


def _run_comm(comm, name):
    ci, co = len(comm.ins), len(comm.outs)

    def body(*refs):
        cins, couts, sems = refs[:ci], refs[ci:ci + co], refs[ci + co:]
        for cp in comm.make(cins, couts, *sems):
            cp.start()
        for cp in comm.make(cins, couts, *sems):
            cp.wait()

    anyspec = pl.BlockSpec(memory_space=pl.ANY)
    dma = pltpu.SemaphoreType.DMA((comm.nsem,))
    aliases = {i: i for i in range(ci)} if comm.in_place else {}
    return pl.pallas_call(body, name=name, in_specs=[anyspec] * ci, out_specs=[anyspec] * co, out_shape=comm.outs,
                          scratch_shapes=[dma, dma, dma], input_output_aliases=aliases)(*comm.ins)


def _half_rows(c, rows):
    h = rows // 2
    return pl.ds(pl.multiple_of(c * h, 16), h)


def _gather_ici(arrays):
    for a in arrays:
        assert a.shape[0] % 32 == 0, a.shape

    def make(ins, outs, send, recv, loc):
        x, y, c, mine, peers = _place()
        cps = []
        for i, a in enumerate(arrays):
            half = _half_rows(c, a.shape[0])
            for k, (px, py) in enumerate(peers):
                cps.append(pltpu.make_async_remote_copy(
                    src_ref=ins[i].at[half], dst_ref=outs[i].at[mine, half], send_sem=send.at[3 * i + k],
                    recv_sem=recv.at[3 * i + k], device_id=(px, py, c), device_id_type=MESH))
        return cps

    outs = [jax.ShapeDtypeStruct((4,) + a.shape, a.dtype) for a in arrays]
    return _Comm(arrays, outs, 3 * len(arrays), make)


def _gather_d2d(bufs):
    def make(ins, outs, send, recv, loc):
        x, y, c, mine, peers = _place()
        cps = []
        for i, a in enumerate(bufs):
            half = _half_rows(c, a.shape[1])
            for k, (px, py) in enumerate(peers):
                mine_half = outs[i].at[2 * px + py, half]
                cps.append(pltpu.make_async_remote_copy(
                    src_ref=mine_half, dst_ref=mine_half, send_sem=send.at[3 * i + k], recv_sem=recv.at[3 * i + k],
                    device_id=(x, y, 1 - c), device_id_type=MESH))
        return cps

    outs = [jax.ShapeDtypeStruct(a.shape, a.dtype) for a in bufs]
    return _Comm(bufs, outs, 3 * len(bufs), make, in_place=True)


def _with_own(gathered, own, chip):
    return lax.dynamic_update_index_in_dim(gathered, own, chip, 0)


def _scatter_d2d(grads):
    for a in grads:
        assert a.shape[1] % 32 == 0, a.shape

    def make(ins, outs, send, recv, loc):
        x, y, c, mine, peers = _place()
        cps = []
        for i, a in enumerate(grads):
            other = _half_rows(1 - c, a.shape[1])
            cps.append(pltpu.make_async_remote_copy(
                src_ref=ins[i].at[:, other], dst_ref=outs[i], send_sem=send.at[i], recv_sem=recv.at[i],
                device_id=(x, y, 1 - c), device_id_type=MESH))
        return cps

    outs = [jax.ShapeDtypeStruct((4, a.shape[1] // 2, a.shape[2]), a.dtype) for a in grads]
    return _Comm(grads, outs, len(grads), make)


def _scatter_ici(parts):
    def make(ins, outs, send, recv, loc):
        x, y, c, mine, peers = _place()
        cps = []
        for i in range(len(parts)):
            cps.append(pltpu.make_async_copy(ins[i].at[mine], outs[i].at[mine], loc.at[i]))
            for k, (px, py) in enumerate(peers):
                cps.append(pltpu.make_async_remote_copy(
                    src_ref=ins[i].at[2 * px + py], dst_ref=outs[i].at[mine], send_sem=send.at[3 * i + k],
                    recv_sem=recv.at[3 * i + k], device_id=(px, py, c), device_id_type=MESH))
        return cps

    outs = [jax.ShapeDtypeStruct(a.shape, a.dtype) for a in parts]
    return _Comm(parts, outs, 3 * len(parts), make)


def _sibling_swap(arrays):
    def make(ins, outs, send, recv, loc):
        x, y, c, mine, peers = _place()
        return [pltpu.make_async_remote_copy(src_ref=ins[i], dst_ref=outs[i], send_sem=send.at[i], recv_sem=recv.at[i],
                                             device_id=(x, y, 1 - c), device_id_type=MESH) for i in range(len(arrays))]

    outs = [jax.ShapeDtypeStruct(a.shape, a.dtype) for a in arrays]
    return _Comm(arrays, outs, len(arrays), make)


def _mm(a, b, mode, out_dtype, name, tm, tn, tk, res=None, comm=None):
    if mode == "nn":
        (m, kd), n = a.shape, b.shape[1]
        a_spec = pl.BlockSpec((tm, tk), lambda i, j, k: (i, k))
        b_spec = pl.BlockSpec((tk, tn), lambda i, j, k: (k, j))
        dims = (((1,), (0,)), ((), ()))
    elif mode == "nt":
        (m, kd), n = a.shape, b.shape[0]
        a_spec = pl.BlockSpec((tm, tk), lambda i, j, k: (i, k))
        b_spec = pl.BlockSpec((tn, tk), lambda i, j, k: (j, k))
        dims = (((1,), (1,)), ((), ()))
    else:
        (kd, m), n = a.shape, b.shape[1]
        a_spec = pl.BlockSpec((tk, tm), lambda i, j, k: (k, i))
        b_spec = pl.BlockSpec((tk, tn), lambda i, j, k: (k, j))
        dims = (((0,), (0,)), ((), ()))
    assert m % tm == 0 and n % tn == 0 and kd % tk == 0, (name, m, n, kd, tm, tn, tk)
    nk = kd // tk
    has_res = res is not None

    def body(*refs):
        if has_res:
            a_ref, b_ref, r_ref, o_ref, acc = refs
        else:
            a_ref, b_ref, o_ref, acc = refs
        k = pl.program_id(2)

        @pl.when(k == 0)
        def _():
            acc[...] = jnp.zeros_like(acc)

        acc[...] += _dot(a_ref[...], b_ref[...], dims)

        @pl.when(k == nk - 1)
        def _():
            r = acc[...]
            if has_res:
                r = r + r_ref[...].astype(F32)
            o_ref[...] = r.astype(out_dtype)

    in_specs = [a_spec, b_spec]
    args = [a, b]
    if has_res:
        in_specs.append(pl.BlockSpec((tm, tn), lambda i, j, k: (i, j)))
        args.append(res)
    res = _call(body, name, (m // tm, n // tn, nk), in_specs, [pl.BlockSpec((tm, tn), lambda i, j, k: (i, j))],
                [jax.ShapeDtypeStruct((m, n), out_dtype)], [pltpu.VMEM((tm, tn), F32)],
                ("parallel", "parallel", "arbitrary"), args, comm)
    return res[0] if comm is None else (res[0], res[1:])


def _rms_fwd(h, w, name):
    t, d = h.shape
    tr = _pick(t, (640, 512, 384, 256, 128))

    def body(h_ref, w_ref, u_ref):
        x = h_ref[...]
        r = lax.rsqrt(jnp.mean(x * x, axis=1, keepdims=True) + EPS)
        u_ref[...] = (x * r * w_ref[...]).astype(MXU_DTYPE)

    return pl.pallas_call(
        body, name=name, grid=(t // tr,),
        in_specs=[pl.BlockSpec((tr, d), lambda i: (i, 0)), pl.BlockSpec((1, d), lambda i: (0, 0))],
        out_specs=pl.BlockSpec((tr, d), lambda i: (i, 0)),
        out_shape=jax.ShapeDtypeStruct((t, d), MXU_DTYPE),
        compiler_params=_params(("parallel",)),
    )(h, w)


def _rms_bwd(h, w, du, res, name, comm=None):
    t, d = h.shape
    tr = _pick(t, (640, 512, 384, 256, 128))

    def body(h_ref, w_ref, du_ref, res_ref, dh_ref, gw_ref):
        @pl.when(pl.program_id(0) == 0)
        def _():
            gw_ref[...] = jnp.zeros_like(gw_ref)

        x = h_ref[...]
        r = lax.rsqrt(jnp.mean(x * x, axis=1, keepdims=True) + EPS)
        xhat = x * r
        dy = du_ref[...].astype(F32)
        dxh = dy * w_ref[...]
        dh = r * (dxh - xhat * jnp.mean(dxh * xhat, axis=1, keepdims=True))
        dh_ref[...] = dh + res_ref[...]
        gw_ref[...] += jnp.sum(dy * xhat, axis=0, keepdims=True)

    row = pl.BlockSpec((tr, d), lambda i: (i, 0))
    vec = pl.BlockSpec((1, d), lambda i: (0, 0))
    return _call(body, name, (t // tr,), [row, vec, row, row], [row, vec],
                 [jax.ShapeDtypeStruct((t, d), F32), jax.ShapeDtypeStruct((1, d), F32)], [], ("arbitrary",),
                 (h, w, du, res), comm)


def _loss_bwd(h2, w, target, name):
    t, d = h2.shape
    nc = t // CHUNK

    def body(h_ref, w_ref, tg_ref, loss_ref, dh_ref, gw_ref):
        i = pl.program_id(0)

        @pl.when(i == 0)
        def _():
            gw_ref[...] = jnp.zeros_like(gw_ref)
            loss_ref[...] = jnp.zeros_like(loss_ref)
            dh_ref[...] = jnp.zeros_like(dh_ref)

        @pl.when(i > 0)
        def _():
            x = h_ref[...]
            r = lax.rsqrt(jnp.mean(x * x, axis=1, keepdims=True) + EPS)
            xhat = x * r
            diff = xhat * w_ref[...] - tg_ref[...]
            loss_ref[...] += 0.5 * jnp.sum(jnp.sum(diff * diff, axis=1, keepdims=True), axis=0, keepdims=True) / d
            dy = diff / d
            dxh = dy * w_ref[...]
            dh_ref[...] = r * (dxh - xhat * jnp.mean(dxh * xhat, axis=1, keepdims=True))
            gw_ref[...] += jnp.sum(dy * xhat, axis=0, keepdims=True)

    row = pl.BlockSpec((CHUNK, d), lambda i: (i, 0))
    vec = pl.BlockSpec((1, d), lambda i: (0, 0))
    return pl.pallas_call(
        body, name=name, grid=(nc,),
        in_specs=[row, vec, pl.BlockSpec((CHUNK, d), lambda i: (jnp.maximum(i - 1, 0), 0))],
        out_specs=[pl.BlockSpec((SUBLANES, LANES), lambda i: (0, 0)), row, vec],
        out_shape=[jax.ShapeDtypeStruct((SUBLANES, LANES), F32), jax.ShapeDtypeStruct((t, d), F32),
                   jax.ShapeDtypeStruct((1, d), F32)],
        compiler_params=_params(("arbitrary",)),
    )(h2, w, target)


def _conv_taps(cur, prev8, w_ref, b_ref, kw):
    taps = [_shift_down(cur, prev8, kw - 1 - k) for k in range(kw)]
    y = b_ref[...] + taps[kw - 1] * w_ref[kw - 1:kw, :]
    for k in range(kw - 1):
        y = y + taps[k] * w_ref[k:k + 1, :]
    return y, taps


def _conv_pre(cur, prev8, w_ref, b_ref, kw):
    return _conv_taps(cur, prev8, w_ref, b_ref, kw)[0]


def _conv_dx(dpre, next8, w_ref, kw):
    acc = dpre * w_ref[kw - 1:kw, :]
    for k in range(kw - 1):
        acc = acc + _shift_up(dpre, next8, kw - 1 - k) * w_ref[k:k + 1, :]
    return acc


def _row_tile(t):
    return _pick(t, (640, 512, 384, 256, 128))


def _ssd_conv_fwd(proj, col0, width, w, b, name):
    t = proj.shape[0]
    kw = w.shape[0]
    tr, tc = _row_tile(t), _pick(width, (512, 256, 128))
    c0, rb = col0 // tc, tr // SUBLANES
    assert col0 % tc == 0

    def body(x_ref, p_ref, w_ref, b_ref, o_ref):
        i = pl.program_id(1)
        prev8 = jnp.where(i > 0, p_ref[...], 0.0)
        pre = _conv_pre(x_ref[...], prev8, w_ref, b_ref, kw)
        rows = _iota((tr, 1), 0) + i * tr
        o_ref[...] = jnp.where(rows >= FRONT, _silu(pre), 0.0)

    return pl.pallas_call(
        body, name=name, grid=(width // tc, t // tr),
        in_specs=[pl.BlockSpec((tr, tc), lambda j, i: (i, c0 + j)),
                  pl.BlockSpec((SUBLANES, tc), lambda j, i: (jnp.maximum(i * rb - 1, 0), c0 + j)),
                  pl.BlockSpec((kw, tc), lambda j, i: (0, j)),
                  pl.BlockSpec((1, tc), lambda j, i: (0, j))],
        out_specs=pl.BlockSpec((tr, tc), lambda j, i: (i, j)),
        out_shape=jax.ShapeDtypeStruct((t, width), F32),
        compiler_params=_params(("parallel", "parallel")),
    )(proj, proj, w, b)


def _ssd_conv_bwd(proj, col0, width, w, b, dact, name, comm=None):
    t = proj.shape[0]
    kw = w.shape[0]
    tr, tc = _row_tile(t), _pick(width, (512, 256, 128))
    c0, rb, nrow = col0 // tc, tr // SUBLANES, t // tr

    def body(x_ref, p_ref, w_ref, b_ref, d_ref, o_ref, gw_ref, gb_ref, carry):
        i = pl.program_id(1)
        ti = nrow - 1 - i

        @pl.when(i == 0)
        def _():
            gw_ref[...] = jnp.zeros_like(gw_ref)
            gb_ref[...] = jnp.zeros_like(gb_ref)
            carry[...] = jnp.zeros_like(carry)

        prev8 = jnp.where(ti > 0, p_ref[...], 0.0)
        pre, taps = _conv_taps(x_ref[...], prev8, w_ref, b_ref, kw)
        valid = _iota((tr, 1), 0) + ti * tr >= FRONT
        dpre = jnp.where(valid, d_ref[...] * _dsilu(pre), 0.0)
        gb_ref[...] += jnp.sum(dpre, axis=0, keepdims=True)
        for k in range(kw):
            gw_ref[k:k + 1, :] += jnp.sum(dpre * taps[k], axis=0, keepdims=True)
        o_ref[...] = jnp.where(valid, _conv_dx(dpre, carry[...], w_ref, kw), 0.0).astype(MXU_DTYPE)
        carry[...] = dpre[0:SUBLANES]

    return _call(
        body, name, (width // tc, nrow),
        [pl.BlockSpec((tr, tc), lambda j, i: (nrow - 1 - i, c0 + j)),
         pl.BlockSpec((SUBLANES, tc), lambda j, i: (jnp.maximum((nrow - 1 - i) * rb - 1, 0), c0 + j)),
         pl.BlockSpec((kw, tc), lambda j, i: (0, j)),
         pl.BlockSpec((1, tc), lambda j, i: (0, j)),
         pl.BlockSpec((tr, tc), lambda j, i: (nrow - 1 - i, j))],
        [pl.BlockSpec((tr, tc), lambda j, i: (nrow - 1 - i, j)),
         pl.BlockSpec((SUBLANES, tc), lambda j, i: (0, j)),
         pl.BlockSpec((1, tc), lambda j, i: (0, j))],
        [jax.ShapeDtypeStruct((t, width), MXU_DTYPE), jax.ShapeDtypeStruct((SUBLANES, width), F32),
         jax.ShapeDtypeStruct((1, width), F32)],
        [pltpu.VMEM((SUBLANES, tc), F32)], ("parallel", "arbitrary"), (proj, proj, w, b, dact), comm)


def _ffn_conv_fwd(up, w, b, name):
    t, f2 = up.shape
    f = f2 // 2
    kw = w.shape[0]
    tr, tc = _row_tile(t), _pick(f, (256, 128))
    nf, rb = f // tc, tr // SUBLANES

    def body(xg, pg, xv, pv, wg, wv, bg, bv, o_ref):
        i = pl.program_id(1)
        ag = _conv_pre(xg[...], jnp.where(i > 0, pg[...], 0.0), wg, bg, kw)
        av = _conv_pre(xv[...], jnp.where(i > 0, pv[...], 0.0), wv, bv, kw)
        o_ref[...] = (_silu(ag) * av).astype(MXU_DTYPE)

    def cur(off):
        return pl.BlockSpec((tr, tc), lambda j, i: (i, j + off))

    def prev(off):
        return pl.BlockSpec((SUBLANES, tc), lambda j, i: (jnp.maximum(i * rb - 1, 0), j + off))

    def par(rows, off):
        return pl.BlockSpec((rows, tc), lambda j, i: (0, j + off))

    return pl.pallas_call(
        body, name=name, grid=(nf, t // tr),
        in_specs=[cur(0), prev(0), cur(nf), prev(nf), par(kw, 0), par(kw, nf), par(1, 0), par(1, nf)],
        out_specs=pl.BlockSpec((tr, tc), lambda j, i: (i, j)),
        out_shape=jax.ShapeDtypeStruct((t, f), MXU_DTYPE),
        compiler_params=_params(("parallel", "parallel")),
    )(up, up, up, up, w, w, b, b)


def _ffn_conv_bwd(up, w, b, dact, name):
    t, f2 = up.shape
    f = f2 // 2
    kw = w.shape[0]
    tr, tc = _row_tile(t), _pick(f, (256, 128))
    nf, rb, nrow = f // tc, tr // SUBLANES, t // tr

    def body(xg, pg, xv, pv, wg, wv, bg, bv, d_ref, og_ref, ov_ref, gwg_ref, gwv_ref, gbg_ref, gbv_ref, cg, cv):
        i = pl.program_id(1)
        ti = nrow - 1 - i

        @pl.when(i == 0)
        def _():
            for r in (gwg_ref, gwv_ref, gbg_ref, gbv_ref, cg, cv):
                r[...] = jnp.zeros_like(r)

        ag, tg = _conv_taps(xg[...], jnp.where(ti > 0, pg[...], 0.0), wg, bg, kw)
        av, tv = _conv_taps(xv[...], jnp.where(ti > 0, pv[...], 0.0), wv, bv, kw)
        d = d_ref[...]
        s = jax.nn.sigmoid(ag)
        dag = d * av * (s * (1.0 + ag * (1.0 - s)))
        dav = d * (ag * s)
        gbg_ref[...] += jnp.sum(dag, axis=0, keepdims=True)
        gbv_ref[...] += jnp.sum(dav, axis=0, keepdims=True)
        for k in range(kw):
            gwg_ref[k:k + 1, :] += jnp.sum(dag * tg[k], axis=0, keepdims=True)
            gwv_ref[k:k + 1, :] += jnp.sum(dav * tv[k], axis=0, keepdims=True)
        valid = _iota((tr, 1), 0) + ti * tr >= FRONT
        og_ref[...] = jnp.where(valid, _conv_dx(dag, cg[...], wg, kw), 0.0).astype(MXU_DTYPE)
        ov_ref[...] = jnp.where(valid, _conv_dx(dav, cv[...], wv, kw), 0.0).astype(MXU_DTYPE)
        cg[...] = dag[0:SUBLANES]
        cv[...] = dav[0:SUBLANES]

    def cur(off):
        return pl.BlockSpec((tr, tc), lambda j, i: (nrow - 1 - i, j + off))

    def prev(off):
        return pl.BlockSpec((SUBLANES, tc), lambda j, i: (jnp.maximum((nrow - 1 - i) * rb - 1, 0), j + off))

    def par(rows, off):
        return pl.BlockSpec((rows, tc), lambda j, i: (0, j + off))

    acc8 = pl.BlockSpec((SUBLANES, tc), lambda j, i: (0, j))
    acc1 = pl.BlockSpec((1, tc), lambda j, i: (0, j))
    return pl.pallas_call(
        body, name=name, grid=(nf, nrow),
        in_specs=[cur(0), prev(0), cur(nf), prev(nf), par(kw, 0), par(kw, nf), par(1, 0), par(1, nf), cur(0)],
        out_specs=[cur(0), cur(0), acc8, acc8, acc1, acc1],
        out_shape=[jax.ShapeDtypeStruct((t, f), MXU_DTYPE), jax.ShapeDtypeStruct((t, f), MXU_DTYPE),
                   jax.ShapeDtypeStruct((SUBLANES, f), F32), jax.ShapeDtypeStruct((SUBLANES, f), F32),
                   jax.ShapeDtypeStruct((1, f), F32), jax.ShapeDtypeStruct((1, f), F32)],
        scratch_shapes=[pltpu.VMEM((SUBLANES, tc), F32), pltpu.VMEM((SUBLANES, tc), F32)],
        compiler_params=_params(("parallel", "arbitrary")),
    )(up, up, up, up, w, w, b, b, dact)


def _gate_fwd(bs, br, proj, c_gs, c_gr, name):
    t, d = bs.shape
    tr = _row_tile(t)

    def body(bs_ref, br_ref, gs_ref, gr_ref, o_ref):
        o_ref[...] = (jax.nn.sigmoid(gs_ref[...]) * bs_ref[...] + jax.nn.sigmoid(gr_ref[...]) * br_ref[...]).astype(MXU_DTYPE)

    row = pl.BlockSpec((tr, d), lambda i: (i, 0))
    return pl.pallas_call(
        body, name=name, grid=(t // tr,),
        in_specs=[row, row, pl.BlockSpec((tr, d), lambda i: (i, c_gs // d)), pl.BlockSpec((tr, d), lambda i: (i, c_gr // d))],
        out_specs=row, out_shape=jax.ShapeDtypeStruct((t, d), MXU_DTYPE),
        compiler_params=_params(("parallel",)),
    )(bs, br, proj, proj)


def _gate_bwd(dm, bs, br, proj, c_gs, c_gr, name):
    t, d = bs.shape
    tr = _row_tile(t)

    def body(dm_ref, bs_ref, br_ref, gs_ref, gr_ref, dbs_ref, dbr_ref, dgs_ref, dgr_ref):
        g = dm_ref[...]
        ss, sr = jax.nn.sigmoid(gs_ref[...]), jax.nn.sigmoid(gr_ref[...])
        dbs_ref[...] = (g * ss).astype(MXU_DTYPE)
        dbr_ref[...] = (g * sr).astype(MXU_DTYPE)
        dgs_ref[...] = (g * bs_ref[...] * ss * (1.0 - ss)).astype(MXU_DTYPE)
        dgr_ref[...] = (g * br_ref[...] * sr * (1.0 - sr)).astype(MXU_DTYPE)

    row = pl.BlockSpec((tr, d), lambda i: (i, 0))
    out = jax.ShapeDtypeStruct((t, d), MXU_DTYPE)
    return pl.pallas_call(
        body, name=name, grid=(t // tr,),
        in_specs=[row, row, row, pl.BlockSpec((tr, d), lambda i: (i, c_gs // d)), pl.BlockSpec((tr, d), lambda i: (i, c_gr // d))],
        out_specs=[row] * 4, out_shape=[out] * 4,
        compiler_params=_params(("parallel",)),
    )(dm, bs, br, proj, proj)


def _ret_consts(h):
    lg = math.log(1.0 - 2.0 ** (-5.0 - h))
    l = _iota((CHUNK, 1), 0).astype(F32)
    diff = l - _iota((1, CHUNK), 1).astype(F32)
    dm = jnp.exp(jnp.where(diff >= 0, diff * lg, -jnp.inf))
    dmt = jnp.exp(jnp.where(diff <= 0, -diff * lg, -jnp.inf))
    cs = jnp.exp((l + 1.0) * lg)
    kdec = jnp.exp((CHUNK - 1.0 - l) * lg)
    return dm, dmt, cs, kdec, math.exp(CHUNK * lg)


def _ret_fwd(proj, c_q, c_k, c_v, c_g, cos, sin, d, name, comm=None):
    t = proj.shape[0]
    nc = t // CHUNK
    hq, hv = d // RET_HEADS, 2 * d // RET_HEADS
    half = hq // 2
    scale = hq ** -0.5

    def body(q_ref, k_ref, v_ref, g_ref, cos_ref, sin_ref, o_ref, y_ref, qr_ref, kr_ref, st_ref, rs):
        @pl.when(pl.program_id(0) == 0)
        def _():
            rs[...] = jnp.zeros_like(rs)

        co, si = cos_ref[...], sin_ref[...]
        for h in range(RET_HEADS):
            dm, _, cs, kdec, gam = _ret_consts(h)
            q1, q2 = q_ref[:, h * hq:h * hq + half], q_ref[:, h * hq + half:(h + 1) * hq]
            k1, k2 = k_ref[:, h * hq:h * hq + half], k_ref[:, h * hq + half:(h + 1) * hq]
            qr = jnp.concatenate([q1 * co - q2 * si, q2 * co + q1 * si], axis=1)
            kr = jnp.concatenate([k1 * co - k2 * si, k2 * co + k1 * si], axis=1) * scale
            qr_ref[:, h * hq:(h + 1) * hq] = qr.astype(MXU_DTYPE)
            kr_ref[:, h * hq:(h + 1) * hq] = kr.astype(MXU_DTYPE)
            v = v_ref[:, h * hv:(h + 1) * hv]
            r_in = rs[h * hq:(h + 1) * hq, :]
            st_ref[0, h * hq:(h + 1) * hq, :] = r_in.astype(MXU_DTYPE)
            s = _dot_nt(qr, kr) * dm
            o = _dot(s, v) + cs * _dot(qr, r_in)
            rs[h * hq:(h + 1) * hq, :] = gam * r_in + _dot_tn(kr * kdec, v)
            o_ref[:, h * hv:(h + 1) * hv] = o
            on = o * lax.rsqrt(jnp.mean(o * o, axis=1, keepdims=True) + EPS)
            y_ref[:, h * hv:(h + 1) * hv] = (_silu(g_ref[:, h * hv:(h + 1) * hv]) * on).astype(MXU_DTYPE)

    def col(width, c0):
        return pl.BlockSpec((CHUNK, width), lambda i: (i, c0 // width))

    tab = pl.BlockSpec((CHUNK, half), lambda i: (i, 0))
    return _call(
        body, name, (nc,),
        [col(d, c_q), col(d, c_k), col(2 * d, c_v), col(2 * d, c_g), tab, tab],
        [col(2 * d, 0), col(2 * d, 0), col(d, 0), col(d, 0), pl.BlockSpec((1, d, hv), lambda i: (i, 0, 0))],
        [jax.ShapeDtypeStruct((t, 2 * d), F32), jax.ShapeDtypeStruct((t, 2 * d), MXU_DTYPE),
         jax.ShapeDtypeStruct((t, d), MXU_DTYPE), jax.ShapeDtypeStruct((t, d), MXU_DTYPE),
         jax.ShapeDtypeStruct((nc, d, hv), MXU_DTYPE)],
        [pltpu.VMEM((d, hv), F32)], ("arbitrary",), (proj, proj, proj, proj, cos, sin), comm)


def _ret_bwd(dy, proj, c_v, c_g, o, qr, kr, st, cos, sin, d, name, comm=None):
    t = proj.shape[0]
    nc = t // CHUNK
    hq, hv = d // RET_HEADS, 2 * d // RET_HEADS
    half = hq // 2
    scale = hq ** -0.5

    def body(dy_ref, v_ref, g_ref, o_ref, qr_ref, kr_ref, st_ref, cos_ref, sin_ref,
             dq_ref, dk_ref, dv_ref, dg_ref, drs):
        @pl.when(pl.program_id(0) == 0)
        def _():
            drs[...] = jnp.zeros_like(drs)

        co, si = cos_ref[...], sin_ref[...]
        for h in range(RET_HEADS):
            dm, dmt, cs, kdec, gam = _ret_consts(h)
            vs = slice(h * hv, (h + 1) * hv)
            qs = slice(h * hq, (h + 1) * hq)
            o_h = o_ref[:, vs]
            g_h = g_ref[:, vs]
            d_y = dy_ref[:, vs]
            r = lax.rsqrt(jnp.mean(o_h * o_h, axis=1, keepdims=True) + EPS)
            on = o_h * r
            d_on = d_y * _silu(g_h)
            dg_ref[:, vs] = (d_y * on * _dsilu(g_h)).astype(MXU_DTYPE)
            d_o = r * (d_on - on * jnp.mean(d_on * on, axis=1, keepdims=True))
            q_h, k_h, v_h = qr_ref[:, qs], kr_ref[:, qs], v_ref[:, vs]
            r_in = st_ref[0, qs, :]
            dr_n = drs[qs, :]
            csdo = cs * d_o
            ds = _dot_nt(d_o, v_h) * dm
            dst = _dot_nt(v_h, d_o) * dmt
            s_t = _dot_nt(k_h, q_h) * dmt
            dqr = _dot(ds, k_h) + _dot_nt(csdo, r_in)
            dkr = _dot(dst, q_h) + kdec * _dot_nt(v_h, dr_n)
            dv_ref[:, vs] = (_dot(s_t, d_o) + _dot(k_h.astype(F32) * kdec, dr_n)).astype(MXU_DTYPE)
            drs[qs, :] = gam * dr_n + _dot_tn(q_h, csdo)
            a1, a2 = dqr[:, :half], dqr[:, half:]
            dq_ref[:, qs] = jnp.concatenate([a1 * co + a2 * si, a2 * co - a1 * si], axis=1).astype(MXU_DTYPE)
            b1, b2 = dkr[:, :half] * scale, dkr[:, half:] * scale
            dk_ref[:, qs] = jnp.concatenate([b1 * co + b2 * si, b2 * co - b1 * si], axis=1).astype(MXU_DTYPE)

    def col(width, c0=0):
        return pl.BlockSpec((CHUNK, width), lambda i: (nc - 1 - i, c0 // width))

    tab = pl.BlockSpec((CHUNK, half), lambda i: (nc - 1 - i, 0))
    return _call(
        body, name, (nc,),
        [col(2 * d), col(2 * d, c_v), col(2 * d, c_g), col(2 * d), col(d), col(d),
         pl.BlockSpec((1, d, hv), lambda i: (nc - 1 - i, 0, 0)), tab, tab],
        [col(d), col(d), col(2 * d), col(2 * d)],
        [jax.ShapeDtypeStruct((t, d), MXU_DTYPE), jax.ShapeDtypeStruct((t, d), MXU_DTYPE),
         jax.ShapeDtypeStruct((t, 2 * d), MXU_DTYPE), jax.ShapeDtypeStruct((t, 2 * d), MXU_DTYPE)],
        [pltpu.VMEM((d, hv), F32)], ("arbitrary",), (dy, proj, proj, o, qr, kr, st, cos, sin), comm)


def _ssd_small(dtraw_ref, dtb_ref, alog_ref, chunk_idx, nh):
    rows = _iota((CHUNK, 1), 0)
    ok = ((rows >= FRONT) | (chunk_idx > 0)) & (_iota((1, LANES), 1) < nh)
    z = dtraw_ref[...] + dtb_ref[...]
    dt = jnp.where(ok, jax.nn.softplus(z), 0.0)
    sig = jnp.where(ok, jax.nn.sigmoid(z), 0.0)
    a = jnp.where(_iota((1, LANES), 1) < nh, -jnp.exp(alog_ref[...]), 0.0)
    tri = (_iota((CHUNK, CHUNK), 0) >= _iota((CHUNK, CHUNK), 1)).astype(F32)
    acs = _dot01(tri, dt * a, "b", 3)
    return dt, sig, a, acs, acs.T


def _head_expand(g, hpg, gw):
    shift = int(math.log2(SSD_HEAD_DIM))
    return (_iota((LANES, gw), 0) == g * hpg + lax.shift_right_logical(_iota((LANES, gw), 1), shift)).astype(F32)


def _ssd_fwd(xa, proj, c_dt, c_z, dtb, alog, dvec, nw, di, name, comm=None):
    t = xa.shape[0]
    nc = t // CHUNK
    nh = di // SSD_HEAD_DIM
    hpg = nh // SSD_GROUPS
    gw = di // SSD_GROUPS
    n = SSD_STATE
    gn = SSD_GROUPS * n
    hd = SSD_HEAD_DIM

    def body(x_ref, b_ref, c_ref, dtraw_ref, z_ref, dtb_ref, alog_ref, d_ref, nw_ref,
             y_ref, ys_ref, st_ref, hts, xdt_s):
        c = pl.program_id(0)

        @pl.when(c == 0)
        def _():
            hts[...] = jnp.zeros_like(hts)

        dt, _, _, acs, acs_t = _ssd_small(dtraw_ref, dtb_ref, alog_ref, c, nh)
        tri = _iota((CHUNK, CHUNK), 0) >= _iota((CHUNK, CHUNK), 1)
        dvec8 = jnp.broadcast_to(d_ref[...], (SUBLANES, LANES))
        for g in range(SSD_GROUPS):
            gs = slice(g * gw, (g + 1) * gw)
            ns = slice(g * n, (g + 1) * n)
            e_mat = _head_expand(g, hpg, gw)
            ax = _dot01(acs, e_mat, "a", 3)
            dtx = _dot01(dt, e_mat, "a", 3)
            dx = _dot01(dvec8, e_mat, "a", 3)[0:1, :]
            xg, bg, cg = x_ref[:, gs], b_ref[:, ns], c_ref[:, ns]
            xdt = xg * dtx
            xdt_s[...] = xdt.astype(MXU_DTYPE)
            cb = _dot_nt(cg, bg)
            ht = hts[ns, :]
            st_ref[0, ns, :] = ht.astype(MXU_DTYPE)
            y_ref[:, gs] = jnp.exp(ax) * _dot(cg, ht) + dx * xg
            for hh in range(hpg):
                h = g * hpg + hh
                lmat = jnp.exp(jnp.where(tri, acs[:, h:h + 1] - acs_t[h:h + 1, :], -jnp.inf))
                hs = slice(g * gw + hh * hd, g * gw + (hh + 1) * hd)
                y_ref[:, hs] += _dot(cb * lmat, xdt_s[:, hh * hd:(hh + 1) * hd])
            aend = ax[CHUNK - 1:CHUNK, :]
            hts[ns, :] = jnp.exp(aend) * ht + _dot_tn(bg, xdt * jnp.exp(aend - ax))
        for g in range(SSD_GROUPS):
            gs = slice(g * gw, (g + 1) * gw)
            yz = y_ref[:, gs] * _silu(z_ref[:, gs])
            r = lax.rsqrt(jnp.mean(yz * yz, axis=1, keepdims=True) + EPS)
            ys_ref[:, gs] = (yz * r * nw_ref[:, gs]).astype(MXU_DTYPE)

    def col(width, c0, arr_is_xa=False):
        return pl.BlockSpec((CHUNK, width), lambda i: (i, c0 // width))

    vec = pl.BlockSpec((1, LANES), lambda i: (0, 0))
    assert di % gn == 0 and c_dt % LANES == 0 and c_z % di == 0
    return _call(
        body, name, (nc,),
        [col(di, 0), col(gn, di), col(gn, di + gn), col(LANES, c_dt), col(di, c_z), vec, vec, vec,
         pl.BlockSpec((1, di), lambda i: (0, 0))],
        [col(di, 0), col(di, 0), pl.BlockSpec((1, gn, gw), lambda i: (i, 0, 0))],
        [jax.ShapeDtypeStruct((t, di), F32), jax.ShapeDtypeStruct((t, di), MXU_DTYPE),
         jax.ShapeDtypeStruct((nc, gn, gw), MXU_DTYPE)],
        [pltpu.VMEM((gn, gw), F32), pltpu.VMEM((CHUNK, gw), MXU_DTYPE)], ("arbitrary",),
        (xa, xa, xa, proj, proj, dtb, alog, dvec, nw), comm)


def _ssd_bwd(dys, xa, proj, c_dt, c_z, ypre, st, dtb, alog, dvec, nw, di, name, comm=None):
    t = xa.shape[0]
    nc = t // CHUNK
    nh = di // SSD_HEAD_DIM
    hpg = nh // SSD_GROUPS
    gw = di // SSD_GROUPS
    n = SSD_STATE
    gn = SSD_GROUPS * n
    hd = SSD_HEAD_DIM

    def body(dys_ref, x_ref, b_ref, c_ref, dtraw_ref, z_ref, y_ref, st_ref, dtb_ref, alog_ref, d_ref, nw_ref,
             dxa_ref, dz_ref, ddt_ref, gb_ref, ga_ref, gd_ref, gnw_ref, dhts, dy_s, xdt_s, dxdt_s):
        i = pl.program_id(0)
        c = nc - 1 - i

        @pl.when(i == 0)
        def _():
            dhts[...] = jnp.zeros_like(dhts)
            gb_ref[...] = jnp.zeros_like(gb_ref)
            ga_ref[...] = jnp.zeros_like(ga_ref)
            gd_ref[...] = jnp.zeros_like(gd_ref)
            gnw_ref[...] = jnp.zeros_like(gnw_ref)

        dt, sig, a, acs, acs_t = _ssd_small(dtraw_ref, dtb_ref, alog_ref, c, nh)
        tri = _iota((CHUNK, CHUNK), 0) >= _iota((CHUNK, CHUNK), 1)
        triu = _iota((CHUNK, CHUNK), 0) <= _iota((CHUNK, CHUNK), 1)
        lane = _iota((1, LANES), 1)
        rows = _iota((CHUNK, 1), 0)
        head_row = _iota((LANES, 1), 0)
        dvec8 = jnp.broadcast_to(d_ref[...], (SUBLANES, LANES))
        da = jnp.zeros((CHUNK, LANES), F32)
        da_t = jnp.zeros((LANES, CHUNK), F32)
        ddt = jnp.zeros((CHUNK, LANES), F32)
        gd = jnp.zeros((1, LANES), F32)
        for g in range(SSD_GROUPS):
            gs = slice(g * gw, (g + 1) * gw)
            ns = slice(g * n, (g + 1) * n)
            y_g, z_g = y_ref[:, gs], z_ref[:, gs]
            sz = _silu(z_g)
            yz = y_g * sz
            r = lax.rsqrt(jnp.mean(yz * yz, axis=1, keepdims=True) + EPS)
            nrm = yz * r
            dyo = dys_ref[:, gs]
            gnw_ref[:, gs] += jnp.sum(dyo * nrm, axis=0, keepdims=True)
            dn = dyo * nw_ref[:, gs]
            dyz = r * (dn - nrm * jnp.mean(dn * nrm, axis=1, keepdims=True))
            dz_ref[:, gs] = (dyz * y_g * _dsilu(z_g)).astype(MXU_DTYPE)
            dy_g = dyz * sz
            dy_s[...] = dy_g.astype(MXU_DTYPE)
            e_mat = _head_expand(g, hpg, gw)
            ax = _dot01(acs, e_mat, "a", 3)
            dtx = _dot01(dt, e_mat, "a", 3)
            dx = _dot01(dvec8, e_mat, "a", 3)[0:1, :]
            xg, bg, cg = x_ref[:, gs], b_ref[:, ns], c_ref[:, ns]
            xdt = xg * dtx
            xdt_s[...] = xdt.astype(MXU_DTYPE)
            aend = ax[CHUNK - 1:CHUNK, :]
            e = jnp.exp(aend - ax)
            ea = jnp.exp(ax)
            eend = jnp.exp(aend)
            htp = st_ref[0, ns, :].astype(F32)
            dht = dhts[ns, :]
            cb = _dot_nt(cg, bg)
            q = _dot(bg, dht)
            dxdt_s[...] = e * q
            wl = e * q * xdt
            d_b = _dot_nt(e * xdt, dht)
            yi = ea * _dot(cg, htp)
            eady = ea * dy_g
            d_c = _dot_nt(eady, htp)
            t1 = jnp.sum(dht * htp, axis=0, keepdims=True) * eend
            dhts[ns, :] = eend * dht + _dot_tn(cg, eady)
            da = da + _dot01(dy_g * yi - wl, e_mat, "a", 3, _NT)
            tail = jnp.broadcast_to(jnp.sum(wl, axis=0, keepdims=True) + t1, (SUBLANES, gw))
            da_end = _dot01(tail, e_mat, "a", 3, _NT)[0:1, :]
            da = da + jnp.where(rows == CHUNK - 1, da_end, 0.0)
            dcb = jnp.zeros((CHUNK, CHUNK), F32)
            for hh in range(hpg):
                h = g * hpg + hh
                lmat = jnp.exp(jnp.where(tri, acs[:, h:h + 1] - acs_t[h:h + 1, :], -jnp.inf))
                hl = slice(hh * hd, (hh + 1) * hd)
                dy_h, xdt_h = dy_s[:, hl], xdt_s[:, hl]
                dxdt_s[:, hl] += _dot_tn(cb * lmat, dy_h)
                dml = _dot_nt(dy_h, xdt_h) * lmat
                dcb = dcb + dml
                gmat = dml * cb
                da = da + jnp.where(lane == h, jnp.sum(gmat, axis=1, keepdims=True), 0.0)
                da_t = da_t - jnp.where(head_row == h, jnp.sum(gmat, axis=0, keepdims=True), 0.0)
            d_c = d_c + _dot(dcb, bg)
            d_b = d_b + _dot_tn(dcb, cg)
            dxdt = dxdt_s[...]
            dxa_ref[:, gs] = dxdt * dtx + dx * dy_g
            dxa_ref[:, di + g * n:di + (g + 1) * n] = d_b
            dxa_ref[:, di + gn + g * n:di + gn + (g + 1) * n] = d_c
            ddt = ddt + _dot01(dxdt * xg, e_mat, "a", 3, _NT)
            gd8 = jnp.broadcast_to(jnp.sum(dy_g * xg, axis=0, keepdims=True), (SUBLANES, gw))
            gd = gd + _dot01(gd8, e_mat, "a", 3, _NT)[0:1, :]
        da = da + da_t.T
        triu_f = triu.astype(F32)
        ddta = _dot01(triu_f, da, "b", 3)
        ddt = ddt + ddta * a
        draw = ddt * sig
        ddt_ref[...] = draw.astype(MXU_DTYPE)
        gb_ref[...] += jnp.sum(draw, axis=0, keepdims=True)
        ga_ref[...] += jnp.sum(ddta * dt, axis=0, keepdims=True) * a
        gd_ref[...] += gd

    def col(width, c0):
        return pl.BlockSpec((CHUNK, width), lambda i: (nc - 1 - i, c0 // width))

    vec = pl.BlockSpec((1, LANES), lambda i: (0, 0))
    wide = pl.BlockSpec((1, di), lambda i: (0, 0))
    wa = di + 2 * gn
    return _call(
        body, name, (nc,),
        [col(di, 0), col(di, 0), col(gn, di), col(gn, di + gn), col(LANES, c_dt), col(di, c_z), col(di, 0),
         pl.BlockSpec((1, gn, gw), lambda i: (nc - 1 - i, 0, 0)), vec, vec, vec, wide],
        [col(wa, 0), col(di, 0), col(LANES, 0), vec, vec, vec, wide],
        [jax.ShapeDtypeStruct((t, wa), F32), jax.ShapeDtypeStruct((t, di), MXU_DTYPE),
         jax.ShapeDtypeStruct((t, LANES), MXU_DTYPE), jax.ShapeDtypeStruct((1, LANES), F32),
         jax.ShapeDtypeStruct((1, LANES), F32), jax.ShapeDtypeStruct((1, LANES), F32),
         jax.ShapeDtypeStruct((1, di), F32)],
        [pltpu.VMEM((gn, gw), F32), pltpu.VMEM((CHUNK, gw), MXU_DTYPE), pltpu.VMEM((CHUNK, gw), MXU_DTYPE),
         pltpu.VMEM((CHUNK, gw), F32)], ("arbitrary",),
        (dys, xa, xa, xa, proj, proj, ypre, st, dtb, alog, dvec, nw), comm)


def _adam_math(w, g, m, v):
    m2 = ADAM_B1 * m + (1.0 - ADAM_B1) * g
    v2 = ADAM_B2 * v + (1.0 - ADAM_B2) * (g * g)
    m_hat = m2 / (1.0 - ADAM_B1 ** ADAM_STEP)
    v_hat = v2 / (1.0 - ADAM_B2 ** ADAM_STEP)
    delta = -ADAM_LR * (m_hat / (jnp.sqrt(v_hat) + ADAM_EPS) + ADAM_WD * w)
    return delta, m2, v2


def _adam_big(w, g_mine, g_sib, m, v, core, name):
    r, c = w.shape
    h = r // 2
    tr = _pick(h, (128, 64, 32, 16, 8))
    nbh = h // tr

    def body(core_ref, w_ref, a_ref, b_ref, m_ref, v_ref, g_ref, d_ref, m2_ref, v2_ref):
        g = jnp.where(pl.program_id(0) // nbh == core_ref[0], a_ref[...], b_ref[...])
        delta, m2, v2 = _adam_math(w_ref[...], g, m_ref[...], v_ref[...])
        g_ref[...] = g
        d_ref[...] = delta
        m2_ref[...] = m2
        v2_ref[...] = v2

    blk = pl.BlockSpec((tr, c), lambda i, core_ref: (i, 0))
    hblk = pl.BlockSpec((tr, c), lambda i, core_ref: (i % nbh, 0))
    out = jax.ShapeDtypeStruct((r, c), F32)
    return pl.pallas_call(
        body, name=name,
        grid_spec=pltpu.PrefetchScalarGridSpec(num_scalar_prefetch=1, grid=(2 * nbh,),
                                               in_specs=[blk, hblk, hblk, blk, blk], out_specs=[blk] * 4),
        out_shape=[out] * 4, compiler_params=_params(("parallel",)),
    )(core, w, g_mine, g_sib, m, v)


def _pair_sum(g, sib, core, name):
    _, r, c = g.shape
    h = r // 2
    tr = _pick(h, (128, 64, 32, 16))
    nb = h // tr

    def body(core_ref, g_ref, s_ref, o_ref):
        o_ref[...] = (g_ref[...] + s_ref[...]).astype(WIRE_DTYPE)

    return pl.pallas_call(
        body, name=name,
        grid_spec=pltpu.PrefetchScalarGridSpec(
            num_scalar_prefetch=1, grid=(4, nb),
            in_specs=[pl.BlockSpec((1, tr, c), lambda j, i, core_ref: (j, core_ref[0] * nb + i, 0)),
                      pl.BlockSpec((1, tr, c), lambda j, i, core_ref: (j, i, 0))],
            out_specs=pl.BlockSpec((1, tr, c), lambda j, i, core_ref: (j, i, 0))),
        out_shape=jax.ShapeDtypeStruct((4, h, c), WIRE_DTYPE), compiler_params=_params(("parallel", "parallel")),
    )(core, g, sib)


def _sum4(parts, name):
    _, r, c = parts.shape
    tr = _pick(r, (128, 64, 32, 16, 8))

    def body(p_ref, o_ref):
        acc = p_ref[0].astype(F32)
        for j in range(1, 4):
            acc = acc + p_ref[j].astype(F32)
        o_ref[...] = acc

    return pl.pallas_call(
        body, name=name, grid=(r // tr,),
        in_specs=[pl.BlockSpec((4, tr, c), lambda i: (0, i, 0))],
        out_specs=pl.BlockSpec((tr, c), lambda i: (i, 0)),
        out_shape=jax.ShapeDtypeStruct((r, c), F32),
        compiler_params=_params(("parallel",)),
    )(parts)


def _adam_small(w, gparts, m, v, name):
    r = w.shape[0]

    def body(w_ref, g_ref, m_ref, v_ref, go_ref, d_ref, m2_ref, v2_ref):
        g = g_ref[0]
        for j in range(1, 8):
            g = g + g_ref[j]
        delta, m2, v2 = _adam_math(w_ref[...], g, m_ref[...], v_ref[...])
        go_ref[...] = g
        d_ref[...] = delta
        m2_ref[...] = m2
        v2_ref[...] = v2

    out = jax.ShapeDtypeStruct((r, LANES), F32)
    return pl.pallas_call(body, name=name, out_shape=[out] * 4)(w, gparts, m, v)


def _allgather8(v, name):
    def body(src, dst, send, recv, loc):
        x, y, c = lax.axis_index("x"), lax.axis_index("y"), lax.axis_index("c")
        mine = 4 * x + 2 * y + c
        lc = pltpu.make_async_copy(src, dst.at[mine], loc)
        lc.start()
        copies = []
        for k in range(1, 8):
            fx, fy, fc = (k >> 2) & 1, (k >> 1) & 1, k & 1
            peer = (1 - x if fx else x, 1 - y if fy else y, 1 - c if fc else c)
            cp = pltpu.make_async_remote_copy(src_ref=src, dst_ref=dst.at[mine], send_sem=send.at[k - 1],
                                              recv_sem=recv.at[k - 1], device_id=peer, device_id_type=MESH)
            cp.start()
            copies.append(cp)
        for cp in copies:
            cp.wait()
        lc.wait()

    anyspec = pl.BlockSpec(memory_space=pl.ANY)
    return pl.pallas_call(
        body, name=name, in_specs=[anyspec], out_specs=anyspec,
        out_shape=jax.ShapeDtypeStruct((8,) + v.shape, v.dtype),
        scratch_shapes=[pltpu.SemaphoreType.DMA((7,)), pltpu.SemaphoreType.DMA((7,)), pltpu.SemaphoreType.DMA],
        compiler_params=pltpu.CompilerParams(has_side_effects=True),
    )(v)


def _pack(parts):
    flat = jnp.concatenate([p.reshape(-1).astype(F32) for p in parts])
    pad = (-flat.shape[0]) % (32 * LANES)
    return jnp.pad(flat, (0, pad)).reshape(-1, LANES)


def _unpack(slab, shapes):
    flat = slab.reshape(-1)
    out, off = [], 0
    for s in shapes:
        size = int(np.prod(s))
        out.append(flat[off:off + size].reshape(s))
        off += size
    return out


def _pad_lanes(v):
    return jnp.pad(v.reshape(1, -1), ((0, 0), (0, LANES - v.shape[-1])))


def kernel(x, meta_tokens, mix_norm_w, w_in, ssd_conv_w, ssd_conv_b, ssd_dt_bias, ssd_A_log, ssd_D, ssd_norm_w, w_branch_ssd, w_branch_ret, w_out, ffn_norm_w, w_up, ffn_conv_w, ffn_conv_b, w_down, final_norm_w, loss_target, m_meta_tokens, m_mix_norm_w, m_w_in, m_ssd_conv_w, m_ssd_conv_b, m_ssd_dt_bias, m_ssd_A_log, m_ssd_D, m_ssd_norm_w, m_w_branch_ssd, m_w_branch_ret, m_w_out, m_ffn_norm_w, m_w_up, m_ffn_conv_w, m_ffn_conv_b, m_w_down, m_final_norm_w, v_meta_tokens, v_mix_norm_w, v_w_in, v_ssd_conv_w, v_ssd_conv_b, v_ssd_dt_bias, v_ssd_A_log, v_ssd_D, v_ssd_norm_w, v_w_branch_ssd, v_w_branch_ret, v_w_out, v_ffn_norm_w, v_w_up, v_ffn_conv_w, v_ffn_conv_b, v_w_down, v_final_norm_w):
    seq, d = x.shape[1], x.shape[2]
    t = CHUNK + seq
    di = 2 * d
    nh = di // SSD_HEAD_DIM
    gn = SSD_GROUPS * SSD_STATE
    cw = di + 2 * gn
    f = w_down.shape[1] * 4
    chip = 2 * lax.axis_index("x") + lax.axis_index("y")

    order = [("z", di), ("v", di), ("g", di), ("xbc", cw), ("q", d), ("k", d), ("gs", d), ("gr", d), ("dt", LANES)]
    col, acc = {}, 0
    for nm, wd in order:
        col[nm] = acc
        acc += wd
    wp = acc
    ref_order = [("z", di), ("xbc", cw), ("dt", nh), ("q", d), ("k", d), ("v", di), ("g", di), ("gs", d), ("gr", d)]
    ref_off, acc = {}, 0
    for nm, wd in ref_order:
        ref_off[nm] = (acc, wd)
        acc += wd
    in_dim = acc

    core = lax.axis_index("c").astype(jnp.int32).reshape(1)
    small_shapes = [meta_tokens.shape, ssd_conv_w.shape[1:], ffn_conv_w.shape[1:]]
    small_local = _pack([meta_tokens, ssd_conv_w[0], ffn_conv_w[0]])
    first_local = [w_in[0].astype(WIRE_DTYPE), small_local]
    first_half = _run_comm(_gather_ici(first_local), "gather_w_in_ici")
    g_in, g_small = [_with_own(g, own, chip)
                     for g, own in zip(_run_comm(_gather_d2d(first_half), "gather_w_in_d2d"), first_local)]
    rest_local = [a[0].astype(WIRE_DTYPE) for a in (w_branch_ssd, w_branch_ret, w_out, w_up, w_down)]
    w_in_full = jnp.moveaxis(g_in, 0, 1).reshape(d, in_dim)
    pieces = []
    for nm, wd in order:
        o, rw = ref_off[nm]
        p = w_in_full[:, o:o + rw]
        if rw < wd:
            p = jnp.pad(p, ((0, 0), (0, wd - rw)))
        pieces.append(p)
    w_p = jnp.concatenate(pieces, axis=1)
    smalls = [_unpack(g_small[j], small_shapes) for j in range(4)]
    meta_full = jnp.concatenate([s[0] for s in smalls], axis=1)
    scw = jnp.concatenate([s[1] for s in smalls], axis=1)
    fcw = jnp.concatenate([s[2] for s in smalls], axis=1)
    scb, fcb = ssd_conv_b, ffn_conv_b
    dtb, alog, dvec = _pad_lanes(ssd_dt_bias), _pad_lanes(ssd_A_log), _pad_lanes(ssd_D)
    fin_w = final_norm_w.reshape(1, d)

    hq = d // RET_HEADS
    pos = jnp.arange(t, dtype=F32) - FRONT
    inv_freq = ROPE_BASE ** (-jnp.linspace(0.0, 1.0, hq // 2, dtype=F32))
    ang = pos[:, None] * inv_freq[None, :]
    cos, sin = jnp.cos(ang), jnp.sin(ang)

    h0 = jnp.concatenate([jnp.zeros((FRONT, d), F32), meta_full, x[0]], axis=0)
    tm = _row_tile(t)
    u1 = _rms_fwd(h0, mix_norm_w, "rms1_fwd")
    proj = _mm(u1, w_p, "nn", F32, "proj", tm, _pick(wp, (1920, 1536, 1280, 1024, 896, 768, 640, 512, 384, 256, 128)), d)
    xa = _ssd_conv_fwd(proj, col["xbc"], cw, scw, scb, "ssd_conv_fwd")
    res = _ssd_fwd(xa, proj, col["dt"], col["z"], dtb, alog, dvec, ssd_norm_w, di, "ssd_fwd", comm=_gather_ici(rest_local))
    (ypre, yssd, st_ssd), rest_half = res[:3], res[3:]
    res = _ret_fwd(proj, col["q"], col["k"], col["v"], col["g"], cos, sin, d, "ret_fwd", comm=_gather_d2d(rest_half))
    o_ret, yret, qr, kr, st_ret = res[:5]
    g_bs, g_br, g_out, g_up, g_down = [_with_own(g, own, chip) for g, own in zip(res[5:], rest_local)]
    w_bs = g_bs.reshape(di, d)
    w_br = g_br.reshape(di, d)
    w_o = g_out.reshape(d, d)
    w_u = jnp.moveaxis(g_up, 0, 1).reshape(d, 2 * f)
    w_d = g_down.reshape(f, d)
    tn_d = _pick(d, (1024, 512, 256, 128))
    bs = _mm(yssd, w_bs, "nn", F32, "branch_ssd", tm, tn_d, _pick(di, (1024, 512, 256)))
    br = _mm(yret, w_br, "nn", F32, "branch_ret", tm, tn_d, _pick(di, (1024, 512, 256)))
    merged = _gate_fwd(bs, br, proj, col["gs"], col["gr"], "gate_fwd")
    h1 = _mm(merged, w_o, "nn", F32, "out_proj", tm, tn_d, d, res=h0)
    u2 = _rms_fwd(h1, ffn_norm_w, "rms2_fwd")
    tn_f = _pick(2 * f, (1408, 1024, 768, 512, 256, 128))
    up = _mm(u2, w_u, "nn", F32, "up_proj", tm, tn_f, d)
    act = _ffn_conv_fwd(up, fcw, fcb, "ffn_conv_fwd")
    tk_f = _pick(f, (1408, 768, 704, 512, 256, 128))
    h2 = _mm(act, w_d, "nn", F32, "down_proj", tm, tn_d, tk_f, res=h1)
    loss8, d_h2, g_fin = _loss_bwd(h2, fin_w, loss_target[0], "loss_head")

    d_act = _mm(d_h2, w_d, "nt", F32, "d_act", tm, tk_f, d)
    g_wd = _mm(act, d_h2, "tn", F32, "g_w_down", tk_f, tn_d, tm)
    d_upg, d_upv, g_fcwg, g_fcwv, g_fcbg, g_fcbv = _ffn_conv_bwd(up, fcw, fcb, d_act, "ffn_conv_bwd")
    g_fcw = jnp.concatenate([g_fcwg, g_fcwv], axis=1)
    g_fcb = jnp.concatenate([g_fcbg, g_fcbv], axis=1)
    d_u2 = _mm(d_upg, w_u[:, :f], "nt", F32, "d_u2_gate", tm, tn_d, tk_f)
    d_u2 = _mm(d_upv, w_u[:, f:], "nt", F32, "d_u2_value", tm, tn_d, tk_f, res=d_u2)
    g_wu = jnp.concatenate([_mm(u2, d_upg, "tn", F32, "g_w_up_gate", tn_d, tk_f, tm),
                            _mm(u2, d_upv, "tn", F32, "g_w_up_value", tn_d, tk_f, tm)], axis=1)
    d_h1, g_ffnw = _rms_bwd(h1, ffn_norm_w, d_u2, d_h2, "rms2_bwd")
    d_merged = _mm(d_h1, w_o, "nt", F32, "d_merged", tm, tn_d, d)
    g_wo = _mm(merged, d_h1, "tn", F32, "g_w_out", tn_d, tn_d, tm)
    d_bs, d_br, d_gs, d_gr = _gate_bwd(d_merged, bs, br, proj, col["gs"], col["gr"], "gate_bwd")
    tk_i = _pick(di, (1024, 512, 256))
    d_yssd = _mm(d_bs, w_bs, "nt", F32, "d_y_ssd", tm, tk_i, d)
    g_wbs = _mm(yssd, d_bs, "tn", F32, "g_w_branch_ssd", tk_i, tn_d, tm)
    d_yret = _mm(d_br, w_br, "nt", F32, "d_y_ret", tm, tk_i, d)
    g_wbr = _mm(yret, d_br, "tn", F32, "g_w_branch_ret", tk_i, tn_d, tm)

    early_names = ["w_branch_ssd", "w_branch_ret", "w_out", "w_up", "w_down"]
    early = [g_wbs.reshape(4, di // 4, d), g_wbr.reshape(4, di // 4, d), g_wo.reshape(4, d // 4, d),
             jnp.moveaxis(g_wu.reshape(d, 4, 2 * f // 4), 1, 0), g_wd.reshape(4, f // 4, d)]
    res = _ret_bwd(d_yret, proj, col["v"], col["g"], o_ret, qr, kr, st_ret, cos, sin, d, "ret_bwd", comm=_scatter_d2d(early))
    (dq, dk, dv, dg), early_sib = res[:4], res[4:]
    early_pair = [_pair_sum(g_, s_, core, "pair_" + nm) for g_, s_, nm in zip(early, early_sib, early_names)]
    res = _ssd_bwd(d_yssd, xa, proj, col["dt"], col["z"], ypre, st_ssd, dtb, alog, dvec, ssd_norm_w, di, "ssd_bwd",
                   comm=_scatter_ici(early_pair))
    (d_xa, dz, ddt, g_dtb, g_alog, g_dvec, g_snw), early_recv = res[:7], res[7:]
    early_mine = [_sum4(p, "sum4_" + nm) for p, nm in zip(early_recv, early_names)]
    res = _ssd_conv_bwd(proj, col["xbc"], cw, scw, scb, d_xa, "ssd_conv_bwd", comm=_sibling_swap(early_mine))
    (d_xbc, g_scw, g_scb), early_other = res[:3], res[3:]
    d_proj = jnp.concatenate([dz, dv, dg, d_xbc, dq, dk, d_gs, d_gr, ddt], axis=1)
    tn_p = _pick(wp, (1920, 1536, 1280, 1024, 896, 768, 640, 512, 384, 256, 128))

    g_wp = _mm(u1, d_proj, "tn", F32, "g_w_in", tn_d, tn_p, tm)
    g_in_ref = jnp.concatenate([g_wp[:, col[nm]:col[nm] + rw] for nm, rw in ref_order], axis=1)
    sc_in = jnp.moveaxis(g_in_ref.reshape(d, 4, in_dim // 4), 1, 0)
    in_sib = _run_comm(_scatter_d2d([sc_in]), "scatter_w_in_d2d")[0]
    in_pair = _pair_sum(sc_in, in_sib, core, "pair_w_in")
    d_u1, (in_recv,) = _mm(d_proj, w_p, "nt", F32, "d_u1", tm, tn_d, tn_p, comm=_scatter_ici([in_pair]))
    in_mine = _sum4(in_recv, "sum4_w_in")
    d_h0, g_mixw, in_other = _rms_bwd(h0, mix_norm_w, d_u1, d_h1, "rms1_bwd", comm=_sibling_swap([in_mine]))
    grad_x = d_h0[CHUNK:][None]
    g_meta = d_h0[FRONT:CHUNK]

    names = ["w_in"] + early_names
    mine_half = [in_mine] + early_mine
    other_half = [in_other] + list(early_other)
    big_w = [w_in, w_branch_ssd, w_branch_ret, w_out, w_up, w_down]
    big_m = [m_w_in, m_w_branch_ssd, m_w_branch_ret, m_w_out, m_w_up, m_w_down]
    big_v = [v_w_in, v_w_branch_ssd, v_w_branch_ret, v_w_out, v_w_up, v_w_down]
    big_out = {}
    for nm, w_, p_, s_, m_, v_ in zip(names, big_w, mine_half, other_half, big_m, big_v):
        res = _adam_big(w_[0], p_, s_, m_[0], v_[0], core, "adam_" + nm)
        big_out[nm] = [r[None] for r in res]

    kws, kwf = ssd_conv_w.shape[1], ffn_conv_w.shape[1]
    small_grads = [g_meta, g_mixw, g_scw[:kws], g_scb, g_dtb[:, :nh], g_alog[:, :nh], g_dvec[:, :nh], g_snw, g_ffnw,
                   g_fcw[:kwf], g_fcb, g_fin, loss8[0:1, 0:1]]
    sg_shapes = [g.shape for g in small_grads]
    gparts = _allgather8(_pack(small_grads), "gather_small_grads")

    def own_cols(a, width):
        return lax.dynamic_slice_in_dim(a, chip * width, width, axis=1)

    def widen(a, width_full):
        z = jnp.zeros(a.shape[:-1] + (width_full,), F32)
        return lax.dynamic_update_slice_in_dim(z, a, chip * a.shape[-1], axis=a.ndim - 1)

    def small_slab(meta_, mix_, scw_, scb_, dtb_, alog_, d_, snw_, ffnw_, fcw_, fcb_, fin_):
        return _pack([widen(meta_, d), mix_, widen(scw_[0], cw), scb_, dtb_, alog_, d_, snw_, ffnw_, widen(fcw_[0], 2 * f),
                      fcb_, fin_.reshape(1, d), jnp.zeros((1, 1), F32)])

    w_slab = small_slab(meta_tokens, mix_norm_w, ssd_conv_w, ssd_conv_b, ssd_dt_bias, ssd_A_log, ssd_D, ssd_norm_w,
                        ffn_norm_w, ffn_conv_w, ffn_conv_b, final_norm_w)
    m_slab = small_slab(m_meta_tokens, m_mix_norm_w, m_ssd_conv_w, m_ssd_conv_b, m_ssd_dt_bias, m_ssd_A_log, m_ssd_D,
                        m_ssd_norm_w, m_ffn_norm_w, m_ffn_conv_w, m_ffn_conv_b, m_final_norm_w)
    v_slab = small_slab(v_meta_tokens, v_mix_norm_w, v_ssd_conv_w, v_ssd_conv_b, v_ssd_dt_bias, v_ssd_A_log, v_ssd_D,
                        v_ssd_norm_w, v_ffn_norm_w, v_ffn_conv_w, v_ffn_conv_b, v_final_norm_w)
    small_res = [_unpack(s, sg_shapes) for s in _adam_small(w_slab, gparts, m_slab, v_slab, "adam_small")]
    loss = small_res[0][12].reshape(())

    def small_outputs(vals):
        meta_, mix_, scw_, scb_, dtb_, alog_, d_, snw_, ffnw_, fcw_, fcb_, fin_ = vals[:12]
        return {
            "meta_tokens": own_cols(meta_, d // 4), "mix_norm_w": mix_, "ssd_conv_w": own_cols(scw_, cw // 4)[None],
            "ssd_conv_b": scb_, "ssd_dt_bias": dtb_, "ssd_A_log": alog_, "ssd_D": d_, "ssd_norm_w": snw_,
            "ffn_norm_w": ffnw_, "ffn_conv_w": own_cols(fcw_, 2 * f // 4)[None], "ffn_conv_b": fcb_,
            "final_norm_w": fin_.reshape(d),
        }

    weights = ["meta_tokens", "mix_norm_w", "w_in", "ssd_conv_w", "ssd_conv_b", "ssd_dt_bias", "ssd_A_log", "ssd_D",
               "ssd_norm_w", "w_branch_ssd", "w_branch_ret", "w_out", "ffn_norm_w", "w_up", "ffn_conv_w", "ffn_conv_b",
               "w_down", "final_norm_w"]
    outs = [loss, grad_x]
    for kind in range(4):
        so = small_outputs(small_res[kind])
        for nm in weights:
            outs.append(big_out[nm][kind] if nm in big_out else so[nm])
    return tuple(outs)
```

```python
import functools
import math

import jax
import jax.numpy as jnp
import numpy as np
from jax import lax
from jax.experimental import pallas as pl
from jax.experimental.pallas import tpu as pltpu

F32 = jnp.float32
BF16 = jnp.bfloat16
MXU_DTYPE = BF16
WIRE_DTYPE = BF16

N_META = 16
CHUNK = 128
FRONT = CHUNK - N_META
EPS = 1e-6
SSD_HEAD_DIM = 64
SSD_GROUPS = 4
SSD_STATE = 128
SSD_CONV = 4
RET_HEADS = 4
ROPE_BASE = 10000.0
FFN_CONV = 3
LANES = 128
SUBLANES = 8
VMEM_LIMIT = 56 * 1024 * 1024

ADAM_LR = 0.001
ADAM_B1 = 0.9
ADAM_B2 = 0.999
ADAM_EPS = 1e-08
ADAM_WD = 0.01
ADAM_STEP = 10
MESH = pl.DeviceIdType.MESH


def _params(sem=None, vmem=VMEM_LIMIT):
    return pltpu.CompilerParams(dimension_semantics=sem, vmem_limit_bytes=vmem)


def _pick(n, cands):
    for c in cands:
        if n % c == 0:
            return c
    return n


def _silu(x):
    return x * jax.nn.sigmoid(x)


def _dsilu(x):
    s = jax.nn.sigmoid(x)
    return s * (1.0 + x * (1.0 - s))


def _dot(a, b, dims=(((1,), (0,)), ((), ()))):
    return lax.dot_general(a.astype(MXU_DTYPE), b.astype(MXU_DTYPE), dims, preferred_element_type=F32)


def _dot_nt(a, b):
    return _dot(a, b, (((1,), (1,)), ((), ())))


def _dot_tn(a, b):
    return _dot(a, b, (((0,), (0,)), ((), ())))


def _dot01(a, b, split, npass, dims=(((1,), (0,)), ((), ()))):
    rest = (a if split == "a" else b).astype(F32)
    fixed = (b if split == "a" else a).astype(BF16)
    acc = None
    for p in range(npass):
        piece = rest.astype(BF16)
        ops = (piece, fixed) if split == "a" else (fixed, piece)
        term = lax.dot_general(ops[0], ops[1], dims, preferred_element_type=F32)
        acc = term if acc is None else acc + term
        if p + 1 < npass:
            rest = rest - piece.astype(F32)
    return acc


_NT = (((1,), (1,)), ((), ()))


def _iota(shape, dim):
    return lax.broadcasted_iota(jnp.int32, shape, dim)


def _shift_down(cur, prev8, k):
    if k == 0:
        return cur
    rolled = pltpu.roll(cur, k, 0)
    i8 = _iota((SUBLANES, cur.shape[1]), 0)
    head = jnp.where(i8 < k, pltpu.roll(prev8, k, 0), rolled[0:SUBLANES])
    return jnp.concatenate([head, rolled[SUBLANES:]], axis=0)


def _shift_up(cur, next8, k):
    if k == 0:
        return cur
    n = cur.shape[0]
    rolled = pltpu.roll(cur, n - k, 0)
    i8 = _iota((SUBLANES, cur.shape[1]), 0)
    tail = jnp.where(i8 >= SUBLANES - k, pltpu.roll(next8, SUBLANES - k, 0), rolled[n - SUBLANES:])
    return jnp.concatenate([rolled[:n - SUBLANES], tail], axis=0)


class _Comm:
    def __init__(self, ins, outs, nsem, make, in_place=False):
        self.ins, self.outs, self.nsem, self.make = list(ins), list(outs), nsem, make
        self.in_place = in_place


def _place():
    x, y, c = lax.axis_index("x"), lax.axis_index("y"), lax.axis_index("c")
    return x, y, c, 2 * x + y, [(1 - x, y), (x, 1 - y), (1 - x, 1 - y)]


def _call(body, name, grid, in_specs, out_specs, out_shape, scratch, sem, args, comm=None):
    if comm is None:
        return pl.pallas_call(body, name=name, grid=grid, in_specs=in_specs, out_specs=out_specs, out_shape=out_shape,
                              scratch_shapes=scratch, compiler_params=_params(sem))(*args)
    n_in, n_out, n_scr = len(in_specs), len(out_specs), len(scratch)
    ci, co = len(comm.ins), len(comm.outs)

    def wrapped(*refs):
        ins, refs = refs[:n_in], refs[n_in:]
        cins, refs = refs[:ci], refs[ci:]
        outs, refs = refs[:n_out], refs[n_out:]
        couts, refs = refs[:co], refs[co:]
        scr, sems = refs[:n_scr], refs[n_scr:]
        first = functools.reduce(jnp.logical_and, [pl.program_id(a) == 0 for a in range(len(grid))])
        last = functools.reduce(jnp.logical_and, [pl.program_id(a) == grid[a] - 1 for a in range(len(grid))])

        @pl.when(first)
        def _():
            for cp in comm.make(cins, couts, *sems):
                cp.start()

        body(*ins, *outs, *scr)

        @pl.when(last)
        def _():
            for cp in comm.make(cins, couts, *sems):
                cp.wait()

    anyspec = pl.BlockSpec(memory_space=pl.ANY)
    dma = pltpu.SemaphoreType.DMA((comm.nsem,))
    aliases = {n_in + i: n_out + i for i in range(ci)} if comm.in_place else {}
    return pl.pallas_call(
        wrapped, name=name, grid=grid, in_specs=list(in_specs) + [anyspec] * ci,
        out_specs=list(out_specs) + [anyspec] * co, out_shape=list(out_shape) + comm.outs,
        scratch_shapes=list(scratch) + [dma, dma, dma], input_output_aliases=aliases,
        compiler_params=_params(("arbitrary",) * len(grid)))(*args, *comm.ins)


def _run_comm(comm, name):
    ci, co = len(comm.ins), len(comm.outs)

    def body(*refs):
        cins, couts, sems = refs[:ci], refs[ci:ci + co], refs[ci + co:]
        for cp in comm.make(cins, couts, *sems):
            cp.start()
        for cp in comm.make(cins, couts, *sems):
            cp.wait()

    anyspec = pl.BlockSpec(memory_space=pl.ANY)
    dma = pltpu.SemaphoreType.DMA((comm.nsem,))
    aliases = {i: i for i in range(ci)} if comm.in_place else {}
    return pl.pallas_call(body, name=name, in_specs=[anyspec] * ci, out_specs=[anyspec] * co, out_shape=comm.outs,
                          scratch_shapes=[dma, dma, dma], input_output_aliases=aliases)(*comm.ins)


def _half_rows(c, rows):
    h = rows // 2
    return pl.ds(pl.multiple_of(c * h, 16), h)


def _gather_ici(arrays):
    for a in arrays:
        assert a.shape[0] % 32 == 0, a.shape

    def make(ins, outs, send, recv, loc):
        x, y, c, mine, peers = _place()
        cps = []
        for i, a in enumerate(arrays):
            half = _half_rows(c, a.shape[0])
            for k, (px, py) in enumerate(peers):
                cps.append(pltpu.make_async_remote_copy(
                    src_ref=ins[i].at[half], dst_ref=outs[i].at[mine, half], send_sem=send.at[3 * i + k],
                    recv_sem=recv.at[3 * i + k], device_id=(px, py, c), device_id_type=MESH))
        return cps

    outs = [jax.ShapeDtypeStruct((4,) + a.shape, a.dtype) for a in arrays]
    return _Comm(arrays, outs, 3 * len(arrays), make)


def _gather_d2d(bufs):
    def make(ins, outs, send, recv, loc):
        x, y, c, mine, peers = _place()
        cps = []
        for i, a in enumerate(bufs):
            half = _half_rows(c, a.shape[1])
            for k, (px, py) in enumerate(peers):
                mine_half = outs[i].at[2 * px + py, half]
                cps.append(pltpu.make_async_remote_copy(
                    src_ref=mine_half, dst_ref=mine_half, send_sem=send.at[3 * i + k], recv_sem=recv.at[3 * i + k],
                    device_id=(x, y, 1 - c), device_id_type=MESH))
        return cps

    outs = [jax.ShapeDtypeStruct(a.shape, a.dtype) for a in bufs]
    return _Comm(bufs, outs, 3 * len(bufs), make, in_place=True)


def _with_own(gathered, own, chip):
    return lax.dynamic_update_index_in_dim(gathered, own, chip, 0)


def _scatter_d2d(grads):
    for a in grads:
        assert a.shape[1] % 32 == 0, a.shape

    def make(ins, outs, send, recv, loc):
        x, y, c, mine, peers = _place()
        cps = []
        for i, a in enumerate(grads):
            other = _half_rows(1 - c, a.shape[1])
            cps.append(pltpu.make_async_remote_copy(
                src_ref=ins[i].at[:, other], dst_ref=outs[i], send_sem=send.at[i], recv_sem=recv.at[i],
                device_id=(x, y, 1 - c), device_id_type=MESH))
        return cps

    outs = [jax.ShapeDtypeStruct((4, a.shape[1] // 2, a.shape[2]), a.dtype) for a in grads]
    return _Comm(grads, outs, len(grads), make)


def _scatter_ici(parts):
    def make(ins, outs, send, recv, loc):
        x, y, c, mine, peers = _place()
        cps = []
        for i in range(len(parts)):
            cps.append(pltpu.make_async_copy(ins[i].at[mine], outs[i].at[mine], loc.at[i]))
            for k, (px, py) in enumerate(peers):
                cps.append(pltpu.make_async_remote_copy(
                    src_ref=ins[i].at[2 * px + py], dst_ref=outs[i].at[mine], send_sem=send.at[3 * i + k],
                    recv_sem=recv.at[3 * i + k], device_id=(px, py, c), device_id_type=MESH))
        return cps

    outs = [jax.ShapeDtypeStruct(a.shape, a.dtype) for a in parts]
    return _Comm(parts, outs, 3 * len(parts), make)


def _sibling_swap(arrays):
    def make(ins, outs, send, recv, loc):
        x, y, c, mine, peers = _place()
        return [pltpu.make_async_remote_copy(src_ref=ins[i], dst_ref=outs[i], send_sem=send.at[i], recv_sem=recv.at[i],
                                             device_id=(x, y, 1 - c), device_id_type=MESH) for i in range(len(arrays))]

    outs = [jax.ShapeDtypeStruct(a.shape, a.dtype) for a in arrays]
    return _Comm(arrays, outs, len(arrays), make)


def _mm(a, b, mode, out_dtype, name, tm, tn, tk, res=None, comm=None):
    if mode == "nn":
        (m, kd), n = a.shape, b.shape[1]
        a_spec = pl.BlockSpec((tm, tk), lambda i, j, k: (i, k))
        b_spec = pl.BlockSpec((tk, tn), lambda i, j, k: (k, j))
        dims = (((1,), (0,)), ((), ()))
    elif mode == "nt":
        (m, kd), n = a.shape, b.shape[0]
        a_spec = pl.BlockSpec((tm, tk), lambda i, j, k: (i, k))
        b_spec = pl.BlockSpec((tn, tk), lambda i, j, k: (j, k))
        dims = (((1,), (1,)), ((), ()))
    else:
        (kd, m), n = a.shape, b.shape[1]
        a_spec = pl.BlockSpec((tk, tm), lambda i, j, k: (k, i))
        b_spec = pl.BlockSpec((tk, tn), lambda i, j, k: (k, j))
        dims = (((0,), (0,)), ((), ()))
    assert m % tm == 0 and n % tn == 0 and kd % tk == 0, (name, m, n, kd, tm, tn, tk)
    nk = kd // tk
    has_res = res is not None

    def body(*refs):
        if has_res:
            a_ref, b_ref, r_ref, o_ref, acc = refs
        else:
            a_ref, b_ref, o_ref, acc = refs
        k = pl.program_id(2)

        @pl.when(k == 0)
        def _():
            acc[...] = jnp.zeros_like(acc)

        acc[...] += _dot(a_ref[...], b_ref[...], dims)

        @pl.when(k == nk - 1)
        def _():
            r = acc[...]
            if has_res:
                r = r + r_ref[...].astype(F32)
            o_ref[...] = r.astype(out_dtype)

    in_specs = [a_spec, b_spec]
    args = [a, b]
    if has_res:
        in_specs.append(pl.BlockSpec((tm, tn), lambda i, j, k: (i, j)))
        args.append(res)
    res = _call(body, name, (m // tm, n // tn, nk), in_specs, [pl.BlockSpec((tm, tn), lambda i, j, k: (i, j))],
                [jax.ShapeDtypeStruct((m, n), out_dtype)], [pltpu.VMEM((tm, tn), F32)],
                ("parallel", "parallel", "arbitrary"), args, comm)
    return res[0] if comm is None else (res[0], res[1:])


def _mm_pieces_nt(pieces, b, out_dtype, name, tm, tn, comm=None):
    m, n = pieces[0][0].shape[0], b.shape[0]
    tks = [_pick(math.gcd(a.shape[1], c0) if c0 else a.shape[1], (512, 256, 128)) for a, c0 in pieces]
    nks = [a.shape[1] // tk for (a, _), tk in zip(pieces, tks)]
    starts = [sum(nks[:p]) for p in range(len(pieces))]
    ktot = sum(nks)
    npc = len(pieces)

    def body(*refs):
        a_refs, b_refs, o_ref, acc = refs[:npc], refs[npc:2 * npc], refs[2 * npc], refs[2 * npc + 1]
        k = pl.program_id(2)

        @pl.when(k == 0)
        def _():
            acc[...] = jnp.zeros_like(acc)

        for p in range(npc):
            @pl.when((k >= starts[p]) & (k < starts[p] + nks[p]))
            def _(p=p):
                acc[...] += _dot_nt(a_refs[p][...], b_refs[p][...])

        @pl.when(k == ktot - 1)
        def _():
            o_ref[...] = acc[...].astype(out_dtype)

    def a_spec(p):
        return pl.BlockSpec((tm, tks[p]), lambda i, j, k: (i, jnp.clip(k - starts[p], 0, nks[p] - 1)))

    def b_spec(p):
        c0 = pieces[p][1] // tks[p]
        return pl.BlockSpec((tn, tks[p]), lambda i, j, k: (j, c0 + jnp.clip(k - starts[p], 0, nks[p] - 1)))

    res = _call(body, name, (m // tm, n // tn, ktot), [a_spec(p) for p in range(npc)] + [b_spec(p) for p in range(npc)],
                [pl.BlockSpec((tm, tn), lambda i, j, k: (i, j))], [jax.ShapeDtypeStruct((m, n), out_dtype)],
                [pltpu.VMEM((tm, tn), F32)], ("parallel", "parallel", "arbitrary"),
                [a for a, _ in pieces] + [b] * npc, comm)
    return res[0] if comm is None else (res[0], res[1:])


def _rms_fwd(h, w, name):
    t, d = h.shape
    tr = _pick(t, (640, 512, 384, 256, 128))

    def body(h_ref, w_ref, u_ref):
        x = h_ref[...]
        r = lax.rsqrt(jnp.mean(x * x, axis=1, keepdims=True) + EPS)
        u_ref[...] = (x * r * w_ref[...]).astype(MXU_DTYPE)

    return pl.pallas_call(
        body, name=name, grid=(t // tr,),
        in_specs=[pl.BlockSpec((tr, d), lambda i: (i, 0)), pl.BlockSpec((1, d), lambda i: (0, 0))],
        out_specs=pl.BlockSpec((tr, d), lambda i: (i, 0)),
        out_shape=jax.ShapeDtypeStruct((t, d), MXU_DTYPE),
        compiler_params=_params(("parallel",)),
    )(h, w)


def _rms_bwd(h, w, du, res, name, comm=None):
    t, d = h.shape
    tr = _pick(t, (640, 512, 384, 256, 128))

    def body(h_ref, w_ref, du_ref, res_ref, dh_ref, gw_ref):
        @pl.when(pl.program_id(0) == 0)
        def _():
            gw_ref[...] = jnp.zeros_like(gw_ref)

        x = h_ref[...]
        r = lax.rsqrt(jnp.mean(x * x, axis=1, keepdims=True) + EPS)
        xhat = x * r
        dy = du_ref[...].astype(F32)
        dxh = dy * w_ref[...]
        dh = r * (dxh - xhat * jnp.mean(dxh * xhat, axis=1, keepdims=True))
        dh_ref[...] = dh + res_ref[...]
        gw_ref[...] += jnp.sum(dy * xhat, axis=0, keepdims=True)

    row = pl.BlockSpec((tr, d), lambda i: (i, 0))
    vec = pl.BlockSpec((1, d), lambda i: (0, 0))
    return _call(body, name, (t // tr,), [row, vec, row, row], [row, vec],
                 [jax.ShapeDtypeStruct((t, d), F32), jax.ShapeDtypeStruct((1, d), F32)], [], ("arbitrary",),
                 (h, w, du, res), comm)


def _loss_bwd(h2, w, target, name):
    t, d = h2.shape
    nc = t // CHUNK

    def body(h_ref, w_ref, tg_ref, loss_ref, dh_ref, gw_ref):
        i = pl.program_id(0)

        @pl.when(i == 0)
        def _():
            gw_ref[...] = jnp.zeros_like(gw_ref)
            loss_ref[...] = jnp.zeros_like(loss_ref)
            dh_ref[...] = jnp.zeros_like(dh_ref)

        @pl.when(i > 0)
        def _():
            x = h_ref[...]
            r = lax.rsqrt(jnp.mean(x * x, axis=1, keepdims=True) + EPS)
            xhat = x * r
            diff = xhat * w_ref[...] - tg_ref[...]
            loss_ref[...] += 0.5 * jnp.sum(jnp.sum(diff * diff, axis=1, keepdims=True), axis=0, keepdims=True) / d
            dy = diff / d
            dxh = dy * w_ref[...]
            dh_ref[...] = r * (dxh - xhat * jnp.mean(dxh * xhat, axis=1, keepdims=True))
            gw_ref[...] += jnp.sum(dy * xhat, axis=0, keepdims=True)

    row = pl.BlockSpec((CHUNK, d), lambda i: (i, 0))
    vec = pl.BlockSpec((1, d), lambda i: (0, 0))
    return pl.pallas_call(
        body, name=name, grid=(nc,),
        in_specs=[row, vec, pl.BlockSpec((CHUNK, d), lambda i: (jnp.maximum(i - 1, 0), 0))],
        out_specs=[pl.BlockSpec((SUBLANES, LANES), lambda i: (0, 0)), row, vec],
        out_shape=[jax.ShapeDtypeStruct((SUBLANES, LANES), F32), jax.ShapeDtypeStruct((t, d), F32),
                   jax.ShapeDtypeStruct((1, d), F32)],
        compiler_params=_params(("arbitrary",)),
    )(h2, w, target)


_SUB = 16


def _sub_rows(s):
    return pl.ds(0 if isinstance(s, int) else pl.multiple_of(s * _SUB, _SUB), _SUB)


def _window(x_ref, prev8, s):
    if isinstance(s, int):
        return jnp.concatenate([prev8, x_ref[0:_SUB, :]], axis=0)
    return x_ref[pl.ds(pl.multiple_of(s * _SUB - SUBLANES, SUBLANES), _SUB + SUBLANES), :]


def _conv_step(win, w, b, kw):
    taps = [win[SUBLANES:] if k == kw - 1 else pltpu.roll(win, kw - 1 - k, 0)[SUBLANES:] for k in range(kw)]
    y = b + taps[kw - 1] * w[kw - 1:kw, :]
    for k in range(kw - 1):
        y = y + taps[k] * w[k:k + 1, :]
    return y, taps


def _conv_dx_step(dpre, next8, w, kw):
    n = _SUB + SUBLANES
    win = jnp.concatenate([dpre, next8], axis=0)
    acc = dpre * w[kw - 1:kw, :]
    for k in range(kw - 1):
        acc = acc + pltpu.roll(win, n - (kw - 1 - k), 0)[0:_SUB] * w[k:k + 1, :]
    return acc


def _fold8(v):
    return functools.reduce(jnp.add, [v[r:r + SUBLANES] for r in range(0, _SUB, SUBLANES)])


def _row_tile(t):
    return _pick(t, (640, 512, 384, 256, 128))


def _ssd_conv_fwd(proj, col0, width, w, b, name):
    t = proj.shape[0]
    kw = w.shape[0]
    tr, tc = _row_tile(t), _pick(width, (512, 256, 128))
    c0, rb = col0 // tc, tr // SUBLANES
    assert col0 % tc == 0

    def body(x_ref, p_ref, w_ref, b_ref, o_ref):
        i = pl.program_id(1)
        w, b = w_ref[...], b_ref[...]
        prev8 = jnp.where(i > 0, p_ref[...], 0.0)

        def step(s, carry):
            pre, _ = _conv_step(_window(x_ref, prev8, s), w, b, kw)
            rows = _iota((_SUB, 1), 0) + (i * tr + s * _SUB)
            o_ref[_sub_rows(s), :] = jnp.where(rows >= FRONT, _silu(pre), 0.0)
            return carry

        lax.fori_loop(1, tr // _SUB, step, step(0, 0))

    return pl.pallas_call(
        body, name=name, grid=(width // tc, t // tr),
        in_specs=[pl.BlockSpec((tr, tc), lambda j, i: (i, c0 + j)),
                  pl.BlockSpec((SUBLANES, tc), lambda j, i: (jnp.maximum(i * rb - 1, 0), c0 + j)),
                  pl.BlockSpec((kw, tc), lambda j, i: (0, j)),
                  pl.BlockSpec((1, tc), lambda j, i: (0, j))],
        out_specs=pl.BlockSpec((tr, tc), lambda j, i: (i, j)),
        out_shape=jax.ShapeDtypeStruct((t, width), F32),
        compiler_params=_params(("parallel", "parallel")),
    )(proj, proj, w, b)


def _ssd_conv_bwd(proj, col0, width, w, b, dact, name, comm=None):
    t = proj.shape[0]
    kw = w.shape[0]
    tr, tc = _row_tile(t), _pick(width, (512, 256, 128))
    c0, rb, nrow = col0 // tc, tr // SUBLANES, t // tr

    def body(x_ref, p_ref, w_ref, b_ref, d_ref, o_ref, gw_ref, gb_ref, carry):
        i = pl.program_id(1)
        ti = nrow - 1 - i

        @pl.when(i == 0)
        def _():
            gw_ref[...] = jnp.zeros_like(gw_ref)
            gb_ref[...] = jnp.zeros_like(gb_ref)
            carry[...] = jnp.zeros_like(carry)

        w, b = w_ref[...], b_ref[...]
        prev8 = jnp.where(ti > 0, p_ref[...], 0.0)
        nsub = tr // _SUB

        def step(s, state):
            next8, gb8, gw8 = state
            pre, taps = _conv_step(_window(x_ref, prev8, s), w, b, kw)
            valid = _iota((_SUB, 1), 0) + (ti * tr + s * _SUB) >= FRONT
            dpre = jnp.where(valid, d_ref[_sub_rows(s), :] * _dsilu(pre), 0.0)
            o_ref[_sub_rows(s), :] = jnp.where(valid, _conv_dx_step(dpre, next8, w, kw), 0.0).astype(MXU_DTYPE)
            return (dpre[0:SUBLANES], gb8 + _fold8(dpre), tuple(g + _fold8(dpre * tp) for g, tp in zip(gw8, taps)))

        zero8 = jnp.zeros((SUBLANES, tc), F32)
        state = lax.fori_loop(0, nsub - 1, lambda n, st: step(nsub - 1 - n, st), (carry[...], zero8, (zero8,) * kw))
        next8, gb8, gw8 = step(0, state)
        carry[...] = next8
        gb_ref[...] += jnp.sum(gb8, axis=0, keepdims=True)
        for k in range(kw):
            gw_ref[k:k + 1, :] += jnp.sum(gw8[k], axis=0, keepdims=True)

    return _call(
        body, name, (width // tc, nrow),
        [pl.BlockSpec((tr, tc), lambda j, i: (nrow - 1 - i, c0 + j)),
         pl.BlockSpec((SUBLANES, tc), lambda j, i: (jnp.maximum((nrow - 1 - i) * rb - 1, 0), c0 + j)),
         pl.BlockSpec((kw, tc), lambda j, i: (0, j)),
         pl.BlockSpec((1, tc), lambda j, i: (0, j)),
         pl.BlockSpec((tr, tc), lambda j, i: (nrow - 1 - i, j))],
        [pl.BlockSpec((tr, tc), lambda j, i: (nrow - 1 - i, j)),
         pl.BlockSpec((SUBLANES, tc), lambda j, i: (0, j)),
         pl.BlockSpec((1, tc), lambda j, i: (0, j))],
        [jax.ShapeDtypeStruct((t, width), MXU_DTYPE), jax.ShapeDtypeStruct((SUBLANES, width), F32),
         jax.ShapeDtypeStruct((1, width), F32)],
        [pltpu.VMEM((SUBLANES, tc), F32)], ("parallel", "arbitrary"), (proj, proj, w, b, dact), comm)


def _ffn_conv_fwd(up, w, b, name):
    t, f2 = up.shape
    f = f2 // 2
    kw = w.shape[0]
    tr, tc = _row_tile(t), _pick(f, (256, 128))
    nf, rb = f // tc, tr // SUBLANES

    def body(xg, pg, xv, pv, wg_ref, wv_ref, bg_ref, bv_ref, o_ref):
        i = pl.program_id(1)
        wg, wv, bg, bv = wg_ref[...], wv_ref[...], bg_ref[...], bv_ref[...]
        p8g, p8v = jnp.where(i > 0, pg[...], 0.0), jnp.where(i > 0, pv[...], 0.0)

        def step(s, carry):
            ag, _ = _conv_step(_window(xg, p8g, s), wg, bg, kw)
            av, _ = _conv_step(_window(xv, p8v, s), wv, bv, kw)
            o_ref[_sub_rows(s), :] = (_silu(ag) * av).astype(MXU_DTYPE)
            return carry

        lax.fori_loop(1, tr // _SUB, step, step(0, 0))

    def cur(off):
        return pl.BlockSpec((tr, tc), lambda j, i: (i, j + off))

    def prev(off):
        return pl.BlockSpec((SUBLANES, tc), lambda j, i: (jnp.maximum(i * rb - 1, 0), j + off))

    def par(rows, off):
        return pl.BlockSpec((rows, tc), lambda j, i: (0, j + off))

    return pl.pallas_call(
        body, name=name, grid=(nf, t // tr),
        in_specs=[cur(0), prev(0), cur(nf), prev(nf), par(kw, 0), par(kw, nf), par(1, 0), par(1, nf)],
        out_specs=pl.BlockSpec((tr, tc), lambda j, i: (i, j)),
        out_shape=jax.ShapeDtypeStruct((t, f), MXU_DTYPE),
        compiler_params=_params(("parallel", "parallel")),
    )(up, up, up, up, w, w, b, b)


def _ffn_conv_bwd(up, w, b, dact, name):
    t, f2 = up.shape
    f = f2 // 2
    kw = w.shape[0]
    tr, tc = _row_tile(t), _pick(f, (256, 128))
    nf, rb, nrow = f // tc, tr // SUBLANES, t // tr

    def body(xg, pg, xv, pv, wg_ref, wv_ref, bg_ref, bv_ref, d_ref, og_ref, ov_ref, gwg_ref, gwv_ref, gbg_ref, gbv_ref,
             cg, cv):
        i = pl.program_id(1)
        ti = nrow - 1 - i

        @pl.when(i == 0)
        def _():
            for r in (gwg_ref, gwv_ref, gbg_ref, gbv_ref, cg, cv):
                r[...] = jnp.zeros_like(r)

        wg, wv, bg, bv = wg_ref[...], wv_ref[...], bg_ref[...], bv_ref[...]
        p8g, p8v = jnp.where(ti > 0, pg[...], 0.0), jnp.where(ti > 0, pv[...], 0.0)
        nsub = tr // _SUB

        def step(s, state):
            ng, nv, gbg8, gbv8, gwg8, gwv8 = state
            ag, tg = _conv_step(_window(xg, p8g, s), wg, bg, kw)
            av, tv = _conv_step(_window(xv, p8v, s), wv, bv, kw)
            d = d_ref[_sub_rows(s), :]
            sg = jax.nn.sigmoid(ag)
            dag = d * av * (sg * (1.0 + ag * (1.0 - sg)))
            dav = d * (ag * sg)
            valid = _iota((_SUB, 1), 0) + (ti * tr + s * _SUB) >= FRONT
            og_ref[_sub_rows(s), :] = jnp.where(valid, _conv_dx_step(dag, ng, wg, kw), 0.0).astype(MXU_DTYPE)
            ov_ref[_sub_rows(s), :] = jnp.where(valid, _conv_dx_step(dav, nv, wv, kw), 0.0).astype(MXU_DTYPE)
            return (dag[0:SUBLANES], dav[0:SUBLANES], gbg8 + _fold8(dag), gbv8 + _fold8(dav),
                    tuple(g + _fold8(dag * tp) for g, tp in zip(gwg8, tg)),
                    tuple(g + _fold8(dav * tp) for g, tp in zip(gwv8, tv)))

        zero8 = jnp.zeros((SUBLANES, tc), F32)
        state = lax.fori_loop(0, nsub - 1, lambda n, st: step(nsub - 1 - n, st),
                              (cg[...], cv[...], zero8, zero8, (zero8,) * kw, (zero8,) * kw))
        ng, nv, gbg8, gbv8, gwg8, gwv8 = step(0, state)
        cg[...] = ng
        cv[...] = nv
        gbg_ref[...] += jnp.sum(gbg8, axis=0, keepdims=True)
        gbv_ref[...] += jnp.sum(gbv8, axis=0, keepdims=True)
        for k in range(kw):
            gwg_ref[k:k + 1, :] += jnp.sum(gwg8[k], axis=0, keepdims=True)
            gwv_ref[k:k + 1, :] += jnp.sum(gwv8[k], axis=0, keepdims=True)

    def cur(off):
        return pl.BlockSpec((tr, tc), lambda j, i: (nrow - 1 - i, j + off))

    def prev(off):
        return pl.BlockSpec((SUBLANES, tc), lambda j, i: (jnp.maximum((nrow - 1 - i) * rb - 1, 0), j + off))

    def par(rows, off):
        return pl.BlockSpec((rows, tc), lambda j, i: (0, j + off))

    acc8 = pl.BlockSpec((SUBLANES, tc), lambda j, i: (0, j))
    acc1 = pl.BlockSpec((1, tc), lambda j, i: (0, j))
    return pl.pallas_call(
        body, name=name, grid=(nf, nrow),
        in_specs=[cur(0), prev(0), cur(nf), prev(nf), par(kw, 0), par(kw, nf), par(1, 0), par(1, nf), cur(0)],
        out_specs=[cur(0), cur(0), acc8, acc8, acc1, acc1],
        out_shape=[jax.ShapeDtypeStruct((t, f), MXU_DTYPE), jax.ShapeDtypeStruct((t, f), MXU_DTYPE),
                   jax.ShapeDtypeStruct((SUBLANES, f), F32), jax.ShapeDtypeStruct((SUBLANES, f), F32),
                   jax.ShapeDtypeStruct((1, f), F32), jax.ShapeDtypeStruct((1, f), F32)],
        scratch_shapes=[pltpu.VMEM((SUBLANES, tc), F32), pltpu.VMEM((SUBLANES, tc), F32)],
        compiler_params=_params(("parallel", "arbitrary")),
    )(up, up, up, up, w, w, b, b, dact)


def _gate_fwd(bs, br, proj, c_gs, c_gr, name):
    t, d = bs.shape
    tr = _row_tile(t)

    def body(bs_ref, br_ref, gs_ref, gr_ref, o_ref):
        o_ref[...] = (jax.nn.sigmoid(gs_ref[...]) * bs_ref[...] + jax.nn.sigmoid(gr_ref[...]) * br_ref[...]).astype(MXU_DTYPE)

    row = pl.BlockSpec((tr, d), lambda i: (i, 0))
    return pl.pallas_call(
        body, name=name, grid=(t // tr,),
        in_specs=[row, row, pl.BlockSpec((tr, d), lambda i: (i, c_gs // d)), pl.BlockSpec((tr, d), lambda i: (i, c_gr // d))],
        out_specs=row, out_shape=jax.ShapeDtypeStruct((t, d), MXU_DTYPE),
        compiler_params=_params(("parallel",)),
    )(bs, br, proj, proj)


def _gate_bwd(dm, bs, br, proj, c_gs, c_gr, name):
    t, d = bs.shape
    tr = _row_tile(t)

    def body(dm_ref, bs_ref, br_ref, gs_ref, gr_ref, dbs_ref, dbr_ref, dgs_ref, dgr_ref):
        g = dm_ref[...]
        ss, sr = jax.nn.sigmoid(gs_ref[...]), jax.nn.sigmoid(gr_ref[...])
        dbs_ref[...] = (g * ss).astype(MXU_DTYPE)
        dbr_ref[...] = (g * sr).astype(MXU_DTYPE)
        dgs_ref[...] = (g * bs_ref[...] * ss * (1.0 - ss)).astype(MXU_DTYPE)
        dgr_ref[...] = (g * br_ref[...] * sr * (1.0 - sr)).astype(MXU_DTYPE)

    row = pl.BlockSpec((tr, d), lambda i: (i, 0))
    out = jax.ShapeDtypeStruct((t, d), MXU_DTYPE)
    return pl.pallas_call(
        body, name=name, grid=(t // tr,),
        in_specs=[row, row, row, pl.BlockSpec((tr, d), lambda i: (i, c_gs // d)), pl.BlockSpec((tr, d), lambda i: (i, c_gr // d))],
        out_specs=[row] * 4, out_shape=[out] * 4,
        compiler_params=_params(("parallel",)),
    )(dm, bs, br, proj, proj)


def _ret_consts(h):
    lg = math.log(1.0 - 2.0 ** (-5.0 - h))
    l = _iota((CHUNK, 1), 0).astype(F32)
    diff = l - _iota((1, CHUNK), 1).astype(F32)
    dm = jnp.exp(jnp.where(diff >= 0, diff * lg, -jnp.inf))
    dmt = jnp.exp(jnp.where(diff <= 0, -diff * lg, -jnp.inf))
    cs = jnp.exp((l + 1.0) * lg)
    kdec = jnp.exp((CHUNK - 1.0 - l) * lg)
    return dm, dmt, cs, kdec, math.exp(CHUNK * lg)


def _ret_fwd(proj, c_q, c_k, c_v, c_g, cos, sin, d, name, comm=None):
    t = proj.shape[0]
    nc = t // CHUNK
    hq, hv = d // RET_HEADS, 2 * d // RET_HEADS
    half = hq // 2
    scale = hq ** -0.5

    def body(q_ref, k_ref, v_ref, g_ref, cos_ref, sin_ref, o_ref, y_ref, qr_ref, kr_ref, st_ref, rs):
        @pl.when(pl.program_id(0) == 0)
        def _():
            rs[...] = jnp.zeros_like(rs)

        co, si = cos_ref[...], sin_ref[...]
        for h in range(RET_HEADS):
            dm, _, cs, kdec, gam = _ret_consts(h)
            q1, q2 = q_ref[:, h * hq:h * hq + half], q_ref[:, h * hq + half:(h + 1) * hq]
            k1, k2 = k_ref[:, h * hq:h * hq + half], k_ref[:, h * hq + half:(h + 1) * hq]
            qr = jnp.concatenate([q1 * co - q2 * si, q2 * co + q1 * si], axis=1)
            kr = jnp.concatenate([k1 * co - k2 * si, k2 * co + k1 * si], axis=1) * scale
            qr_ref[:, h * hq:(h + 1) * hq] = qr.astype(MXU_DTYPE)
            kr_ref[:, h * hq:(h + 1) * hq] = kr.astype(MXU_DTYPE)
            v = v_ref[:, h * hv:(h + 1) * hv]
            r_in = rs[h * hq:(h + 1) * hq, :]
            st_ref[0, h * hq:(h + 1) * hq, :] = r_in.astype(MXU_DTYPE)
            s = _dot_nt(qr, kr) * dm
            o = _dot(s, v) + cs * _dot(qr, r_in)
            rs[h * hq:(h + 1) * hq, :] = gam * r_in + _dot_tn(kr * kdec, v)
            o_ref[:, h * hv:(h + 1) * hv] = o
            on = o * lax.rsqrt(jnp.mean(o * o, axis=1, keepdims=True) + EPS)
            y_ref[:, h * hv:(h + 1) * hv] = (_silu(g_ref[:, h * hv:(h + 1) * hv]) * on).astype(MXU_DTYPE)

    def col(width, c0):
        return pl.BlockSpec((CHUNK, width), lambda i: (i, c0 // width))

    tab = pl.BlockSpec((CHUNK, half), lambda i: (i, 0))
    return _call(
        body, name, (nc,),
        [col(d, c_q), col(d, c_k), col(2 * d, c_v), col(2 * d, c_g), tab, tab],
        [col(2 * d, 0), col(2 * d, 0), col(d, 0), col(d, 0), pl.BlockSpec((1, d, hv), lambda i: (i, 0, 0))],
        [jax.ShapeDtypeStruct((t, 2 * d), F32), jax.ShapeDtypeStruct((t, 2 * d), MXU_DTYPE),
         jax.ShapeDtypeStruct((t, d), MXU_DTYPE), jax.ShapeDtypeStruct((t, d), MXU_DTYPE),
         jax.ShapeDtypeStruct((nc, d, hv), MXU_DTYPE)],
        [pltpu.VMEM((d, hv), F32)], ("arbitrary",), (proj, proj, proj, proj, cos, sin), comm)


def _ret_bwd(dy, proj, c_v, c_g, o, qr, kr, st, cos, sin, d, name, comm=None):
    t = proj.shape[0]
    nc = t // CHUNK
    hq, hv = d // RET_HEADS, 2 * d // RET_HEADS
    half = hq // 2
    scale = hq ** -0.5

    def body(dy_ref, v_ref, g_ref, o_ref, qr_ref, kr_ref, st_ref, cos_ref, sin_ref,
             dq_ref, dk_ref, dv_ref, dg_ref, drs):
        @pl.when(pl.program_id(0) == 0)
        def _():
            drs[...] = jnp.zeros_like(drs)

        co, si = cos_ref[...], sin_ref[...]
        for h in range(RET_HEADS):
            dm, dmt, cs, kdec, gam = _ret_consts(h)
            vs = slice(h * hv, (h + 1) * hv)
            qs = slice(h * hq, (h + 1) * hq)
            o_h = o_ref[:, vs]
            g_h = g_ref[:, vs]
            d_y = dy_ref[:, vs]
            r = lax.rsqrt(jnp.mean(o_h * o_h, axis=1, keepdims=True) + EPS)
            on = o_h * r
            d_on = d_y * _silu(g_h)
            dg_ref[:, vs] = (d_y * on * _dsilu(g_h)).astype(MXU_DTYPE)
            d_o = r * (d_on - on * jnp.mean(d_on * on, axis=1, keepdims=True))
            q_h, k_h, v_h = qr_ref[:, qs], kr_ref[:, qs], v_ref[:, vs]
            r_in = st_ref[0, qs, :]
            dr_n = drs[qs, :]
            csdo = cs * d_o
            ds = _dot_nt(d_o, v_h) * dm
            dst = _dot_nt(v_h, d_o) * dmt
            s_t = _dot_nt(k_h, q_h) * dmt
            dqr = _dot(ds, k_h) + _dot_nt(csdo, r_in)
            dkr = _dot(dst, q_h) + kdec * _dot_nt(v_h, dr_n)
            dv_ref[:, vs] = (_dot(s_t, d_o) + _dot(k_h.astype(F32) * kdec, dr_n)).astype(MXU_DTYPE)
            drs[qs, :] = gam * dr_n + _dot_tn(q_h, csdo)
            a1, a2 = dqr[:, :half], dqr[:, half:]
            dq_ref[:, qs] = jnp.concatenate([a1 * co + a2 * si, a2 * co - a1 * si], axis=1).astype(MXU_DTYPE)
            b1, b2 = dkr[:, :half] * scale, dkr[:, half:] * scale
            dk_ref[:, qs] = jnp.concatenate([b1 * co + b2 * si, b2 * co - b1 * si], axis=1).astype(MXU_DTYPE)

    def col(width, c0=0):
        return pl.BlockSpec((CHUNK, width), lambda i: (nc - 1 - i, c0 // width))

    tab = pl.BlockSpec((CHUNK, half), lambda i: (nc - 1 - i, 0))
    return _call(
        body, name, (nc,),
        [col(2 * d), col(2 * d, c_v), col(2 * d, c_g), col(2 * d), col(d), col(d),
         pl.BlockSpec((1, d, hv), lambda i: (nc - 1 - i, 0, 0)), tab, tab],
        [col(d), col(d), col(2 * d), col(2 * d)],
        [jax.ShapeDtypeStruct((t, d), MXU_DTYPE), jax.ShapeDtypeStruct((t, d), MXU_DTYPE),
         jax.ShapeDtypeStruct((t, 2 * d), MXU_DTYPE), jax.ShapeDtypeStruct((t, 2 * d), MXU_DTYPE)],
        [pltpu.VMEM((d, hv), F32)], ("arbitrary",), (dy, proj, proj, o, qr, kr, st, cos, sin), comm)


def _ssd_small(dtraw_ref, dtb_ref, alog_ref, chunk_idx, nh):
    rows = _iota((CHUNK, 1), 0)
    ok = ((rows >= FRONT) | (chunk_idx > 0)) & (_iota((1, LANES), 1) < nh)
    z = dtraw_ref[...] + dtb_ref[...]
    dt = jnp.where(ok, jax.nn.softplus(z), 0.0)
    sig = jnp.where(ok, jax.nn.sigmoid(z), 0.0)
    a = jnp.where(_iota((1, LANES), 1) < nh, -jnp.exp(alog_ref[...]), 0.0)
    tri = (_iota((CHUNK, CHUNK), 0) >= _iota((CHUNK, CHUNK), 1)).astype(F32)
    acs = _dot01(tri, dt * a, "b", 3)
    return dt, sig, a, acs, acs.T


def _head_expand(g, hpg, gw):
    shift = int(math.log2(SSD_HEAD_DIM))
    return (_iota((LANES, gw), 0) == g * hpg + lax.shift_right_logical(_iota((LANES, gw), 1), shift)).astype(F32)


def _ssd_fwd(xa, proj, c_dt, c_z, dtb, alog, dvec, nw, di, name, comm=None):
    t = xa.shape[0]
    nc = t // CHUNK
    nh = di // SSD_HEAD_DIM
    hpg = nh // SSD_GROUPS
    gw = di // SSD_GROUPS
    n = SSD_STATE
    gn = SSD_GROUPS * n
    hd = SSD_HEAD_DIM

    def body(x_ref, b_ref, c_ref, dtraw_ref, z_ref, dtb_ref, alog_ref, d_ref, nw_ref,
             y_ref, ys_ref, st_ref, hts, xdt_s):
        c = pl.program_id(0)

        @pl.when(c == 0)
        def _():
            hts[...] = jnp.zeros_like(hts)

        dt, _, _, acs, acs_t = _ssd_small(dtraw_ref, dtb_ref, alog_ref, c, nh)
        tri = _iota((CHUNK, CHUNK), 0) >= _iota((CHUNK, CHUNK), 1)
        dvec8 = jnp.broadcast_to(d_ref[...], (SUBLANES, LANES))
        for g in range(SSD_GROUPS):
            gs = slice(g * gw, (g + 1) * gw)
            ns = slice(g * n, (g + 1) * n)
            e_mat = _head_expand(g, hpg, gw)
            ax = _dot01(acs, e_mat, "a", 3)
            dtx = _dot01(dt, e_mat, "a", 3)
            dx = _dot01(dvec8, e_mat, "a", 3)[0:1, :]
            xg, bg, cg = x_ref[:, gs], b_ref[:, ns], c_ref[:, ns]
            xdt = xg * dtx
            xdt_s[...] = xdt.astype(MXU_DTYPE)
            cb = _dot_nt(cg, bg)
            ht = hts[ns, :]
            st_ref[0, ns, :] = ht.astype(MXU_DTYPE)
            y_ref[:, gs] = jnp.exp(ax) * _dot(cg, ht) + dx * xg
            for hh in range(hpg):
                h = g * hpg + hh
                lmat = jnp.exp(jnp.where(tri, acs[:, h:h + 1] - acs_t[h:h + 1, :], -jnp.inf))
                hs = slice(g * gw + hh * hd, g * gw + (hh + 1) * hd)
                y_ref[:, hs] += _dot(cb * lmat, xdt_s[:, hh * hd:(hh + 1) * hd])
            aend = ax[CHUNK - 1:CHUNK, :]
            hts[ns, :] = jnp.exp(aend) * ht + _dot_tn(bg, xdt * jnp.exp(aend - ax))
        for g in range(SSD_GROUPS):
            gs = slice(g * gw, (g + 1) * gw)
            yz = y_ref[:, gs] * _silu(z_ref[:, gs])
            r = lax.rsqrt(jnp.mean(yz * yz, axis=1, keepdims=True) + EPS)
            ys_ref[:, gs] = (yz * r * nw_ref[:, gs]).astype(MXU_DTYPE)

    def col(width, c0, arr_is_xa=False):
        return pl.BlockSpec((CHUNK, width), lambda i: (i, c0 // width))

    vec = pl.BlockSpec((1, LANES), lambda i: (0, 0))
    assert di % gn == 0 and c_dt % LANES == 0 and c_z % di == 0
    return _call(
        body, name, (nc,),
        [col(di, 0), col(gn, di), col(gn, di + gn), col(LANES, c_dt), col(di, c_z), vec, vec, vec,
         pl.BlockSpec((1, di), lambda i: (0, 0))],
        [col(di, 0), col(di, 0), pl.BlockSpec((1, gn, gw), lambda i: (i, 0, 0))],
        [jax.ShapeDtypeStruct((t, di), F32), jax.ShapeDtypeStruct((t, di), MXU_DTYPE),
         jax.ShapeDtypeStruct((nc, gn, gw), MXU_DTYPE)],
        [pltpu.VMEM((gn, gw), F32), pltpu.VMEM((CHUNK, gw), MXU_DTYPE)], ("arbitrary",),
        (xa, xa, xa, proj, proj, dtb, alog, dvec, nw), comm)


def _ssd_bwd(dys, xa, proj, c_dt, c_z, ypre, st, dtb, alog, dvec, nw, di, name, comm=None):
    t = xa.shape[0]
    nc = t // CHUNK
    nh = di // SSD_HEAD_DIM
    hpg = nh // SSD_GROUPS
    gw = di // SSD_GROUPS
    n = SSD_STATE
    gn = SSD_GROUPS * n
    hd = SSD_HEAD_DIM

    def body(dys_ref, x_ref, b_ref, c_ref, dtraw_ref, z_ref, y_ref, st_ref, dtb_ref, alog_ref, d_ref, nw_ref,
             dxa_ref, dz_ref, ddt_ref, gb_ref, ga_ref, gd_ref, gnw_ref, dhts, dy_s, xdt_s, dxdt_s):
        i = pl.program_id(0)
        c = nc - 1 - i

        @pl.when(i == 0)
        def _():
            dhts[...] = jnp.zeros_like(dhts)
            gb_ref[...] = jnp.zeros_like(gb_ref)
            ga_ref[...] = jnp.zeros_like(ga_ref)
            gd_ref[...] = jnp.zeros_like(gd_ref)
            gnw_ref[...] = jnp.zeros_like(gnw_ref)

        dt, sig, a, acs, acs_t = _ssd_small(dtraw_ref, dtb_ref, alog_ref, c, nh)
        tri = _iota((CHUNK, CHUNK), 0) >= _iota((CHUNK, CHUNK), 1)
        triu = _iota((CHUNK, CHUNK), 0) <= _iota((CHUNK, CHUNK), 1)
        lane = _iota((1, LANES), 1)
        rows = _iota((CHUNK, 1), 0)
        head_row = _iota((LANES, 1), 0)
        dvec8 = jnp.broadcast_to(d_ref[...], (SUBLANES, LANES))
        da = jnp.zeros((CHUNK, LANES), F32)
        da_t = jnp.zeros((LANES, CHUNK), F32)
        ddt = jnp.zeros((CHUNK, LANES), F32)
        gd = jnp.zeros((1, LANES), F32)
        for g in range(SSD_GROUPS):
            gs = slice(g * gw, (g + 1) * gw)
            ns = slice(g * n, (g + 1) * n)
            y_g, z_g = y_ref[:, gs], z_ref[:, gs]
            sz = _silu(z_g)
            yz = y_g * sz
            r = lax.rsqrt(jnp.mean(yz * yz, axis=1, keepdims=True) + EPS)
            nrm = yz * r
            dyo = dys_ref[:, gs]
            gnw_ref[:, gs] += jnp.sum(dyo * nrm, axis=0, keepdims=True)
            dn = dyo * nw_ref[:, gs]
            dyz = r * (dn - nrm * jnp.mean(dn * nrm, axis=1, keepdims=True))
            dz_ref[:, gs] = (dyz * y_g * _dsilu(z_g)).astype(MXU_DTYPE)
            dy_g = dyz * sz
            dy_s[...] = dy_g.astype(MXU_DTYPE)
            e_mat = _head_expand(g, hpg, gw)
            ax = _dot01(acs, e_mat, "a", 3)
            dtx = _dot01(dt, e_mat, "a", 3)
            dx = _dot01(dvec8, e_mat, "a", 3)[0:1, :]
            xg, bg, cg = x_ref[:, gs], b_ref[:, ns], c_ref[:, ns]
            xdt = xg * dtx
            xdt_s[...] = xdt.astype(MXU_DTYPE)
            aend = ax[CHUNK - 1:CHUNK, :]
            e = jnp.exp(aend - ax)
            ea = jnp.exp(ax)
            eend = jnp.exp(aend)
            htp = st_ref[0, ns, :].astype(F32)
            dht = dhts[ns, :]
            cb = _dot_nt(cg, bg)
            q = _dot(bg, dht)
            dxdt_s[...] = e * q
            wl = e * q * xdt
            d_b = _dot_nt(e * xdt, dht)
            yi = ea * _dot(cg, htp)
            eady = ea * dy_g
            d_c = _dot_nt(eady, htp)
            t1 = jnp.sum(dht * htp, axis=0, keepdims=True) * eend
            dhts[ns, :] = eend * dht + _dot_tn(cg, eady)
            da = da + _dot01(dy_g * yi - wl, e_mat, "a", 3, _NT)
            tail = jnp.broadcast_to(jnp.sum(wl, axis=0, keepdims=True) + t1, (SUBLANES, gw))
            da_end = _dot01(tail, e_mat, "a", 3, _NT)[0:1, :]
            da = da + jnp.where(rows == CHUNK - 1, da_end, 0.0)
            dcb = jnp.zeros((CHUNK, CHUNK), F32)
            for hh in range(hpg):
                h = g * hpg + hh
                lmat = jnp.exp(jnp.where(tri, acs[:, h:h + 1] - acs_t[h:h + 1, :], -jnp.inf))
                hl = slice(hh * hd, (hh + 1) * hd)
                dy_h, xdt_h = dy_s[:, hl], xdt_s[:, hl]
                dxdt_s[:, hl] += _dot_tn(cb * lmat, dy_h)
                dml = _dot_nt(dy_h, xdt_h) * lmat
                dcb = dcb + dml
                gmat = dml * cb
                da = da + jnp.where(lane == h, jnp.sum(gmat, axis=1, keepdims=True), 0.0)
                da_t = da_t - jnp.where(head_row == h, jnp.sum(gmat, axis=0, keepdims=True), 0.0)
            d_c = d_c + _dot(dcb, bg)
            d_b = d_b + _dot_tn(dcb, cg)
            dxdt = dxdt_s[...]
            dxa_ref[:, gs] = dxdt * dtx + dx * dy_g
            dxa_ref[:, di + g * n:di + (g + 1) * n] = d_b
            dxa_ref[:, di + gn + g * n:di + gn + (g + 1) * n] = d_c
            ddt = ddt + _dot01(dxdt * xg, e_mat, "a", 3, _NT)
            gd8 = jnp.broadcast_to(jnp.sum(dy_g * xg, axis=0, keepdims=True), (SUBLANES, gw))
            gd = gd + _dot01(gd8, e_mat, "a", 3, _NT)[0:1, :]
        da = da + da_t.T
        triu_f = triu.astype(F32)
        ddta = _dot01(triu_f, da, "b", 3)
        ddt = ddt + ddta * a
        draw = ddt * sig
        ddt_ref[...] = draw.astype(MXU_DTYPE)
        gb_ref[...] += jnp.sum(draw, axis=0, keepdims=True)
        ga_ref[...] += jnp.sum(ddta * dt, axis=0, keepdims=True) * a
        gd_ref[...] += gd

    def col(width, c0):
        return pl.BlockSpec((CHUNK, width), lambda i: (nc - 1 - i, c0 // width))

    vec = pl.BlockSpec((1, LANES), lambda i: (0, 0))
    wide = pl.BlockSpec((1, di), lambda i: (0, 0))
    wa = di + 2 * gn
    return _call(
        body, name, (nc,),
        [col(di, 0), col(di, 0), col(gn, di), col(gn, di + gn), col(LANES, c_dt), col(di, c_z), col(di, 0),
         pl.BlockSpec((1, gn, gw), lambda i: (nc - 1 - i, 0, 0)), vec, vec, vec, wide],
        [col(wa, 0), col(di, 0), col(LANES, 0), vec, vec, vec, wide],
        [jax.ShapeDtypeStruct((t, wa), F32), jax.ShapeDtypeStruct((t, di), MXU_DTYPE),
         jax.ShapeDtypeStruct((t, LANES), MXU_DTYPE), jax.ShapeDtypeStruct((1, LANES), F32),
         jax.ShapeDtypeStruct((1, LANES), F32), jax.ShapeDtypeStruct((1, LANES), F32),
         jax.ShapeDtypeStruct((1, di), F32)],
        [pltpu.VMEM((gn, gw), F32), pltpu.VMEM((CHUNK, gw), MXU_DTYPE), pltpu.VMEM((CHUNK, gw), MXU_DTYPE),
         pltpu.VMEM((CHUNK, gw), F32)], ("arbitrary",),
        (dys, xa, xa, xa, proj, proj, ypre, st, dtb, alog, dvec, nw), comm)


def _adam_math(w, g, m, v):
    m2 = ADAM_B1 * m + (1.0 - ADAM_B1) * g
    v2 = ADAM_B2 * v + (1.0 - ADAM_B2) * (g * g)
    m_hat = m2 / (1.0 - ADAM_B1 ** ADAM_STEP)
    v_hat = v2 / (1.0 - ADAM_B2 ** ADAM_STEP)
    delta = -ADAM_LR * (m_hat / (jnp.sqrt(v_hat) + ADAM_EPS) + ADAM_WD * w)
    return delta, m2, v2


def _adam_big(w, g_mine, g_sib, m, v, core, name):
    r, c = w.shape
    h = r // 2
    tr = _pick(h, (128, 64, 32, 16, 8))
    nbh = h // tr

    def body(core_ref, w_ref, a_ref, b_ref, m_ref, v_ref, g_ref, d_ref, m2_ref, v2_ref):
        g = jnp.where(pl.program_id(0) // nbh == core_ref[0], a_ref[...], b_ref[...])
        delta, m2, v2 = _adam_math(w_ref[...], g, m_ref[...], v_ref[...])
        g_ref[...] = g
        d_ref[...] = delta
        m2_ref[...] = m2
        v2_ref[...] = v2

    blk = pl.BlockSpec((tr, c), lambda i, core_ref: (i, 0))
    hblk = pl.BlockSpec((tr, c), lambda i, core_ref: (i % nbh, 0))
    out = jax.ShapeDtypeStruct((r, c), F32)
    return pl.pallas_call(
        body, name=name,
        grid_spec=pltpu.PrefetchScalarGridSpec(num_scalar_prefetch=1, grid=(2 * nbh,),
                                               in_specs=[blk, hblk, hblk, blk, blk], out_specs=[blk] * 4),
        out_shape=[out] * 4, compiler_params=_params(("parallel",)),
    )(core, w, g_mine, g_sib, m, v)


def _pair_sum(g, sib, core, name):
    _, r, c = g.shape
    h = r // 2
    tr = _pick(h, (128, 64, 32, 16))
    nb = h // tr

    def body(core_ref, g_ref, s_ref, o_ref):
        o_ref[...] = (g_ref[...] + s_ref[...]).astype(WIRE_DTYPE)

    return pl.pallas_call(
        body, name=name,
        grid_spec=pltpu.PrefetchScalarGridSpec(
            num_scalar_prefetch=1, grid=(4, nb),
            in_specs=[pl.BlockSpec((1, tr, c), lambda j, i, core_ref: (j, core_ref[0] * nb + i, 0)),
                      pl.BlockSpec((1, tr, c), lambda j, i, core_ref: (j, i, 0))],
            out_specs=pl.BlockSpec((1, tr, c), lambda j, i, core_ref: (j, i, 0))),
        out_shape=jax.ShapeDtypeStruct((4, h, c), WIRE_DTYPE), compiler_params=_params(("parallel", "parallel")),
    )(core, g, sib)


def _sum4(parts, name):
    _, r, c = parts.shape
    tr = _pick(r, (128, 64, 32, 16, 8))

    def body(p_ref, o_ref):
        acc = p_ref[0].astype(F32)
        for j in range(1, 4):
            acc = acc + p_ref[j].astype(F32)
        o_ref[...] = acc

    return pl.pallas_call(
        body, name=name, grid=(r // tr,),
        in_specs=[pl.BlockSpec((4, tr, c), lambda i: (0, i, 0))],
        out_specs=pl.BlockSpec((tr, c), lambda i: (i, 0)),
        out_shape=jax.ShapeDtypeStruct((r, c), F32),
        compiler_params=_params(("parallel",)),
    )(parts)


def _adam_small(w, gparts, m, v, name):
    r = w.shape[0]

    def body(w_ref, g_ref, m_ref, v_ref, go_ref, d_ref, m2_ref, v2_ref):
        g = g_ref[0]
        for j in range(1, 8):
            g = g + g_ref[j]
        delta, m2, v2 = _adam_math(w_ref[...], g, m_ref[...], v_ref[...])
        go_ref[...] = g
        d_ref[...] = delta
        m2_ref[...] = m2
        v2_ref[...] = v2

    out = jax.ShapeDtypeStruct((r, LANES), F32)
    return pl.pallas_call(body, name=name, out_shape=[out] * 4)(w, gparts, m, v)


def _allgather8(v, name):
    def body(src, dst, send, recv, loc):
        x, y, c = lax.axis_index("x"), lax.axis_index("y"), lax.axis_index("c")
        mine = 4 * x + 2 * y + c
        lc = pltpu.make_async_copy(src, dst.at[mine], loc)
        lc.start()
        copies = []
        for k in range(1, 8):
            fx, fy, fc = (k >> 2) & 1, (k >> 1) & 1, k & 1
            peer = (1 - x if fx else x, 1 - y if fy else y, 1 - c if fc else c)
            cp = pltpu.make_async_remote_copy(src_ref=src, dst_ref=dst.at[mine], send_sem=send.at[k - 1],
                                              recv_sem=recv.at[k - 1], device_id=peer, device_id_type=MESH)
            cp.start()
            copies.append(cp)
        for cp in copies:
            cp.wait()
        lc.wait()

    anyspec = pl.BlockSpec(memory_space=pl.ANY)
    return pl.pallas_call(
        body, name=name, in_specs=[anyspec], out_specs=anyspec,
        out_shape=jax.ShapeDtypeStruct((8,) + v.shape, v.dtype),
        scratch_shapes=[pltpu.SemaphoreType.DMA((7,)), pltpu.SemaphoreType.DMA((7,)), pltpu.SemaphoreType.DMA],
        compiler_params=pltpu.CompilerParams(has_side_effects=True),
    )(v)


def _pack(parts):
    flat = jnp.concatenate([p.reshape(-1).astype(F32) for p in parts])
    pad = (-flat.shape[0]) % (32 * LANES)
    return jnp.pad(flat, (0, pad)).reshape(-1, LANES)


def _unpack(slab, shapes):
    flat = slab.reshape(-1)
    out, off = [], 0
    for s in shapes:
        size = int(np.prod(s))
        out.append(flat[off:off + size].reshape(s))
        off += size
    return out


def _pad_lanes(v):
    return jnp.pad(v.reshape(1, -1), ((0, 0), (0, LANES - v.shape[-1])))


def kernel(x, meta_tokens, mix_norm_w, w_in, ssd_conv_w, ssd_conv_b, ssd_dt_bias, ssd_A_log, ssd_D, ssd_norm_w, w_branch_ssd, w_branch_ret, w_out, ffn_norm_w, w_up, ffn_conv_w, ffn_conv_b, w_down, final_norm_w, loss_target, m_meta_tokens, m_mix_norm_w, m_w_in, m_ssd_conv_w, m_ssd_conv_b, m_ssd_dt_bias, m_ssd_A_log, m_ssd_D, m_ssd_norm_w, m_w_branch_ssd, m_w_branch_ret, m_w_out, m_ffn_norm_w, m_w_up, m_ffn_conv_w, m_ffn_conv_b, m_w_down, m_final_norm_w, v_meta_tokens, v_mix_norm_w, v_w_in, v_ssd_conv_w, v_ssd_conv_b, v_ssd_dt_bias, v_ssd_A_log, v_ssd_D, v_ssd_norm_w, v_w_branch_ssd, v_w_branch_ret, v_w_out, v_ffn_norm_w, v_w_up, v_ffn_conv_w, v_ffn_conv_b, v_w_down, v_final_norm_w):
    seq, d = x.shape[1], x.shape[2]
    t = CHUNK + seq
    di = 2 * d
    nh = di // SSD_HEAD_DIM
    gn = SSD_GROUPS * SSD_STATE
    cw = di + 2 * gn
    f = w_down.shape[1] * 4
    chip = 2 * lax.axis_index("x") + lax.axis_index("y")

    order = [("z", di), ("v", di), ("g", di), ("xbc", cw), ("q", d), ("k", d), ("gs", d), ("gr", d), ("dt", LANES)]
    col, acc = {}, 0
    for nm, wd in order:
        col[nm] = acc
        acc += wd
    wp = acc
    ref_order = [("z", di), ("xbc", cw), ("dt", nh), ("q", d), ("k", d), ("v", di), ("g", di), ("gs", d), ("gr", d)]
    ref_off, acc = {}, 0
    for nm, wd in ref_order:
        ref_off[nm] = (acc, wd)
        acc += wd
    in_dim = acc

    core = lax.axis_index("c").astype(jnp.int32).reshape(1)
    small_shapes = [meta_tokens.shape, ssd_conv_w.shape[1:], ffn_conv_w.shape[1:]]
    small_local = _pack([meta_tokens, ssd_conv_w[0], ffn_conv_w[0]])
    first_local = [w_in[0].astype(WIRE_DTYPE), small_local]
    first_half = _run_comm(_gather_ici(first_local), "gather_w_in_ici")
    g_in, g_small = [_with_own(g, own, chip)
                     for g, own in zip(_run_comm(_gather_d2d(first_half), "gather_w_in_d2d"), first_local)]
    rest_local = [a[0].astype(WIRE_DTYPE) for a in (w_branch_ssd, w_branch_ret, w_out, w_up, w_down)]
    w_in_full = jnp.moveaxis(g_in, 0, 1).reshape(d, in_dim)
    pieces = []
    for nm, wd in order:
        o, rw = ref_off[nm]
        p = w_in_full[:, o:o + rw]
        if rw < wd:
            p = jnp.pad(p, ((0, 0), (0, wd - rw)))
        pieces.append(p)
    w_p = jnp.concatenate(pieces, axis=1)
    smalls = [_unpack(g_small[j], small_shapes) for j in range(4)]
    meta_full = jnp.concatenate([s[0] for s in smalls], axis=1)
    scw = jnp.concatenate([s[1] for s in smalls], axis=1)
    fcw = jnp.concatenate([s[2] for s in smalls], axis=1)
    scb, fcb = ssd_conv_b, ffn_conv_b
    dtb, alog, dvec = _pad_lanes(ssd_dt_bias), _pad_lanes(ssd_A_log), _pad_lanes(ssd_D)
    fin_w = final_norm_w.reshape(1, d)

    hq = d // RET_HEADS
    pos = jnp.arange(t, dtype=F32) - FRONT
    inv_freq = ROPE_BASE ** (-jnp.linspace(0.0, 1.0, hq // 2, dtype=F32))
    ang = pos[:, None] * inv_freq[None, :]
    cos, sin = jnp.cos(ang), jnp.sin(ang)

    h0 = jnp.concatenate([jnp.zeros((FRONT, d), F32), meta_full, x[0]], axis=0)
    tm = _row_tile(t)
    u1 = _rms_fwd(h0, mix_norm_w, "rms1_fwd")
    proj = _mm(u1, w_p, "nn", F32, "proj", tm, _pick(wp, (1920, 1536, 1280, 1024, 896, 768, 640, 512, 384, 256, 128)), d)
    xa = _ssd_conv_fwd(proj, col["xbc"], cw, scw, scb, "ssd_conv_fwd")
    res = _ssd_fwd(xa, proj, col["dt"], col["z"], dtb, alog, dvec, ssd_norm_w, di, "ssd_fwd", comm=_gather_ici(rest_local))
    (ypre, yssd, st_ssd), rest_half = res[:3], res[3:]
    res = _ret_fwd(proj, col["q"], col["k"], col["v"], col["g"], cos, sin, d, "ret_fwd", comm=_gather_d2d(rest_half))
    o_ret, yret, qr, kr, st_ret = res[:5]
    g_bs, g_br, g_out, g_up, g_down = [_with_own(g, own, chip) for g, own in zip(res[5:], rest_local)]
    w_bs = g_bs.reshape(di, d)
    w_br = g_br.reshape(di, d)
    w_o = g_out.reshape(d, d)
    w_u = jnp.moveaxis(g_up, 0, 1).reshape(d, 2 * f)
    w_d = g_down.reshape(f, d)
    tn_d = _pick(d, (1024, 512, 256, 128))
    bs = _mm(yssd, w_bs, "nn", F32, "branch_ssd", tm, tn_d, _pick(di, (1024, 512, 256)))
    br = _mm(yret, w_br, "nn", F32, "branch_ret", tm, tn_d, _pick(di, (1024, 512, 256)))
    merged = _gate_fwd(bs, br, proj, col["gs"], col["gr"], "gate_fwd")
    h1 = _mm(merged, w_o, "nn", F32, "out_proj", tm, tn_d, d, res=h0)
    u2 = _rms_fwd(h1, ffn_norm_w, "rms2_fwd")
    tn_f = _pick(2 * f, (1408, 1024, 768, 512, 256, 128))
    up = _mm(u2, w_u, "nn", F32, "up_proj", tm, tn_f, d)
    act = _ffn_conv_fwd(up, fcw, fcb, "ffn_conv_fwd")
    tk_f = _pick(f, (1408, 768, 704, 512, 256, 128))
    h2 = _mm(act, w_d, "nn", F32, "down_proj", tm, tn_d, tk_f, res=h1)
    loss8, d_h2, g_fin = _loss_bwd(h2, fin_w, loss_target[0], "loss_head")

    d_act = _mm(d_h2, w_d, "nt", F32, "d_act", tm, tk_f, d)
    g_wd = _mm(act, d_h2, "tn", F32, "g_w_down", tk_f, tn_d, tm)
    d_upg, d_upv, g_fcwg, g_fcwv, g_fcbg, g_fcbv = _ffn_conv_bwd(up, fcw, fcb, d_act, "ffn_conv_bwd")
    g_fcw = jnp.concatenate([g_fcwg, g_fcwv], axis=1)
    g_fcb = jnp.concatenate([g_fcbg, g_fcbv], axis=1)
    d_u2 = _mm(d_upg, w_u[:, :f], "nt", F32, "d_u2_gate", tm, tn_d, tk_f)
    d_u2 = _mm(d_upv, w_u[:, f:], "nt", F32, "d_u2_value", tm, tn_d, tk_f, res=d_u2)
    g_wu = jnp.concatenate([_mm(u2, d_upg, "tn", F32, "g_w_up_gate", tn_d, tk_f, tm),
                            _mm(u2, d_upv, "tn", F32, "g_w_up_value", tn_d, tk_f, tm)], axis=1)
    d_h1, g_ffnw = _rms_bwd(h1, ffn_norm_w, d_u2, d_h2, "rms2_bwd")
    d_merged = _mm(d_h1, w_o, "nt", F32, "d_merged", tm, tn_d, d)
    g_wo = _mm(merged, d_h1, "tn", F32, "g_w_out", tn_d, tn_d, tm)
    d_bs, d_br, d_gs, d_gr = _gate_bwd(d_merged, bs, br, proj, col["gs"], col["gr"], "gate_bwd")
    tk_i = _pick(di, (1024, 512, 256))
    d_yssd = _mm(d_bs, w_bs, "nt", F32, "d_y_ssd", tm, tk_i, d)
    g_wbs = _mm(yssd, d_bs, "tn", F32, "g_w_branch_ssd", tk_i, tn_d, tm)
    d_yret = _mm(d_br, w_br, "nt", F32, "d_y_ret", tm, tk_i, d)
    g_wbr = _mm(yret, d_br, "tn", F32, "g_w_branch_ret", tk_i, tn_d, tm)

    early_names = ["w_branch_ssd", "w_branch_ret", "w_out", "w_up", "w_down"]
    early = [g_wbs.reshape(4, di // 4, d), g_wbr.reshape(4, di // 4, d), g_wo.reshape(4, d // 4, d),
             jnp.moveaxis(g_wu.reshape(d, 4, 2 * f // 4), 1, 0), g_wd.reshape(4, f // 4, d)]
    res = _ret_bwd(d_yret, proj, col["v"], col["g"], o_ret, qr, kr, st_ret, cos, sin, d, "ret_bwd", comm=_scatter_d2d(early))
    (dq, dk, dv, dg), early_sib = res[:4], res[4:]
    early_pair = [_pair_sum(g_, s_, core, "pair_" + nm) for g_, s_, nm in zip(early, early_sib, early_names)]
    res = _ssd_bwd(d_yssd, xa, proj, col["dt"], col["z"], ypre, st_ssd, dtb, alog, dvec, ssd_norm_w, di, "ssd_bwd",
                   comm=_scatter_ici(early_pair))
    (d_xa, dz, ddt, g_dtb, g_alog, g_dvec, g_snw), early_recv = res[:7], res[7:]
    early_mine = [_sum4(p, "sum4_" + nm) for p, nm in zip(early_recv, early_names)]
    res = _ssd_conv_bwd(proj, col["xbc"], cw, scw, scb, d_xa, "ssd_conv_bwd", comm=_sibling_swap(early_mine))
    (d_xbc, g_scw, g_scb), early_other = res[:3], res[3:]
    d_pieces = dict(z=dz, v=dv, g=dg, xbc=d_xbc, q=dq, k=dk, gs=d_gs, gr=d_gr, dt=ddt)

    g_piece = {nm: _mm(u1, d_pieces[nm], "tn", F32, "g_w_in_" + nm, tn_d, _pick(wd, (1024, 768, 512, 256, 128)), tm)
               for nm, wd in order}
    g_in_ref = jnp.concatenate([g_piece[nm][:, :rw] for nm, rw in ref_order], axis=1)
    sc_in = jnp.moveaxis(g_in_ref.reshape(d, 4, in_dim // 4), 1, 0)
    in_sib = _run_comm(_scatter_d2d([sc_in]), "scatter_w_in_d2d")[0]
    in_pair = _pair_sum(sc_in, in_sib, core, "pair_w_in")
    d_u1, (in_recv,) = _mm_pieces_nt([(d_pieces[nm], col[nm]) for nm, _ in order], w_p, F32, "d_u1", tm, tn_d,
                                     comm=_scatter_ici([in_pair]))
    in_mine = _sum4(in_recv, "sum4_w_in")
    d_h0, g_mixw, in_other = _rms_bwd(h0, mix_norm_w, d_u1, d_h1, "rms1_bwd", comm=_sibling_swap([in_mine]))
    grad_x = d_h0[CHUNK:][None]
    g_meta = d_h0[FRONT:CHUNK]

    names = ["w_in"] + early_names
    mine_half = [in_mine] + early_mine
    other_half = [in_other] + list(early_other)
    big_w = [w_in, w_branch_ssd, w_branch_ret, w_out, w_up, w_down]
    big_m = [m_w_in, m_w_branch_ssd, m_w_branch_ret, m_w_out, m_w_up, m_w_down]
    big_v = [v_w_in, v_w_branch_ssd, v_w_branch_ret, v_w_out, v_w_up, v_w_down]
    big_out = {}
    for nm, w_, p_, s_, m_, v_ in zip(names, big_w, mine_half, other_half, big_m, big_v):
        res = _adam_big(w_[0], p_, s_, m_[0], v_[0], core, "adam_" + nm)
        big_out[nm] = [r[None] for r in res]

    kws, kwf = ssd_conv_w.shape[1], ffn_conv_w.shape[1]
    small_grads = [g_meta, g_mixw, g_scw[:kws], g_scb, g_dtb[:, :nh], g_alog[:, :nh], g_dvec[:, :nh], g_snw, g_ffnw,
                   g_fcw[:kwf], g_fcb, g_fin, loss8[0:1, 0:1]]
    sg_shapes = [g.shape for g in small_grads]
    gparts = _allgather8(_pack(small_grads), "gather_small_grads")

    def own_cols(a, width):
        return lax.dynamic_slice_in_dim(a, chip * width, width, axis=1)

    def widen(a, width_full):
        z = jnp.zeros(a.shape[:-1] + (width_full,), F32)
        return lax.dynamic_update_slice_in_dim(z, a, chip * a.shape[-1], axis=a.ndim - 1)

    def small_slab(meta_, mix_, scw_, scb_, dtb_, alog_, d_, snw_, ffnw_, fcw_, fcb_, fin_):
        return _pack([widen(meta_, d), mix_, widen(scw_[0], cw), scb_, dtb_, alog_, d_, snw_, ffnw_, widen(fcw_[0], 2 * f),
                      fcb_, fin_.reshape(1, d), jnp.zeros((1, 1), F32)])

    w_slab = small_slab(meta_tokens, mix_norm_w, ssd_conv_w, ssd_conv_b, ssd_dt_bias, ssd_A_log, ssd_D, ssd_norm_w,
                        ffn_norm_w, ffn_conv_w, ffn_conv_b, final_norm_w)
    m_slab = small_slab(m_meta_tokens, m_mix_norm_w, m_ssd_conv_w, m_ssd_conv_b, m_ssd_dt_bias, m_ssd_A_log, m_ssd_D,
                        m_ssd_norm_w, m_ffn_norm_w, m_ffn_conv_w, m_ffn_conv_b, m_final_norm_w)
    v_slab = small_slab(v_meta_tokens, v_mix_norm_w, v_ssd_conv_w, v_ssd_conv_b, v_ssd_dt_bias, v_ssd_A_log, v_ssd_D,
                        v_ssd_norm_w, v_ffn_norm_w, v_ffn_conv_w, v_ffn_conv_b, v_final_norm_w)
    small_res = [_unpack(s, sg_shapes) for s in _adam_small(w_slab, gparts, m_slab, v_slab, "adam_small")]
    loss = small_res[0][12].reshape(())

    def small_outputs(vals):
        meta_, mix_, scw_, scb_, dtb_, alog_, d_, snw_, ffnw_, fcw_, fcb_, fin_ = vals[:12]
        return {
            "meta_tokens": own_cols(meta_, d // 4), "mix_norm_w": mix_, "ssd_conv_w": own_cols(scw_, cw // 4)[None],
            "ssd_conv_b": scb_, "ssd_dt_bias": dtb_, "ssd_A_log": alog_, "ssd_D": d_, "ssd_norm_w": snw_,
            "ffn_norm_w": ffnw_, "ffn_conv_w": own_cols(fcw_, 2 * f // 4)[None], "ffn_conv_b": fcb_,
            "final_norm_w": fin_.reshape(d),
        }

    weights = ["meta_tokens", "mix_norm_w", "w_in", "ssd_conv_w", "ssd_conv_b", "ssd_dt_bias", "ssd_A_log", "ssd_D",
               "ssd_norm_w", "w_branch_ssd", "w_branch_ret", "w_out", "ffn_norm_w", "w_up", "ffn_conv_w", "ffn_conv_b",
               "w_down", "final_norm_w"]
    outs = [loss, grad_x]
    for kind in range(4):
        so = small_outputs(small_res[kind])
        for nm in weights:
            outs.append(big_out[nm][kind] if nm in big_out else so[nm])
    return tuple(outs)
```

```python
import functools
import math

import jax
import jax.numpy as jnp
import numpy as np
from jax import lax
from jax.experimental import pallas as pl
from jax.experimental.pallas import tpu as pltpu

F32 = jnp.float32
BF16 = jnp.bfloat16
MXU_DTYPE = BF16
WIRE_DTYPE = BF16

N_META = 16
CHUNK = 128
FRONT = CHUNK - N_META
EPS = 1e-6
SSD_HEAD_DIM = 64
SSD_GROUPS = 4
SSD_STATE = 128
SSD_CONV = 4
RET_HEADS = 4
ROPE_BASE = 10000.0
FFN_CONV = 3
LANES = 128
SUBLANES = 8
VMEM_LIMIT = 56 * 1024 * 1024

ADAM_LR = 0.001
ADAM_B1 = 0.9
ADAM_B2 = 0.999
ADAM_EPS = 1e-08
ADAM_WD = 0.01
ADAM_STEP = 10
MESH = pl.DeviceIdType.MESH


def _params(sem=None, vmem=VMEM_LIMIT):
    return pltpu.CompilerParams(dimension_semantics=sem, vmem_limit_bytes=vmem)


def _pick(n, cands):
    for c in cands:
        if n % c == 0:
            return c
    return n


def _silu(x):
    return x * jax.nn.sigmoid(x)


def _dsilu(x):
    s = jax.nn.sigmoid(x)
    return s * (1.0 + x * (1.0 - s))


def _dot(a, b, dims=(((1,), (0,)), ((), ()))):
    return lax.dot_general(a.astype(MXU_DTYPE), b.astype(MXU_DTYPE), dims, preferred_element_type=F32)


def _dot_nt(a, b):
    return _dot(a, b, (((1,), (1,)), ((), ())))


def _dot_tn(a, b):
    return _dot(a, b, (((0,), (0,)), ((), ())))


def _dot01(a, b, split, npass, dims=(((1,), (0,)), ((), ()))):
    rest = (a if split == "a" else b).astype(F32)
    fixed = (b if split == "a" else a).astype(BF16)
    acc = None
    for p in range(npass):
        piece = rest.astype(BF16)
        ops = (piece, fixed) if split == "a" else (fixed, piece)
        term = lax.dot_general(ops[0], ops[1], dims, preferred_element_type=F32)
        acc = term if acc is None else acc + term
        if p + 1 < npass:
            rest = rest - piece.astype(F32)
    return acc


_NT = (((1,), (1,)), ((), ()))


def _iota(shape, dim):
    return lax.broadcasted_iota(jnp.int32, shape, dim)


def _shift_down(cur, prev8, k):
    if k == 0:
        return cur
    rolled = pltpu.roll(cur, k, 0)
    i8 = _iota((SUBLANES, cur.shape[1]), 0)
    head = jnp.where(i8 < k, pltpu.roll(prev8, k, 0), rolled[0:SUBLANES])
    return jnp.concatenate([head, rolled[SUBLANES:]], axis=0)


def _shift_up(cur, next8, k):
    if k == 0:
        return cur
    n = cur.shape[0]
    rolled = pltpu.roll(cur, n - k, 0)
    i8 = _iota((SUBLANES, cur.shape[1]), 0)
    tail = jnp.where(i8 >= SUBLANES - k, pltpu.roll(next8, SUBLANES - k, 0), rolled[n - SUBLANES:])
    return jnp.concatenate([rolled[:n - SUBLANES], tail], axis=0)


class _Comm:
    def __init__(self, ins, outs, nsem, make, in_place=False):
        self.ins, self.outs, self.nsem, self.make = list(ins), list(outs), nsem, make
        self.in_place = in_place


def _place():
    x, y, c = lax.axis_index("x"), lax.axis_index("y"), lax.axis_index("c")
    return x, y, c, 2 * x + y, [(1 - x, y), (x, 1 - y), (1 - x, 1 - y)]


def _call(body, name, grid, in_specs, out_specs, out_shape, scratch, sem, args, comm=None):
    if comm is None:
        return pl.pallas_call(body, name=name, grid=grid, in_specs=in_specs, out_specs=out_specs, out_shape=out_shape,
                              scratch_shapes=scratch, compiler_params=_params(sem))(*args)
    n_in, n_out, n_scr = len(in_specs), len(out_specs), len(scratch)
    ci, co = len(comm.ins), len(comm.outs)

    def wrapped(*refs):
        ins, refs = refs[:n_in], refs[n_in:]
        cins, refs = refs[:ci], refs[ci:]
        outs, refs = refs[:n_out], refs[n_out:]
        couts, refs = refs[:co], refs[co:]
        scr, sems = refs[:n_scr], refs[n_scr:]
        first = functools.reduce(jnp.logical_and, [pl.program_id(a) == 0 for a in range(len(grid))])
        last = functools.reduce(jnp.logical_and, [pl.program_id(a) == grid[a] - 1 for a in range(len(grid))])

        @pl.when(first)
        def _():
            for cp in comm.make(cins, couts, *sems):
                cp.start()

        body(*ins, *outs, *scr)

        @pl.when(last)
        def _():
            for cp in comm.make(cins, couts, *sems):
                cp.wait()

    anyspec = pl.BlockSpec(memory_space=pl.ANY)
    dma = pltpu.SemaphoreType.DMA((comm.nsem,))
    aliases = {n_in + i: n_out + i for i in range(ci)} if comm.in_place else {}
    return pl.pallas_call(
        wrapped, name=name, grid=grid, in_specs=list(in_specs) + [anyspec] * ci,
        out_specs=list(out_specs) + [anyspec] * co, out_shape=list(out_shape) + comm.outs,
        scratch_shapes=list(scratch) + [dma, dma, dma], input_output_aliases=aliases,
        compiler_params=_params(("arbitrary",) * len(grid)))(*args, *comm.ins)


def _run_comm(comm, name):
    ci, co = len(comm.ins), len(comm.outs)

    def body(*refs):
        cins, couts, sems = refs[:ci], refs[ci:ci + co], refs[ci + co:]
        for cp in comm.make(cins, couts, *sems):
            cp.start()
        for cp in comm.make(cins, couts, *sems):
            cp.wait()

    anyspec = pl.BlockSpec(memory_space=pl.ANY)
    dma = pltpu.SemaphoreType.DMA((comm.nsem,))
    aliases = {i: i for i in range(ci)} if comm.in_place else {}
    return pl.pallas_call(body, name=name, in_specs=[anyspec] * ci, out_specs=[anyspec] * co, out_shape=comm.outs,
                          scratch_shapes=[dma, dma, dma], input_output_aliases=aliases)(*comm.ins)


def _half_rows(c, rows):
    h = rows // 2
    return pl.ds(pl.multiple_of(c * h, 16), h)


def _gather_ici(arrays):
    for a in arrays:
        assert a.shape[0] % 32 == 0, a.shape

    def make(ins, outs, send, recv, loc):
        x, y, c, mine, peers = _place()
        cps = []
        for i, a in enumerate(arrays):
            half = _half_rows(c, a.shape[0])
            for k, (px, py) in enumerate(peers):
                cps.append(pltpu.make_async_remote_copy(
                    src_ref=ins[i].at[half], dst_ref=outs[i].at[mine, half], send_sem=send.at[3 * i + k],
                    recv_sem=recv.at[3 * i + k], device_id=(px, py, c), device_id_type=MESH))
        return cps

    outs = [jax.ShapeDtypeStruct((4,) + a.shape, a.dtype) for a in arrays]
    return _Comm(arrays, outs, 3 * len(arrays), make)


def _gather_d2d(bufs):
    def make(ins, outs, send, recv, loc):
        x, y, c, mine, peers = _place()
        cps = []
        for i, a in enumerate(bufs):
            half = _half_rows(c, a.shape[1])
            for k, (px, py) in enumerate(peers):
                mine_half = outs[i].at[2 * px + py, half]
                cps.append(pltpu.make_async_remote_copy(
                    src_ref=mine_half, dst_ref=mine_half, send_sem=send.at[3 * i + k], recv_sem=recv.at[3 * i + k],
                    device_id=(x, y, 1 - c), device_id_type=MESH))
        return cps

    outs = [jax.ShapeDtypeStruct(a.shape, a.dtype) for a in bufs]
    return _Comm(bufs, outs, 3 * len(bufs), make, in_place=True)


def _with_own(gathered, own, chip):
    return lax.dynamic_update_index_in_dim(gathered, own, chip, 0)


def _scatter_d2d(grads):
    for a in grads:
        assert a.shape[1] % 32 == 0, a.shape

    def make(ins, outs, send, recv, loc):
        x, y, c, mine, peers = _place()
        cps = []
        for i, a in enumerate(grads):
            other = _half_rows(1 - c, a.shape[1])
            cps.append(pltpu.make_async_remote_copy(
                src_ref=ins[i].at[:, other], dst_ref=outs[i], send_sem=send.at[i], recv_sem=recv.at[i],
                device_id=(x, y, 1 - c), device_id_type=MESH))
        return cps

    outs = [jax.ShapeDtypeStruct((4, a.shape[1] // 2, a.shape[2]), a.dtype) for a in grads]
    return _Comm(grads, outs, len(grads), make)


def _scatter_ici(parts):
    def make(ins, outs, send, recv, loc):
        x, y, c, mine, peers = _place()
        cps = []
        for i in range(len(parts)):
            cps.append(pltpu.make_async_copy(ins[i].at[mine], outs[i].at[mine], loc.at[i]))
            for k, (px, py) in enumerate(peers):
                cps.append(pltpu.make_async_remote_copy(
                    src_ref=ins[i].at[2 * px + py], dst_ref=outs[i].at[mine], send_sem=send.at[3 * i + k],
                    recv_sem=recv.at[3 * i + k], device_id=(px, py, c), device_id_type=MESH))
        return cps

    outs = [jax.ShapeDtypeStruct(a.shape, a.dtype) for a in parts]
    return _Comm(parts, outs, 3 * len(parts), make)


def _sibling_swap(arrays):
    def make(ins, outs, send, recv, loc):
        x, y, c, mine, peers = _place()
        return [pltpu.make_async_remote_copy(src_ref=ins[i], dst_ref=outs[i], send_sem=send.at[i], recv_sem=recv.at[i],
                                             device_id=(x, y, 1 - c), device_id_type=MESH) for i in range(len(arrays))]

    outs = [jax.ShapeDtypeStruct(a.shape, a.dtype) for a in arrays]
    return _Comm(arrays, outs, len(arrays), make)


def _mm(a, b, mode, out_dtype, name, tm, tn, tk, res=None, comm=None):
    if mode == "nn":
        (m, kd), n = a.shape, b.shape[1]
        a_spec = pl.BlockSpec((tm, tk), lambda i, j, k: (i, k))
        b_spec = pl.BlockSpec((tk, tn), lambda i, j, k: (k, j))
        dims = (((1,), (0,)), ((), ()))
    elif mode == "nt":
        (m, kd), n = a.shape, b.shape[0]
        a_spec = pl.BlockSpec((tm, tk), lambda i, j, k: (i, k))
        b_spec = pl.BlockSpec((tn, tk), lambda i, j, k: (j, k))
        dims = (((1,), (1,)), ((), ()))
    else:
        (kd, m), n = a.shape, b.shape[1]
        a_spec = pl.BlockSpec((tk, tm), lambda i, j, k: (k, i))
        b_spec = pl.BlockSpec((tk, tn), lambda i, j, k: (k, j))
        dims = (((0,), (0,)), ((), ()))
    assert m % tm == 0 and n % tn == 0 and kd % tk == 0, (name, m, n, kd, tm, tn, tk)
    nk = kd // tk
    has_res = res is not None

    def body(*refs):
        a_ref, b_ref = refs[:2]
        r_ref = refs[2] if has_res else None
        o_ref = refs[3 if has_res else 2]

        def finish(r):
            if has_res:
                r = r + r_ref[...].astype(F32)
            o_ref[...] = r.astype(out_dtype)

        if nk == 1:
            finish(_dot(a_ref[...], b_ref[...], dims))
            return
        acc = refs[-1]
        k = pl.program_id(2)

        @pl.when(k == 0)
        def _():
            acc[...] = _dot(a_ref[...], b_ref[...], dims)

        @pl.when((k > 0) & (k < nk - 1))
        def _():
            acc[...] += _dot(a_ref[...], b_ref[...], dims)

        @pl.when(k == nk - 1)
        def _():
            finish(acc[...] + _dot(a_ref[...], b_ref[...], dims))

    in_specs = [a_spec, b_spec]
    args = [a, b]
    if has_res:
        in_specs.append(pl.BlockSpec((tm, tn), lambda i, j, k: (i, j)))
        args.append(res)
    res = _call(body, name, (m // tm, n // tn, nk), in_specs, [pl.BlockSpec((tm, tn), lambda i, j, k: (i, j))],
                [jax.ShapeDtypeStruct((m, n), out_dtype)], [] if nk == 1 else [pltpu.VMEM((tm, tn), F32)],
                ("parallel", "parallel", "arbitrary"), args, comm)
    return res[0] if comm is None else (res[0], res[1:])


def _mm_pieces_nt(pieces, b, out_dtype, name, tm, tn, comm=None):
    m, n = pieces[0][0].shape[0], b.shape[0]
    tks = [_pick(math.gcd(a.shape[1], c0) if c0 else a.shape[1], (1024, 512, 256, 128)) for a, c0 in pieces]
    nks = [a.shape[1] // tk for (a, _), tk in zip(pieces, tks)]
    starts = [sum(nks[:p]) for p in range(len(pieces))]
    ktot = sum(nks)
    npc = len(pieces)

    def body(*refs):
        a_refs, b_refs, o_ref, acc = refs[:npc], refs[npc:2 * npc], refs[2 * npc], refs[2 * npc + 1]
        k = pl.program_id(2)

        @pl.when(k == 0)
        def _():
            acc[...] = jnp.zeros_like(acc)

        for p in range(npc):
            @pl.when((k >= starts[p]) & (k < starts[p] + nks[p]))
            def _(p=p):
                acc[...] += _dot_nt(a_refs[p][...], b_refs[p][...])

        @pl.when(k == ktot - 1)
        def _():
            o_ref[...] = acc[...].astype(out_dtype)

    def a_spec(p):
        return pl.BlockSpec((tm, tks[p]), lambda i, j, k: (i, jnp.clip(k - starts[p], 0, nks[p] - 1)))

    def b_spec(p):
        c0 = pieces[p][1] // tks[p]
        return pl.BlockSpec((tn, tks[p]), lambda i, j, k: (j, c0 + jnp.clip(k - starts[p], 0, nks[p] - 1)))

    res = _call(body, name, (m // tm, n // tn, ktot), [a_spec(p) for p in range(npc)] + [b_spec(p) for p in range(npc)],
                [pl.BlockSpec((tm, tn), lambda i, j, k: (i, j))], [jax.ShapeDtypeStruct((m, n), out_dtype)],
                [pltpu.VMEM((tm, tn), F32)], ("parallel", "parallel", "arbitrary"),
                [a for a, _ in pieces] + [b] * npc, comm)
    return res[0] if comm is None else (res[0], res[1:])


def _rms_fwd(h, w, name):
    t, d = h.shape
    tr = _pick(t, (640, 512, 384, 256, 128))

    def body(h_ref, w_ref, u_ref):
        x = h_ref[...]
        r = lax.rsqrt(jnp.mean(x * x, axis=1, keepdims=True) + EPS)
        u_ref[...] = (x * r * w_ref[...]).astype(MXU_DTYPE)

    return pl.pallas_call(
        body, name=name, grid=(t // tr,),
        in_specs=[pl.BlockSpec((tr, d), lambda i: (i, 0)), pl.BlockSpec((1, d), lambda i: (0, 0))],
        out_specs=pl.BlockSpec((tr, d), lambda i: (i, 0)),
        out_shape=jax.ShapeDtypeStruct((t, d), MXU_DTYPE),
        compiler_params=_params(("parallel",)),
    )(h, w)


def _rms_bwd(h, w, du, res, name, comm=None):
    t, d = h.shape
    tr = _pick(t, (640, 512, 384, 256, 128))

    def body(h_ref, w_ref, du_ref, res_ref, dh_ref, gw_ref):
        @pl.when(pl.program_id(0) == 0)
        def _():
            gw_ref[...] = jnp.zeros_like(gw_ref)

        x = h_ref[...]
        r = lax.rsqrt(jnp.mean(x * x, axis=1, keepdims=True) + EPS)
        xhat = x * r
        dy = du_ref[...].astype(F32)
        dxh = dy * w_ref[...]
        dh = r * (dxh - xhat * jnp.mean(dxh * xhat, axis=1, keepdims=True))
        dh_ref[...] = dh + res_ref[...]
        gw_ref[...] += jnp.sum(dy * xhat, axis=0, keepdims=True)

    row = pl.BlockSpec((tr, d), lambda i: (i, 0))
    vec = pl.BlockSpec((1, d), lambda i: (0, 0))
    return _call(body, name, (t // tr,), [row, vec, row, row], [row, vec],
                 [jax.ShapeDtypeStruct((t, d), F32), jax.ShapeDtypeStruct((1, d), F32)], [], ("arbitrary",),
                 (h, w, du, res), comm)


def _loss_bwd(h2, w, target, name):
    t, d = h2.shape
    nc = t // CHUNK

    def body(h_ref, w_ref, tg_ref, loss_ref, dh_ref, gw_ref):
        i = pl.program_id(0)

        @pl.when(i == 0)
        def _():
            gw_ref[...] = jnp.zeros_like(gw_ref)
            loss_ref[...] = jnp.zeros_like(loss_ref)
            dh_ref[...] = jnp.zeros_like(dh_ref)

        @pl.when(i > 0)
        def _():
            x = h_ref[...]
            r = lax.rsqrt(jnp.mean(x * x, axis=1, keepdims=True) + EPS)
            xhat = x * r
            diff = xhat * w_ref[...] - tg_ref[...]
            loss_ref[...] += 0.5 * jnp.sum(jnp.sum(diff * diff, axis=1, keepdims=True), axis=0, keepdims=True) / d
            dy = diff / d
            dxh = dy * w_ref[...]
            dh_ref[...] = r * (dxh - xhat * jnp.mean(dxh * xhat, axis=1, keepdims=True))
            gw_ref[...] += jnp.sum(dy * xhat, axis=0, keepdims=True)

    row = pl.BlockSpec((CHUNK, d), lambda i: (i, 0))
    vec = pl.BlockSpec((1, d), lambda i: (0, 0))
    return pl.pallas_call(
        body, name=name, grid=(nc,),
        in_specs=[row, vec, pl.BlockSpec((CHUNK, d), lambda i: (jnp.maximum(i - 1, 0), 0))],
        out_specs=[pl.BlockSpec((SUBLANES, LANES), lambda i: (0, 0)), row, vec],
        out_shape=[jax.ShapeDtypeStruct((SUBLANES, LANES), F32), jax.ShapeDtypeStruct((t, d), F32),
                   jax.ShapeDtypeStruct((1, d), F32)],
        compiler_params=_params(("arbitrary",)),
    )(h2, w, target)


def _conv_tile(cur, prev8, w_ref, b_ref, kw):
    y = b_ref[...] + cur * w_ref[kw - 1:kw, :]
    for k in range(kw - 1):
        y = y + _shift_down(cur, prev8, kw - 1 - k) * w_ref[k:k + 1, :]
    return y


_SUB = 16


def _sub_rows(s):
    return pl.ds(0 if isinstance(s, int) else pl.multiple_of(s * _SUB, _SUB), _SUB)


def _window(x_ref, prev8, s):
    if isinstance(s, int):
        return jnp.concatenate([prev8, x_ref[0:_SUB, :]], axis=0)
    return x_ref[pl.ds(pl.multiple_of(s * _SUB - SUBLANES, SUBLANES), _SUB + SUBLANES), :]


def _conv_step(win, w, b, kw):
    taps = [win[SUBLANES:] if k == kw - 1 else pltpu.roll(win, kw - 1 - k, 0)[SUBLANES:] for k in range(kw)]
    y = b + taps[kw - 1] * w[kw - 1:kw, :]
    for k in range(kw - 1):
        y = y + taps[k] * w[k:k + 1, :]
    return y, taps


def _conv_dx_step(dpre, next8, w, kw):
    n = _SUB + SUBLANES
    win = jnp.concatenate([dpre, next8], axis=0)
    acc = dpre * w[kw - 1:kw, :]
    for k in range(kw - 1):
        acc = acc + pltpu.roll(win, n - (kw - 1 - k), 0)[0:_SUB] * w[k:k + 1, :]
    return acc


def _fold8(v):
    return functools.reduce(jnp.add, [v[r:r + SUBLANES] for r in range(0, _SUB, SUBLANES)])


def _row_tile(t):
    return _pick(t, (640, 512, 384, 256, 128))


def _ssd_conv_fwd(proj, col0, width, w, b, name):
    t = proj.shape[0]
    kw = w.shape[0]
    tr, tc = _row_tile(t), _pick(width, (512, 256, 128))
    c0, rb = col0 // tc, tr // SUBLANES
    assert col0 % tc == 0

    def body(x_ref, p_ref, w_ref, b_ref, o_ref):
        i = pl.program_id(1)
        prev8 = jnp.where(i > 0, p_ref[...], 0.0)
        pre = _conv_tile(x_ref[...], prev8, w_ref, b_ref, kw)
        rows = _iota((tr, 1), 0) + i * tr
        o_ref[...] = jnp.where(rows >= FRONT, _silu(pre), 0.0)

    return pl.pallas_call(
        body, name=name, grid=(width // tc, t // tr),
        in_specs=[pl.BlockSpec((tr, tc), lambda j, i: (i, c0 + j)),
                  pl.BlockSpec((SUBLANES, tc), lambda j, i: (jnp.maximum(i * rb - 1, 0), c0 + j)),
                  pl.BlockSpec((kw, tc), lambda j, i: (0, j)),
                  pl.BlockSpec((1, tc), lambda j, i: (0, j))],
        out_specs=pl.BlockSpec((tr, tc), lambda j, i: (i, j)),
        out_shape=jax.ShapeDtypeStruct((t, width), F32),
        compiler_params=_params(("parallel", "parallel")),
    )(proj, proj, w, b)


def _ssd_conv_bwd(proj, col0, width, w, b, dact, name, comm=None):
    t = proj.shape[0]
    kw = w.shape[0]
    tr, tc = _row_tile(t), _pick(width, (512, 256, 128))
    c0, rb, nrow = col0 // tc, tr // SUBLANES, t // tr

    def body(x_ref, p_ref, w_ref, b_ref, d_ref, o_ref, gw_ref, gb_ref, carry):
        i = pl.program_id(1)
        ti = nrow - 1 - i

        @pl.when(i == 0)
        def _():
            gw_ref[...] = jnp.zeros_like(gw_ref)
            gb_ref[...] = jnp.zeros_like(gb_ref)
            carry[...] = jnp.zeros_like(carry)

        w, b = w_ref[...], b_ref[...]
        prev8 = jnp.where(ti > 0, p_ref[...], 0.0)
        nsub = tr // _SUB

        def step(s, state):
            next8, gb8, gw8 = state
            pre, taps = _conv_step(_window(x_ref, prev8, s), w, b, kw)
            valid = _iota((_SUB, 1), 0) + (ti * tr + s * _SUB) >= FRONT
            dpre = jnp.where(valid, d_ref[_sub_rows(s), :] * _dsilu(pre), 0.0)
            o_ref[_sub_rows(s), :] = jnp.where(valid, _conv_dx_step(dpre, next8, w, kw), 0.0).astype(MXU_DTYPE)
            return (dpre[0:SUBLANES], gb8 + _fold8(dpre), tuple(g + _fold8(dpre * tp) for g, tp in zip(gw8, taps)))

        zero8 = jnp.zeros((SUBLANES, tc), F32)
        state = lax.fori_loop(0, nsub - 1, lambda n, st: step(nsub - 1 - n, st), (carry[...], zero8, (zero8,) * kw))
        next8, gb8, gw8 = step(0, state)
        carry[...] = next8
        gb_ref[...] += jnp.sum(gb8, axis=0, keepdims=True)
        for k in range(kw):
            gw_ref[k:k + 1, :] += jnp.sum(gw8[k], axis=0, keepdims=True)

    return _call(
        body, name, (width // tc, nrow),
        [pl.BlockSpec((tr, tc), lambda j, i: (nrow - 1 - i, c0 + j)),
         pl.BlockSpec((SUBLANES, tc), lambda j, i: (jnp.maximum((nrow - 1 - i) * rb - 1, 0), c0 + j)),
         pl.BlockSpec((kw, tc), lambda j, i: (0, j)),
         pl.BlockSpec((1, tc), lambda j, i: (0, j)),
         pl.BlockSpec((tr, tc), lambda j, i: (nrow - 1 - i, j))],
        [pl.BlockSpec((tr, tc), lambda j, i: (nrow - 1 - i, j)),
         pl.BlockSpec((SUBLANES, tc), lambda j, i: (0, j)),
         pl.BlockSpec((1, tc), lambda j, i: (0, j))],
        [jax.ShapeDtypeStruct((t, width), MXU_DTYPE), jax.ShapeDtypeStruct((SUBLANES, width), F32),
         jax.ShapeDtypeStruct((1, width), F32)],
        [pltpu.VMEM((SUBLANES, tc), F32)], ("parallel", "arbitrary"), (proj, proj, w, b, dact), comm)


def _ffn_conv_fwd(up, w, b, name):
    t, f2 = up.shape
    f = f2 // 2
    kw = w.shape[0]
    tr, tc = _row_tile(t), _pick(f, (256, 128))
    nf, rb = f // tc, tr // SUBLANES

    def body(xg, pg, xv, pv, wg, wv, bg, bv, o_ref):
        i = pl.program_id(1)
        ag = _conv_tile(xg[...], jnp.where(i > 0, pg[...], 0.0), wg, bg, kw)
        av = _conv_tile(xv[...], jnp.where(i > 0, pv[...], 0.0), wv, bv, kw)
        o_ref[...] = (_silu(ag) * av).astype(MXU_DTYPE)

    def cur(off):
        return pl.BlockSpec((tr, tc), lambda j, i: (i, j + off))

    def prev(off):
        return pl.BlockSpec((SUBLANES, tc), lambda j, i: (jnp.maximum(i * rb - 1, 0), j + off))

    def par(rows, off):
        return pl.BlockSpec((rows, tc), lambda j, i: (0, j + off))

    return pl.pallas_call(
        body, name=name, grid=(nf, t // tr),
        in_specs=[cur(0), prev(0), cur(nf), prev(nf), par(kw, 0), par(kw, nf), par(1, 0), par(1, nf)],
        out_specs=pl.BlockSpec((tr, tc), lambda j, i: (i, j)),
        out_shape=jax.ShapeDtypeStruct((t, f), MXU_DTYPE),
        compiler_params=_params(("parallel", "parallel")),
    )(up, up, up, up, w, w, b, b)


def _ffn_conv_bwd(up, w, b, dact, name):
    t, f2 = up.shape
    f = f2 // 2
    kw = w.shape[0]
    tr, tc = _row_tile(t), _pick(f, (256, 128))
    nf, rb, nrow = f // tc, tr // SUBLANES, t // tr

    def body(xg, pg, xv, pv, wg_ref, wv_ref, bg_ref, bv_ref, d_ref, og_ref, ov_ref, gwg_ref, gwv_ref, gbg_ref, gbv_ref,
             cg, cv):
        i = pl.program_id(1)
        ti = nrow - 1 - i

        @pl.when(i == 0)
        def _():
            for r in (gwg_ref, gwv_ref, gbg_ref, gbv_ref, cg, cv):
                r[...] = jnp.zeros_like(r)

        wg, wv, bg, bv = wg_ref[...], wv_ref[...], bg_ref[...], bv_ref[...]
        p8g, p8v = jnp.where(ti > 0, pg[...], 0.0), jnp.where(ti > 0, pv[...], 0.0)
        nsub = tr // _SUB

        def step(s, state):
            ng, nv, gbg8, gbv8, gwg8, gwv8 = state
            ag, tg = _conv_step(_window(xg, p8g, s), wg, bg, kw)
            av, tv = _conv_step(_window(xv, p8v, s), wv, bv, kw)
            d = d_ref[_sub_rows(s), :]
            sg = jax.nn.sigmoid(ag)
            dag = d * av * (sg * (1.0 + ag * (1.0 - sg)))
            dav = d * (ag * sg)
            valid = _iota((_SUB, 1), 0) + (ti * tr + s * _SUB) >= FRONT
            og_ref[_sub_rows(s), :] = jnp.where(valid, _conv_dx_step(dag, ng, wg, kw), 0.0).astype(MXU_DTYPE)
            ov_ref[_sub_rows(s), :] = jnp.where(valid, _conv_dx_step(dav, nv, wv, kw), 0.0).astype(MXU_DTYPE)
            return (dag[0:SUBLANES], dav[0:SUBLANES], gbg8 + _fold8(dag), gbv8 + _fold8(dav),
                    tuple(g + _fold8(dag * tp) for g, tp in zip(gwg8, tg)),
                    tuple(g + _fold8(dav * tp) for g, tp in zip(gwv8, tv)))

        zero8 = jnp.zeros((SUBLANES, tc), F32)
        state = lax.fori_loop(0, nsub - 1, lambda n, st: step(nsub - 1 - n, st),
                              (cg[...], cv[...], zero8, zero8, (zero8,) * kw, (zero8,) * kw))
        ng, nv, gbg8, gbv8, gwg8, gwv8 = step(0, state)
        cg[...] = ng
        cv[...] = nv
        gbg_ref[...] += jnp.sum(gbg8, axis=0, keepdims=True)
        gbv_ref[...] += jnp.sum(gbv8, axis=0, keepdims=True)
        for k in range(kw):
            gwg_ref[k:k + 1, :] += jnp.sum(gwg8[k], axis=0, keepdims=True)
            gwv_ref[k:k + 1, :] += jnp.sum(gwv8[k], axis=0, keepdims=True)

    def cur(off):
        return pl.BlockSpec((tr, tc), lambda j, i: (nrow - 1 - i, j + off))

    def prev(off):
        return pl.BlockSpec((SUBLANES, tc), lambda j, i: (jnp.maximum((nrow - 1 - i) * rb - 1, 0), j + off))

    def par(rows, off):
        return pl.BlockSpec((rows, tc), lambda j, i: (0, j + off))

    acc8 = pl.BlockSpec((SUBLANES, tc), lambda j, i: (0, j))
    acc1 = pl.BlockSpec((1, tc), lambda j, i: (0, j))
    return pl.pallas_call(
        body, name=name, grid=(nf, nrow),
        in_specs=[cur(0), prev(0), cur(nf), prev(nf), par(kw, 0), par(kw, nf), par(1, 0), par(1, nf), cur(0)],
        out_specs=[cur(0), cur(0), acc8, acc8, acc1, acc1],
        out_shape=[jax.ShapeDtypeStruct((t, f), MXU_DTYPE), jax.ShapeDtypeStruct((t, f), MXU_DTYPE),
                   jax.ShapeDtypeStruct((SUBLANES, f), F32), jax.ShapeDtypeStruct((SUBLANES, f), F32),
                   jax.ShapeDtypeStruct((1, f), F32), jax.ShapeDtypeStruct((1, f), F32)],
        scratch_shapes=[pltpu.VMEM((SUBLANES, tc), F32), pltpu.VMEM((SUBLANES, tc), F32)],
        compiler_params=_params(("parallel", "arbitrary")),
    )(up, up, up, up, w, w, b, b, dact)


def _gate_fwd(bs, br, proj, c_gs, c_gr, name):
    t, d = bs.shape
    tr = _row_tile(t)

    def body(bs_ref, br_ref, gs_ref, gr_ref, o_ref):
        o_ref[...] = (jax.nn.sigmoid(gs_ref[...]) * bs_ref[...] + jax.nn.sigmoid(gr_ref[...]) * br_ref[...]).astype(MXU_DTYPE)

    row = pl.BlockSpec((tr, d), lambda i: (i, 0))
    return pl.pallas_call(
        body, name=name, grid=(t // tr,),
        in_specs=[row, row, pl.BlockSpec((tr, d), lambda i: (i, c_gs // d)), pl.BlockSpec((tr, d), lambda i: (i, c_gr // d))],
        out_specs=row, out_shape=jax.ShapeDtypeStruct((t, d), MXU_DTYPE),
        compiler_params=_params(("parallel",)),
    )(bs, br, proj, proj)


def _gate_bwd(dm, bs, br, proj, c_gs, c_gr, name):
    t, d = bs.shape
    tr = _row_tile(t)

    def body(dm_ref, bs_ref, br_ref, gs_ref, gr_ref, dbs_ref, dbr_ref, dgg_ref):
        g = dm_ref[...]
        ss, sr = jax.nn.sigmoid(gs_ref[...]), jax.nn.sigmoid(gr_ref[...])
        dbs_ref[...] = (g * ss).astype(MXU_DTYPE)
        dbr_ref[...] = (g * sr).astype(MXU_DTYPE)
        dgg_ref[:, :d] = (g * bs_ref[...] * ss * (1.0 - ss)).astype(MXU_DTYPE)
        dgg_ref[:, d:] = (g * br_ref[...] * sr * (1.0 - sr)).astype(MXU_DTYPE)

    row = pl.BlockSpec((tr, d), lambda i: (i, 0))
    out = jax.ShapeDtypeStruct((t, d), MXU_DTYPE)
    return pl.pallas_call(
        body, name=name, grid=(t // tr,),
        in_specs=[row, row, row, pl.BlockSpec((tr, d), lambda i: (i, c_gs // d)), pl.BlockSpec((tr, d), lambda i: (i, c_gr // d))],
        out_specs=[row, row, pl.BlockSpec((tr, 2 * d), lambda i: (i, 0))],
        out_shape=[out, out, jax.ShapeDtypeStruct((t, 2 * d), MXU_DTYPE)],
        compiler_params=_params(("parallel",)),
    )(dm, bs, br, proj, proj)


def _ret_consts(h):
    lg = math.log(1.0 - 2.0 ** (-5.0 - h))
    l = _iota((CHUNK, 1), 0).astype(F32)
    diff = l - _iota((1, CHUNK), 1).astype(F32)
    dm = jnp.exp(jnp.where(diff >= 0, diff * lg, -jnp.inf))
    dmt = jnp.exp(jnp.where(diff <= 0, -diff * lg, -jnp.inf))
    cs = jnp.exp((l + 1.0) * lg)
    kdec = jnp.exp((CHUNK - 1.0 - l) * lg)
    return dm, dmt, cs, kdec, math.exp(CHUNK * lg)


def _ret_fwd(proj, c_q, c_k, c_v, c_g, cos, sin, d, name, comm=None):
    t = proj.shape[0]
    nc = t // CHUNK
    hq, hv = d // RET_HEADS, 2 * d // RET_HEADS
    half = hq // 2
    scale = hq ** -0.5

    def body(q_ref, k_ref, v_ref, g_ref, cos_ref, sin_ref, o_ref, y_ref, qr_ref, kr_ref, st_ref, rs):
        @pl.when(pl.program_id(0) == 0)
        def _():
            rs[...] = jnp.zeros_like(rs)

        co, si = cos_ref[...], sin_ref[...]
        for h in range(RET_HEADS):
            dm, _, cs, kdec, gam = _ret_consts(h)
            q1, q2 = q_ref[:, h * hq:h * hq + half], q_ref[:, h * hq + half:(h + 1) * hq]
            k1, k2 = k_ref[:, h * hq:h * hq + half], k_ref[:, h * hq + half:(h + 1) * hq]
            qr = jnp.concatenate([q1 * co - q2 * si, q2 * co + q1 * si], axis=1)
            kr = jnp.concatenate([k1 * co - k2 * si, k2 * co + k1 * si], axis=1) * scale
            qr_ref[:, h * hq:(h + 1) * hq] = qr.astype(MXU_DTYPE)
            kr_ref[:, h * hq:(h + 1) * hq] = kr.astype(MXU_DTYPE)
            v = v_ref[:, h * hv:(h + 1) * hv]
            r_in = rs[h * hq:(h + 1) * hq, :]
            st_ref[0, h * hq:(h + 1) * hq, :] = r_in.astype(MXU_DTYPE)
            s = _dot_nt(qr, kr) * dm
            o = _dot(s, v) + cs * _dot(qr, r_in)
            rs[h * hq:(h + 1) * hq, :] = gam * r_in + _dot_tn(kr * kdec, v)
            o_ref[:, h * hv:(h + 1) * hv] = o
            on = o * lax.rsqrt(jnp.mean(o * o, axis=1, keepdims=True) + EPS)
            y_ref[:, h * hv:(h + 1) * hv] = (_silu(g_ref[:, h * hv:(h + 1) * hv]) * on).astype(MXU_DTYPE)

    def col(width, c0):
        return pl.BlockSpec((CHUNK, width), lambda i: (i, c0 // width))

    tab = pl.BlockSpec((CHUNK, half), lambda i: (i, 0))
    return _call(
        body, name, (nc,),
        [col(d, c_q), col(d, c_k), col(2 * d, c_v), col(2 * d, c_g), tab, tab],
        [col(2 * d, 0), col(2 * d, 0), col(d, 0), col(d, 0), pl.BlockSpec((1, d, hv), lambda i: (i, 0, 0))],
        [jax.ShapeDtypeStruct((t, 2 * d), F32), jax.ShapeDtypeStruct((t, 2 * d), MXU_DTYPE),
         jax.ShapeDtypeStruct((t, d), MXU_DTYPE), jax.ShapeDtypeStruct((t, d), MXU_DTYPE),
         jax.ShapeDtypeStruct((nc, d, hv), MXU_DTYPE)],
        [pltpu.VMEM((d, hv), F32)], ("arbitrary",), (proj, proj, proj, proj, cos, sin), comm)


def _ret_bwd(dy, proj, c_v, c_g, o, qr, kr, st, cos, sin, d, name, comm=None):
    t = proj.shape[0]
    nc = t // CHUNK
    hq, hv = d // RET_HEADS, 2 * d // RET_HEADS
    half = hq // 2
    scale = hq ** -0.5

    def body(dy_ref, v_ref, g_ref, o_ref, qr_ref, kr_ref, st_ref, cos_ref, sin_ref, dqk_ref, dvg_ref, drs):
        dq_ref, dk_ref = dqk_ref.at[:, pl.ds(0, d)], dqk_ref.at[:, pl.ds(d, d)]
        dv_ref, dg_ref = dvg_ref.at[:, pl.ds(0, 2 * d)], dvg_ref.at[:, pl.ds(2 * d, 2 * d)]

        @pl.when(pl.program_id(0) == 0)
        def _():
            drs[...] = jnp.zeros_like(drs)

        co, si = cos_ref[...], sin_ref[...]
        for h in range(RET_HEADS):
            dm, dmt, cs, kdec, gam = _ret_consts(h)
            vs = slice(h * hv, (h + 1) * hv)
            qs = slice(h * hq, (h + 1) * hq)
            o_h = o_ref[:, vs]
            g_h = g_ref[:, vs]
            d_y = dy_ref[:, vs]
            r = lax.rsqrt(jnp.mean(o_h * o_h, axis=1, keepdims=True) + EPS)
            on = o_h * r
            d_on = d_y * _silu(g_h)
            dg_ref[:, vs] = (d_y * on * _dsilu(g_h)).astype(MXU_DTYPE)
            d_o = r * (d_on - on * jnp.mean(d_on * on, axis=1, keepdims=True))
            q_h, k_h, v_h = qr_ref[:, qs], kr_ref[:, qs], v_ref[:, vs]
            r_in = st_ref[0, qs, :]
            dr_n = drs[qs, :]
            csdo = cs * d_o
            ds = _dot_nt(d_o, v_h) * dm
            dst = _dot_nt(v_h, d_o) * dmt
            s_t = _dot_nt(k_h, q_h) * dmt
            dqr = _dot(ds, k_h) + _dot_nt(csdo, r_in)
            dkr = _dot(dst, q_h) + kdec * _dot_nt(v_h, dr_n)
            dv_ref[:, vs] = (_dot(s_t, d_o) + _dot(k_h.astype(F32) * kdec, dr_n)).astype(MXU_DTYPE)
            drs[qs, :] = gam * dr_n + _dot_tn(q_h, csdo)
            a1, a2 = dqr[:, :half], dqr[:, half:]
            dq_ref[:, qs] = jnp.concatenate([a1 * co + a2 * si, a2 * co - a1 * si], axis=1).astype(MXU_DTYPE)
            b1, b2 = dkr[:, :half] * scale, dkr[:, half:] * scale
            dk_ref[:, qs] = jnp.concatenate([b1 * co + b2 * si, b2 * co - b1 * si], axis=1).astype(MXU_DTYPE)

    def col(width, c0=0):
        return pl.BlockSpec((CHUNK, width), lambda i: (nc - 1 - i, c0 // width))

    tab = pl.BlockSpec((CHUNK, half), lambda i: (nc - 1 - i, 0))
    return _call(
        body, name, (nc,),
        [col(2 * d), col(2 * d, c_v), col(2 * d, c_g), col(2 * d), col(d), col(d),
         pl.BlockSpec((1, d, hv), lambda i: (nc - 1 - i, 0, 0)), tab, tab],
        [col(2 * d), col(4 * d)],
        [jax.ShapeDtypeStruct((t, 2 * d), MXU_DTYPE), jax.ShapeDtypeStruct((t, 4 * d), MXU_DTYPE)],
        [pltpu.VMEM((d, hv), F32)], ("arbitrary",), (dy, proj, proj, o, qr, kr, st, cos, sin), comm)


def _ssd_small(dtraw_ref, dtb_ref, alog_ref, chunk_idx, nh):
    rows = _iota((CHUNK, 1), 0)
    ok = ((rows >= FRONT) | (chunk_idx > 0)) & (_iota((1, LANES), 1) < nh)
    z = dtraw_ref[...] + dtb_ref[...]
    dt = jnp.where(ok, jax.nn.softplus(z), 0.0)
    sig = jnp.where(ok, jax.nn.sigmoid(z), 0.0)
    a = jnp.where(_iota((1, LANES), 1) < nh, -jnp.exp(alog_ref[...]), 0.0)
    tri = (_iota((CHUNK, CHUNK), 0) >= _iota((CHUNK, CHUNK), 1)).astype(F32)
    acs = _dot01(tri, dt * a, "b", 3)
    return dt, sig, a, acs, acs.T


def _head_expand(g, hpg, gw):
    shift = int(math.log2(SSD_HEAD_DIM))
    return (_iota((LANES, gw), 0) == g * hpg + lax.shift_right_logical(_iota((LANES, gw), 1), shift)).astype(F32)


def _ssd_fwd(xa, proj, c_dt, c_z, dtb, alog, dvec, nw, di, name, comm=None):
    t = xa.shape[0]
    nc = t // CHUNK
    nh = di // SSD_HEAD_DIM
    hpg = nh // SSD_GROUPS
    gw = di // SSD_GROUPS
    n = SSD_STATE
    gn = SSD_GROUPS * n
    hd = SSD_HEAD_DIM

    def body(x_ref, b_ref, c_ref, dtraw_ref, z_ref, dtb_ref, alog_ref, d_ref, nw_ref,
             y_ref, ys_ref, st_ref, hts, xdt_s):
        c = pl.program_id(0)

        @pl.when(c == 0)
        def _():
            hts[...] = jnp.zeros_like(hts)

        dt, _, _, acs, acs_t = _ssd_small(dtraw_ref, dtb_ref, alog_ref, c, nh)
        tri = _iota((CHUNK, CHUNK), 0) >= _iota((CHUNK, CHUNK), 1)
        dvec8 = jnp.broadcast_to(d_ref[...], (SUBLANES, LANES))
        for g in range(SSD_GROUPS):
            gs = slice(g * gw, (g + 1) * gw)
            ns = slice(g * n, (g + 1) * n)
            e_mat = _head_expand(g, hpg, gw)
            ax = _dot01(acs, e_mat, "a", 3)
            dtx = _dot01(dt, e_mat, "a", 3)
            dx = _dot01(dvec8, e_mat, "a", 3)[0:1, :]
            xg, bg, cg = x_ref[:, gs], b_ref[:, ns], c_ref[:, ns]
            xdt = xg * dtx
            xdt_s[...] = xdt.astype(MXU_DTYPE)
            cb = _dot_nt(cg, bg)
            ht = hts[ns, :]
            st_ref[0, ns, :] = ht.astype(MXU_DTYPE)
            y_ref[:, gs] = jnp.exp(ax) * _dot(cg, ht) + dx * xg
            for hh in range(hpg):
                h = g * hpg + hh
                lmat = jnp.exp(jnp.where(tri, acs[:, h:h + 1] - acs_t[h:h + 1, :], -jnp.inf))
                hs = slice(g * gw + hh * hd, g * gw + (hh + 1) * hd)
                y_ref[:, hs] += _dot(cb * lmat, xdt_s[:, hh * hd:(hh + 1) * hd])
            aend = ax[CHUNK - 1:CHUNK, :]
            hts[ns, :] = jnp.exp(aend) * ht + _dot_tn(bg, xdt * jnp.exp(aend - ax))
        for g in range(SSD_GROUPS):
            gs = slice(g * gw, (g + 1) * gw)
            yz = y_ref[:, gs] * _silu(z_ref[:, gs])
            r = lax.rsqrt(jnp.mean(yz * yz, axis=1, keepdims=True) + EPS)
            ys_ref[:, gs] = (yz * r * nw_ref[:, gs]).astype(MXU_DTYPE)

    def col(width, c0, arr_is_xa=False):
        return pl.BlockSpec((CHUNK, width), lambda i: (i, c0 // width))

    vec = pl.BlockSpec((1, LANES), lambda i: (0, 0))
    assert di % gn == 0 and c_dt % LANES == 0 and c_z % di == 0
    return _call(
        body, name, (nc,),
        [col(di, 0), col(gn, di), col(gn, di + gn), col(LANES, c_dt), col(di, c_z), vec, vec, vec,
         pl.BlockSpec((1, di), lambda i: (0, 0))],
        [col(di, 0), col(di, 0), pl.BlockSpec((1, gn, gw), lambda i: (i, 0, 0))],
        [jax.ShapeDtypeStruct((t, di), F32), jax.ShapeDtypeStruct((t, di), MXU_DTYPE),
         jax.ShapeDtypeStruct((nc, gn, gw), MXU_DTYPE)],
        [pltpu.VMEM((gn, gw), F32), pltpu.VMEM((CHUNK, gw), MXU_DTYPE)], ("arbitrary",),
        (xa, xa, xa, proj, proj, dtb, alog, dvec, nw), comm)


def _ssd_bwd(dys, xa, proj, c_dt, c_z, ypre, st, dtb, alog, dvec, nw, di, name, comm=None):
    t = xa.shape[0]
    nc = t // CHUNK
    nh = di // SSD_HEAD_DIM
    hpg = nh // SSD_GROUPS
    gw = di // SSD_GROUPS
    n = SSD_STATE
    gn = SSD_GROUPS * n
    hd = SSD_HEAD_DIM

    def body(dys_ref, x_ref, b_ref, c_ref, dtraw_ref, z_ref, y_ref, st_ref, dtb_ref, alog_ref, d_ref, nw_ref,
             dxa_ref, dz_ref, ddt_ref, gb_ref, ga_ref, gd_ref, gnw_ref, dhts, dy_s, xdt_s, dxdt_s):
        i = pl.program_id(0)
        c = nc - 1 - i

        @pl.when(i == 0)
        def _():
            dhts[...] = jnp.zeros_like(dhts)
            gb_ref[...] = jnp.zeros_like(gb_ref)
            ga_ref[...] = jnp.zeros_like(ga_ref)
            gd_ref[...] = jnp.zeros_like(gd_ref)
            gnw_ref[...] = jnp.zeros_like(gnw_ref)

        dt, sig, a, acs, acs_t = _ssd_small(dtraw_ref, dtb_ref, alog_ref, c, nh)
        tri = _iota((CHUNK, CHUNK), 0) >= _iota((CHUNK, CHUNK), 1)
        triu = _iota((CHUNK, CHUNK), 0) <= _iota((CHUNK, CHUNK), 1)
        lane = _iota((1, LANES), 1)
        rows = _iota((CHUNK, 1), 0)
        head_row = _iota((LANES, 1), 0)
        dvec8 = jnp.broadcast_to(d_ref[...], (SUBLANES, LANES))
        da = jnp.zeros((CHUNK, LANES), F32)
        da_t = jnp.zeros((LANES, CHUNK), F32)
        ddt = jnp.zeros((CHUNK, LANES), F32)
        gd = jnp.zeros((1, LANES), F32)
        for g in range(SSD_GROUPS):
            gs = slice(g * gw, (g + 1) * gw)
            ns = slice(g * n, (g + 1) * n)
            y_g, z_g = y_ref[:, gs], z_ref[:, gs]
            sz = _silu(z_g)
            yz = y_g * sz
            r = lax.rsqrt(jnp.mean(yz * yz, axis=1, keepdims=True) + EPS)
            nrm = yz * r
            dyo = dys_ref[:, gs]
            gnw_ref[:, gs] += jnp.sum(dyo * nrm, axis=0, keepdims=True)
            dn = dyo * nw_ref[:, gs]
            dyz = r * (dn - nrm * jnp.mean(dn * nrm, axis=1, keepdims=True))
            dz_ref[:, gs] = (dyz * y_g * _dsilu(z_g)).astype(MXU_DTYPE)
            dy_g = dyz * sz
            dy_s[...] = dy_g.astype(MXU_DTYPE)
            e_mat = _head_expand(g, hpg, gw)
            ax = _dot01(acs, e_mat, "a", 3)
            dtx = _dot01(dt, e_mat, "a", 3)
            dx = _dot01(dvec8, e_mat, "a", 3)[0:1, :]
            xg, bg, cg = x_ref[:, gs], b_ref[:, ns], c_ref[:, ns]
            xdt = xg * dtx
            xdt_s[...] = xdt.astype(MXU_DTYPE)
            aend = ax[CHUNK - 1:CHUNK, :]
            e = jnp.exp(aend - ax)
            ea = jnp.exp(ax)
            eend = jnp.exp(aend)
            htp = st_ref[0, ns, :].astype(F32)
            dht = dhts[ns, :]
            cb = _dot_nt(cg, bg)
            q = _dot(bg, dht)
            dxdt_s[...] = e * q
            wl = e * q * xdt
            d_b = _dot_nt(e * xdt, dht)
            yi = ea * _dot(cg, htp)
            eady = ea * dy_g
            d_c = _dot_nt(eady, htp)
            t1 = jnp.sum(dht * htp, axis=0, keepdims=True) * eend
            dhts[ns, :] = eend * dht + _dot_tn(cg, eady)
            da = da + _dot01(dy_g * yi - wl, e_mat, "a", 3, _NT)
            tail = jnp.broadcast_to(jnp.sum(wl, axis=0, keepdims=True) + t1, (SUBLANES, gw))
            da_end = _dot01(tail, e_mat, "a", 3, _NT)[0:1, :]
            da = da + jnp.where(rows == CHUNK - 1, da_end, 0.0)
            dcb = jnp.zeros((CHUNK, CHUNK), F32)
            for hh in range(hpg):
                h = g * hpg + hh
                lmat = jnp.exp(jnp.where(tri, acs[:, h:h + 1] - acs_t[h:h + 1, :], -jnp.inf))
                hl = slice(hh * hd, (hh + 1) * hd)
                dy_h, xdt_h = dy_s[:, hl], xdt_s[:, hl]
                dxdt_s[:, hl] += _dot_tn(cb * lmat, dy_h)
                dml = _dot_nt(dy_h, xdt_h) * lmat
                dcb = dcb + dml
                gmat = dml * cb
                da = da + jnp.where(lane == h, jnp.sum(gmat, axis=1, keepdims=True), 0.0)
                da_t = da_t - jnp.where(head_row == h, jnp.sum(gmat, axis=0, keepdims=True), 0.0)
            d_c = d_c + _dot(dcb, bg)
            d_b = d_b + _dot_tn(dcb, cg)
            dxdt = dxdt_s[...]
            dxa_ref[:, gs] = dxdt * dtx + dx * dy_g
            dxa_ref[:, di + g * n:di + (g + 1) * n] = d_b
            dxa_ref[:, di + gn + g * n:di + gn + (g + 1) * n] = d_c
            ddt = ddt + _dot01(dxdt * xg, e_mat, "a", 3, _NT)
            gd8 = jnp.broadcast_to(jnp.sum(dy_g * xg, axis=0, keepdims=True), (SUBLANES, gw))
            gd = gd + _dot01(gd8, e_mat, "a", 3, _NT)[0:1, :]
        da = da + da_t.T
        triu_f = triu.astype(F32)
        ddta = _dot01(triu_f, da, "b", 3)
        ddt = ddt + ddta * a
        draw = ddt * sig
        ddt_ref[...] = draw.astype(MXU_DTYPE)
        gb_ref[...] += jnp.sum(draw, axis=0, keepdims=True)
        ga_ref[...] += jnp.sum(ddta * dt, axis=0, keepdims=True) * a
        gd_ref[...] += gd

    def col(width, c0):
        return pl.BlockSpec((CHUNK, width), lambda i: (nc - 1 - i, c0 // width))

    vec = pl.BlockSpec((1, LANES), lambda i: (0, 0))
    wide = pl.BlockSpec((1, di), lambda i: (0, 0))
    wa = di + 2 * gn
    return _call(
        body, name, (nc,),
        [col(di, 0), col(di, 0), col(gn, di), col(gn, di + gn), col(LANES, c_dt), col(di, c_z), col(di, 0),
         pl.BlockSpec((1, gn, gw), lambda i: (nc - 1 - i, 0, 0)), vec, vec, vec, wide],
        [col(wa, 0), col(di, 0), col(LANES, 0), vec, vec, vec, wide],
        [jax.ShapeDtypeStruct((t, wa), F32), jax.ShapeDtypeStruct((t, di), MXU_DTYPE),
         jax.ShapeDtypeStruct((t, LANES), MXU_DTYPE), jax.ShapeDtypeStruct((1, LANES), F32),
         jax.ShapeDtypeStruct((1, LANES), F32), jax.ShapeDtypeStruct((1, LANES), F32),
         jax.ShapeDtypeStruct((1, di), F32)],
        [pltpu.VMEM((gn, gw), F32), pltpu.VMEM((CHUNK, gw), MXU_DTYPE), pltpu.VMEM((CHUNK, gw), MXU_DTYPE),
         pltpu.VMEM((CHUNK, gw), F32)], ("arbitrary",),
        (dys, xa, xa, xa, proj, proj, ypre, st, dtb, alog, dvec, nw), comm)


def _adam_math(w, g, m, v):
    m2 = ADAM_B1 * m + (1.0 - ADAM_B1) * g
    v2 = ADAM_B2 * v + (1.0 - ADAM_B2) * (g * g)
    m_hat = m2 / (1.0 - ADAM_B1 ** ADAM_STEP)
    v_hat = v2 / (1.0 - ADAM_B2 ** ADAM_STEP)
    delta = -ADAM_LR * (m_hat / (jnp.sqrt(v_hat) + ADAM_EPS) + ADAM_WD * w)
    return delta, m2, v2


def _adam_big(w, g_mine, g_sib, m, v, core, name):
    r, c = w.shape
    h = r // 2
    tr = _pick(h, (128, 64, 32, 16, 8))
    nbh = h // tr

    def body(core_ref, w_ref, a_ref, b_ref, m_ref, v_ref, g_ref, d_ref, m2_ref, v2_ref):
        g = jnp.where(pl.program_id(0) // nbh == core_ref[0], a_ref[...], b_ref[...])
        delta, m2, v2 = _adam_math(w_ref[...], g, m_ref[...], v_ref[...])
        g_ref[...] = g
        d_ref[...] = delta
        m2_ref[...] = m2
        v2_ref[...] = v2

    blk = pl.BlockSpec((tr, c), lambda i, core_ref: (i, 0))
    hblk = pl.BlockSpec((tr, c), lambda i, core_ref: (i % nbh, 0))
    out = jax.ShapeDtypeStruct((r, c), F32)
    return pl.pallas_call(
        body, name=name,
        grid_spec=pltpu.PrefetchScalarGridSpec(num_scalar_prefetch=1, grid=(2 * nbh,),
                                               in_specs=[blk, hblk, hblk, blk, blk], out_specs=[blk] * 4),
        out_shape=[out] * 4, compiler_params=_params(("parallel",)),
    )(core, w, g_mine, g_sib, m, v)


def _pair_sum(g, sib, core, name):
    _, r, c = g.shape
    h = r // 2
    tr = _pick(h, (128, 64, 32, 16))
    nb = h // tr

    def body(core_ref, g_ref, s_ref, o_ref):
        o_ref[...] = (g_ref[...] + s_ref[...]).astype(WIRE_DTYPE)

    return pl.pallas_call(
        body, name=name,
        grid_spec=pltpu.PrefetchScalarGridSpec(
            num_scalar_prefetch=1, grid=(4, nb),
            in_specs=[pl.BlockSpec((1, tr, c), lambda j, i, core_ref: (j, core_ref[0] * nb + i, 0)),
                      pl.BlockSpec((1, tr, c), lambda j, i, core_ref: (j, i, 0))],
            out_specs=pl.BlockSpec((1, tr, c), lambda j, i, core_ref: (j, i, 0))),
        out_shape=jax.ShapeDtypeStruct((4, h, c), WIRE_DTYPE), compiler_params=_params(("parallel", "parallel")),
    )(core, g, sib)


def _sum4(parts, name):
    _, r, c = parts.shape
    tr = _pick(r, (128, 64, 32, 16, 8))

    def body(p_ref, o_ref):
        acc = p_ref[0].astype(F32)
        for j in range(1, 4):
            acc = acc + p_ref[j].astype(F32)
        o_ref[...] = acc

    return pl.pallas_call(
        body, name=name, grid=(r // tr,),
        in_specs=[pl.BlockSpec((4, tr, c), lambda i: (0, i, 0))],
        out_specs=pl.BlockSpec((tr, c), lambda i: (i, 0)),
        out_shape=jax.ShapeDtypeStruct((r, c), F32),
        compiler_params=_params(("parallel",)),
    )(parts)


def _adam_small(w, gparts, m, v, name):
    r = w.shape[0]

    def body(w_ref, g_ref, m_ref, v_ref, go_ref, d_ref, m2_ref, v2_ref):
        g = g_ref[0]
        for j in range(1, 8):
            g = g + g_ref[j]
        delta, m2, v2 = _adam_math(w_ref[...], g, m_ref[...], v_ref[...])
        go_ref[...] = g
        d_ref[...] = delta
        m2_ref[...] = m2
        v2_ref[...] = v2

    out = jax.ShapeDtypeStruct((r, LANES), F32)
    return pl.pallas_call(body, name=name, out_shape=[out] * 4)(w, gparts, m, v)


def _allgather8(v, name):
    def body(src, dst, send, recv, loc):
        x, y, c = lax.axis_index("x"), lax.axis_index("y"), lax.axis_index("c")
        mine = 4 * x + 2 * y + c
        lc = pltpu.make_async_copy(src, dst.at[mine], loc)
        lc.start()
        copies = []
        for k in range(1, 8):
            fx, fy, fc = (k >> 2) & 1, (k >> 1) & 1, k & 1
            peer = (1 - x if fx else x, 1 - y if fy else y, 1 - c if fc else c)
            cp = pltpu.make_async_remote_copy(src_ref=src, dst_ref=dst.at[mine], send_sem=send.at[k - 1],
                                              recv_sem=recv.at[k - 1], device_id=peer, device_id_type=MESH)
            cp.start()
            copies.append(cp)
        for cp in copies:
            cp.wait()
        lc.wait()

    anyspec = pl.BlockSpec(memory_space=pl.ANY)
    return pl.pallas_call(
        body, name=name, in_specs=[anyspec], out_specs=anyspec,
        out_shape=jax.ShapeDtypeStruct((8,) + v.shape, v.dtype),
        scratch_shapes=[pltpu.SemaphoreType.DMA((7,)), pltpu.SemaphoreType.DMA((7,)), pltpu.SemaphoreType.DMA],
        compiler_params=pltpu.CompilerParams(has_side_effects=True),
    )(v)


def _pack(parts):
    flat = jnp.concatenate([p.reshape(-1).astype(F32) for p in parts])
    pad = (-flat.shape[0]) % (32 * LANES)
    return jnp.pad(flat, (0, pad)).reshape(-1, LANES)


def _unpack(slab, shapes):
    flat = slab.reshape(-1)
    out, off = [], 0
    for s in shapes:
        size = int(np.prod(s))
        out.append(flat[off:off + size].reshape(s))
        off += size
    return out


def _pad_lanes(v):
    return jnp.pad(v.reshape(1, -1), ((0, 0), (0, LANES - v.shape[-1])))


def kernel(x, meta_tokens, mix_norm_w, w_in, ssd_conv_w, ssd_conv_b, ssd_dt_bias, ssd_A_log, ssd_D, ssd_norm_w, w_branch_ssd, w_branch_ret, w_out, ffn_norm_w, w_up, ffn_conv_w, ffn_conv_b, w_down, final_norm_w, loss_target, m_meta_tokens, m_mix_norm_w, m_w_in, m_ssd_conv_w, m_ssd_conv_b, m_ssd_dt_bias, m_ssd_A_log, m_ssd_D, m_ssd_norm_w, m_w_branch_ssd, m_w_branch_ret, m_w_out, m_ffn_norm_w, m_w_up, m_ffn_conv_w, m_ffn_conv_b, m_w_down, m_final_norm_w, v_meta_tokens, v_mix_norm_w, v_w_in, v_ssd_conv_w, v_ssd_conv_b, v_ssd_dt_bias, v_ssd_A_log, v_ssd_D, v_ssd_norm_w, v_w_branch_ssd, v_w_branch_ret, v_w_out, v_ffn_norm_w, v_w_up, v_ffn_conv_w, v_ffn_conv_b, v_w_down, v_final_norm_w):
    seq, d = x.shape[1], x.shape[2]
    t = CHUNK + seq
    di = 2 * d
    nh = di // SSD_HEAD_DIM
    gn = SSD_GROUPS * SSD_STATE
    cw = di + 2 * gn
    f = w_down.shape[1] * 4
    chip = 2 * lax.axis_index("x") + lax.axis_index("y")

    order = [("z", di), ("v", di), ("g", di), ("xbc", cw), ("q", d), ("k", d), ("gs", d), ("gr", d), ("dt", LANES)]
    col, acc = {}, 0
    for nm, wd in order:
        col[nm] = acc
        acc += wd
    wp = acc
    ref_order = [("z", di), ("xbc", cw), ("dt", nh), ("q", d), ("k", d), ("v", di), ("g", di), ("gs", d), ("gr", d)]
    ref_off, acc = {}, 0
    for nm, wd in ref_order:
        ref_off[nm] = (acc, wd)
        acc += wd
    in_dim = acc

    core = lax.axis_index("c").astype(jnp.int32).reshape(1)
    small_shapes = [meta_tokens.shape, ssd_conv_w.shape[1:], ffn_conv_w.shape[1:]]
    small_local = _pack([meta_tokens, ssd_conv_w[0], ffn_conv_w[0]])
    first_local = [w_in[0].astype(WIRE_DTYPE), small_local]
    first_half = _run_comm(_gather_ici(first_local), "gather_w_in_ici")
    g_in, g_small = [_with_own(g, own, chip)
                     for g, own in zip(_run_comm(_gather_d2d(first_half), "gather_w_in_d2d"), first_local)]
    rest_local = [a[0].astype(WIRE_DTYPE) for a in (w_branch_ssd, w_branch_ret, w_out, w_up, w_down)]
    w_in_full = jnp.moveaxis(g_in, 0, 1).reshape(d, in_dim)
    pieces = []
    for nm, wd in order:
        o, rw = ref_off[nm]
        p = w_in_full[:, o:o + rw]
        if rw < wd:
            p = jnp.pad(p, ((0, 0), (0, wd - rw)))
        pieces.append(p)
    w_p = jnp.concatenate(pieces, axis=1)
    smalls = [_unpack(g_small[j], small_shapes) for j in range(4)]
    meta_full = jnp.concatenate([s[0] for s in smalls], axis=1)
    scw = jnp.concatenate([s[1] for s in smalls], axis=1)
    fcw = jnp.concatenate([s[2] for s in smalls], axis=1)
    scb, fcb = ssd_conv_b, ffn_conv_b
    dtb, alog, dvec = _pad_lanes(ssd_dt_bias), _pad_lanes(ssd_A_log), _pad_lanes(ssd_D)
    fin_w = final_norm_w.reshape(1, d)

    hq = d // RET_HEADS
    pos = jnp.arange(t, dtype=F32) - FRONT
    inv_freq = ROPE_BASE ** (-jnp.linspace(0.0, 1.0, hq // 2, dtype=F32))
    ang = pos[:, None] * inv_freq[None, :]
    cos, sin = jnp.cos(ang), jnp.sin(ang)

    h0 = jnp.concatenate([jnp.zeros((FRONT, d), F32), meta_full, x[0]], axis=0)
    tm = _row_tile(t)
    u1 = _rms_fwd(h0, mix_norm_w, "rms1_fwd")
    proj = _mm(u1, w_p, "nn", F32, "proj", tm, _pick(wp, (1920, 1536, 1280, 1024, 896, 768, 640, 512, 384, 256, 128)), d)
    xa = _ssd_conv_fwd(proj, col["xbc"], cw, scw, scb, "ssd_conv_fwd")
    res = _ssd_fwd(xa, proj, col["dt"], col["z"], dtb, alog, dvec, ssd_norm_w, di, "ssd_fwd", comm=_gather_ici(rest_local))
    (ypre, yssd, st_ssd), rest_half = res[:3], res[3:]
    res = _ret_fwd(proj, col["q"], col["k"], col["v"], col["g"], cos, sin, d, "ret_fwd", comm=_gather_d2d(rest_half))
    o_ret, yret, qr, kr, st_ret = res[:5]
    g_bs, g_br, g_out, g_up, g_down = [_with_own(g, own, chip) for g, own in zip(res[5:], rest_local)]
    w_bs = g_bs.reshape(di, d)
    w_br = g_br.reshape(di, d)
    w_o = g_out.reshape(d, d)
    w_u = jnp.moveaxis(g_up, 0, 1).reshape(d, 2 * f)
    w_d = g_down.reshape(f, d)
    tn_d = _pick(d, (1024, 512, 256, 128))
    bs = _mm(yssd, w_bs, "nn", F32, "branch_ssd", tm, tn_d, _pick(di, (1024, 512, 256)))
    br = _mm(yret, w_br, "nn", F32, "branch_ret", tm, tn_d, _pick(di, (1024, 512, 256)))
    merged = _gate_fwd(bs, br, proj, col["gs"], col["gr"], "gate_fwd")
    h1 = _mm(merged, w_o, "nn", F32, "out_proj", tm, tn_d, d, res=h0)
    u2 = _rms_fwd(h1, ffn_norm_w, "rms2_fwd")
    tn_f = _pick(2 * f, (1408, 1024, 768, 512, 256, 128))
    up = _mm(u2, w_u, "nn", F32, "up_proj", tm, tn_f, d)
    act = _ffn_conv_fwd(up, fcw, fcb, "ffn_conv_fwd")
    tk_f = _pick(f, (1408, 768, 704, 512, 256, 128))
    h2 = _mm(act, w_d, "nn", F32, "down_proj", tm, tn_d, tk_f, res=h1)
    loss8, d_h2, g_fin = _loss_bwd(h2, fin_w, loss_target[0], "loss_head")

    tkt = _pick(t, (1664, 1280, 1024, 640, 512, 384, 256, 128))
    d_act = _mm(d_h2, w_d, "nt", F32, "d_act", tm, tk_f, d)
    g_wd = _mm(act, d_h2, "tn", F32, "g_w_down", tk_f, tn_d, tkt)
    d_upg, d_upv, g_fcwg, g_fcwv, g_fcbg, g_fcbv = _ffn_conv_bwd(up, fcw, fcb, d_act, "ffn_conv_bwd")
    g_fcw = jnp.concatenate([g_fcwg, g_fcwv], axis=1)
    g_fcb = jnp.concatenate([g_fcbg, g_fcbv], axis=1)
    d_u2 = _mm(d_upg, w_u[:, :f], "nt", F32, "d_u2_gate", tm, tn_d, tk_f)
    d_u2 = _mm(d_upv, w_u[:, f:], "nt", F32, "d_u2_value", tm, tn_d, tk_f, res=d_u2)
    g_wu = jnp.concatenate([_mm(u2, d_upg, "tn", F32, "g_w_up_gate", tn_d, tk_f, tkt),
                            _mm(u2, d_upv, "tn", F32, "g_w_up_value", tn_d, tk_f, tkt)], axis=1)
    d_h1, g_ffnw = _rms_bwd(h1, ffn_norm_w, d_u2, d_h2, "rms2_bwd")
    d_merged = _mm(d_h1, w_o, "nt", F32, "d_merged", tm, tn_d, d)
    g_wo = _mm(merged, d_h1, "tn", F32, "g_w_out", tn_d, tn_d, tkt)
    d_bs, d_br, d_gsr = _gate_bwd(d_merged, bs, br, proj, col["gs"], col["gr"], "gate_bwd")
    tk_i = _pick(di, (1024, 512, 256))
    d_yssd = _mm(d_bs, w_bs, "nt", F32, "d_y_ssd", tm, tk_i, d)
    g_wbs = _mm(yssd, d_bs, "tn", F32, "g_w_branch_ssd", tk_i, tn_d, tkt)
    d_yret = _mm(d_br, w_br, "nt", F32, "d_y_ret", tm, tk_i, d)
    g_wbr = _mm(yret, d_br, "tn", F32, "g_w_branch_ret", tk_i, tn_d, tkt)

    early_names = ["w_branch_ssd", "w_branch_ret", "w_out", "w_up", "w_down"]
    early = [g_wbs.reshape(4, di // 4, d), g_wbr.reshape(4, di // 4, d), g_wo.reshape(4, d // 4, d),
             jnp.moveaxis(g_wu.reshape(d, 4, 2 * f // 4), 1, 0), g_wd.reshape(4, f // 4, d)]
    res = _ret_bwd(d_yret, proj, col["v"], col["g"], o_ret, qr, kr, st_ret, cos, sin, d, "ret_bwd", comm=_scatter_d2d(early))
    (dqk, dvg), early_sib = res[:2], res[2:]
    early_pair = [_pair_sum(g_, s_, core, "pair_" + nm) for g_, s_, nm in zip(early, early_sib, early_names)]
    res = _ssd_bwd(d_yssd, xa, proj, col["dt"], col["z"], ypre, st_ssd, dtb, alog, dvec, ssd_norm_w, di, "ssd_bwd",
                   comm=_scatter_ici(early_pair))
    (d_xa, dz, ddt, g_dtb, g_alog, g_dvec, g_snw), early_recv = res[:7], res[7:]
    early_mine = [_sum4(p, "sum4_" + nm) for p, nm in zip(early_recv, early_names)]
    res = _ssd_conv_bwd(proj, col["xbc"], cw, scw, scb, d_xa, "ssd_conv_bwd", comm=_sibling_swap(early_mine))
    (d_xbc, g_scw, g_scb), early_other = res[:3], res[3:]
    d_pieces = [("z", dz), ("v", dvg), ("xbc", d_xbc), ("q", dqk), ("gs", d_gsr), ("dt", ddt)]

    g_piece = {nm: _mm(u1, a, "tn", F32, "g_w_in_" + nm, tn_d, _pick(a.shape[1], (1024, 768, 512, 256, 128)), tkt)
               for nm, a in d_pieces}
    g_cols = dict(z=g_piece["z"], v=g_piece["v"][:, :di], g=g_piece["v"][:, di:], xbc=g_piece["xbc"],
                  q=g_piece["q"][:, :d], k=g_piece["q"][:, d:], gs=g_piece["gs"][:, :d], gr=g_piece["gs"][:, d:],
                  dt=g_piece["dt"])
    g_in_ref = jnp.concatenate([g_cols[nm][:, :rw] for nm, rw in ref_order], axis=1)
    sc_in = jnp.moveaxis(g_in_ref.reshape(d, 4, in_dim // 4), 1, 0)
    in_sib = _run_comm(_scatter_d2d([sc_in]), "scatter_w_in_d2d")[0]
    in_pair = _pair_sum(sc_in, in_sib, core, "pair_w_in")
    d_u1, (in_recv,) = _mm_pieces_nt([(a, col[nm]) for nm, a in d_pieces], w_p, F32, "d_u1", tm, tn_d,
                                     comm=_scatter_ici([in_pair]))
    in_mine = _sum4(in_recv, "sum4_w_in")
    d_h0, g_mixw, in_other = _rms_bwd(h0, mix_norm_w, d_u1, d_h1, "rms1_bwd", comm=_sibling_swap([in_mine]))
    grad_x = d_h0[CHUNK:][None]
    g_meta = d_h0[FRONT:CHUNK]

    names = ["w_in"] + early_names
    mine_half = [in_mine] + early_mine
    other_half = [in_other] + list(early_other)
    big_w = [w_in, w_branch_ssd, w_branch_ret, w_out, w_up, w_down]
    big_m = [m_w_in, m_w_branch_ssd, m_w_branch_ret, m_w_out, m_w_up, m_w_down]
    big_v = [v_w_in, v_w_branch_ssd, v_w_branch_ret, v_w_out, v_w_up, v_w_down]
    big_out = {}
    for nm, w_, p_, s_, m_, v_ in zip(names, big_w, mine_half, other_half, big_m, big_v):
        res = _adam_big(w_[0], p_, s_, m_[0], v_[0], core, "adam_" + nm)
        big_out[nm] = [r[None] for r in res]

    kws, kwf = ssd_conv_w.shape[1], ffn_conv_w.shape[1]
    small_grads = [g_meta, g_mixw, g_scw[:kws], g_scb, g_dtb[:, :nh], g_alog[:, :nh], g_dvec[:, :nh], g_snw, g_ffnw,
                   g_fcw[:kwf], g_fcb, g_fin, loss8[0:1, 0:1]]
    sg_shapes = [g.shape for g in small_grads]
    gparts = _allgather8(_pack(small_grads), "gather_small_grads")

    def own_cols(a, width):
        return lax.dynamic_slice_in_dim(a, chip * width, width, axis=1)

    def widen(a, width_full):
        z = jnp.zeros(a.shape[:-1] + (width_full,), F32)
        return lax.dynamic_update_slice_in_dim(z, a, chip * a.shape[-1], axis=a.ndim - 1)

    def small_slab(meta_, mix_, scw_, scb_, dtb_, alog_, d_, snw_, ffnw_, fcw_, fcb_, fin_):
        return _pack([widen(meta_, d), mix_, widen(scw_[0], cw), scb_, dtb_, alog_, d_, snw_, ffnw_, widen(fcw_[0], 2 * f),
                      fcb_, fin_.reshape(1, d), jnp.zeros((1, 1), F32)])

    w_slab = small_slab(meta_tokens, mix_norm_w, ssd_conv_w, ssd_conv_b, ssd_dt_bias, ssd_A_log, ssd_D, ssd_norm_w,
                        ffn_norm_w, ffn_conv_w, ffn_conv_b, final_norm_w)
    m_slab = small_slab(m_meta_tokens, m_mix_norm_w, m_ssd_conv_w, m_ssd_conv_b, m_ssd_dt_bias, m_ssd_A_log, m_ssd_D,
                        m_ssd_norm_w, m_ffn_norm_w, m_ffn_conv_w, m_ffn_conv_b, m_final_norm_w)
    v_slab = small_slab(v_meta_tokens, v_mix_norm_w, v_ssd_conv_w, v_ssd_conv_b, v_ssd_dt_bias, v_ssd_A_log, v_ssd_D,
                        v_ssd_norm_w, v_ffn_norm_w, v_ffn_conv_w, v_ffn_conv_b, v_final_norm_w)
    small_res = [_unpack(s, sg_shapes) for s in _adam_small(w_slab, gparts, m_slab, v_slab, "adam_small")]
    loss = small_res[0][12].reshape(())

    def small_outputs(vals):
        meta_, mix_, scw_, scb_, dtb_, alog_, d_, snw_, ffnw_, fcw_, fcb_, fin_ = vals[:12]
        return {
            "meta_tokens": own_cols(meta_, d // 4), "mix_norm_w": mix_, "ssd_conv_w": own_cols(scw_, cw // 4)[None],
            "ssd_conv_b": scb_, "ssd_dt_bias": dtb_, "ssd_A_log": alog_, "ssd_D": d_, "ssd_norm_w": snw_,
            "ffn_norm_w": ffnw_, "ffn_conv_w": own_cols(fcw_, 2 * f // 4)[None], "ffn_conv_b": fcb_,
            "final_norm_w": fin_.reshape(d),
        }

    weights = ["meta_tokens", "mix_norm_w", "w_in", "ssd_conv_w", "ssd_conv_b", "ssd_dt_bias", "ssd_A_log", "ssd_D",
               "ssd_norm_w", "w_branch_ssd", "w_branch_ret", "w_out", "ffn_norm_w", "w_up", "ffn_conv_w", "ffn_conv_b",
               "w_down", "final_norm_w"]
    outs = [loss, grad_x]
    for kind in range(4):
        so = small_outputs(small_res[kind])
        for nm in weights:
            outs.append(big_out[nm][kind] if nm in big_out else so[nm])
    return tuple(outs)
```

```python
import functools
import math

import jax
import jax.numpy as jnp
import numpy as np
from jax import lax
from jax.experimental import pallas as pl
from jax.experimental.pallas import tpu as pltpu

F32 = jnp.float32
BF16 = jnp.bfloat16
MXU_DTYPE = BF16
WIRE_DTYPE = BF16

N_META = 16
CHUNK = 128
FRONT = CHUNK - N_META
EPS = 1e-6
SSD_HEAD_DIM = 64
SSD_GROUPS = 4
SSD_STATE = 128
SSD_CONV = 4
RET_HEADS = 4
ROPE_BASE = 10000.0
FFN_CONV = 3
LANES = 128
SUBLANES = 8
VMEM_LIMIT = 56 * 1024 * 1024

ADAM_LR = 0.001
ADAM_B1 = 0.9
ADAM_B2 = 0.999
ADAM_EPS = 1e-08
ADAM_WD = 0.01
ADAM_STEP = 10
MESH = pl.DeviceIdType.MESH


def _params(sem=None, vmem=VMEM_LIMIT):
    return pltpu.CompilerParams(dimension_semantics=sem, vmem_limit_bytes=vmem)


def _pick(n, cands):
    for c in cands:
        if n % c == 0:
            return c
    return n


def _silu(x):
    return x * jax.nn.sigmoid(x)


def _dsilu(x):
    s = jax.nn.sigmoid(x)
    return s * (1.0 + x * (1.0 - s))


def _dot(a, b, dims=(((1,), (0,)), ((), ()))):
    return lax.dot_general(a.astype(MXU_DTYPE), b.astype(MXU_DTYPE), dims, preferred_element_type=F32)


def _dot_nt(a, b):
    return _dot(a, b, (((1,), (1,)), ((), ())))


def _dot_tn(a, b):
    return _dot(a, b, (((0,), (0,)), ((), ())))


def _dot01(a, b, split, npass, dims=(((1,), (0,)), ((), ()))):
    rest = (a if split == "a" else b).astype(F32)
    fixed = (b if split == "a" else a).astype(BF16)
    acc = None
    for p in range(npass):
        piece = rest.astype(BF16)
        ops = (piece, fixed) if split == "a" else (fixed, piece)
        term = lax.dot_general(ops[0], ops[1], dims, preferred_element_type=F32)
        acc = term if acc is None else acc + term
        if p + 1 < npass:
            rest = rest - piece.astype(F32)
    return acc


_NT = (((1,), (1,)), ((), ()))


def _iota(shape, dim):
    return lax.broadcasted_iota(jnp.int32, shape, dim)


def _shift_down(cur, prev8, k):
    if k == 0:
        return cur
    rolled = pltpu.roll(cur, k, 0)
    i8 = _iota((SUBLANES, cur.shape[1]), 0)
    head = jnp.where(i8 < k, pltpu.roll(prev8, k, 0), rolled[0:SUBLANES])
    return jnp.concatenate([head, rolled[SUBLANES:]], axis=0)


def _shift_up(cur, next8, k):
    if k == 0:
        return cur
    n = cur.shape[0]
    rolled = pltpu.roll(cur, n - k, 0)
    i8 = _iota((SUBLANES, cur.shape[1]), 0)
    tail = jnp.where(i8 >= SUBLANES - k, pltpu.roll(next8, SUBLANES - k, 0), rolled[n - SUBLANES:])
    return jnp.concatenate([rolled[:n - SUBLANES], tail], axis=0)


class _Comm:
    def __init__(self, ins, outs, nsem, make, in_place=False):
        self.ins, self.outs, self.nsem, self.make = list(ins), list(outs), nsem, make
        self.in_place = in_place


def _place():
    x, y, c = lax.axis_index("x"), lax.axis_index("y"), lax.axis_index("c")
    return x, y, c, 2 * x + y, [(1 - x, y), (x, 1 - y), (1 - x, 1 - y)]


def _call(body, name, grid, in_specs, out_specs, out_shape, scratch, sem, args, comm=None):
    if comm is None:
        return pl.pallas_call(body, name=name, grid=grid, in_specs=in_specs, out_specs=out_specs, out_shape=out_shape,
                              scratch_shapes=scratch, compiler_params=_params(sem))(*args)
    n_in, n_out, n_scr = len(in_specs), len(out_specs), len(scratch)
    ci, co = len(comm.ins), len(comm.outs)

    def wrapped(*refs):
        ins, refs = refs[:n_in], refs[n_in:]
        cins, refs = refs[:ci], refs[ci:]
        outs, refs = refs[:n_out], refs[n_out:]
        couts, refs = refs[:co], refs[co:]
        scr, sems = refs[:n_scr], refs[n_scr:]
        first = functools.reduce(jnp.logical_and, [pl.program_id(a) == 0 for a in range(len(grid))])
        last = functools.reduce(jnp.logical_and, [pl.program_id(a) == grid[a] - 1 for a in range(len(grid))])

        @pl.when(first)
        def _():
            for cp in comm.make(cins, couts, *sems):
                cp.start()

        body(*ins, *outs, *scr)

        @pl.when(last)
        def _():
            for cp in comm.make(cins, couts, *sems):
                cp.wait()

    anyspec = pl.BlockSpec(memory_space=pl.ANY)
    dma = pltpu.SemaphoreType.DMA((comm.nsem,))
    aliases = {n_in + i: n_out + i for i in range(ci)} if comm.in_place else {}
    return pl.pallas_call(
        wrapped, name=name, grid=grid, in_specs=list(in_specs) + [anyspec] * ci,
        out_specs=list(out_specs) + [anyspec] * co, out_shape=list(out_shape) + comm.outs,
        scratch_shapes=list(scratch) + [dma, dma, dma], input_output_aliases=aliases,
        compiler_params=_params(("arbitrary",) * len(grid)))(*args, *comm.ins)


def _run_comm(comm, name):
    ci, co = len(comm.ins), len(comm.outs)

    def body(*refs):
        cins, couts, sems = refs[:ci], refs[ci:ci + co], refs[ci + co:]
        for cp in comm.make(cins, couts, *sems):
            cp.start()
        for cp in comm.make(cins, couts, *sems):
            cp.wait()

    anyspec = pl.BlockSpec(memory_space=pl.ANY)
    dma = pltpu.SemaphoreType.DMA((comm.nsem,))
    aliases = {i: i for i in range(ci)} if comm.in_place else {}
    return pl.pallas_call(body, name=name, in_specs=[anyspec] * ci, out_specs=[anyspec] * co, out_shape=comm.outs,
                          scratch_shapes=[dma, dma, dma], input_output_aliases=aliases)(*comm.ins)


def _half_rows(c, rows):
    h = rows // 2
    return pl.ds(pl.multiple_of(c * h, 16), h)


def _gather_ici(arrays):
    for a in arrays:
        assert a.shape[0] % 32 == 0, a.shape

    def make(ins, outs, send, recv, loc):
        x, y, c, mine, peers = _place()
        cps = []
        for i, a in enumerate(arrays):
            half = _half_rows(c, a.shape[0])
            for k, (px, py) in enumerate(peers):
                cps.append(pltpu.make_async_remote_copy(
                    src_ref=ins[i].at[half], dst_ref=outs[i].at[mine, half], send_sem=send.at[3 * i + k],
                    recv_sem=recv.at[3 * i + k], device_id=(px, py, c), device_id_type=MESH))
        return cps

    outs = [jax.ShapeDtypeStruct((4,) + a.shape, a.dtype) for a in arrays]
    return _Comm(arrays, outs, 3 * len(arrays), make)


def _gather_d2d(bufs):
    def make(ins, outs, send, recv, loc):
        x, y, c, mine, peers = _place()
        cps = []
        for i, a in enumerate(bufs):
            half = _half_rows(c, a.shape[1])
            for k, (px, py) in enumerate(peers):
                mine_half = outs[i].at[2 * px + py, half]
                cps.append(pltpu.make_async_remote_copy(
                    src_ref=mine_half, dst_ref=mine_half, send_sem=send.at[3 * i + k], recv_sem=recv.at[3 * i + k],
                    device_id=(x, y, 1 - c), device_id_type=MESH))
        return cps

    outs = [jax.ShapeDtypeStruct(a.shape, a.dtype) for a in bufs]
    return _Comm(bufs, outs, 3 * len(bufs), make, in_place=True)


def _with_own(gathered, own, chip):
    return lax.dynamic_update_index_in_dim(gathered, own, chip, 0)


def _scatter_d2d(grads):
    for a in grads:
        assert a.shape[1] % 32 == 0, a.shape

    def make(ins, outs, send, recv, loc):
        x, y, c, mine, peers = _place()
        cps = []
        for i, a in enumerate(grads):
            other = _half_rows(1 - c, a.shape[1])
            cps.append(pltpu.make_async_remote_copy(
                src_ref=ins[i].at[:, other], dst_ref=outs[i], send_sem=send.at[i], recv_sem=recv.at[i],
                device_id=(x, y, 1 - c), device_id_type=MESH))
        return cps

    outs = [jax.ShapeDtypeStruct((4, a.shape[1] // 2, a.shape[2]), a.dtype) for a in grads]
    return _Comm(grads, outs, len(grads), make)


def _scatter_ici(parts):
    def make(ins, outs, send, recv, loc):
        x, y, c, mine, peers = _place()
        cps = []
        for i in range(len(parts)):
            cps.append(pltpu.make_async_copy(ins[i].at[mine], outs[i].at[mine], loc.at[i]))
            for k, (px, py) in enumerate(peers):
                cps.append(pltpu.make_async_remote_copy(
                    src_ref=ins[i].at[2 * px + py], dst_ref=outs[i].at[mine], send_sem=send.at[3 * i + k],
                    recv_sem=recv.at[3 * i + k], device_id=(px, py, c), device_id_type=MESH))
        return cps

    outs = [jax.ShapeDtypeStruct(a.shape, a.dtype) for a in parts]
    return _Comm(parts, outs, 3 * len(parts), make)


def _sibling_swap(arrays):
    def make(ins, outs, send, recv, loc):
        x, y, c, mine, peers = _place()
        return [pltpu.make_async_remote_copy(src_ref=ins[i], dst_ref=outs[i], send_sem=send.at[i], recv_sem=recv.at[i],
                                             device_id=(x, y, 1 - c), device_id_type=MESH) for i in range(len(arrays))]

    outs = [jax.ShapeDtypeStruct(a.shape, a.dtype) for a in arrays]
    return _Comm(arrays, outs, len(arrays), make)


def _mm(a, b, mode, out_dtype, name, tm, tn, tk, res=None, comm=None):
    if mode == "nn":
        (m, kd), n = a.shape, b.shape[1]
        a_spec = pl.BlockSpec((tm, tk), lambda i, j, k: (i, k))
        b_spec = pl.BlockSpec((tk, tn), lambda i, j, k: (k, j))
        dims = (((1,), (0,)), ((), ()))
    elif mode == "nt":
        (m, kd), n = a.shape, b.shape[0]
        a_spec = pl.BlockSpec((tm, tk), lambda i, j, k: (i, k))
        b_spec = pl.BlockSpec((tn, tk), lambda i, j, k: (j, k))
        dims = (((1,), (1,)), ((), ()))
    else:
        (kd, m), n = a.shape, b.shape[1]
        a_spec = pl.BlockSpec((tk, tm), lambda i, j, k: (k, i))
        b_spec = pl.BlockSpec((tk, tn), lambda i, j, k: (k, j))
        dims = (((0,), (0,)), ((), ()))
    assert m % tm == 0 and n % tn == 0 and kd % tk == 0, (name, m, n, kd, tm, tn, tk)
    nk = kd // tk
    has_res = res is not None

    def body(*refs):
        a_ref, b_ref = refs[:2]
        r_ref = refs[2] if has_res else None
        o_ref = refs[3 if has_res else 2]

        def finish(r):
            if has_res:
                r = r + r_ref[...].astype(F32)
            o_ref[...] = r.astype(out_dtype)

        if nk == 1:
            finish(_dot(a_ref[...], b_ref[...], dims))
            return
        acc = refs[-1]
        k = pl.program_id(2)

        @pl.when(k == 0)
        def _():
            acc[...] = _dot(a_ref[...], b_ref[...], dims)

        @pl.when((k > 0) & (k < nk - 1))
        def _():
            acc[...] += _dot(a_ref[...], b_ref[...], dims)

        @pl.when(k == nk - 1)
        def _():
            finish(acc[...] + _dot(a_ref[...], b_ref[...], dims))

    in_specs = [a_spec, b_spec]
    args = [a, b]
    if has_res:
        in_specs.append(pl.BlockSpec((tm, tn), lambda i, j, k: (i, j)))
        args.append(res)
    res = _call(body, name, (m // tm, n // tn, nk), in_specs, [pl.BlockSpec((tm, tn), lambda i, j, k: (i, j))],
                [jax.ShapeDtypeStruct((m, n), out_dtype)], [] if nk == 1 else [pltpu.VMEM((tm, tn), F32)],
                ("parallel", "parallel", "arbitrary"), args, comm)
    return res[0] if comm is None else (res[0], res[1:])


def _mm_pieces_nt(pieces, b, out_dtype, name, tm, tn, comm=None):
    m, n = pieces[0][0].shape[0], b.shape[0]
    tks = [_pick(math.gcd(a.shape[1], c0) if c0 else a.shape[1], (1024, 512, 256, 128)) for a, c0 in pieces]
    nks = [a.shape[1] // tk for (a, _), tk in zip(pieces, tks)]
    starts = [sum(nks[:p]) for p in range(len(pieces))]
    ktot = sum(nks)
    npc = len(pieces)

    def body(*refs):
        a_refs, b_refs, o_ref, acc = refs[:npc], refs[npc:2 * npc], refs[2 * npc], refs[2 * npc + 1]
        k = pl.program_id(2)

        @pl.when(k == 0)
        def _():
            acc[...] = jnp.zeros_like(acc)

        for p in range(npc):
            @pl.when((k >= starts[p]) & (k < starts[p] + nks[p]))
            def _(p=p):
                acc[...] += _dot_nt(a_refs[p][...], b_refs[p][...])

        @pl.when(k == ktot - 1)
        def _():
            o_ref[...] = acc[...].astype(out_dtype)

    def a_spec(p):
        return pl.BlockSpec((tm, tks[p]), lambda i, j, k: (i, jnp.clip(k - starts[p], 0, nks[p] - 1)))

    def b_spec(p):
        c0 = pieces[p][1] // tks[p]
        return pl.BlockSpec((tn, tks[p]), lambda i, j, k: (j, c0 + jnp.clip(k - starts[p], 0, nks[p] - 1)))

    res = _call(body, name, (m // tm, n // tn, ktot), [a_spec(p) for p in range(npc)] + [b_spec(p) for p in range(npc)],
                [pl.BlockSpec((tm, tn), lambda i, j, k: (i, j))], [jax.ShapeDtypeStruct((m, n), out_dtype)],
                [pltpu.VMEM((tm, tn), F32)], ("parallel", "parallel", "arbitrary"),
                [a for a, _ in pieces] + [b] * npc, comm)
    return res[0] if comm is None else (res[0], res[1:])


def _rms_fwd(h, w, name):
    t, d = h.shape
    tr = _pick(t, (640, 512, 384, 256, 128))

    def body(h_ref, w_ref, u_ref):
        x = h_ref[...]
        r = lax.rsqrt(jnp.mean(x * x, axis=1, keepdims=True) + EPS)
        u_ref[...] = (x * r * w_ref[...]).astype(MXU_DTYPE)

    return pl.pallas_call(
        body, name=name, grid=(t // tr,),
        in_specs=[pl.BlockSpec((tr, d), lambda i: (i, 0)), pl.BlockSpec((1, d), lambda i: (0, 0))],
        out_specs=pl.BlockSpec((tr, d), lambda i: (i, 0)),
        out_shape=jax.ShapeDtypeStruct((t, d), MXU_DTYPE),
        compiler_params=_params(("parallel",)),
    )(h, w)


def _rms_bwd(h, w, du, res, name, comm=None):
    t, d = h.shape
    tr = _pick(t, (640, 512, 384, 256, 128))

    def body(h_ref, w_ref, du_ref, res_ref, dh_ref, gw_ref):
        @pl.when(pl.program_id(0) == 0)
        def _():
            gw_ref[...] = jnp.zeros_like(gw_ref)

        x = h_ref[...]
        r = lax.rsqrt(jnp.mean(x * x, axis=1, keepdims=True) + EPS)
        xhat = x * r
        dy = du_ref[...].astype(F32)
        dxh = dy * w_ref[...]
        dh = r * (dxh - xhat * jnp.mean(dxh * xhat, axis=1, keepdims=True))
        dh_ref[...] = dh + res_ref[...]
        gw_ref[...] += jnp.sum(dy * xhat, axis=0, keepdims=True)

    row = pl.BlockSpec((tr, d), lambda i: (i, 0))
    vec = pl.BlockSpec((1, d), lambda i: (0, 0))
    return _call(body, name, (t // tr,), [row, vec, row, row], [row, vec],
                 [jax.ShapeDtypeStruct((t, d), F32), jax.ShapeDtypeStruct((1, d), F32)], [], ("arbitrary",),
                 (h, w, du, res), comm)


def _loss_bwd(h2, w, target, name):
    t, d = h2.shape
    nc = t // CHUNK

    def body(h_ref, w_ref, tg_ref, loss_ref, dh_ref, gw_ref):
        i = pl.program_id(0)

        @pl.when(i == 0)
        def _():
            gw_ref[...] = jnp.zeros_like(gw_ref)
            loss_ref[...] = jnp.zeros_like(loss_ref)
            dh_ref[...] = jnp.zeros_like(dh_ref)

        @pl.when(i > 0)
        def _():
            x = h_ref[...]
            r = lax.rsqrt(jnp.mean(x * x, axis=1, keepdims=True) + EPS)
            xhat = x * r
            diff = xhat * w_ref[...] - tg_ref[...]
            loss_ref[...] += 0.5 * jnp.sum(jnp.sum(diff * diff, axis=1, keepdims=True), axis=0, keepdims=True) / d
            dy = diff / d
            dxh = dy * w_ref[...]
            dh_ref[...] = r * (dxh - xhat * jnp.mean(dxh * xhat, axis=1, keepdims=True))
            gw_ref[...] += jnp.sum(dy * xhat, axis=0, keepdims=True)

    row = pl.BlockSpec((CHUNK, d), lambda i: (i, 0))
    vec = pl.BlockSpec((1, d), lambda i: (0, 0))
    return pl.pallas_call(
        body, name=name, grid=(nc,),
        in_specs=[row, vec, pl.BlockSpec((CHUNK, d), lambda i: (jnp.maximum(i - 1, 0), 0))],
        out_specs=[pl.BlockSpec((SUBLANES, LANES), lambda i: (0, 0)), row, vec],
        out_shape=[jax.ShapeDtypeStruct((SUBLANES, LANES), F32), jax.ShapeDtypeStruct((t, d), F32),
                   jax.ShapeDtypeStruct((1, d), F32)],
        compiler_params=_params(("arbitrary",)),
    )(h2, w, target)


def _conv_tile(cur, prev8, w_ref, b_ref, kw):
    y = b_ref[...] + cur * w_ref[kw - 1:kw, :]
    for k in range(kw - 1):
        y = y + _shift_down(cur, prev8, kw - 1 - k) * w_ref[k:k + 1, :]
    return y


_SUB = 16


def _sub_rows(s):
    return pl.ds(0 if isinstance(s, int) else pl.multiple_of(s * _SUB, _SUB), _SUB)


def _window(x_ref, prev8, s):
    if isinstance(s, int):
        return jnp.concatenate([prev8, x_ref[0:_SUB, :]], axis=0)
    return x_ref[pl.ds(pl.multiple_of(s * _SUB - SUBLANES, SUBLANES), _SUB + SUBLANES), :]


def _conv_step(win, w, b, kw):
    taps = [win[SUBLANES:] if k == kw - 1 else pltpu.roll(win, kw - 1 - k, 0)[SUBLANES:] for k in range(kw)]
    y = b + taps[kw - 1] * w[kw - 1:kw, :]
    for k in range(kw - 1):
        y = y + taps[k] * w[k:k + 1, :]
    return y, taps


def _conv_dx_step(dpre, next8, w, kw):
    n = _SUB + SUBLANES
    win = jnp.concatenate([dpre, next8], axis=0)
    acc = dpre * w[kw - 1:kw, :]
    for k in range(kw - 1):
        acc = acc + pltpu.roll(win, n - (kw - 1 - k), 0)[0:_SUB] * w[k:k + 1, :]
    return acc


def _fold8(v):
    return functools.reduce(jnp.add, [v[r:r + SUBLANES] for r in range(0, _SUB, SUBLANES)])


def _row_tile(t):
    return _pick(t, (640, 512, 384, 256, 128))


def _ssd_conv_fwd(proj, col0, width, w, b, name):
    t = proj.shape[0]
    kw = w.shape[0]
    tr, tc = _row_tile(t), _pick(width, (512, 256, 128))
    c0, rb = col0 // tc, tr // SUBLANES
    assert col0 % tc == 0

    def body(x_ref, p_ref, w_ref, b_ref, o_ref):
        i = pl.program_id(1)
        prev8 = jnp.where(i > 0, p_ref[...], 0.0)
        pre = _conv_tile(x_ref[...], prev8, w_ref, b_ref, kw)
        rows = _iota((tr, 1), 0) + i * tr
        o_ref[...] = jnp.where(rows >= FRONT, _silu(pre), 0.0)

    return pl.pallas_call(
        body, name=name, grid=(width // tc, t // tr),
        in_specs=[pl.BlockSpec((tr, tc), lambda j, i: (i, c0 + j)),
                  pl.BlockSpec((SUBLANES, tc), lambda j, i: (jnp.maximum(i * rb - 1, 0), c0 + j)),
                  pl.BlockSpec((kw, tc), lambda j, i: (0, j)),
                  pl.BlockSpec((1, tc), lambda j, i: (0, j))],
        out_specs=pl.BlockSpec((tr, tc), lambda j, i: (i, j)),
        out_shape=jax.ShapeDtypeStruct((t, width), F32),
        compiler_params=_params(("parallel", "parallel")),
    )(proj, proj, w, b)


def _ssd_conv_bwd(proj, col0, width, w, b, dact, name, comm=None):
    t = proj.shape[0]
    kw = w.shape[0]
    tr, tc = _row_tile(t), _pick(width, (512, 256, 128))
    c0, rb, nrow = col0 // tc, tr // SUBLANES, t // tr

    def body(x_ref, p_ref, w_ref, b_ref, d_ref, o_ref, gw_ref, gb_ref, carry):
        i = pl.program_id(1)
        ti = nrow - 1 - i

        @pl.when(i == 0)
        def _():
            gw_ref[...] = jnp.zeros_like(gw_ref)
            gb_ref[...] = jnp.zeros_like(gb_ref)
            carry[...] = jnp.zeros_like(carry)

        w, b = w_ref[...], b_ref[...]
        prev8 = jnp.where(ti > 0, p_ref[...], 0.0)
        nsub = tr // _SUB

        def step(s, state):
            next8, gb8, gw8 = state
            pre, taps = _conv_step(_window(x_ref, prev8, s), w, b, kw)
            valid = _iota((_SUB, 1), 0) + (ti * tr + s * _SUB) >= FRONT
            dpre = jnp.where(valid, d_ref[_sub_rows(s), :] * _dsilu(pre), 0.0)
            o_ref[_sub_rows(s), :] = jnp.where(valid, _conv_dx_step(dpre, next8, w, kw), 0.0).astype(MXU_DTYPE)
            return (dpre[0:SUBLANES], gb8 + _fold8(dpre), tuple(g + _fold8(dpre * tp) for g, tp in zip(gw8, taps)))

        zero8 = jnp.zeros((SUBLANES, tc), F32)
        state = lax.fori_loop(0, nsub - 1, lambda n, st: step(nsub - 1 - n, st), (carry[...], zero8, (zero8,) * kw))
        next8, gb8, gw8 = step(0, state)
        carry[...] = next8
        gb_ref[...] += jnp.sum(gb8, axis=0, keepdims=True)
        for k in range(kw):
            gw_ref[k:k + 1, :] += jnp.sum(gw8[k], axis=0, keepdims=True)

    return _call(
        body, name, (width // tc, nrow),
        [pl.BlockSpec((tr, tc), lambda j, i: (nrow - 1 - i, c0 + j)),
         pl.BlockSpec((SUBLANES, tc), lambda j, i: (jnp.maximum((nrow - 1 - i) * rb - 1, 0), c0 + j)),
         pl.BlockSpec((kw, tc), lambda j, i: (0, j)),
         pl.BlockSpec((1, tc), lambda j, i: (0, j)),
         pl.BlockSpec((tr, tc), lambda j, i: (nrow - 1 - i, j))],
        [pl.BlockSpec((tr, tc), lambda j, i: (nrow - 1 - i, j)),
         pl.BlockSpec((SUBLANES, tc), lambda j, i: (0, j)),
         pl.BlockSpec((1, tc), lambda j, i: (0, j))],
        [jax.ShapeDtypeStruct((t, width), MXU_DTYPE), jax.ShapeDtypeStruct((SUBLANES, width), F32),
         jax.ShapeDtypeStruct((1, width), F32)],
        [pltpu.VMEM((SUBLANES, tc), F32)], ("parallel", "arbitrary"), (proj, proj, w, b, dact), comm)


def _ffn_conv_fwd(up, w, b, name):
    t, f2 = up.shape
    f = f2 // 2
    kw = w.shape[0]
    tr, tc = _row_tile(t), _pick(f, (256, 128))
    nf, rb = f // tc, tr // SUBLANES

    def body(xg, pg, xv, pv, wg, wv, bg, bv, o_ref):
        i = pl.program_id(1)
        ag = _conv_tile(xg[...], jnp.where(i > 0, pg[...], 0.0), wg, bg, kw)
        av = _conv_tile(xv[...], jnp.where(i > 0, pv[...], 0.0), wv, bv, kw)
        o_ref[...] = (_silu(ag) * av).astype(MXU_DTYPE)

    def cur(off):
        return pl.BlockSpec((tr, tc), lambda j, i: (i, j + off))

    def prev(off):
        return pl.BlockSpec((SUBLANES, tc), lambda j, i: (jnp.maximum(i * rb - 1, 0), j + off))

    def par(rows, off):
        return pl.BlockSpec((rows, tc), lambda j, i: (0, j + off))

    return pl.pallas_call(
        body, name=name, grid=(nf, t // tr),
        in_specs=[cur(0), prev(0), cur(nf), prev(nf), par(kw, 0), par(kw, nf), par(1, 0), par(1, nf)],
        out_specs=pl.BlockSpec((tr, tc), lambda j, i: (i, j)),
        out_shape=jax.ShapeDtypeStruct((t, f), MXU_DTYPE),
        compiler_params=_params(("parallel", "parallel")),
    )(up, up, up, up, w, w, b, b)


def _ffn_conv_bwd(up, w, b, dact, name):
    t, f2 = up.shape
    f = f2 // 2
    kw = w.shape[0]
    tr, tc = _row_tile(t), _pick(f, (256, 128))
    nf, rb, nrow = f // tc, tr // SUBLANES, t // tr

    def body(xg, pg, xv, pv, wg_ref, wv_ref, bg_ref, bv_ref, d_ref, og_ref, ov_ref, gwg_ref, gwv_ref, gbg_ref, gbv_ref,
             cg, cv):
        i = pl.program_id(1)
        ti = nrow - 1 - i

        @pl.when(i == 0)
        def _():
            for r in (gwg_ref, gwv_ref, gbg_ref, gbv_ref, cg, cv):
                r[...] = jnp.zeros_like(r)

        wg, wv, bg, bv = wg_ref[...], wv_ref[...], bg_ref[...], bv_ref[...]
        p8g, p8v = jnp.where(ti > 0, pg[...], 0.0), jnp.where(ti > 0, pv[...], 0.0)
        nsub = tr // _SUB

        def step(s, state):
            ng, nv, gbg8, gbv8, gwg8, gwv8 = state
            ag, tg = _conv_step(_window(xg, p8g, s), wg, bg, kw)
            av, tv = _conv_step(_window(xv, p8v, s), wv, bv, kw)
            d = d_ref[_sub_rows(s), :]
            sg = jax.nn.sigmoid(ag)
            dag = d * av * (sg * (1.0 + ag * (1.0 - sg)))
            dav = d * (ag * sg)
            valid = _iota((_SUB, 1), 0) + (ti * tr + s * _SUB) >= FRONT
            og_ref[_sub_rows(s), :] = jnp.where(valid, _conv_dx_step(dag, ng, wg, kw), 0.0).astype(MXU_DTYPE)
            ov_ref[_sub_rows(s), :] = jnp.where(valid, _conv_dx_step(dav, nv, wv, kw), 0.0).astype(MXU_DTYPE)
            return (dag[0:SUBLANES], dav[0:SUBLANES], gbg8 + _fold8(dag), gbv8 + _fold8(dav),
                    tuple(g + _fold8(dag * tp) for g, tp in zip(gwg8, tg)),
                    tuple(g + _fold8(dav * tp) for g, tp in zip(gwv8, tv)))

        zero8 = jnp.zeros((SUBLANES, tc), F32)
        state = lax.fori_loop(0, nsub - 1, lambda n, st: step(nsub - 1 - n, st),
                              (cg[...], cv[...], zero8, zero8, (zero8,) * kw, (zero8,) * kw))
        ng, nv, gbg8, gbv8, gwg8, gwv8 = step(0, state)
        cg[...] = ng
        cv[...] = nv
        gbg_ref[...] += jnp.sum(gbg8, axis=0, keepdims=True)
        gbv_ref[...] += jnp.sum(gbv8, axis=0, keepdims=True)
        for k in range(kw):
            gwg_ref[k:k + 1, :] += jnp.sum(gwg8[k], axis=0, keepdims=True)
            gwv_ref[k:k + 1, :] += jnp.sum(gwv8[k], axis=0, keepdims=True)

    def cur(off):
        return pl.BlockSpec((tr, tc), lambda j, i: (nrow - 1 - i, j + off))

    def prev(off):
        return pl.BlockSpec((SUBLANES, tc), lambda j, i: (jnp.maximum((nrow - 1 - i) * rb - 1, 0), j + off))

    def par(rows, off):
        return pl.BlockSpec((rows, tc), lambda j, i: (0, j + off))

    acc8 = pl.BlockSpec((SUBLANES, tc), lambda j, i: (0, j))
    acc1 = pl.BlockSpec((1, tc), lambda j, i: (0, j))
    return pl.pallas_call(
        body, name=name, grid=(nf, nrow),
        in_specs=[cur(0), prev(0), cur(nf), prev(nf), par(kw, 0), par(kw, nf), par(1, 0), par(1, nf), cur(0)],
        out_specs=[cur(0), cur(0), acc8, acc8, acc1, acc1],
        out_shape=[jax.ShapeDtypeStruct((t, f), MXU_DTYPE), jax.ShapeDtypeStruct((t, f), MXU_DTYPE),
                   jax.ShapeDtypeStruct((SUBLANES, f), F32), jax.ShapeDtypeStruct((SUBLANES, f), F32),
                   jax.ShapeDtypeStruct((1, f), F32), jax.ShapeDtypeStruct((1, f), F32)],
        scratch_shapes=[pltpu.VMEM((SUBLANES, tc), F32), pltpu.VMEM((SUBLANES, tc), F32)],
        compiler_params=_params(("parallel", "arbitrary")),
    )(up, up, up, up, w, w, b, b, dact)


def _gate_fwd(bs, br, proj, c_gs, c_gr, name):
    t, d = bs.shape
    tr = _row_tile(t)

    def body(bs_ref, br_ref, gs_ref, gr_ref, o_ref):
        o_ref[...] = (jax.nn.sigmoid(gs_ref[...]) * bs_ref[...] + jax.nn.sigmoid(gr_ref[...]) * br_ref[...]).astype(MXU_DTYPE)

    row = pl.BlockSpec((tr, d), lambda i: (i, 0))
    return pl.pallas_call(
        body, name=name, grid=(t // tr,),
        in_specs=[row, row, pl.BlockSpec((tr, d), lambda i: (i, c_gs // d)), pl.BlockSpec((tr, d), lambda i: (i, c_gr // d))],
        out_specs=row, out_shape=jax.ShapeDtypeStruct((t, d), MXU_DTYPE),
        compiler_params=_params(("parallel",)),
    )(bs, br, proj, proj)


def _gate_bwd(dm, bs, br, proj, c_gs, c_gr, name):
    t, d = bs.shape
    tr = _row_tile(t)

    def body(dm_ref, bs_ref, br_ref, gs_ref, gr_ref, dbs_ref, dbr_ref, dgg_ref):
        g = dm_ref[...]
        ss, sr = jax.nn.sigmoid(gs_ref[...]), jax.nn.sigmoid(gr_ref[...])
        dbs_ref[...] = (g * ss).astype(MXU_DTYPE)
        dbr_ref[...] = (g * sr).astype(MXU_DTYPE)
        dgg_ref[:, :d] = (g * bs_ref[...] * ss * (1.0 - ss)).astype(MXU_DTYPE)
        dgg_ref[:, d:] = (g * br_ref[...] * sr * (1.0 - sr)).astype(MXU_DTYPE)

    row = pl.BlockSpec((tr, d), lambda i: (i, 0))
    out = jax.ShapeDtypeStruct((t, d), MXU_DTYPE)
    return pl.pallas_call(
        body, name=name, grid=(t // tr,),
        in_specs=[row, row, row, pl.BlockSpec((tr, d), lambda i: (i, c_gs // d)), pl.BlockSpec((tr, d), lambda i: (i, c_gr // d))],
        out_specs=[row, row, pl.BlockSpec((tr, 2 * d), lambda i: (i, 0))],
        out_shape=[out, out, jax.ShapeDtypeStruct((t, 2 * d), MXU_DTYPE)],
        compiler_params=_params(("parallel",)),
    )(dm, bs, br, proj, proj)


def _ret_consts(h):
    lg = math.log(1.0 - 2.0 ** (-5.0 - h))
    l = _iota((CHUNK, 1), 0).astype(F32)
    diff = l - _iota((1, CHUNK), 1).astype(F32)
    dm = jnp.exp(jnp.where(diff >= 0, diff * lg, -jnp.inf))
    dmt = jnp.exp(jnp.where(diff <= 0, -diff * lg, -jnp.inf))
    cs = jnp.exp((l + 1.0) * lg)
    kdec = jnp.exp((CHUNK - 1.0 - l) * lg)
    return dm, dmt, cs, kdec, math.exp(CHUNK * lg)


def _ret_fwd(proj, c_q, c_k, c_v, c_g, cos, sin, d, name, comm=None):
    t = proj.shape[0]
    nc = t // CHUNK
    hq, hv = d // RET_HEADS, 2 * d // RET_HEADS
    half = hq // 2
    scale = hq ** -0.5

    def body(q_ref, k_ref, v_ref, g_ref, cos_ref, sin_ref, o_ref, y_ref, qr_ref, kr_ref, st_ref, rs):
        @pl.when(pl.program_id(0) == 0)
        def _():
            rs[...] = jnp.zeros_like(rs)

        co, si = cos_ref[...], sin_ref[...]
        for h in range(RET_HEADS):
            dm, _, cs, kdec, gam = _ret_consts(h)
            q1, q2 = q_ref[:, h * hq:h * hq + half], q_ref[:, h * hq + half:(h + 1) * hq]
            k1, k2 = k_ref[:, h * hq:h * hq + half], k_ref[:, h * hq + half:(h + 1) * hq]
            qr = jnp.concatenate([q1 * co - q2 * si, q2 * co + q1 * si], axis=1)
            kr = jnp.concatenate([k1 * co - k2 * si, k2 * co + k1 * si], axis=1) * scale
            qr_ref[:, h * hq:(h + 1) * hq] = qr.astype(MXU_DTYPE)
            kr_ref[:, h * hq:(h + 1) * hq] = kr.astype(MXU_DTYPE)
            v = v_ref[:, h * hv:(h + 1) * hv]
            r_in = rs[h * hq:(h + 1) * hq, :]
            st_ref[0, h * hq:(h + 1) * hq, :] = r_in.astype(MXU_DTYPE)
            s = _dot_nt(qr, kr) * dm
            o = _dot(s, v) + cs * _dot(qr, r_in)
            rs[h * hq:(h + 1) * hq, :] = gam * r_in + _dot_tn(kr * kdec, v)
            o_ref[:, h * hv:(h + 1) * hv] = o
            on = o * lax.rsqrt(jnp.mean(o * o, axis=1, keepdims=True) + EPS)
            y_ref[:, h * hv:(h + 1) * hv] = (_silu(g_ref[:, h * hv:(h + 1) * hv]) * on).astype(MXU_DTYPE)

    def col(width, c0):
        return pl.BlockSpec((CHUNK, width), lambda i: (i, c0 // width))

    tab = pl.BlockSpec((CHUNK, half), lambda i: (i, 0))
    return _call(
        body, name, (nc,),
        [col(d, c_q), col(d, c_k), col(2 * d, c_v), col(2 * d, c_g), tab, tab],
        [col(2 * d, 0), col(2 * d, 0), col(d, 0), col(d, 0), pl.BlockSpec((1, d, hv), lambda i: (i, 0, 0))],
        [jax.ShapeDtypeStruct((t, 2 * d), F32), jax.ShapeDtypeStruct((t, 2 * d), MXU_DTYPE),
         jax.ShapeDtypeStruct((t, d), MXU_DTYPE), jax.ShapeDtypeStruct((t, d), MXU_DTYPE),
         jax.ShapeDtypeStruct((nc, d, hv), MXU_DTYPE)],
        [pltpu.VMEM((d, hv), F32)], ("arbitrary",), (proj, proj, proj, proj, cos, sin), comm)


def _ret_bwd(dy, proj, c_v, c_g, o, qr, kr, st, cos, sin, d, name, comm=None):
    t = proj.shape[0]
    nc = t // CHUNK
    hq, hv = d // RET_HEADS, 2 * d // RET_HEADS
    half = hq // 2
    scale = hq ** -0.5

    def body(dy_ref, v_ref, g_ref, o_ref, qr_ref, kr_ref, st_ref, cos_ref, sin_ref, dqk_ref, dvg_ref, drs):
        dq_ref, dk_ref = dqk_ref.at[:, pl.ds(0, d)], dqk_ref.at[:, pl.ds(d, d)]
        dv_ref, dg_ref = dvg_ref.at[:, pl.ds(0, 2 * d)], dvg_ref.at[:, pl.ds(2 * d, 2 * d)]

        @pl.when(pl.program_id(0) == 0)
        def _():
            drs[...] = jnp.zeros_like(drs)

        co, si = cos_ref[...], sin_ref[...]
        for h in range(RET_HEADS):
            dm, dmt, cs, kdec, gam = _ret_consts(h)
            vs = slice(h * hv, (h + 1) * hv)
            qs = slice(h * hq, (h + 1) * hq)
            o_h = o_ref[:, vs]
            g_h = g_ref[:, vs]
            d_y = dy_ref[:, vs]
            r = lax.rsqrt(jnp.mean(o_h * o_h, axis=1, keepdims=True) + EPS)
            on = o_h * r
            d_on = d_y * _silu(g_h)
            dg_ref[:, vs] = (d_y * on * _dsilu(g_h)).astype(MXU_DTYPE)
            d_o = r * (d_on - on * jnp.mean(d_on * on, axis=1, keepdims=True))
            q_h, k_h, v_h = qr_ref[:, qs], kr_ref[:, qs], v_ref[:, vs]
            r_in = st_ref[0, qs, :]
            dr_n = drs[qs, :]
            csdo = cs * d_o
            ds = _dot_nt(d_o, v_h) * dm
            dst = _dot_nt(v_h, d_o) * dmt
            s_t = _dot_nt(k_h, q_h) * dmt
            dqr = _dot(ds, k_h) + _dot_nt(csdo, r_in)
            dkr = _dot(dst, q_h) + kdec * _dot_nt(v_h, dr_n)
            dv_ref[:, vs] = (_dot(s_t, d_o) + _dot(k_h.astype(F32) * kdec, dr_n)).astype(MXU_DTYPE)
            drs[qs, :] = gam * dr_n + _dot_tn(q_h, csdo)
            a1, a2 = dqr[:, :half], dqr[:, half:]
            dq_ref[:, qs] = jnp.concatenate([a1 * co + a2 * si, a2 * co - a1 * si], axis=1).astype(MXU_DTYPE)
            b1, b2 = dkr[:, :half] * scale, dkr[:, half:] * scale
            dk_ref[:, qs] = jnp.concatenate([b1 * co + b2 * si, b2 * co - b1 * si], axis=1).astype(MXU_DTYPE)

    def col(width, c0=0):
        return pl.BlockSpec((CHUNK, width), lambda i: (nc - 1 - i, c0 // width))

    tab = pl.BlockSpec((CHUNK, half), lambda i: (nc - 1 - i, 0))
    return _call(
        body, name, (nc,),
        [col(2 * d), col(2 * d, c_v), col(2 * d, c_g), col(2 * d), col(d), col(d),
         pl.BlockSpec((1, d, hv), lambda i: (nc - 1 - i, 0, 0)), tab, tab],
        [col(2 * d), col(4 * d)],
        [jax.ShapeDtypeStruct((t, 2 * d), MXU_DTYPE), jax.ShapeDtypeStruct((t, 4 * d), MXU_DTYPE)],
        [pltpu.VMEM((d, hv), F32)], ("arbitrary",), (dy, proj, proj, o, qr, kr, st, cos, sin), comm)


def _ssd_small(dtraw_ref, dtb_ref, alog_ref, chunk_idx, nh):
    rows = _iota((CHUNK, 1), 0)
    ok = ((rows >= FRONT) | (chunk_idx > 0)) & (_iota((1, LANES), 1) < nh)
    z = dtraw_ref[...] + dtb_ref[...]
    dt = jnp.where(ok, jax.nn.softplus(z), 0.0)
    sig = jnp.where(ok, jax.nn.sigmoid(z), 0.0)
    a = jnp.where(_iota((1, LANES), 1) < nh, -jnp.exp(alog_ref[...]), 0.0)
    tri = (_iota((CHUNK, CHUNK), 0) >= _iota((CHUNK, CHUNK), 1)).astype(F32)
    acs = _dot01(tri, dt * a, "b", 3)
    return dt, sig, a, acs, acs.T


def _head_expand(g, hpg, gw):
    shift = int(math.log2(SSD_HEAD_DIM))
    return (_iota((LANES, gw), 0) == g * hpg + lax.shift_right_logical(_iota((LANES, gw), 1), shift)).astype(F32)


def _ssd_fwd(xa, proj, c_dt, c_z, dtb, alog, dvec, nw, di, name, comm=None):
    t = xa.shape[0]
    nc = t // CHUNK
    nh = di // SSD_HEAD_DIM
    hpg = nh // SSD_GROUPS
    gw = di // SSD_GROUPS
    n = SSD_STATE
    gn = SSD_GROUPS * n
    hd = SSD_HEAD_DIM

    def body(x_ref, b_ref, c_ref, dtraw_ref, z_ref, dtb_ref, alog_ref, d_ref, nw_ref,
             y_ref, ys_ref, st_ref, hts, xdt_s):
        c = pl.program_id(0)

        @pl.when(c == 0)
        def _():
            hts[...] = jnp.zeros_like(hts)

        dt, _, _, acs, acs_t = _ssd_small(dtraw_ref, dtb_ref, alog_ref, c, nh)
        tri = _iota((CHUNK, CHUNK), 0) >= _iota((CHUNK, CHUNK), 1)
        dvec8 = jnp.broadcast_to(d_ref[...], (SUBLANES, LANES))
        for g in range(SSD_GROUPS):
            gs = slice(g * gw, (g + 1) * gw)
            ns = slice(g * n, (g + 1) * n)
            e_mat = _head_expand(g, hpg, gw)
            ax = _dot01(acs, e_mat, "a", 3)
            dtx = _dot01(dt, e_mat, "a", 3)
            dx = _dot01(dvec8, e_mat, "a", 3)[0:1, :]
            xg, bg, cg = x_ref[:, gs], b_ref[:, ns], c_ref[:, ns]
            xdt = xg * dtx
            xdt_s[...] = xdt.astype(MXU_DTYPE)
            cb = _dot_nt(cg, bg)
            ht = hts[ns, :]
            st_ref[0, ns, :] = ht.astype(MXU_DTYPE)
            y_ref[:, gs] = jnp.exp(ax) * _dot(cg, ht) + dx * xg
            for hh in range(hpg):
                h = g * hpg + hh
                lmat = jnp.exp(jnp.where(tri, acs[:, h:h + 1] - acs_t[h:h + 1, :], -jnp.inf))
                hs = slice(g * gw + hh * hd, g * gw + (hh + 1) * hd)
                y_ref[:, hs] += _dot(cb * lmat, xdt_s[:, hh * hd:(hh + 1) * hd])
            aend = ax[CHUNK - 1:CHUNK, :]
            hts[ns, :] = jnp.exp(aend) * ht + _dot_tn(bg, xdt * jnp.exp(aend - ax))
        for g in range(SSD_GROUPS):
            gs = slice(g * gw, (g + 1) * gw)
            yz = y_ref[:, gs] * _silu(z_ref[:, gs])
            r = lax.rsqrt(jnp.mean(yz * yz, axis=1, keepdims=True) + EPS)
            ys_ref[:, gs] = (yz * r * nw_ref[:, gs]).astype(MXU_DTYPE)

    def col(width, c0, arr_is_xa=False):
        return pl.BlockSpec((CHUNK, width), lambda i: (i, c0 // width))

    vec = pl.BlockSpec((1, LANES), lambda i: (0, 0))
    assert di % gn == 0 and c_dt % LANES == 0 and c_z % di == 0
    return _call(
        body, name, (nc,),
        [col(di, 0), col(gn, di), col(gn, di + gn), col(LANES, c_dt), col(di, c_z), vec, vec, vec,
         pl.BlockSpec((1, di), lambda i: (0, 0))],
        [col(di, 0), col(di, 0), pl.BlockSpec((1, gn, gw), lambda i: (i, 0, 0))],
        [jax.ShapeDtypeStruct((t, di), F32), jax.ShapeDtypeStruct((t, di), MXU_DTYPE),
         jax.ShapeDtypeStruct((nc, gn, gw), MXU_DTYPE)],
        [pltpu.VMEM((gn, gw), F32), pltpu.VMEM((CHUNK, gw), MXU_DTYPE)], ("arbitrary",),
        (xa, xa, xa, proj, proj, dtb, alog, dvec, nw), comm)


def _ssd_bwd(dys, xa, proj, c_dt, c_z, ypre, st, dtb, alog, dvec, nw, di, name, comm=None):
    t = xa.shape[0]
    nc = t // CHUNK
    nh = di // SSD_HEAD_DIM
    hpg = nh // SSD_GROUPS
    gw = di // SSD_GROUPS
    n = SSD_STATE
    gn = SSD_GROUPS * n
    hd = SSD_HEAD_DIM

    def body(dys_ref, x_ref, b_ref, c_ref, dtraw_ref, z_ref, y_ref, st_ref, dtb_ref, alog_ref, d_ref, nw_ref,
             dxa_ref, dz_ref, ddt_ref, gb_ref, ga_ref, gd_ref, gnw_ref, dhts, dy_s, xdt_s, dxdt_s):
        i = pl.program_id(0)
        c = nc - 1 - i

        @pl.when(i == 0)
        def _():
            dhts[...] = jnp.zeros_like(dhts)
            gb_ref[...] = jnp.zeros_like(gb_ref)
            ga_ref[...] = jnp.zeros_like(ga_ref)
            gd_ref[...] = jnp.zeros_like(gd_ref)
            gnw_ref[...] = jnp.zeros_like(gnw_ref)

        dt, sig, a, acs, acs_t = _ssd_small(dtraw_ref, dtb_ref, alog_ref, c, nh)
        tri = _iota((CHUNK, CHUNK), 0) >= _iota((CHUNK, CHUNK), 1)
        triu = _iota((CHUNK, CHUNK), 0) <= _iota((CHUNK, CHUNK), 1)
        lane = _iota((1, LANES), 1)
        rows = _iota((CHUNK, 1), 0)
        head_row = _iota((LANES, 1), 0)
        dvec8 = jnp.broadcast_to(d_ref[...], (SUBLANES, LANES))
        da = jnp.zeros((CHUNK, LANES), F32)
        da_t = jnp.zeros((LANES, CHUNK), F32)
        ddt = jnp.zeros((CHUNK, LANES), F32)
        gd = jnp.zeros((1, LANES), F32)
        for g in range(SSD_GROUPS):
            gs = slice(g * gw, (g + 1) * gw)
            ns = slice(g * n, (g + 1) * n)
            y_g, z_g = y_ref[:, gs], z_ref[:, gs]
            sz = _silu(z_g)
            yz = y_g * sz
            r = lax.rsqrt(jnp.mean(yz * yz, axis=1, keepdims=True) + EPS)
            nrm = yz * r
            dyo = dys_ref[:, gs]
            gnw_ref[:, gs] += jnp.sum(dyo * nrm, axis=0, keepdims=True)
            dn = dyo * nw_ref[:, gs]
            dyz = r * (dn - nrm * jnp.mean(dn * nrm, axis=1, keepdims=True))
            dz_ref[:, gs] = (dyz * y_g * _dsilu(z_g)).astype(MXU_DTYPE)
            dy_g = dyz * sz
            dy_s[...] = dy_g.astype(MXU_DTYPE)
            e_mat = _head_expand(g, hpg, gw)
            ax = _dot01(acs, e_mat, "a", 3)
            dtx = _dot01(dt, e_mat, "a", 3)
            dx = _dot01(dvec8, e_mat, "a", 3)[0:1, :]
            xg, bg, cg = x_ref[:, gs], b_ref[:, ns], c_ref[:, ns]
            xdt = xg * dtx
            xdt_s[...] = xdt.astype(MXU_DTYPE)
            aend = ax[CHUNK - 1:CHUNK, :]
            e = jnp.exp(aend - ax)
            ea = jnp.exp(ax)
            eend = jnp.exp(aend)
            htp = st_ref[0, ns, :].astype(F32)
            dht = dhts[ns, :]
            cb = _dot_nt(cg, bg)
            q = _dot(bg, dht)
            dxdt_s[...] = e * q
            wl = e * q * xdt
            d_b = _dot_nt(e * xdt, dht)
            yi = ea * _dot(cg, htp)
            eady = ea * dy_g
            d_c = _dot_nt(eady, htp)
            t1 = jnp.sum(dht * htp, axis=0, keepdims=True) * eend
            dhts[ns, :] = eend * dht + _dot_tn(cg, eady)
            da = da + _dot01(dy_g * yi - wl, e_mat, "a", 3, _NT)
            tail = jnp.broadcast_to(jnp.sum(wl, axis=0, keepdims=True) + t1, (SUBLANES, gw))
            da_end = _dot01(tail, e_mat, "a", 3, _NT)[0:1, :]
            da = da + jnp.where(rows == CHUNK - 1, da_end, 0.0)
            dcb = jnp.zeros((CHUNK, CHUNK), F32)
            for hh in range(hpg):
                h = g * hpg + hh
                lmat = jnp.exp(jnp.where(tri, acs[:, h:h + 1] - acs_t[h:h + 1, :], -jnp.inf))
                hl = slice(hh * hd, (hh + 1) * hd)
                dy_h, xdt_h = dy_s[:, hl], xdt_s[:, hl]
                dxdt_s[:, hl] += _dot_tn(cb * lmat, dy_h)
                dml = _dot_nt(dy_h, xdt_h) * lmat
                dcb = dcb + dml
                gmat = dml * cb
                da = da + jnp.where(lane == h, jnp.sum(gmat, axis=1, keepdims=True), 0.0)
                da_t = da_t - jnp.where(head_row == h, jnp.sum(gmat, axis=0, keepdims=True), 0.0)
            d_c = d_c + _dot(dcb, bg)
            d_b = d_b + _dot_tn(dcb, cg)
            dxdt = dxdt_s[...]
            dxa_ref[:, gs] = dxdt * dtx + dx * dy_g
            dxa_ref[:, di + g * n:di + (g + 1) * n] = d_b
            dxa_ref[:, di + gn + g * n:di + gn + (g + 1) * n] = d_c
            ddt = ddt + _dot01(dxdt * xg, e_mat, "a", 3, _NT)
            gd8 = jnp.broadcast_to(jnp.sum(dy_g * xg, axis=0, keepdims=True), (SUBLANES, gw))
            gd = gd + _dot01(gd8, e_mat, "a", 3, _NT)[0:1, :]
        da = da + da_t.T
        triu_f = triu.astype(F32)
        ddta = _dot01(triu_f, da, "b", 3)
        ddt = ddt + ddta * a
        draw = ddt * sig
        ddt_ref[...] = draw.astype(MXU_DTYPE)
        gb_ref[...] += jnp.sum(draw, axis=0, keepdims=True)
        ga_ref[...] += jnp.sum(ddta * dt, axis=0, keepdims=True) * a
        gd_ref[...] += gd

    def col(width, c0):
        return pl.BlockSpec((CHUNK, width), lambda i: (nc - 1 - i, c0 // width))

    vec = pl.BlockSpec((1, LANES), lambda i: (0, 0))
    wide = pl.BlockSpec((1, di), lambda i: (0, 0))
    wa = di + 2 * gn
    return _call(
        body, name, (nc,),
        [col(di, 0), col(di, 0), col(gn, di), col(gn, di + gn), col(LANES, c_dt), col(di, c_z), col(di, 0),
         pl.BlockSpec((1, gn, gw), lambda i: (nc - 1 - i, 0, 0)), vec, vec, vec, wide],
        [col(wa, 0), col(di, 0), col(LANES, 0), vec, vec, vec, wide],
        [jax.ShapeDtypeStruct((t, wa), F32), jax.ShapeDtypeStruct((t, di), MXU_DTYPE),
         jax.ShapeDtypeStruct((t, LANES), MXU_DTYPE), jax.ShapeDtypeStruct((1, LANES), F32),
         jax.ShapeDtypeStruct((1, LANES), F32), jax.ShapeDtypeStruct((1, LANES), F32),
         jax.ShapeDtypeStruct((1, di), F32)],
        [pltpu.VMEM((gn, gw), F32), pltpu.VMEM((CHUNK, gw), MXU_DTYPE), pltpu.VMEM((CHUNK, gw), MXU_DTYPE),
         pltpu.VMEM((CHUNK, gw), F32)], ("arbitrary",),
        (dys, xa, xa, xa, proj, proj, ypre, st, dtb, alog, dvec, nw), comm)


def _adam_math(w, g, m, v):
    m2 = ADAM_B1 * m + (1.0 - ADAM_B1) * g
    v2 = ADAM_B2 * v + (1.0 - ADAM_B2) * (g * g)
    m_hat = m2 / (1.0 - ADAM_B1 ** ADAM_STEP)
    v_hat = v2 / (1.0 - ADAM_B2 ** ADAM_STEP)
    delta = -ADAM_LR * (m_hat / (jnp.sqrt(v_hat) + ADAM_EPS) + ADAM_WD * w)
    return delta, m2, v2


def _adam_big(w, g_mine, g_sib, m, v, core, name):
    r, c = w.shape
    h = r // 2
    tr = _pick(h, (128, 64, 32, 16, 8))
    nbh = h // tr

    def body(core_ref, w_ref, a_ref, b_ref, m_ref, v_ref, g_ref, d_ref, m2_ref, v2_ref):
        g = jnp.where(pl.program_id(0) // nbh == core_ref[0], a_ref[...], b_ref[...])
        delta, m2, v2 = _adam_math(w_ref[...], g, m_ref[...], v_ref[...])
        g_ref[...] = g
        d_ref[...] = delta
        m2_ref[...] = m2
        v2_ref[...] = v2

    blk = pl.BlockSpec((tr, c), lambda i, core_ref: (i, 0))
    hblk = pl.BlockSpec((tr, c), lambda i, core_ref: (i % nbh, 0))
    out = jax.ShapeDtypeStruct((r, c), F32)
    return pl.pallas_call(
        body, name=name,
        grid_spec=pltpu.PrefetchScalarGridSpec(num_scalar_prefetch=1, grid=(2 * nbh,),
                                               in_specs=[blk, hblk, hblk, blk, blk], out_specs=[blk] * 4),
        out_shape=[out] * 4, compiler_params=_params(("parallel",)),
    )(core, w, g_mine, g_sib, m, v)


def _pair_sum(g, sib, core, name):
    _, r, c = g.shape
    h = r // 2
    tr = _pick(h, (128, 64, 32, 16))
    nb = h // tr

    def body(core_ref, g_ref, s_ref, o_ref):
        o_ref[...] = (g_ref[...].astype(F32) + s_ref[...].astype(F32)).astype(WIRE_DTYPE)

    return pl.pallas_call(
        body, name=name,
        grid_spec=pltpu.PrefetchScalarGridSpec(
            num_scalar_prefetch=1, grid=(4, nb),
            in_specs=[pl.BlockSpec((1, tr, c), lambda j, i, core_ref: (j, core_ref[0] * nb + i, 0)),
                      pl.BlockSpec((1, tr, c), lambda j, i, core_ref: (j, i, 0))],
            out_specs=pl.BlockSpec((1, tr, c), lambda j, i, core_ref: (j, i, 0))),
        out_shape=jax.ShapeDtypeStruct((4, h, c), WIRE_DTYPE), compiler_params=_params(("parallel", "parallel")),
    )(core, g, sib)


def _sum4(parts, name):
    _, r, c = parts.shape
    tr = _pick(r, (128, 64, 32, 16, 8))

    def body(p_ref, o_ref):
        acc = p_ref[0].astype(F32)
        for j in range(1, 4):
            acc = acc + p_ref[j].astype(F32)
        o_ref[...] = acc

    return pl.pallas_call(
        body, name=name, grid=(r // tr,),
        in_specs=[pl.BlockSpec((4, tr, c), lambda i: (0, i, 0))],
        out_specs=pl.BlockSpec((tr, c), lambda i: (i, 0)),
        out_shape=jax.ShapeDtypeStruct((r, c), F32),
        compiler_params=_params(("parallel",)),
    )(parts)


def _adam_small(w, gparts, m, v, name):
    r = w.shape[0]

    def body(w_ref, g_ref, m_ref, v_ref, go_ref, d_ref, m2_ref, v2_ref):
        g = g_ref[0]
        for j in range(1, 8):
            g = g + g_ref[j]
        delta, m2, v2 = _adam_math(w_ref[...], g, m_ref[...], v_ref[...])
        go_ref[...] = g
        d_ref[...] = delta
        m2_ref[...] = m2
        v2_ref[...] = v2

    out = jax.ShapeDtypeStruct((r, LANES), F32)
    return pl.pallas_call(body, name=name, out_shape=[out] * 4)(w, gparts, m, v)


def _allgather8(v, name):
    def body(src, dst, send, recv, loc):
        x, y, c = lax.axis_index("x"), lax.axis_index("y"), lax.axis_index("c")
        mine = 4 * x + 2 * y + c
        lc = pltpu.make_async_copy(src, dst.at[mine], loc)
        lc.start()
        copies = []
        for k in range(1, 8):
            fx, fy, fc = (k >> 2) & 1, (k >> 1) & 1, k & 1
            peer = (1 - x if fx else x, 1 - y if fy else y, 1 - c if fc else c)
            cp = pltpu.make_async_remote_copy(src_ref=src, dst_ref=dst.at[mine], send_sem=send.at[k - 1],
                                              recv_sem=recv.at[k - 1], device_id=peer, device_id_type=MESH)
            cp.start()
            copies.append(cp)
        for cp in copies:
            cp.wait()
        lc.wait()

    anyspec = pl.BlockSpec(memory_space=pl.ANY)
    return pl.pallas_call(
        body, name=name, in_specs=[anyspec], out_specs=anyspec,
        out_shape=jax.ShapeDtypeStruct((8,) + v.shape, v.dtype),
        scratch_shapes=[pltpu.SemaphoreType.DMA((7,)), pltpu.SemaphoreType.DMA((7,)), pltpu.SemaphoreType.DMA],
        compiler_params=pltpu.CompilerParams(has_side_effects=True),
    )(v)


def _pack(parts):
    flat = jnp.concatenate([p.reshape(-1).astype(F32) for p in parts])
    pad = (-flat.shape[0]) % (32 * LANES)
    return jnp.pad(flat, (0, pad)).reshape(-1, LANES)


def _unpack(slab, shapes):
    flat = slab.reshape(-1)
    out, off = [], 0
    for s in shapes:
        size = int(np.prod(s))
        out.append(flat[off:off + size].reshape(s))
        off += size
    return out


def _pad_lanes(v):
    return jnp.pad(v.reshape(1, -1), ((0, 0), (0, LANES - v.shape[-1])))


def kernel(x, meta_tokens, mix_norm_w, w_in, ssd_conv_w, ssd_conv_b, ssd_dt_bias, ssd_A_log, ssd_D, ssd_norm_w, w_branch_ssd, w_branch_ret, w_out, ffn_norm_w, w_up, ffn_conv_w, ffn_conv_b, w_down, final_norm_w, loss_target, m_meta_tokens, m_mix_norm_w, m_w_in, m_ssd_conv_w, m_ssd_conv_b, m_ssd_dt_bias, m_ssd_A_log, m_ssd_D, m_ssd_norm_w, m_w_branch_ssd, m_w_branch_ret, m_w_out, m_ffn_norm_w, m_w_up, m_ffn_conv_w, m_ffn_conv_b, m_w_down, m_final_norm_w, v_meta_tokens, v_mix_norm_w, v_w_in, v_ssd_conv_w, v_ssd_conv_b, v_ssd_dt_bias, v_ssd_A_log, v_ssd_D, v_ssd_norm_w, v_w_branch_ssd, v_w_branch_ret, v_w_out, v_ffn_norm_w, v_w_up, v_ffn_conv_w, v_ffn_conv_b, v_w_down, v_final_norm_w):
    seq, d = x.shape[1], x.shape[2]
    t = CHUNK + seq
    di = 2 * d
    nh = di // SSD_HEAD_DIM
    gn = SSD_GROUPS * SSD_STATE
    cw = di + 2 * gn
    f = w_down.shape[1] * 4
    chip = 2 * lax.axis_index("x") + lax.axis_index("y")

    order = [("z", di), ("v", di), ("g", di), ("xbc", cw), ("q", d), ("k", d), ("gs", d), ("gr", d), ("dt", LANES)]
    col, acc = {}, 0
    for nm, wd in order:
        col[nm] = acc
        acc += wd
    wp = acc
    ref_order = [("z", di), ("xbc", cw), ("dt", nh), ("q", d), ("k", d), ("v", di), ("g", di), ("gs", d), ("gr", d)]
    ref_off, acc = {}, 0
    for nm, wd in ref_order:
        ref_off[nm] = (acc, wd)
        acc += wd
    in_dim = acc

    core = lax.axis_index("c").astype(jnp.int32).reshape(1)
    small_shapes = [meta_tokens.shape, ssd_conv_w.shape[1:], ffn_conv_w.shape[1:]]
    small_local = _pack([meta_tokens, ssd_conv_w[0], ffn_conv_w[0]])
    first_local = [w_in[0].astype(WIRE_DTYPE), small_local]
    first_half = _run_comm(_gather_ici(first_local), "gather_w_in_ici")
    g_in, g_small = [_with_own(g, own, chip)
                     for g, own in zip(_run_comm(_gather_d2d(first_half), "gather_w_in_d2d"), first_local)]
    rest_local = [a[0].astype(WIRE_DTYPE) for a in (w_branch_ssd, w_branch_ret, w_out, w_up, w_down)]
    w_in_full = jnp.moveaxis(g_in, 0, 1).reshape(d, in_dim)
    pieces = []
    for nm, wd in order:
        o, rw = ref_off[nm]
        p = w_in_full[:, o:o + rw]
        if rw < wd:
            p = jnp.pad(p, ((0, 0), (0, wd - rw)))
        pieces.append(p)
    w_p = jnp.concatenate(pieces, axis=1)
    smalls = [_unpack(g_small[j], small_shapes) for j in range(4)]
    meta_full = jnp.concatenate([s[0] for s in smalls], axis=1)
    scw = jnp.concatenate([s[1] for s in smalls], axis=1)
    fcw = jnp.concatenate([s[2] for s in smalls], axis=1)
    scb, fcb = ssd_conv_b, ffn_conv_b
    dtb, alog, dvec = _pad_lanes(ssd_dt_bias), _pad_lanes(ssd_A_log), _pad_lanes(ssd_D)
    fin_w = final_norm_w.reshape(1, d)

    hq = d // RET_HEADS
    pos = jnp.arange(t, dtype=F32) - FRONT
    inv_freq = ROPE_BASE ** (-jnp.linspace(0.0, 1.0, hq // 2, dtype=F32))
    ang = pos[:, None] * inv_freq[None, :]
    cos, sin = jnp.cos(ang), jnp.sin(ang)

    h0 = jnp.concatenate([jnp.zeros((FRONT, d), F32), meta_full, x[0]], axis=0)
    tm = _row_tile(t)
    tmb = _pick(t, (1664, 1280, 640, 512, 384, 256, 128))
    u1 = _rms_fwd(h0, mix_norm_w, "rms1_fwd")
    proj = _mm(u1, w_p, "nn", F32, "proj", tmb, _pick(wp, (1920, 1536, 1280, 1024, 896, 768, 640, 512, 384, 256, 128)), d)
    xa = _ssd_conv_fwd(proj, col["xbc"], cw, scw, scb, "ssd_conv_fwd")
    res = _ssd_fwd(xa, proj, col["dt"], col["z"], dtb, alog, dvec, ssd_norm_w, di, "ssd_fwd", comm=_gather_ici(rest_local))
    (ypre, yssd, st_ssd), rest_half = res[:3], res[3:]
    res = _ret_fwd(proj, col["q"], col["k"], col["v"], col["g"], cos, sin, d, "ret_fwd", comm=_gather_d2d(rest_half))
    o_ret, yret, qr, kr, st_ret = res[:5]
    g_bs, g_br, g_out, g_up, g_down = [_with_own(g, own, chip) for g, own in zip(res[5:], rest_local)]
    w_bs = g_bs.reshape(di, d)
    w_br = g_br.reshape(di, d)
    w_o = g_out.reshape(d, d)
    w_u = jnp.moveaxis(g_up, 0, 1).reshape(d, 2 * f)
    w_d = g_down.reshape(f, d)
    tn_d = _pick(d, (1024, 512, 256, 128))
    bs = _mm(yssd, w_bs, "nn", F32, "branch_ssd", tmb, tn_d, _pick(di, (1024, 512, 256)))
    br = _mm(yret, w_br, "nn", F32, "branch_ret", tmb, tn_d, _pick(di, (1024, 512, 256)))
    merged = _gate_fwd(bs, br, proj, col["gs"], col["gr"], "gate_fwd")
    h1 = _mm(merged, w_o, "nn", F32, "out_proj", tmb, tn_d, d, res=h0)
    u2 = _rms_fwd(h1, ffn_norm_w, "rms2_fwd")
    tn_f = _pick(2 * f, (1408, 1024, 768, 512, 256, 128))
    up = _mm(u2, w_u, "nn", F32, "up_proj", tmb, tn_f, d)
    act = _ffn_conv_fwd(up, fcw, fcb, "ffn_conv_fwd")
    tk_f = _pick(f, (1408, 768, 704, 512, 256, 128))
    h2 = _mm(act, w_d, "nn", F32, "down_proj", tm, tn_d, tk_f, res=h1)
    loss8, d_h2, g_fin = _loss_bwd(h2, fin_w, loss_target[0], "loss_head")

    tkt = _pick(t, (1664, 1280, 1024, 640, 512, 384, 256, 128))
    d_act = _mm(d_h2, w_d, "nt", F32, "d_act", tmb, tk_f, d)
    g_wd = _mm(act, d_h2, "tn", F32, "g_w_down", tk_f, tn_d, tkt)
    d_upg, d_upv, g_fcwg, g_fcwv, g_fcbg, g_fcbv = _ffn_conv_bwd(up, fcw, fcb, d_act, "ffn_conv_bwd")
    g_fcw = jnp.concatenate([g_fcwg, g_fcwv], axis=1)
    g_fcb = jnp.concatenate([g_fcbg, g_fcbv], axis=1)
    d_u2 = _mm(d_upg, w_u[:, :f], "nt", F32, "d_u2_gate", tm, tn_d, tk_f)
    d_u2 = _mm(d_upv, w_u[:, f:], "nt", F32, "d_u2_value", tm, tn_d, tk_f, res=d_u2)
    g_wu = jnp.concatenate([_mm(u2, d_upg, "tn", F32, "g_w_up_gate", tn_d, tk_f, tkt),
                            _mm(u2, d_upv, "tn", F32, "g_w_up_value", tn_d, tk_f, tkt)], axis=1)
    d_h1, g_ffnw = _rms_bwd(h1, ffn_norm_w, d_u2, d_h2, "rms2_bwd")
    d_merged = _mm(d_h1, w_o, "nt", F32, "d_merged", tmb, tn_d, d)
    g_wo = _mm(merged, d_h1, "tn", F32, "g_w_out", tn_d, tn_d, tkt)
    d_bs, d_br, d_gsr = _gate_bwd(d_merged, bs, br, proj, col["gs"], col["gr"], "gate_bwd")
    tk_i = _pick(di, (1024, 512, 256))
    d_yssd = _mm(d_bs, w_bs, "nt", F32, "d_y_ssd", tmb, tk_i, d)
    g_wbs = _mm(yssd, d_bs, "tn", F32, "g_w_branch_ssd", tk_i, tn_d, tkt)
    d_yret = _mm(d_br, w_br, "nt", F32, "d_y_ret", tmb, tk_i, d)
    g_wbr = _mm(yret, d_br, "tn", F32, "g_w_branch_ret", tk_i, tn_d, tkt)

    early_names = ["w_branch_ssd", "w_branch_ret", "w_out", "w_up", "w_down"]
    early = [g_wbs.reshape(4, di // 4, d), g_wbr.reshape(4, di // 4, d), g_wo.reshape(4, d // 4, d),
             jnp.moveaxis(g_wu.reshape(d, 4, 2 * f // 4), 1, 0), g_wd.reshape(4, f // 4, d)]
    res = _ret_bwd(d_yret, proj, col["v"], col["g"], o_ret, qr, kr, st_ret, cos, sin, d, "ret_bwd", comm=_scatter_d2d(early))
    (dqk, dvg), early_sib = res[:2], res[2:]
    early_pair = [_pair_sum(g_, s_, core, "pair_" + nm) for g_, s_, nm in zip(early, early_sib, early_names)]
    res = _ssd_bwd(d_yssd, xa, proj, col["dt"], col["z"], ypre, st_ssd, dtb, alog, dvec, ssd_norm_w, di, "ssd_bwd",
                   comm=_scatter_ici(early_pair))
    (d_xa, dz, ddt, g_dtb, g_alog, g_dvec, g_snw), early_recv = res[:7], res[7:]
    early_mine = [_sum4(p, "sum4_" + nm) for p, nm in zip(early_recv, early_names)]
    res = _ssd_conv_bwd(proj, col["xbc"], cw, scw, scb, d_xa, "ssd_conv_bwd", comm=_sibling_swap(early_mine))
    (d_xbc, g_scw, g_scb), early_other = res[:3], res[3:]
    d_pieces = [("z", dz), ("v", dvg), ("xbc", d_xbc), ("q", dqk), ("gs", d_gsr), ("dt", ddt)]

    g_piece = {nm: _mm(u1, a, "tn", WIRE_DTYPE, "g_w_in_" + nm, tn_d, _pick(a.shape[1], (1024, 768, 512, 256, 128)), tkt)
               for nm, a in d_pieces}
    g_cols = dict(z=g_piece["z"], v=g_piece["v"][:, :di], g=g_piece["v"][:, di:], xbc=g_piece["xbc"],
                  q=g_piece["q"][:, :d], k=g_piece["q"][:, d:], gs=g_piece["gs"][:, :d], gr=g_piece["gs"][:, d:],
                  dt=g_piece["dt"])
    g_in_ref = jnp.concatenate([g_cols[nm][:, :rw] for nm, rw in ref_order], axis=1)
    sc_in = jnp.moveaxis(g_in_ref.reshape(d, 4, in_dim // 4), 1, 0)
    in_sib = _run_comm(_scatter_d2d([sc_in]), "scatter_w_in_d2d")[0]
    in_pair = _pair_sum(sc_in, in_sib, core, "pair_w_in")
    d_u1, (in_recv,) = _mm_pieces_nt([(a, col[nm]) for nm, a in d_pieces], w_p, F32, "d_u1", tm, tn_d,
                                     comm=_scatter_ici([in_pair]))
    in_mine = _sum4(in_recv, "sum4_w_in")
    d_h0, g_mixw, in_other = _rms_bwd(h0, mix_norm_w, d_u1, d_h1, "rms1_bwd", comm=_sibling_swap([in_mine]))
    grad_x = d_h0[CHUNK:][None]
    g_meta = d_h0[FRONT:CHUNK]

    names = ["w_in"] + early_names
    mine_half = [in_mine] + early_mine
    other_half = [in_other] + list(early_other)
    big_w = [w_in, w_branch_ssd, w_branch_ret, w_out, w_up, w_down]
    big_m = [m_w_in, m_w_branch_ssd, m_w_branch_ret, m_w_out, m_w_up, m_w_down]
    big_v = [v_w_in, v_w_branch_ssd, v_w_branch_ret, v_w_out, v_w_up, v_w_down]
    big_out = {}
    for nm, w_, p_, s_, m_, v_ in zip(names, big_w, mine_half, other_half, big_m, big_v):
        res = _adam_big(w_[0], p_, s_, m_[0], v_[0], core, "adam_" + nm)
        big_out[nm] = [r[None] for r in res]

    kws, kwf = ssd_conv_w.shape[1], ffn_conv_w.shape[1]
    small_grads = [g_meta, g_mixw, g_scw[:kws], g_scb, g_dtb[:, :nh], g_alog[:, :nh], g_dvec[:, :nh], g_snw, g_ffnw,
                   g_fcw[:kwf], g_fcb, g_fin, loss8[0:1, 0:1]]
    sg_shapes = [g.shape for g in small_grads]
    gparts = _allgather8(_pack(small_grads), "gather_small_grads")

    def own_cols(a, width):
        return lax.dynamic_slice_in_dim(a, chip * width, width, axis=1)

    def widen(a, width_full):
        z = jnp.zeros(a.shape[:-1] + (width_full,), F32)
        return lax.dynamic_update_slice_in_dim(z, a, chip * a.shape[-1], axis=a.ndim - 1)

    def small_slab(meta_, mix_, scw_, scb_, dtb_, alog_, d_, snw_, ffnw_, fcw_, fcb_, fin_):
        return _pack([widen(meta_, d), mix_, widen(scw_[0], cw), scb_, dtb_, alog_, d_, snw_, ffnw_, widen(fcw_[0], 2 * f),
                      fcb_, fin_.reshape(1, d), jnp.zeros((1, 1), F32)])

    w_slab = small_slab(meta_tokens, mix_norm_w, ssd_conv_w, ssd_conv_b, ssd_dt_bias, ssd_A_log, ssd_D, ssd_norm_w,
                        ffn_norm_w, ffn_conv_w, ffn_conv_b, final_norm_w)
    m_slab = small_slab(m_meta_tokens, m_mix_norm_w, m_ssd_conv_w, m_ssd_conv_b, m_ssd_dt_bias, m_ssd_A_log, m_ssd_D,
                        m_ssd_norm_w, m_ffn_norm_w, m_ffn_conv_w, m_ffn_conv_b, m_final_norm_w)
    v_slab = small_slab(v_meta_tokens, v_mix_norm_w, v_ssd_conv_w, v_ssd_conv_b, v_ssd_dt_bias, v_ssd_A_log, v_ssd_D,
                        v_ssd_norm_w, v_ffn_norm_w, v_ffn_conv_w, v_ffn_conv_b, v_final_norm_w)
    small_res = [_unpack(s, sg_shapes) for s in _adam_small(w_slab, gparts, m_slab, v_slab, "adam_small")]
    loss = small_res[0][12].reshape(())

    def small_outputs(vals):
        meta_, mix_, scw_, scb_, dtb_, alog_, d_, snw_, ffnw_, fcw_, fcb_, fin_ = vals[:12]
        return {
            "meta_tokens": own_cols(meta_, d // 4), "mix_norm_w": mix_, "ssd_conv_w": own_cols(scw_, cw // 4)[None],
            "ssd_conv_b": scb_, "ssd_dt_bias": dtb_, "ssd_A_log": alog_, "ssd_D": d_, "ssd_norm_w": snw_,
            "ffn_norm_w": ffnw_, "ffn_conv_w": own_cols(fcw_, 2 * f // 4)[None], "ffn_conv_b": fcb_,
            "final_norm_w": fin_.reshape(d),
        }

    weights = ["meta_tokens", "mix_norm_w", "w_in", "ssd_conv_w", "ssd_conv_b", "ssd_dt_bias", "ssd_A_log", "ssd_D",
               "ssd_norm_w", "w_branch_ssd", "w_branch_ret", "w_out", "ffn_norm_w", "w_up", "ffn_conv_w", "ffn_conv_b",
               "w_down", "final_norm_w"]
    outs = [loss, grad_x]
    for kind in range(4):
        so = small_outputs(small_res[kind])
        for nm in weights:
            outs.append(big_out[nm][kind] if nm in big_out else so[nm])
    return tuple(outs)
```

```python
import functools
import math

import jax
import jax.numpy as jnp
import numpy as np
from jax import lax
from jax.experimental import pallas as pl
from jax.experimental.pallas import tpu as pltpu

F32 = jnp.float32
BF16 = jnp.bfloat16
MXU_DTYPE = BF16
WIRE_DTYPE = BF16

N_META = 16
CHUNK = 128
FRONT = CHUNK - N_META
EPS = 1e-6
SSD_HEAD_DIM = 64
SSD_GROUPS = 4
SSD_STATE = 128
SSD_CONV = 4
RET_HEADS = 4
ROPE_BASE = 10000.0
FFN_CONV = 3
LANES = 128
SUBLANES = 8
VMEM_LIMIT = 56 * 1024 * 1024

ADAM_LR = 0.001
ADAM_B1 = 0.9
ADAM_B2 = 0.999
ADAM_EPS = 1e-08
ADAM_WD = 0.01
ADAM_STEP = 10
MESH = pl.DeviceIdType.MESH


def _params(sem=None, vmem=VMEM_LIMIT):
    return pltpu.CompilerParams(dimension_semantics=sem, vmem_limit_bytes=vmem)


def _pick(n, cands):
    for c in cands:
        if n % c == 0:
            return c
    return n


def _silu(x):
    return x * jax.nn.sigmoid(x)


def _dsilu(x):
    s = jax.nn.sigmoid(x)
    return s * (1.0 + x * (1.0 - s))


def _dot(a, b, dims=(((1,), (0,)), ((), ()))):
    return lax.dot_general(a.astype(MXU_DTYPE), b.astype(MXU_DTYPE), dims, preferred_element_type=F32)


def _dot_nt(a, b):
    return _dot(a, b, (((1,), (1,)), ((), ())))


def _dot_tn(a, b):
    return _dot(a, b, (((0,), (0,)), ((), ())))


def _dot01(a, b, split, npass, dims=(((1,), (0,)), ((), ()))):
    rest = (a if split == "a" else b).astype(F32)
    fixed = (b if split == "a" else a).astype(BF16)
    acc = None
    for p in range(npass):
        piece = rest.astype(BF16)
        ops = (piece, fixed) if split == "a" else (fixed, piece)
        term = lax.dot_general(ops[0], ops[1], dims, preferred_element_type=F32)
        acc = term if acc is None else acc + term
        if p + 1 < npass:
            rest = rest - piece.astype(F32)
    return acc


_NT = (((1,), (1,)), ((), ()))


def _iota(shape, dim):
    return lax.broadcasted_iota(jnp.int32, shape, dim)


def _shift_down(cur, prev8, k):
    if k == 0:
        return cur
    rolled = pltpu.roll(cur, k, 0)
    i8 = _iota((SUBLANES, cur.shape[1]), 0)
    head = jnp.where(i8 < k, pltpu.roll(prev8, k, 0), rolled[0:SUBLANES])
    return jnp.concatenate([head, rolled[SUBLANES:]], axis=0)


def _shift_up(cur, next8, k):
    if k == 0:
        return cur
    n = cur.shape[0]
    rolled = pltpu.roll(cur, n - k, 0)
    i8 = _iota((SUBLANES, cur.shape[1]), 0)
    tail = jnp.where(i8 >= SUBLANES - k, pltpu.roll(next8, SUBLANES - k, 0), rolled[n - SUBLANES:])
    return jnp.concatenate([rolled[:n - SUBLANES], tail], axis=0)


class _Comm:
    def __init__(self, ins, outs, nsem, make, in_place=False):
        self.ins, self.outs, self.nsem, self.make = list(ins), list(outs), nsem, make
        self.in_place = in_place


def _place():
    x, y, c = lax.axis_index("x"), lax.axis_index("y"), lax.axis_index("c")
    return x, y, c, 2 * x + y, [(1 - x, y), (x, 1 - y), (1 - x, 1 - y)]


def _call(body, name, grid, in_specs, out_specs, out_shape, scratch, sem, args, comm=None):
    if comm is None:
        return pl.pallas_call(body, name=name, grid=grid, in_specs=in_specs, out_specs=out_specs, out_shape=out_shape,
                              scratch_shapes=scratch, compiler_params=_params(sem))(*args)
    n_in, n_out, n_scr = len(in_specs), len(out_specs), len(scratch)
    ci, co = len(comm.ins), len(comm.outs)

    def wrapped(*refs):
        ins, refs = refs[:n_in], refs[n_in:]
        cins, refs = refs[:ci], refs[ci:]
        outs, refs = refs[:n_out], refs[n_out:]
        couts, refs = refs[:co], refs[co:]
        scr, sems = refs[:n_scr], refs[n_scr:]
        first = functools.reduce(jnp.logical_and, [pl.program_id(a) == 0 for a in range(len(grid))])
        last = functools.reduce(jnp.logical_and, [pl.program_id(a) == grid[a] - 1 for a in range(len(grid))])

        @pl.when(first)
        def _():
            for cp in comm.make(cins, couts, *sems):
                cp.start()

        body(*ins, *outs, *scr)

        @pl.when(last)
        def _():
            for cp in comm.make(cins, couts, *sems):
                cp.wait()

    anyspec = pl.BlockSpec(memory_space=pl.ANY)
    dma = pltpu.SemaphoreType.DMA((comm.nsem,))
    aliases = {n_in + i: n_out + i for i in range(ci)} if comm.in_place else {}
    return pl.pallas_call(
        wrapped, name=name, grid=grid, in_specs=list(in_specs) + [anyspec] * ci,
        out_specs=list(out_specs) + [anyspec] * co, out_shape=list(out_shape) + comm.outs,
        scratch_shapes=list(scratch) + [dma, dma, dma], input_output_aliases=aliases,
        compiler_params=_params(("arbitrary",) * len(grid)))(*args, *comm.ins)


def _run_comm(comm, name):
    ci, co = len(comm.ins), len(comm.outs)

    def body(*refs):
        cins, couts, sems = refs[:ci], refs[ci:ci + co], refs[ci + co:]
        for cp in comm.make(cins, couts, *sems):
            cp.start()
        for cp in comm.make(cins, couts, *sems):
            cp.wait()

    anyspec = pl.BlockSpec(memory_space=pl.ANY)
    dma = pltpu.SemaphoreType.DMA((comm.nsem,))
    aliases = {i: i for i in range(ci)} if comm.in_place else {}
    return pl.pallas_call(body, name=name, in_specs=[anyspec] * ci, out_specs=[anyspec] * co, out_shape=comm.outs,
                          scratch_shapes=[dma, dma, dma], input_output_aliases=aliases)(*comm.ins)


def _half_rows(c, rows):
    h = rows // 2
    return pl.ds(pl.multiple_of(c * h, 16), h)


def _gather_ici(arrays):
    for a in arrays:
        assert a.shape[0] % 32 == 0, a.shape

    def make(ins, outs, send, recv, loc):
        x, y, c, mine, peers = _place()
        cps = []
        for i, a in enumerate(arrays):
            half = _half_rows(c, a.shape[0])
            for k, (px, py) in enumerate(peers):
                cps.append(pltpu.make_async_remote_copy(
                    src_ref=ins[i].at[half], dst_ref=outs[i].at[mine, half], send_sem=send.at[3 * i + k],
                    recv_sem=recv.at[3 * i + k], device_id=(px, py, c), device_id_type=MESH))
        return cps

    outs = [jax.ShapeDtypeStruct((4,) + a.shape, a.dtype) for a in arrays]
    return _Comm(arrays, outs, 3 * len(arrays), make)


def _gather_d2d(bufs):
    def make(ins, outs, send, recv, loc):
        x, y, c, mine, peers = _place()
        cps = []
        for i, a in enumerate(bufs):
            half = _half_rows(c, a.shape[1])
            for k, (px, py) in enumerate(peers):
                mine_half = outs[i].at[2 * px + py, half]
                cps.append(pltpu.make_async_remote_copy(
                    src_ref=mine_half, dst_ref=mine_half, send_sem=send.at[3 * i + k], recv_sem=recv.at[3 * i + k],
                    device_id=(x, y, 1 - c), device_id_type=MESH))
        return cps

    outs = [jax.ShapeDtypeStruct(a.shape, a.dtype) for a in bufs]
    return _Comm(bufs, outs, 3 * len(bufs), make, in_place=True)


def _with_own(gathered, own, chip):
    return lax.dynamic_update_index_in_dim(gathered, own, chip, 0)


def _scatter_d2d(grads):
    for a in grads:
        assert a.shape[1] % 32 == 0, a.shape

    def make(ins, outs, send, recv, loc):
        x, y, c, mine, peers = _place()
        cps = []
        for i, a in enumerate(grads):
            other = _half_rows(1 - c, a.shape[1])
            cps.append(pltpu.make_async_remote_copy(
                src_ref=ins[i].at[:, other], dst_ref=outs[i], send_sem=send.at[i], recv_sem=recv.at[i],
                device_id=(x, y, 1 - c), device_id_type=MESH))
        return cps

    outs = [jax.ShapeDtypeStruct((4, a.shape[1] // 2, a.shape[2]), a.dtype) for a in grads]
    return _Comm(grads, outs, len(grads), make)


def _scatter_ici(parts):
    def make(ins, outs, send, recv, loc):
        x, y, c, mine, peers = _place()
        cps = []
        for i in range(len(parts)):
            cps.append(pltpu.make_async_copy(ins[i].at[mine], outs[i].at[mine], loc.at[i]))
            for k, (px, py) in enumerate(peers):
                cps.append(pltpu.make_async_remote_copy(
                    src_ref=ins[i].at[2 * px + py], dst_ref=outs[i].at[mine], send_sem=send.at[3 * i + k],
                    recv_sem=recv.at[3 * i + k], device_id=(px, py, c), device_id_type=MESH))
        return cps

    outs = [jax.ShapeDtypeStruct(a.shape, a.dtype) for a in parts]
    return _Comm(parts, outs, 3 * len(parts), make)


def _sibling_swap(arrays):
    def make(ins, outs, send, recv, loc):
        x, y, c, mine, peers = _place()
        return [pltpu.make_async_remote_copy(src_ref=ins[i], dst_ref=outs[i], send_sem=send.at[i], recv_sem=recv.at[i],
                                             device_id=(x, y, 1 - c), device_id_type=MESH) for i in range(len(arrays))]

    outs = [jax.ShapeDtypeStruct(a.shape, a.dtype) for a in arrays]
    return _Comm(arrays, outs, len(arrays), make)


def _mm(a, b, mode, out_dtype, name, tm, tn, tk, res=None, comm=None):
    if mode == "nn":
        (m, kd), n = a.shape, b.shape[1]
        a_spec = pl.BlockSpec((tm, tk), lambda i, j, k: (i, k))
        b_spec = pl.BlockSpec((tk, tn), lambda i, j, k: (k, j))
        dims = (((1,), (0,)), ((), ()))
    elif mode == "nt":
        (m, kd), n = a.shape, b.shape[0]
        a_spec = pl.BlockSpec((tm, tk), lambda i, j, k: (i, k))
        b_spec = pl.BlockSpec((tn, tk), lambda i, j, k: (j, k))
        dims = (((1,), (1,)), ((), ()))
    else:
        (kd, m), n = a.shape, b.shape[1]
        a_spec = pl.BlockSpec((tk, tm), lambda i, j, k: (k, i))
        b_spec = pl.BlockSpec((tk, tn), lambda i, j, k: (k, j))
        dims = (((0,), (0,)), ((), ()))
    assert m % tm == 0 and n % tn == 0 and kd % tk == 0, (name, m, n, kd, tm, tn, tk)
    nk = kd // tk
    has_res = res is not None

    def body(*refs):
        a_ref, b_ref = refs[:2]
        r_ref = refs[2] if has_res else None
        o_ref = refs[3 if has_res else 2]

        def finish(r):
            if has_res:
                r = r + r_ref[...].astype(F32)
            o_ref[...] = r.astype(out_dtype)

        if nk == 1:
            finish(_dot(a_ref[...], b_ref[...], dims))
            return
        acc = refs[-1]
        k = pl.program_id(2)

        @pl.when(k == 0)
        def _():
            acc[...] = _dot(a_ref[...], b_ref[...], dims)

        @pl.when((k > 0) & (k < nk - 1))
        def _():
            acc[...] += _dot(a_ref[...], b_ref[...], dims)

        @pl.when(k == nk - 1)
        def _():
            finish(acc[...] + _dot(a_ref[...], b_ref[...], dims))

    in_specs = [a_spec, b_spec]
    args = [a, b]
    if has_res:
        in_specs.append(pl.BlockSpec((tm, tn), lambda i, j, k: (i, j)))
        args.append(res)
    res = _call(body, name, (m // tm, n // tn, nk), in_specs, [pl.BlockSpec((tm, tn), lambda i, j, k: (i, j))],
                [jax.ShapeDtypeStruct((m, n), out_dtype)], [] if nk == 1 else [pltpu.VMEM((tm, tn), F32)],
                ("parallel", "parallel", "arbitrary"), args, comm)
    return res[0] if comm is None else (res[0], res[1:])


def _mm_pieces_nt(pieces, b, name, tm, tn, tk_max, comm=None):
    m, n = pieces[0][0].shape[0], b.shape[0]
    out_dtype = F32
    cands = [c for c in (1024, 512, 256, 128) if c <= tk_max]
    tks = [_pick(math.gcd(a.shape[1], c0) if c0 else a.shape[1], cands) for a, c0 in pieces]
    nks = [a.shape[1] // tk for (a, _), tk in zip(pieces, tks)]
    starts = [sum(nks[:p]) for p in range(len(pieces))]
    ktot = sum(nks)
    npc = len(pieces)

    def body(*refs):
        a_refs, b_refs, o_ref = refs[:npc], refs[npc:2 * npc], refs[2 * npc]
        k = pl.program_id(2)

        @pl.when(k == 0)
        def _():
            o_ref[...] = _dot_nt(a_refs[0][...], b_refs[0][...])

        for p in range(npc):
            @pl.when((k >= max(starts[p], 1)) & (k < starts[p] + nks[p]))
            def _(p=p):
                o_ref[...] += _dot_nt(a_refs[p][...], b_refs[p][...])

    def a_spec(p):
        return pl.BlockSpec((tm, tks[p]), lambda i, j, k: (i, jnp.clip(k - starts[p], 0, nks[p] - 1)))

    def b_spec(p):
        c0 = pieces[p][1] // tks[p]
        return pl.BlockSpec((tn, tks[p]), lambda i, j, k: (j, c0 + jnp.clip(k - starts[p], 0, nks[p] - 1)))

    res = _call(body, name, (m // tm, n // tn, ktot), [a_spec(p) for p in range(npc)] + [b_spec(p) for p in range(npc)],
                [pl.BlockSpec((tm, tn), lambda i, j, k: (i, j))], [jax.ShapeDtypeStruct((m, n), out_dtype)],
                [], ("parallel", "parallel", "arbitrary"), [a for a, _ in pieces] + [b] * npc, comm)
    return res[0] if comm is None else (res[0], res[1:])


def _rms_fwd(h, w, name):
    t, d = h.shape
    tr = _pick(t, (640, 512, 384, 256, 128))

    def body(h_ref, w_ref, u_ref):
        x = h_ref[...]
        r = lax.rsqrt(jnp.mean(x * x, axis=1, keepdims=True) + EPS)
        u_ref[...] = (x * r * w_ref[...]).astype(MXU_DTYPE)

    return pl.pallas_call(
        body, name=name, grid=(t // tr,),
        in_specs=[pl.BlockSpec((tr, d), lambda i: (i, 0)), pl.BlockSpec((1, d), lambda i: (0, 0))],
        out_specs=pl.BlockSpec((tr, d), lambda i: (i, 0)),
        out_shape=jax.ShapeDtypeStruct((t, d), MXU_DTYPE),
        compiler_params=_params(("parallel",)),
    )(h, w)


def _rms_bwd(h, w, du, res, name, comm=None):
    t, d = h.shape
    tr = _pick(t, (640, 512, 384, 256, 128))

    def body(h_ref, w_ref, du_ref, res_ref, dh_ref, gw_ref):
        @pl.when(pl.program_id(0) == 0)
        def _():
            gw_ref[...] = jnp.zeros_like(gw_ref)

        x = h_ref[...]
        r = lax.rsqrt(jnp.mean(x * x, axis=1, keepdims=True) + EPS)
        xhat = x * r
        dy = du_ref[...].astype(F32)
        dxh = dy * w_ref[...]
        dh = r * (dxh - xhat * jnp.mean(dxh * xhat, axis=1, keepdims=True))
        dh_ref[...] = dh + res_ref[...]
        gw_ref[...] += jnp.sum(dy * xhat, axis=0, keepdims=True)

    row = pl.BlockSpec((tr, d), lambda i: (i, 0))
    vec = pl.BlockSpec((1, d), lambda i: (0, 0))
    return _call(body, name, (t // tr,), [row, vec, row, row], [row, vec],
                 [jax.ShapeDtypeStruct((t, d), F32), jax.ShapeDtypeStruct((1, d), F32)], [], ("arbitrary",),
                 (h, w, du, res), comm)


def _loss_bwd(h2, w, target, name):
    t, d = h2.shape
    nc = t // CHUNK

    def body(h_ref, w_ref, tg_ref, loss_ref, dh_ref, gw_ref):
        i = pl.program_id(0)

        @pl.when(i == 0)
        def _():
            gw_ref[...] = jnp.zeros_like(gw_ref)
            loss_ref[...] = jnp.zeros_like(loss_ref)
            dh_ref[...] = jnp.zeros_like(dh_ref)

        @pl.when(i > 0)
        def _():
            x = h_ref[...]
            r = lax.rsqrt(jnp.mean(x * x, axis=1, keepdims=True) + EPS)
            xhat = x * r
            diff = xhat * w_ref[...] - tg_ref[...]
            loss_ref[...] += 0.5 * jnp.sum(jnp.sum(diff * diff, axis=1, keepdims=True), axis=0, keepdims=True) / d
            dy = diff / d
            dxh = dy * w_ref[...]
            dh_ref[...] = r * (dxh - xhat * jnp.mean(dxh * xhat, axis=1, keepdims=True))
            gw_ref[...] += jnp.sum(dy * xhat, axis=0, keepdims=True)

    row = pl.BlockSpec((CHUNK, d), lambda i: (i, 0))
    vec = pl.BlockSpec((1, d), lambda i: (0, 0))
    return pl.pallas_call(
        body, name=name, grid=(nc,),
        in_specs=[row, vec, pl.BlockSpec((CHUNK, d), lambda i: (jnp.maximum(i - 1, 0), 0))],
        out_specs=[pl.BlockSpec((SUBLANES, LANES), lambda i: (0, 0)), row, vec],
        out_shape=[jax.ShapeDtypeStruct((SUBLANES, LANES), F32), jax.ShapeDtypeStruct((t, d), F32),
                   jax.ShapeDtypeStruct((1, d), F32)],
        compiler_params=_params(("arbitrary",)),
    )(h2, w, target)


def _conv_tile(cur, prev8, w_ref, b_ref, kw):
    y = b_ref[...] + cur * w_ref[kw - 1:kw, :]
    for k in range(kw - 1):
        y = y + _shift_down(cur, prev8, kw - 1 - k) * w_ref[k:k + 1, :]
    return y


_SUB = 16


def _sub_rows(s):
    return pl.ds(0 if isinstance(s, int) else pl.multiple_of(s * _SUB, _SUB), _SUB)


def _window(x_ref, prev8, s):
    if isinstance(s, int):
        return jnp.concatenate([prev8, x_ref[0:_SUB, :]], axis=0)
    return x_ref[pl.ds(pl.multiple_of(s * _SUB - SUBLANES, SUBLANES), _SUB + SUBLANES), :]


def _conv_step(win, w, b, kw):
    taps = [win[SUBLANES:] if k == kw - 1 else pltpu.roll(win, kw - 1 - k, 0)[SUBLANES:] for k in range(kw)]
    y = b + taps[kw - 1] * w[kw - 1:kw, :]
    for k in range(kw - 1):
        y = y + taps[k] * w[k:k + 1, :]
    return y, taps


def _conv_dx_step(dpre, next8, w, kw):
    n = _SUB + SUBLANES
    win = jnp.concatenate([dpre, next8], axis=0)
    acc = dpre * w[kw - 1:kw, :]
    for k in range(kw - 1):
        acc = acc + pltpu.roll(win, n - (kw - 1 - k), 0)[0:_SUB] * w[k:k + 1, :]
    return acc


def _fold8(v):
    return functools.reduce(jnp.add, [v[r:r + SUBLANES] for r in range(0, _SUB, SUBLANES)])


def _row_tile(t):
    return _pick(t, (640, 512, 384, 256, 128))


def _ssd_conv_fwd(proj, col0, width, w, b, name):
    t = proj.shape[0]
    kw = w.shape[0]
    tr, tc = _row_tile(t), _pick(width, (512, 256, 128))
    c0, rb = col0 // tc, tr // SUBLANES
    assert col0 % tc == 0

    def body(x_ref, p_ref, w_ref, b_ref, o_ref):
        i = pl.program_id(1)
        prev8 = jnp.where(i > 0, p_ref[...], 0.0)
        pre = _conv_tile(x_ref[...], prev8, w_ref, b_ref, kw)
        rows = _iota((tr, 1), 0) + i * tr
        o_ref[...] = jnp.where(rows >= FRONT, _silu(pre), 0.0)

    return pl.pallas_call(
        body, name=name, grid=(width // tc, t // tr),
        in_specs=[pl.BlockSpec((tr, tc), lambda j, i: (i, c0 + j)),
                  pl.BlockSpec((SUBLANES, tc), lambda j, i: (jnp.maximum(i * rb - 1, 0), c0 + j)),
                  pl.BlockSpec((kw, tc), lambda j, i: (0, j)),
                  pl.BlockSpec((1, tc), lambda j, i: (0, j))],
        out_specs=pl.BlockSpec((tr, tc), lambda j, i: (i, j)),
        out_shape=jax.ShapeDtypeStruct((t, width), F32),
        compiler_params=_params(("parallel", "parallel")),
    )(proj, proj, w, b)


def _ssd_conv_bwd(proj, col0, width, w, b, dact, name, comm=None):
    t = proj.shape[0]
    kw = w.shape[0]
    tr, tc = _row_tile(t), _pick(width, (512, 256, 128))
    c0, rb, nrow = col0 // tc, tr // SUBLANES, t // tr

    def body(x_ref, p_ref, w_ref, b_ref, d_ref, o_ref, gw_ref, gb_ref, carry):
        i = pl.program_id(1)
        ti = nrow - 1 - i

        @pl.when(i == 0)
        def _():
            gw_ref[...] = jnp.zeros_like(gw_ref)
            gb_ref[...] = jnp.zeros_like(gb_ref)
            carry[...] = jnp.zeros_like(carry)

        w, b = w_ref[...], b_ref[...]
        prev8 = jnp.where(ti > 0, p_ref[...], 0.0)
        nsub = tr // _SUB

        def step(s, state):
            next8, gb8, gw8 = state
            pre, taps = _conv_step(_window(x_ref, prev8, s), w, b, kw)
            valid = _iota((_SUB, 1), 0) + (ti * tr + s * _SUB) >= FRONT
            dpre = jnp.where(valid, d_ref[_sub_rows(s), :] * _dsilu(pre), 0.0)
            o_ref[_sub_rows(s), :] = jnp.where(valid, _conv_dx_step(dpre, next8, w, kw), 0.0).astype(MXU_DTYPE)
            return (dpre[0:SUBLANES], gb8 + _fold8(dpre), tuple(g + _fold8(dpre * tp) for g, tp in zip(gw8, taps)))

        zero8 = jnp.zeros((SUBLANES, tc), F32)
        state = lax.fori_loop(0, nsub - 1, lambda n, st: step(nsub - 1 - n, st), (carry[...], zero8, (zero8,) * kw))
        next8, gb8, gw8 = step(0, state)
        carry[...] = next8
        gb_ref[...] += jnp.sum(gb8, axis=0, keepdims=True)
        for k in range(kw):
            gw_ref[k:k + 1, :] += jnp.sum(gw8[k], axis=0, keepdims=True)

    return _call(
        body, name, (width // tc, nrow),
        [pl.BlockSpec((tr, tc), lambda j, i: (nrow - 1 - i, c0 + j)),
         pl.BlockSpec((SUBLANES, tc), lambda j, i: (jnp.maximum((nrow - 1 - i) * rb - 1, 0), c0 + j)),
         pl.BlockSpec((kw, tc), lambda j, i: (0, j)),
         pl.BlockSpec((1, tc), lambda j, i: (0, j)),
         pl.BlockSpec((tr, tc), lambda j, i: (nrow - 1 - i, j))],
        [pl.BlockSpec((tr, tc), lambda j, i: (nrow - 1 - i, j)),
         pl.BlockSpec((SUBLANES, tc), lambda j, i: (0, j)),
         pl.BlockSpec((1, tc), lambda j, i: (0, j))],
        [jax.ShapeDtypeStruct((t, width), MXU_DTYPE), jax.ShapeDtypeStruct((SUBLANES, width), F32),
         jax.ShapeDtypeStruct((1, width), F32)],
        [pltpu.VMEM((SUBLANES, tc), F32)], ("parallel", "arbitrary"), (proj, proj, w, b, dact), comm)


def _ffn_conv_fwd(up, w, b, name):
    t, f2 = up.shape
    f = f2 // 2
    kw = w.shape[0]
    tr, tc = _row_tile(t), _pick(f, (256, 128))
    nf, rb = f // tc, tr // SUBLANES

    def body(xg, pg, xv, pv, wg, wv, bg, bv, o_ref):
        i = pl.program_id(1)
        ag = _conv_tile(xg[...], jnp.where(i > 0, pg[...], 0.0), wg, bg, kw)
        av = _conv_tile(xv[...], jnp.where(i > 0, pv[...], 0.0), wv, bv, kw)
        o_ref[...] = (_silu(ag) * av).astype(MXU_DTYPE)

    def cur(off):
        return pl.BlockSpec((tr, tc), lambda j, i: (i, j + off))

    def prev(off):
        return pl.BlockSpec((SUBLANES, tc), lambda j, i: (jnp.maximum(i * rb - 1, 0), j + off))

    def par(rows, off):
        return pl.BlockSpec((rows, tc), lambda j, i: (0, j + off))

    return pl.pallas_call(
        body, name=name, grid=(nf, t // tr),
        in_specs=[cur(0), prev(0), cur(nf), prev(nf), par(kw, 0), par(kw, nf), par(1, 0), par(1, nf)],
        out_specs=pl.BlockSpec((tr, tc), lambda j, i: (i, j)),
        out_shape=jax.ShapeDtypeStruct((t, f), MXU_DTYPE),
        compiler_params=_params(("parallel", "parallel")),
    )(up, up, up, up, w, w, b, b)


def _ffn_conv_bwd(up, w, b, dact, name):
    t, f2 = up.shape
    f = f2 // 2
    kw = w.shape[0]
    tr, tc = _row_tile(t), _pick(f, (256, 128))
    nf, rb, nrow = f // tc, tr // SUBLANES, t // tr

    def body(xg, pg, xv, pv, wg_ref, wv_ref, bg_ref, bv_ref, d_ref, og_ref, ov_ref, gwg_ref, gwv_ref, gbg_ref, gbv_ref,
             cg, cv):
        i = pl.program_id(1)
        ti = nrow - 1 - i

        @pl.when(i == 0)
        def _():
            for r in (gwg_ref, gwv_ref, gbg_ref, gbv_ref, cg, cv):
                r[...] = jnp.zeros_like(r)

        wg, wv, bg, bv = wg_ref[...], wv_ref[...], bg_ref[...], bv_ref[...]
        p8g, p8v = jnp.where(ti > 0, pg[...], 0.0), jnp.where(ti > 0, pv[...], 0.0)
        nsub = tr // _SUB

        def step(s, state):
            ng, nv, gbg8, gbv8, gwg8, gwv8 = state
            ag, tg = _conv_step(_window(xg, p8g, s), wg, bg, kw)
            av, tv = _conv_step(_window(xv, p8v, s), wv, bv, kw)
            d = d_ref[_sub_rows(s), :]
            sg = jax.nn.sigmoid(ag)
            dag = d * av * (sg * (1.0 + ag * (1.0 - sg)))
            dav = d * (ag * sg)
            valid = _iota((_SUB, 1), 0) + (ti * tr + s * _SUB) >= FRONT
            og_ref[_sub_rows(s), :] = jnp.where(valid, _conv_dx_step(dag, ng, wg, kw), 0.0).astype(MXU_DTYPE)
            ov_ref[_sub_rows(s), :] = jnp.where(valid, _conv_dx_step(dav, nv, wv, kw), 0.0).astype(MXU_DTYPE)
            return (dag[0:SUBLANES], dav[0:SUBLANES], gbg8 + _fold8(dag), gbv8 + _fold8(dav),
                    tuple(g + _fold8(dag * tp) for g, tp in zip(gwg8, tg)),
                    tuple(g + _fold8(dav * tp) for g, tp in zip(gwv8, tv)))

        zero8 = jnp.zeros((SUBLANES, tc), F32)
        state = lax.fori_loop(0, nsub - 1, lambda n, st: step(nsub - 1 - n, st),
                              (cg[...], cv[...], zero8, zero8, (zero8,) * kw, (zero8,) * kw))
        ng, nv, gbg8, gbv8, gwg8, gwv8 = step(0, state)
        cg[...] = ng
        cv[...] = nv
        gbg_ref[...] += jnp.sum(gbg8, axis=0, keepdims=True)
        gbv_ref[...] += jnp.sum(gbv8, axis=0, keepdims=True)
        for k in range(kw):
            gwg_ref[k:k + 1, :] += jnp.sum(gwg8[k], axis=0, keepdims=True)
            gwv_ref[k:k + 1, :] += jnp.sum(gwv8[k], axis=0, keepdims=True)

    def cur(off):
        return pl.BlockSpec((tr, tc), lambda j, i: (nrow - 1 - i, j + off))

    def prev(off):
        return pl.BlockSpec((SUBLANES, tc), lambda j, i: (jnp.maximum((nrow - 1 - i) * rb - 1, 0), j + off))

    def par(rows, off):
        return pl.BlockSpec((rows, tc), lambda j, i: (0, j + off))

    acc8 = pl.BlockSpec((SUBLANES, tc), lambda j, i: (0, j))
    acc1 = pl.BlockSpec((1, tc), lambda j, i: (0, j))
    return pl.pallas_call(
        body, name=name, grid=(nf, nrow),
        in_specs=[cur(0), prev(0), cur(nf), prev(nf), par(kw, 0), par(kw, nf), par(1, 0), par(1, nf), cur(0)],
        out_specs=[cur(0), cur(0), acc8, acc8, acc1, acc1],
        out_shape=[jax.ShapeDtypeStruct((t, f), MXU_DTYPE), jax.ShapeDtypeStruct((t, f), MXU_DTYPE),
                   jax.ShapeDtypeStruct((SUBLANES, f), F32), jax.ShapeDtypeStruct((SUBLANES, f), F32),
                   jax.ShapeDtypeStruct((1, f), F32), jax.ShapeDtypeStruct((1, f), F32)],
        scratch_shapes=[pltpu.VMEM((SUBLANES, tc), F32), pltpu.VMEM((SUBLANES, tc), F32)],
        compiler_params=_params(("parallel", "arbitrary")),
    )(up, up, up, up, w, w, b, b, dact)


def _gate_fwd(bs, br, proj, c_gs, c_gr, name):
    t, d = bs.shape
    tr = _row_tile(t)

    def body(bs_ref, br_ref, gs_ref, gr_ref, o_ref):
        o_ref[...] = (jax.nn.sigmoid(gs_ref[...]) * bs_ref[...] + jax.nn.sigmoid(gr_ref[...]) * br_ref[...]).astype(MXU_DTYPE)

    row = pl.BlockSpec((tr, d), lambda i: (i, 0))
    return pl.pallas_call(
        body, name=name, grid=(t // tr,),
        in_specs=[row, row, pl.BlockSpec((tr, d), lambda i: (i, c_gs // d)), pl.BlockSpec((tr, d), lambda i: (i, c_gr // d))],
        out_specs=row, out_shape=jax.ShapeDtypeStruct((t, d), MXU_DTYPE),
        compiler_params=_params(("parallel",)),
    )(bs, br, proj, proj)


def _gate_bwd(dm, bs, br, proj, c_gs, c_gr, name):
    t, d = bs.shape
    tr = _row_tile(t)

    def body(dm_ref, bs_ref, br_ref, gs_ref, gr_ref, dbs_ref, dbr_ref, dgg_ref):
        g = dm_ref[...]
        ss, sr = jax.nn.sigmoid(gs_ref[...]), jax.nn.sigmoid(gr_ref[...])
        dbs_ref[...] = (g * ss).astype(MXU_DTYPE)
        dbr_ref[...] = (g * sr).astype(MXU_DTYPE)
        dgg_ref[:, :d] = (g * bs_ref[...] * ss * (1.0 - ss)).astype(MXU_DTYPE)
        dgg_ref[:, d:] = (g * br_ref[...] * sr * (1.0 - sr)).astype(MXU_DTYPE)

    row = pl.BlockSpec((tr, d), lambda i: (i, 0))
    out = jax.ShapeDtypeStruct((t, d), MXU_DTYPE)
    return pl.pallas_call(
        body, name=name, grid=(t // tr,),
        in_specs=[row, row, row, pl.BlockSpec((tr, d), lambda i: (i, c_gs // d)), pl.BlockSpec((tr, d), lambda i: (i, c_gr // d))],
        out_specs=[row, row, pl.BlockSpec((tr, 2 * d), lambda i: (i, 0))],
        out_shape=[out, out, jax.ShapeDtypeStruct((t, 2 * d), MXU_DTYPE)],
        compiler_params=_params(("parallel",)),
    )(dm, bs, br, proj, proj)


def _ret_consts(h):
    lg = math.log(1.0 - 2.0 ** (-5.0 - h))
    l = _iota((CHUNK, 1), 0).astype(F32)
    diff = l - _iota((1, CHUNK), 1).astype(F32)
    dm = jnp.exp(jnp.where(diff >= 0, diff * lg, -jnp.inf))
    dmt = jnp.exp(jnp.where(diff <= 0, -diff * lg, -jnp.inf))
    cs = jnp.exp((l + 1.0) * lg)
    kdec = jnp.exp((CHUNK - 1.0 - l) * lg)
    return dm, dmt, cs, kdec, math.exp(CHUNK * lg)


def _ret_fwd(proj, c_q, c_k, c_v, c_g, cos, sin, d, name, comm=None):
    t = proj.shape[0]
    nc = t // CHUNK
    hq, hv = d // RET_HEADS, 2 * d // RET_HEADS
    half = hq // 2
    scale = hq ** -0.5

    def body(q_ref, k_ref, v_ref, g_ref, cos_ref, sin_ref, o_ref, y_ref, qr_ref, kr_ref, st_ref, rs):
        @pl.when(pl.program_id(0) == 0)
        def _():
            rs[...] = jnp.zeros_like(rs)

        co, si = cos_ref[...], sin_ref[...]
        for h in range(RET_HEADS):
            dm, _, cs, kdec, gam = _ret_consts(h)
            q1, q2 = q_ref[:, h * hq:h * hq + half], q_ref[:, h * hq + half:(h + 1) * hq]
            k1, k2 = k_ref[:, h * hq:h * hq + half], k_ref[:, h * hq + half:(h + 1) * hq]
            qr = jnp.concatenate([q1 * co - q2 * si, q2 * co + q1 * si], axis=1)
            kr = jnp.concatenate([k1 * co - k2 * si, k2 * co + k1 * si], axis=1) * scale
            qr_ref[:, h * hq:(h + 1) * hq] = qr.astype(MXU_DTYPE)
            kr_ref[:, h * hq:(h + 1) * hq] = kr.astype(MXU_DTYPE)
            v = v_ref[:, h * hv:(h + 1) * hv]
            r_in = rs[h * hq:(h + 1) * hq, :]
            st_ref[0, h * hq:(h + 1) * hq, :] = r_in.astype(MXU_DTYPE)
            s = _dot_nt(qr, kr) * dm
            o = _dot(s, v) + cs * _dot(qr, r_in)
            rs[h * hq:(h + 1) * hq, :] = gam * r_in + _dot_tn(kr * kdec, v)
            o_ref[:, h * hv:(h + 1) * hv] = o
            on = o * lax.rsqrt(jnp.mean(o * o, axis=1, keepdims=True) + EPS)
            y_ref[:, h * hv:(h + 1) * hv] = (_silu(g_ref[:, h * hv:(h + 1) * hv]) * on).astype(MXU_DTYPE)

    def col(width, c0):
        return pl.BlockSpec((CHUNK, width), lambda i: (i, c0 // width))

    tab = pl.BlockSpec((CHUNK, half), lambda i: (i, 0))
    return _call(
        body, name, (nc,),
        [col(d, c_q), col(d, c_k), col(2 * d, c_v), col(2 * d, c_g), tab, tab],
        [col(2 * d, 0), col(2 * d, 0), col(d, 0), col(d, 0), pl.BlockSpec((1, d, hv), lambda i: (i, 0, 0))],
        [jax.ShapeDtypeStruct((t, 2 * d), F32), jax.ShapeDtypeStruct((t, 2 * d), MXU_DTYPE),
         jax.ShapeDtypeStruct((t, d), MXU_DTYPE), jax.ShapeDtypeStruct((t, d), MXU_DTYPE),
         jax.ShapeDtypeStruct((nc, d, hv), MXU_DTYPE)],
        [pltpu.VMEM((d, hv), F32)], ("arbitrary",), (proj, proj, proj, proj, cos, sin), comm)


def _ret_bwd(dy, proj, c_v, c_g, o, qr, kr, st, cos, sin, d, name, comm=None):
    t = proj.shape[0]
    nc = t // CHUNK
    hq, hv = d // RET_HEADS, 2 * d // RET_HEADS
    half = hq // 2
    scale = hq ** -0.5

    def body(dy_ref, v_ref, g_ref, o_ref, qr_ref, kr_ref, st_ref, cos_ref, sin_ref, dqk_ref, dvg_ref, drs):
        dq_ref, dk_ref = dqk_ref.at[:, pl.ds(0, d)], dqk_ref.at[:, pl.ds(d, d)]
        dv_ref, dg_ref = dvg_ref.at[:, pl.ds(0, 2 * d)], dvg_ref.at[:, pl.ds(2 * d, 2 * d)]

        @pl.when(pl.program_id(0) == 0)
        def _():
            drs[...] = jnp.zeros_like(drs)

        co, si = cos_ref[...], sin_ref[...]
        for h in range(RET_HEADS):
            dm, dmt, cs, kdec, gam = _ret_consts(h)
            vs = slice(h * hv, (h + 1) * hv)
            qs = slice(h * hq, (h + 1) * hq)
            o_h = o_ref[:, vs]
            g_h = g_ref[:, vs]
            d_y = dy_ref[:, vs]
            r = lax.rsqrt(jnp.mean(o_h * o_h, axis=1, keepdims=True) + EPS)
            on = o_h * r
            d_on = d_y * _silu(g_h)
            dg_ref[:, vs] = (d_y * on * _dsilu(g_h)).astype(MXU_DTYPE)
            d_o = r * (d_on - on * jnp.mean(d_on * on, axis=1, keepdims=True))
            q_h, k_h, v_h = qr_ref[:, qs], kr_ref[:, qs], v_ref[:, vs]
            r_in = st_ref[0, qs, :]
            dr_n = drs[qs, :]
            csdo = cs * d_o
            ds = _dot_nt(d_o, v_h) * dm
            dst = _dot_nt(v_h, d_o) * dmt
            s_t = _dot_nt(k_h, q_h) * dmt
            dqr = _dot(ds, k_h) + _dot_nt(csdo, r_in)
            dkr = _dot(dst, q_h) + kdec * _dot_nt(v_h, dr_n)
            dv_ref[:, vs] = (_dot(s_t, d_o) + _dot(k_h.astype(F32) * kdec, dr_n)).astype(MXU_DTYPE)
            drs[qs, :] = gam * dr_n + _dot_tn(q_h, csdo)
            a1, a2 = dqr[:, :half], dqr[:, half:]
            dq_ref[:, qs] = jnp.concatenate([a1 * co + a2 * si, a2 * co - a1 * si], axis=1).astype(MXU_DTYPE)
            b1, b2 = dkr[:, :half] * scale, dkr[:, half:] * scale
            dk_ref[:, qs] = jnp.concatenate([b1 * co + b2 * si, b2 * co - b1 * si], axis=1).astype(MXU_DTYPE)

    def col(width, c0=0):
        return pl.BlockSpec((CHUNK, width), lambda i: (nc - 1 - i, c0 // width))

    tab = pl.BlockSpec((CHUNK, half), lambda i: (nc - 1 - i, 0))
    return _call(
        body, name, (nc,),
        [col(2 * d), col(2 * d, c_v), col(2 * d, c_g), col(2 * d), col(d), col(d),
         pl.BlockSpec((1, d, hv), lambda i: (nc - 1 - i, 0, 0)), tab, tab],
        [col(2 * d), col(4 * d)],
        [jax.ShapeDtypeStruct((t, 2 * d), MXU_DTYPE), jax.ShapeDtypeStruct((t, 4 * d), MXU_DTYPE)],
        [pltpu.VMEM((d, hv), F32)], ("arbitrary",), (dy, proj, proj, o, qr, kr, st, cos, sin), comm)


def _ssd_small(dtraw_ref, dtb_ref, alog_ref, chunk_idx, nh):
    rows = _iota((CHUNK, 1), 0)
    ok = ((rows >= FRONT) | (chunk_idx > 0)) & (_iota((1, LANES), 1) < nh)
    z = dtraw_ref[...] + dtb_ref[...]
    dt = jnp.where(ok, jax.nn.softplus(z), 0.0)
    sig = jnp.where(ok, jax.nn.sigmoid(z), 0.0)
    a = jnp.where(_iota((1, LANES), 1) < nh, -jnp.exp(alog_ref[...]), 0.0)
    tri = (_iota((CHUNK, CHUNK), 0) >= _iota((CHUNK, CHUNK), 1)).astype(F32)
    acs = _dot01(tri, dt * a, "b", 3)
    return dt, sig, a, acs, acs.T


def _head_expand(g, hpg, gw):
    shift = int(math.log2(SSD_HEAD_DIM))
    return (_iota((LANES, gw), 0) == g * hpg + lax.shift_right_logical(_iota((LANES, gw), 1), shift)).astype(F32)


def _ssd_fwd(xa, proj, c_dt, c_z, dtb, alog, dvec, nw, di, name, comm=None):
    t = xa.shape[0]
    nc = t // CHUNK
    nh = di // SSD_HEAD_DIM
    hpg = nh // SSD_GROUPS
    gw = di // SSD_GROUPS
    n = SSD_STATE
    gn = SSD_GROUPS * n
    hd = SSD_HEAD_DIM

    def body(x_ref, b_ref, c_ref, dtraw_ref, z_ref, dtb_ref, alog_ref, d_ref, nw_ref,
             y_ref, ys_ref, st_ref, hts, xdt_s):
        c = pl.program_id(0)

        @pl.when(c == 0)
        def _():
            hts[...] = jnp.zeros_like(hts)

        dt, _, _, acs, acs_t = _ssd_small(dtraw_ref, dtb_ref, alog_ref, c, nh)
        tri = _iota((CHUNK, CHUNK), 0) >= _iota((CHUNK, CHUNK), 1)
        dvec8 = jnp.broadcast_to(d_ref[...], (SUBLANES, LANES))
        for g in range(SSD_GROUPS):
            gs = slice(g * gw, (g + 1) * gw)
            ns = slice(g * n, (g + 1) * n)
            e_mat = _head_expand(g, hpg, gw)
            ax = _dot01(acs, e_mat, "a", 3)
            dtx = _dot01(dt, e_mat, "a", 3)
            dx = _dot01(dvec8, e_mat, "a", 3)[0:1, :]
            xg, bg, cg = x_ref[:, gs], b_ref[:, ns], c_ref[:, ns]
            xdt = xg * dtx
            xdt_s[...] = xdt.astype(MXU_DTYPE)
            cb = _dot_nt(cg, bg)
            ht = hts[ns, :]
            st_ref[0, ns, :] = ht.astype(MXU_DTYPE)
            y_ref[:, gs] = jnp.exp(ax) * _dot(cg, ht) + dx * xg
            for hh in range(hpg):
                h = g * hpg + hh
                lmat = jnp.exp(jnp.where(tri, acs[:, h:h + 1] - acs_t[h:h + 1, :], -jnp.inf))
                hs = slice(g * gw + hh * hd, g * gw + (hh + 1) * hd)
                y_ref[:, hs] += _dot(cb * lmat, xdt_s[:, hh * hd:(hh + 1) * hd])
            aend = ax[CHUNK - 1:CHUNK, :]
            hts[ns, :] = jnp.exp(aend) * ht + _dot_tn(bg, xdt * jnp.exp(aend - ax))
        for g in range(SSD_GROUPS):
            gs = slice(g * gw, (g + 1) * gw)
            yz = y_ref[:, gs] * _silu(z_ref[:, gs])
            r = lax.rsqrt(jnp.mean(yz * yz, axis=1, keepdims=True) + EPS)
            ys_ref[:, gs] = (yz * r * nw_ref[:, gs]).astype(MXU_DTYPE)

    def col(width, c0, arr_is_xa=False):
        return pl.BlockSpec((CHUNK, width), lambda i: (i, c0 // width))

    vec = pl.BlockSpec((1, LANES), lambda i: (0, 0))
    assert di % gn == 0 and c_dt % LANES == 0 and c_z % di == 0
    return _call(
        body, name, (nc,),
        [col(di, 0), col(gn, di), col(gn, di + gn), col(LANES, c_dt), col(di, c_z), vec, vec, vec,
         pl.BlockSpec((1, di), lambda i: (0, 0))],
        [col(di, 0), col(di, 0), pl.BlockSpec((1, gn, gw), lambda i: (i, 0, 0))],
        [jax.ShapeDtypeStruct((t, di), F32), jax.ShapeDtypeStruct((t, di), MXU_DTYPE),
         jax.ShapeDtypeStruct((nc, gn, gw), MXU_DTYPE)],
        [pltpu.VMEM((gn, gw), F32), pltpu.VMEM((CHUNK, gw), MXU_DTYPE)], ("arbitrary",),
        (xa, xa, xa, proj, proj, dtb, alog, dvec, nw), comm)


def _ssd_bwd(dys, xa, proj, c_dt, c_z, ypre, st, dtb, alog, dvec, nw, di, name, comm=None):
    t = xa.shape[0]
    nc = t // CHUNK
    nh = di // SSD_HEAD_DIM
    hpg = nh // SSD_GROUPS
    gw = di // SSD_GROUPS
    n = SSD_STATE
    gn = SSD_GROUPS * n
    hd = SSD_HEAD_DIM

    def body(dys_ref, x_ref, b_ref, c_ref, dtraw_ref, z_ref, y_ref, st_ref, dtb_ref, alog_ref, d_ref, nw_ref,
             dxa_ref, dz_ref, ddt_ref, gb_ref, ga_ref, gd_ref, gnw_ref, dhts, dy_s, xdt_s, dxdt_s):
        i = pl.program_id(0)
        c = nc - 1 - i

        @pl.when(i == 0)
        def _():
            dhts[...] = jnp.zeros_like(dhts)
            gb_ref[...] = jnp.zeros_like(gb_ref)
            ga_ref[...] = jnp.zeros_like(ga_ref)
            gd_ref[...] = jnp.zeros_like(gd_ref)
            gnw_ref[...] = jnp.zeros_like(gnw_ref)

        dt, sig, a, acs, acs_t = _ssd_small(dtraw_ref, dtb_ref, alog_ref, c, nh)
        tri = _iota((CHUNK, CHUNK), 0) >= _iota((CHUNK, CHUNK), 1)
        triu = _iota((CHUNK, CHUNK), 0) <= _iota((CHUNK, CHUNK), 1)
        lane = _iota((1, LANES), 1)
        rows = _iota((CHUNK, 1), 0)
        head_row = _iota((LANES, 1), 0)
        dvec8 = jnp.broadcast_to(d_ref[...], (SUBLANES, LANES))
        da = jnp.zeros((CHUNK, LANES), F32)
        da_t = jnp.zeros((LANES, CHUNK), F32)
        ddt = jnp.zeros((CHUNK, LANES), F32)
        gd = jnp.zeros((1, LANES), F32)
        for g in range(SSD_GROUPS):
            gs = slice(g * gw, (g + 1) * gw)
            ns = slice(g * n, (g + 1) * n)
            y_g, z_g = y_ref[:, gs], z_ref[:, gs]
            sz = _silu(z_g)
            yz = y_g * sz
            r = lax.rsqrt(jnp.mean(yz * yz, axis=1, keepdims=True) + EPS)
            nrm = yz * r
            dyo = dys_ref[:, gs]
            gnw_ref[:, gs] += jnp.sum(dyo * nrm, axis=0, keepdims=True)
            dn = dyo * nw_ref[:, gs]
            dyz = r * (dn - nrm * jnp.mean(dn * nrm, axis=1, keepdims=True))
            dz_ref[:, gs] = (dyz * y_g * _dsilu(z_g)).astype(MXU_DTYPE)
            dy_g = dyz * sz
            dy_s[...] = dy_g.astype(MXU_DTYPE)
            e_mat = _head_expand(g, hpg, gw)
            ax = _dot01(acs, e_mat, "a", 3)
            dtx = _dot01(dt, e_mat, "a", 3)
            dx = _dot01(dvec8, e_mat, "a", 3)[0:1, :]
            xg, bg, cg = x_ref[:, gs], b_ref[:, ns], c_ref[:, ns]
            xdt = xg * dtx
            xdt_s[...] = xdt.astype(MXU_DTYPE)
            aend = ax[CHUNK - 1:CHUNK, :]
            e = jnp.exp(aend - ax)
            ea = jnp.exp(ax)
            eend = jnp.exp(aend)
            htp = st_ref[0, ns, :].astype(F32)
            dht = dhts[ns, :]
            cb = _dot_nt(cg, bg)
            q = _dot(bg, dht)
            dxdt_s[...] = e * q
            wl = e * q * xdt
            d_b = _dot_nt(e * xdt, dht)
            yi = ea * _dot(cg, htp)
            eady = ea * dy_g
            d_c = _dot_nt(eady, htp)
            t1 = jnp.sum(dht * htp, axis=0, keepdims=True) * eend
            dhts[ns, :] = eend * dht + _dot_tn(cg, eady)
            da = da + _dot01(dy_g * yi - wl, e_mat, "a", 3, _NT)
            tail = jnp.broadcast_to(jnp.sum(wl, axis=0, keepdims=True) + t1, (SUBLANES, gw))
            da_end = _dot01(tail, e_mat, "a", 3, _NT)[0:1, :]
            da = da + jnp.where(rows == CHUNK - 1, da_end, 0.0)
            dcb = jnp.zeros((CHUNK, CHUNK), F32)
            for hh in range(hpg):
                h = g * hpg + hh
                lmat = jnp.exp(jnp.where(tri, acs[:, h:h + 1] - acs_t[h:h + 1, :], -jnp.inf))
                hl = slice(hh * hd, (hh + 1) * hd)
                dy_h, xdt_h = dy_s[:, hl], xdt_s[:, hl]
                dxdt_s[:, hl] += _dot_tn(cb * lmat, dy_h)
                dml = _dot_nt(dy_h, xdt_h) * lmat
                dcb = dcb + dml
                gmat = dml * cb
                da = da + jnp.where(lane == h, jnp.sum(gmat, axis=1, keepdims=True), 0.0)
                da_t = da_t - jnp.where(head_row == h, jnp.sum(gmat, axis=0, keepdims=True), 0.0)
            d_c = d_c + _dot(dcb, bg)
            d_b = d_b + _dot_tn(dcb, cg)
            dxdt = dxdt_s[...]
            dxa_ref[:, gs] = dxdt * dtx + dx * dy_g
            dxa_ref[:, di + g * n:di + (g + 1) * n] = d_b
            dxa_ref[:, di + gn + g * n:di + gn + (g + 1) * n] = d_c
            ddt = ddt + _dot01(dxdt * xg, e_mat, "a", 3, _NT)
            gd8 = jnp.broadcast_to(jnp.sum(dy_g * xg, axis=0, keepdims=True), (SUBLANES, gw))
            gd = gd + _dot01(gd8, e_mat, "a", 3, _NT)[0:1, :]
        da = da + da_t.T
        triu_f = triu.astype(F32)
        ddta = _dot01(triu_f, da, "b", 3)
        ddt = ddt + ddta * a
        draw = ddt * sig
        ddt_ref[...] = draw.astype(MXU_DTYPE)
        gb_ref[...] += jnp.sum(draw, axis=0, keepdims=True)
        ga_ref[...] += jnp.sum(ddta * dt, axis=0, keepdims=True) * a
        gd_ref[...] += gd

    def col(width, c0):
        return pl.BlockSpec((CHUNK, width), lambda i: (nc - 1 - i, c0 // width))

    vec = pl.BlockSpec((1, LANES), lambda i: (0, 0))
    wide = pl.BlockSpec((1, di), lambda i: (0, 0))
    wa = di + 2 * gn
    return _call(
        body, name, (nc,),
        [col(di, 0), col(di, 0), col(gn, di), col(gn, di + gn), col(LANES, c_dt), col(di, c_z), col(di, 0),
         pl.BlockSpec((1, gn, gw), lambda i: (nc - 1 - i, 0, 0)), vec, vec, vec, wide],
        [col(wa, 0), col(di, 0), col(LANES, 0), vec, vec, vec, wide],
        [jax.ShapeDtypeStruct((t, wa), F32), jax.ShapeDtypeStruct((t, di), MXU_DTYPE),
         jax.ShapeDtypeStruct((t, LANES), MXU_DTYPE), jax.ShapeDtypeStruct((1, LANES), F32),
         jax.ShapeDtypeStruct((1, LANES), F32), jax.ShapeDtypeStruct((1, LANES), F32),
         jax.ShapeDtypeStruct((1, di), F32)],
        [pltpu.VMEM((gn, gw), F32), pltpu.VMEM((CHUNK, gw), MXU_DTYPE), pltpu.VMEM((CHUNK, gw), MXU_DTYPE),
         pltpu.VMEM((CHUNK, gw), F32)], ("arbitrary",),
        (dys, xa, xa, xa, proj, proj, ypre, st, dtb, alog, dvec, nw), comm)


def _adam_math(w, g, m, v):
    m2 = ADAM_B1 * m + (1.0 - ADAM_B1) * g
    v2 = ADAM_B2 * v + (1.0 - ADAM_B2) * (g * g)
    m_hat = m2 / (1.0 - ADAM_B1 ** ADAM_STEP)
    v_hat = v2 / (1.0 - ADAM_B2 ** ADAM_STEP)
    delta = -ADAM_LR * (m_hat / (jnp.sqrt(v_hat) + ADAM_EPS) + ADAM_WD * w)
    return delta, m2, v2


def _adam_big(w, g_mine, g_sib, m, v, core, name):
    r, c = w.shape
    h = r // 2
    tr = _pick(h, (128, 64, 32, 16, 8))
    nbh = h // tr

    def body(core_ref, w_ref, a_ref, b_ref, m_ref, v_ref, g_ref, d_ref, m2_ref, v2_ref):
        g = jnp.where(pl.program_id(0) // nbh == core_ref[0], a_ref[...], b_ref[...])
        delta, m2, v2 = _adam_math(w_ref[...], g, m_ref[...], v_ref[...])
        g_ref[...] = g
        d_ref[...] = delta
        m2_ref[...] = m2
        v2_ref[...] = v2

    blk = pl.BlockSpec((tr, c), lambda i, core_ref: (i, 0))
    hblk = pl.BlockSpec((tr, c), lambda i, core_ref: (i % nbh, 0))
    out = jax.ShapeDtypeStruct((r, c), F32)
    return pl.pallas_call(
        body, name=name,
        grid_spec=pltpu.PrefetchScalarGridSpec(num_scalar_prefetch=1, grid=(2 * nbh,),
                                               in_specs=[blk, hblk, hblk, blk, blk], out_specs=[blk] * 4),
        out_shape=[out] * 4, compiler_params=_params(("parallel",)),
    )(core, w, g_mine, g_sib, m, v)


def _pair_sum(g, sib, core, name):
    _, r, c = g.shape
    h = r // 2
    tr = _pick(h, (128, 64, 32, 16))
    nb = h // tr

    def body(core_ref, g_ref, s_ref, o_ref):
        o_ref[...] = (g_ref[...].astype(F32) + s_ref[...].astype(F32)).astype(WIRE_DTYPE)

    return pl.pallas_call(
        body, name=name,
        grid_spec=pltpu.PrefetchScalarGridSpec(
            num_scalar_prefetch=1, grid=(4, nb),
            in_specs=[pl.BlockSpec((1, tr, c), lambda j, i, core_ref: (j, core_ref[0] * nb + i, 0)),
                      pl.BlockSpec((1, tr, c), lambda j, i, core_ref: (j, i, 0))],
            out_specs=pl.BlockSpec((1, tr, c), lambda j, i, core_ref: (j, i, 0))),
        out_shape=jax.ShapeDtypeStruct((4, h, c), WIRE_DTYPE), compiler_params=_params(("parallel", "parallel")),
    )(core, g, sib)


def _sum4(parts, name):
    _, r, c = parts.shape
    tr = _pick(r, (128, 64, 32, 16, 8))

    def body(p_ref, o_ref):
        acc = p_ref[0].astype(F32)
        for j in range(1, 4):
            acc = acc + p_ref[j].astype(F32)
        o_ref[...] = acc

    return pl.pallas_call(
        body, name=name, grid=(r // tr,),
        in_specs=[pl.BlockSpec((4, tr, c), lambda i: (0, i, 0))],
        out_specs=pl.BlockSpec((tr, c), lambda i: (i, 0)),
        out_shape=jax.ShapeDtypeStruct((r, c), F32),
        compiler_params=_params(("parallel",)),
    )(parts)


def _adam_small(items, chip, name):
    n = len(items) - 1

    def total(g_ref, r, c):
        acc = g_ref[0, 0:r, 0:c]
        for j in range(1, 8):
            acc = acc + g_ref[j, 0:r, 0:c]
        return acc

    def body(chip_ref, *refs):
        g_refs, wmv, outs = refs[:n + 1], refs[n + 1:4 * n + 1], refs[4 * n + 1:]
        for p in range(n):
            r, c = items[p][1].shape
            g = total(g_refs[p], r, c)
            w_ref, m_ref, v_ref = wmv[3 * p:3 * p + 3]
            delta, m2, v2 = _adam_math(w_ref[...], g, m_ref[...], v_ref[...])
            for o_ref, val in zip(outs[4 * p:4 * p + 4], (g, delta, m2, v2)):
                o_ref[...] = val
        outs[4 * n][...] = total(g_refs[n], 1, 1)

    def full(shape):
        return pl.BlockSpec(shape, lambda i, chip_ref: (0,) * len(shape))

    g_specs, args, out_specs, out_shapes = [], [], [], []
    for g, w, m, v, sharded in items[:n]:
        if sharded:
            g_specs.append(pl.BlockSpec((8, g.shape[1], w.shape[1]), lambda i, chip_ref: (0, 0, chip_ref[0])))
        else:
            g_specs.append(full(g.shape))
        args += [w, m, v]
        out_specs += [full(w.shape)] * 4
        out_shapes += [jax.ShapeDtypeStruct(w.shape, F32)] * 4
    g_specs.append(full(items[n][0].shape))
    return pl.pallas_call(
        body, name=name,
        grid_spec=pltpu.PrefetchScalarGridSpec(
            num_scalar_prefetch=1, grid=(1,), in_specs=g_specs + [full(a.shape) for a in args],
            out_specs=out_specs + [full((1, 1))]),
        out_shape=out_shapes + [jax.ShapeDtypeStruct((1, 1), F32)],
    )(chip, *[it[0] for it in items], *args)


def _gather8(arrays):
    def make(ins, outs, send, recv, loc):
        x, y, c, _, _ = _place()
        mine = 4 * x + 2 * y + c
        cps = []
        for i in range(len(arrays)):
            cps.append(pltpu.make_async_copy(ins[i], outs[i].at[mine], loc.at[i]))
            for k in range(1, 8):
                fx, fy, fc = (k >> 2) & 1, (k >> 1) & 1, k & 1
                peer = (1 - x if fx else x, 1 - y if fy else y, 1 - c if fc else c)
                cps.append(pltpu.make_async_remote_copy(
                    src_ref=ins[i], dst_ref=outs[i].at[mine], send_sem=send.at[7 * i + k - 1],
                    recv_sem=recv.at[7 * i + k - 1], device_id=peer, device_id_type=MESH))
        return cps

    outs = [jax.ShapeDtypeStruct((8,) + a.shape, a.dtype) for a in arrays]
    return _Comm(arrays, outs, 7 * len(arrays), make)


def _pack(parts):
    flat = jnp.concatenate([p.reshape(-1).astype(F32) for p in parts])
    pad = (-flat.shape[0]) % (32 * LANES)
    return jnp.pad(flat, (0, pad)).reshape(-1, LANES)


def _unpack(slab, shapes):
    flat = slab.reshape(-1)
    out, off = [], 0
    for s in shapes:
        size = int(np.prod(s))
        out.append(flat[off:off + size].reshape(s))
        off += size
    return out


def _pad_lanes(v):
    return jnp.pad(v.reshape(1, -1), ((0, 0), (0, LANES - v.shape[-1])))


def kernel(x, meta_tokens, mix_norm_w, w_in, ssd_conv_w, ssd_conv_b, ssd_dt_bias, ssd_A_log, ssd_D, ssd_norm_w, w_branch_ssd, w_branch_ret, w_out, ffn_norm_w, w_up, ffn_conv_w, ffn_conv_b, w_down, final_norm_w, loss_target, m_meta_tokens, m_mix_norm_w, m_w_in, m_ssd_conv_w, m_ssd_conv_b, m_ssd_dt_bias, m_ssd_A_log, m_ssd_D, m_ssd_norm_w, m_w_branch_ssd, m_w_branch_ret, m_w_out, m_ffn_norm_w, m_w_up, m_ffn_conv_w, m_ffn_conv_b, m_w_down, m_final_norm_w, v_meta_tokens, v_mix_norm_w, v_w_in, v_ssd_conv_w, v_ssd_conv_b, v_ssd_dt_bias, v_ssd_A_log, v_ssd_D, v_ssd_norm_w, v_w_branch_ssd, v_w_branch_ret, v_w_out, v_ffn_norm_w, v_w_up, v_ffn_conv_w, v_ffn_conv_b, v_w_down, v_final_norm_w):
    seq, d = x.shape[1], x.shape[2]
    t = CHUNK + seq
    di = 2 * d
    nh = di // SSD_HEAD_DIM
    gn = SSD_GROUPS * SSD_STATE
    cw = di + 2 * gn
    f = w_down.shape[1] * 4
    chip = 2 * lax.axis_index("x") + lax.axis_index("y")

    order = [("z", di), ("v", di), ("g", di), ("xbc", cw), ("q", d), ("k", d), ("gs", d), ("gr", d), ("dt", LANES)]
    col, acc = {}, 0
    for nm, wd in order:
        col[nm] = acc
        acc += wd
    wp = acc
    ref_order = [("z", di), ("xbc", cw), ("dt", nh), ("q", d), ("k", d), ("v", di), ("g", di), ("gs", d), ("gr", d)]
    ref_off, acc = {}, 0
    for nm, wd in ref_order:
        ref_off[nm] = (acc, wd)
        acc += wd
    in_dim = acc

    core = lax.axis_index("c").astype(jnp.int32).reshape(1)
    small_shapes = [meta_tokens.shape, ssd_conv_w.shape[1:], ffn_conv_w.shape[1:]]
    small_local = _pack([meta_tokens, ssd_conv_w[0], ffn_conv_w[0]])
    first_local = [w_in[0].astype(WIRE_DTYPE), small_local]
    first_half = _run_comm(_gather_ici(first_local), "gather_w_in_ici")
    g_in, g_small = [_with_own(g, own, chip)
                     for g, own in zip(_run_comm(_gather_d2d(first_half), "gather_w_in_d2d"), first_local)]
    rest_local = [a[0].astype(WIRE_DTYPE) for a in (w_branch_ssd, w_branch_ret, w_out, w_up, w_down)]
    w_in_full = jnp.moveaxis(g_in, 0, 1).reshape(d, in_dim)
    pieces = []
    for nm, wd in order:
        o, rw = ref_off[nm]
        p = w_in_full[:, o:o + rw]
        if rw < wd:
            p = jnp.pad(p, ((0, 0), (0, wd - rw)))
        pieces.append(p)
    w_p = jnp.concatenate(pieces, axis=1)
    smalls = [_unpack(g_small[j], small_shapes) for j in range(4)]
    meta_full = jnp.concatenate([s[0] for s in smalls], axis=1)
    scw = jnp.concatenate([s[1] for s in smalls], axis=1)
    fcw = jnp.concatenate([s[2] for s in smalls], axis=1)
    scb, fcb = ssd_conv_b, ffn_conv_b
    dtb, alog, dvec = _pad_lanes(ssd_dt_bias), _pad_lanes(ssd_A_log), _pad_lanes(ssd_D)
    fin_w = final_norm_w.reshape(1, d)

    hq = d // RET_HEADS
    pos = jnp.arange(t, dtype=F32) - FRONT
    inv_freq = ROPE_BASE ** (-jnp.linspace(0.0, 1.0, hq // 2, dtype=F32))
    ang = pos[:, None] * inv_freq[None, :]
    cos, sin = jnp.cos(ang), jnp.sin(ang)

    h0 = jnp.concatenate([jnp.zeros((FRONT, d), F32), meta_full, x[0]], axis=0)
    tm = _row_tile(t)
    tmb = _pick(t, (1664, 1280, 640, 512, 384, 256, 128))
    u1 = _rms_fwd(h0, mix_norm_w, "rms1_fwd")
    proj = _mm(u1, w_p, "nn", F32, "proj", tmb, _pick(wp, (1920, 1536, 1280, 1024, 896, 768, 640, 512, 384, 256, 128)), d)
    xa = _ssd_conv_fwd(proj, col["xbc"], cw, scw, scb, "ssd_conv_fwd")
    res = _ssd_fwd(xa, proj, col["dt"], col["z"], dtb, alog, dvec, ssd_norm_w, di, "ssd_fwd", comm=_gather_ici(rest_local))
    (ypre, yssd, st_ssd), rest_half = res[:3], res[3:]
    res = _ret_fwd(proj, col["q"], col["k"], col["v"], col["g"], cos, sin, d, "ret_fwd", comm=_gather_d2d(rest_half))
    o_ret, yret, qr, kr, st_ret = res[:5]
    g_bs, g_br, g_out, g_up, g_down = [_with_own(g, own, chip) for g, own in zip(res[5:], rest_local)]
    w_bs = g_bs.reshape(di, d)
    w_br = g_br.reshape(di, d)
    w_o = g_out.reshape(d, d)
    w_u = jnp.moveaxis(g_up, 0, 1).reshape(d, 2 * f)
    w_d = g_down.reshape(f, d)
    tn_d = _pick(d, (1024, 512, 256, 128))
    bs = _mm(yssd, w_bs, "nn", F32, "branch_ssd", tmb, tn_d, _pick(di, (1024, 512, 256)))
    br = _mm(yret, w_br, "nn", F32, "branch_ret", tmb, tn_d, _pick(di, (1024, 512, 256)))
    merged = _gate_fwd(bs, br, proj, col["gs"], col["gr"], "gate_fwd")
    h1 = _mm(merged, w_o, "nn", F32, "out_proj", tmb, tn_d, d, res=h0)
    u2 = _rms_fwd(h1, ffn_norm_w, "rms2_fwd")
    tn_f = _pick(2 * f, (1408, 1024, 768, 512, 256, 128))
    up = _mm(u2, w_u, "nn", F32, "up_proj", tmb, tn_f, d)
    act = _ffn_conv_fwd(up, fcw, fcb, "ffn_conv_fwd")
    tk_f = _pick(f, (1408, 768, 704, 512, 256, 128))
    h2 = _mm(act, w_d, "nn", F32, "down_proj", tm, tn_d, tk_f, res=h1)
    loss8, d_h2, g_fin = _loss_bwd(h2, fin_w, loss_target[0], "loss_head")

    tkt = _pick(t, (1664, 1280, 1024, 640, 512, 384, 256, 128))
    d_act = _mm(d_h2, w_d, "nt", F32, "d_act", tmb, tk_f, d)
    g_wd = _mm(act, d_h2, "tn", F32, "g_w_down", tk_f, tn_d, tkt)
    d_upg, d_upv, g_fcwg, g_fcwv, g_fcbg, g_fcbv = _ffn_conv_bwd(up, fcw, fcb, d_act, "ffn_conv_bwd")
    g_fcw = jnp.concatenate([g_fcwg, g_fcwv], axis=1)
    g_fcb = jnp.concatenate([g_fcbg, g_fcbv], axis=1)
    d_u2 = _mm(d_upg, w_u[:, :f], "nt", F32, "d_u2_gate", tm, tn_d, tk_f)
    d_u2 = _mm(d_upv, w_u[:, f:], "nt", F32, "d_u2_value", tm, tn_d, tk_f, res=d_u2)
    g_wu = jnp.concatenate([_mm(u2, d_upg, "tn", F32, "g_w_up_gate", tn_d, tk_f, tkt),
                            _mm(u2, d_upv, "tn", F32, "g_w_up_value", tn_d, tk_f, tkt)], axis=1)
    d_h1, g_ffnw = _rms_bwd(h1, ffn_norm_w, d_u2, d_h2, "rms2_bwd")
    d_merged = _mm(d_h1, w_o, "nt", F32, "d_merged", tmb, tn_d, d)
    g_wo = _mm(merged, d_h1, "tn", F32, "g_w_out", tn_d, tn_d, tkt)
    d_bs, d_br, d_gsr = _gate_bwd(d_merged, bs, br, proj, col["gs"], col["gr"], "gate_bwd")
    tk_i = _pick(di, (1024, 512, 256))
    d_yssd = _mm(d_bs, w_bs, "nt", F32, "d_y_ssd", tmb, tk_i, d)
    g_wbs = _mm(yssd, d_bs, "tn", F32, "g_w_branch_ssd", tk_i, tn_d, tkt)
    d_yret = _mm(d_br, w_br, "nt", F32, "d_y_ret", tmb, tk_i, d)
    g_wbr = _mm(yret, d_br, "tn", F32, "g_w_branch_ret", tk_i, tn_d, tkt)

    early_names = ["w_branch_ssd", "w_branch_ret", "w_out", "w_up", "w_down"]
    early = [g_wbs.reshape(4, di // 4, d), g_wbr.reshape(4, di // 4, d), g_wo.reshape(4, d // 4, d),
             jnp.moveaxis(g_wu.reshape(d, 4, 2 * f // 4), 1, 0), g_wd.reshape(4, f // 4, d)]
    res = _ret_bwd(d_yret, proj, col["v"], col["g"], o_ret, qr, kr, st_ret, cos, sin, d, "ret_bwd", comm=_scatter_d2d(early))
    (dqk, dvg), early_sib = res[:2], res[2:]
    early_pair = [_pair_sum(g_, s_, core, "pair_" + nm) for g_, s_, nm in zip(early, early_sib, early_names)]
    res = _ssd_bwd(d_yssd, xa, proj, col["dt"], col["z"], ypre, st_ssd, dtb, alog, dvec, ssd_norm_w, di, "ssd_bwd",
                   comm=_scatter_ici(early_pair))
    (d_xa, dz, ddt, g_dtb, g_alog, g_dvec, g_snw), early_recv = res[:7], res[7:]
    early_mine = [_sum4(p, "sum4_" + nm) for p, nm in zip(early_recv, early_names)]
    res = _ssd_conv_bwd(proj, col["xbc"], cw, scw, scb, d_xa, "ssd_conv_bwd", comm=_sibling_swap(early_mine))
    (d_xbc, g_scw, g_scb), early_other = res[:3], res[3:]
    d_pieces = [("z", dz), ("v", dvg), ("xbc", d_xbc), ("q", dqk), ("gs", d_gsr), ("dt", ddt)]

    g_piece = {nm: _mm(u1, a, "tn", WIRE_DTYPE, "g_w_in_" + nm, tn_d, _pick(a.shape[1], (1024, 768, 512, 256, 128)), tkt)
               for nm, a in d_pieces}
    g_cols = dict(z=g_piece["z"], v=g_piece["v"][:, :di], g=g_piece["v"][:, di:], xbc=g_piece["xbc"],
                  q=g_piece["q"][:, :d], k=g_piece["q"][:, d:], gs=g_piece["gs"][:, :d], gr=g_piece["gs"][:, d:],
                  dt=g_piece["dt"])
    g_in_ref = jnp.concatenate([g_cols[nm][:, :rw] for nm, rw in ref_order], axis=1)
    sc_in = jnp.moveaxis(g_in_ref.reshape(d, 4, in_dim // 4), 1, 0)
    in_sib = _run_comm(_scatter_d2d([sc_in]), "scatter_w_in_d2d")[0]
    in_pair = _pair_sum(sc_in, in_sib, core, "pair_w_in")
    d_u1, (in_recv,) = _mm_pieces_nt([(a, col[nm]) for nm, a in d_pieces], w_p, "d_u1", tmb, tn_d, 512,
                                     comm=_scatter_ici([in_pair]))
    in_mine = _sum4(in_recv, "sum4_w_in")
    d_h0, g_mixw, in_other = _rms_bwd(h0, mix_norm_w, d_u1, d_h1, "rms1_bwd", comm=_sibling_swap([in_mine]))
    grad_x = d_h0[CHUNK:][None]
    g_meta = d_h0[FRONT:CHUNK]

    names = ["w_in"] + early_names
    mine_half = [in_mine] + early_mine
    other_half = [in_other] + list(early_other)
    big_w = [w_in, w_branch_ssd, w_branch_ret, w_out, w_up, w_down]
    big_m = [m_w_in, m_w_branch_ssd, m_w_branch_ret, m_w_out, m_w_up, m_w_down]
    big_v = [v_w_in, v_w_branch_ssd, v_w_branch_ret, v_w_out, v_w_up, v_w_down]
    big_out = {}
    for nm, w_, p_, s_, m_, v_ in zip(names, big_w, mine_half, other_half, big_m, big_v):
        res = _adam_big(w_[0], p_, s_, m_[0], v_[0], core, "adam_" + nm)
        big_out[nm] = [r[None] for r in res]

    small = [
        ("meta_tokens", g_meta, meta_tokens, m_meta_tokens, v_meta_tokens, True),
        ("mix_norm_w", g_mixw, mix_norm_w, m_mix_norm_w, v_mix_norm_w, False),
        ("ssd_conv_w", g_scw, ssd_conv_w[0], m_ssd_conv_w[0], v_ssd_conv_w[0], True),
        ("ssd_conv_b", g_scb, ssd_conv_b, m_ssd_conv_b, v_ssd_conv_b, False),
        ("ssd_dt_bias", g_dtb, ssd_dt_bias, m_ssd_dt_bias, v_ssd_dt_bias, False),
        ("ssd_A_log", g_alog, ssd_A_log, m_ssd_A_log, v_ssd_A_log, False),
        ("ssd_D", g_dvec, ssd_D, m_ssd_D, v_ssd_D, False),
        ("ssd_norm_w", g_snw, ssd_norm_w, m_ssd_norm_w, v_ssd_norm_w, False),
        ("ffn_norm_w", g_ffnw, ffn_norm_w, m_ffn_norm_w, v_ffn_norm_w, False),
        ("ffn_conv_w", g_fcw, ffn_conv_w[0], m_ffn_conv_w[0], v_ffn_conv_w[0], True),
        ("ffn_conv_b", g_fcb, ffn_conv_b, m_ffn_conv_b, v_ffn_conv_b, False),
        ("final_norm_w", g_fin, fin_w, m_final_norm_w.reshape(1, d), v_final_norm_w.reshape(1, d), False),
    ]
    gathered8 = _run_comm(_gather8([s[1] for s in small] + [loss8]), "gather_small_grads")
    items = [(g8,) + s[2:] for g8, s in zip(gathered8, small)] + [(gathered8[-1], None, None, None, False)]
    small_res = _adam_small(items, chip.astype(jnp.int32).reshape(1), "adam_small")
    loss = small_res[-1].reshape(())
    out_shape = dict(meta_tokens=meta_tokens.shape, ssd_conv_w=ssd_conv_w.shape, ffn_conv_w=ffn_conv_w.shape,
                     final_norm_w=final_norm_w.shape)
    small_out = {s[0]: [r.reshape(out_shape.get(s[0], r.shape)) for r in small_res[4 * p:4 * p + 4]]
                 for p, s in enumerate(small)}

    weights = ["meta_tokens", "mix_norm_w", "w_in", "ssd_conv_w", "ssd_conv_b", "ssd_dt_bias", "ssd_A_log", "ssd_D",
               "ssd_norm_w", "w_branch_ssd", "w_branch_ret", "w_out", "ffn_norm_w", "w_up", "ffn_conv_w", "ffn_conv_b",
               "w_down", "final_norm_w"]
    outs = [loss, grad_x]
    for kind in range(4):
        for nm in weights:
            outs.append(big_out[nm][kind] if nm in big_out else small_out[nm][kind])
    return tuple(outs)
```

```python
import functools
import math

import jax
import jax.numpy as jnp
import numpy as np
from jax import lax
from jax.experimental import pallas as pl
from jax.experimental.pallas import tpu as pltpu

F32 = jnp.float32
BF16 = jnp.bfloat16
MXU_DTYPE = BF16
WIRE_DTYPE = BF16

N_META = 16
CHUNK = 128
FRONT = CHUNK - N_META
EPS = 1e-6
SSD_HEAD_DIM = 64
SSD_GROUPS = 4
SSD_STATE = 128
SSD_CONV = 4
RET_HEADS = 4
ROPE_BASE = 10000.0
FFN_CONV = 3
LANES = 128
SUBLANES = 8
VMEM_LIMIT = 56 * 1024 * 1024

ADAM_LR = 0.001
ADAM_B1 = 0.9
ADAM_B2 = 0.999
ADAM_EPS = 1e-08
ADAM_WD = 0.01
ADAM_STEP = 10
MESH = pl.DeviceIdType.MESH


def _params(sem=None, vmem=VMEM_LIMIT):
    return pltpu.CompilerParams(dimension_semantics=sem, vmem_limit_bytes=vmem)


def _pick(n, cands):
    for c in cands:
        if n % c == 0:
            return c
    return n


def _silu(x):
    return x * jax.nn.sigmoid(x)


def _dsilu(x):
    s = jax.nn.sigmoid(x)
    return s * (1.0 + x * (1.0 - s))


def _dot(a, b, dims=(((1,), (0,)), ((), ()))):
    return lax.dot_general(a.astype(MXU_DTYPE), b.astype(MXU_DTYPE), dims, preferred_element_type=F32)


def _dot_nt(a, b):
    return _dot(a, b, (((1,), (1,)), ((), ())))


def _dot_tn(a, b):
    return _dot(a, b, (((0,), (0,)), ((), ())))


def _dot01(a, b, split, npass, dims=(((1,), (0,)), ((), ()))):
    rest = (a if split == "a" else b).astype(F32)
    fixed = (b if split == "a" else a).astype(BF16)
    acc = None
    for p in range(npass):
        piece = rest.astype(BF16)
        ops = (piece, fixed) if split == "a" else (fixed, piece)
        term = lax.dot_general(ops[0], ops[1], dims, preferred_element_type=F32)
        acc = term if acc is None else acc + term
        if p + 1 < npass:
            rest = rest - piece.astype(F32)
    return acc


_NT = (((1,), (1,)), ((), ()))


def _iota(shape, dim):
    return lax.broadcasted_iota(jnp.int32, shape, dim)


def _shift_down(cur, prev8, k):
    if k == 0:
        return cur
    rolled = pltpu.roll(cur, k, 0)
    i8 = _iota((SUBLANES, cur.shape[1]), 0)
    head = jnp.where(i8 < k, pltpu.roll(prev8, k, 0), rolled[0:SUBLANES])
    return jnp.concatenate([head, rolled[SUBLANES:]], axis=0)


class _Comm:
    def __init__(self, ins, outs, nsem, make, in_place=False):
        self.ins, self.outs, self.nsem, self.make = list(ins), list(outs), nsem, make
        self.in_place = in_place


def _place():
    x, y, c = lax.axis_index("x"), lax.axis_index("y"), lax.axis_index("c")
    return x, y, c, 2 * x + y, [(1 - x, y), (x, 1 - y), (1 - x, 1 - y)]


def _call(body, name, grid, in_specs, out_specs, out_shape, scratch, sem, args, comm=None):
    if comm is None:
        return pl.pallas_call(body, name=name, grid=grid, in_specs=in_specs, out_specs=out_specs, out_shape=out_shape,
                              scratch_shapes=scratch, compiler_params=_params(sem))(*args)
    n_in, n_out, n_scr = len(in_specs), len(out_specs), len(scratch)
    ci, co = len(comm.ins), len(comm.outs)

    def wrapped(*refs):
        ins, refs = refs[:n_in], refs[n_in:]
        cins, refs = refs[:ci], refs[ci:]
        outs, refs = refs[:n_out], refs[n_out:]
        couts, refs = refs[:co], refs[co:]
        scr, sems = refs[:n_scr], refs[n_scr:]
        first = functools.reduce(jnp.logical_and, [pl.program_id(a) == 0 for a in range(len(grid))])
        last = functools.reduce(jnp.logical_and, [pl.program_id(a) == grid[a] - 1 for a in range(len(grid))])

        @pl.when(first)
        def _():
            for cp in comm.make(cins, couts, *sems):
                cp.start()

        body(*ins, *outs, *scr)

        @pl.when(last)
        def _():
            for cp in comm.make(cins, couts, *sems):
                cp.wait()

    anyspec = pl.BlockSpec(memory_space=pl.ANY)
    dma = pltpu.SemaphoreType.DMA((comm.nsem,))
    aliases = {n_in + i: n_out + i for i in range(ci)} if comm.in_place else {}
    return pl.pallas_call(
        wrapped, name=name, grid=grid, in_specs=list(in_specs) + [anyspec] * ci,
        out_specs=list(out_specs) + [anyspec] * co, out_shape=list(out_shape) + comm.outs,
        scratch_shapes=list(scratch) + [dma, dma, dma], input_output_aliases=aliases,
        compiler_params=_params(("arbitrary",) * len(grid)))(*args, *comm.ins)


def _run_comm(comm, name):
    ci, co = len(comm.ins), len(comm.outs)

    def body(*refs):
        cins, couts, sems = refs[:ci], refs[ci:ci + co], refs[ci + co:]
        for cp in comm.make(cins, couts, *sems):
            cp.start()
        for cp in comm.make(cins, couts, *sems):
            cp.wait()

    anyspec = pl.BlockSpec(memory_space=pl.ANY)
    dma = pltpu.SemaphoreType.DMA((comm.nsem,))
    aliases = {i: i for i in range(ci)} if comm.in_place else {}
    return pl.pallas_call(body, name=name, in_specs=[anyspec] * ci, out_specs=[anyspec] * co, out_shape=comm.outs,
                          scratch_shapes=[dma, dma, dma], input_output_aliases=aliases)(*comm.ins)


def _half_rows(c, rows):
    h = rows // 2
    return pl.ds(pl.multiple_of(c * h, 16), h)


def _gather_ici(arrays):
    for a in arrays:
        assert a.shape[0] % 32 == 0, a.shape

    def make(ins, outs, send, recv, loc):
        x, y, c, mine, peers = _place()
        cps = []
        for i, a in enumerate(arrays):
            half = _half_rows(c, a.shape[0])
            for k, (px, py) in enumerate(peers):
                cps.append(pltpu.make_async_remote_copy(
                    src_ref=ins[i].at[half], dst_ref=outs[i].at[mine, half], send_sem=send.at[3 * i + k],
                    recv_sem=recv.at[3 * i + k], device_id=(px, py, c), device_id_type=MESH))
        return cps

    outs = [jax.ShapeDtypeStruct((4,) + a.shape, a.dtype) for a in arrays]
    return _Comm(arrays, outs, 3 * len(arrays), make)


def _gather_d2d(bufs):
    def make(ins, outs, send, recv, loc):
        x, y, c, mine, peers = _place()
        cps = []
        for i, a in enumerate(bufs):
            half = _half_rows(c, a.shape[1])
            for k, (px, py) in enumerate(peers):
                mine_half = outs[i].at[2 * px + py, half]
                cps.append(pltpu.make_async_remote_copy(
                    src_ref=mine_half, dst_ref=mine_half, send_sem=send.at[3 * i + k], recv_sem=recv.at[3 * i + k],
                    device_id=(x, y, 1 - c), device_id_type=MESH))
        return cps

    outs = [jax.ShapeDtypeStruct(a.shape, a.dtype) for a in bufs]
    return _Comm(bufs, outs, 3 * len(bufs), make, in_place=True)


def _with_own(gathered, own, chip):
    return lax.dynamic_update_index_in_dim(gathered, own, chip, 0)


def _scatter_d2d(grads):
    for a in grads:
        assert a.shape[1] % 32 == 0, a.shape

    def make(ins, outs, send, recv, loc):
        x, y, c, mine, peers = _place()
        cps = []
        for i, a in enumerate(grads):
            other = _half_rows(1 - c, a.shape[1])
            cps.append(pltpu.make_async_remote_copy(
                src_ref=ins[i].at[:, other], dst_ref=outs[i], send_sem=send.at[i], recv_sem=recv.at[i],
                device_id=(x, y, 1 - c), device_id_type=MESH))
        return cps

    outs = [jax.ShapeDtypeStruct((4, a.shape[1] // 2, a.shape[2]), a.dtype) for a in grads]
    return _Comm(grads, outs, len(grads), make)


def _scatter_ici(parts):
    def make(ins, outs, send, recv, loc):
        x, y, c, mine, peers = _place()
        cps = []
        for i in range(len(parts)):
            cps.append(pltpu.make_async_copy(ins[i].at[mine], outs[i].at[mine], loc.at[i]))
            for k, (px, py) in enumerate(peers):
                cps.append(pltpu.make_async_remote_copy(
                    src_ref=ins[i].at[2 * px + py], dst_ref=outs[i].at[mine], send_sem=send.at[3 * i + k],
                    recv_sem=recv.at[3 * i + k], device_id=(px, py, c), device_id_type=MESH))
        return cps

    outs = [jax.ShapeDtypeStruct(a.shape, a.dtype) for a in parts]
    return _Comm(parts, outs, 3 * len(parts), make)


def _sibling_swap(arrays):
    def make(ins, outs, send, recv, loc):
        x, y, c, mine, peers = _place()
        return [pltpu.make_async_remote_copy(src_ref=ins[i], dst_ref=outs[i], send_sem=send.at[i], recv_sem=recv.at[i],
                                             device_id=(x, y, 1 - c), device_id_type=MESH) for i in range(len(arrays))]

    outs = [jax.ShapeDtypeStruct(a.shape, a.dtype) for a in arrays]
    return _Comm(arrays, outs, len(arrays), make)


def _mm(a, b, mode, out_dtype, name, tm, tn, tk, res=None, comm=None):
    if mode == "nn":
        (m, kd), n = a.shape, b.shape[1]
        a_spec = pl.BlockSpec((tm, tk), lambda i, j, k: (i, k))
        b_spec = pl.BlockSpec((tk, tn), lambda i, j, k: (k, j))
        dims = (((1,), (0,)), ((), ()))
    elif mode == "nt":
        (m, kd), n = a.shape, b.shape[0]
        a_spec = pl.BlockSpec((tm, tk), lambda i, j, k: (i, k))
        b_spec = pl.BlockSpec((tn, tk), lambda i, j, k: (j, k))
        dims = (((1,), (1,)), ((), ()))
    else:
        (kd, m), n = a.shape, b.shape[1]
        a_spec = pl.BlockSpec((tk, tm), lambda i, j, k: (k, i))
        b_spec = pl.BlockSpec((tk, tn), lambda i, j, k: (k, j))
        dims = (((0,), (0,)), ((), ()))
    assert m % tm == 0 and n % tn == 0 and kd % tk == 0, (name, m, n, kd, tm, tn, tk)
    nk = kd // tk
    has_res = res is not None
    in_place = out_dtype == F32

    def body(*refs):
        a_ref, b_ref = refs[:2]
        r_ref = refs[2] if has_res else None
        o_ref = refs[3 if has_res else 2]

        def finish(r):
            if has_res:
                r = r + r_ref[...].astype(F32)
            o_ref[...] = r.astype(out_dtype)

        if nk == 1:
            finish(_dot(a_ref[...], b_ref[...], dims))
            return
        k = pl.program_id(2)
        if in_place:
            @pl.when(k == 0)
            def _():
                finish(_dot(a_ref[...], b_ref[...], dims))

            @pl.when(k > 0)
            def _():
                o_ref[...] += _dot(a_ref[...], b_ref[...], dims)
            return
        acc = refs[-1]

        @pl.when(k == 0)
        def _():
            acc[...] = _dot(a_ref[...], b_ref[...], dims)

        @pl.when((k > 0) & (k < nk - 1))
        def _():
            acc[...] += _dot(a_ref[...], b_ref[...], dims)

        @pl.when(k == nk - 1)
        def _():
            finish(acc[...] + _dot(a_ref[...], b_ref[...], dims))

    in_specs = [a_spec, b_spec]
    args = [a, b]
    if has_res:
        in_specs.append(pl.BlockSpec((tm, tn), lambda i, j, k: (i, j)))
        args.append(res)
    res = _call(body, name, (m // tm, n // tn, nk), in_specs, [pl.BlockSpec((tm, tn), lambda i, j, k: (i, j))],
                [jax.ShapeDtypeStruct((m, n), out_dtype)], [] if nk == 1 or in_place else [pltpu.VMEM((tm, tn), F32)],
                ("parallel", "parallel", "arbitrary"), args, comm)
    return res[0] if comm is None else (res[0], res[1:])


def _mm_pieces_nt(pieces, b, name, tm, tn, tk_max, comm=None):
    m, n = pieces[0][0].shape[0], b.shape[0]
    out_dtype = F32
    cands = [c for c in (1024, 512, 256, 128) if c <= tk_max]
    tks = [_pick(math.gcd(a.shape[1], c0) if c0 else a.shape[1], cands) for a, c0 in pieces]
    nks = [a.shape[1] // tk for (a, _), tk in zip(pieces, tks)]
    starts = [sum(nks[:p]) for p in range(len(pieces))]
    ktot = sum(nks)
    npc = len(pieces)

    def body(*refs):
        a_refs, b_refs, o_ref = refs[:npc], refs[npc:2 * npc], refs[2 * npc]
        k = pl.program_id(2)

        @pl.when(k == 0)
        def _():
            o_ref[...] = _dot_nt(a_refs[0][...], b_refs[0][...])

        for p in range(npc):
            @pl.when((k >= max(starts[p], 1)) & (k < starts[p] + nks[p]))
            def _(p=p):
                o_ref[...] += _dot_nt(a_refs[p][...], b_refs[p][...])

    def a_spec(p):
        return pl.BlockSpec((tm, tks[p]), lambda i, j, k: (i, jnp.clip(k - starts[p], 0, nks[p] - 1)))

    def b_spec(p):
        c0 = pieces[p][1] // tks[p]
        return pl.BlockSpec((tn, tks[p]), lambda i, j, k: (j, c0 + jnp.clip(k - starts[p], 0, nks[p] - 1)))

    res = _call(body, name, (m // tm, n // tn, ktot), [a_spec(p) for p in range(npc)] + [b_spec(p) for p in range(npc)],
                [pl.BlockSpec((tm, tn), lambda i, j, k: (i, j))], [jax.ShapeDtypeStruct((m, n), out_dtype)],
                [], ("parallel", "parallel", "arbitrary"), [a for a, _ in pieces] + [b] * npc, comm)
    return res[0] if comm is None else (res[0], res[1:])


def _rms_fwd(h, w, name):
    t, d = h.shape
    tr = _pick(t, (640, 512, 384, 256, 128))

    def body(h_ref, w_ref, u_ref):
        x = h_ref[...]
        r = lax.rsqrt(jnp.mean(x * x, axis=1, keepdims=True) + EPS)
        u_ref[...] = (x * r * w_ref[...]).astype(MXU_DTYPE)

    return pl.pallas_call(
        body, name=name, grid=(t // tr,),
        in_specs=[pl.BlockSpec((tr, d), lambda i: (i, 0)), pl.BlockSpec((1, d), lambda i: (0, 0))],
        out_specs=pl.BlockSpec((tr, d), lambda i: (i, 0)),
        out_shape=jax.ShapeDtypeStruct((t, d), MXU_DTYPE),
        compiler_params=_params(("parallel",)),
    )(h, w)


def _rms_bwd(h, w, du, res, name, comm=None):
    t, d = h.shape
    tr = _pick(t, (640, 512, 384, 256, 128))

    def body(h_ref, w_ref, du_ref, res_ref, dh_ref, gw_ref):
        @pl.when(pl.program_id(0) == 0)
        def _():
            gw_ref[...] = jnp.zeros_like(gw_ref)

        x = h_ref[...]
        r = lax.rsqrt(jnp.mean(x * x, axis=1, keepdims=True) + EPS)
        xhat = x * r
        dy = du_ref[...].astype(F32)
        dxh = dy * w_ref[...]
        dh = r * (dxh - xhat * jnp.mean(dxh * xhat, axis=1, keepdims=True))
        dh_ref[...] = dh + res_ref[...]
        gw_ref[...] += jnp.sum(dy * xhat, axis=0, keepdims=True)

    row = pl.BlockSpec((tr, d), lambda i: (i, 0))
    vec = pl.BlockSpec((1, d), lambda i: (0, 0))
    return _call(body, name, (t // tr,), [row, vec, row, row], [row, vec],
                 [jax.ShapeDtypeStruct((t, d), F32), jax.ShapeDtypeStruct((1, d), F32)], [], ("arbitrary",),
                 (h, w, du, res), comm)


def _loss_bwd(h2, w, target, name):
    t, d = h2.shape
    nc = t // CHUNK

    def body(h_ref, w_ref, tg_ref, loss_ref, dh_ref, gw_ref):
        i = pl.program_id(0)

        @pl.when(i == 0)
        def _():
            gw_ref[...] = jnp.zeros_like(gw_ref)
            loss_ref[...] = jnp.zeros_like(loss_ref)
            dh_ref[...] = jnp.zeros_like(dh_ref)

        @pl.when(i > 0)
        def _():
            x = h_ref[...]
            r = lax.rsqrt(jnp.mean(x * x, axis=1, keepdims=True) + EPS)
            xhat = x * r
            diff = xhat * w_ref[...] - tg_ref[...]
            loss_ref[...] += 0.5 * jnp.sum(jnp.sum(diff * diff, axis=1, keepdims=True), axis=0, keepdims=True) / d
            dy = diff / d
            dxh = dy * w_ref[...]
            dh_ref[...] = r * (dxh - xhat * jnp.mean(dxh * xhat, axis=1, keepdims=True))
            gw_ref[...] += jnp.sum(dy * xhat, axis=0, keepdims=True)

    row = pl.BlockSpec((CHUNK, d), lambda i: (i, 0))
    vec = pl.BlockSpec((1, d), lambda i: (0, 0))
    return pl.pallas_call(
        body, name=name, grid=(nc,),
        in_specs=[row, vec, pl.BlockSpec((CHUNK, d), lambda i: (jnp.maximum(i - 1, 0), 0))],
        out_specs=[pl.BlockSpec((SUBLANES, LANES), lambda i: (0, 0)), row, vec],
        out_shape=[jax.ShapeDtypeStruct((SUBLANES, LANES), F32), jax.ShapeDtypeStruct((t, d), F32),
                   jax.ShapeDtypeStruct((1, d), F32)],
        compiler_params=_params(("arbitrary",)),
    )(h2, w, target)


def _conv_tile(cur, prev8, w_ref, b_ref, kw):
    y = b_ref[...] + cur * w_ref[kw - 1:kw, :]
    for k in range(kw - 1):
        y = y + _shift_down(cur, prev8, kw - 1 - k) * w_ref[k:k + 1, :]
    return y


_SUB = 16


def _sub_rows(s):
    return pl.ds(0 if isinstance(s, int) else pl.multiple_of(s * _SUB, _SUB), _SUB)


def _window(x_ref, prev8, s):
    if isinstance(s, int):
        return jnp.concatenate([prev8, x_ref[0:_SUB, :]], axis=0)
    return x_ref[pl.ds(pl.multiple_of(s * _SUB - SUBLANES, SUBLANES), _SUB + SUBLANES), :]


def _conv_step(win, w, b, kw):
    taps = [win[SUBLANES:] if k == kw - 1 else pltpu.roll(win, kw - 1 - k, 0)[SUBLANES:] for k in range(kw)]
    y = b + taps[kw - 1] * w[kw - 1:kw, :]
    for k in range(kw - 1):
        y = y + taps[k] * w[k:k + 1, :]
    return y, taps


def _conv_dx_step(dpre, next8, w, kw):
    n = _SUB + SUBLANES
    win = jnp.concatenate([dpre, next8], axis=0)
    acc = dpre * w[kw - 1:kw, :]
    for k in range(kw - 1):
        acc = acc + pltpu.roll(win, n - (kw - 1 - k), 0)[0:_SUB] * w[k:k + 1, :]
    return acc


def _fold8(v):
    return functools.reduce(jnp.add, [v[r:r + SUBLANES] for r in range(0, _SUB, SUBLANES)])


def _row_tile(t):
    return _pick(t, (640, 512, 384, 256, 128))


def _ssd_conv_fwd(proj, col0, width, w, b, name):
    t = proj.shape[0]
    kw = w.shape[0]
    tr, tc = _row_tile(t), _pick(width, (512, 256, 128))
    c0, rb = col0 // tc, tr // SUBLANES
    assert col0 % tc == 0

    def body(x_ref, p_ref, w_ref, b_ref, o_ref):
        i = pl.program_id(1)
        prev8 = jnp.where(i > 0, p_ref[...], 0.0)
        pre = _conv_tile(x_ref[...], prev8, w_ref, b_ref, kw)
        rows = _iota((tr, 1), 0) + i * tr
        o_ref[...] = jnp.where(rows >= FRONT, _silu(pre), 0.0)

    return pl.pallas_call(
        body, name=name, grid=(width // tc, t // tr),
        in_specs=[pl.BlockSpec((tr, tc), lambda j, i: (i, c0 + j)),
                  pl.BlockSpec((SUBLANES, tc), lambda j, i: (jnp.maximum(i * rb - 1, 0), c0 + j)),
                  pl.BlockSpec((kw, tc), lambda j, i: (0, j)),
                  pl.BlockSpec((1, tc), lambda j, i: (0, j))],
        out_specs=pl.BlockSpec((tr, tc), lambda j, i: (i, j)),
        out_shape=jax.ShapeDtypeStruct((t, width), F32),
        compiler_params=_params(("parallel", "parallel")),
    )(proj, proj, w, b)


def _ssd_conv_bwd(proj, col0, width, w, b, dact, name, comm=None):
    t = proj.shape[0]
    kw = w.shape[0]
    tr, tc = _row_tile(t), _pick(width, (512, 256, 128))
    c0, rb, nrow = col0 // tc, tr // SUBLANES, t // tr

    def body(x_ref, p_ref, w_ref, b_ref, d_ref, o_ref, gw_ref, gb_ref, carry):
        i = pl.program_id(1)
        ti = nrow - 1 - i

        @pl.when(i == 0)
        def _():
            gw_ref[...] = jnp.zeros_like(gw_ref)
            gb_ref[...] = jnp.zeros_like(gb_ref)
            carry[...] = jnp.zeros_like(carry)

        w, b = w_ref[...], b_ref[...]
        prev8 = jnp.where(ti > 0, p_ref[...], 0.0)
        nsub = tr // _SUB

        def step(s, state):
            next8, gb8, gw8 = state
            pre, taps = _conv_step(_window(x_ref, prev8, s), w, b, kw)
            valid = _iota((_SUB, 1), 0) + (ti * tr + s * _SUB) >= FRONT
            dpre = jnp.where(valid, d_ref[_sub_rows(s), :] * _dsilu(pre), 0.0)
            o_ref[_sub_rows(s), :] = jnp.where(valid, _conv_dx_step(dpre, next8, w, kw), 0.0).astype(MXU_DTYPE)
            return (dpre[0:SUBLANES], gb8 + _fold8(dpre), tuple(g + _fold8(dpre * tp) for g, tp in zip(gw8, taps)))

        zero8 = jnp.zeros((SUBLANES, tc), F32)
        state = lax.fori_loop(0, nsub - 1, lambda n, st: step(nsub - 1 - n, st), (carry[...], zero8, (zero8,) * kw))
        next8, gb8, gw8 = step(0, state)
        carry[...] = next8
        gb_ref[...] += jnp.sum(gb8, axis=0, keepdims=True)
        for k in range(kw):
            gw_ref[k:k + 1, :] += jnp.sum(gw8[k], axis=0, keepdims=True)

    return _call(
        body, name, (width // tc, nrow),
        [pl.BlockSpec((tr, tc), lambda j, i: (nrow - 1 - i, c0 + j)),
         pl.BlockSpec((SUBLANES, tc), lambda j, i: (jnp.maximum((nrow - 1 - i) * rb - 1, 0), c0 + j)),
         pl.BlockSpec((kw, tc), lambda j, i: (0, j)),
         pl.BlockSpec((1, tc), lambda j, i: (0, j)),
         pl.BlockSpec((tr, tc), lambda j, i: (nrow - 1 - i, j))],
        [pl.BlockSpec((tr, tc), lambda j, i: (nrow - 1 - i, j)),
         pl.BlockSpec((SUBLANES, tc), lambda j, i: (0, j)),
         pl.BlockSpec((1, tc), lambda j, i: (0, j))],
        [jax.ShapeDtypeStruct((t, width), MXU_DTYPE), jax.ShapeDtypeStruct((SUBLANES, width), F32),
         jax.ShapeDtypeStruct((1, width), F32)],
        [pltpu.VMEM((SUBLANES, tc), F32)], ("parallel", "arbitrary"), (proj, proj, w, b, dact), comm)


def _ffn_conv_fwd(up, w, b, name):
    t, f2 = up.shape
    f = f2 // 2
    kw = w.shape[0]
    tr, tc = _row_tile(t), _pick(f, (256, 128))
    nf, rb = f // tc, tr // SUBLANES

    def body(xg, pg, xv, pv, wg, wv, bg, bv, o_ref):
        i = pl.program_id(1)
        ag = _conv_tile(xg[...], jnp.where(i > 0, pg[...], 0.0), wg, bg, kw)
        av = _conv_tile(xv[...], jnp.where(i > 0, pv[...], 0.0), wv, bv, kw)
        o_ref[...] = (_silu(ag) * av).astype(MXU_DTYPE)

    def cur(off):
        return pl.BlockSpec((tr, tc), lambda j, i: (i, j + off))

    def prev(off):
        return pl.BlockSpec((SUBLANES, tc), lambda j, i: (jnp.maximum(i * rb - 1, 0), j + off))

    def par(rows, off):
        return pl.BlockSpec((rows, tc), lambda j, i: (0, j + off))

    return pl.pallas_call(
        body, name=name, grid=(nf, t // tr),
        in_specs=[cur(0), prev(0), cur(nf), prev(nf), par(kw, 0), par(kw, nf), par(1, 0), par(1, nf)],
        out_specs=pl.BlockSpec((tr, tc), lambda j, i: (i, j)),
        out_shape=jax.ShapeDtypeStruct((t, f), MXU_DTYPE),
        compiler_params=_params(("parallel", "parallel")),
    )(up, up, up, up, w, w, b, b)


def _ffn_conv_bwd(up, w, b, dact, name):
    t, f2 = up.shape
    f = f2 // 2
    kw = w.shape[0]
    tr, tc = _row_tile(t), _pick(f, (256, 128))
    nf, rb, nrow = f // tc, tr // SUBLANES, t // tr

    def body(xg, pg, xv, pv, wg_ref, wv_ref, bg_ref, bv_ref, d_ref, og_ref, ov_ref, gwg_ref, gwv_ref, gbg_ref, gbv_ref,
             cg, cv):
        i = pl.program_id(1)
        ti = nrow - 1 - i

        @pl.when(i == 0)
        def _():
            for r in (gwg_ref, gwv_ref, gbg_ref, gbv_ref, cg, cv):
                r[...] = jnp.zeros_like(r)

        wg, wv, bg, bv = wg_ref[...], wv_ref[...], bg_ref[...], bv_ref[...]
        p8g, p8v = jnp.where(ti > 0, pg[...], 0.0), jnp.where(ti > 0, pv[...], 0.0)
        nsub = tr // _SUB

        def step(s, state):
            ng, nv, gbg8, gbv8, gwg8, gwv8 = state
            ag, tg = _conv_step(_window(xg, p8g, s), wg, bg, kw)
            av, tv = _conv_step(_window(xv, p8v, s), wv, bv, kw)
            d = d_ref[_sub_rows(s), :]
            sg = jax.nn.sigmoid(ag)
            dag = d * av * (sg * (1.0 + ag * (1.0 - sg)))
            dav = d * (ag * sg)
            valid = _iota((_SUB, 1), 0) + (ti * tr + s * _SUB) >= FRONT
            og_ref[_sub_rows(s), :] = jnp.where(valid, _conv_dx_step(dag, ng, wg, kw), 0.0).astype(MXU_DTYPE)
            ov_ref[_sub_rows(s), :] = jnp.where(valid, _conv_dx_step(dav, nv, wv, kw), 0.0).astype(MXU_DTYPE)
            return (dag[0:SUBLANES], dav[0:SUBLANES], gbg8 + _fold8(dag), gbv8 + _fold8(dav),
                    tuple(g + _fold8(dag * tp) for g, tp in zip(gwg8, tg)),
                    tuple(g + _fold8(dav * tp) for g, tp in zip(gwv8, tv)))

        zero8 = jnp.zeros((SUBLANES, tc), F32)
        state = lax.fori_loop(0, nsub - 1, lambda n, st: step(nsub - 1 - n, st),
                              (cg[...], cv[...], zero8, zero8, (zero8,) * kw, (zero8,) * kw))
        ng, nv, gbg8, gbv8, gwg8, gwv8 = step(0, state)
        cg[...] = ng
        cv[...] = nv
        gbg_ref[...] += jnp.sum(gbg8, axis=0, keepdims=True)
        gbv_ref[...] += jnp.sum(gbv8, axis=0, keepdims=True)
        for k in range(kw):
            gwg_ref[k:k + 1, :] += jnp.sum(gwg8[k], axis=0, keepdims=True)
            gwv_ref[k:k + 1, :] += jnp.sum(gwv8[k], axis=0, keepdims=True)

    def cur(off):
        return pl.BlockSpec((tr, tc), lambda j, i: (nrow - 1 - i, j + off))

    def prev(off):
        return pl.BlockSpec((SUBLANES, tc), lambda j, i: (jnp.maximum((nrow - 1 - i) * rb - 1, 0), j + off))

    def par(rows, off):
        return pl.BlockSpec((rows, tc), lambda j, i: (0, j + off))

    acc8 = pl.BlockSpec((SUBLANES, tc), lambda j, i: (0, j))
    acc1 = pl.BlockSpec((1, tc), lambda j, i: (0, j))
    return pl.pallas_call(
        body, name=name, grid=(nf, nrow),
        in_specs=[cur(0), prev(0), cur(nf), prev(nf), par(kw, 0), par(kw, nf), par(1, 0), par(1, nf), cur(0)],
        out_specs=[cur(0), cur(0), acc8, acc8, acc1, acc1],
        out_shape=[jax.ShapeDtypeStruct((t, f), MXU_DTYPE), jax.ShapeDtypeStruct((t, f), MXU_DTYPE),
                   jax.ShapeDtypeStruct((SUBLANES, f), F32), jax.ShapeDtypeStruct((SUBLANES, f), F32),
                   jax.ShapeDtypeStruct((1, f), F32), jax.ShapeDtypeStruct((1, f), F32)],
        scratch_shapes=[pltpu.VMEM((SUBLANES, tc), F32), pltpu.VMEM((SUBLANES, tc), F32)],
        compiler_params=_params(("parallel", "arbitrary")),
    )(up, up, up, up, w, w, b, b, dact)


def _gate_fwd(bs, br, proj, c_gs, c_gr, name):
    t, d = bs.shape
    tr = _row_tile(t)

    def body(bs_ref, br_ref, gs_ref, gr_ref, o_ref):
        o_ref[...] = (jax.nn.sigmoid(gs_ref[...]) * bs_ref[...] + jax.nn.sigmoid(gr_ref[...]) * br_ref[...]).astype(MXU_DTYPE)

    row = pl.BlockSpec((tr, d), lambda i: (i, 0))
    return pl.pallas_call(
        body, name=name, grid=(t // tr,),
        in_specs=[row, row, pl.BlockSpec((tr, d), lambda i: (i, c_gs // d)), pl.BlockSpec((tr, d), lambda i: (i, c_gr // d))],
        out_specs=row, out_shape=jax.ShapeDtypeStruct((t, d), MXU_DTYPE),
        compiler_params=_params(("parallel",)),
    )(bs, br, proj, proj)


def _gate_bwd(dm, bs, br, proj, c_gs, c_gr, name):
    t, d = bs.shape
    tr = _row_tile(t)

    def body(dm_ref, bs_ref, br_ref, gs_ref, gr_ref, dbs_ref, dbr_ref, dgg_ref):
        g = dm_ref[...]
        ss, sr = jax.nn.sigmoid(gs_ref[...]), jax.nn.sigmoid(gr_ref[...])
        dbs_ref[...] = (g * ss).astype(MXU_DTYPE)
        dbr_ref[...] = (g * sr).astype(MXU_DTYPE)
        dgg_ref[:, :d] = (g * bs_ref[...] * ss * (1.0 - ss)).astype(MXU_DTYPE)
        dgg_ref[:, d:] = (g * br_ref[...] * sr * (1.0 - sr)).astype(MXU_DTYPE)

    row = pl.BlockSpec((tr, d), lambda i: (i, 0))
    out = jax.ShapeDtypeStruct((t, d), MXU_DTYPE)
    return pl.pallas_call(
        body, name=name, grid=(t // tr,),
        in_specs=[row, row, row, pl.BlockSpec((tr, d), lambda i: (i, c_gs // d)), pl.BlockSpec((tr, d), lambda i: (i, c_gr // d))],
        out_specs=[row, row, pl.BlockSpec((tr, 2 * d), lambda i: (i, 0))],
        out_shape=[out, out, jax.ShapeDtypeStruct((t, 2 * d), MXU_DTYPE)],
        compiler_params=_params(("parallel",)),
    )(dm, bs, br, proj, proj)


def _ret_consts(h):
    lg = math.log(1.0 - 2.0 ** (-5.0 - h))
    l = _iota((CHUNK, 1), 0).astype(F32)
    diff = l - _iota((1, CHUNK), 1).astype(F32)
    dm = jnp.exp(jnp.where(diff >= 0, diff * lg, -jnp.inf))
    dmt = jnp.exp(jnp.where(diff <= 0, -diff * lg, -jnp.inf))
    cs = jnp.exp((l + 1.0) * lg)
    kdec = jnp.exp((CHUNK - 1.0 - l) * lg)
    return dm, dmt, cs, kdec, math.exp(CHUNK * lg)


def _ret_fwd(proj, c_q, c_k, c_v, c_g, cos, sin, d, name, comm=None):
    t = proj.shape[0]
    nc = t // CHUNK
    hq, hv = d // RET_HEADS, 2 * d // RET_HEADS
    half = hq // 2
    scale = hq ** -0.5

    def body(q_ref, k_ref, v_ref, g_ref, cos_ref, sin_ref, o_ref, y_ref, qr_ref, kr_ref, st_ref, rs):
        @pl.when(pl.program_id(0) == 0)
        def _():
            rs[...] = jnp.zeros_like(rs)

        co, si = cos_ref[...], sin_ref[...]
        for h in range(RET_HEADS):
            dm, _, cs, kdec, gam = _ret_consts(h)
            q1, q2 = q_ref[:, h * hq:h * hq + half], q_ref[:, h * hq + half:(h + 1) * hq]
            k1, k2 = k_ref[:, h * hq:h * hq + half], k_ref[:, h * hq + half:(h + 1) * hq]
            qr = jnp.concatenate([q1 * co - q2 * si, q2 * co + q1 * si], axis=1)
            kr = jnp.concatenate([k1 * co - k2 * si, k2 * co + k1 * si], axis=1) * scale
            qr_ref[:, h * hq:(h + 1) * hq] = qr.astype(MXU_DTYPE)
            kr_ref[:, h * hq:(h + 1) * hq] = kr.astype(MXU_DTYPE)
            v = v_ref[:, h * hv:(h + 1) * hv]
            r_in = rs[h * hq:(h + 1) * hq, :]
            st_ref[0, h * hq:(h + 1) * hq, :] = r_in.astype(MXU_DTYPE)
            s = _dot_nt(qr, kr) * dm
            o = _dot(s, v) + cs * _dot(qr, r_in)
            rs[h * hq:(h + 1) * hq, :] = gam * r_in + _dot_tn(kr * kdec, v)
            o_ref[:, h * hv:(h + 1) * hv] = o
            on = o * lax.rsqrt(jnp.mean(o * o, axis=1, keepdims=True) + EPS)
            y_ref[:, h * hv:(h + 1) * hv] = (_silu(g_ref[:, h * hv:(h + 1) * hv]) * on).astype(MXU_DTYPE)

    def col(width, c0):
        return pl.BlockSpec((CHUNK, width), lambda i: (i, c0 // width))

    tab = pl.BlockSpec((CHUNK, half), lambda i: (i, 0))
    return _call(
        body, name, (nc,),
        [col(d, c_q), col(d, c_k), col(2 * d, c_v), col(2 * d, c_g), tab, tab],
        [col(2 * d, 0), col(2 * d, 0), col(d, 0), col(d, 0), pl.BlockSpec((1, d, hv), lambda i: (i, 0, 0))],
        [jax.ShapeDtypeStruct((t, 2 * d), F32), jax.ShapeDtypeStruct((t, 2 * d), MXU_DTYPE),
         jax.ShapeDtypeStruct((t, d), MXU_DTYPE), jax.ShapeDtypeStruct((t, d), MXU_DTYPE),
         jax.ShapeDtypeStruct((nc, d, hv), MXU_DTYPE)],
        [pltpu.VMEM((d, hv), F32)], ("arbitrary",), (proj, proj, proj, proj, cos, sin), comm)


def _ret_bwd(dy, proj, c_v, c_g, o, qr, kr, st, cos, sin, d, name, comm=None):
    t = proj.shape[0]
    nc = t // CHUNK
    hq, hv = d // RET_HEADS, 2 * d // RET_HEADS
    half = hq // 2
    scale = hq ** -0.5

    def body(dy_ref, v_ref, g_ref, o_ref, qr_ref, kr_ref, st_ref, cos_ref, sin_ref, dqk_ref, dvg_ref, drs):
        dq_ref, dk_ref = dqk_ref.at[:, pl.ds(0, d)], dqk_ref.at[:, pl.ds(d, d)]
        dv_ref, dg_ref = dvg_ref.at[:, pl.ds(0, 2 * d)], dvg_ref.at[:, pl.ds(2 * d, 2 * d)]

        @pl.when(pl.program_id(0) == 0)
        def _():
            drs[...] = jnp.zeros_like(drs)

        co, si = cos_ref[...], sin_ref[...]
        for h in range(RET_HEADS):
            dm, dmt, cs, kdec, gam = _ret_consts(h)
            vs = slice(h * hv, (h + 1) * hv)
            qs = slice(h * hq, (h + 1) * hq)
            o_h = o_ref[:, vs]
            g_h = g_ref[:, vs]
            d_y = dy_ref[:, vs]
            r = lax.rsqrt(jnp.mean(o_h * o_h, axis=1, keepdims=True) + EPS)
            on = o_h * r
            d_on = d_y * _silu(g_h)
            dg_ref[:, vs] = (d_y * on * _dsilu(g_h)).astype(MXU_DTYPE)
            d_o = r * (d_on - on * jnp.mean(d_on * on, axis=1, keepdims=True))
            q_h, k_h, v_h = qr_ref[:, qs], kr_ref[:, qs], v_ref[:, vs]
            r_in = st_ref[0, qs, :]
            dr_n = drs[qs, :]
            csdo = cs * d_o
            ds = _dot_nt(d_o, v_h) * dm
            dst = _dot_nt(v_h, d_o) * dmt
            s_t = _dot_nt(k_h, q_h) * dmt
            dqr = _dot(ds, k_h) + _dot_nt(csdo, r_in)
            dkr = _dot(dst, q_h) + kdec * _dot_nt(v_h, dr_n)
            dv_ref[:, vs] = (_dot(s_t, d_o) + _dot(k_h.astype(F32) * kdec, dr_n)).astype(MXU_DTYPE)
            drs[qs, :] = gam * dr_n + _dot_tn(q_h, csdo)
            a1, a2 = dqr[:, :half], dqr[:, half:]
            dq_ref[:, qs] = jnp.concatenate([a1 * co + a2 * si, a2 * co - a1 * si], axis=1).astype(MXU_DTYPE)
            b1, b2 = dkr[:, :half] * scale, dkr[:, half:] * scale
            dk_ref[:, qs] = jnp.concatenate([b1 * co + b2 * si, b2 * co - b1 * si], axis=1).astype(MXU_DTYPE)

    def col(width, c0=0):
        return pl.BlockSpec((CHUNK, width), lambda i: (nc - 1 - i, c0 // width))

    tab = pl.BlockSpec((CHUNK, half), lambda i: (nc - 1 - i, 0))
    return _call(
        body, name, (nc,),
        [col(2 * d), col(2 * d, c_v), col(2 * d, c_g), col(2 * d), col(d), col(d),
         pl.BlockSpec((1, d, hv), lambda i: (nc - 1 - i, 0, 0)), tab, tab],
        [col(2 * d), col(4 * d)],
        [jax.ShapeDtypeStruct((t, 2 * d), MXU_DTYPE), jax.ShapeDtypeStruct((t, 4 * d), MXU_DTYPE)],
        [pltpu.VMEM((d, hv), F32)], ("arbitrary",), (dy, proj, proj, o, qr, kr, st, cos, sin), comm)


def _ssd_small(dtraw_ref, dtb_ref, alog_ref, chunk_idx, nh):
    rows = _iota((CHUNK, 1), 0)
    ok = ((rows >= FRONT) | (chunk_idx > 0)) & (_iota((1, LANES), 1) < nh)
    z = dtraw_ref[...] + dtb_ref[...]
    dt = jnp.where(ok, jax.nn.softplus(z), 0.0)
    sig = jnp.where(ok, jax.nn.sigmoid(z), 0.0)
    a = jnp.where(_iota((1, LANES), 1) < nh, -jnp.exp(alog_ref[...]), 0.0)
    tri = (_iota((CHUNK, CHUNK), 0) >= _iota((CHUNK, CHUNK), 1)).astype(F32)
    acs = _dot01(tri, dt * a, "b", 3)
    return dt, sig, a, acs, acs.T


def _head_expand(g, hpg, gw):
    shift = int(math.log2(SSD_HEAD_DIM))
    return (_iota((LANES, gw), 0) == g * hpg + lax.shift_right_logical(_iota((LANES, gw), 1), shift)).astype(F32)


def _ssd_fwd(xa, proj, c_dt, c_z, dtb, alog, dvec, nw, di, name, comm=None):
    t = xa.shape[0]
    nc = t // CHUNK
    nh = di // SSD_HEAD_DIM
    hpg = nh // SSD_GROUPS
    gw = di // SSD_GROUPS
    n = SSD_STATE
    gn = SSD_GROUPS * n
    hd = SSD_HEAD_DIM

    def body(x_ref, b_ref, c_ref, dtraw_ref, z_ref, dtb_ref, alog_ref, d_ref, nw_ref,
             y_ref, ys_ref, st_ref, hts, xdt_s):
        c = pl.program_id(0)

        @pl.when(c == 0)
        def _():
            hts[...] = jnp.zeros_like(hts)

        dt, _, _, acs, acs_t = _ssd_small(dtraw_ref, dtb_ref, alog_ref, c, nh)
        tri = _iota((CHUNK, CHUNK), 0) >= _iota((CHUNK, CHUNK), 1)
        dvec8 = jnp.broadcast_to(d_ref[...], (SUBLANES, LANES))
        for g in range(SSD_GROUPS):
            gs = slice(g * gw, (g + 1) * gw)
            ns = slice(g * n, (g + 1) * n)
            e_mat = _head_expand(g, hpg, gw)
            ax = _dot01(acs, e_mat, "a", 3)
            dtx = _dot01(dt, e_mat, "a", 3)
            dx = _dot01(dvec8, e_mat, "a", 3)[0:1, :]
            xg, bg, cg = x_ref[:, gs], b_ref[:, ns], c_ref[:, ns]
            xdt = xg * dtx
            xdt_s[...] = xdt.astype(MXU_DTYPE)
            cb = _dot_nt(cg, bg)
            ht = hts[ns, :]
            st_ref[0, ns, :] = ht.astype(MXU_DTYPE)
            y_ref[:, gs] = jnp.exp(ax) * _dot(cg, ht) + dx * xg
            for hh in range(hpg):
                h = g * hpg + hh
                lmat = jnp.exp(jnp.where(tri, acs[:, h:h + 1] - acs_t[h:h + 1, :], -jnp.inf))
                hs = slice(g * gw + hh * hd, g * gw + (hh + 1) * hd)
                y_ref[:, hs] += _dot(cb * lmat, xdt_s[:, hh * hd:(hh + 1) * hd])
            aend = ax[CHUNK - 1:CHUNK, :]
            hts[ns, :] = jnp.exp(aend) * ht + _dot_tn(bg, xdt * jnp.exp(aend - ax))
        for g in range(SSD_GROUPS):
            gs = slice(g * gw, (g + 1) * gw)
            yz = y_ref[:, gs] * _silu(z_ref[:, gs])
            r = lax.rsqrt(jnp.mean(yz * yz, axis=1, keepdims=True) + EPS)
            ys_ref[:, gs] = (yz * r * nw_ref[:, gs]).astype(MXU_DTYPE)

    def col(width, c0, arr_is_xa=False):
        return pl.BlockSpec((CHUNK, width), lambda i: (i, c0 // width))

    vec = pl.BlockSpec((1, LANES), lambda i: (0, 0))
    assert di % gn == 0 and c_dt % LANES == 0 and c_z % di == 0
    return _call(
        body, name, (nc,),
        [col(di, 0), col(gn, di), col(gn, di + gn), col(LANES, c_dt), col(di, c_z), vec, vec, vec,
         pl.BlockSpec((1, di), lambda i: (0, 0))],
        [col(di, 0), col(di, 0), pl.BlockSpec((1, gn, gw), lambda i: (i, 0, 0))],
        [jax.ShapeDtypeStruct((t, di), F32), jax.ShapeDtypeStruct((t, di), MXU_DTYPE),
         jax.ShapeDtypeStruct((nc, gn, gw), MXU_DTYPE)],
        [pltpu.VMEM((gn, gw), F32), pltpu.VMEM((CHUNK, gw), MXU_DTYPE)], ("arbitrary",),
        (xa, xa, xa, proj, proj, dtb, alog, dvec, nw), comm)


def _ssd_bwd(dys, xa, proj, c_dt, c_z, ypre, st, dtb, alog, dvec, nw, di, name, comm=None):
    t = xa.shape[0]
    nc = t // CHUNK
    nh = di // SSD_HEAD_DIM
    hpg = nh // SSD_GROUPS
    gw = di // SSD_GROUPS
    n = SSD_STATE
    gn = SSD_GROUPS * n
    hd = SSD_HEAD_DIM

    def body(dys_ref, x_ref, b_ref, c_ref, dtraw_ref, z_ref, y_ref, st_ref, dtb_ref, alog_ref, d_ref, nw_ref,
             dxa_ref, dz_ref, ddt_ref, gb_ref, ga_ref, gd_ref, gnw_ref, dhts, dy_s, xdt_s, dxdt_s):
        i = pl.program_id(0)
        c = nc - 1 - i

        @pl.when(i == 0)
        def _():
            dhts[...] = jnp.zeros_like(dhts)
            gb_ref[...] = jnp.zeros_like(gb_ref)
            ga_ref[...] = jnp.zeros_like(ga_ref)
            gd_ref[...] = jnp.zeros_like(gd_ref)
            gnw_ref[...] = jnp.zeros_like(gnw_ref)

        dt, sig, a, acs, acs_t = _ssd_small(dtraw_ref, dtb_ref, alog_ref, c, nh)
        tri = _iota((CHUNK, CHUNK), 0) >= _iota((CHUNK, CHUNK), 1)
        triu = _iota((CHUNK, CHUNK), 0) <= _iota((CHUNK, CHUNK), 1)
        lane = _iota((1, LANES), 1)
        rows = _iota((CHUNK, 1), 0)
        head_row = _iota((LANES, 1), 0)
        dvec8 = jnp.broadcast_to(d_ref[...], (SUBLANES, LANES))
        da = jnp.zeros((CHUNK, LANES), F32)
        da_t = jnp.zeros((LANES, CHUNK), F32)
        ddt = jnp.zeros((CHUNK, LANES), F32)
        gd = jnp.zeros((1, LANES), F32)
        for g in range(SSD_GROUPS):
            gs = slice(g * gw, (g + 1) * gw)
            ns = slice(g * n, (g + 1) * n)
            y_g, z_g = y_ref[:, gs], z_ref[:, gs]
            sz = _silu(z_g)
            yz = y_g * sz
            r = lax.rsqrt(jnp.mean(yz * yz, axis=1, keepdims=True) + EPS)
            nrm = yz * r
            dyo = dys_ref[:, gs]
            gnw_ref[:, gs] += jnp.sum(dyo * nrm, axis=0, keepdims=True)
            dn = dyo * nw_ref[:, gs]
            dyz = r * (dn - nrm * jnp.mean(dn * nrm, axis=1, keepdims=True))
            dz_ref[:, gs] = (dyz * y_g * _dsilu(z_g)).astype(MXU_DTYPE)
            dy_g = dyz * sz
            dy_s[...] = dy_g.astype(MXU_DTYPE)
            e_mat = _head_expand(g, hpg, gw)
            ax = _dot01(acs, e_mat, "a", 3)
            dtx = _dot01(dt, e_mat, "a", 3)
            dx = _dot01(dvec8, e_mat, "a", 3)[0:1, :]
            xg, bg, cg = x_ref[:, gs], b_ref[:, ns], c_ref[:, ns]
            xdt = xg * dtx
            xdt_s[...] = xdt.astype(MXU_DTYPE)
            aend = ax[CHUNK - 1:CHUNK, :]
            e = jnp.exp(aend - ax)
            ea = jnp.exp(ax)
            eend = jnp.exp(aend)
            htp = st_ref[0, ns, :].astype(F32)
            dht = dhts[ns, :]
            cb = _dot_nt(cg, bg)
            q = _dot(bg, dht)
            dxdt_s[...] = e * q
            wl = e * q * xdt
            d_b = _dot_nt(e * xdt, dht)
            yi = ea * _dot(cg, htp)
            eady = ea * dy_g
            d_c = _dot_nt(eady, htp)
            t1 = jnp.sum(dht * htp, axis=0, keepdims=True) * eend
            dhts[ns, :] = eend * dht + _dot_tn(cg, eady)
            da = da + _dot01(dy_g * yi - wl, e_mat, "a", 3, _NT)
            tail = jnp.broadcast_to(jnp.sum(wl, axis=0, keepdims=True) + t1, (SUBLANES, gw))
            da_end = _dot01(tail, e_mat, "a", 3, _NT)[0:1, :]
            da = da + jnp.where(rows == CHUNK - 1, da_end, 0.0)
            dcb = jnp.zeros((CHUNK, CHUNK), F32)
            for hh in range(hpg):
                h = g * hpg + hh
                lmat = jnp.exp(jnp.where(tri, acs[:, h:h + 1] - acs_t[h:h + 1, :], -jnp.inf))
                hl = slice(hh * hd, (hh + 1) * hd)
                dy_h, xdt_h = dy_s[:, hl], xdt_s[:, hl]
                dxdt_s[:, hl] += _dot_tn(cb * lmat, dy_h)
                dml = _dot_nt(dy_h, xdt_h) * lmat
                dcb = dcb + dml
                gmat = dml * cb
                da = da + jnp.where(lane == h, jnp.sum(gmat, axis=1, keepdims=True), 0.0)
                da_t = da_t - jnp.where(head_row == h, jnp.sum(gmat, axis=0, keepdims=True), 0.0)
            d_c = d_c + _dot(dcb, bg)
            d_b = d_b + _dot_tn(dcb, cg)
            dxdt = dxdt_s[...]
            dxa_ref[:, gs] = dxdt * dtx + dx * dy_g
            dxa_ref[:, di + g * n:di + (g + 1) * n] = d_b
            dxa_ref[:, di + gn + g * n:di + gn + (g + 1) * n] = d_c
            ddt = ddt + _dot01(dxdt * xg, e_mat, "a", 3, _NT)
            gd8 = jnp.broadcast_to(jnp.sum(dy_g * xg, axis=0, keepdims=True), (SUBLANES, gw))
            gd = gd + _dot01(gd8, e_mat, "a", 3, _NT)[0:1, :]
        da = da + da_t.T
        triu_f = triu.astype(F32)
        ddta = _dot01(triu_f, da, "b", 3)
        ddt = ddt + ddta * a
        draw = ddt * sig
        ddt_ref[...] = draw.astype(MXU_DTYPE)
        gb_ref[...] += jnp.sum(draw, axis=0, keepdims=True)
        ga_ref[...] += jnp.sum(ddta * dt, axis=0, keepdims=True) * a
        gd_ref[...] += gd

    def col(width, c0):
        return pl.BlockSpec((CHUNK, width), lambda i: (nc - 1 - i, c0 // width))

    vec = pl.BlockSpec((1, LANES), lambda i: (0, 0))
    wide = pl.BlockSpec((1, di), lambda i: (0, 0))
    wa = di + 2 * gn
    return _call(
        body, name, (nc,),
        [col(di, 0), col(di, 0), col(gn, di), col(gn, di + gn), col(LANES, c_dt), col(di, c_z), col(di, 0),
         pl.BlockSpec((1, gn, gw), lambda i: (nc - 1 - i, 0, 0)), vec, vec, vec, wide],
        [col(wa, 0), col(di, 0), col(LANES, 0), vec, vec, vec, wide],
        [jax.ShapeDtypeStruct((t, wa), F32), jax.ShapeDtypeStruct((t, di), MXU_DTYPE),
         jax.ShapeDtypeStruct((t, LANES), MXU_DTYPE), jax.ShapeDtypeStruct((1, LANES), F32),
         jax.ShapeDtypeStruct((1, LANES), F32), jax.ShapeDtypeStruct((1, LANES), F32),
         jax.ShapeDtypeStruct((1, di), F32)],
        [pltpu.VMEM((gn, gw), F32), pltpu.VMEM((CHUNK, gw), MXU_DTYPE), pltpu.VMEM((CHUNK, gw), MXU_DTYPE),
         pltpu.VMEM((CHUNK, gw), F32)], ("arbitrary",),
        (dys, xa, xa, xa, proj, proj, ypre, st, dtb, alog, dvec, nw), comm)


def _adam_math(w, g, m, v):
    m2 = ADAM_B1 * m + (1.0 - ADAM_B1) * g
    v2 = ADAM_B2 * v + (1.0 - ADAM_B2) * (g * g)
    m_hat = m2 / (1.0 - ADAM_B1 ** ADAM_STEP)
    v_hat = v2 / (1.0 - ADAM_B2 ** ADAM_STEP)
    delta = -ADAM_LR * (m_hat / (jnp.sqrt(v_hat) + ADAM_EPS) + ADAM_WD * w)
    return delta, m2, v2


def _adam_big(w, g_mine, g_sib, m, v, core, name):
    r, c = w.shape
    h = r // 2
    tr = _pick(h, (128, 64, 32, 16, 8))
    nbh = h // tr

    def body(core_ref, w_ref, a_ref, b_ref, m_ref, v_ref, g_ref, d_ref, m2_ref, v2_ref):
        g = jnp.where(pl.program_id(0) // nbh == core_ref[0], a_ref[...], b_ref[...])
        delta, m2, v2 = _adam_math(w_ref[...], g, m_ref[...], v_ref[...])
        g_ref[...] = g
        d_ref[...] = delta
        m2_ref[...] = m2
        v2_ref[...] = v2

    blk = pl.BlockSpec((tr, c), lambda i, core_ref: (i, 0))
    hblk = pl.BlockSpec((tr, c), lambda i, core_ref: (i % nbh, 0))
    out = jax.ShapeDtypeStruct((r, c), F32)
    return pl.pallas_call(
        body, name=name,
        grid_spec=pltpu.PrefetchScalarGridSpec(num_scalar_prefetch=1, grid=(2 * nbh,),
                                               in_specs=[blk, hblk, hblk, blk, blk], out_specs=[blk] * 4),
        out_shape=[out] * 4, compiler_params=_params(("parallel",)),
    )(core, w, g_mine, g_sib, m, v)


def _pair_sum(g, sib, core, name):
    _, r, c = g.shape
    h = r // 2
    tr = _pick(h, (128, 64, 32, 16))
    nb = h // tr

    def body(core_ref, g_ref, s_ref, o_ref):
        o_ref[...] = (g_ref[...].astype(F32) + s_ref[...].astype(F32)).astype(WIRE_DTYPE)

    return pl.pallas_call(
        body, name=name,
        grid_spec=pltpu.PrefetchScalarGridSpec(
            num_scalar_prefetch=1, grid=(4, nb),
            in_specs=[pl.BlockSpec((1, tr, c), lambda j, i, core_ref: (j, core_ref[0] * nb + i, 0)),
                      pl.BlockSpec((1, tr, c), lambda j, i, core_ref: (j, i, 0))],
            out_specs=pl.BlockSpec((1, tr, c), lambda j, i, core_ref: (j, i, 0))),
        out_shape=jax.ShapeDtypeStruct((4, h, c), WIRE_DTYPE), compiler_params=_params(("parallel", "parallel")),
    )(core, g, sib)


def _sum4(parts, name):
    _, r, c = parts.shape
    tr = _pick(r, (128, 64, 32, 16, 8))

    def body(p_ref, o_ref):
        acc = p_ref[0].astype(F32)
        for j in range(1, 4):
            acc = acc + p_ref[j].astype(F32)
        o_ref[...] = acc

    return pl.pallas_call(
        body, name=name, grid=(r // tr,),
        in_specs=[pl.BlockSpec((4, tr, c), lambda i: (0, i, 0))],
        out_specs=pl.BlockSpec((tr, c), lambda i: (i, 0)),
        out_shape=jax.ShapeDtypeStruct((r, c), F32),
        compiler_params=_params(("parallel",)),
    )(parts)


def _adam_small(items, chip, name):
    n = len(items) - 1

    def total(g_ref, r, c):
        acc = g_ref[0, 0:r, 0:c]
        for j in range(1, 8):
            acc = acc + g_ref[j, 0:r, 0:c]
        return acc

    def body(chip_ref, *refs):
        g_refs, wmv, outs = refs[:n + 1], refs[n + 1:4 * n + 1], refs[4 * n + 1:]
        for p in range(n):
            r, c = items[p][1].shape
            g = total(g_refs[p], r, c)
            w_ref, m_ref, v_ref = wmv[3 * p:3 * p + 3]
            delta, m2, v2 = _adam_math(w_ref[...], g, m_ref[...], v_ref[...])
            for o_ref, val in zip(outs[4 * p:4 * p + 4], (g, delta, m2, v2)):
                o_ref[...] = val
        outs[4 * n][...] = total(g_refs[n], 1, 1)

    def full(shape):
        return pl.BlockSpec(shape, lambda i, chip_ref: (0,) * len(shape))

    g_specs, args, out_specs, out_shapes = [], [], [], []
    for g, w, m, v, sharded in items[:n]:
        if sharded:
            g_specs.append(pl.BlockSpec((8, g.shape[1], w.shape[1]), lambda i, chip_ref: (0, 0, chip_ref[0])))
        else:
            g_specs.append(full(g.shape))
        args += [w, m, v]
        out_specs += [full(w.shape)] * 4
        out_shapes += [jax.ShapeDtypeStruct(w.shape, F32)] * 4
    g_specs.append(full(items[n][0].shape))
    return pl.pallas_call(
        body, name=name,
        grid_spec=pltpu.PrefetchScalarGridSpec(
            num_scalar_prefetch=1, grid=(1,), in_specs=g_specs + [full(a.shape) for a in args],
            out_specs=out_specs + [full((1, 1))]),
        out_shape=out_shapes + [jax.ShapeDtypeStruct((1, 1), F32)],
    )(chip, *[it[0] for it in items], *args)


def _gather8(arrays):
    def make(ins, outs, send, recv, loc):
        x, y, c, _, _ = _place()
        mine = 4 * x + 2 * y + c
        cps = []
        for i in range(len(arrays)):
            cps.append(pltpu.make_async_copy(ins[i], outs[i].at[mine], loc.at[i]))
            for k in range(1, 8):
                fx, fy, fc = (k >> 2) & 1, (k >> 1) & 1, k & 1
                peer = (1 - x if fx else x, 1 - y if fy else y, 1 - c if fc else c)
                cps.append(pltpu.make_async_remote_copy(
                    src_ref=ins[i], dst_ref=outs[i].at[mine], send_sem=send.at[7 * i + k - 1],
                    recv_sem=recv.at[7 * i + k - 1], device_id=peer, device_id_type=MESH))
        return cps

    outs = [jax.ShapeDtypeStruct((8,) + a.shape, a.dtype) for a in arrays]
    return _Comm(arrays, outs, 7 * len(arrays), make)


def _pack(parts):
    flat = jnp.concatenate([p.reshape(-1).astype(F32) for p in parts])
    pad = (-flat.shape[0]) % (32 * LANES)
    return jnp.pad(flat, (0, pad)).reshape(-1, LANES)


def _unpack(slab, shapes):
    flat = slab.reshape(-1)
    out, off = [], 0
    for s in shapes:
        size = int(np.prod(s))
        out.append(flat[off:off + size].reshape(s))
        off += size
    return out


def _pad_lanes(v):
    return jnp.pad(v.reshape(1, -1), ((0, 0), (0, LANES - v.shape[-1])))


def kernel(x, meta_tokens, mix_norm_w, w_in, ssd_conv_w, ssd_conv_b, ssd_dt_bias, ssd_A_log, ssd_D, ssd_norm_w, w_branch_ssd, w_branch_ret, w_out, ffn_norm_w, w_up, ffn_conv_w, ffn_conv_b, w_down, final_norm_w, loss_target, m_meta_tokens, m_mix_norm_w, m_w_in, m_ssd_conv_w, m_ssd_conv_b, m_ssd_dt_bias, m_ssd_A_log, m_ssd_D, m_ssd_norm_w, m_w_branch_ssd, m_w_branch_ret, m_w_out, m_ffn_norm_w, m_w_up, m_ffn_conv_w, m_ffn_conv_b, m_w_down, m_final_norm_w, v_meta_tokens, v_mix_norm_w, v_w_in, v_ssd_conv_w, v_ssd_conv_b, v_ssd_dt_bias, v_ssd_A_log, v_ssd_D, v_ssd_norm_w, v_w_branch_ssd, v_w_branch_ret, v_w_out, v_ffn_norm_w, v_w_up, v_ffn_conv_w, v_ffn_conv_b, v_w_down, v_final_norm_w):
    seq, d = x.shape[1], x.shape[2]
    t = CHUNK + seq
    di = 2 * d
    nh = di // SSD_HEAD_DIM
    gn = SSD_GROUPS * SSD_STATE
    cw = di + 2 * gn
    f = w_down.shape[1] * 4
    chip = 2 * lax.axis_index("x") + lax.axis_index("y")

    order = [("z", di), ("v", di), ("g", di), ("xbc", cw), ("q", d), ("k", d), ("gs", d), ("gr", d), ("dt", LANES)]
    col, acc = {}, 0
    for nm, wd in order:
        col[nm] = acc
        acc += wd
    wp = acc
    ref_order = [("z", di), ("xbc", cw), ("dt", nh), ("q", d), ("k", d), ("v", di), ("g", di), ("gs", d), ("gr", d)]
    ref_off, acc = {}, 0
    for nm, wd in ref_order:
        ref_off[nm] = (acc, wd)
        acc += wd
    in_dim = acc

    core = lax.axis_index("c").astype(jnp.int32).reshape(1)
    small_shapes = [meta_tokens.shape, ssd_conv_w.shape[1:], ffn_conv_w.shape[1:]]
    small_local = _pack([meta_tokens, ssd_conv_w[0], ffn_conv_w[0]])
    first_local = [w_in[0].astype(WIRE_DTYPE), small_local]
    first_half = _run_comm(_gather_ici(first_local), "gather_w_in_ici")
    g_in, g_small = [_with_own(g, own, chip)
                     for g, own in zip(_run_comm(_gather_d2d(first_half), "gather_w_in_d2d"), first_local)]
    rest_local = [a[0].astype(WIRE_DTYPE) for a in (w_branch_ssd, w_branch_ret, w_out, w_up, w_down)]
    w_in_full = jnp.moveaxis(g_in, 0, 1).reshape(d, in_dim)
    pieces = []
    for nm, wd in order:
        o, rw = ref_off[nm]
        p = w_in_full[:, o:o + rw]
        if rw < wd:
            p = jnp.pad(p, ((0, 0), (0, wd - rw)))
        pieces.append(p)
    w_p = jnp.concatenate(pieces, axis=1)
    smalls = [_unpack(g_small[j], small_shapes) for j in range(4)]
    meta_full = jnp.concatenate([s[0] for s in smalls], axis=1)
    scw = jnp.concatenate([s[1] for s in smalls], axis=1)
    fcw = jnp.concatenate([s[2] for s in smalls], axis=1)
    scb, fcb = ssd_conv_b, ffn_conv_b
    dtb, alog, dvec = _pad_lanes(ssd_dt_bias), _pad_lanes(ssd_A_log), _pad_lanes(ssd_D)
    fin_w = final_norm_w.reshape(1, d)

    hq = d // RET_HEADS
    pos = jnp.arange(t, dtype=F32) - FRONT
    inv_freq = ROPE_BASE ** (-jnp.linspace(0.0, 1.0, hq // 2, dtype=F32))
    ang = pos[:, None] * inv_freq[None, :]
    cos, sin = jnp.cos(ang), jnp.sin(ang)

    h0 = jnp.concatenate([jnp.zeros((FRONT, d), F32), meta_full, x[0]], axis=0)
    tm = _row_tile(t)
    tmb = _pick(t, (1664, 1280, 640, 512, 384, 256, 128))
    u1 = _rms_fwd(h0, mix_norm_w, "rms1_fwd")
    proj = _mm(u1, w_p, "nn", F32, "proj", tmb, _pick(wp, (1920, 1536, 1280, 1024, 896, 768, 640, 512, 384, 256, 128)), d)
    xa = _ssd_conv_fwd(proj, col["xbc"], cw, scw, scb, "ssd_conv_fwd")
    res = _ssd_fwd(xa, proj, col["dt"], col["z"], dtb, alog, dvec, ssd_norm_w, di, "ssd_fwd", comm=_gather_ici(rest_local))
    (ypre, yssd, st_ssd), rest_half = res[:3], res[3:]
    res = _ret_fwd(proj, col["q"], col["k"], col["v"], col["g"], cos, sin, d, "ret_fwd", comm=_gather_d2d(rest_half))
    o_ret, yret, qr, kr, st_ret = res[:5]
    g_bs, g_br, g_out, g_up, g_down = [_with_own(g, own, chip) for g, own in zip(res[5:], rest_local)]
    w_bs = g_bs.reshape(di, d)
    w_br = g_br.reshape(di, d)
    w_o = g_out.reshape(d, d)
    w_u = jnp.moveaxis(g_up, 0, 1).reshape(d, 2 * f)
    w_d = g_down.reshape(f, d)
    tn_d = _pick(d, (1024, 512, 256, 128))
    bs = _mm(yssd, w_bs, "nn", F32, "branch_ssd", tmb, tn_d, _pick(di, (1024, 512, 256)))
    br = _mm(yret, w_br, "nn", F32, "branch_ret", tmb, tn_d, _pick(di, (1024, 512, 256)))
    merged = _gate_fwd(bs, br, proj, col["gs"], col["gr"], "gate_fwd")
    h1 = _mm(merged, w_o, "nn", F32, "out_proj", tmb, tn_d, d, res=h0)
    u2 = _rms_fwd(h1, ffn_norm_w, "rms2_fwd")
    tn_f = _pick(2 * f, (1408, 1024, 768, 512, 256, 128))
    up = _mm(u2, w_u, "nn", F32, "up_proj", tmb, tn_f, d)
    act = _ffn_conv_fwd(up, fcw, fcb, "ffn_conv_fwd")
    tk_f = _pick(f, (1408, 768, 704, 512, 256, 128))
    h2 = _mm(act, w_d, "nn", F32, "down_proj", tmb, tn_d, tk_f, res=h1)
    loss8, d_h2, g_fin = _loss_bwd(h2, fin_w, loss_target[0], "loss_head")

    tkt = _pick(t, (1664, 1280, 1024, 640, 512, 384, 256, 128))
    d_act = _mm(d_h2, w_d, "nt", F32, "d_act", tmb, tk_f, d)
    g_wd = _mm(act, d_h2, "tn", F32, "g_w_down", tk_f, tn_d, tkt)
    d_upg, d_upv, g_fcwg, g_fcwv, g_fcbg, g_fcbv = _ffn_conv_bwd(up, fcw, fcb, d_act, "ffn_conv_bwd")
    g_fcw = jnp.concatenate([g_fcwg, g_fcwv], axis=1)
    g_fcb = jnp.concatenate([g_fcbg, g_fcbv], axis=1)
    d_u2 = _mm(d_upg, w_u[:, :f], "nt", F32, "d_u2_gate", tmb, tn_d, tk_f)
    d_u2 = _mm(d_upv, w_u[:, f:], "nt", F32, "d_u2_value", tmb, tn_d, tk_f, res=d_u2)
    g_wu = jnp.concatenate([_mm(u2, d_upg, "tn", F32, "g_w_up_gate", tn_d, tk_f, tkt),
                            _mm(u2, d_upv, "tn", F32, "g_w_up_value", tn_d, tk_f, tkt)], axis=1)
    d_h1, g_ffnw = _rms_bwd(h1, ffn_norm_w, d_u2, d_h2, "rms2_bwd")
    d_merged = _mm(d_h1, w_o, "nt", F32, "d_merged", tmb, tn_d, d)
    g_wo = _mm(merged, d_h1, "tn", F32, "g_w_out", tn_d, tn_d, tkt)
    d_bs, d_br, d_gsr = _gate_bwd(d_merged, bs, br, proj, col["gs"], col["gr"], "gate_bwd")
    tk_i = _pick(di, (1024, 512, 256))
    d_yssd = _mm(d_bs, w_bs, "nt", F32, "d_y_ssd", tmb, tk_i, d)
    g_wbs = _mm(yssd, d_bs, "tn", F32, "g_w_branch_ssd", tk_i, tn_d, tkt)
    d_yret = _mm(d_br, w_br, "nt", F32, "d_y_ret", tmb, tk_i, d)
    g_wbr = _mm(yret, d_br, "tn", F32, "g_w_branch_ret", tk_i, tn_d, tkt)

    early_names = ["w_branch_ssd", "w_branch_ret", "w_out", "w_up", "w_down"]
    early = [g_wbs.reshape(4, di // 4, d), g_wbr.reshape(4, di // 4, d), g_wo.reshape(4, d // 4, d),
             jnp.moveaxis(g_wu.reshape(d, 4, 2 * f // 4), 1, 0), g_wd.reshape(4, f // 4, d)]
    res = _ret_bwd(d_yret, proj, col["v"], col["g"], o_ret, qr, kr, st_ret, cos, sin, d, "ret_bwd", comm=_scatter_d2d(early))
    (dqk, dvg), early_sib = res[:2], res[2:]
    early_pair = [_pair_sum(g_, s_, core, "pair_" + nm) for g_, s_, nm in zip(early, early_sib, early_names)]
    res = _ssd_bwd(d_yssd, xa, proj, col["dt"], col["z"], ypre, st_ssd, dtb, alog, dvec, ssd_norm_w, di, "ssd_bwd",
                   comm=_scatter_ici(early_pair))
    (d_xa, dz, ddt, g_dtb, g_alog, g_dvec, g_snw), early_recv = res[:7], res[7:]
    early_mine = [_sum4(p, "sum4_" + nm) for p, nm in zip(early_recv, early_names)]
    res = _ssd_conv_bwd(proj, col["xbc"], cw, scw, scb, d_xa, "ssd_conv_bwd", comm=_sibling_swap(early_mine))
    (d_xbc, g_scw, g_scb), early_other = res[:3], res[3:]
    d_pieces = [("z", dz), ("v", dvg), ("xbc", d_xbc), ("q", dqk), ("gs", d_gsr), ("dt", ddt)]

    g_piece = {nm: _mm(u1, a, "tn", WIRE_DTYPE, "g_w_in_" + nm, tn_d, _pick(a.shape[1], (1024, 768, 512, 256, 128)), tkt)
               for nm, a in d_pieces}
    g_cols = dict(z=g_piece["z"], v=g_piece["v"][:, :di], g=g_piece["v"][:, di:], xbc=g_piece["xbc"],
                  q=g_piece["q"][:, :d], k=g_piece["q"][:, d:], gs=g_piece["gs"][:, :d], gr=g_piece["gs"][:, d:],
                  dt=g_piece["dt"])
    g_in_ref = jnp.concatenate([g_cols[nm][:, :rw] for nm, rw in ref_order], axis=1)
    sc_in = jnp.moveaxis(g_in_ref.reshape(d, 4, in_dim // 4), 1, 0)
    in_sib = _run_comm(_scatter_d2d([sc_in]), "scatter_w_in_d2d")[0]
    in_pair = _pair_sum(sc_in, in_sib, core, "pair_w_in")
    d_u1, (in_recv,) = _mm_pieces_nt([(a, col[nm]) for nm, a in d_pieces], w_p, "d_u1", tmb, tn_d, 512,
                                     comm=_scatter_ici([in_pair]))
    in_mine = _sum4(in_recv, "sum4_w_in")
    d_h0, g_mixw, in_other = _rms_bwd(h0, mix_norm_w, d_u1, d_h1, "rms1_bwd", comm=_sibling_swap([in_mine]))
    grad_x = d_h0[CHUNK:][None]
    g_meta = d_h0[FRONT:CHUNK]

    names = ["w_in"] + early_names
    mine_half = [in_mine] + early_mine
    other_half = [in_other] + list(early_other)
    big_w = [w_in, w_branch_ssd, w_branch_ret, w_out, w_up, w_down]
    big_m = [m_w_in, m_w_branch_ssd, m_w_branch_ret, m_w_out, m_w_up, m_w_down]
    big_v = [v_w_in, v_w_branch_ssd, v_w_branch_ret, v_w_out, v_w_up, v_w_down]
    big_out = {}
    for nm, w_, p_, s_, m_, v_ in zip(names, big_w, mine_half, other_half, big_m, big_v):
        res = _adam_big(w_[0], p_, s_, m_[0], v_[0], core, "adam_" + nm)
        big_out[nm] = [r[None] for r in res]

    small = [
        ("meta_tokens", g_meta, meta_tokens, m_meta_tokens, v_meta_tokens, True),
        ("mix_norm_w", g_mixw, mix_norm_w, m_mix_norm_w, v_mix_norm_w, False),
        ("ssd_conv_w", g_scw, ssd_conv_w[0], m_ssd_conv_w[0], v_ssd_conv_w[0], True),
        ("ssd_conv_b", g_scb, ssd_conv_b, m_ssd_conv_b, v_ssd_conv_b, False),
        ("ssd_dt_bias", g_dtb, ssd_dt_bias, m_ssd_dt_bias, v_ssd_dt_bias, False),
        ("ssd_A_log", g_alog, ssd_A_log, m_ssd_A_log, v_ssd_A_log, False),
        ("ssd_D", g_dvec, ssd_D, m_ssd_D, v_ssd_D, False),
        ("ssd_norm_w", g_snw, ssd_norm_w, m_ssd_norm_w, v_ssd_norm_w, False),
        ("ffn_norm_w", g_ffnw, ffn_norm_w, m_ffn_norm_w, v_ffn_norm_w, False),
        ("ffn_conv_w", g_fcw, ffn_conv_w[0], m_ffn_conv_w[0], v_ffn_conv_w[0], True),
        ("ffn_conv_b", g_fcb, ffn_conv_b, m_ffn_conv_b, v_ffn_conv_b, False),
        ("final_norm_w", g_fin, fin_w, m_final_norm_w.reshape(1, d), v_final_norm_w.reshape(1, d), False),
    ]
    gathered8 = _run_comm(_gather8([s[1] for s in small] + [loss8]), "gather_small_grads")
    items = [(g8,) + s[2:] for g8, s in zip(gathered8, small)] + [(gathered8[-1], None, None, None, False)]
    small_res = _adam_small(items, chip.astype(jnp.int32).reshape(1), "adam_small")
    loss = small_res[-1].reshape(())
    out_shape = dict(meta_tokens=meta_tokens.shape, ssd_conv_w=ssd_conv_w.shape, ffn_conv_w=ffn_conv_w.shape,
                     final_norm_w=final_norm_w.shape)
    small_out = {s[0]: [r.reshape(out_shape.get(s[0], r.shape)) for r in small_res[4 * p:4 * p + 4]]
                 for p, s in enumerate(small)}

    weights = ["meta_tokens", "mix_norm_w", "w_in", "ssd_conv_w", "ssd_conv_b", "ssd_dt_bias", "ssd_A_log", "ssd_D",
               "ssd_norm_w", "w_branch_ssd", "w_branch_ret", "w_out", "ffn_norm_w", "w_up", "ffn_conv_w", "ffn_conv_b",
               "w_down", "final_norm_w"]
    outs = [loss, grad_x]
    for kind in range(4):
        for nm in weights:
            outs.append(big_out[nm][kind] if nm in big_out else small_out[nm][kind])
    return tuple(outs)
```

```python
import functools
import math

import jax
import jax.numpy as jnp
import numpy as np
from jax import lax
from jax.experimental import pallas as pl
from jax.experimental.pallas import tpu as pltpu

F32 = jnp.float32
BF16 = jnp.bfloat16
MXU_DTYPE = BF16
WIRE_DTYPE = BF16

N_META = 16
CHUNK = 128
FRONT = CHUNK - N_META
EPS = 1e-6
SSD_HEAD_DIM = 64
SSD_GROUPS = 4
SSD_STATE = 128
SSD_CONV = 4
RET_HEADS = 4
ROPE_BASE = 10000.0
FFN_CONV = 3
LANES = 128
SUBLANES = 8
VMEM_LIMIT = 56 * 1024 * 1024

ADAM_LR = 0.001
ADAM_B1 = 0.9
ADAM_B2 = 0.999
ADAM_EPS = 1e-08
ADAM_WD = 0.01
ADAM_STEP = 10
MESH = pl.DeviceIdType.MESH


def _params(sem=None, vmem=VMEM_LIMIT):
    return pltpu.CompilerParams(dimension_semantics=sem, vmem_limit_bytes=vmem)


def _pick(n, cands):
    for c in cands:
        if n % c == 0:
            return c
    return n


def _silu(x):
    return x * jax.nn.sigmoid(x)


def _dsilu(x):
    s = jax.nn.sigmoid(x)
    return s * (1.0 + x * (1.0 - s))


def _dot(a, b, dims=(((1,), (0,)), ((), ()))):
    return lax.dot_general(a.astype(MXU_DTYPE), b.astype(MXU_DTYPE), dims, preferred_element_type=F32)


def _dot_nt(a, b):
    return _dot(a, b, (((1,), (1,)), ((), ())))


def _dot_tn(a, b):
    return _dot(a, b, (((0,), (0,)), ((), ())))


def _dot01(a, b, split, npass, dims=(((1,), (0,)), ((), ()))):
    rest = (a if split == "a" else b).astype(F32)
    fixed = (b if split == "a" else a).astype(BF16)
    acc = None
    for p in range(npass):
        piece = rest.astype(BF16)
        ops = (piece, fixed) if split == "a" else (fixed, piece)
        term = lax.dot_general(ops[0], ops[1], dims, preferred_element_type=F32)
        acc = term if acc is None else acc + term
        if p + 1 < npass:
            rest = rest - piece.astype(F32)
    return acc


_NT = (((1,), (1,)), ((), ()))


def _iota(shape, dim):
    return lax.broadcasted_iota(jnp.int32, shape, dim)


def _shift_down(cur, prev8, k):
    if k == 0:
        return cur
    rolled = pltpu.roll(cur, k, 0)
    i8 = _iota((SUBLANES, cur.shape[1]), 0)
    head = jnp.where(i8 < k, pltpu.roll(prev8, k, 0), rolled[0:SUBLANES])
    return jnp.concatenate([head, rolled[SUBLANES:]], axis=0)


class _Comm:
    def __init__(self, ins, outs, nsem, make, in_place=False):
        self.ins, self.outs, self.nsem, self.make = list(ins), list(outs), nsem, make
        self.in_place = in_place


def _place():
    x, y, c = lax.axis_index("x"), lax.axis_index("y"), lax.axis_index("c")
    return x, y, c, 2 * x + y, [(1 - x, y), (x, 1 - y), (1 - x, 1 - y)]


def _call(body, name, grid, in_specs, out_specs, out_shape, scratch, sem, args, comm=None):
    if comm is None:
        return pl.pallas_call(body, name=name, grid=grid, in_specs=in_specs, out_specs=out_specs, out_shape=out_shape,
                              scratch_shapes=scratch, compiler_params=_params(sem))(*args)
    n_in, n_out, n_scr = len(in_specs), len(out_specs), len(scratch)
    ci, co = len(comm.ins), len(comm.outs)

    def wrapped(*refs):
        ins, refs = refs[:n_in], refs[n_in:]
        cins, refs = refs[:ci], refs[ci:]
        outs, refs = refs[:n_out], refs[n_out:]
        couts, refs = refs[:co], refs[co:]
        scr, sems = refs[:n_scr], refs[n_scr:]
        first = functools.reduce(jnp.logical_and, [pl.program_id(a) == 0 for a in range(len(grid))])
        last = functools.reduce(jnp.logical_and, [pl.program_id(a) == grid[a] - 1 for a in range(len(grid))])

        @pl.when(first)
        def _():
            for cp in comm.make(cins, couts, *sems):
                cp.start()

        body(*ins, *outs, *scr)

        @pl.when(last)
        def _():
            for cp in comm.make(cins, couts, *sems):
                cp.wait()

    anyspec = pl.BlockSpec(memory_space=pl.ANY)
    dma = pltpu.SemaphoreType.DMA((comm.nsem,))
    aliases = {n_in + i: n_out + i for i in range(ci)} if comm.in_place else {}
    return pl.pallas_call(
        wrapped, name=name, grid=grid, in_specs=list(in_specs) + [anyspec] * ci,
        out_specs=list(out_specs) + [anyspec] * co, out_shape=list(out_shape) + comm.outs,
        scratch_shapes=list(scratch) + [dma, dma, dma], input_output_aliases=aliases,
        compiler_params=_params(("arbitrary",) * len(grid)))(*args, *comm.ins)


def _run_comm(comm, name):
    ci, co = len(comm.ins), len(comm.outs)

    def body(*refs):
        cins, couts, sems = refs[:ci], refs[ci:ci + co], refs[ci + co:]
        for cp in comm.make(cins, couts, *sems):
            cp.start()
        for cp in comm.make(cins, couts, *sems):
            cp.wait()

    anyspec = pl.BlockSpec(memory_space=pl.ANY)
    dma = pltpu.SemaphoreType.DMA((comm.nsem,))
    aliases = {i: i for i in range(ci)} if comm.in_place else {}
    return pl.pallas_call(body, name=name, in_specs=[anyspec] * ci, out_specs=[anyspec] * co, out_shape=comm.outs,
                          scratch_shapes=[dma, dma, dma], input_output_aliases=aliases)(*comm.ins)


def _half_rows(c, rows):
    h = rows // 2
    return pl.ds(pl.multiple_of(c * h, 16), h)


def _gather_ici(arrays):
    for a in arrays:
        assert a.shape[0] % 32 == 0, a.shape

    def make(ins, outs, send, recv, loc):
        x, y, c, mine, peers = _place()
        cps = []
        for i, a in enumerate(arrays):
            half = _half_rows(c, a.shape[0])
            for k, (px, py) in enumerate(peers):
                cps.append(pltpu.make_async_remote_copy(
                    src_ref=ins[i].at[half], dst_ref=outs[i].at[mine, half], send_sem=send.at[3 * i + k],
                    recv_sem=recv.at[3 * i + k], device_id=(px, py, c), device_id_type=MESH))
        return cps

    outs = [jax.ShapeDtypeStruct((4,) + a.shape, a.dtype) for a in arrays]
    return _Comm(arrays, outs, 3 * len(arrays), make)


def _gather_d2d(bufs):
    def make(ins, outs, send, recv, loc):
        x, y, c, mine, peers = _place()
        cps = []
        for i, a in enumerate(bufs):
            half = _half_rows(c, a.shape[1])
            for k, (px, py) in enumerate(peers):
                mine_half = outs[i].at[2 * px + py, half]
                cps.append(pltpu.make_async_remote_copy(
                    src_ref=mine_half, dst_ref=mine_half, send_sem=send.at[3 * i + k], recv_sem=recv.at[3 * i + k],
                    device_id=(x, y, 1 - c), device_id_type=MESH))
        return cps

    outs = [jax.ShapeDtypeStruct(a.shape, a.dtype) for a in bufs]
    return _Comm(bufs, outs, 3 * len(bufs), make, in_place=True)


def _with_own(gathered, own, chip):
    return lax.dynamic_update_index_in_dim(gathered, own, chip, 0)


def _scatter_d2d(grads):
    for a in grads:
        assert a.shape[1] % 32 == 0, a.shape

    def make(ins, outs, send, recv, loc):
        x, y, c, mine, peers = _place()
        cps = []
        for i, a in enumerate(grads):
            other = _half_rows(1 - c, a.shape[1])
            cps.append(pltpu.make_async_remote_copy(
                src_ref=ins[i].at[:, other], dst_ref=outs[i], send_sem=send.at[i], recv_sem=recv.at[i],
                device_id=(x, y, 1 - c), device_id_type=MESH))
        return cps

    outs = [jax.ShapeDtypeStruct((4, a.shape[1] // 2, a.shape[2]), a.dtype) for a in grads]
    return _Comm(grads, outs, len(grads), make)


def _scatter_ici(parts):
    def make(ins, outs, send, recv, loc):
        x, y, c, mine, peers = _place()
        cps = []
        for i in range(len(parts)):
            cps.append(pltpu.make_async_copy(ins[i].at[mine], outs[i].at[mine], loc.at[i]))
            for k, (px, py) in enumerate(peers):
                cps.append(pltpu.make_async_remote_copy(
                    src_ref=ins[i].at[2 * px + py], dst_ref=outs[i].at[mine], send_sem=send.at[3 * i + k],
                    recv_sem=recv.at[3 * i + k], device_id=(px, py, c), device_id_type=MESH))
        return cps

    outs = [jax.ShapeDtypeStruct(a.shape, a.dtype) for a in parts]
    return _Comm(parts, outs, 3 * len(parts), make)


def _sibling_swap(arrays):
    def make(ins, outs, send, recv, loc):
        x, y, c, mine, peers = _place()
        return [pltpu.make_async_remote_copy(src_ref=ins[i], dst_ref=outs[i], send_sem=send.at[i], recv_sem=recv.at[i],
                                             device_id=(x, y, 1 - c), device_id_type=MESH) for i in range(len(arrays))]

    outs = [jax.ShapeDtypeStruct(a.shape, a.dtype) for a in arrays]
    return _Comm(arrays, outs, len(arrays), make)


def _mm(a, b, mode, out_dtype, name, tm, tn, tk, res=None, comm=None):
    if mode == "nn":
        (m, kd), n = a.shape, b.shape[1]
        a_spec = pl.BlockSpec((tm, tk), lambda i, j, k: (i, k))
        b_spec = pl.BlockSpec((tk, tn), lambda i, j, k: (k, j))
        dims = (((1,), (0,)), ((), ()))
    elif mode == "nt":
        (m, kd), n = a.shape, b.shape[0]
        a_spec = pl.BlockSpec((tm, tk), lambda i, j, k: (i, k))
        b_spec = pl.BlockSpec((tn, tk), lambda i, j, k: (j, k))
        dims = (((1,), (1,)), ((), ()))
    else:
        (kd, m), n = a.shape, b.shape[1]
        a_spec = pl.BlockSpec((tk, tm), lambda i, j, k: (k, i))
        b_spec = pl.BlockSpec((tk, tn), lambda i, j, k: (k, j))
        dims = (((0,), (0,)), ((), ()))
    assert m % tm == 0 and n % tn == 0 and kd % tk == 0, (name, m, n, kd, tm, tn, tk)
    nk = kd // tk
    has_res = res is not None
    in_place = out_dtype == F32

    def body(*refs):
        a_ref, b_ref = refs[:2]
        r_ref = refs[2] if has_res else None
        o_ref = refs[3 if has_res else 2]

        def finish(r):
            if has_res:
                r = r + r_ref[...].astype(F32)
            o_ref[...] = r.astype(out_dtype)

        if nk == 1:
            finish(_dot(a_ref[...], b_ref[...], dims))
            return
        k = pl.program_id(2)
        if in_place:
            @pl.when(k == 0)
            def _():
                finish(_dot(a_ref[...], b_ref[...], dims))

            @pl.when(k > 0)
            def _():
                o_ref[...] += _dot(a_ref[...], b_ref[...], dims)
            return
        acc = refs[-1]

        @pl.when(k == 0)
        def _():
            acc[...] = _dot(a_ref[...], b_ref[...], dims)

        @pl.when((k > 0) & (k < nk - 1))
        def _():
            acc[...] += _dot(a_ref[...], b_ref[...], dims)

        @pl.when(k == nk - 1)
        def _():
            finish(acc[...] + _dot(a_ref[...], b_ref[...], dims))

    in_specs = [a_spec, b_spec]
    args = [a, b]
    if has_res:
        in_specs.append(pl.BlockSpec((tm, tn), lambda i, j, k: (i, j)))
        args.append(res)
    res = _call(body, name, (m // tm, n // tn, nk), in_specs, [pl.BlockSpec((tm, tn), lambda i, j, k: (i, j))],
                [jax.ShapeDtypeStruct((m, n), out_dtype)], [] if nk == 1 or in_place else [pltpu.VMEM((tm, tn), F32)],
                ("parallel", "parallel", "arbitrary"), args, comm)
    return res[0] if comm is None else (res[0], res[1:])


def _mm_pieces_nt(pieces, b, name, tm, tn, tk_max, comm=None):
    m, n = pieces[0][0].shape[0], b.shape[0]
    out_dtype = F32
    cands = [c for c in (1024, 512, 256, 128) if c <= tk_max]
    tks = [_pick(math.gcd(a.shape[1], c0) if c0 else a.shape[1], cands) for a, c0 in pieces]
    nks = [a.shape[1] // tk for (a, _), tk in zip(pieces, tks)]
    starts = [sum(nks[:p]) for p in range(len(pieces))]
    ktot = sum(nks)
    npc = len(pieces)

    def body(*refs):
        a_refs, b_refs, o_ref = refs[:npc], refs[npc:2 * npc], refs[2 * npc]
        k = pl.program_id(2)

        @pl.when(k == 0)
        def _():
            o_ref[...] = _dot_nt(a_refs[0][...], b_refs[0][...])

        for p in range(npc):
            @pl.when((k >= max(starts[p], 1)) & (k < starts[p] + nks[p]))
            def _(p=p):
                o_ref[...] += _dot_nt(a_refs[p][...], b_refs[p][...])

    def a_spec(p):
        return pl.BlockSpec((tm, tks[p]), lambda i, j, k: (i, jnp.clip(k - starts[p], 0, nks[p] - 1)))

    def b_spec(p):
        c0 = pieces[p][1] // tks[p]
        return pl.BlockSpec((tn, tks[p]), lambda i, j, k: (j, c0 + jnp.clip(k - starts[p], 0, nks[p] - 1)))

    res = _call(body, name, (m // tm, n // tn, ktot), [a_spec(p) for p in range(npc)] + [b_spec(p) for p in range(npc)],
                [pl.BlockSpec((tm, tn), lambda i, j, k: (i, j))], [jax.ShapeDtypeStruct((m, n), out_dtype)],
                [], ("parallel", "parallel", "arbitrary"), [a for a, _ in pieces] + [b] * npc, comm)
    return res[0] if comm is None else (res[0], res[1:])


def _rms_fwd(h, w, name):
    t, d = h.shape
    tr = _pick(t, (640, 512, 384, 256, 128))

    def body(h_ref, w_ref, u_ref):
        x = h_ref[...]
        r = lax.rsqrt(jnp.mean(x * x, axis=1, keepdims=True) + EPS)
        u_ref[...] = (x * r * w_ref[...]).astype(MXU_DTYPE)

    return pl.pallas_call(
        body, name=name, grid=(t // tr,),
        in_specs=[pl.BlockSpec((tr, d), lambda i: (i, 0)), pl.BlockSpec((1, d), lambda i: (0, 0))],
        out_specs=pl.BlockSpec((tr, d), lambda i: (i, 0)),
        out_shape=jax.ShapeDtypeStruct((t, d), MXU_DTYPE),
        compiler_params=_params(("parallel",)),
    )(h, w)


def _rms_bwd(h, w, du, res, name, comm=None):
    t, d = h.shape
    tr = _pick(t, (640, 512, 384, 256, 128))

    def body(h_ref, w_ref, du_ref, res_ref, dh_ref, gw_ref):
        @pl.when(pl.program_id(0) == 0)
        def _():
            gw_ref[...] = jnp.zeros_like(gw_ref)

        x = h_ref[...]
        r = lax.rsqrt(jnp.mean(x * x, axis=1, keepdims=True) + EPS)
        xhat = x * r
        dy = du_ref[...].astype(F32)
        dxh = dy * w_ref[...]
        dh = r * (dxh - xhat * jnp.mean(dxh * xhat, axis=1, keepdims=True))
        dh_ref[...] = dh + res_ref[...]
        gw_ref[...] += jnp.sum(dy * xhat, axis=0, keepdims=True)

    row = pl.BlockSpec((tr, d), lambda i: (i, 0))
    vec = pl.BlockSpec((1, d), lambda i: (0, 0))
    return _call(body, name, (t // tr,), [row, vec, row, row], [row, vec],
                 [jax.ShapeDtypeStruct((t, d), F32), jax.ShapeDtypeStruct((1, d), F32)], [], ("arbitrary",),
                 (h, w, du, res), comm)


def _loss_bwd(h2, w, target, name):
    t, d = h2.shape
    nc = t // CHUNK

    def body(h_ref, w_ref, tg_ref, loss_ref, dh_ref, gw_ref):
        i = pl.program_id(0)

        @pl.when(i == 0)
        def _():
            gw_ref[...] = jnp.zeros_like(gw_ref)
            loss_ref[...] = jnp.zeros_like(loss_ref)
            dh_ref[...] = jnp.zeros_like(dh_ref)

        @pl.when(i > 0)
        def _():
            x = h_ref[...]
            r = lax.rsqrt(jnp.mean(x * x, axis=1, keepdims=True) + EPS)
            xhat = x * r
            diff = xhat * w_ref[...] - tg_ref[...]
            loss_ref[...] += 0.5 * jnp.sum(jnp.sum(diff * diff, axis=1, keepdims=True), axis=0, keepdims=True) / d
            dy = diff / d
            dxh = dy * w_ref[...]
            dh_ref[...] = r * (dxh - xhat * jnp.mean(dxh * xhat, axis=1, keepdims=True))
            gw_ref[...] += jnp.sum(dy * xhat, axis=0, keepdims=True)

    row = pl.BlockSpec((CHUNK, d), lambda i: (i, 0))
    vec = pl.BlockSpec((1, d), lambda i: (0, 0))
    return pl.pallas_call(
        body, name=name, grid=(nc,),
        in_specs=[row, vec, pl.BlockSpec((CHUNK, d), lambda i: (jnp.maximum(i - 1, 0), 0))],
        out_specs=[pl.BlockSpec((SUBLANES, LANES), lambda i: (0, 0)), row, vec],
        out_shape=[jax.ShapeDtypeStruct((SUBLANES, LANES), F32), jax.ShapeDtypeStruct((t, d), F32),
                   jax.ShapeDtypeStruct((1, d), F32)],
        compiler_params=_params(("arbitrary",)),
    )(h2, w, target)


def _conv_tile(cur, prev8, w_ref, b_ref, kw):
    y = b_ref[...] + cur * w_ref[kw - 1:kw, :]
    for k in range(kw - 1):
        y = y + _shift_down(cur, prev8, kw - 1 - k) * w_ref[k:k + 1, :]
    return y


_SUB = 16


def _sub_rows(s):
    return pl.ds(0 if isinstance(s, int) else pl.multiple_of(s * _SUB, _SUB), _SUB)


def _window(x_ref, prev8, s):
    if isinstance(s, int):
        return jnp.concatenate([prev8, x_ref[0:_SUB, :]], axis=0)
    return x_ref[pl.ds(pl.multiple_of(s * _SUB - SUBLANES, SUBLANES), _SUB + SUBLANES), :]


def _conv_step(win, w, b, kw):
    taps = [win[SUBLANES:] if k == kw - 1 else pltpu.roll(win, kw - 1 - k, 0)[SUBLANES:] for k in range(kw)]
    y = b + taps[kw - 1] * w[kw - 1:kw, :]
    for k in range(kw - 1):
        y = y + taps[k] * w[k:k + 1, :]
    return y, taps


def _conv_dx_step(dpre, next8, w, kw):
    n = _SUB + SUBLANES
    win = jnp.concatenate([dpre, next8], axis=0)
    acc = dpre * w[kw - 1:kw, :]
    for k in range(kw - 1):
        acc = acc + pltpu.roll(win, n - (kw - 1 - k), 0)[0:_SUB] * w[k:k + 1, :]
    return acc


def _fold8(v):
    return functools.reduce(jnp.add, [v[r:r + SUBLANES] for r in range(0, _SUB, SUBLANES)])


def _row_tile(t):
    return _pick(t, (640, 512, 384, 256, 128))


def _ssd_conv_fwd(proj, col0, width, w, b, name):
    t = proj.shape[0]
    kw = w.shape[0]
    tr, tc = _row_tile(t), _pick(width, (512, 256, 128))
    c0, rb = col0 // tc, tr // SUBLANES
    assert col0 % tc == 0

    def body(x_ref, p_ref, w_ref, b_ref, o_ref):
        i = pl.program_id(1)
        prev8 = jnp.where(i > 0, p_ref[...], 0.0)
        pre = _conv_tile(x_ref[...], prev8, w_ref, b_ref, kw)
        rows = _iota((tr, 1), 0) + i * tr
        o_ref[...] = jnp.where(rows >= FRONT, _silu(pre), 0.0)

    return pl.pallas_call(
        body, name=name, grid=(width // tc, t // tr),
        in_specs=[pl.BlockSpec((tr, tc), lambda j, i: (i, c0 + j)),
                  pl.BlockSpec((SUBLANES, tc), lambda j, i: (jnp.maximum(i * rb - 1, 0), c0 + j)),
                  pl.BlockSpec((kw, tc), lambda j, i: (0, j)),
                  pl.BlockSpec((1, tc), lambda j, i: (0, j))],
        out_specs=pl.BlockSpec((tr, tc), lambda j, i: (i, j)),
        out_shape=jax.ShapeDtypeStruct((t, width), F32),
        compiler_params=_params(("parallel", "parallel")),
    )(proj, proj, w, b)


def _ssd_conv_bwd(proj, col0, width, w, b, dact, name, comm=None):
    t = proj.shape[0]
    kw = w.shape[0]
    tr, tc = _row_tile(t), _pick(width, (512, 256, 128))
    c0, rb, nrow = col0 // tc, tr // SUBLANES, t // tr

    def body(x_ref, p_ref, w_ref, b_ref, d_ref, o_ref, gw_ref, gb_ref, carry):
        i = pl.program_id(1)
        ti = nrow - 1 - i

        @pl.when(i == 0)
        def _():
            gw_ref[...] = jnp.zeros_like(gw_ref)
            gb_ref[...] = jnp.zeros_like(gb_ref)
            carry[...] = jnp.zeros_like(carry)

        w, b = w_ref[...], b_ref[...]
        prev8 = jnp.where(ti > 0, p_ref[...], 0.0)
        nsub = tr // _SUB

        def step(s, state):
            next8, gb8, gw8 = state
            pre, taps = _conv_step(_window(x_ref, prev8, s), w, b, kw)
            valid = _iota((_SUB, 1), 0) + (ti * tr + s * _SUB) >= FRONT
            dpre = jnp.where(valid, d_ref[_sub_rows(s), :] * _dsilu(pre), 0.0)
            o_ref[_sub_rows(s), :] = jnp.where(valid, _conv_dx_step(dpre, next8, w, kw), 0.0).astype(MXU_DTYPE)
            return (dpre[0:SUBLANES], gb8 + _fold8(dpre), tuple(g + _fold8(dpre * tp) for g, tp in zip(gw8, taps)))

        zero8 = jnp.zeros((SUBLANES, tc), F32)
        state = lax.fori_loop(0, nsub - 1, lambda n, st: step(nsub - 1 - n, st), (carry[...], zero8, (zero8,) * kw))
        next8, gb8, gw8 = step(0, state)
        carry[...] = next8
        gb_ref[...] += jnp.sum(gb8, axis=0, keepdims=True)
        for k in range(kw):
            gw_ref[k:k + 1, :] += jnp.sum(gw8[k], axis=0, keepdims=True)

    return _call(
        body, name, (width // tc, nrow),
        [pl.BlockSpec((tr, tc), lambda j, i: (nrow - 1 - i, c0 + j)),
         pl.BlockSpec((SUBLANES, tc), lambda j, i: (jnp.maximum((nrow - 1 - i) * rb - 1, 0), c0 + j)),
         pl.BlockSpec((kw, tc), lambda j, i: (0, j)),
         pl.BlockSpec((1, tc), lambda j, i: (0, j)),
         pl.BlockSpec((tr, tc), lambda j, i: (nrow - 1 - i, j))],
        [pl.BlockSpec((tr, tc), lambda j, i: (nrow - 1 - i, j)),
         pl.BlockSpec((SUBLANES, tc), lambda j, i: (0, j)),
         pl.BlockSpec((1, tc), lambda j, i: (0, j))],
        [jax.ShapeDtypeStruct((t, width), MXU_DTYPE), jax.ShapeDtypeStruct((SUBLANES, width), F32),
         jax.ShapeDtypeStruct((1, width), F32)],
        [pltpu.VMEM((SUBLANES, tc), F32)], ("parallel", "arbitrary"), (proj, proj, w, b, dact), comm)


def _ffn_conv_fwd(up, w, b, name):
    t, f2 = up.shape
    f = f2 // 2
    kw = w.shape[0]
    tr, tc = _row_tile(t), _pick(f, (256, 128))
    nf, rb = f // tc, tr // SUBLANES

    def body(xg, pg, xv, pv, wg, wv, bg, bv, o_ref):
        i = pl.program_id(1)
        ag = _conv_tile(xg[...], jnp.where(i > 0, pg[...], 0.0), wg, bg, kw)
        av = _conv_tile(xv[...], jnp.where(i > 0, pv[...], 0.0), wv, bv, kw)
        o_ref[...] = (_silu(ag) * av).astype(MXU_DTYPE)

    def cur(off):
        return pl.BlockSpec((tr, tc), lambda j, i: (i, j + off))

    def prev(off):
        return pl.BlockSpec((SUBLANES, tc), lambda j, i: (jnp.maximum(i * rb - 1, 0), j + off))

    def par(rows, off):
        return pl.BlockSpec((rows, tc), lambda j, i: (0, j + off))

    return pl.pallas_call(
        body, name=name, grid=(nf, t // tr),
        in_specs=[cur(0), prev(0), cur(nf), prev(nf), par(kw, 0), par(kw, nf), par(1, 0), par(1, nf)],
        out_specs=pl.BlockSpec((tr, tc), lambda j, i: (i, j)),
        out_shape=jax.ShapeDtypeStruct((t, f), MXU_DTYPE),
        compiler_params=_params(("parallel", "parallel")),
    )(up, up, up, up, w, w, b, b)


def _ffn_conv_bwd(up, w, b, dact, name):
    t, f2 = up.shape
    f = f2 // 2
    kw = w.shape[0]
    tr, tc = _row_tile(t), _pick(f, (256, 128))
    nf, rb, nrow = f // tc, tr // SUBLANES, t // tr

    def body(xg, pg, xv, pv, wg_ref, wv_ref, bg_ref, bv_ref, d_ref, og_ref, ov_ref, gwg_ref, gwv_ref, gbg_ref, gbv_ref,
             cg, cv):
        i = pl.program_id(1)
        ti = nrow - 1 - i

        @pl.when(i == 0)
        def _():
            for r in (gwg_ref, gwv_ref, gbg_ref, gbv_ref, cg, cv):
                r[...] = jnp.zeros_like(r)

        wg, wv, bg, bv = wg_ref[...], wv_ref[...], bg_ref[...], bv_ref[...]
        p8g, p8v = jnp.where(ti > 0, pg[...], 0.0), jnp.where(ti > 0, pv[...], 0.0)
        nsub = tr // _SUB

        def step(s, state):
            ng, nv, gbg8, gbv8, gwg8, gwv8 = state
            ag, tg = _conv_step(_window(xg, p8g, s), wg, bg, kw)
            av, tv = _conv_step(_window(xv, p8v, s), wv, bv, kw)
            d = d_ref[_sub_rows(s), :]
            sg = jax.nn.sigmoid(ag)
            dag = d * av * (sg * (1.0 + ag * (1.0 - sg)))
            dav = d * (ag * sg)
            valid = _iota((_SUB, 1), 0) + (ti * tr + s * _SUB) >= FRONT
            og_ref[_sub_rows(s), :] = jnp.where(valid, _conv_dx_step(dag, ng, wg, kw), 0.0).astype(MXU_DTYPE)
            ov_ref[_sub_rows(s), :] = jnp.where(valid, _conv_dx_step(dav, nv, wv, kw), 0.0).astype(MXU_DTYPE)
            return (dag[0:SUBLANES], dav[0:SUBLANES], gbg8 + _fold8(dag), gbv8 + _fold8(dav),
                    tuple(g + _fold8(dag * tp) for g, tp in zip(gwg8, tg)),
                    tuple(g + _fold8(dav * tp) for g, tp in zip(gwv8, tv)))

        zero8 = jnp.zeros((SUBLANES, tc), F32)
        state = lax.fori_loop(0, nsub - 1, lambda n, st: step(nsub - 1 - n, st),
                              (cg[...], cv[...], zero8, zero8, (zero8,) * kw, (zero8,) * kw))
        ng, nv, gbg8, gbv8, gwg8, gwv8 = step(0, state)
        cg[...] = ng
        cv[...] = nv
        gbg_ref[...] += jnp.sum(gbg8, axis=0, keepdims=True)
        gbv_ref[...] += jnp.sum(gbv8, axis=0, keepdims=True)
        for k in range(kw):
            gwg_ref[k:k + 1, :] += jnp.sum(gwg8[k], axis=0, keepdims=True)
            gwv_ref[k:k + 1, :] += jnp.sum(gwv8[k], axis=0, keepdims=True)

    def cur(off):
        return pl.BlockSpec((tr, tc), lambda j, i: (nrow - 1 - i, j + off))

    def prev(off):
        return pl.BlockSpec((SUBLANES, tc), lambda j, i: (jnp.maximum((nrow - 1 - i) * rb - 1, 0), j + off))

    def par(rows, off):
        return pl.BlockSpec((rows, tc), lambda j, i: (0, j + off))

    acc8 = pl.BlockSpec((SUBLANES, tc), lambda j, i: (0, j))
    acc1 = pl.BlockSpec((1, tc), lambda j, i: (0, j))
    return pl.pallas_call(
        body, name=name, grid=(nf, nrow),
        in_specs=[cur(0), prev(0), cur(nf), prev(nf), par(kw, 0), par(kw, nf), par(1, 0), par(1, nf), cur(0)],
        out_specs=[cur(0), cur(0), acc8, acc8, acc1, acc1],
        out_shape=[jax.ShapeDtypeStruct((t, f), MXU_DTYPE), jax.ShapeDtypeStruct((t, f), MXU_DTYPE),
                   jax.ShapeDtypeStruct((SUBLANES, f), F32), jax.ShapeDtypeStruct((SUBLANES, f), F32),
                   jax.ShapeDtypeStruct((1, f), F32), jax.ShapeDtypeStruct((1, f), F32)],
        scratch_shapes=[pltpu.VMEM((SUBLANES, tc), F32), pltpu.VMEM((SUBLANES, tc), F32)],
        compiler_params=_params(("parallel", "arbitrary")),
    )(up, up, up, up, w, w, b, b, dact)


def _gate_fwd(bs, br, proj, c_gs, c_gr, name):
    t, d = bs.shape
    tr = _row_tile(t)

    def body(bs_ref, br_ref, gs_ref, gr_ref, o_ref):
        o_ref[...] = (jax.nn.sigmoid(gs_ref[...]) * bs_ref[...] + jax.nn.sigmoid(gr_ref[...]) * br_ref[...]).astype(MXU_DTYPE)

    row = pl.BlockSpec((tr, d), lambda i: (i, 0))
    return pl.pallas_call(
        body, name=name, grid=(t // tr,),
        in_specs=[row, row, pl.BlockSpec((tr, d), lambda i: (i, c_gs // d)), pl.BlockSpec((tr, d), lambda i: (i, c_gr // d))],
        out_specs=row, out_shape=jax.ShapeDtypeStruct((t, d), MXU_DTYPE),
        compiler_params=_params(("parallel",)),
    )(bs, br, proj, proj)


def _gate_bwd(dm, bs, br, proj, c_gs, c_gr, name):
    t, d = bs.shape
    tr = _row_tile(t)

    def body(dm_ref, bs_ref, br_ref, gs_ref, gr_ref, dbs_ref, dbr_ref, dgg_ref):
        g = dm_ref[...]
        ss, sr = jax.nn.sigmoid(gs_ref[...]), jax.nn.sigmoid(gr_ref[...])
        dbs_ref[...] = (g * ss).astype(MXU_DTYPE)
        dbr_ref[...] = (g * sr).astype(MXU_DTYPE)
        dgg_ref[:, :d] = (g * bs_ref[...] * ss * (1.0 - ss)).astype(MXU_DTYPE)
        dgg_ref[:, d:] = (g * br_ref[...] * sr * (1.0 - sr)).astype(MXU_DTYPE)

    row = pl.BlockSpec((tr, d), lambda i: (i, 0))
    out = jax.ShapeDtypeStruct((t, d), MXU_DTYPE)
    return pl.pallas_call(
        body, name=name, grid=(t // tr,),
        in_specs=[row, row, row, pl.BlockSpec((tr, d), lambda i: (i, c_gs // d)), pl.BlockSpec((tr, d), lambda i: (i, c_gr // d))],
        out_specs=[row, row, pl.BlockSpec((tr, 2 * d), lambda i: (i, 0))],
        out_shape=[out, out, jax.ShapeDtypeStruct((t, 2 * d), MXU_DTYPE)],
        compiler_params=_params(("parallel",)),
    )(dm, bs, br, proj, proj)


def _ret_consts(h):
    lg = math.log(1.0 - 2.0 ** (-5.0 - h))
    l = _iota((CHUNK, 1), 0).astype(F32)
    diff = l - _iota((1, CHUNK), 1).astype(F32)
    dm = jnp.exp(jnp.where(diff >= 0, diff * lg, -jnp.inf))
    dmt = jnp.exp(jnp.where(diff <= 0, -diff * lg, -jnp.inf))
    cs = jnp.exp((l + 1.0) * lg)
    kdec = jnp.exp((CHUNK - 1.0 - l) * lg)
    return dm, dmt, cs, kdec, math.exp(CHUNK * lg)


def _ret_fwd(proj, c_q, c_k, c_v, c_g, cos, sin, d, name, comm=None):
    t = proj.shape[0]
    nc = t // CHUNK
    hq, hv = d // RET_HEADS, 2 * d // RET_HEADS
    half = hq // 2
    scale = hq ** -0.5

    def body(q_ref, k_ref, v_ref, g_ref, cos_ref, sin_ref, o_ref, y_ref, qr_ref, kr_ref, st_ref, rs):
        @pl.when(pl.program_id(0) == 0)
        def _():
            rs[...] = jnp.zeros_like(rs)

        co, si = cos_ref[...], sin_ref[...]
        for h in range(RET_HEADS):
            dm, _, cs, kdec, gam = _ret_consts(h)
            q1, q2 = q_ref[:, h * hq:h * hq + half], q_ref[:, h * hq + half:(h + 1) * hq]
            k1, k2 = k_ref[:, h * hq:h * hq + half], k_ref[:, h * hq + half:(h + 1) * hq]
            qr = jnp.concatenate([q1 * co - q2 * si, q2 * co + q1 * si], axis=1)
            kr = jnp.concatenate([k1 * co - k2 * si, k2 * co + k1 * si], axis=1) * scale
            qr_ref[:, h * hq:(h + 1) * hq] = qr.astype(MXU_DTYPE)
            kr_ref[:, h * hq:(h + 1) * hq] = kr.astype(MXU_DTYPE)
            v = v_ref[:, h * hv:(h + 1) * hv]
            r_in = rs[h * hq:(h + 1) * hq, :]
            st_ref[0, h * hq:(h + 1) * hq, :] = r_in.astype(MXU_DTYPE)
            s = _dot_nt(qr, kr) * dm
            o = _dot(s, v) + cs * _dot(qr, r_in)
            rs[h * hq:(h + 1) * hq, :] = gam * r_in + _dot_tn(kr * kdec, v)
            o_ref[:, h * hv:(h + 1) * hv] = o
            on = o * lax.rsqrt(jnp.mean(o * o, axis=1, keepdims=True) + EPS)
            y_ref[:, h * hv:(h + 1) * hv] = (_silu(g_ref[:, h * hv:(h + 1) * hv]) * on).astype(MXU_DTYPE)

    def col(width, c0):
        return pl.BlockSpec((CHUNK, width), lambda i: (i, c0 // width))

    tab = pl.BlockSpec((CHUNK, half), lambda i: (i, 0))
    return _call(
        body, name, (nc,),
        [col(d, c_q), col(d, c_k), col(2 * d, c_v), col(2 * d, c_g), tab, tab],
        [col(2 * d, 0), col(2 * d, 0), col(d, 0), col(d, 0), pl.BlockSpec((1, d, hv), lambda i: (i, 0, 0))],
        [jax.ShapeDtypeStruct((t, 2 * d), F32), jax.ShapeDtypeStruct((t, 2 * d), MXU_DTYPE),
         jax.ShapeDtypeStruct((t, d), MXU_DTYPE), jax.ShapeDtypeStruct((t, d), MXU_DTYPE),
         jax.ShapeDtypeStruct((nc, d, hv), MXU_DTYPE)],
        [pltpu.VMEM((d, hv), F32)], ("arbitrary",), (proj, proj, proj, proj, cos, sin), comm)


def _ret_bwd(dy, proj, c_v, c_g, o, qr, kr, st, cos, sin, d, name, comm=None):
    t = proj.shape[0]
    nc = t // CHUNK
    hq, hv = d // RET_HEADS, 2 * d // RET_HEADS
    half = hq // 2
    scale = hq ** -0.5

    def body(dy_ref, v_ref, g_ref, o_ref, qr_ref, kr_ref, st_ref, cos_ref, sin_ref, dqk_ref, dvg_ref, drs):
        dq_ref, dk_ref = dqk_ref.at[:, pl.ds(0, d)], dqk_ref.at[:, pl.ds(d, d)]
        dv_ref, dg_ref = dvg_ref.at[:, pl.ds(0, 2 * d)], dvg_ref.at[:, pl.ds(2 * d, 2 * d)]

        @pl.when(pl.program_id(0) == 0)
        def _():
            drs[...] = jnp.zeros_like(drs)

        co, si = cos_ref[...], sin_ref[...]
        for h in range(RET_HEADS):
            dm, dmt, cs, kdec, gam = _ret_consts(h)
            vs = slice(h * hv, (h + 1) * hv)
            qs = slice(h * hq, (h + 1) * hq)
            o_h = o_ref[:, vs]
            g_h = g_ref[:, vs]
            d_y = dy_ref[:, vs]
            r = lax.rsqrt(jnp.mean(o_h * o_h, axis=1, keepdims=True) + EPS)
            on = o_h * r
            d_on = d_y * _silu(g_h)
            dg_ref[:, vs] = (d_y * on * _dsilu(g_h)).astype(MXU_DTYPE)
            d_o = r * (d_on - on * jnp.mean(d_on * on, axis=1, keepdims=True))
            q_h, k_h, v_h = qr_ref[:, qs], kr_ref[:, qs], v_ref[:, vs]
            r_in = st_ref[0, qs, :]
            dr_n = drs[qs, :]
            csdo = cs * d_o
            ds = _dot_nt(d_o, v_h) * dm
            dst = _dot_nt(v_h, d_o) * dmt
            s_t = _dot_nt(k_h, q_h) * dmt
            dqr = _dot(ds, k_h) + _dot_nt(csdo, r_in)
            dkr = _dot(dst, q_h) + kdec * _dot_nt(v_h, dr_n)
            dv_ref[:, vs] = (_dot(s_t, d_o) + _dot(k_h.astype(F32) * kdec, dr_n)).astype(MXU_DTYPE)
            drs[qs, :] = gam * dr_n + _dot_tn(q_h, csdo)
            a1, a2 = dqr[:, :half], dqr[:, half:]
            dq_ref[:, qs] = jnp.concatenate([a1 * co + a2 * si, a2 * co - a1 * si], axis=1).astype(MXU_DTYPE)
            b1, b2 = dkr[:, :half] * scale, dkr[:, half:] * scale
            dk_ref[:, qs] = jnp.concatenate([b1 * co + b2 * si, b2 * co - b1 * si], axis=1).astype(MXU_DTYPE)

    def col(width, c0=0):
        return pl.BlockSpec((CHUNK, width), lambda i: (nc - 1 - i, c0 // width))

    tab = pl.BlockSpec((CHUNK, half), lambda i: (nc - 1 - i, 0))
    return _call(
        body, name, (nc,),
        [col(2 * d), col(2 * d, c_v), col(2 * d, c_g), col(2 * d), col(d), col(d),
         pl.BlockSpec((1, d, hv), lambda i: (nc - 1 - i, 0, 0)), tab, tab],
        [col(2 * d), col(4 * d)],
        [jax.ShapeDtypeStruct((t, 2 * d), MXU_DTYPE), jax.ShapeDtypeStruct((t, 4 * d), MXU_DTYPE)],
        [pltpu.VMEM((d, hv), F32)], ("arbitrary",), (dy, proj, proj, o, qr, kr, st, cos, sin), comm)


def _ssd_small(dtraw_ref, dtb_ref, alog_ref, chunk_idx, nh):
    rows = _iota((CHUNK, 1), 0)
    ok = ((rows >= FRONT) | (chunk_idx > 0)) & (_iota((1, LANES), 1) < nh)
    z = dtraw_ref[...] + dtb_ref[...]
    dt = jnp.where(ok, jax.nn.softplus(z), 0.0)
    sig = jnp.where(ok, jax.nn.sigmoid(z), 0.0)
    a = jnp.where(_iota((1, LANES), 1) < nh, -jnp.exp(alog_ref[...]), 0.0)
    tri = (_iota((CHUNK, CHUNK), 0) >= _iota((CHUNK, CHUNK), 1)).astype(F32)
    acs = _dot01(tri, dt * a, "b", 3)
    return dt, sig, a, acs, acs.T


def _head_expand(g, hpg, gw):
    shift = int(math.log2(SSD_HEAD_DIM))
    return (_iota((LANES, gw), 0) == g * hpg + lax.shift_right_logical(_iota((LANES, gw), 1), shift)).astype(F32)


def _ssd_fwd(xa, proj, c_dt, c_z, dtb, alog, dvec, nw, di, name, comm=None):
    t = xa.shape[0]
    nc = t // CHUNK
    nh = di // SSD_HEAD_DIM
    hpg = nh // SSD_GROUPS
    gw = di // SSD_GROUPS
    n = SSD_STATE
    gn = SSD_GROUPS * n
    hd = SSD_HEAD_DIM

    def body(x_ref, b_ref, c_ref, dtraw_ref, z_ref, dtb_ref, alog_ref, d_ref, nw_ref,
             y_ref, ys_ref, st_ref, hts, xdt_s):
        c = pl.program_id(0)

        @pl.when(c == 0)
        def _():
            hts[...] = jnp.zeros_like(hts)

        dt, _, _, acs, acs_t = _ssd_small(dtraw_ref, dtb_ref, alog_ref, c, nh)
        tri = _iota((CHUNK, CHUNK), 0) >= _iota((CHUNK, CHUNK), 1)
        dvec8 = jnp.broadcast_to(d_ref[...], (SUBLANES, LANES))
        for g in range(SSD_GROUPS):
            gs = slice(g * gw, (g + 1) * gw)
            ns = slice(g * n, (g + 1) * n)
            e_mat = _head_expand(g, hpg, gw)
            ax = _dot01(acs, e_mat, "a", 3)
            dtx = _dot01(dt, e_mat, "a", 3)
            dx = _dot01(dvec8, e_mat, "a", 3)[0:1, :]
            xg, bg, cg = x_ref[:, gs], b_ref[:, ns], c_ref[:, ns]
            xdt = xg * dtx
            xdt_s[...] = xdt.astype(MXU_DTYPE)
            cb = _dot_nt(cg, bg)
            ht = hts[ns, :]
            st_ref[0, ns, :] = ht.astype(MXU_DTYPE)
            y_ref[:, gs] = jnp.exp(ax) * _dot(cg, ht) + dx * xg
            for hh in range(hpg):
                h = g * hpg + hh
                lmat = jnp.exp(jnp.where(tri, acs[:, h:h + 1] - acs_t[h:h + 1, :], -jnp.inf))
                hs = slice(g * gw + hh * hd, g * gw + (hh + 1) * hd)
                y_ref[:, hs] += _dot(cb * lmat, xdt_s[:, hh * hd:(hh + 1) * hd])
            aend = ax[CHUNK - 1:CHUNK, :]
            hts[ns, :] = jnp.exp(aend) * ht + _dot_tn(bg, xdt * jnp.exp(aend - ax))
        for g in range(SSD_GROUPS):
            gs = slice(g * gw, (g + 1) * gw)
            yz = y_ref[:, gs] * _silu(z_ref[:, gs])
            r = lax.rsqrt(jnp.mean(yz * yz, axis=1, keepdims=True) + EPS)
            ys_ref[:, gs] = (yz * r * nw_ref[:, gs]).astype(MXU_DTYPE)

    def col(width, c0, arr_is_xa=False):
        return pl.BlockSpec((CHUNK, width), lambda i: (i, c0 // width))

    vec = pl.BlockSpec((1, LANES), lambda i: (0, 0))
    assert di % gn == 0 and c_dt % LANES == 0 and c_z % di == 0
    return _call(
        body, name, (nc,),
        [col(di, 0), col(gn, di), col(gn, di + gn), col(LANES, c_dt), col(di, c_z), vec, vec, vec,
         pl.BlockSpec((1, di), lambda i: (0, 0))],
        [col(di, 0), col(di, 0), pl.BlockSpec((1, gn, gw), lambda i: (i, 0, 0))],
        [jax.ShapeDtypeStruct((t, di), F32), jax.ShapeDtypeStruct((t, di), MXU_DTYPE),
         jax.ShapeDtypeStruct((nc, gn, gw), MXU_DTYPE)],
        [pltpu.VMEM((gn, gw), F32), pltpu.VMEM((CHUNK, gw), MXU_DTYPE)], ("arbitrary",),
        (xa, xa, xa, proj, proj, dtb, alog, dvec, nw), comm)


def _ssd_bwd(dys, xa, proj, c_dt, c_z, ypre, st, dtb, alog, dvec, nw, di, name, comm=None):
    t = xa.shape[0]
    nc = t // CHUNK
    nh = di // SSD_HEAD_DIM
    hpg = nh // SSD_GROUPS
    gw = di // SSD_GROUPS
    n = SSD_STATE
    gn = SSD_GROUPS * n
    hd = SSD_HEAD_DIM

    def body(dys_ref, x_ref, b_ref, c_ref, dtraw_ref, z_ref, y_ref, st_ref, dtb_ref, alog_ref, d_ref, nw_ref,
             dxa_ref, dz_ref, ddt_ref, gb_ref, ga_ref, gd_ref, gnw_ref, dhts, dy_s, xdt_s, dxdt_s):
        i = pl.program_id(0)
        c = nc - 1 - i

        @pl.when(i == 0)
        def _():
            dhts[...] = jnp.zeros_like(dhts)
            gb_ref[...] = jnp.zeros_like(gb_ref)
            ga_ref[...] = jnp.zeros_like(ga_ref)
            gd_ref[...] = jnp.zeros_like(gd_ref)
            gnw_ref[...] = jnp.zeros_like(gnw_ref)

        dt, sig, a, acs, acs_t = _ssd_small(dtraw_ref, dtb_ref, alog_ref, c, nh)
        tri = _iota((CHUNK, CHUNK), 0) >= _iota((CHUNK, CHUNK), 1)
        triu = _iota((CHUNK, CHUNK), 0) <= _iota((CHUNK, CHUNK), 1)
        lane = _iota((1, LANES), 1)
        rows = _iota((CHUNK, 1), 0)
        head_row = _iota((LANES, 1), 0)
        dvec8 = jnp.broadcast_to(d_ref[...], (SUBLANES, LANES))
        da = jnp.zeros((CHUNK, LANES), F32)
        da_t = jnp.zeros((LANES, CHUNK), F32)
        ddt = jnp.zeros((CHUNK, LANES), F32)
        gd = jnp.zeros((1, LANES), F32)
        for g in range(SSD_GROUPS):
            gs = slice(g * gw, (g + 1) * gw)
            ns = slice(g * n, (g + 1) * n)
            y_g, z_g = y_ref[:, gs], z_ref[:, gs]
            sz = _silu(z_g)
            yz = y_g * sz
            r = lax.rsqrt(jnp.mean(yz * yz, axis=1, keepdims=True) + EPS)
            nrm = yz * r
            dyo = dys_ref[:, gs]
            gnw_ref[:, gs] += jnp.sum(dyo * nrm, axis=0, keepdims=True)
            dn = dyo * nw_ref[:, gs]
            dyz = r * (dn - nrm * jnp.mean(dn * nrm, axis=1, keepdims=True))
            dz_ref[:, gs] = (dyz * y_g * _dsilu(z_g)).astype(MXU_DTYPE)
            dy_g = dyz * sz
            dy_s[...] = dy_g.astype(MXU_DTYPE)
            e_mat = _head_expand(g, hpg, gw)
            ax = _dot01(acs, e_mat, "a", 3)
            dtx = _dot01(dt, e_mat, "a", 3)
            dx = _dot01(dvec8, e_mat, "a", 3)[0:1, :]
            xg, bg, cg = x_ref[:, gs], b_ref[:, ns], c_ref[:, ns]
            xdt = xg * dtx
            xdt_s[...] = xdt.astype(MXU_DTYPE)
            aend = ax[CHUNK - 1:CHUNK, :]
            e = jnp.exp(aend - ax)
            ea = jnp.exp(ax)
            eend = jnp.exp(aend)
            htp = st_ref[0, ns, :].astype(F32)
            dht = dhts[ns, :]
            cb = _dot_nt(cg, bg)
            q = _dot(bg, dht)
            dxdt_s[...] = e * q
            wl = e * q * xdt
            d_b = _dot_nt(e * xdt, dht)
            yi = ea * _dot(cg, htp)
            eady = ea * dy_g
            d_c = _dot_nt(eady, htp)
            t1 = jnp.sum(dht * htp, axis=0, keepdims=True) * eend
            dhts[ns, :] = eend * dht + _dot_tn(cg, eady)
            da = da + _dot01(dy_g * yi - wl, e_mat, "a", 3, _NT)
            tail = jnp.broadcast_to(jnp.sum(wl, axis=0, keepdims=True) + t1, (SUBLANES, gw))
            da_end = _dot01(tail, e_mat, "a", 3, _NT)[0:1, :]
            da = da + jnp.where(rows == CHUNK - 1, da_end, 0.0)
            dcb = jnp.zeros((CHUNK, CHUNK), F32)
            for hh in range(hpg):
                h = g * hpg + hh
                lmat = jnp.exp(jnp.where(tri, acs[:, h:h + 1] - acs_t[h:h + 1, :], -jnp.inf))
                hl = slice(hh * hd, (hh + 1) * hd)
                dy_h, xdt_h = dy_s[:, hl], xdt_s[:, hl]
                dxdt_s[:, hl] += _dot_tn(cb * lmat, dy_h)
                dml = _dot_nt(dy_h, xdt_h) * lmat
                dcb = dcb + dml
                gmat = dml * cb
                da = da + jnp.where(lane == h, jnp.sum(gmat, axis=1, keepdims=True), 0.0)
                da_t = da_t - jnp.where(head_row == h, jnp.sum(gmat, axis=0, keepdims=True), 0.0)
            d_c = d_c + _dot(dcb, bg)
            d_b = d_b + _dot_tn(dcb, cg)
            dxdt = dxdt_s[...]
            dxa_ref[:, gs] = dxdt * dtx + dx * dy_g
            dxa_ref[:, di + g * n:di + (g + 1) * n] = d_b
            dxa_ref[:, di + gn + g * n:di + gn + (g + 1) * n] = d_c
            ddt = ddt + _dot01(dxdt * xg, e_mat, "a", 3, _NT)
            gd8 = jnp.broadcast_to(jnp.sum(dy_g * xg, axis=0, keepdims=True), (SUBLANES, gw))
            gd = gd + _dot01(gd8, e_mat, "a", 3, _NT)[0:1, :]
        da = da + da_t.T
        triu_f = triu.astype(F32)
        ddta = _dot01(triu_f, da, "b", 3)
        ddt = ddt + ddta * a
        draw = ddt * sig
        ddt_ref[...] = draw.astype(MXU_DTYPE)
        gb_ref[...] += jnp.sum(draw, axis=0, keepdims=True)
        ga_ref[...] += jnp.sum(ddta * dt, axis=0, keepdims=True) * a
        gd_ref[...] += gd

    def col(width, c0):
        return pl.BlockSpec((CHUNK, width), lambda i: (nc - 1 - i, c0 // width))

    vec = pl.BlockSpec((1, LANES), lambda i: (0, 0))
    wide = pl.BlockSpec((1, di), lambda i: (0, 0))
    wa = di + 2 * gn
    return _call(
        body, name, (nc,),
        [col(di, 0), col(di, 0), col(gn, di), col(gn, di + gn), col(LANES, c_dt), col(di, c_z), col(di, 0),
         pl.BlockSpec((1, gn, gw), lambda i: (nc - 1 - i, 0, 0)), vec, vec, vec, wide],
        [col(wa, 0), col(di, 0), col(LANES, 0), vec, vec, vec, wide],
        [jax.ShapeDtypeStruct((t, wa), F32), jax.ShapeDtypeStruct((t, di), MXU_DTYPE),
         jax.ShapeDtypeStruct((t, LANES), MXU_DTYPE), jax.ShapeDtypeStruct((1, LANES), F32),
         jax.ShapeDtypeStruct((1, LANES), F32), jax.ShapeDtypeStruct((1, LANES), F32),
         jax.ShapeDtypeStruct((1, di), F32)],
        [pltpu.VMEM((gn, gw), F32), pltpu.VMEM((CHUNK, gw), MXU_DTYPE), pltpu.VMEM((CHUNK, gw), MXU_DTYPE),
         pltpu.VMEM((CHUNK, gw), F32)], ("arbitrary",),
        (dys, xa, xa, xa, proj, proj, ypre, st, dtb, alog, dvec, nw), comm)


def _adam_math(w, g, m, v):
    m2 = ADAM_B1 * m + (1.0 - ADAM_B1) * g
    v2 = ADAM_B2 * v + (1.0 - ADAM_B2) * (g * g)
    m_hat = m2 / (1.0 - ADAM_B1 ** ADAM_STEP)
    v_hat = v2 / (1.0 - ADAM_B2 ** ADAM_STEP)
    delta = -ADAM_LR * (m_hat / (jnp.sqrt(v_hat) + ADAM_EPS) + ADAM_WD * w)
    return delta, m2, v2


def _adam_big(w, g_mine, g_sib, m, v, core, name):
    r, c = w.shape
    h = r // 2
    tr = _pick(h, (128, 64, 32, 16, 8))
    nbh = h // tr

    def body(core_ref, w_ref, a_ref, b_ref, m_ref, v_ref, g_ref, d_ref, m2_ref, v2_ref):
        g = jnp.where(pl.program_id(0) // nbh == core_ref[0], a_ref[...], b_ref[...])
        delta, m2, v2 = _adam_math(w_ref[...], g, m_ref[...], v_ref[...])
        g_ref[...] = g
        d_ref[...] = delta
        m2_ref[...] = m2
        v2_ref[...] = v2

    blk = pl.BlockSpec((tr, c), lambda i, core_ref: (i, 0))
    hblk = pl.BlockSpec((tr, c), lambda i, core_ref: (i % nbh, 0))
    out = jax.ShapeDtypeStruct((r, c), F32)
    return pl.pallas_call(
        body, name=name,
        grid_spec=pltpu.PrefetchScalarGridSpec(num_scalar_prefetch=1, grid=(2 * nbh,),
                                               in_specs=[blk, hblk, hblk, blk, blk], out_specs=[blk] * 4),
        out_shape=[out] * 4, compiler_params=_params(("parallel",)),
    )(core, w, g_mine, g_sib, m, v)


def _pair_sum(g, sib, core, name):
    _, r, c = g.shape
    h = r // 2
    tr = _pick(h, (128, 64, 32, 16))
    nb = h // tr

    def body(core_ref, g_ref, s_ref, o_ref):
        o_ref[...] = (g_ref[...].astype(F32) + s_ref[...].astype(F32)).astype(WIRE_DTYPE)

    return pl.pallas_call(
        body, name=name,
        grid_spec=pltpu.PrefetchScalarGridSpec(
            num_scalar_prefetch=1, grid=(4, nb),
            in_specs=[pl.BlockSpec((1, tr, c), lambda j, i, core_ref: (j, core_ref[0] * nb + i, 0)),
                      pl.BlockSpec((1, tr, c), lambda j, i, core_ref: (j, i, 0))],
            out_specs=pl.BlockSpec((1, tr, c), lambda j, i, core_ref: (j, i, 0))),
        out_shape=jax.ShapeDtypeStruct((4, h, c), WIRE_DTYPE), compiler_params=_params(("parallel", "parallel")),
    )(core, g, sib)


def _sum4(parts, name):
    _, r, c = parts.shape
    tr = _pick(r, (128, 64, 32, 16, 8))

    def body(p_ref, o_ref):
        acc = p_ref[0].astype(F32)
        for j in range(1, 4):
            acc = acc + p_ref[j].astype(F32)
        o_ref[...] = acc

    return pl.pallas_call(
        body, name=name, grid=(r // tr,),
        in_specs=[pl.BlockSpec((4, tr, c), lambda i: (0, i, 0))],
        out_specs=pl.BlockSpec((tr, c), lambda i: (i, 0)),
        out_shape=jax.ShapeDtypeStruct((r, c), F32),
        compiler_params=_params(("parallel",)),
    )(parts)


def _adam_small(items, chip, name):
    n = len(items) - 1

    def total(g_ref, r, c):
        acc = g_ref[0, 0:r, 0:c]
        for j in range(1, 8):
            acc = acc + g_ref[j, 0:r, 0:c]
        return acc

    def body(chip_ref, *refs):
        g_refs, wmv, outs = refs[:n + 1], refs[n + 1:4 * n + 1], refs[4 * n + 1:]
        for p in range(n):
            r, c = items[p][1].shape
            g = total(g_refs[p], r, c)
            w_ref, m_ref, v_ref = wmv[3 * p:3 * p + 3]
            delta, m2, v2 = _adam_math(w_ref[...], g, m_ref[...], v_ref[...])
            for o_ref, val in zip(outs[4 * p:4 * p + 4], (g, delta, m2, v2)):
                o_ref[...] = val
        outs[4 * n][...] = total(g_refs[n], 1, 1)

    def full(shape):
        return pl.BlockSpec(shape, lambda i, chip_ref: (0,) * len(shape))

    g_specs, args, out_specs, out_shapes = [], [], [], []
    for g, w, m, v, sharded in items[:n]:
        if sharded:
            g_specs.append(pl.BlockSpec((8, g.shape[1], w.shape[1]), lambda i, chip_ref: (0, 0, chip_ref[0])))
        else:
            g_specs.append(full(g.shape))
        args += [w, m, v]
        out_specs += [full(w.shape)] * 4
        out_shapes += [jax.ShapeDtypeStruct(w.shape, F32)] * 4
    g_specs.append(full(items[n][0].shape))
    return pl.pallas_call(
        body, name=name,
        grid_spec=pltpu.PrefetchScalarGridSpec(
            num_scalar_prefetch=1, grid=(1,), in_specs=g_specs + [full(a.shape) for a in args],
            out_specs=out_specs + [full((1, 1))]),
        out_shape=out_shapes + [jax.ShapeDtypeStruct((1, 1), F32)],
    )(chip, *[it[0] for it in items], *args)


def _gather8(arrays):
    def make(ins, outs, send, recv, loc):
        x, y, c, _, _ = _place()
        mine = 4 * x + 2 * y + c
        cps = []
        for i in range(len(arrays)):
            cps.append(pltpu.make_async_copy(ins[i], outs[i].at[mine], loc.at[i]))
            for k in range(1, 8):
                fx, fy, fc = (k >> 2) & 1, (k >> 1) & 1, k & 1
                peer = (1 - x if fx else x, 1 - y if fy else y, 1 - c if fc else c)
                cps.append(pltpu.make_async_remote_copy(
                    src_ref=ins[i], dst_ref=outs[i].at[mine], send_sem=send.at[7 * i + k - 1],
                    recv_sem=recv.at[7 * i + k - 1], device_id=peer, device_id_type=MESH))
        return cps

    outs = [jax.ShapeDtypeStruct((8,) + a.shape, a.dtype) for a in arrays]
    return _Comm(arrays, outs, 7 * len(arrays), make)


def _pack(parts):
    flat = jnp.concatenate([p.reshape(-1).astype(F32) for p in parts])
    pad = (-flat.shape[0]) % (32 * LANES)
    return jnp.pad(flat, (0, pad)).reshape(-1, LANES)


def _unpack(slab, shapes):
    flat = slab.reshape(-1)
    out, off = [], 0
    for s in shapes:
        size = int(np.prod(s))
        out.append(flat[off:off + size].reshape(s))
        off += size
    return out


def _gather_cols(segments, lo, hi):
    parts = []
    for arr, start, width in segments:
        a, e = max(lo, start), min(hi, start + width)
        if a < e:
            parts.append(arr[:, a - start:e - start])
    return parts


def _pad_lanes(v):
    return jnp.pad(v.reshape(1, -1), ((0, 0), (0, LANES - v.shape[-1])))


def kernel(x, meta_tokens, mix_norm_w, w_in, ssd_conv_w, ssd_conv_b, ssd_dt_bias, ssd_A_log, ssd_D, ssd_norm_w, w_branch_ssd, w_branch_ret, w_out, ffn_norm_w, w_up, ffn_conv_w, ffn_conv_b, w_down, final_norm_w, loss_target, m_meta_tokens, m_mix_norm_w, m_w_in, m_ssd_conv_w, m_ssd_conv_b, m_ssd_dt_bias, m_ssd_A_log, m_ssd_D, m_ssd_norm_w, m_w_branch_ssd, m_w_branch_ret, m_w_out, m_ffn_norm_w, m_w_up, m_ffn_conv_w, m_ffn_conv_b, m_w_down, m_final_norm_w, v_meta_tokens, v_mix_norm_w, v_w_in, v_ssd_conv_w, v_ssd_conv_b, v_ssd_dt_bias, v_ssd_A_log, v_ssd_D, v_ssd_norm_w, v_w_branch_ssd, v_w_branch_ret, v_w_out, v_ffn_norm_w, v_w_up, v_ffn_conv_w, v_ffn_conv_b, v_w_down, v_final_norm_w):
    seq, d = x.shape[1], x.shape[2]
    t = CHUNK + seq
    di = 2 * d
    nh = di // SSD_HEAD_DIM
    gn = SSD_GROUPS * SSD_STATE
    cw = di + 2 * gn
    f = w_down.shape[1] * 4
    chip = 2 * lax.axis_index("x") + lax.axis_index("y")

    order = [("z", di), ("v", di), ("g", di), ("xbc", cw), ("q", d), ("k", d), ("gs", d), ("gr", d), ("dt", LANES)]
    col, acc = {}, 0
    for nm, wd in order:
        col[nm] = acc
        acc += wd
    wp = acc
    ref_order = [("z", di), ("xbc", cw), ("dt", nh), ("q", d), ("k", d), ("v", di), ("g", di), ("gs", d), ("gr", d)]
    ref_off, acc = {}, 0
    for nm, wd in ref_order:
        ref_off[nm] = (acc, wd)
        acc += wd
    in_dim = acc

    core = lax.axis_index("c").astype(jnp.int32).reshape(1)
    small_shapes = [meta_tokens.shape, ssd_conv_w.shape[1:], ffn_conv_w.shape[1:]]
    small_local = _pack([meta_tokens, ssd_conv_w[0], ffn_conv_w[0]])
    first_local = [w_in[0].astype(WIRE_DTYPE), small_local]
    first_half = _run_comm(_gather_ici(first_local), "gather_w_in_ici")
    g_in, g_small = [_with_own(g, own, chip)
                     for g, own in zip(_run_comm(_gather_d2d(first_half), "gather_w_in_d2d"), first_local)]
    rest_local = [a[0].astype(WIRE_DTYPE) for a in (w_branch_ssd, w_branch_ret, w_out, w_up, w_down)]
    blk_in = in_dim // 4
    in_blocks = [(g_in[j], j * blk_in, blk_in) for j in range(4)]
    pieces = []
    for nm, wd in order:
        o, rw = ref_off[nm]
        pieces += _gather_cols(in_blocks, o, o + rw)
        if rw < wd:
            pieces.append(jnp.zeros((d, wd - rw), WIRE_DTYPE))
    w_p = jnp.concatenate(pieces, axis=1)
    smalls = [_unpack(g_small[j], small_shapes) for j in range(4)]
    meta_full = jnp.concatenate([s[0] for s in smalls], axis=1)
    scw = jnp.concatenate([s[1] for s in smalls], axis=1)
    fcw = jnp.concatenate([s[2] for s in smalls], axis=1)
    scb, fcb = ssd_conv_b, ffn_conv_b
    dtb, alog, dvec = _pad_lanes(ssd_dt_bias), _pad_lanes(ssd_A_log), _pad_lanes(ssd_D)
    fin_w = final_norm_w.reshape(1, d)

    hq = d // RET_HEADS
    pos = jnp.arange(t, dtype=F32) - FRONT
    inv_freq = ROPE_BASE ** (-jnp.linspace(0.0, 1.0, hq // 2, dtype=F32))
    ang = pos[:, None] * inv_freq[None, :]
    cos, sin = jnp.cos(ang), jnp.sin(ang)

    h0 = jnp.concatenate([jnp.zeros((FRONT, d), F32), meta_full, x[0]], axis=0)
    tm = _row_tile(t)
    tmb = _pick(t, (1664, 1280, 640, 512, 384, 256, 128))
    u1 = _rms_fwd(h0, mix_norm_w, "rms1_fwd")
    proj = _mm(u1, w_p, "nn", F32, "proj", tmb, _pick(wp, (1920, 1536, 1280, 1024, 896, 768, 640, 512, 384, 256, 128)), d)
    xa = _ssd_conv_fwd(proj, col["xbc"], cw, scw, scb, "ssd_conv_fwd")
    res = _ssd_fwd(xa, proj, col["dt"], col["z"], dtb, alog, dvec, ssd_norm_w, di, "ssd_fwd", comm=_gather_ici(rest_local))
    (ypre, yssd, st_ssd), rest_half = res[:3], res[3:]
    res = _ret_fwd(proj, col["q"], col["k"], col["v"], col["g"], cos, sin, d, "ret_fwd", comm=_gather_d2d(rest_half))
    o_ret, yret, qr, kr, st_ret = res[:5]
    g_bs, g_br, g_out, g_up, g_down = [_with_own(g, own, chip) for g, own in zip(res[5:], rest_local)]
    w_bs = g_bs.reshape(di, d)
    w_br = g_br.reshape(di, d)
    w_o = g_out.reshape(d, d)
    w_u = jnp.concatenate([g_up[j] for j in range(4)], axis=1)
    w_d = g_down.reshape(f, d)
    tn_d = _pick(d, (1024, 512, 256, 128))
    bs = _mm(yssd, w_bs, "nn", F32, "branch_ssd", tmb, tn_d, _pick(di, (1024, 512, 256)))
    br = _mm(yret, w_br, "nn", F32, "branch_ret", tmb, tn_d, _pick(di, (1024, 512, 256)))
    merged = _gate_fwd(bs, br, proj, col["gs"], col["gr"], "gate_fwd")
    h1 = _mm(merged, w_o, "nn", F32, "out_proj", tmb, tn_d, d, res=h0)
    u2 = _rms_fwd(h1, ffn_norm_w, "rms2_fwd")
    tn_f = _pick(2 * f, (1408, 1024, 768, 512, 256, 128))
    up = _mm(u2, w_u, "nn", F32, "up_proj", tmb, tn_f, d)
    act = _ffn_conv_fwd(up, fcw, fcb, "ffn_conv_fwd")
    tk_f = _pick(f, (1408, 768, 704, 512, 256, 128))
    h2 = _mm(act, w_d, "nn", F32, "down_proj", tmb, tn_d, tk_f, res=h1)
    loss8, d_h2, g_fin = _loss_bwd(h2, fin_w, loss_target[0], "loss_head")

    tkt = _pick(t, (1664, 1280, 1024, 640, 512, 384, 256, 128))
    d_act = _mm(d_h2, w_d, "nt", F32, "d_act", tmb, tk_f, d)
    g_wd = _mm(act, d_h2, "tn", F32, "g_w_down", tk_f, tn_d, tkt)
    d_upg, d_upv, g_fcwg, g_fcwv, g_fcbg, g_fcbv = _ffn_conv_bwd(up, fcw, fcb, d_act, "ffn_conv_bwd")
    g_fcw = jnp.concatenate([g_fcwg, g_fcwv], axis=1)
    g_fcb = jnp.concatenate([g_fcbg, g_fcbv], axis=1)
    d_u2 = _mm(d_upg, w_u[:, :f], "nt", F32, "d_u2_gate", tmb, tn_d, tk_f)
    d_u2 = _mm(d_upv, w_u[:, f:], "nt", F32, "d_u2_value", tmb, tn_d, tk_f, res=d_u2)
    g_wu_segs = [(_mm(u2, d_upg, "tn", F32, "g_w_up_gate", tn_d, tk_f, tkt), 0, f),
                 (_mm(u2, d_upv, "tn", F32, "g_w_up_value", tn_d, tk_f, tkt), f, f)]
    d_h1, g_ffnw = _rms_bwd(h1, ffn_norm_w, d_u2, d_h2, "rms2_bwd")
    d_merged = _mm(d_h1, w_o, "nt", F32, "d_merged", tmb, tn_d, d)
    g_wo = _mm(merged, d_h1, "tn", F32, "g_w_out", tn_d, tn_d, tkt)
    d_bs, d_br, d_gsr = _gate_bwd(d_merged, bs, br, proj, col["gs"], col["gr"], "gate_bwd")
    tk_i = _pick(di, (1024, 512, 256))
    d_yssd = _mm(d_bs, w_bs, "nt", F32, "d_y_ssd", tmb, tk_i, d)
    g_wbs = _mm(yssd, d_bs, "tn", F32, "g_w_branch_ssd", tk_i, tn_d, tkt)
    d_yret = _mm(d_br, w_br, "nt", F32, "d_y_ret", tmb, tk_i, d)
    g_wbr = _mm(yret, d_br, "tn", F32, "g_w_branch_ret", tk_i, tn_d, tkt)

    early_names = ["w_branch_ssd", "w_branch_ret", "w_out", "w_up", "w_down"]
    early = [g_wbs.reshape(4, di // 4, d), g_wbr.reshape(4, di // 4, d), g_wo.reshape(4, d // 4, d),
             jnp.stack([jnp.concatenate(_gather_cols(g_wu_segs, j * (f // 2), (j + 1) * (f // 2)), axis=1) for j in range(4)]),
             g_wd.reshape(4, f // 4, d)]
    res = _ret_bwd(d_yret, proj, col["v"], col["g"], o_ret, qr, kr, st_ret, cos, sin, d, "ret_bwd", comm=_scatter_d2d(early))
    (dqk, dvg), early_sib = res[:2], res[2:]
    early_pair = [_pair_sum(g_, s_, core, "pair_" + nm) for g_, s_, nm in zip(early, early_sib, early_names)]
    res = _ssd_bwd(d_yssd, xa, proj, col["dt"], col["z"], ypre, st_ssd, dtb, alog, dvec, ssd_norm_w, di, "ssd_bwd",
                   comm=_scatter_ici(early_pair))
    (d_xa, dz, ddt, g_dtb, g_alog, g_dvec, g_snw), early_recv = res[:7], res[7:]
    early_mine = [_sum4(p, "sum4_" + nm) for p, nm in zip(early_recv, early_names)]
    res = _ssd_conv_bwd(proj, col["xbc"], cw, scw, scb, d_xa, "ssd_conv_bwd", comm=_sibling_swap(early_mine))
    (d_xbc, g_scw, g_scb), early_other = res[:3], res[3:]
    d_pieces = [("z", dz), ("v", dvg), ("xbc", d_xbc), ("q", dqk), ("gs", d_gsr), ("dt", ddt)]

    g_piece = {nm: _mm(u1, a, "tn", WIRE_DTYPE, "g_w_in_" + nm, tn_d, _pick(a.shape[1], (1024, 768, 512, 256, 128)), tkt)
               for nm, a in d_pieces}
    g_cols = dict(z=g_piece["z"], v=g_piece["v"][:, :di], g=g_piece["v"][:, di:], xbc=g_piece["xbc"],
                  q=g_piece["q"][:, :d], k=g_piece["q"][:, d:], gs=g_piece["gs"][:, :d], gr=g_piece["gs"][:, d:],
                  dt=g_piece["dt"])
    g_segs = [(g_cols[nm], ref_off[nm][0], rw) for nm, rw in ref_order]
    sc_in = jnp.stack([jnp.concatenate(_gather_cols(g_segs, j * blk_in, (j + 1) * blk_in), axis=1) for j in range(4)])
    in_sib = _run_comm(_scatter_d2d([sc_in]), "scatter_w_in_d2d")[0]
    in_pair = _pair_sum(sc_in, in_sib, core, "pair_w_in")
    d_u1, (in_recv,) = _mm_pieces_nt([(a, col[nm]) for nm, a in d_pieces], w_p, "d_u1", tmb, tn_d, 512,
                                     comm=_scatter_ici([in_pair]))
    in_mine = _sum4(in_recv, "sum4_w_in")
    d_h0, g_mixw, in_other = _rms_bwd(h0, mix_norm_w, d_u1, d_h1, "rms1_bwd", comm=_sibling_swap([in_mine]))
    grad_x = d_h0[CHUNK:][None]
    g_meta = d_h0[FRONT:CHUNK]

    names = ["w_in"] + early_names
    mine_half = [in_mine] + early_mine
    other_half = [in_other] + list(early_other)
    big_w = [w_in, w_branch_ssd, w_branch_ret, w_out, w_up, w_down]
    big_m = [m_w_in, m_w_branch_ssd, m_w_branch_ret, m_w_out, m_w_up, m_w_down]
    big_v = [v_w_in, v_w_branch_ssd, v_w_branch_ret, v_w_out, v_w_up, v_w_down]
    big_out = {}
    for nm, w_, p_, s_, m_, v_ in zip(names, big_w, mine_half, other_half, big_m, big_v):
        res = _adam_big(w_[0], p_, s_, m_[0], v_[0], core, "adam_" + nm)
        big_out[nm] = [r[None] for r in res]

    small = [
        ("meta_tokens", g_meta, meta_tokens, m_meta_tokens, v_meta_tokens, True),
        ("mix_norm_w", g_mixw, mix_norm_w, m_mix_norm_w, v_mix_norm_w, False),
        ("ssd_conv_w", g_scw, ssd_conv_w[0], m_ssd_conv_w[0], v_ssd_conv_w[0], True),
        ("ssd_conv_b", g_scb, ssd_conv_b, m_ssd_conv_b, v_ssd_conv_b, False),
        ("ssd_dt_bias", g_dtb, ssd_dt_bias, m_ssd_dt_bias, v_ssd_dt_bias, False),
        ("ssd_A_log", g_alog, ssd_A_log, m_ssd_A_log, v_ssd_A_log, False),
        ("ssd_D", g_dvec, ssd_D, m_ssd_D, v_ssd_D, False),
        ("ssd_norm_w", g_snw, ssd_norm_w, m_ssd_norm_w, v_ssd_norm_w, False),
        ("ffn_norm_w", g_ffnw, ffn_norm_w, m_ffn_norm_w, v_ffn_norm_w, False),
        ("ffn_conv_w", g_fcw, ffn_conv_w[0], m_ffn_conv_w[0], v_ffn_conv_w[0], True),
        ("ffn_conv_b", g_fcb, ffn_conv_b, m_ffn_conv_b, v_ffn_conv_b, False),
        ("final_norm_w", g_fin, fin_w, m_final_norm_w.reshape(1, d), v_final_norm_w.reshape(1, d), False),
    ]
    gathered8 = _run_comm(_gather8([s[1] for s in small] + [loss8]), "gather_small_grads")
    items = [(g8,) + s[2:] for g8, s in zip(gathered8, small)] + [(gathered8[-1], None, None, None, False)]
    small_res = _adam_small(items, chip.astype(jnp.int32).reshape(1), "adam_small")
    loss = small_res[-1].reshape(())
    out_shape = dict(meta_tokens=meta_tokens.shape, ssd_conv_w=ssd_conv_w.shape, ffn_conv_w=ffn_conv_w.shape,
                     final_norm_w=final_norm_w.shape)
    small_out = {s[0]: [r.reshape(out_shape.get(s[0], r.shape)) for r in small_res[4 * p:4 * p + 4]]
                 for p, s in enumerate(small)}

    weights = ["meta_tokens", "mix_norm_w", "w_in", "ssd_conv_w", "ssd_conv_b", "ssd_dt_bias", "ssd_A_log", "ssd_D",
               "ssd_norm_w", "w_branch_ssd", "w_branch_ret", "w_out", "ffn_norm_w", "w_up", "ffn_conv_w", "ffn_conv_b",
               "w_down", "final_norm_w"]
    outs = [loss, grad_x]
    for kind in range(4):
        for nm in weights:
            outs.append(big_out[nm][kind] if nm in big_out else small_out[nm][kind])
    return tuple(outs)
```

```python
import functools
import math

import jax
import jax.numpy as jnp
import numpy as np
from jax import lax
from jax.experimental import pallas as pl
from jax.experimental.pallas import tpu as pltpu

F32 = jnp.float32
BF16 = jnp.bfloat16
MXU_DTYPE = BF16
WIRE_DTYPE = BF16

N_META = 16
CHUNK = 128
FRONT = CHUNK - N_META
EPS = 1e-6
SSD_HEAD_DIM = 64
SSD_GROUPS = 4
SSD_STATE = 128
SSD_CONV = 4
RET_HEADS = 4
ROPE_BASE = 10000.0
FFN_CONV = 3
LANES = 128
SUBLANES = 8
VMEM_LIMIT = 56 * 1024 * 1024

ADAM_LR = 0.001
ADAM_B1 = 0.9
ADAM_B2 = 0.999
ADAM_EPS = 1e-08
ADAM_WD = 0.01
ADAM_STEP = 10
MESH = pl.DeviceIdType.MESH


def _params(sem=None, vmem=VMEM_LIMIT):
    return pltpu.CompilerParams(dimension_semantics=sem, vmem_limit_bytes=vmem)


def _pick(n, cands):
    for c in cands:
        if n % c == 0:
            return c
    return n


def _silu(x):
    return x * jax.nn.sigmoid(x)


def _dsilu(x):
    s = jax.nn.sigmoid(x)
    return s * (1.0 + x * (1.0 - s))


def _dot(a, b, dims=(((1,), (0,)), ((), ()))):
    return lax.dot_general(a.astype(MXU_DTYPE), b.astype(MXU_DTYPE), dims, preferred_element_type=F32)


def _dot_nt(a, b):
    return _dot(a, b, (((1,), (1,)), ((), ())))


def _dot_tn(a, b):
    return _dot(a, b, (((0,), (0,)), ((), ())))


def _dot01(a, b, split, npass, dims=(((1,), (0,)), ((), ()))):
    rest = (a if split == "a" else b).astype(F32)
    fixed = (b if split == "a" else a).astype(BF16)
    acc = None
    for p in range(npass):
        piece = rest.astype(BF16)
        ops = (piece, fixed) if split == "a" else (fixed, piece)
        term = lax.dot_general(ops[0], ops[1], dims, preferred_element_type=F32)
        acc = term if acc is None else acc + term
        if p + 1 < npass:
            rest = rest - piece.astype(F32)
    return acc


_NT = (((1,), (1,)), ((), ()))


def _iota(shape, dim):
    return lax.broadcasted_iota(jnp.int32, shape, dim)


def _shift_down(cur, prev8, k):
    if k == 0:
        return cur
    rolled = pltpu.roll(cur, k, 0)
    i8 = _iota((SUBLANES, cur.shape[1]), 0)
    head = jnp.where(i8 < k, pltpu.roll(prev8, k, 0), rolled[0:SUBLANES])
    return jnp.concatenate([head, rolled[SUBLANES:]], axis=0)


class _Comm:
    def __init__(self, ins, outs, nsem, make, in_place=False):
        self.ins, self.outs, self.nsem, self.make = list(ins), list(outs), nsem, make
        self.in_place = in_place


def _place():
    x, y, c = lax.axis_index("x"), lax.axis_index("y"), lax.axis_index("c")
    return x, y, c, 2 * x + y, [(1 - x, y), (x, 1 - y), (1 - x, 1 - y)]


def _call(body, name, grid, in_specs, out_specs, out_shape, scratch, sem, args, comm=None):
    if comm is None:
        return pl.pallas_call(body, name=name, grid=grid, in_specs=in_specs, out_specs=out_specs, out_shape=out_shape,
                              scratch_shapes=scratch, compiler_params=_params(sem))(*args)
    n_in, n_out, n_scr = len(in_specs), len(out_specs), len(scratch)
    ci, co = len(comm.ins), len(comm.outs)

    def wrapped(*refs):
        ins, refs = refs[:n_in], refs[n_in:]
        cins, refs = refs[:ci], refs[ci:]
        outs, refs = refs[:n_out], refs[n_out:]
        couts, refs = refs[:co], refs[co:]
        scr, sems = refs[:n_scr], refs[n_scr:]
        first = functools.reduce(jnp.logical_and, [pl.program_id(a) == 0 for a in range(len(grid))])
        last = functools.reduce(jnp.logical_and, [pl.program_id(a) == grid[a] - 1 for a in range(len(grid))])

        @pl.when(first)
        def _():
            for cp in comm.make(cins, couts, *sems):
                cp.start()

        body(*ins, *outs, *scr)

        @pl.when(last)
        def _():
            for cp in comm.make(cins, couts, *sems):
                cp.wait()

    anyspec = pl.BlockSpec(memory_space=pl.ANY)
    dma = pltpu.SemaphoreType.DMA((comm.nsem,))
    aliases = {n_in + i: n_out + i for i in range(ci)} if comm.in_place else {}
    return pl.pallas_call(
        wrapped, name=name, grid=grid, in_specs=list(in_specs) + [anyspec] * ci,
        out_specs=list(out_specs) + [anyspec] * co, out_shape=list(out_shape) + comm.outs,
        scratch_shapes=list(scratch) + [dma, dma, dma], input_output_aliases=aliases,
        compiler_params=_params(("arbitrary",) * len(grid)))(*args, *comm.ins)


def _run_comm(comm, name):
    ci, co = len(comm.ins), len(comm.outs)

    def body(*refs):
        cins, couts, sems = refs[:ci], refs[ci:ci + co], refs[ci + co:]
        for cp in comm.make(cins, couts, *sems):
            cp.start()
        for cp in comm.make(cins, couts, *sems):
            cp.wait()

    anyspec = pl.BlockSpec(memory_space=pl.ANY)
    dma = pltpu.SemaphoreType.DMA((comm.nsem,))
    aliases = {i: i for i in range(ci)} if comm.in_place else {}
    return pl.pallas_call(body, name=name, in_specs=[anyspec] * ci, out_specs=[anyspec] * co, out_shape=comm.outs,
                          scratch_shapes=[dma, dma, dma], input_output_aliases=aliases)(*comm.ins)


def _half_rows(c, rows):
    h = rows // 2
    return pl.ds(pl.multiple_of(c * h, 16), h)


def _gather_ici(arrays):
    for a in arrays:
        assert a.shape[0] % 32 == 0, a.shape

    def make(ins, outs, send, recv, loc):
        x, y, c, mine, peers = _place()
        cps = []
        for i, a in enumerate(arrays):
            half = _half_rows(c, a.shape[0])
            for k, (px, py) in enumerate(peers):
                cps.append(pltpu.make_async_remote_copy(
                    src_ref=ins[i].at[half], dst_ref=outs[i].at[mine, half], send_sem=send.at[3 * i + k],
                    recv_sem=recv.at[3 * i + k], device_id=(px, py, c), device_id_type=MESH))
        return cps

    outs = [jax.ShapeDtypeStruct((4,) + a.shape, a.dtype) for a in arrays]
    return _Comm(arrays, outs, 3 * len(arrays), make)


def _gather_d2d(bufs):
    def make(ins, outs, send, recv, loc):
        x, y, c, mine, peers = _place()
        cps = []
        for i, a in enumerate(bufs):
            half = _half_rows(c, a.shape[1])
            for k, (px, py) in enumerate(peers):
                mine_half = outs[i].at[2 * px + py, half]
                cps.append(pltpu.make_async_remote_copy(
                    src_ref=mine_half, dst_ref=mine_half, send_sem=send.at[3 * i + k], recv_sem=recv.at[3 * i + k],
                    device_id=(x, y, 1 - c), device_id_type=MESH))
        return cps

    outs = [jax.ShapeDtypeStruct(a.shape, a.dtype) for a in bufs]
    return _Comm(bufs, outs, 3 * len(bufs), make, in_place=True)


def _with_own(gathered, own, chip):
    return lax.dynamic_update_index_in_dim(gathered, own, chip, 0)


def _scatter_d2d(grads):
    for a in grads:
        assert a.shape[1] % 32 == 0, a.shape

    def make(ins, outs, send, recv, loc):
        x, y, c, mine, peers = _place()
        cps = []
        for i, a in enumerate(grads):
            other = _half_rows(1 - c, a.shape[1])
            cps.append(pltpu.make_async_remote_copy(
                src_ref=ins[i].at[:, other], dst_ref=outs[i], send_sem=send.at[i], recv_sem=recv.at[i],
                device_id=(x, y, 1 - c), device_id_type=MESH))
        return cps

    outs = [jax.ShapeDtypeStruct((4, a.shape[1] // 2, a.shape[2]), a.dtype) for a in grads]
    return _Comm(grads, outs, len(grads), make)


def _scatter_ici(parts):
    def make(ins, outs, send, recv, loc):
        x, y, c, mine, peers = _place()
        cps = []
        for i in range(len(parts)):
            cps.append(pltpu.make_async_copy(ins[i].at[mine], outs[i].at[mine], loc.at[i]))
            for k, (px, py) in enumerate(peers):
                cps.append(pltpu.make_async_remote_copy(
                    src_ref=ins[i].at[2 * px + py], dst_ref=outs[i].at[mine], send_sem=send.at[3 * i + k],
                    recv_sem=recv.at[3 * i + k], device_id=(px, py, c), device_id_type=MESH))
        return cps

    outs = [jax.ShapeDtypeStruct(a.shape, a.dtype) for a in parts]
    return _Comm(parts, outs, 3 * len(parts), make)


def _sibling_swap(arrays):
    def make(ins, outs, send, recv, loc):
        x, y, c, mine, peers = _place()
        return [pltpu.make_async_remote_copy(src_ref=ins[i], dst_ref=outs[i], send_sem=send.at[i], recv_sem=recv.at[i],
                                             device_id=(x, y, 1 - c), device_id_type=MESH) for i in range(len(arrays))]

    outs = [jax.ShapeDtypeStruct(a.shape, a.dtype) for a in arrays]
    return _Comm(arrays, outs, len(arrays), make)


def _mm(a, b, mode, out_dtype, name, tm, tn, tk, res=None, comm=None):
    if mode == "nn":
        (m, kd), n = a.shape, b.shape[1]
        a_spec = pl.BlockSpec((tm, tk), lambda i, j, k: (i, k))
        b_spec = pl.BlockSpec((tk, tn), lambda i, j, k: (k, j))
        dims = (((1,), (0,)), ((), ()))
    elif mode == "nt":
        (m, kd), n = a.shape, b.shape[0]
        a_spec = pl.BlockSpec((tm, tk), lambda i, j, k: (i, k))
        b_spec = pl.BlockSpec((tn, tk), lambda i, j, k: (j, k))
        dims = (((1,), (1,)), ((), ()))
    else:
        (kd, m), n = a.shape, b.shape[1]
        a_spec = pl.BlockSpec((tk, tm), lambda i, j, k: (k, i))
        b_spec = pl.BlockSpec((tk, tn), lambda i, j, k: (k, j))
        dims = (((0,), (0,)), ((), ()))
    assert m % tm == 0 and n % tn == 0 and kd % tk == 0, (name, m, n, kd, tm, tn, tk)
    nk = kd // tk
    has_res = res is not None
    in_place = out_dtype == F32

    def body(*refs):
        a_ref, b_ref = refs[:2]
        r_ref = refs[2] if has_res else None
        o_ref = refs[3 if has_res else 2]

        def finish(r):
            if has_res:
                r = r + r_ref[...].astype(F32)
            o_ref[...] = r.astype(out_dtype)

        if nk == 1:
            finish(_dot(a_ref[...], b_ref[...], dims))
            return
        k = pl.program_id(2)
        if in_place:
            @pl.when(k == 0)
            def _():
                finish(_dot(a_ref[...], b_ref[...], dims))

            @pl.when(k > 0)
            def _():
                o_ref[...] += _dot(a_ref[...], b_ref[...], dims)
            return
        acc = refs[-1]

        @pl.when(k == 0)
        def _():
            acc[...] = _dot(a_ref[...], b_ref[...], dims)

        @pl.when((k > 0) & (k < nk - 1))
        def _():
            acc[...] += _dot(a_ref[...], b_ref[...], dims)

        @pl.when(k == nk - 1)
        def _():
            finish(acc[...] + _dot(a_ref[...], b_ref[...], dims))

    in_specs = [a_spec, b_spec]
    args = [a, b]
    if has_res:
        in_specs.append(pl.BlockSpec((tm, tn), lambda i, j, k: (i, j)))
        args.append(res)
    res = _call(body, name, (m // tm, n // tn, nk), in_specs, [pl.BlockSpec((tm, tn), lambda i, j, k: (i, j))],
                [jax.ShapeDtypeStruct((m, n), out_dtype)], [] if nk == 1 or in_place else [pltpu.VMEM((tm, tn), F32)],
                ("parallel", "parallel", "arbitrary"), args, comm)
    return res[0] if comm is None else (res[0], res[1:])


def _mm_pieces_nt(pieces, b, name, tm, tn, tk_max, comm=None):
    m, n = pieces[0][0].shape[0], b.shape[0]
    out_dtype = F32
    cands = [c for c in (1024, 512, 256, 128) if c <= tk_max]
    tks = [_pick(math.gcd(a.shape[1], c0) if c0 else a.shape[1], cands) for a, c0 in pieces]
    nks = [a.shape[1] // tk for (a, _), tk in zip(pieces, tks)]
    starts = [sum(nks[:p]) for p in range(len(pieces))]
    ktot = sum(nks)
    npc = len(pieces)

    def body(*refs):
        a_refs, b_refs, o_ref = refs[:npc], refs[npc:2 * npc], refs[2 * npc]
        k = pl.program_id(2)

        @pl.when(k == 0)
        def _():
            o_ref[...] = _dot_nt(a_refs[0][...], b_refs[0][...])

        for p in range(npc):
            @pl.when((k >= max(starts[p], 1)) & (k < starts[p] + nks[p]))
            def _(p=p):
                o_ref[...] += _dot_nt(a_refs[p][...], b_refs[p][...])

    def a_spec(p):
        return pl.BlockSpec((tm, tks[p]), lambda i, j, k: (i, jnp.clip(k - starts[p], 0, nks[p] - 1)))

    def b_spec(p):
        c0 = pieces[p][1] // tks[p]
        return pl.BlockSpec((tn, tks[p]), lambda i, j, k: (j, c0 + jnp.clip(k - starts[p], 0, nks[p] - 1)))

    res = _call(body, name, (m // tm, n // tn, ktot), [a_spec(p) for p in range(npc)] + [b_spec(p) for p in range(npc)],
                [pl.BlockSpec((tm, tn), lambda i, j, k: (i, j))], [jax.ShapeDtypeStruct((m, n), out_dtype)],
                [], ("parallel", "parallel", "arbitrary"), [a for a, _ in pieces] + [b] * npc, comm)
    return res[0] if comm is None else (res[0], res[1:])


def _rms_fwd(h, w, name):
    t, d = h.shape
    tr = _pick(t, (640, 512, 384, 256, 128))

    def body(h_ref, w_ref, u_ref):
        x = h_ref[...]
        r = lax.rsqrt(jnp.mean(x * x, axis=1, keepdims=True) + EPS)
        u_ref[...] = (x * r * w_ref[...]).astype(MXU_DTYPE)

    return pl.pallas_call(
        body, name=name, grid=(t // tr,),
        in_specs=[pl.BlockSpec((tr, d), lambda i: (i, 0)), pl.BlockSpec((1, d), lambda i: (0, 0))],
        out_specs=pl.BlockSpec((tr, d), lambda i: (i, 0)),
        out_shape=jax.ShapeDtypeStruct((t, d), MXU_DTYPE),
        compiler_params=_params(("parallel",)),
    )(h, w)


def _rms_bwd(h, w, du, res, name, comm=None):
    t, d = h.shape
    tr = _pick(t, (640, 512, 384, 256, 128))

    def body(h_ref, w_ref, du_ref, res_ref, dh_ref, gw_ref):
        @pl.when(pl.program_id(0) == 0)
        def _():
            gw_ref[...] = jnp.zeros_like(gw_ref)

        x = h_ref[...]
        r = lax.rsqrt(jnp.mean(x * x, axis=1, keepdims=True) + EPS)
        xhat = x * r
        dy = du_ref[...].astype(F32)
        dxh = dy * w_ref[...]
        dh = r * (dxh - xhat * jnp.mean(dxh * xhat, axis=1, keepdims=True))
        dh_ref[...] = dh + res_ref[...]
        gw_ref[...] += jnp.sum(dy * xhat, axis=0, keepdims=True)

    row = pl.BlockSpec((tr, d), lambda i: (i, 0))
    vec = pl.BlockSpec((1, d), lambda i: (0, 0))
    return _call(body, name, (t // tr,), [row, vec, row, row], [row, vec],
                 [jax.ShapeDtypeStruct((t, d), F32), jax.ShapeDtypeStruct((1, d), F32)], [], ("arbitrary",),
                 (h, w, du, res), comm)


def _loss_bwd(h2, w, target, name):
    t, d = h2.shape
    nc = t // CHUNK

    def body(h_ref, w_ref, tg_ref, loss_ref, dh_ref, gw_ref):
        i = pl.program_id(0)

        @pl.when(i == 0)
        def _():
            gw_ref[...] = jnp.zeros_like(gw_ref)
            loss_ref[...] = jnp.zeros_like(loss_ref)
            dh_ref[...] = jnp.zeros_like(dh_ref)

        @pl.when(i > 0)
        def _():
            x = h_ref[...]
            r = lax.rsqrt(jnp.mean(x * x, axis=1, keepdims=True) + EPS)
            xhat = x * r
            diff = xhat * w_ref[...] - tg_ref[...]
            loss_ref[...] += 0.5 * jnp.sum(jnp.sum(diff * diff, axis=1, keepdims=True), axis=0, keepdims=True) / d
            dy = diff / d
            dxh = dy * w_ref[...]
            dh_ref[...] = r * (dxh - xhat * jnp.mean(dxh * xhat, axis=1, keepdims=True))
            gw_ref[...] += jnp.sum(dy * xhat, axis=0, keepdims=True)

    row = pl.BlockSpec((CHUNK, d), lambda i: (i, 0))
    vec = pl.BlockSpec((1, d), lambda i: (0, 0))
    return pl.pallas_call(
        body, name=name, grid=(nc,),
        in_specs=[row, vec, pl.BlockSpec((CHUNK, d), lambda i: (jnp.maximum(i - 1, 0), 0))],
        out_specs=[pl.BlockSpec((SUBLANES, LANES), lambda i: (0, 0)), row, vec],
        out_shape=[jax.ShapeDtypeStruct((SUBLANES, LANES), F32), jax.ShapeDtypeStruct((t, d), F32),
                   jax.ShapeDtypeStruct((1, d), F32)],
        compiler_params=_params(("arbitrary",)),
    )(h2, w, target)


def _conv_tile(cur, prev8, w_ref, b_ref, kw):
    y = b_ref[...] + cur * w_ref[kw - 1:kw, :]
    for k in range(kw - 1):
        y = y + _shift_down(cur, prev8, kw - 1 - k) * w_ref[k:k + 1, :]
    return y


_SUB = 16


def _sub_rows(s):
    return pl.ds(0 if isinstance(s, int) else pl.multiple_of(s * _SUB, _SUB), _SUB)


def _window(x_ref, prev8, s):
    if isinstance(s, int):
        return jnp.concatenate([prev8, x_ref[0:_SUB, :]], axis=0)
    return x_ref[pl.ds(pl.multiple_of(s * _SUB - SUBLANES, SUBLANES), _SUB + SUBLANES), :]


def _conv_step(win, w, b, kw):
    taps = [win[SUBLANES:] if k == kw - 1 else pltpu.roll(win, kw - 1 - k, 0)[SUBLANES:] for k in range(kw)]
    y = b + taps[kw - 1] * w[kw - 1:kw, :]
    for k in range(kw - 1):
        y = y + taps[k] * w[k:k + 1, :]
    return y, taps


def _conv_dx_step(dpre, next8, w, kw):
    n = _SUB + SUBLANES
    win = jnp.concatenate([dpre, next8], axis=0)
    acc = dpre * w[kw - 1:kw, :]
    for k in range(kw - 1):
        acc = acc + pltpu.roll(win, n - (kw - 1 - k), 0)[0:_SUB] * w[k:k + 1, :]
    return acc


def _fold8(v):
    return functools.reduce(jnp.add, [v[r:r + SUBLANES] for r in range(0, _SUB, SUBLANES)])


def _row_tile(t):
    return _pick(t, (640, 512, 384, 256, 128))


def _ssd_conv_fwd(proj, col0, width, w, b, name):
    t = proj.shape[0]
    kw = w.shape[0]
    tr, tc = _row_tile(t), _pick(width, (512, 256, 128))
    c0, rb = col0 // tc, tr // SUBLANES
    assert col0 % tc == 0

    def body(x_ref, p_ref, w_ref, b_ref, o_ref):
        i = pl.program_id(1)
        prev8 = jnp.where(i > 0, p_ref[...], 0.0)
        pre = _conv_tile(x_ref[...], prev8, w_ref, b_ref, kw)
        rows = _iota((tr, 1), 0) + i * tr
        o_ref[...] = jnp.where(rows >= FRONT, _silu(pre), 0.0)

    return pl.pallas_call(
        body, name=name, grid=(width // tc, t // tr),
        in_specs=[pl.BlockSpec((tr, tc), lambda j, i: (i, c0 + j)),
                  pl.BlockSpec((SUBLANES, tc), lambda j, i: (jnp.maximum(i * rb - 1, 0), c0 + j)),
                  pl.BlockSpec((kw, tc), lambda j, i: (0, j)),
                  pl.BlockSpec((1, tc), lambda j, i: (0, j))],
        out_specs=pl.BlockSpec((tr, tc), lambda j, i: (i, j)),
        out_shape=jax.ShapeDtypeStruct((t, width), F32),
        compiler_params=_params(("parallel", "parallel")),
    )(proj, proj, w, b)


def _ssd_conv_bwd(proj, col0, width, w, b, dact, name, comm=None):
    t = proj.shape[0]
    kw = w.shape[0]
    tr, tc = _row_tile(t), _pick(width, (512, 256, 128))
    c0, rb, nrow = col0 // tc, tr // SUBLANES, t // tr

    def body(x_ref, p_ref, w_ref, b_ref, d_ref, o_ref, gw_ref, gb_ref, carry):
        i = pl.program_id(1)
        ti = nrow - 1 - i

        @pl.when(i == 0)
        def _():
            gw_ref[...] = jnp.zeros_like(gw_ref)
            gb_ref[...] = jnp.zeros_like(gb_ref)
            carry[...] = jnp.zeros_like(carry)

        w, b = w_ref[...], b_ref[...]
        prev8 = jnp.where(ti > 0, p_ref[...], 0.0)
        nsub = tr // _SUB

        def step(s, state):
            next8, gb8, gw8 = state
            pre, taps = _conv_step(_window(x_ref, prev8, s), w, b, kw)
            valid = _iota((_SUB, 1), 0) + (ti * tr + s * _SUB) >= FRONT
            dpre = jnp.where(valid, d_ref[_sub_rows(s), :] * _dsilu(pre), 0.0)
            o_ref[_sub_rows(s), :] = jnp.where(valid, _conv_dx_step(dpre, next8, w, kw), 0.0).astype(MXU_DTYPE)
            return (dpre[0:SUBLANES], gb8 + _fold8(dpre), tuple(g + _fold8(dpre * tp) for g, tp in zip(gw8, taps)))

        zero8 = jnp.zeros((SUBLANES, tc), F32)
        state = lax.fori_loop(0, nsub - 1, lambda n, st: step(nsub - 1 - n, st), (carry[...], zero8, (zero8,) * kw))
        next8, gb8, gw8 = step(0, state)
        carry[...] = next8
        gb_ref[...] += jnp.sum(gb8, axis=0, keepdims=True)
        for k in range(kw):
            gw_ref[k:k + 1, :] += jnp.sum(gw8[k], axis=0, keepdims=True)

    return _call(
        body, name, (width // tc, nrow),
        [pl.BlockSpec((tr, tc), lambda j, i: (nrow - 1 - i, c0 + j)),
         pl.BlockSpec((SUBLANES, tc), lambda j, i: (jnp.maximum((nrow - 1 - i) * rb - 1, 0), c0 + j)),
         pl.BlockSpec((kw, tc), lambda j, i: (0, j)),
         pl.BlockSpec((1, tc), lambda j, i: (0, j)),
         pl.BlockSpec((tr, tc), lambda j, i: (nrow - 1 - i, j))],
        [pl.BlockSpec((tr, tc), lambda j, i: (nrow - 1 - i, j)),
         pl.BlockSpec((SUBLANES, tc), lambda j, i: (0, j)),
         pl.BlockSpec((1, tc), lambda j, i: (0, j))],
        [jax.ShapeDtypeStruct((t, width), MXU_DTYPE), jax.ShapeDtypeStruct((SUBLANES, width), F32),
         jax.ShapeDtypeStruct((1, width), F32)],
        [pltpu.VMEM((SUBLANES, tc), F32)], ("parallel", "arbitrary"), (proj, proj, w, b, dact), comm)


def _ffn_conv_fwd(up, w, b, name):
    t, f2 = up.shape
    f = f2 // 2
    kw = w.shape[0]
    tr, tc = _row_tile(t), _pick(f, (256, 128))
    nf, rb = f // tc, tr // SUBLANES

    def body(xg, pg, xv, pv, wg, wv, bg, bv, o_ref):
        i = pl.program_id(1)
        ag = _conv_tile(xg[...], jnp.where(i > 0, pg[...], 0.0), wg, bg, kw)
        av = _conv_tile(xv[...], jnp.where(i > 0, pv[...], 0.0), wv, bv, kw)
        o_ref[...] = (_silu(ag) * av).astype(MXU_DTYPE)

    def cur(off):
        return pl.BlockSpec((tr, tc), lambda j, i: (i, j + off))

    def prev(off):
        return pl.BlockSpec((SUBLANES, tc), lambda j, i: (jnp.maximum(i * rb - 1, 0), j + off))

    def par(rows, off):
        return pl.BlockSpec((rows, tc), lambda j, i: (0, j + off))

    return pl.pallas_call(
        body, name=name, grid=(nf, t // tr),
        in_specs=[cur(0), prev(0), cur(nf), prev(nf), par(kw, 0), par(kw, nf), par(1, 0), par(1, nf)],
        out_specs=pl.BlockSpec((tr, tc), lambda j, i: (i, j)),
        out_shape=jax.ShapeDtypeStruct((t, f), MXU_DTYPE),
        compiler_params=_params(("parallel", "parallel")),
    )(up, up, up, up, w, w, b, b)


def _ffn_conv_bwd(up, w, b, dact, name):
    t, f2 = up.shape
    f = f2 // 2
    kw = w.shape[0]
    tr, tc = _row_tile(t), _pick(f, (256, 128))
    nf, rb, nrow = f // tc, tr // SUBLANES, t // tr

    def body(xg, pg, xv, pv, wg_ref, wv_ref, bg_ref, bv_ref, d_ref, og_ref, ov_ref, gwg_ref, gwv_ref, gbg_ref, gbv_ref,
             cg, cv):
        i = pl.program_id(1)
        ti = nrow - 1 - i

        @pl.when(i == 0)
        def _():
            for r in (gwg_ref, gwv_ref, gbg_ref, gbv_ref, cg, cv):
                r[...] = jnp.zeros_like(r)

        wg, wv, bg, bv = wg_ref[...], wv_ref[...], bg_ref[...], bv_ref[...]
        p8g, p8v = jnp.where(ti > 0, pg[...], 0.0), jnp.where(ti > 0, pv[...], 0.0)
        nsub = tr // _SUB

        def step(s, state):
            ng, nv, gbg8, gbv8, gwg8, gwv8 = state
            ag, tg = _conv_step(_window(xg, p8g, s), wg, bg, kw)
            av, tv = _conv_step(_window(xv, p8v, s), wv, bv, kw)
            d = d_ref[_sub_rows(s), :]
            sg = jax.nn.sigmoid(ag)
            dag = d * av * (sg * (1.0 + ag * (1.0 - sg)))
            dav = d * (ag * sg)
            valid = _iota((_SUB, 1), 0) + (ti * tr + s * _SUB) >= FRONT
            og_ref[_sub_rows(s), :] = jnp.where(valid, _conv_dx_step(dag, ng, wg, kw), 0.0).astype(MXU_DTYPE)
            ov_ref[_sub_rows(s), :] = jnp.where(valid, _conv_dx_step(dav, nv, wv, kw), 0.0).astype(MXU_DTYPE)
            return (dag[0:SUBLANES], dav[0:SUBLANES], gbg8 + _fold8(dag), gbv8 + _fold8(dav),
                    tuple(g + _fold8(dag * tp) for g, tp in zip(gwg8, tg)),
                    tuple(g + _fold8(dav * tp) for g, tp in zip(gwv8, tv)))

        zero8 = jnp.zeros((SUBLANES, tc), F32)
        state = lax.fori_loop(0, nsub - 1, lambda n, st: step(nsub - 1 - n, st),
                              (cg[...], cv[...], zero8, zero8, (zero8,) * kw, (zero8,) * kw))
        ng, nv, gbg8, gbv8, gwg8, gwv8 = step(0, state)
        cg[...] = ng
        cv[...] = nv
        gbg_ref[...] += jnp.sum(gbg8, axis=0, keepdims=True)
        gbv_ref[...] += jnp.sum(gbv8, axis=0, keepdims=True)
        for k in range(kw):
            gwg_ref[k:k + 1, :] += jnp.sum(gwg8[k], axis=0, keepdims=True)
            gwv_ref[k:k + 1, :] += jnp.sum(gwv8[k], axis=0, keepdims=True)

    def cur(off):
        return pl.BlockSpec((tr, tc), lambda j, i: (nrow - 1 - i, j + off))

    def prev(off):
        return pl.BlockSpec((SUBLANES, tc), lambda j, i: (jnp.maximum((nrow - 1 - i) * rb - 1, 0), j + off))

    def par(rows, off):
        return pl.BlockSpec((rows, tc), lambda j, i: (0, j + off))

    acc8 = pl.BlockSpec((SUBLANES, tc), lambda j, i: (0, j))
    acc1 = pl.BlockSpec((1, tc), lambda j, i: (0, j))
    return pl.pallas_call(
        body, name=name, grid=(nf, nrow),
        in_specs=[cur(0), prev(0), cur(nf), prev(nf), par(kw, 0), par(kw, nf), par(1, 0), par(1, nf), cur(0)],
        out_specs=[cur(0), cur(0), acc8, acc8, acc1, acc1],
        out_shape=[jax.ShapeDtypeStruct((t, f), MXU_DTYPE), jax.ShapeDtypeStruct((t, f), MXU_DTYPE),
                   jax.ShapeDtypeStruct((SUBLANES, f), F32), jax.ShapeDtypeStruct((SUBLANES, f), F32),
                   jax.ShapeDtypeStruct((1, f), F32), jax.ShapeDtypeStruct((1, f), F32)],
        scratch_shapes=[pltpu.VMEM((SUBLANES, tc), F32), pltpu.VMEM((SUBLANES, tc), F32)],
        compiler_params=_params(("parallel", "arbitrary")),
    )(up, up, up, up, w, w, b, b, dact)


def _gate_fwd(bs, br, proj, c_gs, c_gr, name):
    t, d = bs.shape
    tr = _row_tile(t)

    def body(bs_ref, br_ref, gs_ref, gr_ref, o_ref):
        o_ref[...] = (jax.nn.sigmoid(gs_ref[...]) * bs_ref[...] + jax.nn.sigmoid(gr_ref[...]) * br_ref[...]).astype(MXU_DTYPE)

    row = pl.BlockSpec((tr, d), lambda i: (i, 0))
    return pl.pallas_call(
        body, name=name, grid=(t // tr,),
        in_specs=[row, row, pl.BlockSpec((tr, d), lambda i: (i, c_gs // d)), pl.BlockSpec((tr, d), lambda i: (i, c_gr // d))],
        out_specs=row, out_shape=jax.ShapeDtypeStruct((t, d), MXU_DTYPE),
        compiler_params=_params(("parallel",)),
    )(bs, br, proj, proj)


def _gate_bwd(dm, bs, br, proj, c_gs, c_gr, name):
    t, d = bs.shape
    tr = _row_tile(t)

    def body(dm_ref, bs_ref, br_ref, gs_ref, gr_ref, dbs_ref, dbr_ref, dgg_ref):
        g = dm_ref[...]
        ss, sr = jax.nn.sigmoid(gs_ref[...]), jax.nn.sigmoid(gr_ref[...])
        dbs_ref[...] = (g * ss).astype(MXU_DTYPE)
        dbr_ref[...] = (g * sr).astype(MXU_DTYPE)
        dgg_ref[:, :d] = (g * bs_ref[...] * ss * (1.0 - ss)).astype(MXU_DTYPE)
        dgg_ref[:, d:] = (g * br_ref[...] * sr * (1.0 - sr)).astype(MXU_DTYPE)

    row = pl.BlockSpec((tr, d), lambda i: (i, 0))
    out = jax.ShapeDtypeStruct((t, d), MXU_DTYPE)
    return pl.pallas_call(
        body, name=name, grid=(t // tr,),
        in_specs=[row, row, row, pl.BlockSpec((tr, d), lambda i: (i, c_gs // d)), pl.BlockSpec((tr, d), lambda i: (i, c_gr // d))],
        out_specs=[row, row, pl.BlockSpec((tr, 2 * d), lambda i: (i, 0))],
        out_shape=[out, out, jax.ShapeDtypeStruct((t, 2 * d), MXU_DTYPE)],
        compiler_params=_params(("parallel",)),
    )(dm, bs, br, proj, proj)


def _ret_consts(h):
    lg = math.log(1.0 - 2.0 ** (-5.0 - h))
    l = _iota((CHUNK, 1), 0).astype(F32)
    diff = l - _iota((1, CHUNK), 1).astype(F32)
    dm = jnp.exp(jnp.where(diff >= 0, diff * lg, -jnp.inf))
    dmt = jnp.exp(jnp.where(diff <= 0, -diff * lg, -jnp.inf))
    cs = jnp.exp((l + 1.0) * lg)
    kdec = jnp.exp((CHUNK - 1.0 - l) * lg)
    return dm, dmt, cs, kdec, math.exp(CHUNK * lg)


def _ret_fwd(proj, c_q, c_k, c_v, c_g, cos, sin, d, name, comm=None):
    t = proj.shape[0]
    nc = t // CHUNK
    hq, hv = d // RET_HEADS, 2 * d // RET_HEADS
    half = hq // 2
    scale = hq ** -0.5

    def body(q_ref, k_ref, v_ref, g_ref, cos_ref, sin_ref, o_ref, y_ref, qr_ref, kr_ref, st_ref, rs):
        @pl.when(pl.program_id(0) == 0)
        def _():
            rs[...] = jnp.zeros_like(rs)

        co, si = cos_ref[...], sin_ref[...]
        for h in range(RET_HEADS):
            dm, _, cs, kdec, gam = _ret_consts(h)
            q1, q2 = q_ref[:, h * hq:h * hq + half], q_ref[:, h * hq + half:(h + 1) * hq]
            k1, k2 = k_ref[:, h * hq:h * hq + half], k_ref[:, h * hq + half:(h + 1) * hq]
            qr = jnp.concatenate([q1 * co - q2 * si, q2 * co + q1 * si], axis=1)
            kr = jnp.concatenate([k1 * co - k2 * si, k2 * co + k1 * si], axis=1) * scale
            qr_ref[:, h * hq:(h + 1) * hq] = qr.astype(MXU_DTYPE)
            kr_ref[:, h * hq:(h + 1) * hq] = kr.astype(MXU_DTYPE)
            v = v_ref[:, h * hv:(h + 1) * hv]
            r_in = rs[h * hq:(h + 1) * hq, :]
            st_ref[0, h * hq:(h + 1) * hq, :] = r_in.astype(MXU_DTYPE)
            s = _dot_nt(qr, kr) * dm
            o = _dot(s, v) + cs * _dot(qr, r_in)
            rs[h * hq:(h + 1) * hq, :] = gam * r_in + _dot_tn(kr * kdec, v)
            o_ref[:, h * hv:(h + 1) * hv] = o
            on = o * lax.rsqrt(jnp.mean(o * o, axis=1, keepdims=True) + EPS)
            y_ref[:, h * hv:(h + 1) * hv] = (_silu(g_ref[:, h * hv:(h + 1) * hv]) * on).astype(MXU_DTYPE)

    def col(width, c0):
        return pl.BlockSpec((CHUNK, width), lambda i: (i, c0 // width))

    tab = pl.BlockSpec((CHUNK, half), lambda i: (i, 0))
    return _call(
        body, name, (nc,),
        [col(d, c_q), col(d, c_k), col(2 * d, c_v), col(2 * d, c_g), tab, tab],
        [col(2 * d, 0), col(2 * d, 0), col(d, 0), col(d, 0), pl.BlockSpec((1, d, hv), lambda i: (i, 0, 0))],
        [jax.ShapeDtypeStruct((t, 2 * d), F32), jax.ShapeDtypeStruct((t, 2 * d), MXU_DTYPE),
         jax.ShapeDtypeStruct((t, d), MXU_DTYPE), jax.ShapeDtypeStruct((t, d), MXU_DTYPE),
         jax.ShapeDtypeStruct((nc, d, hv), MXU_DTYPE)],
        [pltpu.VMEM((d, hv), F32)], ("arbitrary",), (proj, proj, proj, proj, cos, sin), comm)


def _ret_bwd(dy, proj, c_v, c_g, o, qr, kr, st, cos, sin, d, name, comm=None):
    t = proj.shape[0]
    nc = t // CHUNK
    hq, hv = d // RET_HEADS, 2 * d // RET_HEADS
    half = hq // 2
    scale = hq ** -0.5

    def body(dy_ref, v_ref, g_ref, o_ref, qr_ref, kr_ref, st_ref, cos_ref, sin_ref, dqk_ref, dvg_ref, drs):
        dq_ref, dk_ref = dqk_ref.at[:, pl.ds(0, d)], dqk_ref.at[:, pl.ds(d, d)]
        dv_ref, dg_ref = dvg_ref.at[:, pl.ds(0, 2 * d)], dvg_ref.at[:, pl.ds(2 * d, 2 * d)]

        @pl.when(pl.program_id(0) == 0)
        def _():
            drs[...] = jnp.zeros_like(drs)

        co, si = cos_ref[...], sin_ref[...]
        for h in range(RET_HEADS):
            dm, dmt, cs, kdec, gam = _ret_consts(h)
            vs = slice(h * hv, (h + 1) * hv)
            qs = slice(h * hq, (h + 1) * hq)
            o_h = o_ref[:, vs]
            g_h = g_ref[:, vs]
            d_y = dy_ref[:, vs]
            r = lax.rsqrt(jnp.mean(o_h * o_h, axis=1, keepdims=True) + EPS)
            on = o_h * r
            d_on = d_y * _silu(g_h)
            dg_ref[:, vs] = (d_y * on * _dsilu(g_h)).astype(MXU_DTYPE)
            d_o = r * (d_on - on * jnp.mean(d_on * on, axis=1, keepdims=True))
            q_h, k_h, v_h = qr_ref[:, qs], kr_ref[:, qs], v_ref[:, vs]
            r_in = st_ref[0, qs, :]
            dr_n = drs[qs, :]
            csdo = cs * d_o
            ds = _dot_nt(d_o, v_h) * dm
            dst = _dot_nt(v_h, d_o) * dmt
            s_t = _dot_nt(k_h, q_h) * dmt
            dqr = _dot(ds, k_h) + _dot_nt(csdo, r_in)
            dkr = _dot(dst, q_h) + kdec * _dot_nt(v_h, dr_n)
            dv_ref[:, vs] = (_dot(s_t, d_o) + _dot(k_h.astype(F32) * kdec, dr_n)).astype(MXU_DTYPE)
            drs[qs, :] = gam * dr_n + _dot_tn(q_h, csdo)
            a1, a2 = dqr[:, :half], dqr[:, half:]
            dq_ref[:, qs] = jnp.concatenate([a1 * co + a2 * si, a2 * co - a1 * si], axis=1).astype(MXU_DTYPE)
            b1, b2 = dkr[:, :half] * scale, dkr[:, half:] * scale
            dk_ref[:, qs] = jnp.concatenate([b1 * co + b2 * si, b2 * co - b1 * si], axis=1).astype(MXU_DTYPE)

    def col(width, c0=0):
        return pl.BlockSpec((CHUNK, width), lambda i: (nc - 1 - i, c0 // width))

    tab = pl.BlockSpec((CHUNK, half), lambda i: (nc - 1 - i, 0))
    return _call(
        body, name, (nc,),
        [col(2 * d), col(2 * d, c_v), col(2 * d, c_g), col(2 * d), col(d), col(d),
         pl.BlockSpec((1, d, hv), lambda i: (nc - 1 - i, 0, 0)), tab, tab],
        [col(2 * d), col(4 * d)],
        [jax.ShapeDtypeStruct((t, 2 * d), MXU_DTYPE), jax.ShapeDtypeStruct((t, 4 * d), MXU_DTYPE)],
        [pltpu.VMEM((d, hv), F32)], ("arbitrary",), (dy, proj, proj, o, qr, kr, st, cos, sin), comm)


def _ssd_small(dtraw_ref, dtb_ref, alog_ref, chunk_idx, nh):
    rows = _iota((CHUNK, 1), 0)
    ok = ((rows >= FRONT) | (chunk_idx > 0)) & (_iota((1, LANES), 1) < nh)
    z = dtraw_ref[...] + dtb_ref[...]
    dt = jnp.where(ok, jax.nn.softplus(z), 0.0)
    sig = jnp.where(ok, jax.nn.sigmoid(z), 0.0)
    a = jnp.where(_iota((1, LANES), 1) < nh, -jnp.exp(alog_ref[...]), 0.0)
    tri = (_iota((CHUNK, CHUNK), 0) >= _iota((CHUNK, CHUNK), 1)).astype(F32)
    acs = _dot01(tri, dt * a, "b", 3)
    return dt, sig, a, acs, acs.T


def _head_expand(g, hpg, gw):
    shift = int(math.log2(SSD_HEAD_DIM))
    return (_iota((LANES, gw), 0) == g * hpg + lax.shift_right_logical(_iota((LANES, gw), 1), shift)).astype(F32)


def _ssd_fwd(xa, proj, c_dt, c_z, dtb, alog, dvec, nw, di, name, comm=None):
    t = xa.shape[0]
    nc = t // CHUNK
    nh = di // SSD_HEAD_DIM
    hpg = nh // SSD_GROUPS
    gw = di // SSD_GROUPS
    n = SSD_STATE
    gn = SSD_GROUPS * n
    hd = SSD_HEAD_DIM

    def body(x_ref, b_ref, c_ref, dtraw_ref, z_ref, dtb_ref, alog_ref, d_ref, nw_ref,
             y_ref, ys_ref, st_ref, hts, xdt_s):
        c = pl.program_id(0)

        @pl.when(c == 0)
        def _():
            hts[...] = jnp.zeros_like(hts)

        dt, _, _, acs, acs_t = _ssd_small(dtraw_ref, dtb_ref, alog_ref, c, nh)
        tri = _iota((CHUNK, CHUNK), 0) >= _iota((CHUNK, CHUNK), 1)
        dvec8 = jnp.broadcast_to(d_ref[...], (SUBLANES, LANES))
        for g in range(SSD_GROUPS):
            gs = slice(g * gw, (g + 1) * gw)
            ns = slice(g * n, (g + 1) * n)
            e_mat = _head_expand(g, hpg, gw)
            ax = _dot01(acs, e_mat, "a", 3)
            dtx = _dot01(dt, e_mat, "a", 3)
            dx = _dot01(dvec8, e_mat, "a", 3)[0:1, :]
            xg, bg, cg = x_ref[:, gs], b_ref[:, ns], c_ref[:, ns]
            xdt = xg * dtx
            xdt_s[...] = xdt.astype(MXU_DTYPE)
            cb = _dot_nt(cg, bg)
            ht = hts[ns, :]
            st_ref[0, ns, :] = ht.astype(MXU_DTYPE)
            y_ref[:, gs] = jnp.exp(ax) * _dot(cg, ht) + dx * xg
            for hh in range(hpg):
                h = g * hpg + hh
                lmat = jnp.exp(jnp.where(tri, acs[:, h:h + 1] - acs_t[h:h + 1, :], -jnp.inf))
                hs = slice(g * gw + hh * hd, g * gw + (hh + 1) * hd)
                y_ref[:, hs] += _dot(cb * lmat, xdt_s[:, hh * hd:(hh + 1) * hd])
            aend = ax[CHUNK - 1:CHUNK, :]
            hts[ns, :] = jnp.exp(aend) * ht + _dot_tn(bg, xdt * jnp.exp(aend - ax))
        for g in range(SSD_GROUPS):
            gs = slice(g * gw, (g + 1) * gw)
            yz = y_ref[:, gs] * _silu(z_ref[:, gs])
            r = lax.rsqrt(jnp.mean(yz * yz, axis=1, keepdims=True) + EPS)
            ys_ref[:, gs] = (yz * r * nw_ref[:, gs]).astype(MXU_DTYPE)

    def col(width, c0, arr_is_xa=False):
        return pl.BlockSpec((CHUNK, width), lambda i: (i, c0 // width))

    vec = pl.BlockSpec((1, LANES), lambda i: (0, 0))
    assert di % gn == 0 and c_dt % LANES == 0 and c_z % di == 0
    return _call(
        body, name, (nc,),
        [col(di, 0), col(gn, di), col(gn, di + gn), col(LANES, c_dt), col(di, c_z), vec, vec, vec,
         pl.BlockSpec((1, di), lambda i: (0, 0))],
        [col(di, 0), col(di, 0), pl.BlockSpec((1, gn, gw), lambda i: (i, 0, 0))],
        [jax.ShapeDtypeStruct((t, di), F32), jax.ShapeDtypeStruct((t, di), MXU_DTYPE),
         jax.ShapeDtypeStruct((nc, gn, gw), MXU_DTYPE)],
        [pltpu.VMEM((gn, gw), F32), pltpu.VMEM((CHUNK, gw), MXU_DTYPE)], ("arbitrary",),
        (xa, xa, xa, proj, proj, dtb, alog, dvec, nw), comm)


def _ssd_bwd(dys, xa, proj, c_dt, c_z, ypre, st, dtb, alog, dvec, nw, di, name, comm=None):
    t = xa.shape[0]
    nc = t // CHUNK
    nh = di // SSD_HEAD_DIM
    hpg = nh // SSD_GROUPS
    gw = di // SSD_GROUPS
    n = SSD_STATE
    gn = SSD_GROUPS * n
    hd = SSD_HEAD_DIM

    def body(dys_ref, x_ref, b_ref, c_ref, dtraw_ref, z_ref, y_ref, st_ref, dtb_ref, alog_ref, d_ref, nw_ref,
             dxa_ref, dz_ref, ddt_ref, gb_ref, ga_ref, gd_ref, gnw_ref, dhts, dy_s, xdt_s, dxdt_s):
        i = pl.program_id(0)
        c = nc - 1 - i

        @pl.when(i == 0)
        def _():
            dhts[...] = jnp.zeros_like(dhts)
            gb_ref[...] = jnp.zeros_like(gb_ref)
            ga_ref[...] = jnp.zeros_like(ga_ref)
            gd_ref[...] = jnp.zeros_like(gd_ref)
            gnw_ref[...] = jnp.zeros_like(gnw_ref)

        dt, sig, a, acs, acs_t = _ssd_small(dtraw_ref, dtb_ref, alog_ref, c, nh)
        tri = _iota((CHUNK, CHUNK), 0) >= _iota((CHUNK, CHUNK), 1)
        triu = _iota((CHUNK, CHUNK), 0) <= _iota((CHUNK, CHUNK), 1)
        lane = _iota((1, LANES), 1)
        rows = _iota((CHUNK, 1), 0)
        head_row = _iota((LANES, 1), 0)
        dvec8 = jnp.broadcast_to(d_ref[...], (SUBLANES, LANES))
        da = jnp.zeros((CHUNK, LANES), F32)
        da_t = jnp.zeros((LANES, CHUNK), F32)
        ddt = jnp.zeros((CHUNK, LANES), F32)
        gd = jnp.zeros((1, LANES), F32)
        for g in range(SSD_GROUPS):
            gs = slice(g * gw, (g + 1) * gw)
            ns = slice(g * n, (g + 1) * n)
            y_g, z_g = y_ref[:, gs], z_ref[:, gs]
            sz = _silu(z_g)
            yz = y_g * sz
            r = lax.rsqrt(jnp.mean(yz * yz, axis=1, keepdims=True) + EPS)
            nrm = yz * r
            dyo = dys_ref[:, gs]
            gnw_ref[:, gs] += jnp.sum(dyo * nrm, axis=0, keepdims=True)
            dn = dyo * nw_ref[:, gs]
            dyz = r * (dn - nrm * jnp.mean(dn * nrm, axis=1, keepdims=True))
            dz_ref[:, gs] = (dyz * y_g * _dsilu(z_g)).astype(MXU_DTYPE)
            dy_g = dyz * sz
            dy_s[...] = dy_g.astype(MXU_DTYPE)
            e_mat = _head_expand(g, hpg, gw)
            ax = _dot01(acs, e_mat, "a", 3)
            dtx = _dot01(dt, e_mat, "a", 3)
            dx = _dot01(dvec8, e_mat, "a", 3)[0:1, :]
            xg, bg, cg = x_ref[:, gs], b_ref[:, ns], c_ref[:, ns]
            xdt = xg * dtx
            xdt_s[...] = xdt.astype(MXU_DTYPE)
            aend = ax[CHUNK - 1:CHUNK, :]
            e = jnp.exp(aend - ax)
            ea = jnp.exp(ax)
            eend = jnp.exp(aend)
            htp = st_ref[0, ns, :].astype(F32)
            dht = dhts[ns, :]
            cb = _dot_nt(cg, bg)
            q = _dot(bg, dht)
            dxdt_s[...] = e * q
            wl = e * q * xdt
            d_b = _dot_nt(e * xdt, dht)
            yi = ea * _dot(cg, htp)
            eady = ea * dy_g
            d_c = _dot_nt(eady, htp)
            t1 = jnp.sum(dht * htp, axis=0, keepdims=True) * eend
            dhts[ns, :] = eend * dht + _dot_tn(cg, eady)
            da = da + _dot01(dy_g * yi - wl, e_mat, "a", 3, _NT)
            tail = jnp.broadcast_to(jnp.sum(wl, axis=0, keepdims=True) + t1, (SUBLANES, gw))
            da_end = _dot01(tail, e_mat, "a", 3, _NT)[0:1, :]
            da = da + jnp.where(rows == CHUNK - 1, da_end, 0.0)
            dcb = jnp.zeros((CHUNK, CHUNK), F32)
            for hh in range(hpg):
                h = g * hpg + hh
                lmat = jnp.exp(jnp.where(tri, acs[:, h:h + 1] - acs_t[h:h + 1, :], -jnp.inf))
                hl = slice(hh * hd, (hh + 1) * hd)
                dy_h, xdt_h = dy_s[:, hl], xdt_s[:, hl]
                dxdt_s[:, hl] += _dot_tn(cb * lmat, dy_h)
                dml = _dot_nt(dy_h, xdt_h) * lmat
                dcb = dcb + dml
                gmat = dml * cb
                da = da + jnp.where(lane == h, jnp.sum(gmat, axis=1, keepdims=True), 0.0)
                da_t = da_t - jnp.where(head_row == h, jnp.sum(gmat, axis=0, keepdims=True), 0.0)
            d_c = d_c + _dot(dcb, bg)
            d_b = d_b + _dot_tn(dcb, cg)
            dxdt = dxdt_s[...]
            dxa_ref[:, gs] = dxdt * dtx + dx * dy_g
            dxa_ref[:, di + g * n:di + (g + 1) * n] = d_b
            dxa_ref[:, di + gn + g * n:di + gn + (g + 1) * n] = d_c
            ddt = ddt + _dot01(dxdt * xg, e_mat, "a", 3, _NT)
            gd8 = jnp.broadcast_to(jnp.sum(dy_g * xg, axis=0, keepdims=True), (SUBLANES, gw))
            gd = gd + _dot01(gd8, e_mat, "a", 3, _NT)[0:1, :]
        da = da + da_t.T
        triu_f = triu.astype(F32)
        ddta = _dot01(triu_f, da, "b", 3)
        ddt = ddt + ddta * a
        draw = ddt * sig
        ddt_ref[...] = draw.astype(MXU_DTYPE)
        gb_ref[...] += jnp.sum(draw, axis=0, keepdims=True)
        ga_ref[...] += jnp.sum(ddta * dt, axis=0, keepdims=True) * a
        gd_ref[...] += gd

    def col(width, c0):
        return pl.BlockSpec((CHUNK, width), lambda i: (nc - 1 - i, c0 // width))

    vec = pl.BlockSpec((1, LANES), lambda i: (0, 0))
    wide = pl.BlockSpec((1, di), lambda i: (0, 0))
    wa = di + 2 * gn
    return _call(
        body, name, (nc,),
        [col(di, 0), col(di, 0), col(gn, di), col(gn, di + gn), col(LANES, c_dt), col(di, c_z), col(di, 0),
         pl.BlockSpec((1, gn, gw), lambda i: (nc - 1 - i, 0, 0)), vec, vec, vec, wide],
        [col(wa, 0), col(di, 0), col(LANES, 0), vec, vec, vec, wide],
        [jax.ShapeDtypeStruct((t, wa), F32), jax.ShapeDtypeStruct((t, di), MXU_DTYPE),
         jax.ShapeDtypeStruct((t, LANES), MXU_DTYPE), jax.ShapeDtypeStruct((1, LANES), F32),
         jax.ShapeDtypeStruct((1, LANES), F32), jax.ShapeDtypeStruct((1, LANES), F32),
         jax.ShapeDtypeStruct((1, di), F32)],
        [pltpu.VMEM((gn, gw), F32), pltpu.VMEM((CHUNK, gw), MXU_DTYPE), pltpu.VMEM((CHUNK, gw), MXU_DTYPE),
         pltpu.VMEM((CHUNK, gw), F32)], ("arbitrary",),
        (dys, xa, xa, xa, proj, proj, ypre, st, dtb, alog, dvec, nw), comm)


def _adam_math(w, g, m, v):
    m2 = ADAM_B1 * m + (1.0 - ADAM_B1) * g
    v2 = ADAM_B2 * v + (1.0 - ADAM_B2) * (g * g)
    m_hat = m2 / (1.0 - ADAM_B1 ** ADAM_STEP)
    v_hat = v2 / (1.0 - ADAM_B2 ** ADAM_STEP)
    delta = -ADAM_LR * (m_hat / (jnp.sqrt(v_hat) + ADAM_EPS) + ADAM_WD * w)
    return delta, m2, v2


def _adam_big(w, g_mine, g_sib, m, v, core, name):
    r, c = w.shape
    h = r // 2
    tr = _pick(h, (128, 176, 64, 32, 16, 8))
    nbh = h // tr

    def body(core_ref, w_ref, a_ref, b_ref, m_ref, v_ref, g_ref, d_ref, m2_ref, v2_ref):
        g = jnp.where(pl.program_id(0) // nbh == core_ref[0], a_ref[...], b_ref[...])
        delta, m2, v2 = _adam_math(w_ref[...], g, m_ref[...], v_ref[...])
        g_ref[...] = g
        d_ref[...] = delta
        m2_ref[...] = m2
        v2_ref[...] = v2

    blk = pl.BlockSpec((tr, c), lambda i, core_ref: (i, 0))
    hblk = pl.BlockSpec((tr, c), lambda i, core_ref: (i % nbh, 0))
    out = jax.ShapeDtypeStruct((r, c), F32)
    return pl.pallas_call(
        body, name=name,
        grid_spec=pltpu.PrefetchScalarGridSpec(num_scalar_prefetch=1, grid=(2 * nbh,),
                                               in_specs=[blk, hblk, hblk, blk, blk], out_specs=[blk] * 4),
        out_shape=[out] * 4, compiler_params=_params(("parallel",)),
    )(core, w, g_mine, g_sib, m, v)


def _pair_sum(g, sib, core, name):
    _, r, c = g.shape
    h = r // 2
    tr = _pick(h, (128, 176, 64, 32, 16))
    nb = h // tr

    def body(core_ref, g_ref, s_ref, o_ref):
        o_ref[...] = (g_ref[...].astype(F32) + s_ref[...].astype(F32)).astype(WIRE_DTYPE)

    return pl.pallas_call(
        body, name=name,
        grid_spec=pltpu.PrefetchScalarGridSpec(
            num_scalar_prefetch=1, grid=(4, nb),
            in_specs=[pl.BlockSpec((1, tr, c), lambda j, i, core_ref: (j, core_ref[0] * nb + i, 0)),
                      pl.BlockSpec((1, tr, c), lambda j, i, core_ref: (j, i, 0))],
            out_specs=pl.BlockSpec((1, tr, c), lambda j, i, core_ref: (j, i, 0))),
        out_shape=jax.ShapeDtypeStruct((4, h, c), WIRE_DTYPE), compiler_params=_params(("parallel", "parallel")),
    )(core, g, sib)


def _sum4(parts, name):
    _, r, c = parts.shape
    tr = _pick(r, (128, 176, 64, 32, 16, 8))

    def body(p_ref, o_ref):
        acc = p_ref[0].astype(F32)
        for j in range(1, 4):
            acc = acc + p_ref[j].astype(F32)
        o_ref[...] = acc

    return pl.pallas_call(
        body, name=name, grid=(r // tr,),
        in_specs=[pl.BlockSpec((4, tr, c), lambda i: (0, i, 0))],
        out_specs=pl.BlockSpec((tr, c), lambda i: (i, 0)),
        out_shape=jax.ShapeDtypeStruct((r, c), F32),
        compiler_params=_params(("parallel",)),
    )(parts)


def _adam_small(items, chip, name):
    n = len(items) - 1

    def total(g_ref, r, c):
        acc = g_ref[0, 0:r, 0:c]
        for j in range(1, 8):
            acc = acc + g_ref[j, 0:r, 0:c]
        return acc

    def body(chip_ref, *refs):
        g_refs, wmv, outs = refs[:n + 1], refs[n + 1:4 * n + 1], refs[4 * n + 1:]
        for p in range(n):
            r, c = items[p][1].shape
            g = total(g_refs[p], r, c)
            w_ref, m_ref, v_ref = wmv[3 * p:3 * p + 3]
            delta, m2, v2 = _adam_math(w_ref[...], g, m_ref[...], v_ref[...])
            for o_ref, val in zip(outs[4 * p:4 * p + 4], (g, delta, m2, v2)):
                o_ref[...] = val
        outs[4 * n][...] = total(g_refs[n], 1, 1)

    def full(shape):
        return pl.BlockSpec(shape, lambda i, chip_ref: (0,) * len(shape))

    g_specs, args, out_specs, out_shapes = [], [], [], []
    for g, w, m, v, sharded in items[:n]:
        if sharded:
            g_specs.append(pl.BlockSpec((8, g.shape[1], w.shape[1]), lambda i, chip_ref: (0, 0, chip_ref[0])))
        else:
            g_specs.append(full(g.shape))
        args += [w, m, v]
        out_specs += [full(w.shape)] * 4
        out_shapes += [jax.ShapeDtypeStruct(w.shape, F32)] * 4
    g_specs.append(full(items[n][0].shape))
    return pl.pallas_call(
        body, name=name,
        grid_spec=pltpu.PrefetchScalarGridSpec(
            num_scalar_prefetch=1, grid=(1,), in_specs=g_specs + [full(a.shape) for a in args],
            out_specs=out_specs + [full((1, 1))]),
        out_shape=out_shapes + [jax.ShapeDtypeStruct((1, 1), F32)],
    )(chip, *[it[0] for it in items], *args)


def _gather8(arrays):
    def make(ins, outs, send, recv, loc):
        x, y, c, _, _ = _place()
        mine = 4 * x + 2 * y + c
        cps = []
        for i in range(len(arrays)):
            cps.append(pltpu.make_async_copy(ins[i], outs[i].at[mine], loc.at[i]))
            for k in range(1, 8):
                fx, fy, fc = (k >> 2) & 1, (k >> 1) & 1, k & 1
                peer = (1 - x if fx else x, 1 - y if fy else y, 1 - c if fc else c)
                cps.append(pltpu.make_async_remote_copy(
                    src_ref=ins[i], dst_ref=outs[i].at[mine], send_sem=send.at[7 * i + k - 1],
                    recv_sem=recv.at[7 * i + k - 1], device_id=peer, device_id_type=MESH))
        return cps

    outs = [jax.ShapeDtypeStruct((8,) + a.shape, a.dtype) for a in arrays]
    return _Comm(arrays, outs, 7 * len(arrays), make)


def _pack(parts):
    flat = jnp.concatenate([p.reshape(-1).astype(F32) for p in parts])
    pad = (-flat.shape[0]) % (32 * LANES)
    return jnp.pad(flat, (0, pad)).reshape(-1, LANES)


def _unpack(slab, shapes):
    flat = slab.reshape(-1)
    out, off = [], 0
    for s in shapes:
        size = int(np.prod(s))
        out.append(flat[off:off + size].reshape(s))
        off += size
    return out


def _gather_cols(segments, lo, hi):
    parts = []
    for arr, start, width in segments:
        a, e = max(lo, start), min(hi, start + width)
        if a < e:
            parts.append(arr[:, a - start:e - start])
    return parts


def _pad_lanes(v):
    return jnp.pad(v.reshape(1, -1), ((0, 0), (0, LANES - v.shape[-1])))


def kernel(x, meta_tokens, mix_norm_w, w_in, ssd_conv_w, ssd_conv_b, ssd_dt_bias, ssd_A_log, ssd_D, ssd_norm_w, w_branch_ssd, w_branch_ret, w_out, ffn_norm_w, w_up, ffn_conv_w, ffn_conv_b, w_down, final_norm_w, loss_target, m_meta_tokens, m_mix_norm_w, m_w_in, m_ssd_conv_w, m_ssd_conv_b, m_ssd_dt_bias, m_ssd_A_log, m_ssd_D, m_ssd_norm_w, m_w_branch_ssd, m_w_branch_ret, m_w_out, m_ffn_norm_w, m_w_up, m_ffn_conv_w, m_ffn_conv_b, m_w_down, m_final_norm_w, v_meta_tokens, v_mix_norm_w, v_w_in, v_ssd_conv_w, v_ssd_conv_b, v_ssd_dt_bias, v_ssd_A_log, v_ssd_D, v_ssd_norm_w, v_w_branch_ssd, v_w_branch_ret, v_w_out, v_ffn_norm_w, v_w_up, v_ffn_conv_w, v_ffn_conv_b, v_w_down, v_final_norm_w):
    seq, d = x.shape[1], x.shape[2]
    t = CHUNK + seq
    di = 2 * d
    nh = di // SSD_HEAD_DIM
    gn = SSD_GROUPS * SSD_STATE
    cw = di + 2 * gn
    f = w_down.shape[1] * 4
    chip = 2 * lax.axis_index("x") + lax.axis_index("y")

    order = [("z", di), ("v", di), ("g", di), ("xbc", cw), ("q", d), ("k", d), ("gs", d), ("gr", d), ("dt", LANES)]
    col, acc = {}, 0
    for nm, wd in order:
        col[nm] = acc
        acc += wd
    wp = acc
    ref_order = [("z", di), ("xbc", cw), ("dt", nh), ("q", d), ("k", d), ("v", di), ("g", di), ("gs", d), ("gr", d)]
    ref_off, acc = {}, 0
    for nm, wd in ref_order:
        ref_off[nm] = (acc, wd)
        acc += wd
    in_dim = acc

    core = lax.axis_index("c").astype(jnp.int32).reshape(1)
    small_shapes = [meta_tokens.shape, ssd_conv_w.shape[1:], ffn_conv_w.shape[1:]]
    small_local = _pack([meta_tokens, ssd_conv_w[0], ffn_conv_w[0]])
    first_local = [w_in[0].astype(WIRE_DTYPE), small_local]
    first_half = _run_comm(_gather_ici(first_local), "gather_w_in_ici")
    g_in, g_small = [_with_own(g, own, chip)
                     for g, own in zip(_run_comm(_gather_d2d(first_half), "gather_w_in_d2d"), first_local)]
    rest_local = [a[0].astype(WIRE_DTYPE) for a in (w_branch_ssd, w_branch_ret, w_out, w_up, w_down)]
    blk_in = in_dim // 4
    in_blocks = [(g_in[j], j * blk_in, blk_in) for j in range(4)]
    pieces = []
    for nm, wd in order:
        o, rw = ref_off[nm]
        pieces += _gather_cols(in_blocks, o, o + rw)
        if rw < wd:
            pieces.append(jnp.zeros((d, wd - rw), WIRE_DTYPE))
    w_p = jnp.concatenate(pieces, axis=1)
    smalls = [_unpack(g_small[j], small_shapes) for j in range(4)]
    meta_full = jnp.concatenate([s[0] for s in smalls], axis=1)
    scw = jnp.concatenate([s[1] for s in smalls], axis=1)
    fcw = jnp.concatenate([s[2] for s in smalls], axis=1)
    scb, fcb = ssd_conv_b, ffn_conv_b
    dtb, alog, dvec = _pad_lanes(ssd_dt_bias), _pad_lanes(ssd_A_log), _pad_lanes(ssd_D)
    fin_w = final_norm_w.reshape(1, d)

    hq = d // RET_HEADS
    pos = jnp.arange(t, dtype=F32) - FRONT
    inv_freq = ROPE_BASE ** (-jnp.linspace(0.0, 1.0, hq // 2, dtype=F32))
    ang = pos[:, None] * inv_freq[None, :]
    cos, sin = jnp.cos(ang), jnp.sin(ang)

    h0 = jnp.concatenate([jnp.zeros((FRONT, d), F32), meta_full, x[0]], axis=0)
    tm = _row_tile(t)
    tmb = _pick(t, (1664, 1280, 640, 512, 384, 256, 128))
    u1 = _rms_fwd(h0, mix_norm_w, "rms1_fwd")
    proj = _mm(u1, w_p, "nn", F32, "proj", tmb, _pick(wp, (1920, 1536, 1280, 1024, 896, 768, 640, 512, 384, 256, 128)), d)
    xa = _ssd_conv_fwd(proj, col["xbc"], cw, scw, scb, "ssd_conv_fwd")
    res = _ssd_fwd(xa, proj, col["dt"], col["z"], dtb, alog, dvec, ssd_norm_w, di, "ssd_fwd", comm=_gather_ici(rest_local))
    (ypre, yssd, st_ssd), rest_half = res[:3], res[3:]
    res = _ret_fwd(proj, col["q"], col["k"], col["v"], col["g"], cos, sin, d, "ret_fwd", comm=_gather_d2d(rest_half))
    o_ret, yret, qr, kr, st_ret = res[:5]
    g_bs, g_br, g_out, g_up, g_down = [_with_own(g, own, chip) for g, own in zip(res[5:], rest_local)]
    w_bs = g_bs.reshape(di, d)
    w_br = g_br.reshape(di, d)
    w_o = g_out.reshape(d, d)
    w_u = jnp.concatenate([g_up[j] for j in range(4)], axis=1)
    w_d = g_down.reshape(f, d)
    tn_d = _pick(d, (1024, 512, 256, 128))
    bs = _mm(yssd, w_bs, "nn", F32, "branch_ssd", tmb, tn_d, _pick(di, (1024, 512, 256)))
    br = _mm(yret, w_br, "nn", F32, "branch_ret", tmb, tn_d, _pick(di, (1024, 512, 256)))
    merged = _gate_fwd(bs, br, proj, col["gs"], col["gr"], "gate_fwd")
    h1 = _mm(merged, w_o, "nn", F32, "out_proj", tmb, tn_d, d, res=h0)
    u2 = _rms_fwd(h1, ffn_norm_w, "rms2_fwd")
    tn_f = _pick(2 * f, (1408, 1024, 768, 512, 256, 128))
    up = _mm(u2, w_u, "nn", F32, "up_proj", tmb, tn_f, d)
    act = _ffn_conv_fwd(up, fcw, fcb, "ffn_conv_fwd")
    tk_f = _pick(f, (1408, 768, 704, 512, 256, 128))
    h2 = _mm(act, w_d, "nn", F32, "down_proj", tmb, tn_d, tk_f, res=h1)
    loss8, d_h2, g_fin = _loss_bwd(h2, fin_w, loss_target[0], "loss_head")

    tkt = _pick(t, (1664, 1280, 1024, 640, 512, 384, 256, 128))
    d_act = _mm(d_h2, w_d, "nt", F32, "d_act", tmb, tk_f, d)
    g_wd = _mm(act, d_h2, "tn", F32, "g_w_down", tk_f, tn_d, tkt)
    d_upg, d_upv, g_fcwg, g_fcwv, g_fcbg, g_fcbv = _ffn_conv_bwd(up, fcw, fcb, d_act, "ffn_conv_bwd")
    g_fcw = jnp.concatenate([g_fcwg, g_fcwv], axis=1)
    g_fcb = jnp.concatenate([g_fcbg, g_fcbv], axis=1)
    d_u2 = _mm(d_upg, w_u[:, :f], "nt", F32, "d_u2_gate", tmb, tn_d, tk_f)
    d_u2 = _mm(d_upv, w_u[:, f:], "nt", F32, "d_u2_value", tmb, tn_d, tk_f, res=d_u2)
    g_wu_segs = [(_mm(u2, d_upg, "tn", F32, "g_w_up_gate", tn_d, tk_f, tkt), 0, f),
                 (_mm(u2, d_upv, "tn", F32, "g_w_up_value", tn_d, tk_f, tkt), f, f)]
    d_h1, g_ffnw = _rms_bwd(h1, ffn_norm_w, d_u2, d_h2, "rms2_bwd")
    d_merged = _mm(d_h1, w_o, "nt", F32, "d_merged", tmb, tn_d, d)
    g_wo = _mm(merged, d_h1, "tn", F32, "g_w_out", tn_d, tn_d, tkt)
    d_bs, d_br, d_gsr = _gate_bwd(d_merged, bs, br, proj, col["gs"], col["gr"], "gate_bwd")
    tk_i = _pick(di, (1024, 512, 256))
    d_yssd = _mm(d_bs, w_bs, "nt", F32, "d_y_ssd", tmb, tk_i, d)
    g_wbs = _mm(yssd, d_bs, "tn", F32, "g_w_branch_ssd", tk_i, tn_d, tkt)
    d_yret = _mm(d_br, w_br, "nt", F32, "d_y_ret", tmb, tk_i, d)
    g_wbr = _mm(yret, d_br, "tn", F32, "g_w_branch_ret", tk_i, tn_d, tkt)

    early_names = ["w_branch_ssd", "w_branch_ret", "w_out", "w_up", "w_down"]
    early = [g_wbs.reshape(4, di // 4, d), g_wbr.reshape(4, di // 4, d), g_wo.reshape(4, d // 4, d),
             jnp.stack([jnp.concatenate(_gather_cols(g_wu_segs, j * (f // 2), (j + 1) * (f // 2)), axis=1) for j in range(4)]),
             g_wd.reshape(4, f // 4, d)]
    res = _ret_bwd(d_yret, proj, col["v"], col["g"], o_ret, qr, kr, st_ret, cos, sin, d, "ret_bwd", comm=_scatter_d2d(early))
    (dqk, dvg), early_sib = res[:2], res[2:]
    early_pair = [_pair_sum(g_, s_, core, "pair_" + nm) for g_, s_, nm in zip(early, early_sib, early_names)]
    res = _ssd_bwd(d_yssd, xa, proj, col["dt"], col["z"], ypre, st_ssd, dtb, alog, dvec, ssd_norm_w, di, "ssd_bwd",
                   comm=_scatter_ici(early_pair))
    (d_xa, dz, ddt, g_dtb, g_alog, g_dvec, g_snw), early_recv = res[:7], res[7:]
    early_mine = [_sum4(p, "sum4_" + nm) for p, nm in zip(early_recv, early_names)]
    res = _ssd_conv_bwd(proj, col["xbc"], cw, scw, scb, d_xa, "ssd_conv_bwd", comm=_sibling_swap(early_mine))
    (d_xbc, g_scw, g_scb), early_other = res[:3], res[3:]
    d_pieces = [("z", dz), ("v", dvg), ("xbc", d_xbc), ("q", dqk), ("gs", d_gsr), ("dt", ddt)]

    g_piece = {nm: _mm(u1, a, "tn", WIRE_DTYPE, "g_w_in_" + nm, tn_d, _pick(a.shape[1], (1024, 768, 512, 256, 128)), tkt)
               for nm, a in d_pieces}
    g_cols = dict(z=g_piece["z"], v=g_piece["v"][:, :di], g=g_piece["v"][:, di:], xbc=g_piece["xbc"],
                  q=g_piece["q"][:, :d], k=g_piece["q"][:, d:], gs=g_piece["gs"][:, :d], gr=g_piece["gs"][:, d:],
                  dt=g_piece["dt"])
    g_segs = [(g_cols[nm], ref_off[nm][0], rw) for nm, rw in ref_order]
    sc_in = jnp.stack([jnp.concatenate(_gather_cols(g_segs, j * blk_in, (j + 1) * blk_in), axis=1) for j in range(4)])
    in_sib = _run_comm(_scatter_d2d([sc_in]), "scatter_w_in_d2d")[0]
    in_pair = _pair_sum(sc_in, in_sib, core, "pair_w_in")
    d_u1, (in_recv,) = _mm_pieces_nt([(a, col[nm]) for nm, a in d_pieces], w_p, "d_u1", tmb, tn_d, 512,
                                     comm=_scatter_ici([in_pair]))
    in_mine = _sum4(in_recv, "sum4_w_in")
    d_h0, g_mixw, in_other = _rms_bwd(h0, mix_norm_w, d_u1, d_h1, "rms1_bwd", comm=_sibling_swap([in_mine]))
    grad_x = d_h0[CHUNK:][None]
    g_meta = d_h0[FRONT:CHUNK]

    names = ["w_in"] + early_names
    mine_half = [in_mine] + early_mine
    other_half = [in_other] + list(early_other)
    big_w = [w_in, w_branch_ssd, w_branch_ret, w_out, w_up, w_down]
    big_m = [m_w_in, m_w_branch_ssd, m_w_branch_ret, m_w_out, m_w_up, m_w_down]
    big_v = [v_w_in, v_w_branch_ssd, v_w_branch_ret, v_w_out, v_w_up, v_w_down]
    big_out = {}
    for nm, w_, p_, s_, m_, v_ in zip(names, big_w, mine_half, other_half, big_m, big_v):
        res = _adam_big(w_[0], p_, s_, m_[0], v_[0], core, "adam_" + nm)
        big_out[nm] = [r[None] for r in res]

    small = [
        ("meta_tokens", g_meta, meta_tokens, m_meta_tokens, v_meta_tokens, True),
        ("mix_norm_w", g_mixw, mix_norm_w, m_mix_norm_w, v_mix_norm_w, False),
        ("ssd_conv_w", g_scw, ssd_conv_w[0], m_ssd_conv_w[0], v_ssd_conv_w[0], True),
        ("ssd_conv_b", g_scb, ssd_conv_b, m_ssd_conv_b, v_ssd_conv_b, False),
        ("ssd_dt_bias", g_dtb, ssd_dt_bias, m_ssd_dt_bias, v_ssd_dt_bias, False),
        ("ssd_A_log", g_alog, ssd_A_log, m_ssd_A_log, v_ssd_A_log, False),
        ("ssd_D", g_dvec, ssd_D, m_ssd_D, v_ssd_D, False),
        ("ssd_norm_w", g_snw, ssd_norm_w, m_ssd_norm_w, v_ssd_norm_w, False),
        ("ffn_norm_w", g_ffnw, ffn_norm_w, m_ffn_norm_w, v_ffn_norm_w, False),
        ("ffn_conv_w", g_fcw, ffn_conv_w[0], m_ffn_conv_w[0], v_ffn_conv_w[0], True),
        ("ffn_conv_b", g_fcb, ffn_conv_b, m_ffn_conv_b, v_ffn_conv_b, False),
        ("final_norm_w", g_fin, fin_w, m_final_norm_w.reshape(1, d), v_final_norm_w.reshape(1, d), False),
    ]
    gathered8 = _run_comm(_gather8([s[1] for s in small] + [loss8]), "gather_small_grads")
    items = [(g8,) + s[2:] for g8, s in zip(gathered8, small)] + [(gathered8[-1], None, None, None, False)]
    small_res = _adam_small(items, chip.astype(jnp.int32).reshape(1), "adam_small")
    loss = small_res[-1].reshape(())
    out_shape = dict(meta_tokens=meta_tokens.shape, ssd_conv_w=ssd_conv_w.shape, ffn_conv_w=ffn_conv_w.shape,
                     final_norm_w=final_norm_w.shape)
    small_out = {s[0]: [r.reshape(out_shape.get(s[0], r.shape)) for r in small_res[4 * p:4 * p + 4]]
                 for p, s in enumerate(small)}

    weights = ["meta_tokens", "mix_norm_w", "w_in", "ssd_conv_w", "ssd_conv_b", "ssd_dt_bias", "ssd_A_log", "ssd_D",
               "ssd_norm_w", "w_branch_ssd", "w_branch_ret", "w_out", "ffn_norm_w", "w_up", "ffn_conv_w", "ffn_conv_b",
               "w_down", "final_norm_w"]
    outs = [loss, grad_x]
    for kind in range(4):
        for nm in weights:
            outs.append(big_out[nm][kind] if nm in big_out else small_out[nm][kind])
    return tuple(outs)
```

```python
import functools
import math

import jax
import jax.numpy as jnp
import numpy as np
from jax import lax
from jax.experimental import pallas as pl
from jax.experimental.pallas import tpu as pltpu

F32 = jnp.float32
BF16 = jnp.bfloat16
MXU_DTYPE = BF16
WIRE_DTYPE = BF16

N_META = 16
CHUNK = 128
FRONT = CHUNK - N_META
EPS = 1e-6
SSD_HEAD_DIM = 64
SSD_GROUPS = 4
SSD_STATE = 128
SSD_CONV = 4
RET_HEADS = 4
ROPE_BASE = 10000.0
FFN_CONV = 3
LANES = 128
SUBLANES = 8
VMEM_LIMIT = 56 * 1024 * 1024

ADAM_LR = 0.001
ADAM_B1 = 0.9
ADAM_B2 = 0.999
ADAM_EPS = 1e-08
ADAM_WD = 0.01
ADAM_STEP = 10
MESH = pl.DeviceIdType.MESH


def _params(sem=None, vmem=VMEM_LIMIT):
    return pltpu.CompilerParams(dimension_semantics=sem, vmem_limit_bytes=vmem)


def _pick(n, cands):
    for c in cands:
        if n % c == 0:
            return c
    return n


def _silu(x):
    return x * jax.nn.sigmoid(x)


def _dsilu(x):
    s = jax.nn.sigmoid(x)
    return s * (1.0 + x * (1.0 - s))


def _dot(a, b, dims=(((1,), (0,)), ((), ()))):
    return lax.dot_general(a.astype(MXU_DTYPE), b.astype(MXU_DTYPE), dims, preferred_element_type=F32)


def _dot_nt(a, b):
    return _dot(a, b, (((1,), (1,)), ((), ())))


def _dot_tn(a, b):
    return _dot(a, b, (((0,), (0,)), ((), ())))


def _dot01(a, b, split, npass, dims=(((1,), (0,)), ((), ()))):
    rest = (a if split == "a" else b).astype(F32)
    fixed = (b if split == "a" else a).astype(BF16)
    acc = None
    for p in range(npass):
        piece = rest.astype(BF16)
        ops = (piece, fixed) if split == "a" else (fixed, piece)
        term = lax.dot_general(ops[0], ops[1], dims, preferred_element_type=F32)
        acc = term if acc is None else acc + term
        if p + 1 < npass:
            rest = rest - piece.astype(F32)
    return acc


_NT = (((1,), (1,)), ((), ()))


def _iota(shape, dim):
    return lax.broadcasted_iota(jnp.int32, shape, dim)


def _shift_down(cur, prev8, k):
    if k == 0:
        return cur
    rolled = pltpu.roll(cur, k, 0)
    i8 = _iota((SUBLANES, cur.shape[1]), 0)
    head = jnp.where(i8 < k, pltpu.roll(prev8, k, 0), rolled[0:SUBLANES])
    return jnp.concatenate([head, rolled[SUBLANES:]], axis=0)


class _Comm:
    def __init__(self, ins, outs, nsem, make, in_place=False):
        self.ins, self.outs, self.nsem, self.make = list(ins), list(outs), nsem, make
        self.in_place = in_place


def _place():
    x, y, c = lax.axis_index("x"), lax.axis_index("y"), lax.axis_index("c")
    return x, y, c, 2 * x + y, [(1 - x, y), (x, 1 - y), (1 - x, 1 - y)]


def _call(body, name, grid, in_specs, out_specs, out_shape, scratch, sem, args, comm=None):
    if comm is None:
        return pl.pallas_call(body, name=name, grid=grid, in_specs=in_specs, out_specs=out_specs, out_shape=out_shape,
                              scratch_shapes=scratch, compiler_params=_params(sem))(*args)
    n_in, n_out, n_scr = len(in_specs), len(out_specs), len(scratch)
    ci, co = len(comm.ins), len(comm.outs)

    def wrapped(*refs):
        ins, refs = refs[:n_in], refs[n_in:]
        cins, refs = refs[:ci], refs[ci:]
        outs, refs = refs[:n_out], refs[n_out:]
        couts, refs = refs[:co], refs[co:]
        scr, sems = refs[:n_scr], refs[n_scr:]
        first = functools.reduce(jnp.logical_and, [pl.program_id(a) == 0 for a in range(len(grid))])
        last = functools.reduce(jnp.logical_and, [pl.program_id(a) == grid[a] - 1 for a in range(len(grid))])

        @pl.when(first)
        def _():
            for cp in comm.make(cins, couts, *sems):
                cp.start()

        body(*ins, *outs, *scr)

        @pl.when(last)
        def _():
            for cp in comm.make(cins, couts, *sems):
                cp.wait()

    anyspec = pl.BlockSpec(memory_space=pl.ANY)
    dma = pltpu.SemaphoreType.DMA((comm.nsem,))
    aliases = {n_in + i: n_out + i for i in range(ci)} if comm.in_place else {}
    return pl.pallas_call(
        wrapped, name=name, grid=grid, in_specs=list(in_specs) + [anyspec] * ci,
        out_specs=list(out_specs) + [anyspec] * co, out_shape=list(out_shape) + comm.outs,
        scratch_shapes=list(scratch) + [dma, dma, dma], input_output_aliases=aliases,
        compiler_params=_params(("arbitrary",) * len(grid)))(*args, *comm.ins)


def _run_comm(comm, name):
    ci, co = len(comm.ins), len(comm.outs)

    def body(*refs):
        cins, couts, sems = refs[:ci], refs[ci:ci + co], refs[ci + co:]
        for cp in comm.make(cins, couts, *sems):
            cp.start()
        for cp in comm.make(cins, couts, *sems):
            cp.wait()

    anyspec = pl.BlockSpec(memory_space=pl.ANY)
    dma = pltpu.SemaphoreType.DMA((comm.nsem,))
    aliases = {i: i for i in range(ci)} if comm.in_place else {}
    return pl.pallas_call(body, name=name, in_specs=[anyspec] * ci, out_specs=[anyspec] * co, out_shape=comm.outs,
                          scratch_shapes=[dma, dma, dma], input_output_aliases=aliases)(*comm.ins)


def _half_rows(c, rows):
    h = rows // 2
    return pl.ds(pl.multiple_of(c * h, 16), h)


def _gather_ici(arrays):
    for a in arrays:
        assert a.shape[0] % 32 == 0, a.shape

    def make(ins, outs, send, recv, loc):
        x, y, c, mine, peers = _place()
        cps = []
        for i, a in enumerate(arrays):
            half = _half_rows(c, a.shape[0])
            for k, (px, py) in enumerate(peers):
                cps.append(pltpu.make_async_remote_copy(
                    src_ref=ins[i].at[half], dst_ref=outs[i].at[mine, half], send_sem=send.at[3 * i + k],
                    recv_sem=recv.at[3 * i + k], device_id=(px, py, c), device_id_type=MESH))
        return cps

    outs = [jax.ShapeDtypeStruct((4,) + a.shape, a.dtype) for a in arrays]
    return _Comm(arrays, outs, 3 * len(arrays), make)


def _gather_d2d(bufs):
    def make(ins, outs, send, recv, loc):
        x, y, c, mine, peers = _place()
        cps = []
        for i, a in enumerate(bufs):
            half = _half_rows(c, a.shape[1])
            for k, (px, py) in enumerate(peers):
                mine_half = outs[i].at[2 * px + py, half]
                cps.append(pltpu.make_async_remote_copy(
                    src_ref=mine_half, dst_ref=mine_half, send_sem=send.at[3 * i + k], recv_sem=recv.at[3 * i + k],
                    device_id=(x, y, 1 - c), device_id_type=MESH))
        return cps

    outs = [jax.ShapeDtypeStruct(a.shape, a.dtype) for a in bufs]
    return _Comm(bufs, outs, 3 * len(bufs), make, in_place=True)


def _with_own(gathered, own, chip):
    return lax.dynamic_update_index_in_dim(gathered, own, chip, 0)


def _scatter_d2d(grads):
    for a in grads:
        assert a.shape[1] % 32 == 0, a.shape

    def make(ins, outs, send, recv, loc):
        x, y, c, mine, peers = _place()
        cps = []
        for i, a in enumerate(grads):
            other = _half_rows(1 - c, a.shape[1])
            cps.append(pltpu.make_async_remote_copy(
                src_ref=ins[i].at[:, other], dst_ref=outs[i], send_sem=send.at[i], recv_sem=recv.at[i],
                device_id=(x, y, 1 - c), device_id_type=MESH))
        return cps

    outs = [jax.ShapeDtypeStruct((4, a.shape[1] // 2, a.shape[2]), a.dtype) for a in grads]
    return _Comm(grads, outs, len(grads), make)


def _scatter_ici(parts):
    def make(ins, outs, send, recv, loc):
        x, y, c, mine, peers = _place()
        cps = []
        for i in range(len(parts)):
            cps.append(pltpu.make_async_copy(ins[i].at[mine], outs[i].at[mine], loc.at[i]))
            for k, (px, py) in enumerate(peers):
                cps.append(pltpu.make_async_remote_copy(
                    src_ref=ins[i].at[2 * px + py], dst_ref=outs[i].at[mine], send_sem=send.at[3 * i + k],
                    recv_sem=recv.at[3 * i + k], device_id=(px, py, c), device_id_type=MESH))
        return cps

    outs = [jax.ShapeDtypeStruct(a.shape, a.dtype) for a in parts]
    return _Comm(parts, outs, 3 * len(parts), make)


def _sibling_swap(arrays):
    def make(ins, outs, send, recv, loc):
        x, y, c, mine, peers = _place()
        return [pltpu.make_async_remote_copy(src_ref=ins[i], dst_ref=outs[i], send_sem=send.at[i], recv_sem=recv.at[i],
                                             device_id=(x, y, 1 - c), device_id_type=MESH) for i in range(len(arrays))]

    outs = [jax.ShapeDtypeStruct(a.shape, a.dtype) for a in arrays]
    return _Comm(arrays, outs, len(arrays), make)


def _mm(a, b, mode, out_dtype, name, tm, tn, tk, res=None, comm=None):
    if mode == "nn":
        (m, kd), n = a.shape, b.shape[1]
        a_spec = pl.BlockSpec((tm, tk), lambda i, j, k: (i, k))
        b_spec = pl.BlockSpec((tk, tn), lambda i, j, k: (k, j))
        dims = (((1,), (0,)), ((), ()))
    elif mode == "nt":
        (m, kd), n = a.shape, b.shape[0]
        a_spec = pl.BlockSpec((tm, tk), lambda i, j, k: (i, k))
        b_spec = pl.BlockSpec((tn, tk), lambda i, j, k: (j, k))
        dims = (((1,), (1,)), ((), ()))
    else:
        (kd, m), n = a.shape, b.shape[1]
        a_spec = pl.BlockSpec((tk, tm), lambda i, j, k: (k, i))
        b_spec = pl.BlockSpec((tk, tn), lambda i, j, k: (k, j))
        dims = (((0,), (0,)), ((), ()))
    assert m % tm == 0 and n % tn == 0 and kd % tk == 0, (name, m, n, kd, tm, tn, tk)
    nk = kd // tk
    has_res = res is not None
    in_place = out_dtype == F32

    def body(*refs):
        a_ref, b_ref = refs[:2]
        r_ref = refs[2] if has_res else None
        o_ref = refs[3 if has_res else 2]

        def finish(r):
            if has_res:
                r = r + r_ref[...].astype(F32)
            o_ref[...] = r.astype(out_dtype)

        if nk == 1:
            finish(_dot(a_ref[...], b_ref[...], dims))
            return
        k = pl.program_id(2)
        if in_place:
            @pl.when(k == 0)
            def _():
                finish(_dot(a_ref[...], b_ref[...], dims))

            @pl.when(k > 0)
            def _():
                o_ref[...] += _dot(a_ref[...], b_ref[...], dims)
            return
        acc = refs[-1]

        @pl.when(k == 0)
        def _():
            acc[...] = _dot(a_ref[...], b_ref[...], dims)

        @pl.when((k > 0) & (k < nk - 1))
        def _():
            acc[...] += _dot(a_ref[...], b_ref[...], dims)

        @pl.when(k == nk - 1)
        def _():
            finish(acc[...] + _dot(a_ref[...], b_ref[...], dims))

    in_specs = [a_spec, b_spec]
    args = [a, b]
    if has_res:
        in_specs.append(pl.BlockSpec((tm, tn), lambda i, j, k: (i, j)))
        args.append(res)
    res = _call(body, name, (m // tm, n // tn, nk), in_specs, [pl.BlockSpec((tm, tn), lambda i, j, k: (i, j))],
                [jax.ShapeDtypeStruct((m, n), out_dtype)], [] if nk == 1 or in_place else [pltpu.VMEM((tm, tn), F32)],
                ("parallel", "parallel", "arbitrary"), args, comm)
    return res[0] if comm is None else (res[0], res[1:])


def _mm_pieces_nt(pieces, b, name, tm, tn, tk_max, comm=None):
    m, n = pieces[0][0].shape[0], b.shape[0]
    out_dtype = F32
    cands = [c for c in (1024, 512, 256, 128) if c <= tk_max]
    tks = [_pick(math.gcd(a.shape[1], c0) if c0 else a.shape[1], cands) for a, c0 in pieces]
    nks = [a.shape[1] // tk for (a, _), tk in zip(pieces, tks)]
    starts = [sum(nks[:p]) for p in range(len(pieces))]
    ktot = sum(nks)
    npc = len(pieces)

    def body(*refs):
        a_refs, b_refs, o_ref = refs[:npc], refs[npc:2 * npc], refs[2 * npc]
        k = pl.program_id(2)

        @pl.when(k == 0)
        def _():
            o_ref[...] = _dot_nt(a_refs[0][...], b_refs[0][...])

        for p in range(npc):
            @pl.when((k >= max(starts[p], 1)) & (k < starts[p] + nks[p]))
            def _(p=p):
                o_ref[...] += _dot_nt(a_refs[p][...], b_refs[p][...])

    def a_spec(p):
        return pl.BlockSpec((tm, tks[p]), lambda i, j, k: (i, jnp.clip(k - starts[p], 0, nks[p] - 1)))

    def b_spec(p):
        c0 = pieces[p][1] // tks[p]
        return pl.BlockSpec((tn, tks[p]), lambda i, j, k: (j, c0 + jnp.clip(k - starts[p], 0, nks[p] - 1)))

    res = _call(body, name, (m // tm, n // tn, ktot), [a_spec(p) for p in range(npc)] + [b_spec(p) for p in range(npc)],
                [pl.BlockSpec((tm, tn), lambda i, j, k: (i, j))], [jax.ShapeDtypeStruct((m, n), out_dtype)],
                [], ("parallel", "parallel", "arbitrary"), [a for a, _ in pieces] + [b] * npc, comm)
    return res[0] if comm is None else (res[0], res[1:])


def _rms_fwd(h, w, name):
    t, d = h.shape
    tr = _pick(t, (640, 512, 384, 256, 128))

    def body(h_ref, w_ref, u_ref):
        x = h_ref[...]
        r = lax.rsqrt(jnp.mean(x * x, axis=1, keepdims=True) + EPS)
        u_ref[...] = (x * r * w_ref[...]).astype(MXU_DTYPE)

    return pl.pallas_call(
        body, name=name, grid=(t // tr,),
        in_specs=[pl.BlockSpec((tr, d), lambda i: (i, 0)), pl.BlockSpec((1, d), lambda i: (0, 0))],
        out_specs=pl.BlockSpec((tr, d), lambda i: (i, 0)),
        out_shape=jax.ShapeDtypeStruct((t, d), MXU_DTYPE),
        compiler_params=_params(("parallel",)),
    )(h, w)


def _rms_bwd(h, w, du, res, name, comm=None):
    t, d = h.shape
    tr = _pick(t, (640, 512, 384, 256, 128))

    def body(h_ref, w_ref, du_ref, res_ref, dh_ref, gw_ref):
        @pl.when(pl.program_id(0) == 0)
        def _():
            gw_ref[...] = jnp.zeros_like(gw_ref)

        x = h_ref[...]
        r = lax.rsqrt(jnp.mean(x * x, axis=1, keepdims=True) + EPS)
        xhat = x * r
        dy = du_ref[...].astype(F32)
        dxh = dy * w_ref[...]
        dh = r * (dxh - xhat * jnp.mean(dxh * xhat, axis=1, keepdims=True))
        dh_ref[...] = dh + res_ref[...]
        gw_ref[...] += jnp.sum(dy * xhat, axis=0, keepdims=True)

    row = pl.BlockSpec((tr, d), lambda i: (i, 0))
    vec = pl.BlockSpec((1, d), lambda i: (0, 0))
    return _call(body, name, (t // tr,), [row, vec, row, row], [row, vec],
                 [jax.ShapeDtypeStruct((t, d), F32), jax.ShapeDtypeStruct((1, d), F32)], [], ("arbitrary",),
                 (h, w, du, res), comm)


def _loss_bwd(h2, w, target, name):
    t, d = h2.shape
    nc = t // CHUNK

    def body(h_ref, w_ref, tg_ref, loss_ref, dh_ref, gw_ref):
        i = pl.program_id(0)

        @pl.when(i == 0)
        def _():
            gw_ref[...] = jnp.zeros_like(gw_ref)
            loss_ref[...] = jnp.zeros_like(loss_ref)
            dh_ref[...] = jnp.zeros_like(dh_ref)

        @pl.when(i > 0)
        def _():
            x = h_ref[...]
            r = lax.rsqrt(jnp.mean(x * x, axis=1, keepdims=True) + EPS)
            xhat = x * r
            diff = xhat * w_ref[...] - tg_ref[...]
            loss_ref[...] += 0.5 * jnp.sum(jnp.sum(diff * diff, axis=1, keepdims=True), axis=0, keepdims=True) / d
            dy = diff / d
            dxh = dy * w_ref[...]
            dh_ref[...] = r * (dxh - xhat * jnp.mean(dxh * xhat, axis=1, keepdims=True))
            gw_ref[...] += jnp.sum(dy * xhat, axis=0, keepdims=True)

    row = pl.BlockSpec((CHUNK, d), lambda i: (i, 0))
    vec = pl.BlockSpec((1, d), lambda i: (0, 0))
    return pl.pallas_call(
        body, name=name, grid=(nc,),
        in_specs=[row, vec, pl.BlockSpec((CHUNK, d), lambda i: (jnp.maximum(i - 1, 0), 0))],
        out_specs=[pl.BlockSpec((SUBLANES, LANES), lambda i: (0, 0)), row, vec],
        out_shape=[jax.ShapeDtypeStruct((SUBLANES, LANES), F32), jax.ShapeDtypeStruct((t, d), F32),
                   jax.ShapeDtypeStruct((1, d), F32)],
        compiler_params=_params(("arbitrary",)),
    )(h2, w, target)


def _conv_tile(cur, prev8, w_ref, b_ref, kw):
    y = b_ref[...] + cur * w_ref[kw - 1:kw, :]
    for k in range(kw - 1):
        y = y + _shift_down(cur, prev8, kw - 1 - k) * w_ref[k:k + 1, :]
    return y


_SUB = 16


def _sub_rows(s):
    return pl.ds(0 if isinstance(s, int) else pl.multiple_of(s * _SUB, _SUB), _SUB)


def _window(x_ref, prev8, s):
    if isinstance(s, int):
        return jnp.concatenate([prev8, x_ref[0:_SUB, :]], axis=0)
    return x_ref[pl.ds(pl.multiple_of(s * _SUB - SUBLANES, SUBLANES), _SUB + SUBLANES), :]


def _conv_step(win, w, b, kw):
    taps = [win[SUBLANES:] if k == kw - 1 else pltpu.roll(win, kw - 1 - k, 0)[SUBLANES:] for k in range(kw)]
    y = b + taps[kw - 1] * w[kw - 1:kw, :]
    for k in range(kw - 1):
        y = y + taps[k] * w[k:k + 1, :]
    return y, taps


def _conv_dx_step(dpre, next8, w, kw):
    n = _SUB + SUBLANES
    win = jnp.concatenate([dpre, next8], axis=0)
    acc = dpre * w[kw - 1:kw, :]
    for k in range(kw - 1):
        acc = acc + pltpu.roll(win, n - (kw - 1 - k), 0)[0:_SUB] * w[k:k + 1, :]
    return acc


def _fold8(v):
    return functools.reduce(jnp.add, [v[r:r + SUBLANES] for r in range(0, _SUB, SUBLANES)])


def _row_tile(t):
    return _pick(t, (640, 512, 384, 256, 128))


def _ssd_conv_fwd(proj, col0, width, w, b, name):
    t = proj.shape[0]
    kw = w.shape[0]
    tr, tc = _row_tile(t), _pick(width, (512, 256, 128))
    c0, rb = col0 // tc, tr // SUBLANES
    assert col0 % tc == 0

    def body(x_ref, p_ref, w_ref, b_ref, o_ref):
        i = pl.program_id(1)
        prev8 = jnp.where(i > 0, p_ref[...], 0.0)
        pre = _conv_tile(x_ref[...], prev8, w_ref, b_ref, kw)
        rows = _iota((tr, 1), 0) + i * tr
        o_ref[...] = jnp.where(rows >= FRONT, _silu(pre), 0.0)

    return pl.pallas_call(
        body, name=name, grid=(width // tc, t // tr),
        in_specs=[pl.BlockSpec((tr, tc), lambda j, i: (i, c0 + j)),
                  pl.BlockSpec((SUBLANES, tc), lambda j, i: (jnp.maximum(i * rb - 1, 0), c0 + j)),
                  pl.BlockSpec((kw, tc), lambda j, i: (0, j)),
                  pl.BlockSpec((1, tc), lambda j, i: (0, j))],
        out_specs=pl.BlockSpec((tr, tc), lambda j, i: (i, j)),
        out_shape=jax.ShapeDtypeStruct((t, width), F32),
        compiler_params=_params(("parallel", "parallel")),
    )(proj, proj, w, b)


def _ssd_conv_bwd(proj, col0, width, w, b, dact, name, comm=None):
    t = proj.shape[0]
    kw = w.shape[0]
    tr, tc = _row_tile(t), _pick(width, (512, 256, 128))
    c0, rb, nrow = col0 // tc, tr // SUBLANES, t // tr

    def body(x_ref, p_ref, w_ref, b_ref, d_ref, o_ref, gw_ref, gb_ref, carry):
        i = pl.program_id(1)
        ti = nrow - 1 - i

        @pl.when(i == 0)
        def _():
            gw_ref[...] = jnp.zeros_like(gw_ref)
            gb_ref[...] = jnp.zeros_like(gb_ref)
            carry[...] = jnp.zeros_like(carry)

        w, b = w_ref[...], b_ref[...]
        prev8 = jnp.where(ti > 0, p_ref[...], 0.0)
        nsub = tr // _SUB

        def step(s, state):
            next8, gb8, gw8 = state
            pre, taps = _conv_step(_window(x_ref, prev8, s), w, b, kw)
            valid = _iota((_SUB, 1), 0) + (ti * tr + s * _SUB) >= FRONT
            dpre = jnp.where(valid, d_ref[_sub_rows(s), :] * _dsilu(pre), 0.0)
            o_ref[_sub_rows(s), :] = jnp.where(valid, _conv_dx_step(dpre, next8, w, kw), 0.0).astype(MXU_DTYPE)
            return (dpre[0:SUBLANES], gb8 + _fold8(dpre), tuple(g + _fold8(dpre * tp) for g, tp in zip(gw8, taps)))

        zero8 = jnp.zeros((SUBLANES, tc), F32)
        state = lax.fori_loop(0, nsub - 1, lambda n, st: step(nsub - 1 - n, st), (carry[...], zero8, (zero8,) * kw))
        next8, gb8, gw8 = step(0, state)
        carry[...] = next8
        gb_ref[...] += jnp.sum(gb8, axis=0, keepdims=True)
        for k in range(kw):
            gw_ref[k:k + 1, :] += jnp.sum(gw8[k], axis=0, keepdims=True)

    return _call(
        body, name, (width // tc, nrow),
        [pl.BlockSpec((tr, tc), lambda j, i: (nrow - 1 - i, c0 + j)),
         pl.BlockSpec((SUBLANES, tc), lambda j, i: (jnp.maximum((nrow - 1 - i) * rb - 1, 0), c0 + j)),
         pl.BlockSpec((kw, tc), lambda j, i: (0, j)),
         pl.BlockSpec((1, tc), lambda j, i: (0, j)),
         pl.BlockSpec((tr, tc), lambda j, i: (nrow - 1 - i, j))],
        [pl.BlockSpec((tr, tc), lambda j, i: (nrow - 1 - i, j)),
         pl.BlockSpec((SUBLANES, tc), lambda j, i: (0, j)),
         pl.BlockSpec((1, tc), lambda j, i: (0, j))],
        [jax.ShapeDtypeStruct((t, width), MXU_DTYPE), jax.ShapeDtypeStruct((SUBLANES, width), F32),
         jax.ShapeDtypeStruct((1, width), F32)],
        [pltpu.VMEM((SUBLANES, tc), F32)], ("parallel", "arbitrary"), (proj, proj, w, b, dact), comm)


def _ffn_conv_fwd(up, w, b, name):
    t, f2 = up.shape
    f = f2 // 2
    kw = w.shape[0]
    tr, tc = _row_tile(t), _pick(f, (256, 128))
    nf, rb = f // tc, tr // SUBLANES

    def body(xg, pg, xv, pv, wg, wv, bg, bv, o_ref):
        i = pl.program_id(1)
        ag = _conv_tile(xg[...], jnp.where(i > 0, pg[...], 0.0), wg, bg, kw)
        av = _conv_tile(xv[...], jnp.where(i > 0, pv[...], 0.0), wv, bv, kw)
        o_ref[...] = (_silu(ag) * av).astype(MXU_DTYPE)

    def cur(off):
        return pl.BlockSpec((tr, tc), lambda j, i: (i, j + off))

    def prev(off):
        return pl.BlockSpec((SUBLANES, tc), lambda j, i: (jnp.maximum(i * rb - 1, 0), j + off))

    def par(rows, off):
        return pl.BlockSpec((rows, tc), lambda j, i: (0, j + off))

    return pl.pallas_call(
        body, name=name, grid=(nf, t // tr),
        in_specs=[cur(0), prev(0), cur(nf), prev(nf), par(kw, 0), par(kw, nf), par(1, 0), par(1, nf)],
        out_specs=pl.BlockSpec((tr, tc), lambda j, i: (i, j)),
        out_shape=jax.ShapeDtypeStruct((t, f), MXU_DTYPE),
        compiler_params=_params(("parallel", "parallel")),
    )(up, up, up, up, w, w, b, b)


def _ffn_conv_bwd(up, w, b, dact, name):
    t, f2 = up.shape
    f = f2 // 2
    kw = w.shape[0]
    tr, tc = _row_tile(t), _pick(f, (256, 128))
    nf, rb, nrow = f // tc, tr // SUBLANES, t // tr

    def body(xg, pg, xv, pv, wg_ref, wv_ref, bg_ref, bv_ref, d_ref, og_ref, ov_ref, gwg_ref, gwv_ref, gbg_ref, gbv_ref,
             cg, cv):
        i = pl.program_id(1)
        ti = nrow - 1 - i

        @pl.when(i == 0)
        def _():
            for r in (gwg_ref, gwv_ref, gbg_ref, gbv_ref, cg, cv):
                r[...] = jnp.zeros_like(r)

        wg, wv, bg, bv = wg_ref[...], wv_ref[...], bg_ref[...], bv_ref[...]
        p8g, p8v = jnp.where(ti > 0, pg[...], 0.0), jnp.where(ti > 0, pv[...], 0.0)
        nsub = tr // _SUB

        def step(s, state):
            ng, nv, gbg8, gbv8, gwg8, gwv8 = state
            ag, tg = _conv_step(_window(xg, p8g, s), wg, bg, kw)
            av, tv = _conv_step(_window(xv, p8v, s), wv, bv, kw)
            d = d_ref[_sub_rows(s), :]
            sg = jax.nn.sigmoid(ag)
            dag = d * av * (sg * (1.0 + ag * (1.0 - sg)))
            dav = d * (ag * sg)
            valid = _iota((_SUB, 1), 0) + (ti * tr + s * _SUB) >= FRONT
            og_ref[_sub_rows(s), :] = jnp.where(valid, _conv_dx_step(dag, ng, wg, kw), 0.0).astype(MXU_DTYPE)
            ov_ref[_sub_rows(s), :] = jnp.where(valid, _conv_dx_step(dav, nv, wv, kw), 0.0).astype(MXU_DTYPE)
            return (dag[0:SUBLANES], dav[0:SUBLANES], gbg8 + _fold8(dag), gbv8 + _fold8(dav),
                    tuple(g + _fold8(dag * tp) for g, tp in zip(gwg8, tg)),
                    tuple(g + _fold8(dav * tp) for g, tp in zip(gwv8, tv)))

        zero8 = jnp.zeros((SUBLANES, tc), F32)
        state = lax.fori_loop(0, nsub - 1, lambda n, st: step(nsub - 1 - n, st),
                              (cg[...], cv[...], zero8, zero8, (zero8,) * kw, (zero8,) * kw))
        ng, nv, gbg8, gbv8, gwg8, gwv8 = step(0, state)
        cg[...] = ng
        cv[...] = nv
        gbg_ref[...] += jnp.sum(gbg8, axis=0, keepdims=True)
        gbv_ref[...] += jnp.sum(gbv8, axis=0, keepdims=True)
        for k in range(kw):
            gwg_ref[k:k + 1, :] += jnp.sum(gwg8[k], axis=0, keepdims=True)
            gwv_ref[k:k + 1, :] += jnp.sum(gwv8[k], axis=0, keepdims=True)

    def cur(off):
        return pl.BlockSpec((tr, tc), lambda j, i: (nrow - 1 - i, j + off))

    def prev(off):
        return pl.BlockSpec((SUBLANES, tc), lambda j, i: (jnp.maximum((nrow - 1 - i) * rb - 1, 0), j + off))

    def par(rows, off):
        return pl.BlockSpec((rows, tc), lambda j, i: (0, j + off))

    acc8 = pl.BlockSpec((SUBLANES, tc), lambda j, i: (0, j))
    acc1 = pl.BlockSpec((1, tc), lambda j, i: (0, j))
    return pl.pallas_call(
        body, name=name, grid=(nf, nrow),
        in_specs=[cur(0), prev(0), cur(nf), prev(nf), par(kw, 0), par(kw, nf), par(1, 0), par(1, nf), cur(0)],
        out_specs=[cur(0), cur(0), acc8, acc8, acc1, acc1],
        out_shape=[jax.ShapeDtypeStruct((t, f), MXU_DTYPE), jax.ShapeDtypeStruct((t, f), MXU_DTYPE),
                   jax.ShapeDtypeStruct((SUBLANES, f), F32), jax.ShapeDtypeStruct((SUBLANES, f), F32),
                   jax.ShapeDtypeStruct((1, f), F32), jax.ShapeDtypeStruct((1, f), F32)],
        scratch_shapes=[pltpu.VMEM((SUBLANES, tc), F32), pltpu.VMEM((SUBLANES, tc), F32)],
        compiler_params=_params(("parallel", "arbitrary")),
    )(up, up, up, up, w, w, b, b, dact)


def _gate_fwd(bs, br, proj, c_gs, c_gr, name):
    t, d = bs.shape
    tr = _row_tile(t)

    def body(bs_ref, br_ref, gs_ref, gr_ref, o_ref):
        o_ref[...] = (jax.nn.sigmoid(gs_ref[...]) * bs_ref[...] + jax.nn.sigmoid(gr_ref[...]) * br_ref[...]).astype(MXU_DTYPE)

    row = pl.BlockSpec((tr, d), lambda i: (i, 0))
    return pl.pallas_call(
        body, name=name, grid=(t // tr,),
        in_specs=[row, row, pl.BlockSpec((tr, d), lambda i: (i, c_gs // d)), pl.BlockSpec((tr, d), lambda i: (i, c_gr // d))],
        out_specs=row, out_shape=jax.ShapeDtypeStruct((t, d), MXU_DTYPE),
        compiler_params=_params(("parallel",)),
    )(bs, br, proj, proj)


def _gate_bwd(dm, bs, br, proj, c_gs, c_gr, name):
    t, d = bs.shape
    tr = _row_tile(t)

    def body(dm_ref, bs_ref, br_ref, gs_ref, gr_ref, dbs_ref, dbr_ref, dgg_ref):
        g = dm_ref[...]
        ss, sr = jax.nn.sigmoid(gs_ref[...]), jax.nn.sigmoid(gr_ref[...])
        dbs_ref[...] = (g * ss).astype(MXU_DTYPE)
        dbr_ref[...] = (g * sr).astype(MXU_DTYPE)
        dgg_ref[:, :d] = (g * bs_ref[...] * ss * (1.0 - ss)).astype(MXU_DTYPE)
        dgg_ref[:, d:] = (g * br_ref[...] * sr * (1.0 - sr)).astype(MXU_DTYPE)

    row = pl.BlockSpec((tr, d), lambda i: (i, 0))
    out = jax.ShapeDtypeStruct((t, d), MXU_DTYPE)
    return pl.pallas_call(
        body, name=name, grid=(t // tr,),
        in_specs=[row, row, row, pl.BlockSpec((tr, d), lambda i: (i, c_gs // d)), pl.BlockSpec((tr, d), lambda i: (i, c_gr // d))],
        out_specs=[row, row, pl.BlockSpec((tr, 2 * d), lambda i: (i, 0))],
        out_shape=[out, out, jax.ShapeDtypeStruct((t, 2 * d), MXU_DTYPE)],
        compiler_params=_params(("parallel",)),
    )(dm, bs, br, proj, proj)


def _ret_consts(h):
    lg = math.log(1.0 - 2.0 ** (-5.0 - h))
    l = _iota((CHUNK, 1), 0).astype(F32)
    diff = l - _iota((1, CHUNK), 1).astype(F32)
    dm = jnp.exp(jnp.where(diff >= 0, diff * lg, -jnp.inf))
    dmt = jnp.exp(jnp.where(diff <= 0, -diff * lg, -jnp.inf))
    cs = jnp.exp((l + 1.0) * lg)
    kdec = jnp.exp((CHUNK - 1.0 - l) * lg)
    return dm, dmt, cs, kdec, math.exp(CHUNK * lg)


def _ret_fwd(proj, c_q, c_k, c_v, c_g, cos, sin, d, name, comm=None):
    t = proj.shape[0]
    nc = t // CHUNK
    hq, hv = d // RET_HEADS, 2 * d // RET_HEADS
    half = hq // 2
    scale = hq ** -0.5

    def body(q_ref, k_ref, v_ref, g_ref, cos_ref, sin_ref, o_ref, y_ref, qr_ref, kr_ref, st_ref, rs):
        @pl.when(pl.program_id(0) == 0)
        def _():
            rs[...] = jnp.zeros_like(rs)

        co, si = cos_ref[...], sin_ref[...]
        for h in range(RET_HEADS):
            dm, _, cs, kdec, gam = _ret_consts(h)
            q1, q2 = q_ref[:, h * hq:h * hq + half], q_ref[:, h * hq + half:(h + 1) * hq]
            k1, k2 = k_ref[:, h * hq:h * hq + half], k_ref[:, h * hq + half:(h + 1) * hq]
            qr = jnp.concatenate([q1 * co - q2 * si, q2 * co + q1 * si], axis=1)
            kr = jnp.concatenate([k1 * co - k2 * si, k2 * co + k1 * si], axis=1) * scale
            qr_ref[:, h * hq:(h + 1) * hq] = qr.astype(MXU_DTYPE)
            kr_ref[:, h * hq:(h + 1) * hq] = kr.astype(MXU_DTYPE)
            v = v_ref[:, h * hv:(h + 1) * hv]
            r_in = rs[h * hq:(h + 1) * hq, :]
            st_ref[0, h * hq:(h + 1) * hq, :] = r_in.astype(MXU_DTYPE)
            s = _dot_nt(qr, kr) * dm
            o = _dot(s, v) + cs * _dot(qr, r_in)
            rs[h * hq:(h + 1) * hq, :] = gam * r_in + _dot_tn(kr * kdec, v)
            o_ref[:, h * hv:(h + 1) * hv] = o
            on = o * lax.rsqrt(jnp.mean(o * o, axis=1, keepdims=True) + EPS)
            y_ref[:, h * hv:(h + 1) * hv] = (_silu(g_ref[:, h * hv:(h + 1) * hv]) * on).astype(MXU_DTYPE)

    def col(width, c0):
        return pl.BlockSpec((CHUNK, width), lambda i: (i, c0 // width))

    tab = pl.BlockSpec((CHUNK, half), lambda i: (i, 0))
    return _call(
        body, name, (nc,),
        [col(d, c_q), col(d, c_k), col(2 * d, c_v), col(2 * d, c_g), tab, tab],
        [col(2 * d, 0), col(2 * d, 0), col(d, 0), col(d, 0), pl.BlockSpec((1, d, hv), lambda i: (i, 0, 0))],
        [jax.ShapeDtypeStruct((t, 2 * d), F32), jax.ShapeDtypeStruct((t, 2 * d), MXU_DTYPE),
         jax.ShapeDtypeStruct((t, d), MXU_DTYPE), jax.ShapeDtypeStruct((t, d), MXU_DTYPE),
         jax.ShapeDtypeStruct((nc, d, hv), MXU_DTYPE)],
        [pltpu.VMEM((d, hv), F32)], ("arbitrary",), (proj, proj, proj, proj, cos, sin), comm)


def _ret_bwd(dy, proj, c_v, c_g, o, qr, kr, st, cos, sin, d, name, comm=None):
    t = proj.shape[0]
    nc = t // CHUNK
    hq, hv = d // RET_HEADS, 2 * d // RET_HEADS
    half = hq // 2
    scale = hq ** -0.5

    def body(dy_ref, v_ref, g_ref, o_ref, qr_ref, kr_ref, st_ref, cos_ref, sin_ref, dqk_ref, dvg_ref, drs):
        dq_ref, dk_ref = dqk_ref.at[:, pl.ds(0, d)], dqk_ref.at[:, pl.ds(d, d)]
        dv_ref, dg_ref = dvg_ref.at[:, pl.ds(0, 2 * d)], dvg_ref.at[:, pl.ds(2 * d, 2 * d)]

        @pl.when(pl.program_id(0) == 0)
        def _():
            drs[...] = jnp.zeros_like(drs)

        co, si = cos_ref[...], sin_ref[...]
        for h in range(RET_HEADS):
            dm, dmt, cs, kdec, gam = _ret_consts(h)
            vs = slice(h * hv, (h + 1) * hv)
            qs = slice(h * hq, (h + 1) * hq)
            o_h = o_ref[:, vs]
            g_h = g_ref[:, vs]
            d_y = dy_ref[:, vs]
            r = lax.rsqrt(jnp.mean(o_h * o_h, axis=1, keepdims=True) + EPS)
            on = o_h * r
            d_on = d_y * _silu(g_h)
            dg_ref[:, vs] = (d_y * on * _dsilu(g_h)).astype(MXU_DTYPE)
            d_o = r * (d_on - on * jnp.mean(d_on * on, axis=1, keepdims=True))
            q_h, k_h, v_h = qr_ref[:, qs], kr_ref[:, qs], v_ref[:, vs]
            r_in = st_ref[0, qs, :]
            dr_n = drs[qs, :]
            csdo = cs * d_o
            ds = _dot_nt(d_o, v_h) * dm
            dst = _dot_nt(v_h, d_o) * dmt
            s_t = _dot_nt(k_h, q_h) * dmt
            dqr = _dot(ds, k_h) + _dot_nt(csdo, r_in)
            dkr = _dot(dst, q_h) + kdec * _dot_nt(v_h, dr_n)
            dv_ref[:, vs] = (_dot(s_t, d_o) + _dot(k_h.astype(F32) * kdec, dr_n)).astype(MXU_DTYPE)
            drs[qs, :] = gam * dr_n + _dot_tn(q_h, csdo)
            a1, a2 = dqr[:, :half], dqr[:, half:]
            dq_ref[:, qs] = jnp.concatenate([a1 * co + a2 * si, a2 * co - a1 * si], axis=1).astype(MXU_DTYPE)
            b1, b2 = dkr[:, :half] * scale, dkr[:, half:] * scale
            dk_ref[:, qs] = jnp.concatenate([b1 * co + b2 * si, b2 * co - b1 * si], axis=1).astype(MXU_DTYPE)

    def col(width, c0=0):
        return pl.BlockSpec((CHUNK, width), lambda i: (nc - 1 - i, c0 // width))

    tab = pl.BlockSpec((CHUNK, half), lambda i: (nc - 1 - i, 0))
    return _call(
        body, name, (nc,),
        [col(2 * d), col(2 * d, c_v), col(2 * d, c_g), col(2 * d), col(d), col(d),
         pl.BlockSpec((1, d, hv), lambda i: (nc - 1 - i, 0, 0)), tab, tab],
        [col(2 * d), col(4 * d)],
        [jax.ShapeDtypeStruct((t, 2 * d), MXU_DTYPE), jax.ShapeDtypeStruct((t, 4 * d), MXU_DTYPE)],
        [pltpu.VMEM((d, hv), F32)], ("arbitrary",), (dy, proj, proj, o, qr, kr, st, cos, sin), comm)


def _ssd_small(dtraw_ref, dtb_ref, alog_ref, chunk_idx, nh):
    rows = _iota((CHUNK, 1), 0)
    ok = ((rows >= FRONT) | (chunk_idx > 0)) & (_iota((1, LANES), 1) < nh)
    z = dtraw_ref[...] + dtb_ref[...]
    dt = jnp.where(ok, jax.nn.softplus(z), 0.0)
    sig = jnp.where(ok, jax.nn.sigmoid(z), 0.0)
    a = jnp.where(_iota((1, LANES), 1) < nh, -jnp.exp(alog_ref[...]), 0.0)
    tri = (_iota((CHUNK, CHUNK), 0) >= _iota((CHUNK, CHUNK), 1)).astype(F32)
    acs = _dot01(tri, dt * a, "b", 3)
    return dt, sig, a, acs, acs.T


def _head_expand(g, hpg, gw):
    shift = int(math.log2(SSD_HEAD_DIM))
    return (_iota((LANES, gw), 0) == g * hpg + lax.shift_right_logical(_iota((LANES, gw), 1), shift)).astype(F32)


def _ssd_fwd(xa, proj, c_dt, c_z, dtb, alog, dvec, nw, di, name, comm=None):
    t = xa.shape[0]
    nc = t // CHUNK
    nh = di // SSD_HEAD_DIM
    hpg = nh // SSD_GROUPS
    gw = di // SSD_GROUPS
    n = SSD_STATE
    gn = SSD_GROUPS * n
    hd = SSD_HEAD_DIM

    def body(x_ref, b_ref, c_ref, dtraw_ref, z_ref, dtb_ref, alog_ref, d_ref, nw_ref,
             y_ref, ys_ref, st_ref, hts, xdt_s):
        c = pl.program_id(0)

        @pl.when(c == 0)
        def _():
            hts[...] = jnp.zeros_like(hts)

        dt, _, _, acs, acs_t = _ssd_small(dtraw_ref, dtb_ref, alog_ref, c, nh)
        tri = _iota((CHUNK, CHUNK), 0) >= _iota((CHUNK, CHUNK), 1)
        dvec8 = jnp.broadcast_to(d_ref[...], (SUBLANES, LANES))
        for g in range(SSD_GROUPS):
            gs = slice(g * gw, (g + 1) * gw)
            ns = slice(g * n, (g + 1) * n)
            e_mat = _head_expand(g, hpg, gw)
            ax = _dot01(acs, e_mat, "a", 3)
            dtx = _dot01(dt, e_mat, "a", 3)
            dx = _dot01(dvec8, e_mat, "a", 3)[0:1, :]
            xg, bg, cg = x_ref[:, gs], b_ref[:, ns], c_ref[:, ns]
            xdt = xg * dtx
            xdt_s[...] = xdt.astype(MXU_DTYPE)
            cb = _dot_nt(cg, bg)
            ht = hts[ns, :]
            st_ref[0, ns, :] = ht.astype(MXU_DTYPE)
            y_ref[:, gs] = jnp.exp(ax) * _dot(cg, ht) + dx * xg
            for hh in range(hpg):
                h = g * hpg + hh
                lmat = jnp.exp(jnp.where(tri, acs[:, h:h + 1] - acs_t[h:h + 1, :], -jnp.inf))
                hs = slice(g * gw + hh * hd, g * gw + (hh + 1) * hd)
                y_ref[:, hs] += _dot(cb * lmat, xdt_s[:, hh * hd:(hh + 1) * hd])
            aend = ax[CHUNK - 1:CHUNK, :]
            hts[ns, :] = jnp.exp(aend) * ht + _dot_tn(bg, xdt * jnp.exp(aend - ax))
        for g in range(SSD_GROUPS):
            gs = slice(g * gw, (g + 1) * gw)
            yz = y_ref[:, gs] * _silu(z_ref[:, gs])
            r = lax.rsqrt(jnp.mean(yz * yz, axis=1, keepdims=True) + EPS)
            ys_ref[:, gs] = (yz * r * nw_ref[:, gs]).astype(MXU_DTYPE)

    def col(width, c0, arr_is_xa=False):
        return pl.BlockSpec((CHUNK, width), lambda i: (i, c0 // width))

    vec = pl.BlockSpec((1, LANES), lambda i: (0, 0))
    assert di % gn == 0 and c_dt % LANES == 0 and c_z % di == 0
    return _call(
        body, name, (nc,),
        [col(di, 0), col(gn, di), col(gn, di + gn), col(LANES, c_dt), col(di, c_z), vec, vec, vec,
         pl.BlockSpec((1, di), lambda i: (0, 0))],
        [col(di, 0), col(di, 0), pl.BlockSpec((1, gn, gw), lambda i: (i, 0, 0))],
        [jax.ShapeDtypeStruct((t, di), F32), jax.ShapeDtypeStruct((t, di), MXU_DTYPE),
         jax.ShapeDtypeStruct((nc, gn, gw), MXU_DTYPE)],
        [pltpu.VMEM((gn, gw), F32), pltpu.VMEM((CHUNK, gw), MXU_DTYPE)], ("arbitrary",),
        (xa, xa, xa, proj, proj, dtb, alog, dvec, nw), comm)


def _ssd_bwd(dys, xa, proj, c_dt, c_z, ypre, st, dtb, alog, dvec, nw, di, name, comm=None):
    t = xa.shape[0]
    nc = t // CHUNK
    nh = di // SSD_HEAD_DIM
    hpg = nh // SSD_GROUPS
    gw = di // SSD_GROUPS
    n = SSD_STATE
    gn = SSD_GROUPS * n
    hd = SSD_HEAD_DIM

    def body(dys_ref, x_ref, b_ref, c_ref, dtraw_ref, z_ref, y_ref, st_ref, dtb_ref, alog_ref, d_ref, nw_ref,
             dxa_ref, dz_ref, ddt_ref, gb_ref, ga_ref, gd_ref, gnw_ref, dhts, dy_s, xdt_s, dxdt_s):
        i = pl.program_id(0)
        c = nc - 1 - i

        @pl.when(i == 0)
        def _():
            dhts[...] = jnp.zeros_like(dhts)
            gb_ref[...] = jnp.zeros_like(gb_ref)
            ga_ref[...] = jnp.zeros_like(ga_ref)
            gd_ref[...] = jnp.zeros_like(gd_ref)
            gnw_ref[...] = jnp.zeros_like(gnw_ref)

        dt, sig, a, acs, acs_t = _ssd_small(dtraw_ref, dtb_ref, alog_ref, c, nh)
        tri = _iota((CHUNK, CHUNK), 0) >= _iota((CHUNK, CHUNK), 1)
        triu = _iota((CHUNK, CHUNK), 0) <= _iota((CHUNK, CHUNK), 1)
        lane = _iota((1, LANES), 1)
        rows = _iota((CHUNK, 1), 0)
        head_row = _iota((LANES, 1), 0)
        dvec8 = jnp.broadcast_to(d_ref[...], (SUBLANES, LANES))
        da = jnp.zeros((CHUNK, LANES), F32)
        da_t = jnp.zeros((LANES, CHUNK), F32)
        ddt = jnp.zeros((CHUNK, LANES), F32)
        gd = jnp.zeros((1, LANES), F32)
        for g in range(SSD_GROUPS):
            gs = slice(g * gw, (g + 1) * gw)
            ns = slice(g * n, (g + 1) * n)
            y_g, z_g = y_ref[:, gs], z_ref[:, gs]
            sz = _silu(z_g)
            yz = y_g * sz
            r = lax.rsqrt(jnp.mean(yz * yz, axis=1, keepdims=True) + EPS)
            nrm = yz * r
            dyo = dys_ref[:, gs]
            gnw_ref[:, gs] += jnp.sum(dyo * nrm, axis=0, keepdims=True)
            dn = dyo * nw_ref[:, gs]
            dyz = r * (dn - nrm * jnp.mean(dn * nrm, axis=1, keepdims=True))
            dz_ref[:, gs] = (dyz * y_g * _dsilu(z_g)).astype(MXU_DTYPE)
            dy_g = dyz * sz
            dy_s[...] = dy_g.astype(MXU_DTYPE)
            e_mat = _head_expand(g, hpg, gw)
            ax = _dot01(acs, e_mat, "a", 3)
            dtx = _dot01(dt, e_mat, "a", 3)
            dx = _dot01(dvec8, e_mat, "a", 3)[0:1, :]
            xg, bg, cg = x_ref[:, gs], b_ref[:, ns], c_ref[:, ns]
            xdt = xg * dtx
            xdt_s[...] = xdt.astype(MXU_DTYPE)
            aend = ax[CHUNK - 1:CHUNK, :]
            e = jnp.exp(aend - ax)
            ea = jnp.exp(ax)
            eend = jnp.exp(aend)
            htp = st_ref[0, ns, :].astype(F32)
            dht = dhts[ns, :]
            cb = _dot_nt(cg, bg)
            q = _dot(bg, dht)
            dxdt_s[...] = e * q
            wl = e * q * xdt
            d_b = _dot_nt(e * xdt, dht)
            yi = ea * _dot(cg, htp)
            eady = ea * dy_g
            d_c = _dot_nt(eady, htp)
            t1 = jnp.sum(dht * htp, axis=0, keepdims=True) * eend
            dhts[ns, :] = eend * dht + _dot_tn(cg, eady)
            da = da + _dot01(dy_g * yi - wl, e_mat, "a", 3, _NT)
            tail = jnp.broadcast_to(jnp.sum(wl, axis=0, keepdims=True) + t1, (SUBLANES, gw))
            da_end = _dot01(tail, e_mat, "a", 3, _NT)[0:1, :]
            da = da + jnp.where(rows == CHUNK - 1, da_end, 0.0)
            dcb = jnp.zeros((CHUNK, CHUNK), F32)
            for hh in range(hpg):
                h = g * hpg + hh
                lmat = jnp.exp(jnp.where(tri, acs[:, h:h + 1] - acs_t[h:h + 1, :], -jnp.inf))
                hl = slice(hh * hd, (hh + 1) * hd)
                dy_h, xdt_h = dy_s[:, hl], xdt_s[:, hl]
                dxdt_s[:, hl] += _dot_tn(cb * lmat, dy_h)
                dml = _dot_nt(dy_h, xdt_h) * lmat
                dcb = dcb + dml
                gmat = dml * cb
                da = da + jnp.where(lane == h, jnp.sum(gmat, axis=1, keepdims=True), 0.0)
                da_t = da_t - jnp.where(head_row == h, jnp.sum(gmat, axis=0, keepdims=True), 0.0)
            d_c = d_c + _dot(dcb, bg)
            d_b = d_b + _dot_tn(dcb, cg)
            dxdt = dxdt_s[...]
            dxa_ref[:, gs] = dxdt * dtx + dx * dy_g
            dxa_ref[:, di + g * n:di + (g + 1) * n] = d_b
            dxa_ref[:, di + gn + g * n:di + gn + (g + 1) * n] = d_c
            ddt = ddt + _dot01(dxdt * xg, e_mat, "a", 3, _NT)
            gd8 = jnp.broadcast_to(jnp.sum(dy_g * xg, axis=0, keepdims=True), (SUBLANES, gw))
            gd = gd + _dot01(gd8, e_mat, "a", 3, _NT)[0:1, :]
        da = da + da_t.T
        triu_f = triu.astype(F32)
        ddta = _dot01(triu_f, da, "b", 3)
        ddt = ddt + ddta * a
        draw = ddt * sig
        ddt_ref[...] = draw.astype(MXU_DTYPE)
        gb_ref[...] += jnp.sum(draw, axis=0, keepdims=True)
        ga_ref[...] += jnp.sum(ddta * dt, axis=0, keepdims=True) * a
        gd_ref[...] += gd

    def col(width, c0):
        return pl.BlockSpec((CHUNK, width), lambda i: (nc - 1 - i, c0 // width))

    vec = pl.BlockSpec((1, LANES), lambda i: (0, 0))
    wide = pl.BlockSpec((1, di), lambda i: (0, 0))
    wa = di + 2 * gn
    return _call(
        body, name, (nc,),
        [col(di, 0), col(di, 0), col(gn, di), col(gn, di + gn), col(LANES, c_dt), col(di, c_z), col(di, 0),
         pl.BlockSpec((1, gn, gw), lambda i: (nc - 1 - i, 0, 0)), vec, vec, vec, wide],
        [col(wa, 0), col(di, 0), col(LANES, 0), vec, vec, vec, wide],
        [jax.ShapeDtypeStruct((t, wa), F32), jax.ShapeDtypeStruct((t, di), MXU_DTYPE),
         jax.ShapeDtypeStruct((t, LANES), MXU_DTYPE), jax.ShapeDtypeStruct((1, LANES), F32),
         jax.ShapeDtypeStruct((1, LANES), F32), jax.ShapeDtypeStruct((1, LANES), F32),
         jax.ShapeDtypeStruct((1, di), F32)],
        [pltpu.VMEM((gn, gw), F32), pltpu.VMEM((CHUNK, gw), MXU_DTYPE), pltpu.VMEM((CHUNK, gw), MXU_DTYPE),
         pltpu.VMEM((CHUNK, gw), F32)], ("arbitrary",),
        (dys, xa, xa, xa, proj, proj, ypre, st, dtb, alog, dvec, nw), comm)


def _adam_math(w, g, m, v):
    m2 = ADAM_B1 * m + (1.0 - ADAM_B1) * g
    v2 = ADAM_B2 * v + (1.0 - ADAM_B2) * (g * g)
    m_hat = m2 / (1.0 - ADAM_B1 ** ADAM_STEP)
    v_hat = v2 / (1.0 - ADAM_B2 ** ADAM_STEP)
    delta = -ADAM_LR * (m_hat / (jnp.sqrt(v_hat) + ADAM_EPS) + ADAM_WD * w)
    return delta, m2, v2


def _adam_big(w, g_mine, g_sib, m, v, core, name, comm=None):
    r, c = w.shape
    h = r // 2
    tr = _pick(h, (128, 176, 64, 32, 16, 8))
    nbh = h // tr

    def body(core_ref, w_ref, a_ref, b_ref, m_ref, v_ref, g_ref, d_ref, m2_ref, v2_ref):
        g = jnp.where(pl.program_id(0) // nbh == core_ref[0], a_ref[...], b_ref[...])
        delta, m2, v2 = _adam_math(w_ref[...], g, m_ref[...], v_ref[...])
        g_ref[...] = g
        d_ref[...] = delta
        m2_ref[...] = m2
        v2_ref[...] = v2

    blk = pl.BlockSpec((tr, c), lambda i: (i, 0))
    hblk = pl.BlockSpec((tr, c), lambda i: (i % nbh, 0))
    out = jax.ShapeDtypeStruct((r, c), F32)
    return _call(body, name, (2 * nbh,), [pl.BlockSpec(memory_space=pltpu.SMEM), blk, hblk, hblk, blk, blk], [blk] * 4,
                 [out] * 4, [], ("parallel",), (core, w, g_mine, g_sib, m, v), comm)


def _pair_sum(g, sib, core, name):
    _, r, c = g.shape
    h = r // 2
    tr = _pick(h, (128, 176, 64, 32, 16))
    nb = h // tr

    def body(core_ref, g_ref, s_ref, o_ref):
        o_ref[...] = (g_ref[...].astype(F32) + s_ref[...].astype(F32)).astype(WIRE_DTYPE)

    return pl.pallas_call(
        body, name=name,
        grid_spec=pltpu.PrefetchScalarGridSpec(
            num_scalar_prefetch=1, grid=(4, nb),
            in_specs=[pl.BlockSpec((1, tr, c), lambda j, i, core_ref: (j, core_ref[0] * nb + i, 0)),
                      pl.BlockSpec((1, tr, c), lambda j, i, core_ref: (j, i, 0))],
            out_specs=pl.BlockSpec((1, tr, c), lambda j, i, core_ref: (j, i, 0))),
        out_shape=jax.ShapeDtypeStruct((4, h, c), WIRE_DTYPE), compiler_params=_params(("parallel", "parallel")),
    )(core, g, sib)


def _sum4(parts, name):
    _, r, c = parts.shape
    tr = _pick(r, (128, 176, 64, 32, 16, 8))

    def body(p_ref, o_ref):
        acc = p_ref[0].astype(F32)
        for j in range(1, 4):
            acc = acc + p_ref[j].astype(F32)
        o_ref[...] = acc

    return pl.pallas_call(
        body, name=name, grid=(r // tr,),
        in_specs=[pl.BlockSpec((4, tr, c), lambda i: (0, i, 0))],
        out_specs=pl.BlockSpec((tr, c), lambda i: (i, 0)),
        out_shape=jax.ShapeDtypeStruct((r, c), F32),
        compiler_params=_params(("parallel",)),
    )(parts)


def _adam_small(items, chip, name):
    n = len(items) - 1

    def total(g_ref, r, c):
        acc = g_ref[0, 0:r, 0:c]
        for j in range(1, 8):
            acc = acc + g_ref[j, 0:r, 0:c]
        return acc

    def body(chip_ref, *refs):
        g_refs, wmv, outs = refs[:n + 1], refs[n + 1:4 * n + 1], refs[4 * n + 1:]
        for p in range(n):
            r, c = items[p][1].shape
            g = total(g_refs[p], r, c)
            w_ref, m_ref, v_ref = wmv[3 * p:3 * p + 3]
            delta, m2, v2 = _adam_math(w_ref[...], g, m_ref[...], v_ref[...])
            for o_ref, val in zip(outs[4 * p:4 * p + 4], (g, delta, m2, v2)):
                o_ref[...] = val
        outs[4 * n][...] = total(g_refs[n], 1, 1)

    def full(shape):
        return pl.BlockSpec(shape, lambda i, chip_ref: (0,) * len(shape))

    g_specs, args, out_specs, out_shapes = [], [], [], []
    for g, w, m, v, sharded in items[:n]:
        if sharded:
            g_specs.append(pl.BlockSpec((8, g.shape[1], w.shape[1]), lambda i, chip_ref: (0, 0, chip_ref[0])))
        else:
            g_specs.append(full(g.shape))
        args += [w, m, v]
        out_specs += [full(w.shape)] * 4
        out_shapes += [jax.ShapeDtypeStruct(w.shape, F32)] * 4
    g_specs.append(full(items[n][0].shape))
    return pl.pallas_call(
        body, name=name,
        grid_spec=pltpu.PrefetchScalarGridSpec(
            num_scalar_prefetch=1, grid=(1,), in_specs=g_specs + [full(a.shape) for a in args],
            out_specs=out_specs + [full((1, 1))]),
        out_shape=out_shapes + [jax.ShapeDtypeStruct((1, 1), F32)],
    )(chip, *[it[0] for it in items], *args)


def _gather8(arrays):
    def make(ins, outs, send, recv, loc):
        x, y, c, _, _ = _place()
        mine = 4 * x + 2 * y + c
        cps = []
        for i in range(len(arrays)):
            cps.append(pltpu.make_async_copy(ins[i], outs[i].at[mine], loc.at[i]))
            for k in range(1, 8):
                fx, fy, fc = (k >> 2) & 1, (k >> 1) & 1, k & 1
                peer = (1 - x if fx else x, 1 - y if fy else y, 1 - c if fc else c)
                cps.append(pltpu.make_async_remote_copy(
                    src_ref=ins[i], dst_ref=outs[i].at[mine], send_sem=send.at[7 * i + k - 1],
                    recv_sem=recv.at[7 * i + k - 1], device_id=peer, device_id_type=MESH))
        return cps

    outs = [jax.ShapeDtypeStruct((8,) + a.shape, a.dtype) for a in arrays]
    return _Comm(arrays, outs, 7 * len(arrays), make)


def _pack(parts):
    flat = jnp.concatenate([p.reshape(-1).astype(F32) for p in parts])
    pad = (-flat.shape[0]) % (32 * LANES)
    return jnp.pad(flat, (0, pad)).reshape(-1, LANES)


def _unpack(slab, shapes):
    flat = slab.reshape(-1)
    out, off = [], 0
    for s in shapes:
        size = int(np.prod(s))
        out.append(flat[off:off + size].reshape(s))
        off += size
    return out


def _gather_cols(segments, lo, hi):
    parts = []
    for arr, start, width in segments:
        a, e = max(lo, start), min(hi, start + width)
        if a < e:
            parts.append(arr[:, a - start:e - start])
    return parts


def _pad_lanes(v):
    return jnp.pad(v.reshape(1, -1), ((0, 0), (0, LANES - v.shape[-1])))


def kernel(x, meta_tokens, mix_norm_w, w_in, ssd_conv_w, ssd_conv_b, ssd_dt_bias, ssd_A_log, ssd_D, ssd_norm_w, w_branch_ssd, w_branch_ret, w_out, ffn_norm_w, w_up, ffn_conv_w, ffn_conv_b, w_down, final_norm_w, loss_target, m_meta_tokens, m_mix_norm_w, m_w_in, m_ssd_conv_w, m_ssd_conv_b, m_ssd_dt_bias, m_ssd_A_log, m_ssd_D, m_ssd_norm_w, m_w_branch_ssd, m_w_branch_ret, m_w_out, m_ffn_norm_w, m_w_up, m_ffn_conv_w, m_ffn_conv_b, m_w_down, m_final_norm_w, v_meta_tokens, v_mix_norm_w, v_w_in, v_ssd_conv_w, v_ssd_conv_b, v_ssd_dt_bias, v_ssd_A_log, v_ssd_D, v_ssd_norm_w, v_w_branch_ssd, v_w_branch_ret, v_w_out, v_ffn_norm_w, v_w_up, v_ffn_conv_w, v_ffn_conv_b, v_w_down, v_final_norm_w):
    seq, d = x.shape[1], x.shape[2]
    t = CHUNK + seq
    di = 2 * d
    nh = di // SSD_HEAD_DIM
    gn = SSD_GROUPS * SSD_STATE
    cw = di + 2 * gn
    f = w_down.shape[1] * 4
    chip = 2 * lax.axis_index("x") + lax.axis_index("y")

    order = [("z", di), ("v", di), ("g", di), ("xbc", cw), ("q", d), ("k", d), ("gs", d), ("gr", d), ("dt", LANES)]
    col, acc = {}, 0
    for nm, wd in order:
        col[nm] = acc
        acc += wd
    wp = acc
    ref_order = [("z", di), ("xbc", cw), ("dt", nh), ("q", d), ("k", d), ("v", di), ("g", di), ("gs", d), ("gr", d)]
    ref_off, acc = {}, 0
    for nm, wd in ref_order:
        ref_off[nm] = (acc, wd)
        acc += wd
    in_dim = acc

    core = lax.axis_index("c").astype(jnp.int32).reshape(1)
    small_shapes = [meta_tokens.shape, ssd_conv_w.shape[1:], ffn_conv_w.shape[1:]]
    small_local = _pack([meta_tokens, ssd_conv_w[0], ffn_conv_w[0]])
    first_local = [w_in[0].astype(WIRE_DTYPE), small_local]
    first_half = _run_comm(_gather_ici(first_local), "gather_w_in_ici")
    g_in, g_small = [_with_own(g, own, chip)
                     for g, own in zip(_run_comm(_gather_d2d(first_half), "gather_w_in_d2d"), first_local)]
    rest_local = [a[0].astype(WIRE_DTYPE) for a in (w_branch_ssd, w_branch_ret, w_out, w_up, w_down)]
    blk_in = in_dim // 4
    in_blocks = [(g_in[j], j * blk_in, blk_in) for j in range(4)]
    pieces = []
    for nm, wd in order:
        o, rw = ref_off[nm]
        pieces += _gather_cols(in_blocks, o, o + rw)
        if rw < wd:
            pieces.append(jnp.zeros((d, wd - rw), WIRE_DTYPE))
    w_p = jnp.concatenate(pieces, axis=1)
    smalls = [_unpack(g_small[j], small_shapes) for j in range(4)]
    meta_full = jnp.concatenate([s[0] for s in smalls], axis=1)
    scw = jnp.concatenate([s[1] for s in smalls], axis=1)
    fcw = jnp.concatenate([s[2] for s in smalls], axis=1)
    scb, fcb = ssd_conv_b, ffn_conv_b
    dtb, alog, dvec = _pad_lanes(ssd_dt_bias), _pad_lanes(ssd_A_log), _pad_lanes(ssd_D)
    fin_w = final_norm_w.reshape(1, d)

    hq = d // RET_HEADS
    pos = jnp.arange(t, dtype=F32) - FRONT
    inv_freq = ROPE_BASE ** (-jnp.linspace(0.0, 1.0, hq // 2, dtype=F32))
    ang = pos[:, None] * inv_freq[None, :]
    cos, sin = jnp.cos(ang), jnp.sin(ang)

    h0 = jnp.concatenate([jnp.zeros((FRONT, d), F32), meta_full, x[0]], axis=0)
    tm = _row_tile(t)
    tmb = _pick(t, (1664, 1280, 640, 512, 384, 256, 128))
    u1 = _rms_fwd(h0, mix_norm_w, "rms1_fwd")
    proj = _mm(u1, w_p, "nn", F32, "proj", tmb, _pick(wp, (1920, 1536, 1280, 1024, 896, 768, 640, 512, 384, 256, 128)), d)
    xa = _ssd_conv_fwd(proj, col["xbc"], cw, scw, scb, "ssd_conv_fwd")
    res = _ssd_fwd(xa, proj, col["dt"], col["z"], dtb, alog, dvec, ssd_norm_w, di, "ssd_fwd", comm=_gather_ici(rest_local))
    (ypre, yssd, st_ssd), rest_half = res[:3], res[3:]
    res = _ret_fwd(proj, col["q"], col["k"], col["v"], col["g"], cos, sin, d, "ret_fwd", comm=_gather_d2d(rest_half))
    o_ret, yret, qr, kr, st_ret = res[:5]
    g_bs, g_br, g_out, g_up, g_down = [_with_own(g, own, chip) for g, own in zip(res[5:], rest_local)]
    w_bs = g_bs.reshape(di, d)
    w_br = g_br.reshape(di, d)
    w_o = g_out.reshape(d, d)
    w_u = jnp.concatenate([g_up[j] for j in range(4)], axis=1)
    w_d = g_down.reshape(f, d)
    tn_d = _pick(d, (1024, 512, 256, 128))
    bs = _mm(yssd, w_bs, "nn", F32, "branch_ssd", tmb, tn_d, _pick(di, (1024, 512, 256)))
    br = _mm(yret, w_br, "nn", F32, "branch_ret", tmb, tn_d, _pick(di, (1024, 512, 256)))
    merged = _gate_fwd(bs, br, proj, col["gs"], col["gr"], "gate_fwd")
    h1 = _mm(merged, w_o, "nn", F32, "out_proj", tmb, tn_d, d, res=h0)
    u2 = _rms_fwd(h1, ffn_norm_w, "rms2_fwd")
    tn_f = _pick(2 * f, (1408, 1024, 768, 512, 256, 128))
    up = _mm(u2, w_u, "nn", F32, "up_proj", tmb, tn_f, d)
    act = _ffn_conv_fwd(up, fcw, fcb, "ffn_conv_fwd")
    tk_f = _pick(f, (1408, 768, 704, 512, 256, 128))
    h2 = _mm(act, w_d, "nn", F32, "down_proj", tmb, tn_d, tk_f, res=h1)
    loss8, d_h2, g_fin = _loss_bwd(h2, fin_w, loss_target[0], "loss_head")

    tkt = _pick(t, (1664, 1280, 1024, 640, 512, 384, 256, 128))
    d_act = _mm(d_h2, w_d, "nt", F32, "d_act", tmb, tk_f, d)
    g_wd = _mm(act, d_h2, "tn", F32, "g_w_down", tk_f, tn_d, tkt)
    d_upg, d_upv, g_fcwg, g_fcwv, g_fcbg, g_fcbv = _ffn_conv_bwd(up, fcw, fcb, d_act, "ffn_conv_bwd")
    g_fcw = jnp.concatenate([g_fcwg, g_fcwv], axis=1)
    g_fcb = jnp.concatenate([g_fcbg, g_fcbv], axis=1)
    d_u2 = _mm(d_upg, w_u[:, :f], "nt", F32, "d_u2_gate", tmb, tn_d, tk_f)
    d_u2 = _mm(d_upv, w_u[:, f:], "nt", F32, "d_u2_value", tmb, tn_d, tk_f, res=d_u2)
    g_wu_segs = [(_mm(u2, d_upg, "tn", F32, "g_w_up_gate", tn_d, tk_f, tkt), 0, f),
                 (_mm(u2, d_upv, "tn", F32, "g_w_up_value", tn_d, tk_f, tkt), f, f)]
    d_h1, g_ffnw = _rms_bwd(h1, ffn_norm_w, d_u2, d_h2, "rms2_bwd")
    d_merged = _mm(d_h1, w_o, "nt", F32, "d_merged", tmb, tn_d, d)
    g_wo = _mm(merged, d_h1, "tn", F32, "g_w_out", tn_d, tn_d, tkt)
    d_bs, d_br, d_gsr = _gate_bwd(d_merged, bs, br, proj, col["gs"], col["gr"], "gate_bwd")
    tk_i = _pick(di, (1024, 512, 256))
    d_yssd = _mm(d_bs, w_bs, "nt", F32, "d_y_ssd", tmb, tk_i, d)
    g_wbs = _mm(yssd, d_bs, "tn", F32, "g_w_branch_ssd", tk_i, tn_d, tkt)
    d_yret = _mm(d_br, w_br, "nt", F32, "d_y_ret", tmb, tk_i, d)
    g_wbr = _mm(yret, d_br, "tn", F32, "g_w_branch_ret", tk_i, tn_d, tkt)

    early_names = ["w_branch_ssd", "w_branch_ret", "w_out", "w_up", "w_down"]
    early = [g_wbs.reshape(4, di // 4, d), g_wbr.reshape(4, di // 4, d), g_wo.reshape(4, d // 4, d),
             jnp.stack([jnp.concatenate(_gather_cols(g_wu_segs, j * (f // 2), (j + 1) * (f // 2)), axis=1) for j in range(4)]),
             g_wd.reshape(4, f // 4, d)]
    res = _ret_bwd(d_yret, proj, col["v"], col["g"], o_ret, qr, kr, st_ret, cos, sin, d, "ret_bwd", comm=_scatter_d2d(early))
    (dqk, dvg), early_sib = res[:2], res[2:]
    early_pair = [_pair_sum(g_, s_, core, "pair_" + nm) for g_, s_, nm in zip(early, early_sib, early_names)]
    res = _ssd_bwd(d_yssd, xa, proj, col["dt"], col["z"], ypre, st_ssd, dtb, alog, dvec, ssd_norm_w, di, "ssd_bwd",
                   comm=_scatter_ici(early_pair))
    (d_xa, dz, ddt, g_dtb, g_alog, g_dvec, g_snw), early_recv = res[:7], res[7:]
    early_mine = [_sum4(p, "sum4_" + nm) for p, nm in zip(early_recv, early_names)]
    res = _ssd_conv_bwd(proj, col["xbc"], cw, scw, scb, d_xa, "ssd_conv_bwd", comm=_sibling_swap(early_mine))
    (d_xbc, g_scw, g_scb), early_other = res[:3], res[3:]
    d_pieces = [("z", dz), ("v", dvg), ("xbc", d_xbc), ("q", dqk), ("gs", d_gsr), ("dt", ddt)]

    g_piece = {nm: _mm(u1, a, "tn", WIRE_DTYPE, "g_w_in_" + nm, tn_d, _pick(a.shape[1], (1024, 768, 512, 256, 128)), tkt)
               for nm, a in d_pieces}
    g_cols = dict(z=g_piece["z"], v=g_piece["v"][:, :di], g=g_piece["v"][:, di:], xbc=g_piece["xbc"],
                  q=g_piece["q"][:, :d], k=g_piece["q"][:, d:], gs=g_piece["gs"][:, :d], gr=g_piece["gs"][:, d:],
                  dt=g_piece["dt"])
    g_segs = [(g_cols[nm], ref_off[nm][0], rw) for nm, rw in ref_order]
    sc_in = jnp.stack([jnp.concatenate(_gather_cols(g_segs, j * blk_in, (j + 1) * blk_in), axis=1) for j in range(4)])
    in_sib = _run_comm(_scatter_d2d([sc_in]), "scatter_w_in_d2d")[0]
    in_pair = _pair_sum(sc_in, in_sib, core, "pair_w_in")
    d_u1, (in_recv,) = _mm_pieces_nt([(a, col[nm]) for nm, a in d_pieces], w_p, "d_u1", tmb, tn_d, 512,
                                     comm=_scatter_ici([in_pair]))
    in_mine = _sum4(in_recv, "sum4_w_in")
    d_h0, g_mixw, in_other = _rms_bwd(h0, mix_norm_w, d_u1, d_h1, "rms1_bwd", comm=_sibling_swap([in_mine]))
    grad_x = d_h0[CHUNK:][None]
    g_meta = d_h0[FRONT:CHUNK]

    names = ["w_in"] + early_names
    mine_half = [in_mine] + early_mine
    other_half = [in_other] + list(early_other)
    big_w = [w_in, w_branch_ssd, w_branch_ret, w_out, w_up, w_down]
    big_m = [m_w_in, m_w_branch_ssd, m_w_branch_ret, m_w_out, m_w_up, m_w_down]
    big_v = [v_w_in, v_w_branch_ssd, v_w_branch_ret, v_w_out, v_w_up, v_w_down]
    small = [
        ("meta_tokens", g_meta, meta_tokens, m_meta_tokens, v_meta_tokens, True),
        ("mix_norm_w", g_mixw, mix_norm_w, m_mix_norm_w, v_mix_norm_w, False),
        ("ssd_conv_w", g_scw, ssd_conv_w[0], m_ssd_conv_w[0], v_ssd_conv_w[0], True),
        ("ssd_conv_b", g_scb, ssd_conv_b, m_ssd_conv_b, v_ssd_conv_b, False),
        ("ssd_dt_bias", g_dtb, ssd_dt_bias, m_ssd_dt_bias, v_ssd_dt_bias, False),
        ("ssd_A_log", g_alog, ssd_A_log, m_ssd_A_log, v_ssd_A_log, False),
        ("ssd_D", g_dvec, ssd_D, m_ssd_D, v_ssd_D, False),
        ("ssd_norm_w", g_snw, ssd_norm_w, m_ssd_norm_w, v_ssd_norm_w, False),
        ("ffn_norm_w", g_ffnw, ffn_norm_w, m_ffn_norm_w, v_ffn_norm_w, False),
        ("ffn_conv_w", g_fcw, ffn_conv_w[0], m_ffn_conv_w[0], v_ffn_conv_w[0], True),
        ("ffn_conv_b", g_fcb, ffn_conv_b, m_ffn_conv_b, v_ffn_conv_b, False),
        ("final_norm_w", g_fin, fin_w, m_final_norm_w.reshape(1, d), v_final_norm_w.reshape(1, d), False),
    ]
    big_out = {}
    for nm, w_, p_, s_, m_, v_ in zip(names, big_w, mine_half, other_half, big_m, big_v):
        comm = _gather8([s[1] for s in small] + [loss8]) if nm == "w_in" else None
        res = _adam_big(w_[0], p_, s_, m_[0], v_[0], core, "adam_" + nm, comm)
        big_out[nm] = [r[None] for r in res[:4]]
        if comm is not None:
            gathered8 = res[4:]
    items =[(g8,) + s[2:] for g8, s in zip(gathered8, small)] + [(gathered8[-1], None, None, None, False)]
    small_res = _adam_small(items, chip.astype(jnp.int32).reshape(1), "adam_small")
    loss = small_res[-1].reshape(())
    out_shape = dict(meta_tokens=meta_tokens.shape, ssd_conv_w=ssd_conv_w.shape, ffn_conv_w=ffn_conv_w.shape,
                     final_norm_w=final_norm_w.shape)
    small_out = {s[0]: [r.reshape(out_shape.get(s[0], r.shape)) for r in small_res[4 * p:4 * p + 4]]
                 for p, s in enumerate(small)}

    weights = ["meta_tokens", "mix_norm_w", "w_in", "ssd_conv_w", "ssd_conv_b", "ssd_dt_bias", "ssd_A_log", "ssd_D",
               "ssd_norm_w", "w_branch_ssd", "w_branch_ret", "w_out", "ffn_norm_w", "w_up", "ffn_conv_w", "ffn_conv_b",
               "w_down", "final_norm_w"]
    outs = [loss, grad_x]
    for kind in range(4):
        for nm in weights:
            outs.append(big_out[nm][kind] if nm in big_out else small_out[nm][kind])
    return tuple(outs)
```
